```python
import math
import jax, jax.numpy as jnp
from jax import lax
import numpy as np

D_MODEL = 1024
BATCH = 8
SEQ = 8192
DEPTH = 1

MEM_LEN = 256
MLA_HEADS = 8
MLA_Q_RANK = 256
MLA_KV_RANK = 128
MLA_NOPE = 64
MLA_ROPE = 32
MLA_V = 64
ROPE_THETA = 10000.0
SB_HEADS = 8
SB_HEAD_DIM = 64
X_HEADS = 4
X_HEAD_DIM = 128
D_FF = -(-8 * D_MODEL // (3 * 256)) * 256
Q_BLOCK = 128
EPS = 1e-6
SB_WIDTH = SB_HEADS * SB_HEAD_DIM
IN_SPLITS = (MLA_Q_RANK, MLA_KV_RANK, MLA_ROPE, SB_WIDTH, SB_WIDTH, SB_WIDTH, D_MODEL, D_MODEL)
D_IN = MLA_Q_RANK + MLA_KV_RANK + MLA_ROPE + 3 * SB_WIDTH + 2 * D_MODEL

kernel_name = "hybrid_mla_stickbreaking_gated_block"


def rms_norm(x, g):
    xf = x.astype(jnp.float32)
    y = xf * lax.rsqrt(jnp.mean(xf * xf, axis=-1, keepdims=True) + EPS)
    return (y * g.astype(jnp.float32)).astype(x.dtype)


def rope_cos_sin(positions):
    inv_freq = ROPE_THETA ** (-jnp.arange(0, MLA_ROPE, 2, dtype=jnp.float32) / MLA_ROPE)
    ang = positions.astype(jnp.float32)[..., None] * inv_freq
    return jnp.cos(ang), jnp.sin(ang)


def apply_rope(x, cos, sin):
    half = x.shape[-1] // 2
    x1 = x[..., :half].astype(jnp.float32)
    x2 = x[..., half:].astype(jnp.float32)
    out = jnp.concatenate([x1 * cos - x2 * sin, x2 * cos + x1 * sin], axis=-1)
    return out.astype(x.dtype)


def mla_block(qb, blk, k, v):
    scale = 1.0 / math.sqrt(MLA_NOPE + MLA_ROPE)
    s = jnp.einsum('bqhd,bkhd->bhqk', qb, k).astype(jnp.float32) * scale
    q_pos = blk * Q_BLOCK + jnp.arange(Q_BLOCK)
    k_pos = jnp.arange(k.shape[1])
    mask = k_pos[None, :] <= q_pos[:, None]
    s = jnp.where(mask, s, jnp.finfo(jnp.float32).min)
    p = jax.nn.softmax(s, axis=-1).astype(v.dtype)
    return jnp.einsum('bhqk,bkhd->bqhd', p, v)


def stick_breaking_block(qb, blk, k, v):
    scale = 1.0 / math.sqrt(SB_HEAD_DIM)
    z = jnp.einsum('bqhd,bkhd->bhqk', qb, k).astype(jnp.float32) * scale
    q_pos = blk * Q_BLOCK + jnp.arange(Q_BLOCK)
    k_pos = jnp.arange(k.shape[1])
    mask = k_pos[None, :] < q_pos[:, None]
    log_beta = jax.nn.log_sigmoid(z)
    log_one_minus = jnp.where(mask, jax.nn.log_sigmoid(-z), 0.0)
    suffix = lax.cumsum(log_one_minus, axis=3, reverse=True) - log_one_minus
    a = jnp.where(mask, jnp.exp(log_beta + suffix), 0.0).astype(v.dtype)
    return jnp.einsum('bhqk,bkhd->bqhd', a, v)


def blocked_attend(block_fn, q, k, v):
    b, s, h, dq = q.shape
    dv = v.shape[-1]
    nb = s // Q_BLOCK
    qb = q.reshape(b, nb, Q_BLOCK, h, dq).transpose(1, 0, 2, 3, 4)
    out = lax.map(lambda a: block_fn(a[0], a[1], k, v), (qb, jnp.arange(nb)))
    return out.transpose(1, 0, 2, 3, 4).reshape(b, s, h * dv)


def _fwd_setup_inputs(seed: int = 0) -> dict:
    key = jax.random.key(seed)
    ks = jax.random.split(key, 24)

    def w(k, shape, fan_in):
        return jax.random.normal(k, shape, jnp.float32) * (fan_in ** -0.5)

    def gain(k, shape):
        return 1.0 + 0.01 * jax.random.normal(k, shape, jnp.float32)

    L = DEPTH
    return {
        "x": jax.random.normal(ks[0], (BATCH, SEQ, D_MODEL), jnp.float32),
        "mem": jax.random.normal(ks[1], (BATCH, MEM_LEN, D_MODEL), jnp.float32),
        "positions": jnp.broadcast_to(jnp.arange(SEQ, dtype=jnp.int32)[None, :], (BATCH, SEQ)),
        "g_mix": gain(ks[2], (L, D_MODEL)),
        "w_in": w(ks[3], (L, D_MODEL, D_IN), D_MODEL),
        "b_gate": 0.01 * jax.random.normal(ks[4], (L, 2, D_MODEL), jnp.float32),
        "g_q_lat": gain(ks[5], (L, MLA_Q_RANK)),
        "w_uq": w(ks[6], (L, MLA_Q_RANK, MLA_HEADS * (MLA_NOPE + MLA_ROPE)), MLA_Q_RANK),
        "g_kv_lat": gain(ks[7], (L, MLA_KV_RANK)),
        "w_ukv": w(ks[8], (L, MLA_KV_RANK, MLA_HEADS * (MLA_NOPE + MLA_V)), MLA_KV_RANK),
        "w_a_proj": w(ks[9], (L, MLA_HEADS * MLA_V, D_MODEL), MLA_HEADS * MLA_V),
        "w_b_proj": w(ks[10], (L, SB_WIDTH, D_MODEL), SB_WIDTH),
        "w_o": w(ks[11], (L, D_MODEL, D_MODEL), D_MODEL),
        "g_x": gain(ks[12], (L, D_MODEL)),
        "g_mem": gain(ks[13], (L, D_MODEL)),
        "w_xq": w(ks[14], (L, D_MODEL, X_HEADS * X_HEAD_DIM), D_MODEL),
        "w_xkv": w(ks[15], (L, D_MODEL, 2 * X_HEADS * X_HEAD_DIM), D_MODEL),
        "w_xo": w(ks[16], (L, X_HEADS * X_HEAD_DIM, D_MODEL), X_HEADS * X_HEAD_DIM),
        "g_ffn": gain(ks[17], (L, D_MODEL)),
        "w_gate": w(ks[18], (L, D_MODEL, D_FF), D_MODEL),
        "w_up": w(ks[19], (L, D_MODEL, D_FF), D_MODEL),
        "w_down": w(ks[20], (L, D_FF, D_MODEL), D_FF),
        "g_final": gain(ks[21], (D_MODEL,)),
    }


def _fwd_reference(x, mem, positions, g_mix, w_in, b_gate, g_q_lat, w_uq, g_kv_lat, w_ukv,
              w_a_proj, w_b_proj, w_o, g_x, g_mem, w_xq, w_xkv, w_xo,
              g_ffn, w_gate, w_up, w_down, g_final):
    b, s, _ = x.shape
    m_len = mem.shape[1]
    cos, sin = rope_cos_sin(positions)
    split_points = list(np.cumsum(IN_SPLITS)[:-1])

    for l in range(DEPTH):
        h = rms_norm(x, g_mix[l])
        proj = h @ w_in[l]
        c_q, c_kv, k_r, sb_q, sb_k, sb_v, gate_a, gate_b = jnp.split(proj, split_points, axis=-1)

        q = (rms_norm(c_q, g_q_lat[l]) @ w_uq[l]).reshape(b, s, MLA_HEADS, MLA_NOPE + MLA_ROPE)
        q_rope = apply_rope(q[..., MLA_NOPE:], cos[:, :, None, :], sin[:, :, None, :])
        q_a = jnp.concatenate([q[..., :MLA_NOPE], q_rope], axis=-1)
        kv = (rms_norm(c_kv, g_kv_lat[l]) @ w_ukv[l]).reshape(b, s, MLA_HEADS, MLA_NOPE + MLA_V)
        k_r = apply_rope(k_r, cos, sin)
        k_a = jnp.concatenate(
            [kv[..., :MLA_NOPE], jnp.broadcast_to(k_r[:, :, None, :], (b, s, MLA_HEADS, MLA_ROPE))], axis=-1)
        v_a = kv[..., MLA_NOPE:]
        o_a = blocked_attend(mla_block, q_a, k_a, v_a)

        o_b = blocked_attend(
            stick_breaking_block,
            sb_q.reshape(b, s, SB_HEADS, SB_HEAD_DIM),
            sb_k.reshape(b, s, SB_HEADS, SB_HEAD_DIM),
            sb_v.reshape(b, s, SB_HEADS, SB_HEAD_DIM))

        merged = (jax.nn.sigmoid(gate_a + b_gate[l, 0]) * (o_a @ w_a_proj[l])
                  + jax.nn.sigmoid(gate_b + b_gate[l, 1]) * (o_b @ w_b_proj[l]))
        x = x + merged @ w_o[l]

        hx = rms_norm(x, g_x[l])
        mn = rms_norm(mem, g_mem[l])
        xq = (hx @ w_xq[l]).reshape(b, s, X_HEADS, X_HEAD_DIM)
        xkv = (mn @ w_xkv[l]).reshape(b, m_len, 2, X_HEADS, X_HEAD_DIM)
        xs = jnp.einsum('bqhd,bkhd->bhqk', xq, xkv[:, :, 0]).astype(jnp.float32) / math.sqrt(X_HEAD_DIM)
        xp = jax.nn.softmax(xs, axis=-1).astype(x.dtype)
        xo = jnp.einsum('bhqk,bkhd->bqhd', xp, xkv[:, :, 1]).reshape(b, s, X_HEADS * X_HEAD_DIM)
        x = x + xo @ w_xo[l]

        hf = rms_norm(x, g_ffn[l])
        x = x + (jax.nn.silu(hf @ w_gate[l]) * (hf @ w_up[l])) @ w_down[l]

    return rms_norm(x, g_final)


import jax as _jax
import jax.numpy as _jnp

TWIN_FORMAT = 'train_step'
FWD_PARAMS = ['x', 'mem', 'positions', 'g_mix', 'w_in', 'b_gate', 'g_q_lat', 'w_uq', 'g_kv_lat', 'w_ukv', 'w_a_proj', 'w_b_proj', 'w_o', 'g_x', 'g_mem', 'w_xq', 'w_xkv', 'w_xo', 'g_ffn', 'w_gate', 'w_up', 'w_down', 'g_final']
TWIN_WEIGHTS = ['g_mix', 'w_in', 'b_gate', 'g_q_lat', 'w_uq', 'g_kv_lat', 'w_ukv', 'w_a_proj', 'w_b_proj', 'w_o', 'g_x', 'g_mem', 'w_xq', 'w_xkv', 'w_xo', 'g_ffn', 'w_gate', 'w_up', 'w_down', 'g_final']
TWIN_DIFF_INPUT = 'x'
TWIN_INPUTS = ['x', 'mem', 'positions', 'g_mix', 'w_in', 'b_gate', 'g_q_lat', 'w_uq', 'g_kv_lat', 'w_ukv', 'w_a_proj', 'w_b_proj', 'w_o', 'g_x', 'g_mem', 'w_xq', 'w_xkv', 'w_xo', 'g_ffn', 'w_gate', 'w_up', 'w_down', 'g_final', 'loss_target', 'm_g_mix', 'm_w_in', 'm_b_gate', 'm_g_q_lat', 'm_w_uq', 'm_g_kv_lat', 'm_w_ukv', 'm_w_a_proj', 'm_w_b_proj', 'm_w_o', 'm_g_x', 'm_g_mem', 'm_w_xq', 'm_w_xkv', 'm_w_xo', 'm_g_ffn', 'm_w_gate', 'm_w_up', 'm_w_down', 'm_g_final', 'v_g_mix', 'v_w_in', 'v_b_gate', 'v_g_q_lat', 'v_w_uq', 'v_g_kv_lat', 'v_w_ukv', 'v_w_a_proj', 'v_w_b_proj', 'v_w_o', 'v_g_x', 'v_g_mem', 'v_w_xq', 'v_w_xkv', 'v_w_xo', 'v_g_ffn', 'v_w_gate', 'v_w_up', 'v_w_down', 'v_g_final']
TWIN_OUTPUTS = ['loss', 'grad_x', 'grad_g_mix', 'grad_w_in', 'grad_b_gate', 'grad_g_q_lat', 'grad_w_uq', 'grad_g_kv_lat', 'grad_w_ukv', 'grad_w_a_proj', 'grad_w_b_proj', 'grad_w_o', 'grad_g_x', 'grad_g_mem', 'grad_w_xq', 'grad_w_xkv', 'grad_w_xo', 'grad_g_ffn', 'grad_w_gate', 'grad_w_up', 'grad_w_down', 'grad_g_final', 'delta_g_mix', 'delta_w_in', 'delta_b_gate', 'delta_g_q_lat', 'delta_w_uq', 'delta_g_kv_lat', 'delta_w_ukv', 'delta_w_a_proj', 'delta_w_b_proj', 'delta_w_o', 'delta_g_x', 'delta_g_mem', 'delta_w_xq', 'delta_w_xkv', 'delta_w_xo', 'delta_g_ffn', 'delta_w_gate', 'delta_w_up', 'delta_w_down', 'delta_g_final', 'new_m_g_mix', 'new_m_w_in', 'new_m_b_gate', 'new_m_g_q_lat', 'new_m_w_uq', 'new_m_g_kv_lat', 'new_m_w_ukv', 'new_m_w_a_proj', 'new_m_w_b_proj', 'new_m_w_o', 'new_m_g_x', 'new_m_g_mem', 'new_m_w_xq', 'new_m_w_xkv', 'new_m_w_xo', 'new_m_g_ffn', 'new_m_w_gate', 'new_m_w_up', 'new_m_w_down', 'new_m_g_final', 'new_v_g_mix', 'new_v_w_in', 'new_v_b_gate', 'new_v_g_q_lat', 'new_v_w_uq', 'new_v_g_kv_lat', 'new_v_w_ukv', 'new_v_w_a_proj', 'new_v_w_b_proj', 'new_v_w_o', 'new_v_g_x', 'new_v_g_mem', 'new_v_w_xq', 'new_v_w_xkv', 'new_v_w_xo', 'new_v_g_ffn', 'new_v_w_gate', 'new_v_w_up', 'new_v_w_down', 'new_v_g_final']
TWIN_LEAF_KINDS = {'loss': 'loss', 'grad_x': 'grad_x', 'grad_g_mix': 'grad_w', 'grad_w_in': 'grad_w', 'grad_b_gate': 'grad_w', 'grad_g_q_lat': 'grad_w', 'grad_w_uq': 'grad_w', 'grad_g_kv_lat': 'grad_w', 'grad_w_ukv': 'grad_w', 'grad_w_a_proj': 'grad_w', 'grad_w_b_proj': 'grad_w', 'grad_w_o': 'grad_w', 'grad_g_x': 'grad_w', 'grad_g_mem': 'grad_w', 'grad_w_xq': 'grad_w', 'grad_w_xkv': 'grad_w', 'grad_w_xo': 'grad_w', 'grad_g_ffn': 'grad_w', 'grad_w_gate': 'grad_w', 'grad_w_up': 'grad_w', 'grad_w_down': 'grad_w', 'grad_g_final': 'grad_w', 'delta_g_mix': 'delta_w', 'delta_w_in': 'delta_w', 'delta_b_gate': 'delta_w', 'delta_g_q_lat': 'delta_w', 'delta_w_uq': 'delta_w', 'delta_g_kv_lat': 'delta_w', 'delta_w_ukv': 'delta_w', 'delta_w_a_proj': 'delta_w', 'delta_w_b_proj': 'delta_w', 'delta_w_o': 'delta_w', 'delta_g_x': 'delta_w', 'delta_g_mem': 'delta_w', 'delta_w_xq': 'delta_w', 'delta_w_xkv': 'delta_w', 'delta_w_xo': 'delta_w', 'delta_g_ffn': 'delta_w', 'delta_w_gate': 'delta_w', 'delta_w_up': 'delta_w', 'delta_w_down': 'delta_w', 'delta_g_final': 'delta_w', 'new_m_g_mix': 'new_m', 'new_m_w_in': 'new_m', 'new_m_b_gate': 'new_m', 'new_m_g_q_lat': 'new_m', 'new_m_w_uq': 'new_m', 'new_m_g_kv_lat': 'new_m', 'new_m_w_ukv': 'new_m', 'new_m_w_a_proj': 'new_m', 'new_m_w_b_proj': 'new_m', 'new_m_w_o': 'new_m', 'new_m_g_x': 'new_m', 'new_m_g_mem': 'new_m', 'new_m_w_xq': 'new_m', 'new_m_w_xkv': 'new_m', 'new_m_w_xo': 'new_m', 'new_m_g_ffn': 'new_m', 'new_m_w_gate': 'new_m', 'new_m_w_up': 'new_m', 'new_m_w_down': 'new_m', 'new_m_g_final': 'new_m', 'new_v_g_mix': 'new_v', 'new_v_w_in': 'new_v', 'new_v_b_gate': 'new_v', 'new_v_g_q_lat': 'new_v', 'new_v_w_uq': 'new_v', 'new_v_g_kv_lat': 'new_v', 'new_v_w_ukv': 'new_v', 'new_v_w_a_proj': 'new_v', 'new_v_w_b_proj': 'new_v', 'new_v_w_o': 'new_v', 'new_v_g_x': 'new_v', 'new_v_g_mem': 'new_v', 'new_v_w_xq': 'new_v', 'new_v_w_xkv': 'new_v', 'new_v_w_xo': 'new_v', 'new_v_g_ffn': 'new_v', 'new_v_w_gate': 'new_v', 'new_v_w_up': 'new_v', 'new_v_w_down': 'new_v', 'new_v_g_final': 'new_v'}


def _forward(args):
    return _fwd_reference(*[args[k] for k in FWD_PARAMS])


def _output_shape():
    def fwd():
        inp = _fwd_setup_inputs(0)
        return _fwd_reference(*[inp[k] for k in FWD_PARAMS])
    out = _jax.eval_shape(fwd)
    return out.shape, out.dtype

N_MICROBATCH = 1
ADAM_LR = 0.001
ADAM_B1 = 0.9
ADAM_B2 = 0.999
ADAM_EPS = 1e-08
ADAM_WD = 0.01
ADAM_STEP = 10
PER_EXAMPLE_BATCH_AXIS = {'x': 0, 'mem': 0, 'positions': 0, 'loss_target': 0}
SHARED_INPUTS = []
_WEIGHT_DTYPES = {'g_mix': _jnp.float32, 'w_in': _jnp.float32, 'b_gate': _jnp.float32, 'g_q_lat': _jnp.float32, 'w_uq': _jnp.float32, 'g_kv_lat': _jnp.float32, 'w_ukv': _jnp.float32, 'w_a_proj': _jnp.float32, 'w_b_proj': _jnp.float32, 'w_o': _jnp.float32, 'g_x': _jnp.float32, 'g_mem': _jnp.float32, 'w_xq': _jnp.float32, 'w_xkv': _jnp.float32, 'w_xo': _jnp.float32, 'g_ffn': _jnp.float32, 'w_gate': _jnp.float32, 'w_up': _jnp.float32, 'w_down': _jnp.float32, 'g_final': _jnp.float32}
MOMENT_SCALE = {'g_mix': 1.266524e-01, 'w_in': 6.320238e-02, 'b_gate': 2.764520e-02, 'g_q_lat': 5.732957e-02, 'w_uq': 3.017265e-02, 'g_kv_lat': 1.143065e-01, 'w_ukv': 3.840449e-02, 'w_a_proj': 3.135399e-02, 'w_b_proj': 9.197563e-02, 'w_o': 9.682644e-02, 'g_x': 2.884557e-02, 'g_mem': 4.234175e-02, 'w_xq': 3.895634e-02, 'w_xkv': 3.918247e-02, 'w_xo': 2.806991e-02, 'g_ffn': 1.758651e-01, 'w_gate': 7.664710e-02, 'w_up': 7.427028e-02, 'w_down': 1.229619e-01, 'g_final': 6.393937e+01}


def _to_microbatches(a, axis):
    t = _jnp.moveaxis(a, axis, 0)
    t = t.reshape((N_MICROBATCH, t.shape[0] // N_MICROBATCH) + t.shape[1:])
    return _jnp.moveaxis(t, 1, axis + 1)


def setup_inputs(seed: int = 0) -> dict:
    inp = _fwd_setup_inputs(seed)
    key = _jax.random.fold_in(_jax.random.key(seed), 7919)
    shape, _ = _output_shape()
    out = dict(inp)
    out["loss_target"] = _jax.random.normal(_jax.random.fold_in(key, 0), shape, _jnp.float32)
    for i, name in enumerate(TWIN_WEIGHTS):
        w = inp[name].astype(_jnp.float32)
        if MOMENT_SCALE is None:
            s = _jnp.sqrt(_jnp.mean(_jnp.square(w)) + 1e-30)
        else:
            s = MOMENT_SCALE[name]
        km, kv = _jax.random.split(_jax.random.fold_in(key, i + 1))
        out[name] = w
        out["m_" + name] = s * _jax.random.normal(km, w.shape, _jnp.float32)
        out["v_" + name] = (s * s) * _jax.random.uniform(kv, w.shape, _jnp.float32, 0.5, 1.5)
    if N_MICROBATCH > 1:
        for name, axis in PER_EXAMPLE_BATCH_AXIS.items():
            out[name] = _to_microbatches(out[name], axis)
    return {'x': out['x'], 'mem': out['mem'], 'positions': out['positions'], 'g_mix': out['g_mix'], 'w_in': out['w_in'], 'b_gate': out['b_gate'], 'g_q_lat': out['g_q_lat'], 'w_uq': out['w_uq'], 'g_kv_lat': out['g_kv_lat'], 'w_ukv': out['w_ukv'], 'w_a_proj': out['w_a_proj'], 'w_b_proj': out['w_b_proj'], 'w_o': out['w_o'], 'g_x': out['g_x'], 'g_mem': out['g_mem'], 'w_xq': out['w_xq'], 'w_xkv': out['w_xkv'], 'w_xo': out['w_xo'], 'g_ffn': out['g_ffn'], 'w_gate': out['w_gate'], 'w_up': out['w_up'], 'w_down': out['w_down'], 'g_final': out['g_final'], 'loss_target': out['loss_target'], 'm_g_mix': out['m_g_mix'], 'm_w_in': out['m_w_in'], 'm_b_gate': out['m_b_gate'], 'm_g_q_lat': out['m_g_q_lat'], 'm_w_uq': out['m_w_uq'], 'm_g_kv_lat': out['m_g_kv_lat'], 'm_w_ukv': out['m_w_ukv'], 'm_w_a_proj': out['m_w_a_proj'], 'm_w_b_proj': out['m_w_b_proj'], 'm_w_o': out['m_w_o'], 'm_g_x': out['m_g_x'], 'm_g_mem': out['m_g_mem'], 'm_w_xq': out['m_w_xq'], 'm_w_xkv': out['m_w_xkv'], 'm_w_xo': out['m_w_xo'], 'm_g_ffn': out['m_g_ffn'], 'm_w_gate': out['m_w_gate'], 'm_w_up': out['m_w_up'], 'm_w_down': out['m_w_down'], 'm_g_final': out['m_g_final'], 'v_g_mix': out['v_g_mix'], 'v_w_in': out['v_w_in'], 'v_b_gate': out['v_b_gate'], 'v_g_q_lat': out['v_g_q_lat'], 'v_w_uq': out['v_w_uq'], 'v_g_kv_lat': out['v_g_kv_lat'], 'v_w_ukv': out['v_w_ukv'], 'v_w_a_proj': out['v_w_a_proj'], 'v_w_b_proj': out['v_w_b_proj'], 'v_w_o': out['v_w_o'], 'v_g_x': out['v_g_x'], 'v_g_mem': out['v_g_mem'], 'v_w_xq': out['v_w_xq'], 'v_w_xkv': out['v_w_xkv'], 'v_w_xo': out['v_w_xo'], 'v_g_ffn': out['v_g_ffn'], 'v_w_gate': out['v_w_gate'], 'v_w_up': out['v_w_up'], 'v_w_down': out['v_w_down'], 'v_g_final': out['v_g_final']}


def _loss(weights, diff, rest, loss_target):
    with _jax.named_scope("forward"):
        args = {**rest, TWIN_DIFF_INPUT: diff, **{k: w.astype(_WEIGHT_DTYPES[k]) for k, w in weights.items()}}
        y = _forward(args)
    with _jax.named_scope("loss_head"):
        err = _jnp.square(y.astype(_jnp.float32) - loss_target)
        return 0.5 * _jnp.sum(_jnp.mean(err, axis=-1)) if err.ndim else 0.5 * err


def _adamw(w, g, m, v):
    m = ADAM_B1 * m + (1.0 - ADAM_B1) * g
    v = ADAM_B2 * v + (1.0 - ADAM_B2) * _jnp.square(g)
    m_hat = m / (1.0 - ADAM_B1 ** ADAM_STEP)
    v_hat = v / (1.0 - ADAM_B2 ** ADAM_STEP)
    delta = -ADAM_LR * (m_hat / (_jnp.sqrt(v_hat) + ADAM_EPS) + ADAM_WD * w)
    return delta, m, v


def reference(x, mem, positions, g_mix, w_in, b_gate, g_q_lat, w_uq, g_kv_lat, w_ukv, w_a_proj, w_b_proj, w_o, g_x, g_mem, w_xq, w_xkv, w_xo, g_ffn, w_gate, w_up, w_down, g_final, loss_target, m_g_mix, m_w_in, m_b_gate, m_g_q_lat, m_w_uq, m_g_kv_lat, m_w_ukv, m_w_a_proj, m_w_b_proj, m_w_o, m_g_x, m_g_mem, m_w_xq, m_w_xkv, m_w_xo, m_g_ffn, m_w_gate, m_w_up, m_w_down, m_g_final, v_g_mix, v_w_in, v_b_gate, v_g_q_lat, v_w_uq, v_g_kv_lat, v_w_ukv, v_w_a_proj, v_w_b_proj, v_w_o, v_g_x, v_g_mem, v_w_xq, v_w_xkv, v_w_xo, v_g_ffn, v_w_gate, v_w_up, v_w_down, v_g_final):
    given = dict(x=x, mem=mem, positions=positions, g_mix=g_mix, w_in=w_in, b_gate=b_gate, g_q_lat=g_q_lat, w_uq=w_uq, g_kv_lat=g_kv_lat, w_ukv=w_ukv, w_a_proj=w_a_proj, w_b_proj=w_b_proj, w_o=w_o, g_x=g_x, g_mem=g_mem, w_xq=w_xq, w_xkv=w_xkv, w_xo=w_xo, g_ffn=g_ffn, w_gate=w_gate, w_up=w_up, w_down=w_down, g_final=g_final, loss_target=loss_target, m_g_mix=m_g_mix, m_w_in=m_w_in, m_b_gate=m_b_gate, m_g_q_lat=m_g_q_lat, m_w_uq=m_w_uq, m_g_kv_lat=m_g_kv_lat, m_w_ukv=m_w_ukv, m_w_a_proj=m_w_a_proj, m_w_b_proj=m_w_b_proj, m_w_o=m_w_o, m_g_x=m_g_x, m_g_mem=m_g_mem, m_w_xq=m_w_xq, m_w_xkv=m_w_xkv, m_w_xo=m_w_xo, m_g_ffn=m_g_ffn, m_w_gate=m_w_gate, m_w_up=m_w_up, m_w_down=m_w_down, m_g_final=m_g_final, v_g_mix=v_g_mix, v_w_in=v_w_in, v_b_gate=v_b_gate, v_g_q_lat=v_g_q_lat, v_w_uq=v_w_uq, v_g_kv_lat=v_g_kv_lat, v_w_ukv=v_w_ukv, v_w_a_proj=v_w_a_proj, v_w_b_proj=v_w_b_proj, v_w_o=v_w_o, v_g_x=v_g_x, v_g_mem=v_g_mem, v_w_xq=v_w_xq, v_w_xkv=v_w_xkv, v_w_xo=v_w_xo, v_g_ffn=v_g_ffn, v_w_gate=v_w_gate, v_w_up=v_w_up, v_w_down=v_w_down, v_g_final=v_g_final)
    weights = {n: given[n] for n in TWIN_WEIGHTS}
    shared = {n: given[n] for n in SHARED_INPUTS}
    per_example = {n: given[n] for n in ['x', 'mem', 'positions']}
    grad_fn = _jax.value_and_grad(_loss, argnums=(0, 1))

    def one_microbatch(ex, loss_target):
        ex = dict(ex)
        diff = ex.pop(TWIN_DIFF_INPUT)
        return grad_fn(weights, diff, {**shared, **ex}, loss_target)

    if N_MICROBATCH == 1:
        loss, (grad_w, grad_x) = one_microbatch(per_example, given["loss_target"])
    else:
        def body(carry, xs):
            loss_sum, grad_sum = carry
            l_k, (gw_k, gx_k) = one_microbatch(xs[0], xs[1])
            with _jax.named_scope("update"):
                return (loss_sum + l_k, _jax.tree.map(_jnp.add, grad_sum, gw_k)), gx_k

        init = (_jnp.zeros((), _jnp.float32), _jax.tree.map(_jnp.zeros_like, weights))
        (loss, grad_w), grad_x = _jax.lax.scan(body, init, (per_example, given["loss_target"]))
    with _jax.named_scope("update"):
        delta_w, new_m, new_v = {}, {}, {}
        for n in TWIN_WEIGHTS:
            delta_w[n], new_m[n], new_v[n] = _adamw(weights[n], grad_w[n], given["m_" + n], given["v_" + n])
    return (loss, grad_x, *[grad_w[n] for n in TWIN_WEIGHTS], *[delta_w[n] for n in TWIN_WEIGHTS],
            *[new_m[n] for n in TWIN_WEIGHTS], *[new_v[n] for n in TWIN_WEIGHTS])
```

```python
import functools
import math

import jax
import jax.numpy as jnp
from jax import lax
from jax.experimental import pallas as pl
from jax.experimental.pallas import tpu as pltpu

F32 = jnp.float32
BF16 = jnp.bfloat16
MESH = pl.DeviceIdType.MESH

D_MODEL = 1024
MLA_HEADS = 8
MLA_Q_RANK = 256
MLA_KV_RANK = 128
MLA_NOPE = 64
MLA_ROPE = 32
MLA_V = 64
ROPE_THETA = 10000.0
SB_HEADS = 8
SB_HEAD_DIM = 64
X_HEADS = 4
X_HEAD_DIM = 128
D_FF = 2816
EPS = 1e-6
ADAM_LR = 0.001
ADAM_B1 = 0.9
ADAM_B2 = 0.999
ADAM_EPS = 1e-08
ADAM_WD = 0.01
ADAM_STEP = 10

LANE = 128
N_CHIPS = 4
VMEM_BYTES = 64 * 1024 * 1024

C_GA, C_GB, C_SBQ, C_SBK, C_SBV, C_CQ, C_CKV, C_KR = 0, 1024, 2048, 3072, 4096, 5120, 5376, 5504
D_INP = 5632
ROPE_LO = MLA_NOPE
HALF = MLA_ROPE // 2

SB_ZERO_LOG = -100.0

SHARDED = (
    ("w_in", (1024, 1000), 1), ("b_gate", (2, 256), 1), ("w_uq", (256, 192), 1), ("w_ukv", (128, 256), 1),
    ("w_a_proj", (512, 256), 1), ("w_b_proj", (512, 256), 1), ("w_o", (256, 1024), 0), ("w_xq", (256, 512), 0),
    ("w_xkv", (256, 1024), 0), ("w_xo", (512, 256), 1), ("w_gate", (1024, 704), 1), ("w_up", (1024, 704), 1),
    ("w_down", (704, 1024), 0),
)
SMALL = ("g_mix", "g_x", "g_mem", "g_ffn", "g_final", "g_q_lat", "g_kv_lat")
PACK_ROWS = 4224
PACK_TILE = 352


def _vmem_limit(block_bytes, temp_bytes):
    est = 2 * block_bytes + temp_bytes + (4 << 20)
    return int(min(max(est, 16 << 20), VMEM_BYTES - (6 << 20)))


def _nbytes(shape, dtype):
    return math.prod(shape) * jnp.dtype(dtype).itemsize


def _tile(n, cap):
    if n <= cap:
        return n
    best = None
    for t in range(LANE, cap + 1, LANE):
        if n % t == 0:
            best = t
    assert best is not None, (n, cap)
    return best


def _mm(a, bs, *, name, ta=False, tb=False, extras=(), epilogue=None, out_dtypes=(F32,), tm=1024, tn=1024, tk=512):
    bs = tuple(bs)
    m, k = (a.shape[1], a.shape[0]) if ta else a.shape
    n = bs[0].shape[0] if tb else bs[0].shape[1]
    tm, tn, tk = _tile(m, tm), _tile(n, tn), _tile(k, tk)
    assert m % tm == 0 and n % tn == 0 and k % tk == 0
    nk = k // tk
    nb, ne, no = len(bs), len(extras), len(out_dtypes)
    dims = (((0,) if ta else (1,)), ((1,) if tb else (0,))), ((), ())
    if epilogue is None:
        epilogue = lambda accs, ex: (accs[0],)

    def body(*refs):
        a_ref, b_refs, e_refs = refs[0], refs[1:1 + nb], refs[1 + nb:1 + nb + ne]
        o_refs, acc_refs = refs[1 + nb + ne:1 + nb + ne + no], refs[1 + nb + ne + no:]
        kk = pl.program_id(2)

        @pl.when(kk == 0)
        def _():
            for acc in acc_refs:
                acc[...] = jnp.zeros_like(acc)

        av = a_ref[...].astype(BF16)
        for b_ref, acc in zip(b_refs, acc_refs):
            acc[...] += lax.dot_general(av, b_ref[...].astype(BF16), dims, preferred_element_type=F32)

        @pl.when(kk == nk - 1)
        def _():
            outs = epilogue([acc[...] for acc in acc_refs], [e[...] for e in e_refs])
            for o_ref, v in zip(o_refs, outs):
                o_ref[...] = v.astype(o_ref.dtype)

    a_spec = pl.BlockSpec((tk, tm), lambda i, j, kk: (kk, i)) if ta else pl.BlockSpec((tm, tk), lambda i, j, kk: (i, kk))
    b_spec = pl.BlockSpec((tn, tk), lambda i, j, kk: (j, kk)) if tb else pl.BlockSpec((tk, tn), lambda i, j, kk: (kk, j))
    mn_spec = pl.BlockSpec((tm, tn), lambda i, j, kk: (i, j))
    blocks = (_nbytes((tm, tk), a.dtype) + sum(_nbytes((tk, tn), b.dtype) for b in bs)
              + sum(_nbytes((tm, tn), e.dtype) for e in extras) + sum(_nbytes((tm, tn), d) for d in out_dtypes))
    temps = (nb + 4) * _nbytes((tm, tn), F32)
    outs = pl.pallas_call(
        body, name=name, grid=(m // tm, n // tn, nk),
        in_specs=[a_spec] + [b_spec] * nb + [mn_spec] * ne,
        out_specs=[mn_spec] * no,
        out_shape=[jax.ShapeDtypeStruct((m, n), d) for d in out_dtypes],
        scratch_shapes=[pltpu.VMEM((tm, tn), F32) for _ in range(nb)],
        compiler_params=pltpu.CompilerParams(
            dimension_semantics=("parallel", "parallel", "arbitrary"),
            vmem_limit_bytes=_vmem_limit(blocks, temps)),
    )(a, *bs, *extras)
    return outs[0] if no == 1 else outs


def _rowwise(body, *, name, rows, tr, row_ins, full_ins=(), row_outs=(), acc_outs=()):
    tr = min(tr, rows)
    assert rows % tr == 0
    n_ri, n_fi, n_ro = len(row_ins), len(full_ins), len(row_outs)

    def kern(*refs):
        body(pl.program_id(0), refs[:n_ri], refs[n_ri:n_ri + n_fi], refs[n_ri + n_fi:n_ri + n_fi + n_ro],
             refs[n_ri + n_fi + n_ro:])

    in_specs = [pl.BlockSpec((tr, w), functools.partial(lambda i, c: (i, c), c=ci)) for _, w, ci in row_ins]
    in_specs += [pl.BlockSpec(f.shape, lambda i: (0, 0)) for f in full_ins]
    out_specs = [pl.BlockSpec((tr, w), lambda i: (i, 0)) for w, _ in row_outs]
    out_specs += [pl.BlockSpec(s, lambda i: (0, 0)) for s, _ in acc_outs]
    out_shape = [jax.ShapeDtypeStruct((rows, w), d) for w, d in row_outs]
    out_shape += [jax.ShapeDtypeStruct(s, d) for s, d in acc_outs]
    blocks = (sum(_nbytes((tr, w), a.dtype) for a, w, _ in row_ins) + sum(_nbytes(f.shape, f.dtype) for f in full_ins)
              + sum(_nbytes((tr, w), d) for w, d in row_outs) + sum(_nbytes(s, d) for s, d in acc_outs))
    widest = max([w for _, w, _ in row_ins] + [w for w, _ in row_outs])
    outs = pl.pallas_call(
        kern, name=name, grid=(rows // tr,), in_specs=in_specs, out_specs=out_specs, out_shape=out_shape,
        compiler_params=pltpu.CompilerParams(
            dimension_semantics=("arbitrary",) if acc_outs else ("parallel",),
            vmem_limit_bytes=_vmem_limit(blocks, 8 * _nbytes((tr, widest), F32))),
    )(*[a for a, _, _ in row_ins], *full_ins)
    return outs


def _rms(x, g):
    r = lax.rsqrt(jnp.mean(x * x, axis=-1, keepdims=True) + EPS)
    return x * r * g


def _rms_bwd(x, g, dy):
    r = lax.rsqrt(jnp.mean(x * x, axis=-1, keepdims=True) + EPS)
    xh = x * r
    dxh = dy * g
    dx = r * (dxh - xh * jnp.mean(dxh * xh, axis=-1, keepdims=True))
    return dx, jnp.sum(dy * xh, axis=0, keepdims=True)


def _sigmoid(x):
    return 1.0 / (1.0 + jnp.exp(-x))


def _acc_init(i, refs):
    @pl.when(i == 0)
    def _():
        for r in refs:
            r[...] = jnp.zeros_like(r)


def _rms_fwd_call(x, g, name):
    rows, c = x.shape

    def body(i, ins, fulls, outs, accs):
        outs[0][...] = _rms(ins[0][...], fulls[0][...]).astype(BF16)

    return _rowwise(body, name=name, rows=rows, tr=512, row_ins=[(x, c, 0)], full_ins=[g], row_outs=[(c, BF16)])[0]


def _rms_bwd_call(x, g, dy, res, name):
    rows, c = x.shape
    row_ins = [(x, c, 0), (dy, c, 0)] + ([(res, c, 0)] if res is not None else [])

    def body(i, ins, fulls, outs, accs):
        _acc_init(i, accs)
        dx, dg = _rms_bwd(ins[0][...], fulls[0][...], ins[1][...].astype(F32))
        if res is not None:
            dx = dx + ins[2][...]
        outs[0][...] = dx
        accs[0][...] += dg

    return _rowwise(body, name=name, rows=rows, tr=512, row_ins=row_ins, full_ins=[g], row_outs=[(c, F32)],
                    acc_outs=[((1, c), F32)])


def _rope_tables(pos_col, freq_lane):
    rows = pos_col.shape[0]

    def body(i, ins, fulls, outs, accs):
        ang = ins[0][...].astype(F32) * fulls[0][...]
        lane = lax.broadcasted_iota(jnp.int32, ang.shape, 1)
        cos, sin = jnp.cos(ang), jnp.sin(ang)
        first = (lane >= ROPE_LO) & (lane < ROPE_LO + HALF)
        second = (lane >= ROPE_LO + HALF) & (lane < ROPE_LO + MLA_ROPE)
        outs[0][:, 0:LANE] = jnp.where(first | second, cos, 1.0)
        outs[0][:, LANE:2 * LANE] = jnp.where(first, -sin, 0.0)
        outs[0][:, 2 * LANE:3 * LANE] = jnp.where(second, sin, 0.0)

    return _rowwise(body, name="rope_tables", rows=rows, tr=1024, row_ins=[(pos_col, 1, 0)], full_ins=[freq_lane],
                    row_outs=[(3 * LANE, F32)])[0]


def _rope(x, tab):
    return (x * tab[:, 0:LANE] + pltpu.roll(x, LANE - HALF, 1) * tab[:, LANE:2 * LANE]
            + pltpu.roll(x, HALF, 1) * tab[:, 2 * LANE:3 * LANE])


def _rope_t(dy, tab):
    return (dy * tab[:, 0:LANE] + pltpu.roll(dy * tab[:, LANE:2 * LANE], HALF, 1)
            + pltpu.roll(dy * tab[:, 2 * LANE:3 * LANE], LANE - HALF, 1))


def _mla_prep_fwd(proj, tab, g_q, g_kv):
    rows = proj.shape[0]

    def body(i, ins, fulls, outs, accs):
        outs[0][...] = _rms(ins[0][...], fulls[0][...]).astype(BF16)
        outs[1][...] = _rms(ins[1][...], fulls[1][...]).astype(BF16)
        outs[2][...] = _rope(ins[2][...], ins[3][...])

    return _rowwise(body, name="mla_prep_fwd", rows=rows, tr=512,
                    row_ins=[(proj, MLA_Q_RANK, C_CQ // MLA_Q_RANK), (proj, LANE, C_CKV // LANE),
                             (proj, LANE, C_KR // LANE), (tab, 3 * LANE, 0)],
                    full_ins=[g_q, g_kv], row_outs=[(MLA_Q_RANK, BF16), (MLA_KV_RANK, BF16), (LANE, F32)])


def _mla_prep_bwd(proj, g_q, g_kv, dcqn, dckvn):
    rows = proj.shape[0]

    def body(i, ins, fulls, outs, accs):
        _acc_init(i, accs)
        dcq, dgq = _rms_bwd(ins[0][...], fulls[0][...], ins[2][...])
        dckv, dgkv = _rms_bwd(ins[1][...], fulls[1][...], ins[3][...])
        outs[0][...] = dcq.astype(BF16)
        outs[1][...] = dckv.astype(BF16)
        accs[0][...] += dgq
        accs[1][...] += dgkv

    return _rowwise(body, name="mla_prep_bwd", rows=rows, tr=512,
                    row_ins=[(proj, MLA_Q_RANK, C_CQ // MLA_Q_RANK), (proj, LANE, C_CKV // LANE),
                             (dcqn, MLA_Q_RANK, 0), (dckvn, MLA_KV_RANK, 0)],
                    full_ins=[g_q, g_kv], row_outs=[(MLA_Q_RANK, BF16), (MLA_KV_RANK, BF16)],
                    acc_outs=[((1, MLA_Q_RANK), F32), ((1, MLA_KV_RANK), F32)])


def _mla_rope_fwd(qp, kvp, krope, tab):
    rows = qp.shape[0]
    hw = MLA_HEADS * LANE

    def body(i, ins, fulls, outs, accs):
        t = ins[3][...]
        kr = ins[2][...]
        for h in range(MLA_HEADS):
            sl = slice(h * LANE, (h + 1) * LANE)
            outs[0][:, sl] = _rope(ins[0][:, sl], t).astype(BF16)
            outs[1][:, sl] = (ins[1][:, sl] + kr).astype(BF16)
        outs[2][...] = ins[1][:, hw:2 * hw].astype(BF16)

    return _rowwise(body, name="mla_rope_fwd", rows=rows, tr=512,
                    row_ins=[(qp, hw, 0), (kvp, 2 * hw, 0), (krope, LANE, 0), (tab, 3 * LANE, 0)],
                    row_outs=[(hw, BF16), (hw, BF16), (hw, BF16)])


def _mla_rope_bwd(dq, dk, dv, tab):
    rows = dq.shape[0]
    hw = MLA_HEADS * LANE

    def body(i, ins, fulls, outs, accs):
        t = ins[3][...]
        dkr = jnp.zeros((ins[0].shape[0], LANE), F32)
        for h in range(MLA_HEADS):
            sl = slice(h * LANE, (h + 1) * LANE)
            outs[0][:, sl] = _rope_t(ins[0][:, sl], t).astype(BF16)
            dkr = dkr + ins[1][:, sl]
        outs[1][:, 0:hw] = ins[1][...].astype(BF16)
        outs[1][:, hw:2 * hw] = ins[2][...].astype(BF16)
        lane = lax.broadcasted_iota(jnp.int32, dkr.shape, 1)
        dkr = jnp.where((lane >= ROPE_LO) & (lane < ROPE_LO + MLA_ROPE), dkr, 0.0)
        outs[2][...] = _rope_t(dkr, t).astype(BF16)

    return _rowwise(body, name="mla_rope_bwd", rows=rows, tr=512,
                    row_ins=[(dq, hw, 0), (dk, hw, 0), (dv, hw, 0), (tab, 3 * LANE, 0)],
                    row_outs=[(hw, BF16), (2 * hw, BF16), (LANE, BF16)])


def _dot_nt(a, b):
    return lax.dot_general(a, b, (((1,), (1,)), ((), ())), preferred_element_type=F32)


def _dot_tn(a, b):
    return lax.dot_general(a, b, (((0,), (0,)), ((), ())), preferred_element_type=F32)


def _dot(a, b):
    return jnp.dot(a, b, preferred_element_type=F32)


def _attn_params(s, t, n_res_f32, n_res_bf16):
    blocks = n_res_f32 * _nbytes((s, LANE), F32) + n_res_bf16 * _nbytes((s, LANE), BF16) + 6 * _nbytes((t, LANE), F32)
    return pltpu.CompilerParams(dimension_semantics=("parallel", "arbitrary"),
                                vmem_limit_bytes=_vmem_limit(blocks, 12 * _nbytes((t, t), F32)))


def _mla_fwd(q, k, v, t):
    s, hw = q.shape
    heads, nq = hw // LANE, s // t
    scale = 1.0 / math.sqrt(MLA_NOPE + MLA_ROPE)

    def body(q_ref, k_ref, v_ref, o_ref, l_ref):
        i = pl.program_id(1)
        qv = q_ref[...]

        def step(j, carry, masked):
            m, l, acc = carry
            sl = pl.ds(pl.multiple_of(j * t, t), t)
            sc = _dot_nt(qv, k_ref[sl, :]) * scale
            if masked:
                row = lax.broadcasted_iota(jnp.int32, (t, t), 0)
                col = lax.broadcasted_iota(jnp.int32, (t, t), 1)
                sc = jnp.where(col <= row, sc, -1e30)
            m_new = jnp.maximum(m, jnp.max(sc, axis=1, keepdims=True))
            p = jnp.exp(sc - m_new)
            alpha = jnp.exp(m - m_new)
            l = alpha * l + jnp.sum(p, axis=1, keepdims=True)
            acc = alpha * acc + _dot(p.astype(BF16), v_ref[sl, :])
            return m_new, l, acc

        init = (jnp.full((t, 1), -1e30, F32), jnp.zeros((t, 1), F32), jnp.zeros((t, LANE), F32))
        carry = lax.fori_loop(0, i, lambda j, c: step(j, c, False), init)
        m, l, acc = step(i, carry, True)
        o_ref[...] = (acc / l).astype(o_ref.dtype)
        l_ref[0] = m + jnp.log(l)

    blk = pl.BlockSpec((t, LANE), lambda h, i: (i, h))
    res = pl.BlockSpec((s, LANE), lambda h, i: (0, h))
    return pl.pallas_call(
        body, name="mla_fwd", grid=(heads, nq), in_specs=[blk, res, res],
        out_specs=[blk, pl.BlockSpec((1, t, 1), lambda h, i: (h, i, 0))],
        out_shape=[jax.ShapeDtypeStruct((s, hw), BF16), jax.ShapeDtypeStruct((heads, s, 1), F32)],
        compiler_params=_attn_params(s, t, 0, 2),
    )(q, k, v)


def _mla_bwd(q, k, v, o, do, lse, t):
    s, hw = q.shape
    heads, nq = hw // LANE, s // t
    scale = 1.0 / math.sqrt(MLA_NOPE + MLA_ROPE)

    def body(q_ref, k_ref, v_ref, o_ref, do_ref, l_ref, dq_ref, dk_ref, dv_ref):
        i = pl.program_id(1)

        @pl.when(i == 0)
        def _():
            dk_ref[...] = jnp.zeros_like(dk_ref)
            dv_ref[...] = jnp.zeros_like(dv_ref)

        qv, dov, lv = q_ref[...], do_ref[...], l_ref[0]
        dlt = jnp.sum(dov.astype(F32) * o_ref[...].astype(F32), axis=1, keepdims=True)

        def step(j, dq, masked):
            sl = pl.ds(pl.multiple_of(j * t, t), t)
            kv, vv = k_ref[sl, :], v_ref[sl, :]
            p = jnp.exp(_dot_nt(qv, kv) * scale - lv)
            if masked:
                row = lax.broadcasted_iota(jnp.int32, (t, t), 0)
                col = lax.broadcasted_iota(jnp.int32, (t, t), 1)
                p = jnp.where(col <= row, p, 0.0)
            ds = (p * (_dot_nt(dov, vv) - dlt) * scale).astype(BF16)
            dk_ref[sl, :] += _dot_tn(ds, qv)
            dv_ref[sl, :] += _dot_tn(p.astype(BF16), dov)
            return dq + _dot(ds, kv)

        dq = lax.fori_loop(0, i, lambda j, c: step(j, c, False), jnp.zeros((t, LANE), F32))
        dq_ref[...] = step(i, dq, True)

    blk = pl.BlockSpec((t, LANE), lambda h, i: (i, h))
    res = pl.BlockSpec((s, LANE), lambda h, i: (0, h))
    full = jax.ShapeDtypeStruct((s, hw), F32)
    return pl.pallas_call(
        body, name="mla_bwd", grid=(heads, nq),
        in_specs=[blk, res, res, blk, blk, pl.BlockSpec((1, t, 1), lambda h, i: (h, i, 0))],
        out_specs=[blk, res, res], out_shape=[full, full, full],
        compiler_params=_attn_params(s, t, 2, 2),
    )(q, k, v, o, do, lse)


def _sb_logits(qv, kv, scale, masked, t, upper):
    z = _dot_nt(qv, kv) * scale
    e = jnp.exp(-jnp.abs(z))
    l1p = jnp.log1p(e)
    lb = jnp.minimum(z, 0.0) - l1p
    lo = -jnp.maximum(z, 0.0) - l1p
    keep = None
    if masked:
        row = lax.broadcasted_iota(jnp.int32, (t, t), 0)
        col = lax.broadcasted_iota(jnp.int32, (t, t), 1)
        keep = col < row
        lo = jnp.where(keep, lo, 0.0)
    hi = lo.astype(BF16)
    rem = (lo - hi.astype(F32)).astype(BF16)
    suf = _dot(hi, upper) + _dot(rem, upper)
    return z, e, lb, lo, suf, keep


def _tri(t, inclusive):
    row = lax.broadcasted_iota(jnp.int32, (t, t), 0)
    col = lax.broadcasted_iota(jnp.int32, (t, t), 1)
    return jnp.where((row >= col) if inclusive else (row > col), 1.0, 0.0).astype(BF16)


def _sb_fwd(proj, t):
    s = proj.shape[0]
    heads, nq = SB_HEADS, s // t
    scale = 1.0 / math.sqrt(SB_HEAD_DIM)

    def body(q_ref, k_ref, v_ref, o_ref):
        i = pl.program_id(1)
        qv = q_ref[...].astype(BF16)
        upper = _tri(t, False)

        def step(jj, carry, masked):
            c, acc = carry
            sl = pl.ds(pl.multiple_of((i - jj) * t, t), t)
            _, _, lb, lo, suf, keep = _sb_logits(qv, k_ref[sl, :].astype(BF16), scale, masked, t, upper)
            a = jnp.exp(lb + suf + c)
            if masked:
                a = jnp.where(keep, a, 0.0)
            acc = acc + _dot(a.astype(BF16), v_ref[sl, :].astype(BF16))
            return c + jnp.sum(lo, axis=1, keepdims=True), acc

        carry = step(0, (jnp.zeros((t, 1), F32), jnp.zeros((t, LANE), F32)), True)
        _, acc = lax.fori_loop(1, i + 1, lambda jj, cr: step(jj, cr, False), carry)
        o_ref[...] = acc

    return pl.pallas_call(
        body, name="sb_fwd", grid=(heads, nq),
        in_specs=[pl.BlockSpec((t, LANE), lambda h, i: (i, C_SBQ // LANE + h)),
                  pl.BlockSpec((s, LANE), lambda h, i: (0, C_SBK // LANE + h)),
                  pl.BlockSpec((s, LANE), lambda h, i: (0, C_SBV // LANE + h))],
        out_specs=pl.BlockSpec((t, LANE), lambda h, i: (i, h)),
        out_shape=jax.ShapeDtypeStruct((s, heads * LANE), F32),
        compiler_params=_attn_params(s, t, 2, 0),
    )(proj, proj, proj)


def _sb_bwd(proj, o, do, t):
    s = proj.shape[0]
    heads, nq = SB_HEADS, s // t
    scale = 1.0 / math.sqrt(SB_HEAD_DIM)

    def body(q_ref, k_ref, v_ref, o_ref, do_ref, dq_ref, dk_ref, dv_ref):
        i = pl.program_id(1)

        @pl.when(i == 0)
        def _():
            dk_ref[...] = jnp.zeros_like(dk_ref)
            dv_ref[...] = jnp.zeros_like(dv_ref)

        qv, dov = q_ref[...].astype(BF16), do_ref[...]
        total = jnp.sum(dov.astype(F32) * o_ref[...], axis=1, keepdims=True)
        upper, upper_incl = _tri(t, False), _tri(t, True)

        def step(jj, carry, masked):
            c, g, dq = carry
            sl = pl.ds(pl.multiple_of((i - jj) * t, t), t)
            kv, vv = k_ref[sl, :].astype(BF16), v_ref[sl, :].astype(BF16)
            z, e, lb, lo, suf, keep = _sb_logits(qv, kv, scale, masked, t, upper)
            tail = suf + c
            a = jnp.exp(lb + tail)
            if masked:
                a = jnp.where(keep, a, 0.0)
            ab = a.astype(BF16)
            gr = ab.astype(F32) * _dot_nt(dov, vv)
            ghi = gr.astype(BF16)
            grem = (gr - ghi.astype(F32)).astype(BF16)
            before = total - g - (_dot(ghi, upper_incl) + _dot(grem, upper_incl))
            before = jnp.where(tail < SB_ZERO_LOG, 0.0, before)
            r = 1.0 / (1.0 + e)
            pos = z >= 0.0
            dz = r * (gr * jnp.where(pos, e, 1.0) - before * jnp.where(pos, 1.0, e))
            if masked:
                dz = jnp.where(keep, dz, 0.0)
            dzb = (dz * scale).astype(BF16)
            dk_ref[sl, :] += _dot_tn(dzb, qv)
            dv_ref[sl, :] += _dot_tn(ab, dov)
            return (c + jnp.sum(lo, axis=1, keepdims=True), g + jnp.sum(gr, axis=1, keepdims=True),
                    dq + _dot(dzb, kv))

        zero = jnp.zeros((t, 1), F32)
        carry = step(0, (zero, zero, jnp.zeros((t, LANE), F32)), True)
        _, _, dq = lax.fori_loop(1, i + 1, lambda jj, cr: step(jj, cr, False), carry)
        dq_ref[...] = dq

    blk = pl.BlockSpec((t, LANE), lambda h, i: (i, h))
    res = pl.BlockSpec((s, LANE), lambda h, i: (0, h))
    full = jax.ShapeDtypeStruct((s, heads * LANE), F32)
    return pl.pallas_call(
        body, name="sb_bwd", grid=(heads, nq),
        in_specs=[pl.BlockSpec((t, LANE), lambda h, i: (i, C_SBQ // LANE + h)),
                  pl.BlockSpec((s, LANE), lambda h, i: (0, C_SBK // LANE + h)),
                  pl.BlockSpec((s, LANE), lambda h, i: (0, C_SBV // LANE + h)), blk, blk],
        out_specs=[blk, res, res], out_shape=[full, full, full],
        compiler_params=_attn_params(s, t, 4, 0),
    )(proj, proj, proj, o, do)


def _xattn_probs(qh, kh):
    sc = _dot_nt(qh, kh) * (1.0 / math.sqrt(X_HEAD_DIM))
    p = jnp.exp(sc - jnp.max(sc, axis=1, keepdims=True))
    return p / jnp.sum(p, axis=1, keepdims=True)


def _xattn_fwd(xq, xkv):
    rows = xq.shape[0]
    w = X_HEADS * X_HEAD_DIM

    def body(i, ins, fulls, outs, accs):
        for h in range(X_HEADS):
            sl = slice(h * LANE, (h + 1) * LANE)
            p = _xattn_probs(ins[0][:, sl], fulls[0][:, sl])
            outs[0][:, sl] = _dot(p.astype(BF16), fulls[0][:, w + h * LANE:w + (h + 1) * LANE]).astype(BF16)

    return _rowwise(body, name="xattn_fwd", rows=rows, tr=512, row_ins=[(xq, w, 0)], full_ins=[xkv],
                    row_outs=[(w, BF16)])[0]


def _xattn_bwd(xq, xkv, dxo):
    rows = xq.shape[0]
    w = X_HEADS * X_HEAD_DIM

    def body(i, ins, fulls, outs, accs):
        _acc_init(i, accs)
        for h in range(X_HEADS):
            sl = slice(h * LANE, (h + 1) * LANE)
            slv = slice(w + h * LANE, w + (h + 1) * LANE)
            qh, kh, vh, doh = ins[0][:, sl], fulls[0][:, sl], fulls[0][:, slv], ins[1][:, sl]
            p = _xattn_probs(qh, kh)
            dp = _dot_nt(doh, vh)
            ds = (p * (dp - jnp.sum(p * dp, axis=1, keepdims=True)) * (1.0 / math.sqrt(X_HEAD_DIM))).astype(BF16)
            outs[0][:, sl] = _dot(ds, kh).astype(BF16)
            accs[0][:, sl] += _dot_tn(ds, qh)
            accs[0][:, slv] += _dot_tn(p.astype(BF16), doh)

    return _rowwise(body, name="xattn_bwd", rows=rows, tr=512, row_ins=[(xq, w, 0), (dxo, w, 0)], full_ins=[xkv],
                    row_outs=[(w, BF16)], acc_outs=[(xkv.shape, F32)])


def _gate_fwd(proj, pa, pb, b_gate):
    rows = proj.shape[0]

    def body(i, ins, fulls, outs, accs):
        sa = _sigmoid(ins[0][...] + fulls[0][0:1, :])
        sb = _sigmoid(ins[1][...] + fulls[0][1:2, :])
        outs[0][...] = (sa * ins[2][...] + sb * ins[3][...]).astype(BF16)

    return _rowwise(body, name="gate_fwd", rows=rows, tr=512,
                    row_ins=[(proj, D_MODEL, C_GA // D_MODEL), (proj, D_MODEL, C_GB // D_MODEL), (pa, D_MODEL, 0),
                             (pb, D_MODEL, 0)],
                    full_ins=[b_gate], row_outs=[(D_MODEL, BF16)])[0]


def _gate_bwd(proj, pa, pb, b_gate, dm):
    rows = proj.shape[0]

    def body(i, ins, fulls, outs, accs):
        _acc_init(i, accs)
        d = ins[4][...]
        sa = _sigmoid(ins[0][...] + fulls[0][0:1, :])
        sb = _sigmoid(ins[1][...] + fulls[0][1:2, :])
        dga = d * ins[2][...] * sa * (1.0 - sa)
        dgb = d * ins[3][...] * sb * (1.0 - sb)
        outs[0][...] = (d * sa).astype(BF16)
        outs[1][...] = (d * sb).astype(BF16)
        outs[2][...] = dga.astype(BF16)
        outs[3][...] = dgb.astype(BF16)
        accs[0][0:1, :] += jnp.sum(dga, axis=0, keepdims=True)
        accs[0][1:2, :] += jnp.sum(dgb, axis=0, keepdims=True)

    return _rowwise(body, name="gate_bwd", rows=rows, tr=512,
                    row_ins=[(proj, D_MODEL, C_GA // D_MODEL), (proj, D_MODEL, C_GB // D_MODEL), (pa, D_MODEL, 0),
                             (pb, D_MODEL, 0), (dm, D_MODEL, 0)],
                    full_ins=[b_gate], row_outs=[(D_MODEL, BF16)] * 4, acc_outs=[((2, D_MODEL), F32)])


def _loss_head(x3, target, g_final):
    rows = x3.shape[0]

    def body(i, ins, fulls, outs, accs):
        _acc_init(i, accs)
        xv, g = ins[0][...], fulls[0][...]
        d = _rms(xv, g) - ins[1][...]
        dx, dg = _rms_bwd(xv, g, d * (1.0 / D_MODEL))
        outs[0][...] = dx
        accs[0][...] += dg
        accs[1][...] += jnp.sum(d * d, axis=0, keepdims=True)

    return _rowwise(body, name="loss_head", rows=rows, tr=512, row_ins=[(x3, D_MODEL, 0), (target, D_MODEL, 0)],
                    full_ins=[g_final], row_outs=[(D_MODEL, F32)], acc_outs=[((1, D_MODEL), F32), ((1, D_MODEL), F32)])


def _adamw(w, g, m, v, name):
    rows, c = w.shape

    def body(i, ins, fulls, outs, accs):
        wv, gv = ins[0][...], ins[1][...]
        mn = ADAM_B1 * ins[2][...] + (1.0 - ADAM_B1) * gv
        vn = ADAM_B2 * ins[3][...] + (1.0 - ADAM_B2) * jnp.square(gv)
        m_hat = mn / (1.0 - ADAM_B1 ** ADAM_STEP)
        v_hat = vn / (1.0 - ADAM_B2 ** ADAM_STEP)
        outs[0][...] = -ADAM_LR * (m_hat / (jnp.sqrt(v_hat) + ADAM_EPS) + ADAM_WD * wv)
        outs[1][...] = mn
        outs[2][...] = vn

    return _rowwise(body, name=name, rows=rows, tr=PACK_TILE, row_ins=[(a, c, 0) for a in (w, g, m, v)],
                    row_outs=[(c, F32)] * 3)


def _place():
    x, y, c = lax.axis_index("x"), lax.axis_index("y"), lax.axis_index("c")
    chips = [(1 - x, y), (x, 1 - y), (1 - x, 1 - y)]
    return x, y, c, chips


ANY = pl.BlockSpec(memory_space=pl.ANY)


def _all_gather_weights(packed):
    rows, cols = packed.shape
    half = rows // 2

    def body(in_ref, out_ref, send_sems, recv_sems, local_sem):
        x, y, c, chips = _place()
        me = 2 * x + y
        mine = pltpu.make_async_copy(in_ref, out_ref.at[me], local_sem)
        mine.start()

        def copy(k, chip_idx, hlf, to, src=None):
            dst = out_ref.at[chip_idx, pl.ds(hlf * half, half), :]
            return pltpu.make_async_remote_copy(src_ref=dst if src is None else src, dst_ref=dst,
                                                send_sem=send_sems.at[k], recv_sem=recv_sems.at[k],
                                                device_id=to, device_id_type=MESH)

        first = [copy(j, me, c, (*chip, c), src=in_ref.at[pl.ds(c * half, half), :]) for j, chip in enumerate(chips)]
        for cp in first:
            cp.start()
        passed = []
        for j, chip in enumerate(chips):
            idx = 2 * chip[0] + chip[1]
            copy(j, idx, c, (x, y, c)).wait_recv()
            fwd = copy(3 + j, idx, c, (x, y, 1 - c))
            fwd.start()
            passed.append(fwd)
        for j, chip in enumerate(chips):
            copy(3 + j, 2 * chip[0] + chip[1], 1 - c, (x, y, c)).wait_recv()
        for cp in first + passed:
            cp.wait_send()
        mine.wait()

    return pl.pallas_call(
        body, name="all_gather_weights", in_specs=[ANY], out_specs=ANY,
        out_shape=jax.ShapeDtypeStruct((N_CHIPS, rows, cols), packed.dtype),
        scratch_shapes=[pltpu.SemaphoreType.DMA((6,)), pltpu.SemaphoreType.DMA((6,)), pltpu.SemaphoreType.DMA],
    )(packed)


def _pair_exchange(theirs, name):
    def body(in_ref, out_ref, send_sem, recv_sem):
        x, y, c, _ = _place()
        cp = pltpu.make_async_remote_copy(src_ref=in_ref, dst_ref=out_ref, send_sem=send_sem, recv_sem=recv_sem,
                                          device_id=(x, y, 1 - c), device_id_type=MESH)
        cp.start()
        cp.wait()

    return pl.pallas_call(
        body, name=name, in_specs=[ANY], out_specs=ANY,
        out_shape=jax.ShapeDtypeStruct(theirs.shape, theirs.dtype),
        scratch_shapes=[pltpu.SemaphoreType.DMA, pltpu.SemaphoreType.DMA],
    )(theirs)


def _chip_scatter(parts):
    _, half, cols = parts.shape

    def body(in_ref, out_ref, send_sems, recv_sems):
        x, y, c, chips = _place()
        sends = []
        for j, chip in enumerate(chips):
            cp = pltpu.make_async_remote_copy(src_ref=in_ref.at[2 * chip[0] + chip[1]], dst_ref=out_ref.at[j],
                                              send_sem=send_sems.at[j], recv_sem=recv_sems.at[j],
                                              device_id=(*chip, c), device_id_type=MESH)
            cp.start()
            sends.append(cp)
        for cp in sends:
            cp.wait()

    return pl.pallas_call(
        body, name="chip_scatter", in_specs=[ANY], out_specs=ANY,
        out_shape=jax.ShapeDtypeStruct((N_CHIPS - 1, half, cols), parts.dtype),
        scratch_shapes=[pltpu.SemaphoreType.DMA((3,)), pltpu.SemaphoreType.DMA((3,))],
    )(parts)


def _pair_sum(mine, got):
    n, half, cols = mine.shape
    a, b = mine.reshape(n * half, cols), got.reshape(n * half, cols)

    def body(i, ins, fulls, outs, accs):
        s = ins[0][...] + ins[1][...]
        outs[0][...] = s.astype(BF16)
        outs[1][...] = s

    lo, hi = _rowwise(body, name="pair_sum", rows=n * half, tr=PACK_TILE, row_ins=[(a, cols, 0), (b, cols, 0)],
                      row_outs=[(cols, BF16), (cols, F32)])
    return lo.reshape(n, half, cols), hi.reshape(n, half, cols)


def _chip_sum(own, got):
    half, cols = own.shape
    g = got.reshape(3 * half, cols)
    nb = half // PACK_TILE

    def kern(o_ref, a_ref, b_ref, c_ref, out_ref):
        out_ref[...] = ((o_ref[...] + a_ref[...].astype(F32)) + b_ref[...].astype(F32)) + c_ref[...].astype(F32)

    spec = lambda off: pl.BlockSpec((PACK_TILE, cols), functools.partial(lambda i, o: (i + o, 0), o=off))
    return pl.pallas_call(
        kern, name="chip_sum", grid=(nb,), in_specs=[spec(0), spec(0), spec(nb), spec(2 * nb)], out_specs=spec(0),
        out_shape=jax.ShapeDtypeStruct((half, cols), F32),
        compiler_params=pltpu.CompilerParams(dimension_semantics=("parallel",)),
    )(own, g, g, g)


def _all_reduce_small(vec):
    r, cols = vec.shape

    def body(in_ref, out_ref, gath, send_sems, recv_sems):
        x, y, c, _ = _place()
        me = 4 * x + 2 * y + c
        gath[me] = in_ref[...]
        sends = []
        for k in range(1, 8):
            to = (x ^ (k >> 2), y ^ ((k >> 1) & 1), c ^ (k & 1))
            cp = pltpu.make_async_remote_copy(src_ref=in_ref, dst_ref=gath.at[me], send_sem=send_sems.at[k - 1],
                                              recv_sem=recv_sems.at[k - 1], device_id=to, device_id_type=MESH)
            cp.start()
            sends.append(cp)
        for k in range(1, 8):
            peer = me ^ k
            pltpu.make_async_remote_copy(src_ref=in_ref, dst_ref=gath.at[peer], send_sem=send_sems.at[k - 1],
                                         recv_sem=recv_sems.at[k - 1], device_id=(x, y, c),
                                         device_id_type=MESH).wait_recv()
        for cp in sends:
            cp.wait_send()
        acc = gath[0]
        for d in range(1, 8):
            acc = acc + gath[d]
        out_ref[...] = acc

    vm = pl.BlockSpec(memory_space=pltpu.VMEM)
    return pl.pallas_call(
        body, name="all_reduce_small", in_specs=[vm], out_specs=vm,
        out_shape=jax.ShapeDtypeStruct((r, cols), F32),
        scratch_shapes=[pltpu.VMEM((8, r, cols), F32), pltpu.SemaphoreType.DMA((7,)), pltpu.SemaphoreType.DMA((7,))],
    )(vec)


def _pack(arrs, dtype):
    flat = jnp.concatenate([a.reshape(-1).astype(dtype) for a in arrs])
    return jnp.pad(flat, (0, PACK_ROWS * 1024 - flat.shape[0])).reshape(PACK_ROWS, 1024)


def _unpack(packed, shapes):
    flat, out, off = packed.reshape(-1), [], 0
    for shp in shapes:
        n = math.prod(shp)
        out.append(flat[off:off + n].reshape(shp))
        off += n
    return out


def _pad_heads(w, heads, dim, axis):
    shp = w.shape[:axis] + (heads, dim) + w.shape[axis + 1:]
    pad = [(0, 0)] * len(shp)
    pad[axis + 1] = (0, LANE - dim)
    w = jnp.pad(w.reshape(shp), pad)
    return w.reshape(w.shape[:axis] + (heads * LANE,) + w.shape[axis + 2:])


def _unpad_heads(w, heads, dim, axis):
    shp = w.shape[:axis] + (heads, LANE) + w.shape[axis + 1:]
    w = lax.slice_in_dim(w.reshape(shp), 0, dim, axis=axis + 1)
    return w.reshape(w.shape[:axis] + (heads * dim,) + w.shape[axis + 2:])


def _w_in_layout(w_in):
    kr = jnp.pad(w_in[:, 384:416], ((0, 0), (ROPE_LO, LANE - ROPE_LO - MLA_ROPE)))
    sb = lambda lo: _pad_heads(w_in[:, lo:lo + 512], SB_HEADS, SB_HEAD_DIM, 1)
    return jnp.concatenate([w_in[:, 1952:2976], w_in[:, 2976:4000], sb(416), sb(928), sb(1440), w_in[:, 0:256],
                            w_in[:, 256:384], kr], axis=1)


def _w_in_unlayout(d):
    sb = lambda lo: _unpad_heads(d[:, lo:lo + 1024], SB_HEADS, SB_HEAD_DIM, 1)
    return jnp.concatenate([d[:, C_CQ:C_CQ + 256], d[:, C_CKV:C_CKV + 128], d[:, C_KR + ROPE_LO:C_KR + ROPE_LO + MLA_ROPE],
                            sb(C_SBQ), sb(C_SBK), sb(C_SBV), d[:, C_GA:C_GA + 1024], d[:, C_GB:C_GB + 1024]], axis=1)


def _w_ukv_layout(w):
    w3 = w.reshape(MLA_KV_RANK, MLA_HEADS, MLA_NOPE + MLA_V)
    pad = lambda part: jnp.pad(part, ((0, 0), (0, 0), (0, LANE - part.shape[2]))).reshape(MLA_KV_RANK, MLA_HEADS * LANE)
    return jnp.concatenate([pad(w3[:, :, :MLA_NOPE]), pad(w3[:, :, MLA_NOPE:])], axis=1)


def _w_ukv_unlayout(d):
    hw = MLA_HEADS * LANE
    kpart = d[:, :hw].reshape(MLA_KV_RANK, MLA_HEADS, LANE)[:, :, :MLA_NOPE]
    vpart = d[:, hw:].reshape(MLA_KV_RANK, MLA_HEADS, LANE)[:, :, :MLA_V]
    return jnp.concatenate([kpart, vpart], axis=2).reshape(MLA_KV_RANK, MLA_HEADS * (MLA_NOPE + MLA_V))


def _shard_of(full, d, axis):
    n = full.shape[axis] // N_CHIPS
    return lax.slice_in_dim(full, d * n, (d + 1) * n, axis=axis)


def _local_step(x, mem, pos, target, w, t_mla, t_sb):
    s = x.shape[0]
    win = _w_in_layout(w["w_in"])
    wuq = _pad_heads(w["w_uq"], MLA_HEADS, MLA_NOPE + MLA_ROPE, 1)
    wkv = _w_ukv_layout(w["w_ukv"])
    wa = _pad_heads(w["w_a_proj"], MLA_HEADS, MLA_V, 0)
    wb = _pad_heads(w["w_b_proj"], SB_HEADS, SB_HEAD_DIM, 0)
    inv_freq = ROPE_THETA ** (-jnp.arange(0, MLA_ROPE, 2, dtype=F32) / MLA_ROPE)
    freq_lane = jnp.pad(jnp.concatenate([inv_freq, inv_freq]), (ROPE_LO, LANE - ROPE_LO - MLA_ROPE)).reshape(1, LANE)
    add = lambda accs, ex: (accs[0] + ex[0],)

    tab = _rope_tables(pos.reshape(s, 1), freq_lane)
    h = _rms_fwd_call(x, w["g_mix"], "rms_mix")
    proj = _mm(h, [win], name="proj_in", tn=1408)
    cqn, ckvn, krope = _mla_prep_fwd(proj, tab, w["g_q_lat"], w["g_kv_lat"])
    qp = _mm(cqn, [wuq], name="q_up")
    kvp = _mm(ckvn, [wkv], name="kv_up")
    qa, ka, va = _mla_rope_fwd(qp, kvp, krope, tab)
    o_a, lse = _mla_fwd(qa, ka, va, t_mla)
    o_b = _sb_fwd(proj, t_sb)
    pa = _mm(o_a, [wa], name="proj_a")
    pb = _mm(o_b, [wb], name="proj_b")
    merged = _gate_fwd(proj, pa, pb, w["b_gate"])
    x1 = _mm(merged, [w["w_o"]], name="proj_o", extras=(x,), epilogue=add)
    hx = _rms_fwd_call(x1, w["g_x"], "rms_x")
    mn = _rms_fwd_call(mem, w["g_mem"], "rms_mem")
    xq = _mm(hx, [w["w_xq"]], name="xq", out_dtypes=(BF16,))
    xkv = _mm(mn, [w["w_xkv"]], name="xkv", out_dtypes=(BF16,))
    xo = _xattn_fwd(xq, xkv)
    x2 = _mm(xo, [w["w_xo"]], name="proj_xo", extras=(x1,), epilogue=add)
    hf = _rms_fwd_call(x2, w["g_ffn"], "rms_ffn")

    def swiglu(accs, ex):
        a, b = accs
        return a, b, a * _sigmoid(a) * b

    ga, gu, hmid = _mm(hf, [w["w_gate"], w["w_up"]], name="ffn_up", epilogue=swiglu, out_dtypes=(BF16, BF16, BF16),
                       tm=512, tn=1408)
    x3 = _mm(hmid, [w["w_down"]], name="ffn_down", extras=(x2,), epilogue=add)

    dx3, dg_final, sq = _loss_head(x3, target, w["g_final"].reshape(1, D_MODEL))
    g = {"g_final": dg_final.reshape(D_MODEL)}

    def swiglu_bwd(accs, ex):
        dh, a, b = accs[0], ex[0].astype(F32), ex[1].astype(F32)
        sg = _sigmoid(a)
        return dh * b * sg * (1.0 + a * (1.0 - sg)), dh * a * sg

    da, db = _mm(dx3, [w["w_down"]], name="ffn_down_dx", tb=True, extras=(ga, gu), epilogue=swiglu_bwd,
                 out_dtypes=(BF16, BF16), tm=512, tn=1408)
    g["w_down"] = _mm(hmid, [dx3], name="ffn_down_dw", ta=True, tm=1408)
    g["w_gate"] = _mm(hf, [da], name="ffn_gate_dw", ta=True, tn=1408)
    g["w_up"] = _mm(hf, [db], name="ffn_up_dw", ta=True, tn=1408)
    dhf = _mm(da, [w["w_gate"]], name="ffn_gate_dx", tb=True)
    dhf = _mm(db, [w["w_up"]], name="ffn_up_dx", tb=True, extras=(dhf,), epilogue=add)
    dx2, g["g_ffn"] = _rms_bwd_call(x2, w["g_ffn"], dhf, dx3, "rms_ffn_bwd")

    dxo = _mm(dx2, [w["w_xo"]], name="proj_xo_dx", tb=True, out_dtypes=(BF16,))
    g["w_xo"] = _mm(xo, [dx2], name="proj_xo_dw", ta=True)
    dxq, dxkv = _xattn_bwd(xq, xkv, dxo)
    dhx = _mm(dxq, [w["w_xq"]], name="xq_dx", tb=True)
    g["w_xq"] = _mm(hx, [dxq], name="xq_dw", ta=True)
    dmn = _mm(dxkv, [w["w_xkv"]], name="xkv_dx", tb=True)
    g["w_xkv"] = _mm(mn, [dxkv], name="xkv_dw", ta=True)
    dx1, g["g_x"] = _rms_bwd_call(x1, w["g_x"], dhx, dx2, "rms_x_bwd")
    _, g["g_mem"] = _rms_bwd_call(mem, w["g_mem"], dmn, None, "rms_mem_bwd")

    dmerged = _mm(dx1, [w["w_o"]], name="proj_o_dx", tb=True)
    g["w_o"] = _mm(merged, [dx1], name="proj_o_dw", ta=True)
    dpa, dpb, dga, dgb, g["b_gate"] = _gate_bwd(proj, pa, pb, w["b_gate"], dmerged)
    do_a = _mm(dpa, [wa], name="proj_a_dx", tb=True, out_dtypes=(BF16,))
    do_b = _mm(dpb, [wb], name="proj_b_dx", tb=True, out_dtypes=(BF16,))
    g["w_a_proj"] = _unpad_heads(_mm(o_a, [dpa], name="proj_a_dw", ta=True), MLA_HEADS, MLA_V, 0)
    g["w_b_proj"] = _unpad_heads(_mm(o_b, [dpb], name="proj_b_dw", ta=True), SB_HEADS, SB_HEAD_DIM, 0)

    dsq, dsk, dsv = _sb_bwd(proj, o_b, do_b, t_sb)
    dqa, dka, dva = _mla_bwd(qa, ka, va, o_a, do_a, lse, t_mla)
    dqp, dkvp, dkr = _mla_rope_bwd(dqa, dka, dva, tab)
    g["w_uq"] = _unpad_heads(_mm(cqn, [dqp], name="q_up_dw", ta=True), MLA_HEADS, MLA_NOPE + MLA_ROPE, 1)
    g["w_ukv"] = _w_ukv_unlayout(_mm(ckvn, [dkvp], name="kv_up_dw", ta=True))
    dcqn = _mm(dqp, [wuq], name="q_up_dx", tb=True)
    dckvn = _mm(dkvp, [wkv], name="kv_up_dx", tb=True)
    dcq, dckv, g["g_q_lat"], g["g_kv_lat"] = _mla_prep_bwd(proj, w["g_q_lat"], w["g_kv_lat"], dcqn, dckvn)

    dproj = jnp.concatenate([dga, dgb, dsq.astype(BF16), dsk.astype(BF16), dsv.astype(BF16), dcq, dckv, dkr], axis=1)
    g["w_in"] = _w_in_unlayout(_mm(h, [dproj], name="proj_in_dw", ta=True, tn=1408))
    dh = _mm(dproj, [win], name="proj_in_dx", tb=True)
    grad_x, g["g_mix"] = _rms_bwd_call(x, w["g_mix"], dh, dx1, "rms_mix_bwd")
    return sq, grad_x, g


def _small_pack(d):
    row5 = jnp.concatenate([d["g_q_lat"].reshape(-1), d["g_kv_lat"].reshape(-1), jnp.zeros((640,), F32)])
    rows = [d[n].reshape(-1) for n in ("g_mix", "g_x", "g_mem", "g_ffn", "g_final")] + [row5]
    return rows


def _small_unpack(p, like):
    out = {n: p[i].reshape(like[n].shape) for i, n in enumerate(("g_mix", "g_x", "g_mem", "g_ffn", "g_final"))}
    out["g_q_lat"] = p[5, 0:256].reshape(like["g_q_lat"].shape)
    out["g_kv_lat"] = p[5, 256:384].reshape(like["g_kv_lat"].shape)
    return out


def kernel(x, mem, positions, g_mix, w_in, b_gate, g_q_lat, w_uq, g_kv_lat, w_ukv, w_a_proj, w_b_proj, w_o, g_x, g_mem, w_xq, w_xkv, w_xo, g_ffn, w_gate, w_up, w_down, g_final, loss_target, m_g_mix, m_w_in, m_b_gate, m_g_q_lat, m_w_uq, m_g_kv_lat, m_w_ukv, m_w_a_proj, m_w_b_proj, m_w_o, m_g_x, m_g_mem, m_w_xq, m_w_xkv, m_w_xo, m_g_ffn, m_w_gate, m_w_up, m_w_down, m_g_final, v_g_mix, v_w_in, v_b_gate, v_g_q_lat, v_w_uq, v_g_kv_lat, v_w_ukv, v_w_a_proj, v_w_b_proj, v_w_o, v_g_x, v_g_mem, v_w_xq, v_w_xkv, v_w_xo, v_g_ffn, v_w_gate, v_w_up, v_w_down, v_g_final):
    given = dict(locals())
    names = [n for n, _, _ in SHARDED] + list(SMALL)
    wts = {n: given[n] for n in names}
    mom = {n: given["m_" + n] for n in names}
    var = {n: given["v_" + n] for n in names}
    shard2d = {n: shp for n, shp, _ in SHARDED}
    axis = {n: ax for n, _, ax in SHARDED}
    c = lax.axis_index("c")

    def payload(n):
        a = wts[n].reshape(shard2d[n])
        return lax.bitcast_convert_type(a, BF16) if n == "b_gate" else a.astype(BF16)

    gathered = _all_gather_weights(_pack([payload(n) for n, _, _ in SHARDED], BF16))
    pay_shapes = [shp + (2,) if n == "b_gate" else shp for n, shp, _ in SHARDED]
    per_chip = [_unpack(gathered[d], pay_shapes) for d in range(N_CHIPS)]
    full = {}
    for k, (n, _, ax) in enumerate(SHARDED):
        parts = [per_chip[d][k] for d in range(N_CHIPS)]
        if n == "b_gate":
            parts = [lax.bitcast_convert_type(p, F32) for p in parts]
        full[n] = jnp.concatenate(parts, axis=ax)
    for n in SMALL:
        full[n] = wts[n].reshape(1, -1) if n != "g_final" else wts[n]

    sq, grad_x, grads = _local_step(x[0], mem[0], positions[0], loss_target[0], full, t_mla=512, t_sb=256)

    by_chip = jnp.stack([_pack([_shard_of(grads[n], d, axis[n]) for n, _, _ in SHARDED], F32) for d in range(N_CHIPS)])
    halves = by_chip.reshape(N_CHIPS, 2, PACK_ROWS // 2, 1024)
    mine = lax.dynamic_index_in_dim(halves, c, axis=1, keepdims=False)
    theirs = lax.dynamic_index_in_dim(halves, 1 - c, axis=1, keepdims=False)
    part_bf, part = _pair_sum(mine, _pair_exchange(theirs, "pair_exchange_grads"))
    me = 2 * lax.axis_index("x") + lax.axis_index("y")
    own = lax.dynamic_index_in_dim(part, me, axis=0, keepdims=False)
    my_half = _chip_sum(own, _chip_scatter(part_bf))
    other_half = _pair_exchange(my_half, "pair_exchange_halves")
    g_shard = jnp.where(c == 0, jnp.concatenate([my_half, other_half]), jnp.concatenate([other_half, my_half]))

    small_rows = _small_pack({n: grads[n] for n in SMALL}) + [sq.reshape(-1), jnp.zeros((1024,), F32)]
    small = _all_reduce_small(jnp.stack(small_rows))
    loss = (0.5 / D_MODEL) * jnp.sum(small[6])
    g_small = _small_unpack(small, wts)

    shapes = [shp for _, shp, _ in SHARDED]
    pk = lambda d: _pack([d[n].reshape(shard2d[n]) for n, _, _ in SHARDED], F32)
    delta_p, m_p, v_p = _adamw(pk(wts), g_shard, pk(mom), pk(var), "adamw_shard")
    sp = lambda d: jnp.stack(_small_pack(d) + [jnp.zeros((1024,), F32)] * 2)
    delta_s, m_s, v_s = _adamw(sp(wts), small.at[6].set(0.0), sp(mom), sp(var), "adamw_small")

    def spread(packed, packed_small):
        out = {n: a.reshape(wts[n].shape) for (n, _, _), a in zip(SHARDED, _unpack(packed, shapes))}
        out.update(_small_unpack(packed_small, wts))
        return out

    order = ["g_mix", "w_in", "b_gate", "g_q_lat", "w_uq", "g_kv_lat", "w_ukv", "w_a_proj", "w_b_proj", "w_o", "g_x",
             "g_mem", "w_xq", "w_xkv", "w_xo", "g_ffn", "w_gate", "w_up", "w_down", "g_final"]
    groups = [spread(g_shard, small), spread(delta_p, delta_s), spread(m_p, m_s), spread(v_p, v_s)]
    return (loss, grad_x[None], *[grp[n] for grp in groups for n in order])
```

```python
import functools
import math

import jax
import jax.numpy as jnp
from jax import lax
from jax.experimental import pallas as pl
from jax.experimental.pallas import tpu as pltpu

F32 = jnp.float32
BF16 = jnp.bfloat16
MESH = pl.DeviceIdType.MESH

D_MODEL = 1024
MLA_HEADS = 8
MLA_Q_RANK = 256
MLA_KV_RANK = 128
MLA_NOPE = 64
MLA_ROPE = 32
MLA_V = 64
ROPE_THETA = 10000.0
SB_HEADS = 8
SB_HEAD_DIM = 64
X_HEADS = 4
X_HEAD_DIM = 128
D_FF = 2816
EPS = 1e-6
ADAM_LR = 0.001
ADAM_B1 = 0.9
ADAM_B2 = 0.999
ADAM_EPS = 1e-08
ADAM_WD = 0.01
ADAM_STEP = 10

LANE = 128
N_CHIPS = 4
VMEM_BYTES = 64 * 1024 * 1024

C_GA, C_GB, C_SBQ, C_SBK, C_SBV, C_CQ, C_CKV, C_KR = 0, 1024, 2048, 3072, 4096, 5120, 5376, 5504
D_INP = 5632
ROPE_LO = MLA_NOPE
HALF = MLA_ROPE // 2

SB_ZERO_LOG = -104.0

SHARDED = (
    ("w_in", (1024, 1000), 1), ("b_gate", (2, 256), 1), ("w_uq", (256, 192), 1), ("w_ukv", (128, 256), 1),
    ("w_a_proj", (512, 256), 1), ("w_b_proj", (512, 256), 1), ("w_o", (256, 1024), 0), ("w_xq", (256, 512), 0),
    ("w_xkv", (256, 1024), 0), ("w_xo", (512, 256), 1), ("w_gate", (1024, 704), 1), ("w_up", (1024, 704), 1),
    ("w_down", (704, 1024), 0),
)
SMALL = ("g_mix", "g_x", "g_mem", "g_ffn", "g_final", "g_q_lat", "g_kv_lat")
PACK_ROWS = 4224
PACK_TILE = 352


def _vmem_limit(block_bytes, temp_bytes):
    est = 2 * block_bytes + temp_bytes + (4 << 20)
    return int(min(max(est, 16 << 20), VMEM_BYTES - (6 << 20)))


def _nbytes(shape, dtype):
    return math.prod(shape) * jnp.dtype(dtype).itemsize


def _tile(n, cap):
    if n <= cap:
        return n
    best = None
    for t in range(LANE, cap + 1, LANE):
        if n % t == 0:
            best = t
    assert best is not None, (n, cap)
    return best


def _mm(a, bs, *, name, ta=False, tb=False, extras=(), epilogue=None, out_dtypes=(F32,), tm=1024, tn=1024, tk=512):
    bs = tuple(bs)
    m, k = (a.shape[1], a.shape[0]) if ta else a.shape
    n = bs[0].shape[0] if tb else bs[0].shape[1]
    tm, tn, tk = _tile(m, tm), _tile(n, tn), _tile(k, tk)
    assert m % tm == 0 and n % tn == 0 and k % tk == 0
    nk = k // tk
    nb, ne, no = len(bs), len(extras), len(out_dtypes)
    dims = (((0,) if ta else (1,)), ((1,) if tb else (0,))), ((), ())
    if epilogue is None:
        epilogue = lambda accs, ex: (accs[0],)

    def body(*refs):
        a_ref, b_refs, e_refs = refs[0], refs[1:1 + nb], refs[1 + nb:1 + nb + ne]
        o_refs, acc_refs = refs[1 + nb + ne:1 + nb + ne + no], refs[1 + nb + ne + no:]
        kk = pl.program_id(2)

        @pl.when(kk == 0)
        def _():
            for acc in acc_refs:
                acc[...] = jnp.zeros_like(acc)

        av = a_ref[...].astype(BF16)
        for b_ref, acc in zip(b_refs, acc_refs):
            acc[...] += lax.dot_general(av, b_ref[...].astype(BF16), dims, preferred_element_type=F32)

        @pl.when(kk == nk - 1)
        def _():
            outs = epilogue([acc[...] for acc in acc_refs], [e[...] for e in e_refs])
            for o_ref, v in zip(o_refs, outs):
                o_ref[...] = v.astype(o_ref.dtype)

    a_spec = pl.BlockSpec((tk, tm), lambda i, j, kk: (kk, i)) if ta else pl.BlockSpec((tm, tk), lambda i, j, kk: (i, kk))
    b_spec = pl.BlockSpec((tn, tk), lambda i, j, kk: (j, kk)) if tb else pl.BlockSpec((tk, tn), lambda i, j, kk: (kk, j))
    mn_spec = pl.BlockSpec((tm, tn), lambda i, j, kk: (i, j))
    blocks = (_nbytes((tm, tk), a.dtype) + sum(_nbytes((tk, tn), b.dtype) for b in bs)
              + sum(_nbytes((tm, tn), e.dtype) for e in extras) + sum(_nbytes((tm, tn), d) for d in out_dtypes))
    temps = (nb + 4) * _nbytes((tm, tn), F32)
    outs = pl.pallas_call(
        body, name=name, grid=(m // tm, n // tn, nk),
        in_specs=[a_spec] + [b_spec] * nb + [mn_spec] * ne,
        out_specs=[mn_spec] * no,
        out_shape=[jax.ShapeDtypeStruct((m, n), d) for d in out_dtypes],
        scratch_shapes=[pltpu.VMEM((tm, tn), F32) for _ in range(nb)],
        compiler_params=pltpu.CompilerParams(
            dimension_semantics=("parallel", "parallel", "arbitrary"),
            vmem_limit_bytes=_vmem_limit(blocks, temps)),
    )(a, *bs, *extras)
    return outs[0] if no == 1 else outs


def _rowwise(body, *, name, rows, tr, row_ins, full_ins=(), row_outs=(), acc_outs=()):
    tr = min(tr, rows)
    assert rows % tr == 0
    n_ri, n_fi, n_ro = len(row_ins), len(full_ins), len(row_outs)

    def kern(*refs):
        body(pl.program_id(0), refs[:n_ri], refs[n_ri:n_ri + n_fi], refs[n_ri + n_fi:n_ri + n_fi + n_ro],
             refs[n_ri + n_fi + n_ro:])

    in_specs = [pl.BlockSpec((tr, w), functools.partial(lambda i, c: (i, c), c=ci)) for _, w, ci in row_ins]
    in_specs += [pl.BlockSpec(f.shape, lambda i: (0, 0)) for f in full_ins]
    out_specs = [pl.BlockSpec((tr, w), lambda i: (i, 0)) for w, _ in row_outs]
    out_specs += [pl.BlockSpec(s, lambda i: (0, 0)) for s, _ in acc_outs]
    out_shape = [jax.ShapeDtypeStruct((rows, w), d) for w, d in row_outs]
    out_shape += [jax.ShapeDtypeStruct(s, d) for s, d in acc_outs]
    blocks = (sum(_nbytes((tr, w), a.dtype) for a, w, _ in row_ins) + sum(_nbytes(f.shape, f.dtype) for f in full_ins)
              + sum(_nbytes((tr, w), d) for w, d in row_outs) + sum(_nbytes(s, d) for s, d in acc_outs))
    widest = max([w for _, w, _ in row_ins] + [w for w, _ in row_outs])
    outs = pl.pallas_call(
        kern, name=name, grid=(rows // tr,), in_specs=in_specs, out_specs=out_specs, out_shape=out_shape,
        compiler_params=pltpu.CompilerParams(
            dimension_semantics=("arbitrary",) if acc_outs else ("parallel",),
            vmem_limit_bytes=_vmem_limit(blocks, 8 * _nbytes((tr, widest), F32))),
    )(*[a for a, _, _ in row_ins], *full_ins)
    return outs


def _rms(x, g):
    r = lax.rsqrt(jnp.mean(x * x, axis=-1, keepdims=True) + EPS)
    return x * r * g


def _rms_bwd(x, g, dy):
    r = lax.rsqrt(jnp.mean(x * x, axis=-1, keepdims=True) + EPS)
    xh = x * r
    dxh = dy * g
    dx = r * (dxh - xh * jnp.mean(dxh * xh, axis=-1, keepdims=True))
    return dx, jnp.sum(dy * xh, axis=0, keepdims=True)


def _sigmoid(x):
    return 1.0 / (1.0 + jnp.exp(-x))


def _acc_init(i, refs):
    @pl.when(i == 0)
    def _():
        for r in refs:
            r[...] = jnp.zeros_like(r)


def _rms_fwd_call(x, g, name):
    rows, c = x.shape

    def body(i, ins, fulls, outs, accs):
        outs[0][...] = _rms(ins[0][...], fulls[0][...]).astype(BF16)

    return _rowwise(body, name=name, rows=rows, tr=512, row_ins=[(x, c, 0)], full_ins=[g], row_outs=[(c, BF16)])[0]


def _rms_bwd_call(x, g, dy, res, name):
    rows, c = x.shape
    row_ins = [(x, c, 0), (dy, c, 0)] + ([(res, c, 0)] if res is not None else [])

    def body(i, ins, fulls, outs, accs):
        _acc_init(i, accs)
        dx, dg = _rms_bwd(ins[0][...], fulls[0][...], ins[1][...].astype(F32))
        if res is not None:
            dx = dx + ins[2][...]
        outs[0][...] = dx
        accs[0][...] += dg

    return _rowwise(body, name=name, rows=rows, tr=512, row_ins=row_ins, full_ins=[g], row_outs=[(c, F32)],
                    acc_outs=[((1, c), F32)])


def _rope_tables(pos_col, freq_lane):
    rows = pos_col.shape[0]

    def body(i, ins, fulls, outs, accs):
        ang = ins[0][...].astype(F32) * fulls[0][...]
        lane = lax.broadcasted_iota(jnp.int32, ang.shape, 1)
        cos, sin = jnp.cos(ang), jnp.sin(ang)
        first = (lane >= ROPE_LO) & (lane < ROPE_LO + HALF)
        second = (lane >= ROPE_LO + HALF) & (lane < ROPE_LO + MLA_ROPE)
        outs[0][:, 0:LANE] = jnp.where(first | second, cos, 1.0)
        outs[0][:, LANE:2 * LANE] = jnp.where(first, -sin, 0.0)
        outs[0][:, 2 * LANE:3 * LANE] = jnp.where(second, sin, 0.0)

    return _rowwise(body, name="rope_tables", rows=rows, tr=1024, row_ins=[(pos_col, 1, 0)], full_ins=[freq_lane],
                    row_outs=[(3 * LANE, F32)])[0]


def _rope(x, tab):
    return (x * tab[:, 0:LANE] + pltpu.roll(x, LANE - HALF, 1) * tab[:, LANE:2 * LANE]
            + pltpu.roll(x, HALF, 1) * tab[:, 2 * LANE:3 * LANE])


def _rope_t(dy, tab):
    return (dy * tab[:, 0:LANE] + pltpu.roll(dy * tab[:, LANE:2 * LANE], HALF, 1)
            + pltpu.roll(dy * tab[:, 2 * LANE:3 * LANE], LANE - HALF, 1))


def _mla_prep_fwd(proj, tab, g_q, g_kv):
    rows = proj.shape[0]

    def body(i, ins, fulls, outs, accs):
        outs[0][...] = _rms(ins[0][...], fulls[0][...]).astype(BF16)
        outs[1][...] = _rms(ins[1][...], fulls[1][...]).astype(BF16)
        outs[2][...] = _rope(ins[2][...], ins[3][...])

    return _rowwise(body, name="mla_prep_fwd", rows=rows, tr=512,
                    row_ins=[(proj, MLA_Q_RANK, C_CQ // MLA_Q_RANK), (proj, LANE, C_CKV // LANE),
                             (proj, LANE, C_KR // LANE), (tab, 3 * LANE, 0)],
                    full_ins=[g_q, g_kv], row_outs=[(MLA_Q_RANK, BF16), (MLA_KV_RANK, BF16), (LANE, F32)])


def _mla_prep_bwd(proj, g_q, g_kv, dcqn, dckvn):
    rows = proj.shape[0]

    def body(i, ins, fulls, outs, accs):
        _acc_init(i, accs)
        dcq, dgq = _rms_bwd(ins[0][...], fulls[0][...], ins[2][...])
        dckv, dgkv = _rms_bwd(ins[1][...], fulls[1][...], ins[3][...])
        outs[0][...] = dcq.astype(BF16)
        outs[1][...] = dckv.astype(BF16)
        accs[0][...] += dgq
        accs[1][...] += dgkv

    return _rowwise(body, name="mla_prep_bwd", rows=rows, tr=512,
                    row_ins=[(proj, MLA_Q_RANK, C_CQ // MLA_Q_RANK), (proj, LANE, C_CKV // LANE),
                             (dcqn, MLA_Q_RANK, 0), (dckvn, MLA_KV_RANK, 0)],
                    full_ins=[g_q, g_kv], row_outs=[(MLA_Q_RANK, BF16), (MLA_KV_RANK, BF16)],
                    acc_outs=[((1, MLA_Q_RANK), F32), ((1, MLA_KV_RANK), F32)])


def _mla_rope_fwd(qp, kvp, krope, tab):
    rows = qp.shape[0]
    hw = MLA_HEADS * LANE

    def body(i, ins, fulls, outs, accs):
        t = ins[3][...]
        kr = ins[2][...]
        for h in range(MLA_HEADS):
            sl = slice(h * LANE, (h + 1) * LANE)
            outs[0][:, sl] = _rope(ins[0][:, sl], t).astype(BF16)
            outs[1][:, sl] = (ins[1][:, sl] + kr).astype(BF16)
        outs[2][...] = ins[1][:, hw:2 * hw].astype(BF16)

    return _rowwise(body, name="mla_rope_fwd", rows=rows, tr=512,
                    row_ins=[(qp, hw, 0), (kvp, 2 * hw, 0), (krope, LANE, 0), (tab, 3 * LANE, 0)],
                    row_outs=[(hw, BF16), (hw, BF16), (hw, BF16)])


def _mla_rope_bwd(dq, dk, dv, tab):
    rows = dq.shape[0]
    hw = MLA_HEADS * LANE

    def body(i, ins, fulls, outs, accs):
        t = ins[3][...]
        dkr = jnp.zeros((ins[0].shape[0], LANE), F32)
        for h in range(MLA_HEADS):
            sl = slice(h * LANE, (h + 1) * LANE)
            outs[0][:, sl] = _rope_t(ins[0][:, sl], t).astype(BF16)
            dkr = dkr + ins[1][:, sl]
        outs[1][:, 0:hw] = ins[1][...].astype(BF16)
        outs[1][:, hw:2 * hw] = ins[2][...].astype(BF16)
        lane = lax.broadcasted_iota(jnp.int32, dkr.shape, 1)
        dkr = jnp.where((lane >= ROPE_LO) & (lane < ROPE_LO + MLA_ROPE), dkr, 0.0)
        outs[2][...] = _rope_t(dkr, t).astype(BF16)

    return _rowwise(body, name="mla_rope_bwd", rows=rows, tr=512,
                    row_ins=[(dq, hw, 0), (dk, hw, 0), (dv, hw, 0), (tab, 3 * LANE, 0)],
                    row_outs=[(hw, BF16), (2 * hw, BF16), (LANE, BF16)])


def _dot_nt(a, b):
    return lax.dot_general(a, b, (((1,), (1,)), ((), ())), preferred_element_type=F32)


def _dot_tn(a, b):
    return lax.dot_general(a, b, (((0,), (0,)), ((), ())), preferred_element_type=F32)


def _dot(a, b):
    return jnp.dot(a, b, preferred_element_type=F32)


def _attn_params(s, t, n_res_f32, n_res_bf16):
    blocks = n_res_f32 * _nbytes((s, LANE), F32) + n_res_bf16 * _nbytes((s, LANE), BF16) + 6 * _nbytes((t, LANE), F32)
    return pltpu.CompilerParams(dimension_semantics=("parallel", "arbitrary"),
                                vmem_limit_bytes=_vmem_limit(blocks, 12 * _nbytes((t, t), F32)))


def _mla_fwd(q, k, v, t):
    s, hw = q.shape
    heads, nq = hw // LANE, s // t
    scale = 1.0 / math.sqrt(MLA_NOPE + MLA_ROPE)

    def body(q_ref, k_ref, v_ref, o_ref, l_ref):
        i = pl.program_id(1)
        qv = q_ref[...]

        def step(j, carry, masked):
            m, l, acc = carry
            sl = pl.ds(pl.multiple_of(j * t, t), t)
            sc = _dot_nt(qv, k_ref[sl, :]) * scale
            if masked:
                row = lax.broadcasted_iota(jnp.int32, (t, t), 0)
                col = lax.broadcasted_iota(jnp.int32, (t, t), 1)
                sc = jnp.where(col <= row, sc, -1e30)
            m_new = jnp.maximum(m, jnp.max(sc, axis=1, keepdims=True))
            p = jnp.exp(sc - m_new)
            alpha = jnp.exp(m - m_new)
            l = alpha * l + jnp.sum(p, axis=1, keepdims=True)
            acc = alpha * acc + _dot(p.astype(BF16), v_ref[sl, :])
            return m_new, l, acc

        init = (jnp.full((t, 1), -1e30, F32), jnp.zeros((t, 1), F32), jnp.zeros((t, LANE), F32))
        carry = lax.fori_loop(0, i, lambda j, c: step(j, c, False), init)
        m, l, acc = step(i, carry, True)
        o_ref[...] = (acc / l).astype(o_ref.dtype)
        l_ref[0] = m + jnp.log(l)

    blk = pl.BlockSpec((t, LANE), lambda h, i: (i, h))
    res = pl.BlockSpec((s, LANE), lambda h, i: (0, h))
    return pl.pallas_call(
        body, name="mla_fwd", grid=(heads, nq), in_specs=[blk, res, res],
        out_specs=[blk, pl.BlockSpec((1, t, 1), lambda h, i: (h, i, 0))],
        out_shape=[jax.ShapeDtypeStruct((s, hw), BF16), jax.ShapeDtypeStruct((heads, s, 1), F32)],
        compiler_params=_attn_params(s, t, 0, 2),
    )(q, k, v)


def _mla_bwd(q, k, v, o, do, lse, t):
    s, hw = q.shape
    heads, nq = hw // LANE, s // t
    scale = 1.0 / math.sqrt(MLA_NOPE + MLA_ROPE)

    def body(q_ref, k_ref, v_ref, o_ref, do_ref, l_ref, dq_ref, dk_ref, dv_ref):
        i = pl.program_id(1)

        @pl.when(i == 0)
        def _():
            dk_ref[...] = jnp.zeros_like(dk_ref)
            dv_ref[...] = jnp.zeros_like(dv_ref)

        qv, dov, lv = q_ref[...], do_ref[...], l_ref[0]
        dlt = jnp.sum(dov.astype(F32) * o_ref[...].astype(F32), axis=1, keepdims=True)

        def step(j, dq, masked):
            sl = pl.ds(pl.multiple_of(j * t, t), t)
            kv, vv = k_ref[sl, :], v_ref[sl, :]
            p = jnp.exp(_dot_nt(qv, kv) * scale - lv)
            if masked:
                row = lax.broadcasted_iota(jnp.int32, (t, t), 0)
                col = lax.broadcasted_iota(jnp.int32, (t, t), 1)
                p = jnp.where(col <= row, p, 0.0)
            ds = (p * (_dot_nt(dov, vv) - dlt) * scale).astype(BF16)
            dk_ref[sl, :] += _dot_tn(ds, qv)
            dv_ref[sl, :] += _dot_tn(p.astype(BF16), dov)
            return dq + _dot(ds, kv)

        dq = lax.fori_loop(0, i, lambda j, c: step(j, c, False), jnp.zeros((t, LANE), F32))
        dq_ref[...] = step(i, dq, True)

    blk = pl.BlockSpec((t, LANE), lambda h, i: (i, h))
    res = pl.BlockSpec((s, LANE), lambda h, i: (0, h))
    full = jax.ShapeDtypeStruct((s, hw), F32)
    return pl.pallas_call(
        body, name="mla_bwd", grid=(heads, nq),
        in_specs=[blk, res, res, blk, blk, pl.BlockSpec((1, t, 1), lambda h, i: (h, i, 0))],
        out_specs=[blk, res, res], out_shape=[full, full, full],
        compiler_params=_attn_params(s, t, 2, 2),
    )(q, k, v, o, do, lse)


def _sb_logits(qv, kv, scale, masked, t, upper):
    z = _dot_nt(qv, kv) * scale
    e = jnp.exp(-jnp.abs(z))
    l1p = jnp.log1p(e)
    lb = jnp.minimum(z, 0.0) - l1p
    lo = -jnp.maximum(z, 0.0) - l1p
    keep = None
    if masked:
        row = lax.broadcasted_iota(jnp.int32, (t, t), 0)
        col = lax.broadcasted_iota(jnp.int32, (t, t), 1)
        keep = col < row
        lo = jnp.where(keep, lo, 0.0)
    hi = lo.astype(BF16)
    rem = (lo - hi.astype(F32)).astype(BF16)
    suf = _dot(hi, upper) + _dot(rem, upper)
    return z, e, lb, lo, suf, keep


def _tri(t, inclusive):
    row = lax.broadcasted_iota(jnp.int32, (t, t), 0)
    col = lax.broadcasted_iota(jnp.int32, (t, t), 1)
    return jnp.where((row >= col) if inclusive else (row > col), 1.0, 0.0).astype(BF16)


def _sb_fwd(proj, t):
    s = proj.shape[0]
    heads, nq = SB_HEADS, s // t
    scale = 1.0 / math.sqrt(SB_HEAD_DIM)

    def body(q_ref, k_ref, v_ref, o_ref):
        i = pl.program_id(1)
        qv = q_ref[...].astype(BF16)
        upper = _tri(t, False)

        def step(jj, carry, masked):
            c, acc = carry
            sl = pl.ds(pl.multiple_of((i - jj) * t, t), t)
            _, _, lb, lo, suf, keep = _sb_logits(qv, k_ref[sl, :].astype(BF16), scale, masked, t, upper)
            a = jnp.exp(lb + suf + c)
            if masked:
                a = jnp.where(keep, a, 0.0)
            acc = acc + _dot(a.astype(BF16), v_ref[sl, :].astype(BF16))
            return c + jnp.sum(lo, axis=1, keepdims=True), acc

        carry = step(0, (jnp.zeros((t, 1), F32), jnp.zeros((t, LANE), F32)), True)
        def live(st):
            return (st[0] <= i) & (jnp.max(st[1]) >= SB_ZERO_LOG)

        def more(st):
            return (st[0] + 1,) + step(st[0], st[1:], False)

        o_ref[...] = lax.while_loop(live, more, (jnp.int32(1),) + carry)[2]

    return pl.pallas_call(
        body, name="sb_fwd", grid=(heads, nq),
        in_specs=[pl.BlockSpec((t, LANE), lambda h, i: (i, C_SBQ // LANE + h)),
                  pl.BlockSpec((s, LANE), lambda h, i: (0, C_SBK // LANE + h)),
                  pl.BlockSpec((s, LANE), lambda h, i: (0, C_SBV // LANE + h))],
        out_specs=pl.BlockSpec((t, LANE), lambda h, i: (i, h)),
        out_shape=jax.ShapeDtypeStruct((s, heads * LANE), F32),
        compiler_params=_attn_params(s, t, 2, 0),
    )(proj, proj, proj)


def _sb_bwd(proj, o, do, t):
    s = proj.shape[0]
    heads, nq = SB_HEADS, s // t
    scale = 1.0 / math.sqrt(SB_HEAD_DIM)

    def body(q_ref, k_ref, v_ref, o_ref, do_ref, dq_ref, dk_ref, dv_ref):
        i = pl.program_id(1)

        @pl.when(i == 0)
        def _():
            dk_ref[...] = jnp.zeros_like(dk_ref)
            dv_ref[...] = jnp.zeros_like(dv_ref)

        qv, dov = q_ref[...].astype(BF16), do_ref[...]
        total = jnp.sum(dov.astype(F32) * o_ref[...], axis=1, keepdims=True)
        upper, upper_incl = _tri(t, False), _tri(t, True)

        def step(jj, carry, masked):
            c, g, dq = carry
            sl = pl.ds(pl.multiple_of((i - jj) * t, t), t)
            kv, vv = k_ref[sl, :].astype(BF16), v_ref[sl, :].astype(BF16)
            z, e, lb, lo, suf, keep = _sb_logits(qv, kv, scale, masked, t, upper)
            tail = suf + c
            a = jnp.exp(lb + tail)
            if masked:
                a = jnp.where(keep, a, 0.0)
            ab = a.astype(BF16)
            gr = ab.astype(F32) * _dot_nt(dov, vv)
            ghi = gr.astype(BF16)
            grem = (gr - ghi.astype(F32)).astype(BF16)
            before = total - g - (_dot(ghi, upper_incl) + _dot(grem, upper_incl))
            before = jnp.where(tail < SB_ZERO_LOG, 0.0, before)
            r = 1.0 / (1.0 + e)
            pos = z >= 0.0
            dz = r * (gr * jnp.where(pos, e, 1.0) - before * jnp.where(pos, 1.0, e))
            if masked:
                dz = jnp.where(keep, dz, 0.0)
            dzb = (dz * scale).astype(BF16)
            dk_ref[sl, :] += _dot_tn(dzb, qv)
            dv_ref[sl, :] += _dot_tn(ab, dov)
            return (c + jnp.sum(lo, axis=1, keepdims=True), g + jnp.sum(gr, axis=1, keepdims=True),
                    dq + _dot(dzb, kv))

        zero = jnp.zeros((t, 1), F32)
        carry = step(0, (zero, zero, jnp.zeros((t, LANE), F32)), True)
        def live(st):
            return (st[0] <= i) & (jnp.max(st[1]) >= SB_ZERO_LOG)

        def more(st):
            return (st[0] + 1,) + step(st[0], st[1:], False)

        dq_ref[...] = lax.while_loop(live, more, (jnp.int32(1),) + carry)[3]

    blk = pl.BlockSpec((t, LANE), lambda h, i: (i, h))
    res = pl.BlockSpec((s, LANE), lambda h, i: (0, h))
    full = jax.ShapeDtypeStruct((s, heads * LANE), F32)
    return pl.pallas_call(
        body, name="sb_bwd", grid=(heads, nq),
        in_specs=[pl.BlockSpec((t, LANE), lambda h, i: (i, C_SBQ // LANE + h)),
                  pl.BlockSpec((s, LANE), lambda h, i: (0, C_SBK // LANE + h)),
                  pl.BlockSpec((s, LANE), lambda h, i: (0, C_SBV // LANE + h)), blk, blk],
        out_specs=[blk, res, res], out_shape=[full, full, full],
        compiler_params=_attn_params(s, t, 4, 0),
    )(proj, proj, proj, o, do)


def _xattn_probs(qh, kh):
    sc = _dot_nt(qh, kh) * (1.0 / math.sqrt(X_HEAD_DIM))
    p = jnp.exp(sc - jnp.max(sc, axis=1, keepdims=True))
    return p / jnp.sum(p, axis=1, keepdims=True)


def _xattn_fwd(xq, xkv):
    rows = xq.shape[0]
    w = X_HEADS * X_HEAD_DIM

    def body(i, ins, fulls, outs, accs):
        for h in range(X_HEADS):
            sl = slice(h * LANE, (h + 1) * LANE)
            p = _xattn_probs(ins[0][:, sl], fulls[0][:, sl])
            outs[0][:, sl] = _dot(p.astype(BF16), fulls[0][:, w + h * LANE:w + (h + 1) * LANE]).astype(BF16)

    return _rowwise(body, name="xattn_fwd", rows=rows, tr=512, row_ins=[(xq, w, 0)], full_ins=[xkv],
                    row_outs=[(w, BF16)])[0]


def _xattn_bwd(xq, xkv, dxo):
    rows = xq.shape[0]
    w = X_HEADS * X_HEAD_DIM

    def body(i, ins, fulls, outs, accs):
        _acc_init(i, accs)
        for h in range(X_HEADS):
            sl = slice(h * LANE, (h + 1) * LANE)
            slv = slice(w + h * LANE, w + (h + 1) * LANE)
            qh, kh, vh, doh = ins[0][:, sl], fulls[0][:, sl], fulls[0][:, slv], ins[1][:, sl]
            p = _xattn_probs(qh, kh)
            dp = _dot_nt(doh, vh)
            ds = (p * (dp - jnp.sum(p * dp, axis=1, keepdims=True)) * (1.0 / math.sqrt(X_HEAD_DIM))).astype(BF16)
            outs[0][:, sl] = _dot(ds, kh).astype(BF16)
            accs[0][:, sl] += _dot_tn(ds, qh)
            accs[0][:, slv] += _dot_tn(p.astype(BF16), doh)

    return _rowwise(body, name="xattn_bwd", rows=rows, tr=512, row_ins=[(xq, w, 0), (dxo, w, 0)], full_ins=[xkv],
                    row_outs=[(w, BF16)], acc_outs=[(xkv.shape, F32)])


def _gate_fwd(proj, pa, pb, b_gate):
    rows = proj.shape[0]

    def body(i, ins, fulls, outs, accs):
        sa = _sigmoid(ins[0][...] + fulls[0][0:1, :])
        sb = _sigmoid(ins[1][...] + fulls[0][1:2, :])
        outs[0][...] = (sa * ins[2][...] + sb * ins[3][...]).astype(BF16)

    return _rowwise(body, name="gate_fwd", rows=rows, tr=512,
                    row_ins=[(proj, D_MODEL, C_GA // D_MODEL), (proj, D_MODEL, C_GB // D_MODEL), (pa, D_MODEL, 0),
                             (pb, D_MODEL, 0)],
                    full_ins=[b_gate], row_outs=[(D_MODEL, BF16)])[0]


def _gate_bwd(proj, pa, pb, b_gate, dm):
    rows = proj.shape[0]

    def body(i, ins, fulls, outs, accs):
        _acc_init(i, accs)
        d = ins[4][...]
        sa = _sigmoid(ins[0][...] + fulls[0][0:1, :])
        sb = _sigmoid(ins[1][...] + fulls[0][1:2, :])
        dga = d * ins[2][...] * sa * (1.0 - sa)
        dgb = d * ins[3][...] * sb * (1.0 - sb)
        outs[0][...] = (d * sa).astype(BF16)
        outs[1][...] = (d * sb).astype(BF16)
        outs[2][...] = dga.astype(BF16)
        outs[3][...] = dgb.astype(BF16)
        accs[0][0:1, :] += jnp.sum(dga, axis=0, keepdims=True)
        accs[0][1:2, :] += jnp.sum(dgb, axis=0, keepdims=True)

    return _rowwise(body, name="gate_bwd", rows=rows, tr=512,
                    row_ins=[(proj, D_MODEL, C_GA // D_MODEL), (proj, D_MODEL, C_GB // D_MODEL), (pa, D_MODEL, 0),
                             (pb, D_MODEL, 0), (dm, D_MODEL, 0)],
                    full_ins=[b_gate], row_outs=[(D_MODEL, BF16)] * 4, acc_outs=[((2, D_MODEL), F32)])


def _loss_head(x3, target, g_final):
    rows = x3.shape[0]

    def body(i, ins, fulls, outs, accs):
        _acc_init(i, accs)
        xv, g = ins[0][...], fulls[0][...]
        d = _rms(xv, g) - ins[1][...]
        dx, dg = _rms_bwd(xv, g, d * (1.0 / D_MODEL))
        outs[0][...] = dx
        accs[0][...] += dg
        accs[1][...] += jnp.sum(d * d, axis=0, keepdims=True)

    return _rowwise(body, name="loss_head", rows=rows, tr=512, row_ins=[(x3, D_MODEL, 0), (target, D_MODEL, 0)],
                    full_ins=[g_final], row_outs=[(D_MODEL, F32)], acc_outs=[((1, D_MODEL), F32), ((1, D_MODEL), F32)])


def _adamw(w, g, m, v, name):
    rows, c = w.shape

    def body(i, ins, fulls, outs, accs):
        wv, gv = ins[0][...], ins[1][...]
        mn = ADAM_B1 * ins[2][...] + (1.0 - ADAM_B1) * gv
        vn = ADAM_B2 * ins[3][...] + (1.0 - ADAM_B2) * jnp.square(gv)
        m_hat = mn / (1.0 - ADAM_B1 ** ADAM_STEP)
        v_hat = vn / (1.0 - ADAM_B2 ** ADAM_STEP)
        outs[0][...] = -ADAM_LR * (m_hat / (jnp.sqrt(v_hat) + ADAM_EPS) + ADAM_WD * wv)
        outs[1][...] = mn
        outs[2][...] = vn

    return _rowwise(body, name=name, rows=rows, tr=PACK_TILE, row_ins=[(a, c, 0) for a in (w, g, m, v)],
                    row_outs=[(c, F32)] * 3)


def _place():
    x, y, c = lax.axis_index("x"), lax.axis_index("y"), lax.axis_index("c")
    chips = [(1 - x, y), (x, 1 - y), (1 - x, 1 - y)]
    return x, y, c, chips


ANY = pl.BlockSpec(memory_space=pl.ANY)


def _all_gather_weights(packed):
    rows, cols = packed.shape
    half = rows // 2

    def body(in_ref, out_ref, send_sems, recv_sems, local_sem):
        x, y, c, chips = _place()
        me = 2 * x + y
        mine = pltpu.make_async_copy(in_ref, out_ref.at[me], local_sem)
        mine.start()

        def copy(k, chip_idx, hlf, to, src=None):
            dst = out_ref.at[chip_idx, pl.ds(hlf * half, half), :]
            return pltpu.make_async_remote_copy(src_ref=dst if src is None else src, dst_ref=dst,
                                                send_sem=send_sems.at[k], recv_sem=recv_sems.at[k],
                                                device_id=to, device_id_type=MESH)

        first = [copy(j, me, c, (*chip, c), src=in_ref.at[pl.ds(c * half, half), :]) for j, chip in enumerate(chips)]
        for cp in first:
            cp.start()
        passed = []
        for j, chip in enumerate(chips):
            idx = 2 * chip[0] + chip[1]
            copy(j, idx, c, (x, y, c)).wait_recv()
            fwd = copy(3 + j, idx, c, (x, y, 1 - c))
            fwd.start()
            passed.append(fwd)
        for j, chip in enumerate(chips):
            copy(3 + j, 2 * chip[0] + chip[1], 1 - c, (x, y, c)).wait_recv()
        for cp in first + passed:
            cp.wait_send()
        mine.wait()

    return pl.pallas_call(
        body, name="all_gather_weights", in_specs=[ANY], out_specs=ANY,
        out_shape=jax.ShapeDtypeStruct((N_CHIPS, rows, cols), packed.dtype),
        scratch_shapes=[pltpu.SemaphoreType.DMA((6,)), pltpu.SemaphoreType.DMA((6,)), pltpu.SemaphoreType.DMA],
    )(packed)


def _pair_exchange(theirs, name):
    def body(in_ref, out_ref, send_sem, recv_sem):
        x, y, c, _ = _place()
        cp = pltpu.make_async_remote_copy(src_ref=in_ref, dst_ref=out_ref, send_sem=send_sem, recv_sem=recv_sem,
                                          device_id=(x, y, 1 - c), device_id_type=MESH)
        cp.start()
        cp.wait()

    return pl.pallas_call(
        body, name=name, in_specs=[ANY], out_specs=ANY,
        out_shape=jax.ShapeDtypeStruct(theirs.shape, theirs.dtype),
        scratch_shapes=[pltpu.SemaphoreType.DMA, pltpu.SemaphoreType.DMA],
    )(theirs)


def _chip_scatter(parts):
    _, half, cols = parts.shape

    def body(in_ref, out_ref, send_sems, recv_sems):
        x, y, c, chips = _place()
        sends = []
        for j, chip in enumerate(chips):
            cp = pltpu.make_async_remote_copy(src_ref=in_ref.at[2 * chip[0] + chip[1]], dst_ref=out_ref.at[j],
                                              send_sem=send_sems.at[j], recv_sem=recv_sems.at[j],
                                              device_id=(*chip, c), device_id_type=MESH)
            cp.start()
            sends.append(cp)
        for cp in sends:
            cp.wait()

    return pl.pallas_call(
        body, name="chip_scatter", in_specs=[ANY], out_specs=ANY,
        out_shape=jax.ShapeDtypeStruct((N_CHIPS - 1, half, cols), parts.dtype),
        scratch_shapes=[pltpu.SemaphoreType.DMA((3,)), pltpu.SemaphoreType.DMA((3,))],
    )(parts)


def _pair_sum(mine, got):
    n, half, cols = mine.shape
    a, b = mine.reshape(n * half, cols), got.reshape(n * half, cols)

    def body(i, ins, fulls, outs, accs):
        s = ins[0][...] + ins[1][...]
        outs[0][...] = s.astype(BF16)
        outs[1][...] = s

    lo, hi = _rowwise(body, name="pair_sum", rows=n * half, tr=PACK_TILE, row_ins=[(a, cols, 0), (b, cols, 0)],
                      row_outs=[(cols, BF16), (cols, F32)])
    return lo.reshape(n, half, cols), hi.reshape(n, half, cols)


def _chip_sum(own, got):
    half, cols = own.shape
    g = got.reshape(3 * half, cols)
    nb = half // PACK_TILE

    def kern(o_ref, a_ref, b_ref, c_ref, out_ref):
        out_ref[...] = ((o_ref[...] + a_ref[...].astype(F32)) + b_ref[...].astype(F32)) + c_ref[...].astype(F32)

    spec = lambda off: pl.BlockSpec((PACK_TILE, cols), functools.partial(lambda i, o: (i + o, 0), o=off))
    return pl.pallas_call(
        kern, name="chip_sum", grid=(nb,), in_specs=[spec(0), spec(0), spec(nb), spec(2 * nb)], out_specs=spec(0),
        out_shape=jax.ShapeDtypeStruct((half, cols), F32),
        compiler_params=pltpu.CompilerParams(dimension_semantics=("parallel",)),
    )(own, g, g, g)


def _all_reduce_small(vec):
    r, cols = vec.shape

    def body(in_ref, out_ref, gath, send_sems, recv_sems):
        x, y, c, _ = _place()
        me = 4 * x + 2 * y + c
        gath[me] = in_ref[...]
        sends = []
        for k in range(1, 8):
            to = (x ^ (k >> 2), y ^ ((k >> 1) & 1), c ^ (k & 1))
            cp = pltpu.make_async_remote_copy(src_ref=in_ref, dst_ref=gath.at[me], send_sem=send_sems.at[k - 1],
                                              recv_sem=recv_sems.at[k - 1], device_id=to, device_id_type=MESH)
            cp.start()
            sends.append(cp)
        for k in range(1, 8):
            peer = me ^ k
            pltpu.make_async_remote_copy(src_ref=in_ref, dst_ref=gath.at[peer], send_sem=send_sems.at[k - 1],
                                         recv_sem=recv_sems.at[k - 1], device_id=(x, y, c),
                                         device_id_type=MESH).wait_recv()
        for cp in sends:
            cp.wait_send()
        acc = gath[0]
        for d in range(1, 8):
            acc = acc + gath[d]
        out_ref[...] = acc

    vm = pl.BlockSpec(memory_space=pltpu.VMEM)
    return pl.pallas_call(
        body, name="all_reduce_small", in_specs=[vm], out_specs=vm,
        out_shape=jax.ShapeDtypeStruct((r, cols), F32),
        scratch_shapes=[pltpu.VMEM((8, r, cols), F32), pltpu.SemaphoreType.DMA((7,)), pltpu.SemaphoreType.DMA((7,))],
    )(vec)


def _pack(arrs, dtype):
    flat = jnp.concatenate([a.reshape(-1).astype(dtype) for a in arrs])
    return jnp.pad(flat, (0, PACK_ROWS * 1024 - flat.shape[0])).reshape(PACK_ROWS, 1024)


def _unpack(packed, shapes):
    flat, out, off = packed.reshape(-1), [], 0
    for shp in shapes:
        n = math.prod(shp)
        out.append(flat[off:off + n].reshape(shp))
        off += n
    return out


def _pad_heads(w, heads, dim, axis):
    shp = w.shape[:axis] + (heads, dim) + w.shape[axis + 1:]
    pad = [(0, 0)] * len(shp)
    pad[axis + 1] = (0, LANE - dim)
    w = jnp.pad(w.reshape(shp), pad)
    return w.reshape(w.shape[:axis] + (heads * LANE,) + w.shape[axis + 2:])


def _unpad_heads(w, heads, dim, axis):
    shp = w.shape[:axis] + (heads, LANE) + w.shape[axis + 1:]
    w = lax.slice_in_dim(w.reshape(shp), 0, dim, axis=axis + 1)
    return w.reshape(w.shape[:axis] + (heads * dim,) + w.shape[axis + 2:])


def _w_in_layout(w_in):
    kr = jnp.pad(w_in[:, 384:416], ((0, 0), (ROPE_LO, LANE - ROPE_LO - MLA_ROPE)))
    sb = lambda lo: _pad_heads(w_in[:, lo:lo + 512], SB_HEADS, SB_HEAD_DIM, 1)
    return jnp.concatenate([w_in[:, 1952:2976], w_in[:, 2976:4000], sb(416), sb(928), sb(1440), w_in[:, 0:256],
                            w_in[:, 256:384], kr], axis=1)


def _w_in_unlayout(d):
    sb = lambda lo: _unpad_heads(d[:, lo:lo + 1024], SB_HEADS, SB_HEAD_DIM, 1)
    return jnp.concatenate([d[:, C_CQ:C_CQ + 256], d[:, C_CKV:C_CKV + 128], d[:, C_KR + ROPE_LO:C_KR + ROPE_LO + MLA_ROPE],
                            sb(C_SBQ), sb(C_SBK), sb(C_SBV), d[:, C_GA:C_GA + 1024], d[:, C_GB:C_GB + 1024]], axis=1)


def _w_ukv_layout(w):
    w3 = w.reshape(MLA_KV_RANK, MLA_HEADS, MLA_NOPE + MLA_V)
    pad = lambda part: jnp.pad(part, ((0, 0), (0, 0), (0, LANE - part.shape[2]))).reshape(MLA_KV_RANK, MLA_HEADS * LANE)
    return jnp.concatenate([pad(w3[:, :, :MLA_NOPE]), pad(w3[:, :, MLA_NOPE:])], axis=1)


def _w_ukv_unlayout(d):
    hw = MLA_HEADS * LANE
    kpart = d[:, :hw].reshape(MLA_KV_RANK, MLA_HEADS, LANE)[:, :, :MLA_NOPE]
    vpart = d[:, hw:].reshape(MLA_KV_RANK, MLA_HEADS, LANE)[:, :, :MLA_V]
    return jnp.concatenate([kpart, vpart], axis=2).reshape(MLA_KV_RANK, MLA_HEADS * (MLA_NOPE + MLA_V))


def _shard_of(full, d, axis):
    n = full.shape[axis] // N_CHIPS
    return lax.slice_in_dim(full, d * n, (d + 1) * n, axis=axis)


def _local_step(x, mem, pos, target, w, t_mla, t_sb):
    s = x.shape[0]
    win = _w_in_layout(w["w_in"])
    wuq = _pad_heads(w["w_uq"], MLA_HEADS, MLA_NOPE + MLA_ROPE, 1)
    wkv = _w_ukv_layout(w["w_ukv"])
    wa = _pad_heads(w["w_a_proj"], MLA_HEADS, MLA_V, 0)
    wb = _pad_heads(w["w_b_proj"], SB_HEADS, SB_HEAD_DIM, 0)
    inv_freq = ROPE_THETA ** (-jnp.arange(0, MLA_ROPE, 2, dtype=F32) / MLA_ROPE)
    freq_lane = jnp.pad(jnp.concatenate([inv_freq, inv_freq]), (ROPE_LO, LANE - ROPE_LO - MLA_ROPE)).reshape(1, LANE)
    add = lambda accs, ex: (accs[0] + ex[0],)

    tab = _rope_tables(pos.reshape(s, 1), freq_lane)
    h = _rms_fwd_call(x, w["g_mix"], "rms_mix")
    proj = _mm(h, [win], name="proj_in", tn=1408)
    cqn, ckvn, krope = _mla_prep_fwd(proj, tab, w["g_q_lat"], w["g_kv_lat"])
    qp = _mm(cqn, [wuq], name="q_up")
    kvp = _mm(ckvn, [wkv], name="kv_up")
    qa, ka, va = _mla_rope_fwd(qp, kvp, krope, tab)
    o_a, lse = _mla_fwd(qa, ka, va, t_mla)
    o_b = _sb_fwd(proj, t_sb)
    pa = _mm(o_a, [wa], name="proj_a")
    pb = _mm(o_b, [wb], name="proj_b")
    merged = _gate_fwd(proj, pa, pb, w["b_gate"])
    x1 = _mm(merged, [w["w_o"]], name="proj_o", extras=(x,), epilogue=add)
    hx = _rms_fwd_call(x1, w["g_x"], "rms_x")
    mn = _rms_fwd_call(mem, w["g_mem"], "rms_mem")
    xq = _mm(hx, [w["w_xq"]], name="xq", out_dtypes=(BF16,))
    xkv = _mm(mn, [w["w_xkv"]], name="xkv", out_dtypes=(BF16,))
    xo = _xattn_fwd(xq, xkv)
    x2 = _mm(xo, [w["w_xo"]], name="proj_xo", extras=(x1,), epilogue=add)
    hf = _rms_fwd_call(x2, w["g_ffn"], "rms_ffn")

    def swiglu(accs, ex):
        a, b = accs
        return a, b, a * _sigmoid(a) * b

    ga, gu, hmid = _mm(hf, [w["w_gate"], w["w_up"]], name="ffn_up", epilogue=swiglu, out_dtypes=(BF16, BF16, BF16),
                       tm=512, tn=1408)
    x3 = _mm(hmid, [w["w_down"]], name="ffn_down", extras=(x2,), epilogue=add)

    dx3, dg_final, sq = _loss_head(x3, target, w["g_final"].reshape(1, D_MODEL))
    g = {"g_final": dg_final.reshape(D_MODEL)}

    def swiglu_bwd(accs, ex):
        dh, a, b = accs[0], ex[0].astype(F32), ex[1].astype(F32)
        sg = _sigmoid(a)
        return dh * b * sg * (1.0 + a * (1.0 - sg)), dh * a * sg

    da, db = _mm(dx3, [w["w_down"]], name="ffn_down_dx", tb=True, extras=(ga, gu), epilogue=swiglu_bwd,
                 out_dtypes=(BF16, BF16), tm=512, tn=1408)
    g["w_down"] = _mm(hmid, [dx3], name="ffn_down_dw", ta=True, tm=1408)
    g["w_gate"] = _mm(hf, [da], name="ffn_gate_dw", ta=True, tn=1408)
    g["w_up"] = _mm(hf, [db], name="ffn_up_dw", ta=True, tn=1408)
    dhf = _mm(da, [w["w_gate"]], name="ffn_gate_dx", tb=True)
    dhf = _mm(db, [w["w_up"]], name="ffn_up_dx", tb=True, extras=(dhf,), epilogue=add)
    dx2, g["g_ffn"] = _rms_bwd_call(x2, w["g_ffn"], dhf, dx3, "rms_ffn_bwd")

    dxo = _mm(dx2, [w["w_xo"]], name="proj_xo_dx", tb=True, out_dtypes=(BF16,))
    g["w_xo"] = _mm(xo, [dx2], name="proj_xo_dw", ta=True)
    dxq, dxkv = _xattn_bwd(xq, xkv, dxo)
    dhx = _mm(dxq, [w["w_xq"]], name="xq_dx", tb=True)
    g["w_xq"] = _mm(hx, [dxq], name="xq_dw", ta=True)
    dmn = _mm(dxkv, [w["w_xkv"]], name="xkv_dx", tb=True)
    g["w_xkv"] = _mm(mn, [dxkv], name="xkv_dw", ta=True)
    dx1, g["g_x"] = _rms_bwd_call(x1, w["g_x"], dhx, dx2, "rms_x_bwd")
    _, g["g_mem"] = _rms_bwd_call(mem, w["g_mem"], dmn, None, "rms_mem_bwd")

    dmerged = _mm(dx1, [w["w_o"]], name="proj_o_dx", tb=True)
    g["w_o"] = _mm(merged, [dx1], name="proj_o_dw", ta=True)
    dpa, dpb, dga, dgb, g["b_gate"] = _gate_bwd(proj, pa, pb, w["b_gate"], dmerged)
    do_a = _mm(dpa, [wa], name="proj_a_dx", tb=True, out_dtypes=(BF16,))
    do_b = _mm(dpb, [wb], name="proj_b_dx", tb=True, out_dtypes=(BF16,))
    g["w_a_proj"] = _unpad_heads(_mm(o_a, [dpa], name="proj_a_dw", ta=True), MLA_HEADS, MLA_V, 0)
    g["w_b_proj"] = _unpad_heads(_mm(o_b, [dpb], name="proj_b_dw", ta=True), SB_HEADS, SB_HEAD_DIM, 0)

    dsq, dsk, dsv = _sb_bwd(proj, o_b, do_b, t_sb)
    dqa, dka, dva = _mla_bwd(qa, ka, va, o_a, do_a, lse, t_mla)
    dqp, dkvp, dkr = _mla_rope_bwd(dqa, dka, dva, tab)
    g["w_uq"] = _unpad_heads(_mm(cqn, [dqp], name="q_up_dw", ta=True), MLA_HEADS, MLA_NOPE + MLA_ROPE, 1)
    g["w_ukv"] = _w_ukv_unlayout(_mm(ckvn, [dkvp], name="kv_up_dw", ta=True))
    dcqn = _mm(dqp, [wuq], name="q_up_dx", tb=True)
    dckvn = _mm(dkvp, [wkv], name="kv_up_dx", tb=True)
    dcq, dckv, g["g_q_lat"], g["g_kv_lat"] = _mla_prep_bwd(proj, w["g_q_lat"], w["g_kv_lat"], dcqn, dckvn)

    dproj = jnp.concatenate([dga, dgb, dsq.astype(BF16), dsk.astype(BF16), dsv.astype(BF16), dcq, dckv, dkr], axis=1)
    g["w_in"] = _w_in_unlayout(_mm(h, [dproj], name="proj_in_dw", ta=True, tn=1408))
    dh = _mm(dproj, [win], name="proj_in_dx", tb=True)
    grad_x, g["g_mix"] = _rms_bwd_call(x, w["g_mix"], dh, dx1, "rms_mix_bwd")
    return sq, grad_x, g


def _small_pack(d):
    row5 = jnp.concatenate([d["g_q_lat"].reshape(-1), d["g_kv_lat"].reshape(-1), jnp.zeros((640,), F32)])
    rows = [d[n].reshape(-1) for n in ("g_mix", "g_x", "g_mem", "g_ffn", "g_final")] + [row5]
    return rows


def _small_unpack(p, like):
    out = {n: p[i].reshape(like[n].shape) for i, n in enumerate(("g_mix", "g_x", "g_mem", "g_ffn", "g_final"))}
    out["g_q_lat"] = p[5, 0:256].reshape(like["g_q_lat"].shape)
    out["g_kv_lat"] = p[5, 256:384].reshape(like["g_kv_lat"].shape)
    return out


def kernel(x, mem, positions, g_mix, w_in, b_gate, g_q_lat, w_uq, g_kv_lat, w_ukv, w_a_proj, w_b_proj, w_o, g_x, g_mem, w_xq, w_xkv, w_xo, g_ffn, w_gate, w_up, w_down, g_final, loss_target, m_g_mix, m_w_in, m_b_gate, m_g_q_lat, m_w_uq, m_g_kv_lat, m_w_ukv, m_w_a_proj, m_w_b_proj, m_w_o, m_g_x, m_g_mem, m_w_xq, m_w_xkv, m_w_xo, m_g_ffn, m_w_gate, m_w_up, m_w_down, m_g_final, v_g_mix, v_w_in, v_b_gate, v_g_q_lat, v_w_uq, v_g_kv_lat, v_w_ukv, v_w_a_proj, v_w_b_proj, v_w_o, v_g_x, v_g_mem, v_w_xq, v_w_xkv, v_w_xo, v_g_ffn, v_w_gate, v_w_up, v_w_down, v_g_final):
    given = dict(locals())
    names = [n for n, _, _ in SHARDED] + list(SMALL)
    wts = {n: given[n] for n in names}
    mom = {n: given["m_" + n] for n in names}
    var = {n: given["v_" + n] for n in names}
    shard2d = {n: shp for n, shp, _ in SHARDED}
    axis = {n: ax for n, _, ax in SHARDED}
    c = lax.axis_index("c")

    def payload(n):
        a = wts[n].reshape(shard2d[n])
        return lax.bitcast_convert_type(a, BF16) if n == "b_gate" else a.astype(BF16)

    gathered = _all_gather_weights(_pack([payload(n) for n, _, _ in SHARDED], BF16))
    pay_shapes = [shp + (2,) if n == "b_gate" else shp for n, shp, _ in SHARDED]
    per_chip = [_unpack(gathered[d], pay_shapes) for d in range(N_CHIPS)]
    full = {}
    for k, (n, _, ax) in enumerate(SHARDED):
        parts = [per_chip[d][k] for d in range(N_CHIPS)]
        if n == "b_gate":
            parts = [lax.bitcast_convert_type(p, F32) for p in parts]
        full[n] = jnp.concatenate(parts, axis=ax)
    for n in SMALL:
        full[n] = wts[n].reshape(1, -1) if n != "g_final" else wts[n]

    sq, grad_x, grads = _local_step(x[0], mem[0], positions[0], loss_target[0], full, t_mla=512, t_sb=256)

    by_chip = jnp.stack([_pack([_shard_of(grads[n], d, axis[n]) for n, _, _ in SHARDED], F32) for d in range(N_CHIPS)])
    halves = by_chip.reshape(N_CHIPS, 2, PACK_ROWS // 2, 1024)
    mine = lax.dynamic_index_in_dim(halves, c, axis=1, keepdims=False)
    theirs = lax.dynamic_index_in_dim(halves, 1 - c, axis=1, keepdims=False)
    part_bf, part = _pair_sum(mine, _pair_exchange(theirs, "pair_exchange_grads"))
    me = 2 * lax.axis_index("x") + lax.axis_index("y")
    own = lax.dynamic_index_in_dim(part, me, axis=0, keepdims=False)
    my_half = _chip_sum(own, _chip_scatter(part_bf))
    other_half = _pair_exchange(my_half, "pair_exchange_halves")
    g_shard = jnp.where(c == 0, jnp.concatenate([my_half, other_half]), jnp.concatenate([other_half, my_half]))

    small_rows = _small_pack({n: grads[n] for n in SMALL}) + [sq.reshape(-1), jnp.zeros((1024,), F32)]
    small = _all_reduce_small(jnp.stack(small_rows))
    loss = (0.5 / D_MODEL) * jnp.sum(small[6])
    g_small = _small_unpack(small, wts)

    shapes = [shp for _, shp, _ in SHARDED]
    pk = lambda d: _pack([d[n].reshape(shard2d[n]) for n, _, _ in SHARDED], F32)
    delta_p, m_p, v_p = _adamw(pk(wts), g_shard, pk(mom), pk(var), "adamw_shard")
    sp = lambda d: jnp.stack(_small_pack(d) + [jnp.zeros((1024,), F32)] * 2)
    delta_s, m_s, v_s = _adamw(sp(wts), small.at[6].set(0.0), sp(mom), sp(var), "adamw_small")

    def spread(packed, packed_small):
        out = {n: a.reshape(wts[n].shape) for (n, _, _), a in zip(SHARDED, _unpack(packed, shapes))}
        out.update(_small_unpack(packed_small, wts))
        return out

    order = ["g_mix", "w_in", "b_gate", "g_q_lat", "w_uq", "g_kv_lat", "w_ukv", "w_a_proj", "w_b_proj", "w_o", "g_x",
             "g_mem", "w_xq", "w_xkv", "w_xo", "g_ffn", "w_gate", "w_up", "w_down", "g_final"]
    groups = [spread(g_shard, small), spread(delta_p, delta_s), spread(m_p, m_s), spread(v_p, v_s)]
    return (loss, grad_x[None], *[grp[n] for grp in groups for n in order])
```

```python
import functools
import math

import jax
import jax.numpy as jnp
from jax import lax
from jax.experimental import pallas as pl
from jax.experimental.pallas import tpu as pltpu

F32 = jnp.float32
BF16 = jnp.bfloat16
MESH = pl.DeviceIdType.MESH

D_MODEL = 1024
MLA_HEADS = 8
MLA_Q_RANK = 256
MLA_KV_RANK = 128
MLA_NOPE = 64
MLA_ROPE = 32
MLA_V = 64
ROPE_THETA = 10000.0
SB_HEADS = 8
SB_HEAD_DIM = 64
X_HEADS = 4
X_HEAD_DIM = 128
D_FF = 2816
EPS = 1e-6
ADAM_LR = 0.001
ADAM_B1 = 0.9
ADAM_B2 = 0.999
ADAM_EPS = 1e-08
ADAM_WD = 0.01
ADAM_STEP = 10

LANE = 128
N_CHIPS = 4
VMEM_BYTES = 64 * 1024 * 1024

C_GA, C_GB, C_SBQ, C_SBK, C_SBV, C_CQ, C_CKV, C_KR = 0, 1024, 2048, 3072, 4096, 5120, 5376, 5504
D_INP = 5632
ROPE_LO = MLA_NOPE
HALF = MLA_ROPE // 2

SB_ZERO_LOG = -104.0

MATS = (
    ("w_in", (1024, 1000), 1), ("w_uq", (256, 192), 1), ("w_ukv", (128, 256), 1), ("w_a_proj", (512, 256), 1),
    ("w_b_proj", (512, 256), 1), ("w_o", (256, 1024), 0), ("w_xq", (256, 512), 0), ("w_xkv", (256, 1024), 0),
    ("w_xo", (512, 256), 1), ("w_gate", (1024, 704), 1), ("w_up", (1024, 704), 1), ("w_down", (704, 1024), 0),
)
B_GATE_SHARD = (2, 256)
SMALL = ("g_mix", "g_x", "g_mem", "g_ffn", "g_final", "g_q_lat", "g_kv_lat")
SMALL_ROWS = 16


def _vmem_limit(block_bytes, temp_bytes):
    est = 2 * block_bytes + temp_bytes + (4 << 20)
    return int(min(max(est, 16 << 20), VMEM_BYTES - (6 << 20)))


def _nbytes(shape, dtype):
    return math.prod(shape) * jnp.dtype(dtype).itemsize


def _row_tile(rows, cap):
    if rows <= cap:
        return rows
    return max(t for t in range(8, cap + 1, 8) if rows % t == 0)


def _tile(n, cap):
    if n <= cap:
        return n
    best = None
    for t in range(LANE, cap + 1, LANE):
        if n % t == 0:
            best = t
    assert best is not None, (n, cap)
    return best


def _mm(a, bs, *, name, ta=False, tb=False, extras=(), epilogue=None, out_dtypes=(F32,), tm=1024, tn=1024, tk=512):
    bs = tuple(bs)
    m, k = (a.shape[1], a.shape[0]) if ta else a.shape
    n = bs[0].shape[0] if tb else bs[0].shape[1]
    tm, tn, tk = _tile(m, tm), _tile(n, tn), _tile(k, tk)
    assert m % tm == 0 and n % tn == 0 and k % tk == 0
    nk = k // tk
    nb, ne, no = len(bs), len(extras), len(out_dtypes)
    dims = (((0,) if ta else (1,)), ((1,) if tb else (0,))), ((), ())
    if epilogue is None:
        epilogue = lambda accs, ex: (accs[0],)

    def body(*refs):
        a_ref, b_refs, e_refs = refs[0], refs[1:1 + nb], refs[1 + nb:1 + nb + ne]
        o_refs, acc_refs = refs[1 + nb + ne:1 + nb + ne + no], refs[1 + nb + ne + no:]
        kk = pl.program_id(2)

        @pl.when(kk == 0)
        def _():
            for acc in acc_refs:
                acc[...] = jnp.zeros_like(acc)

        av = a_ref[...].astype(BF16)
        for b_ref, acc in zip(b_refs, acc_refs):
            acc[...] += lax.dot_general(av, b_ref[...].astype(BF16), dims, preferred_element_type=F32)

        @pl.when(kk == nk - 1)
        def _():
            outs = epilogue([acc[...] for acc in acc_refs], [e[...] for e in e_refs])
            for o_ref, v in zip(o_refs, outs):
                o_ref[...] = v.astype(o_ref.dtype)

    a_spec = pl.BlockSpec((tk, tm), lambda i, j, kk: (kk, i)) if ta else pl.BlockSpec((tm, tk), lambda i, j, kk: (i, kk))
    b_spec = pl.BlockSpec((tn, tk), lambda i, j, kk: (j, kk)) if tb else pl.BlockSpec((tk, tn), lambda i, j, kk: (kk, j))
    mn_spec = pl.BlockSpec((tm, tn), lambda i, j, kk: (i, j))
    blocks = (_nbytes((tm, tk), a.dtype) + sum(_nbytes((tk, tn), b.dtype) for b in bs)
              + sum(_nbytes((tm, tn), e.dtype) for e in extras) + sum(_nbytes((tm, tn), d) for d in out_dtypes))
    temps = (nb + 4) * _nbytes((tm, tn), F32)
    outs = pl.pallas_call(
        body, name=name, grid=(m // tm, n // tn, nk),
        in_specs=[a_spec] + [b_spec] * nb + [mn_spec] * ne,
        out_specs=[mn_spec] * no,
        out_shape=[jax.ShapeDtypeStruct((m, n), d) for d in out_dtypes],
        scratch_shapes=[pltpu.VMEM((tm, tn), F32) for _ in range(nb)],
        compiler_params=pltpu.CompilerParams(
            dimension_semantics=("parallel", "parallel", "arbitrary"),
            vmem_limit_bytes=_vmem_limit(blocks, temps)),
    )(a, *bs, *extras)
    return outs[0] if no == 1 else outs


def _rowwise(body, *, name, rows, tr, row_ins, full_ins=(), row_outs=(), acc_outs=()):
    tr = min(tr, rows)
    assert rows % tr == 0
    n_ri, n_fi, n_ro = len(row_ins), len(full_ins), len(row_outs)

    def kern(*refs):
        body(pl.program_id(0), refs[:n_ri], refs[n_ri:n_ri + n_fi], refs[n_ri + n_fi:n_ri + n_fi + n_ro],
             refs[n_ri + n_fi + n_ro:])

    in_specs = [pl.BlockSpec((tr, w), functools.partial(lambda i, c: (i, c), c=ci)) for _, w, ci in row_ins]
    in_specs += [pl.BlockSpec(f.shape, lambda i: (0, 0)) for f in full_ins]
    out_specs = [pl.BlockSpec((tr, w), lambda i: (i, 0)) for w, _ in row_outs]
    out_specs += [pl.BlockSpec(s, lambda i: (0, 0)) for s, _ in acc_outs]
    out_shape = [jax.ShapeDtypeStruct((rows, w), d) for w, d in row_outs]
    out_shape += [jax.ShapeDtypeStruct(s, d) for s, d in acc_outs]
    blocks = (sum(_nbytes((tr, w), a.dtype) for a, w, _ in row_ins) + sum(_nbytes(f.shape, f.dtype) for f in full_ins)
              + sum(_nbytes((tr, w), d) for w, d in row_outs) + sum(_nbytes(s, d) for s, d in acc_outs))
    widest = max([w for _, w, _ in row_ins] + [w for w, _ in row_outs])
    outs = pl.pallas_call(
        kern, name=name, grid=(rows // tr,), in_specs=in_specs, out_specs=out_specs, out_shape=out_shape,
        compiler_params=pltpu.CompilerParams(
            dimension_semantics=("arbitrary",) if acc_outs else ("parallel",),
            vmem_limit_bytes=_vmem_limit(blocks, 8 * _nbytes((tr, widest), F32))),
    )(*[a for a, _, _ in row_ins], *full_ins)
    return outs


def _rms(x, g):
    r = lax.rsqrt(jnp.mean(x * x, axis=-1, keepdims=True) + EPS)
    return x * r * g


def _rms_bwd(x, g, dy):
    r = lax.rsqrt(jnp.mean(x * x, axis=-1, keepdims=True) + EPS)
    xh = x * r
    dxh = dy * g
    dx = r * (dxh - xh * jnp.mean(dxh * xh, axis=-1, keepdims=True))
    return dx, jnp.sum(dy * xh, axis=0, keepdims=True)


def _sigmoid(x):
    return 1.0 / (1.0 + jnp.exp(-x))


def _acc_init(i, refs):
    @pl.when(i == 0)
    def _():
        for r in refs:
            r[...] = jnp.zeros_like(r)


def _rms_fwd_call(x, g, name):
    rows, c = x.shape

    def body(i, ins, fulls, outs, accs):
        outs[0][...] = _rms(ins[0][...], fulls[0][...]).astype(BF16)

    return _rowwise(body, name=name, rows=rows, tr=512, row_ins=[(x, c, 0)], full_ins=[g], row_outs=[(c, BF16)])[0]


def _rms_bwd_call(x, g, dy, res, name):
    rows, c = x.shape
    row_ins = [(x, c, 0), (dy, c, 0)] + ([(res, c, 0)] if res is not None else [])

    def body(i, ins, fulls, outs, accs):
        _acc_init(i, accs)
        dx, dg = _rms_bwd(ins[0][...], fulls[0][...], ins[1][...].astype(F32))
        if res is not None:
            dx = dx + ins[2][...]
        outs[0][...] = dx
        accs[0][...] += dg

    return _rowwise(body, name=name, rows=rows, tr=512, row_ins=row_ins, full_ins=[g], row_outs=[(c, F32)],
                    acc_outs=[((1, c), F32)])


def _rope_tables(pos_col, freq_lane):
    rows = pos_col.shape[0]

    def body(i, ins, fulls, outs, accs):
        ang = ins[0][...].astype(F32) * fulls[0][...]
        lane = lax.broadcasted_iota(jnp.int32, ang.shape, 1)
        cos, sin = jnp.cos(ang), jnp.sin(ang)
        first = (lane >= ROPE_LO) & (lane < ROPE_LO + HALF)
        second = (lane >= ROPE_LO + HALF) & (lane < ROPE_LO + MLA_ROPE)
        outs[0][:, 0:LANE] = jnp.where(first | second, cos, 1.0)
        outs[0][:, LANE:2 * LANE] = jnp.where(first, -sin, 0.0)
        outs[0][:, 2 * LANE:3 * LANE] = jnp.where(second, sin, 0.0)

    return _rowwise(body, name="rope_tables", rows=rows, tr=1024, row_ins=[(pos_col, 1, 0)], full_ins=[freq_lane],
                    row_outs=[(3 * LANE, F32)])[0]


def _rope(x, tab):
    return (x * tab[:, 0:LANE] + pltpu.roll(x, LANE - HALF, 1) * tab[:, LANE:2 * LANE]
            + pltpu.roll(x, HALF, 1) * tab[:, 2 * LANE:3 * LANE])


def _rope_t(dy, tab):
    return (dy * tab[:, 0:LANE] + pltpu.roll(dy * tab[:, LANE:2 * LANE], HALF, 1)
            + pltpu.roll(dy * tab[:, 2 * LANE:3 * LANE], LANE - HALF, 1))


def _mla_prep_fwd(proj, tab, g_q, g_kv):
    rows = proj.shape[0]

    def body(i, ins, fulls, outs, accs):
        outs[0][...] = _rms(ins[0][...], fulls[0][...]).astype(BF16)
        outs[1][...] = _rms(ins[1][...], fulls[1][...]).astype(BF16)
        outs[2][...] = _rope(ins[2][...], ins[3][...])

    return _rowwise(body, name="mla_prep_fwd", rows=rows, tr=512,
                    row_ins=[(proj, MLA_Q_RANK, C_CQ // MLA_Q_RANK), (proj, LANE, C_CKV // LANE),
                             (proj, LANE, C_KR // LANE), (tab, 3 * LANE, 0)],
                    full_ins=[g_q, g_kv], row_outs=[(MLA_Q_RANK, BF16), (MLA_KV_RANK, BF16), (LANE, F32)])


def _mla_prep_bwd(proj, g_q, g_kv, dcqn, dckvn):
    rows = proj.shape[0]

    def body(i, ins, fulls, outs, accs):
        _acc_init(i, accs)
        dcq, dgq = _rms_bwd(ins[0][...], fulls[0][...], ins[2][...])
        dckv, dgkv = _rms_bwd(ins[1][...], fulls[1][...], ins[3][...])
        outs[0][...] = dcq.astype(BF16)
        outs[1][...] = dckv.astype(BF16)
        accs[0][...] += dgq
        accs[1][...] += dgkv

    return _rowwise(body, name="mla_prep_bwd", rows=rows, tr=512,
                    row_ins=[(proj, MLA_Q_RANK, C_CQ // MLA_Q_RANK), (proj, LANE, C_CKV // LANE),
                             (dcqn, MLA_Q_RANK, 0), (dckvn, MLA_KV_RANK, 0)],
                    full_ins=[g_q, g_kv], row_outs=[(MLA_Q_RANK, BF16), (MLA_KV_RANK, BF16)],
                    acc_outs=[((1, MLA_Q_RANK), F32), ((1, MLA_KV_RANK), F32)])


def _mla_rope_fwd(qp, kvp, krope, tab):
    rows = qp.shape[0]
    hw = MLA_HEADS * LANE

    def body(i, ins, fulls, outs, accs):
        t = ins[3][...]
        kr = ins[2][...]
        for h in range(MLA_HEADS):
            sl = slice(h * LANE, (h + 1) * LANE)
            outs[0][:, sl] = _rope(ins[0][:, sl], t).astype(BF16)
            outs[1][:, sl] = (ins[1][:, sl] + kr).astype(BF16)
        outs[2][...] = ins[1][:, hw:2 * hw].astype(BF16)

    return _rowwise(body, name="mla_rope_fwd", rows=rows, tr=512,
                    row_ins=[(qp, hw, 0), (kvp, 2 * hw, 0), (krope, LANE, 0), (tab, 3 * LANE, 0)],
                    row_outs=[(hw, BF16), (hw, BF16), (hw, BF16)])


def _mla_rope_bwd(dq, dk, dv, tab):
    rows = dq.shape[0]
    hw = MLA_HEADS * LANE

    def body(i, ins, fulls, outs, accs):
        t = ins[3][...]
        dkr = jnp.zeros((ins[0].shape[0], LANE), F32)
        for h in range(MLA_HEADS):
            sl = slice(h * LANE, (h + 1) * LANE)
            outs[0][:, sl] = _rope_t(ins[0][:, sl], t).astype(BF16)
            dkr = dkr + ins[1][:, sl]
        outs[1][:, 0:hw] = ins[1][...].astype(BF16)
        outs[1][:, hw:2 * hw] = ins[2][...].astype(BF16)
        lane = lax.broadcasted_iota(jnp.int32, dkr.shape, 1)
        dkr = jnp.where((lane >= ROPE_LO) & (lane < ROPE_LO + MLA_ROPE), dkr, 0.0)
        outs[2][...] = _rope_t(dkr, t).astype(BF16)

    return _rowwise(body, name="mla_rope_bwd", rows=rows, tr=512,
                    row_ins=[(dq, hw, 0), (dk, hw, 0), (dv, hw, 0), (tab, 3 * LANE, 0)],
                    row_outs=[(hw, BF16), (2 * hw, BF16), (LANE, BF16)])


def _dot_nt(a, b):
    return lax.dot_general(a, b, (((1,), (1,)), ((), ())), preferred_element_type=F32)


def _dot_tn(a, b):
    return lax.dot_general(a, b, (((0,), (0,)), ((), ())), preferred_element_type=F32)


def _dot(a, b):
    return jnp.dot(a, b, preferred_element_type=F32)


def _attn_params(s, t, n_res_f32, n_res_bf16):
    blocks = n_res_f32 * _nbytes((s, LANE), F32) + n_res_bf16 * _nbytes((s, LANE), BF16) + 6 * _nbytes((t, LANE), F32)
    return pltpu.CompilerParams(dimension_semantics=("parallel", "arbitrary"),
                                vmem_limit_bytes=_vmem_limit(blocks, 12 * _nbytes((t, t), F32)))


def _mla_fwd(q, k, v, t):
    s, hw = q.shape
    heads, nq = hw // LANE, s // t
    scale = 1.0 / math.sqrt(MLA_NOPE + MLA_ROPE)

    def body(q_ref, k_ref, v_ref, o_ref, l_ref):
        i = pl.program_id(1)
        qv = q_ref[...]

        def step(j, carry, masked):
            m, l, acc = carry
            sl = pl.ds(pl.multiple_of(j * t, t), t)
            sc = _dot_nt(qv, k_ref[sl, :]) * scale
            if masked:
                row = lax.broadcasted_iota(jnp.int32, (t, t), 0)
                col = lax.broadcasted_iota(jnp.int32, (t, t), 1)
                sc = jnp.where(col <= row, sc, -1e30)
            m_new = jnp.maximum(m, jnp.max(sc, axis=1, keepdims=True))
            p = jnp.exp(sc - m_new)
            alpha = jnp.exp(m - m_new)
            l = alpha * l + jnp.sum(p, axis=1, keepdims=True)
            acc = alpha * acc + _dot(p.astype(BF16), v_ref[sl, :])
            return m_new, l, acc

        init = (jnp.full((t, 1), -1e30, F32), jnp.zeros((t, 1), F32), jnp.zeros((t, LANE), F32))
        carry = lax.fori_loop(0, i, lambda j, c: step(j, c, False), init)
        m, l, acc = step(i, carry, True)
        o_ref[...] = (acc / l).astype(o_ref.dtype)
        l_ref[0] = m + jnp.log(l)

    blk = pl.BlockSpec((t, LANE), lambda h, i: (i, h))
    res = pl.BlockSpec((s, LANE), lambda h, i: (0, h))
    return pl.pallas_call(
        body, name="mla_fwd", grid=(heads, nq), in_specs=[blk, res, res],
        out_specs=[blk, pl.BlockSpec((1, t, 1), lambda h, i: (h, i, 0))],
        out_shape=[jax.ShapeDtypeStruct((s, hw), BF16), jax.ShapeDtypeStruct((heads, s, 1), F32)],
        compiler_params=_attn_params(s, t, 0, 2),
    )(q, k, v)


def _mla_bwd(q, k, v, o, do, lse, t):
    s, hw = q.shape
    heads, nq = hw // LANE, s // t
    scale = 1.0 / math.sqrt(MLA_NOPE + MLA_ROPE)

    def body(q_ref, k_ref, v_ref, o_ref, do_ref, l_ref, dq_ref, dk_ref, dv_ref):
        i = pl.program_id(1)

        @pl.when(i == 0)
        def _():
            dk_ref[...] = jnp.zeros_like(dk_ref)
            dv_ref[...] = jnp.zeros_like(dv_ref)

        qv, dov, lv = q_ref[...], do_ref[...], l_ref[0]
        dlt = jnp.sum(dov.astype(F32) * o_ref[...].astype(F32), axis=1, keepdims=True)

        def step(j, dq, masked):
            sl = pl.ds(pl.multiple_of(j * t, t), t)
            kv, vv = k_ref[sl, :], v_ref[sl, :]
            p = jnp.exp(_dot_nt(qv, kv) * scale - lv)
            if masked:
                row = lax.broadcasted_iota(jnp.int32, (t, t), 0)
                col = lax.broadcasted_iota(jnp.int32, (t, t), 1)
                p = jnp.where(col <= row, p, 0.0)
            ds = (p * (_dot_nt(dov, vv) - dlt) * scale).astype(BF16)
            dk_ref[sl, :] += _dot_tn(ds, qv)
            dv_ref[sl, :] += _dot_tn(p.astype(BF16), dov)
            return dq + _dot(ds, kv)

        dq = lax.fori_loop(0, i, lambda j, c: step(j, c, False), jnp.zeros((t, LANE), F32))
        dq_ref[...] = step(i, dq, True)

    blk = pl.BlockSpec((t, LANE), lambda h, i: (i, h))
    res = pl.BlockSpec((s, LANE), lambda h, i: (0, h))
    full = jax.ShapeDtypeStruct((s, hw), F32)
    return pl.pallas_call(
        body, name="mla_bwd", grid=(heads, nq),
        in_specs=[blk, res, res, blk, blk, pl.BlockSpec((1, t, 1), lambda h, i: (h, i, 0))],
        out_specs=[blk, res, res], out_shape=[full, full, full],
        compiler_params=_attn_params(s, t, 2, 2),
    )(q, k, v, o, do, lse)


def _sb_logits(qv, kv, scale, masked, t, upper):
    z = _dot_nt(qv, kv) * scale
    e = jnp.exp(-jnp.abs(z))
    l1p = jnp.log1p(e)
    lb = jnp.minimum(z, 0.0) - l1p
    lo = -jnp.maximum(z, 0.0) - l1p
    keep = None
    if masked:
        row = lax.broadcasted_iota(jnp.int32, (t, t), 0)
        col = lax.broadcasted_iota(jnp.int32, (t, t), 1)
        keep = col < row
        lo = jnp.where(keep, lo, 0.0)
    hi = lo.astype(BF16)
    rem = (lo - hi.astype(F32)).astype(BF16)
    suf = _dot(hi, upper) + _dot(rem, upper)
    return z, e, lb, lo, suf, keep


def _tri(t, inclusive):
    row = lax.broadcasted_iota(jnp.int32, (t, t), 0)
    col = lax.broadcasted_iota(jnp.int32, (t, t), 1)
    return jnp.where((row >= col) if inclusive else (row > col), 1.0, 0.0).astype(BF16)


def _sb_fwd(proj, t):
    s = proj.shape[0]
    heads, nq = SB_HEADS, s // t
    scale = 1.0 / math.sqrt(SB_HEAD_DIM)

    def body(q_ref, k_ref, v_ref, o_ref):
        i = pl.program_id(1)
        qv = q_ref[...].astype(BF16)
        upper = _tri(t, False)

        def step(jj, carry, masked):
            c, acc = carry
            sl = pl.ds(pl.multiple_of((i - jj) * t, t), t)
            _, _, lb, lo, suf, keep = _sb_logits(qv, k_ref[sl, :].astype(BF16), scale, masked, t, upper)
            a = jnp.exp(lb + suf + c)
            if masked:
                a = jnp.where(keep, a, 0.0)
            acc = acc + _dot(a.astype(BF16), v_ref[sl, :].astype(BF16))
            return c + jnp.sum(lo, axis=1, keepdims=True), acc

        carry = step(0, (jnp.zeros((t, 1), F32), jnp.zeros((t, LANE), F32)), True)
        def live(st):
            return (st[0] <= i) & (jnp.max(st[1]) >= SB_ZERO_LOG)

        def more(st):
            return (st[0] + 1,) + step(st[0], st[1:], False)

        o_ref[...] = lax.while_loop(live, more, (jnp.int32(1),) + carry)[2]

    return pl.pallas_call(
        body, name="sb_fwd", grid=(heads, nq),
        in_specs=[pl.BlockSpec((t, LANE), lambda h, i: (i, C_SBQ // LANE + h)),
                  pl.BlockSpec((s, LANE), lambda h, i: (0, C_SBK // LANE + h)),
                  pl.BlockSpec((s, LANE), lambda h, i: (0, C_SBV // LANE + h))],
        out_specs=pl.BlockSpec((t, LANE), lambda h, i: (i, h)),
        out_shape=jax.ShapeDtypeStruct((s, heads * LANE), F32),
        compiler_params=_attn_params(s, t, 2, 0),
    )(proj, proj, proj)


def _sb_bwd(proj, o, do, t):
    s = proj.shape[0]
    heads, nq = SB_HEADS, s // t
    scale = 1.0 / math.sqrt(SB_HEAD_DIM)

    def body(q_ref, k_ref, v_ref, o_ref, do_ref, dq_ref, dk_ref, dv_ref):
        i = pl.program_id(1)

        @pl.when(i == 0)
        def _():
            dk_ref[...] = jnp.zeros_like(dk_ref)
            dv_ref[...] = jnp.zeros_like(dv_ref)

        qv, dov = q_ref[...].astype(BF16), do_ref[...]
        total = jnp.sum(dov.astype(F32) * o_ref[...], axis=1, keepdims=True)
        upper, upper_incl = _tri(t, False), _tri(t, True)

        def step(jj, carry, masked):
            c, g, dq = carry
            sl = pl.ds(pl.multiple_of((i - jj) * t, t), t)
            kv, vv = k_ref[sl, :].astype(BF16), v_ref[sl, :].astype(BF16)
            z, e, lb, lo, suf, keep = _sb_logits(qv, kv, scale, masked, t, upper)
            tail = suf + c
            a = jnp.exp(lb + tail)
            if masked:
                a = jnp.where(keep, a, 0.0)
            ab = a.astype(BF16)
            gr = ab.astype(F32) * _dot_nt(dov, vv)
            ghi = gr.astype(BF16)
            grem = (gr - ghi.astype(F32)).astype(BF16)
            before = total - g - (_dot(ghi, upper_incl) + _dot(grem, upper_incl))
            before = jnp.where(tail < SB_ZERO_LOG, 0.0, before)
            r = 1.0 / (1.0 + e)
            pos = z >= 0.0
            dz = r * (gr * jnp.where(pos, e, 1.0) - before * jnp.where(pos, 1.0, e))
            if masked:
                dz = jnp.where(keep, dz, 0.0)
            dzb = (dz * scale).astype(BF16)
            dk_ref[sl, :] += _dot_tn(dzb, qv)
            dv_ref[sl, :] += _dot_tn(ab, dov)
            return (c + jnp.sum(lo, axis=1, keepdims=True), g + jnp.sum(gr, axis=1, keepdims=True),
                    dq + _dot(dzb, kv))

        zero = jnp.zeros((t, 1), F32)
        carry = step(0, (zero, zero, jnp.zeros((t, LANE), F32)), True)
        def live(st):
            return (st[0] <= i) & (jnp.max(st[1]) >= SB_ZERO_LOG)

        def more(st):
            return (st[0] + 1,) + step(st[0], st[1:], False)

        dq_ref[...] = lax.while_loop(live, more, (jnp.int32(1),) + carry)[3]

    blk = pl.BlockSpec((t, LANE), lambda h, i: (i, h))
    res = pl.BlockSpec((s, LANE), lambda h, i: (0, h))
    full = jax.ShapeDtypeStruct((s, heads * LANE), F32)
    return pl.pallas_call(
        body, name="sb_bwd", grid=(heads, nq),
        in_specs=[pl.BlockSpec((t, LANE), lambda h, i: (i, C_SBQ // LANE + h)),
                  pl.BlockSpec((s, LANE), lambda h, i: (0, C_SBK // LANE + h)),
                  pl.BlockSpec((s, LANE), lambda h, i: (0, C_SBV // LANE + h)), blk, blk],
        out_specs=[blk, res, res], out_shape=[full, full, full],
        compiler_params=_attn_params(s, t, 4, 0),
    )(proj, proj, proj, o, do)


def _xattn_probs(qh, kh):
    sc = _dot_nt(qh, kh) * (1.0 / math.sqrt(X_HEAD_DIM))
    p = jnp.exp(sc - jnp.max(sc, axis=1, keepdims=True))
    return p / jnp.sum(p, axis=1, keepdims=True)


def _xattn_fwd(xq, xkv):
    rows = xq.shape[0]
    w = X_HEADS * X_HEAD_DIM

    def body(i, ins, fulls, outs, accs):
        for h in range(X_HEADS):
            sl = slice(h * LANE, (h + 1) * LANE)
            p = _xattn_probs(ins[0][:, sl], fulls[0][:, sl])
            outs[0][:, sl] = _dot(p.astype(BF16), fulls[0][:, w + h * LANE:w + (h + 1) * LANE]).astype(BF16)

    return _rowwise(body, name="xattn_fwd", rows=rows, tr=512, row_ins=[(xq, w, 0)], full_ins=[xkv],
                    row_outs=[(w, BF16)])[0]


def _xattn_bwd(xq, xkv, dxo):
    rows = xq.shape[0]
    w = X_HEADS * X_HEAD_DIM

    def body(i, ins, fulls, outs, accs):
        _acc_init(i, accs)
        for h in range(X_HEADS):
            sl = slice(h * LANE, (h + 1) * LANE)
            slv = slice(w + h * LANE, w + (h + 1) * LANE)
            qh, kh, vh, doh = ins[0][:, sl], fulls[0][:, sl], fulls[0][:, slv], ins[1][:, sl]
            p = _xattn_probs(qh, kh)
            dp = _dot_nt(doh, vh)
            ds = (p * (dp - jnp.sum(p * dp, axis=1, keepdims=True)) * (1.0 / math.sqrt(X_HEAD_DIM))).astype(BF16)
            outs[0][:, sl] = _dot(ds, kh).astype(BF16)
            accs[0][:, sl] += _dot_tn(ds, qh)
            accs[0][:, slv] += _dot_tn(p.astype(BF16), doh)

    return _rowwise(body, name="xattn_bwd", rows=rows, tr=512, row_ins=[(xq, w, 0), (dxo, w, 0)], full_ins=[xkv],
                    row_outs=[(w, BF16)], acc_outs=[(xkv.shape, F32)])


def _gate_fwd(proj, pa, pb, b_gate):
    rows = proj.shape[0]

    def body(i, ins, fulls, outs, accs):
        sa = _sigmoid(ins[0][...] + fulls[0][0:1, :])
        sb = _sigmoid(ins[1][...] + fulls[0][1:2, :])
        outs[0][...] = (sa * ins[2][...] + sb * ins[3][...]).astype(BF16)

    return _rowwise(body, name="gate_fwd", rows=rows, tr=512,
                    row_ins=[(proj, D_MODEL, C_GA // D_MODEL), (proj, D_MODEL, C_GB // D_MODEL), (pa, D_MODEL, 0),
                             (pb, D_MODEL, 0)],
                    full_ins=[b_gate], row_outs=[(D_MODEL, BF16)])[0]


def _gate_bwd(proj, pa, pb, b_gate, dm):
    rows = proj.shape[0]

    def body(i, ins, fulls, outs, accs):
        _acc_init(i, accs)
        d = ins[4][...]
        sa = _sigmoid(ins[0][...] + fulls[0][0:1, :])
        sb = _sigmoid(ins[1][...] + fulls[0][1:2, :])
        dga = d * ins[2][...] * sa * (1.0 - sa)
        dgb = d * ins[3][...] * sb * (1.0 - sb)
        outs[0][...] = (d * sa).astype(BF16)
        outs[1][...] = (d * sb).astype(BF16)
        outs[2][...] = dga.astype(BF16)
        outs[3][...] = dgb.astype(BF16)
        accs[0][0:1, :] += jnp.sum(dga, axis=0, keepdims=True)
        accs[0][1:2, :] += jnp.sum(dgb, axis=0, keepdims=True)

    return _rowwise(body, name="gate_bwd", rows=rows, tr=512,
                    row_ins=[(proj, D_MODEL, C_GA // D_MODEL), (proj, D_MODEL, C_GB // D_MODEL), (pa, D_MODEL, 0),
                             (pb, D_MODEL, 0), (dm, D_MODEL, 0)],
                    full_ins=[b_gate], row_outs=[(D_MODEL, BF16)] * 4, acc_outs=[((2, D_MODEL), F32)])


def _loss_head(x3, target, g_final):
    rows = x3.shape[0]

    def body(i, ins, fulls, outs, accs):
        _acc_init(i, accs)
        xv, g = ins[0][...], fulls[0][...]
        d = _rms(xv, g) - ins[1][...]
        dx, dg = _rms_bwd(xv, g, d * (1.0 / D_MODEL))
        outs[0][...] = dx
        accs[0][...] += dg
        accs[1][...] += jnp.sum(d * d, axis=0, keepdims=True)

    return _rowwise(body, name="loss_head", rows=rows, tr=512, row_ins=[(x3, D_MODEL, 0), (target, D_MODEL, 0)],
                    full_ins=[g_final], row_outs=[(D_MODEL, F32)], acc_outs=[((1, D_MODEL), F32), ((1, D_MODEL), F32)])


def _adamw(w, g, m, v, name):
    rows, c = w.shape

    def body(i, ins, fulls, outs, accs):
        wv, gv = ins[0][...], ins[1][...]
        mn = ADAM_B1 * ins[2][...] + (1.0 - ADAM_B1) * gv
        vn = ADAM_B2 * ins[3][...] + (1.0 - ADAM_B2) * jnp.square(gv)
        m_hat = mn / (1.0 - ADAM_B1 ** ADAM_STEP)
        v_hat = vn / (1.0 - ADAM_B2 ** ADAM_STEP)
        outs[0][...] = -ADAM_LR * (m_hat / (jnp.sqrt(v_hat) + ADAM_EPS) + ADAM_WD * wv)
        outs[1][...] = mn
        outs[2][...] = vn

    return _rowwise(body, name=name, rows=rows, tr=_row_tile(rows, 256), row_ins=[(a, c, 0) for a in (w, g, m, v)],
                    row_outs=[(c, F32)] * 3)


def _place():
    x, y, c = lax.axis_index("x"), lax.axis_index("y"), lax.axis_index("c")
    chips = [(1 - x, y), (x, 1 - y), (1 - x, 1 - y)]
    return x, y, c, chips


ANY = pl.BlockSpec(memory_space=pl.ANY)


def _remote(src, dst, send_sem, recv_sem, to):
    return pltpu.make_async_remote_copy(src_ref=src, dst_ref=dst, send_sem=send_sem, recv_sem=recv_sem,
                                        device_id=to, device_id_type=MESH)


def _dma_sems(n):
    return [pltpu.SemaphoreType.DMA((n,)), pltpu.SemaphoreType.DMA((n,))]


def _all_gather_weights(shards):
    n = len(shards)

    def body(*refs):
        ins, outs = refs[:n], refs[n:2 * n]
        send_sems, recv_sems, local_sems = refs[2 * n:]
        x, y, c, chips = _place()
        me = 2 * x + y
        local = [pltpu.make_async_copy(ins[t], outs[t].at[me], local_sems.at[t]) for t in range(n)]
        for cp in local:
            cp.start()

        def copy(t, j, chip_idx, hlf, to, src=None):
            h = ins[t].shape[0] // 2
            dst = outs[t].at[chip_idx, pl.ds(hlf * h, h), :]
            return _remote(dst if src is None else src, dst, send_sems.at[6 * t + j], recv_sems.at[6 * t + j], to)

        first = []
        for t in range(n):
            h = ins[t].shape[0] // 2
            for j, chip in enumerate(chips):
                first.append(copy(t, j, me, c, (*chip, c), src=ins[t].at[pl.ds(c * h, h), :]))
                first[-1].start()
        passed = []
        for t in range(n):
            for j, chip in enumerate(chips):
                idx = 2 * chip[0] + chip[1]
                copy(t, j, idx, c, (x, y, c)).wait_recv()
                passed.append(copy(t, 3 + j, idx, c, (x, y, 1 - c)))
                passed[-1].start()
        for t in range(n):
            for j, chip in enumerate(chips):
                copy(t, 3 + j, 2 * chip[0] + chip[1], 1 - c, (x, y, c)).wait_recv()
        for cp in first + passed:
            cp.wait_send()
        for cp in local:
            cp.wait()

    return pl.pallas_call(
        body, name="all_gather_weights", in_specs=[ANY] * n, out_specs=[ANY] * n,
        out_shape=[jax.ShapeDtypeStruct((N_CHIPS,) + s.shape, s.dtype) for s in shards],
        scratch_shapes=_dma_sems(6 * n) + [pltpu.SemaphoreType.DMA((n,))],
    )(*shards)


def _pair_exchange_grads(stacked):
    n = len(stacked)

    def body(*refs):
        ins, outs = refs[:n], refs[n:2 * n]
        send_sems, recv_sems = refs[2 * n:]
        x, y, c, _ = _place()
        cps = []
        for t in range(n):
            h = ins[t].shape[1] // 2
            cps.append(_remote(ins[t].at[:, pl.ds((1 - c) * h, h), :], outs[t], send_sems.at[t], recv_sems.at[t],
                               (x, y, 1 - c)))
            cps[-1].start()
        for cp in cps:
            cp.wait()

    return pl.pallas_call(
        body, name="pair_exchange_grads", in_specs=[ANY] * n, out_specs=[ANY] * n,
        out_shape=[jax.ShapeDtypeStruct((N_CHIPS, s.shape[1] // 2, s.shape[2]), s.dtype) for s in stacked],
        scratch_shapes=_dma_sems(n),
    )(*stacked)


def _chip_scatter(parts):
    n = len(parts)

    def body(*refs):
        ins, outs = refs[:n], refs[n:2 * n]
        send_sems, recv_sems = refs[2 * n:]
        x, y, c, chips = _place()
        cps = []
        for t in range(n):
            for j, chip in enumerate(chips):
                cps.append(_remote(ins[t].at[2 * chip[0] + chip[1]], outs[t].at[j], send_sems.at[3 * t + j],
                                   recv_sems.at[3 * t + j], (*chip, c)))
                cps[-1].start()
        for cp in cps:
            cp.wait()

    return pl.pallas_call(
        body, name="chip_scatter", in_specs=[ANY] * n, out_specs=[ANY] * n,
        out_shape=[jax.ShapeDtypeStruct((N_CHIPS - 1,) + s.shape[1:], s.dtype) for s in parts],
        scratch_shapes=_dma_sems(3 * n),
    )(*parts)


def _pair_exchange_halves(halves):
    n = len(halves)

    def body(*refs):
        ins, outs = refs[:n], refs[n:2 * n]
        send_sems, recv_sems, local_sems = refs[2 * n:]
        x, y, c, _ = _place()
        cps = []
        for t in range(n):
            h = ins[t].shape[0]
            rows = outs[t].at[pl.ds(c * h, h), :]
            cps.append(pltpu.make_async_copy(ins[t], rows, local_sems.at[t]))
            cps.append(_remote(ins[t], rows, send_sems.at[t], recv_sems.at[t], (x, y, 1 - c)))
        for cp in cps:
            cp.start()
        for cp in cps:
            cp.wait()

    return pl.pallas_call(
        body, name="pair_exchange_halves", in_specs=[ANY] * n, out_specs=[ANY] * n,
        out_shape=[jax.ShapeDtypeStruct((2 * s.shape[0], s.shape[1]), s.dtype) for s in halves],
        scratch_shapes=_dma_sems(n) + [pltpu.SemaphoreType.DMA((n,))],
    )(*halves)


def _pair_sum(gs, recv, place, name):
    _, r, cols = gs.shape
    h = r // 2

    def kern(p_ref, a_ref, b_ref, o_ref):
        o_ref[...] = (a_ref[...] + b_ref[...]).astype(BF16)

    blk = lambda f: pl.BlockSpec((1, h, cols), f)
    return pl.pallas_call(
        kern, name=name,
        grid_spec=pltpu.PrefetchScalarGridSpec(
            num_scalar_prefetch=1, grid=(N_CHIPS,),
            in_specs=[blk(lambda d, p: (d, p[1], 0)), blk(lambda d, p: (d, 0, 0))],
            out_specs=blk(lambda d, p: (d, 0, 0))),
        out_shape=jax.ShapeDtypeStruct((N_CHIPS, h, cols), BF16),
        compiler_params=pltpu.CompilerParams(dimension_semantics=("arbitrary",),
                                             vmem_limit_bytes=_vmem_limit(3 * _nbytes((h, cols), F32), 0)),
    )(place, gs, recv)


def _chip_sum(gs, recv, got, place, name):
    _, r, cols = gs.shape
    h = r // 2

    def kern(p_ref, a_ref, b_ref, g0, g1, g2, o_ref):
        own = a_ref[0] + b_ref[0]
        o_ref[...] = ((own + g0[0].astype(F32)) + g1[0].astype(F32)) + g2[0].astype(F32)

    blk = lambda f: pl.BlockSpec((1, h, cols), f)
    return pl.pallas_call(
        kern, name=name,
        grid_spec=pltpu.PrefetchScalarGridSpec(
            num_scalar_prefetch=1, grid=(1,),
            in_specs=[blk(lambda i, p: (p[0], p[1], 0)), blk(lambda i, p: (p[0], 0, 0)), blk(lambda i, p: (0, 0, 0)),
                      blk(lambda i, p: (1, 0, 0)), blk(lambda i, p: (2, 0, 0))],
            out_specs=pl.BlockSpec((h, cols), lambda i, p: (0, 0))),
        out_shape=jax.ShapeDtypeStruct((h, cols), F32),
        compiler_params=pltpu.CompilerParams(dimension_semantics=("arbitrary",),
                                             vmem_limit_bytes=_vmem_limit(5 * _nbytes((h, cols), F32), 0)),
    )(place, gs, recv, got, got, got)


def _all_reduce_small(vec, name):
    r, cols = vec.shape

    def body(in_ref, out_ref, gath, send_sems, recv_sems):
        x, y, c, _ = _place()
        me = 4 * x + 2 * y + c
        gath[me] = in_ref[...]
        sends = []
        for k in range(1, 8):
            to = (x ^ (k >> 2), y ^ ((k >> 1) & 1), c ^ (k & 1))
            cp = pltpu.make_async_remote_copy(src_ref=in_ref, dst_ref=gath.at[me], send_sem=send_sems.at[k - 1],
                                              recv_sem=recv_sems.at[k - 1], device_id=to, device_id_type=MESH)
            cp.start()
            sends.append(cp)
        for k in range(1, 8):
            peer = me ^ k
            pltpu.make_async_remote_copy(src_ref=in_ref, dst_ref=gath.at[peer], send_sem=send_sems.at[k - 1],
                                         recv_sem=recv_sems.at[k - 1], device_id=(x, y, c),
                                         device_id_type=MESH).wait_recv()
        for cp in sends:
            cp.wait_send()
        acc = gath[0]
        for d in range(1, 8):
            acc = acc + gath[d]
        out_ref[...] = acc

    vm = pl.BlockSpec(memory_space=pltpu.VMEM)
    return pl.pallas_call(
        body, name=name, in_specs=[vm], out_specs=vm,
        out_shape=jax.ShapeDtypeStruct((r, cols), F32),
        scratch_shapes=[pltpu.VMEM((8, r, cols), F32), pltpu.SemaphoreType.DMA((7,)), pltpu.SemaphoreType.DMA((7,))],
    )(vec)


def _pad_heads(w, heads, dim, axis):
    shp = w.shape[:axis] + (heads, dim) + w.shape[axis + 1:]
    pad = [(0, 0)] * len(shp)
    pad[axis + 1] = (0, LANE - dim)
    w = jnp.pad(w.reshape(shp), pad)
    return w.reshape(w.shape[:axis] + (heads * LANE,) + w.shape[axis + 2:])


def _unpad_heads(w, heads, dim, axis):
    shp = w.shape[:axis] + (heads, LANE) + w.shape[axis + 1:]
    w = lax.slice_in_dim(w.reshape(shp), 0, dim, axis=axis + 1)
    return w.reshape(w.shape[:axis] + (heads * dim,) + w.shape[axis + 2:])


def _w_in_layout(w_in):
    kr = jnp.pad(w_in[:, 384:416], ((0, 0), (ROPE_LO, LANE - ROPE_LO - MLA_ROPE)))
    sb = lambda lo: _pad_heads(w_in[:, lo:lo + 512], SB_HEADS, SB_HEAD_DIM, 1)
    return jnp.concatenate([w_in[:, 1952:2976], w_in[:, 2976:4000], sb(416), sb(928), sb(1440), w_in[:, 0:256],
                            w_in[:, 256:384], kr], axis=1)


def _w_in_unlayout(d):
    sb = lambda lo: _unpad_heads(d[:, lo:lo + 1024], SB_HEADS, SB_HEAD_DIM, 1)
    return jnp.concatenate([d[:, C_CQ:C_CQ + 256], d[:, C_CKV:C_CKV + 128], d[:, C_KR + ROPE_LO:C_KR + ROPE_LO + MLA_ROPE],
                            sb(C_SBQ), sb(C_SBK), sb(C_SBV), d[:, C_GA:C_GA + 1024], d[:, C_GB:C_GB + 1024]], axis=1)


def _w_ukv_layout(w):
    w3 = w.reshape(MLA_KV_RANK, MLA_HEADS, MLA_NOPE + MLA_V)
    pad = lambda part: jnp.pad(part, ((0, 0), (0, 0), (0, LANE - part.shape[2]))).reshape(MLA_KV_RANK, MLA_HEADS * LANE)
    return jnp.concatenate([pad(w3[:, :, :MLA_NOPE]), pad(w3[:, :, MLA_NOPE:])], axis=1)


def _w_ukv_unlayout(d):
    hw = MLA_HEADS * LANE
    kpart = d[:, :hw].reshape(MLA_KV_RANK, MLA_HEADS, LANE)[:, :, :MLA_NOPE]
    vpart = d[:, hw:].reshape(MLA_KV_RANK, MLA_HEADS, LANE)[:, :, :MLA_V]
    return jnp.concatenate([kpart, vpart], axis=2).reshape(MLA_KV_RANK, MLA_HEADS * (MLA_NOPE + MLA_V))


def _shard_of(full, d, axis):
    n = full.shape[axis] // N_CHIPS
    return lax.slice_in_dim(full, d * n, (d + 1) * n, axis=axis)


def _local_step(x, mem, pos, target, w, t_mla, t_sb):
    s = x.shape[0]
    win = _w_in_layout(w["w_in"])
    wuq = _pad_heads(w["w_uq"], MLA_HEADS, MLA_NOPE + MLA_ROPE, 1)
    wkv = _w_ukv_layout(w["w_ukv"])
    wa = _pad_heads(w["w_a_proj"], MLA_HEADS, MLA_V, 0)
    wb = _pad_heads(w["w_b_proj"], SB_HEADS, SB_HEAD_DIM, 0)
    inv_freq = ROPE_THETA ** (-jnp.arange(0, MLA_ROPE, 2, dtype=F32) / MLA_ROPE)
    freq_lane = jnp.pad(jnp.concatenate([inv_freq, inv_freq]), (ROPE_LO, LANE - ROPE_LO - MLA_ROPE)).reshape(1, LANE)
    add = lambda accs, ex: (accs[0] + ex[0],)

    tab = _rope_tables(pos.reshape(s, 1), freq_lane)
    h = _rms_fwd_call(x, w["g_mix"], "rms_mix")
    proj = _mm(h, [win], name="proj_in", tn=1408)
    cqn, ckvn, krope = _mla_prep_fwd(proj, tab, w["g_q_lat"], w["g_kv_lat"])
    qp = _mm(cqn, [wuq], name="q_up")
    kvp = _mm(ckvn, [wkv], name="kv_up")
    qa, ka, va = _mla_rope_fwd(qp, kvp, krope, tab)
    o_a, lse = _mla_fwd(qa, ka, va, t_mla)
    o_b = _sb_fwd(proj, t_sb)
    pa = _mm(o_a, [wa], name="proj_a")
    pb = _mm(o_b, [wb], name="proj_b")
    merged = _gate_fwd(proj, pa, pb, w["b_gate"])
    x1 = _mm(merged, [w["w_o"]], name="proj_o", extras=(x,), epilogue=add)
    hx = _rms_fwd_call(x1, w["g_x"], "rms_x")
    mn = _rms_fwd_call(mem, w["g_mem"], "rms_mem")
    xq = _mm(hx, [w["w_xq"]], name="xq", out_dtypes=(BF16,))
    xkv = _mm(mn, [w["w_xkv"]], name="xkv", out_dtypes=(BF16,))
    xo = _xattn_fwd(xq, xkv)
    x2 = _mm(xo, [w["w_xo"]], name="proj_xo", extras=(x1,), epilogue=add)
    hf = _rms_fwd_call(x2, w["g_ffn"], "rms_ffn")

    def swiglu(accs, ex):
        a, b = accs
        return a, b, a * _sigmoid(a) * b

    ga, gu, hmid = _mm(hf, [w["w_gate"], w["w_up"]], name="ffn_up", epilogue=swiglu, out_dtypes=(BF16, BF16, BF16),
                       tm=512, tn=1408)
    x3 = _mm(hmid, [w["w_down"]], name="ffn_down", extras=(x2,), epilogue=add)

    dx3, dg_final, sq = _loss_head(x3, target, w["g_final"].reshape(1, D_MODEL))
    g = {"g_final": dg_final.reshape(D_MODEL)}

    def swiglu_bwd(accs, ex):
        dh, a, b = accs[0], ex[0].astype(F32), ex[1].astype(F32)
        sg = _sigmoid(a)
        return dh * b * sg * (1.0 + a * (1.0 - sg)), dh * a * sg

    da, db = _mm(dx3, [w["w_down"]], name="ffn_down_dx", tb=True, extras=(ga, gu), epilogue=swiglu_bwd,
                 out_dtypes=(BF16, BF16), tm=512, tn=1408)
    g["w_down"] = _mm(hmid, [dx3], name="ffn_down_dw", ta=True, tm=1408)
    g["w_gate"] = _mm(hf, [da], name="ffn_gate_dw", ta=True, tn=1408)
    g["w_up"] = _mm(hf, [db], name="ffn_up_dw", ta=True, tn=1408)
    dhf = _mm(da, [w["w_gate"]], name="ffn_gate_dx", tb=True)
    dhf = _mm(db, [w["w_up"]], name="ffn_up_dx", tb=True, extras=(dhf,), epilogue=add)
    dx2, g["g_ffn"] = _rms_bwd_call(x2, w["g_ffn"], dhf, dx3, "rms_ffn_bwd")

    dxo = _mm(dx2, [w["w_xo"]], name="proj_xo_dx", tb=True, out_dtypes=(BF16,))
    g["w_xo"] = _mm(xo, [dx2], name="proj_xo_dw", ta=True)
    dxq, dxkv = _xattn_bwd(xq, xkv, dxo)
    dhx = _mm(dxq, [w["w_xq"]], name="xq_dx", tb=True)
    g["w_xq"] = _mm(hx, [dxq], name="xq_dw", ta=True)
    dmn = _mm(dxkv, [w["w_xkv"]], name="xkv_dx", tb=True)
    g["w_xkv"] = _mm(mn, [dxkv], name="xkv_dw", ta=True)
    dx1, g["g_x"] = _rms_bwd_call(x1, w["g_x"], dhx, dx2, "rms_x_bwd")
    _, g["g_mem"] = _rms_bwd_call(mem, w["g_mem"], dmn, None, "rms_mem_bwd")

    dmerged = _mm(dx1, [w["w_o"]], name="proj_o_dx", tb=True)
    g["w_o"] = _mm(merged, [dx1], name="proj_o_dw", ta=True)
    dpa, dpb, dga, dgb, g["b_gate"] = _gate_bwd(proj, pa, pb, w["b_gate"], dmerged)
    do_a = _mm(dpa, [wa], name="proj_a_dx", tb=True, out_dtypes=(BF16,))
    do_b = _mm(dpb, [wb], name="proj_b_dx", tb=True, out_dtypes=(BF16,))
    g["w_a_proj"] = _unpad_heads(_mm(o_a, [dpa], name="proj_a_dw", ta=True), MLA_HEADS, MLA_V, 0)
    g["w_b_proj"] = _unpad_heads(_mm(o_b, [dpb], name="proj_b_dw", ta=True), SB_HEADS, SB_HEAD_DIM, 0)

    dsq, dsk, dsv = _sb_bwd(proj, o_b, do_b, t_sb)
    dqa, dka, dva = _mla_bwd(qa, ka, va, o_a, do_a, lse, t_mla)
    dqp, dkvp, dkr = _mla_rope_bwd(dqa, dka, dva, tab)
    g["w_uq"] = _unpad_heads(_mm(cqn, [dqp], name="q_up_dw", ta=True), MLA_HEADS, MLA_NOPE + MLA_ROPE, 1)
    g["w_ukv"] = _w_ukv_unlayout(_mm(ckvn, [dkvp], name="kv_up_dw", ta=True))
    dcqn = _mm(dqp, [wuq], name="q_up_dx", tb=True)
    dckvn = _mm(dkvp, [wkv], name="kv_up_dx", tb=True)
    dcq, dckv, g["g_q_lat"], g["g_kv_lat"] = _mla_prep_bwd(proj, w["g_q_lat"], w["g_kv_lat"], dcqn, dckvn)

    dproj = jnp.concatenate([dga, dgb, dsq.astype(BF16), dsk.astype(BF16), dsv.astype(BF16), dcq, dckv, dkr], axis=1)
    g["w_in"] = _w_in_unlayout(_mm(h, [dproj], name="proj_in_dw", ta=True, tn=1408))
    dh = _mm(dproj, [win], name="proj_in_dx", tb=True)
    grad_x, g["g_mix"] = _rms_bwd_call(x, w["g_mix"], dh, dx1, "rms_mix_bwd")
    return sq, grad_x, g


def _small_pack(d):
    row5 = jnp.concatenate([d["g_q_lat"].reshape(-1), d["g_kv_lat"].reshape(-1), jnp.zeros((640,), F32)])
    rows = [d[n].reshape(-1) for n in ("g_mix", "g_x", "g_mem", "g_ffn", "g_final")] + [row5]
    return rows


def _small_unpack(p, like):
    out = {n: p[i].reshape(like[n].shape) for i, n in enumerate(("g_mix", "g_x", "g_mem", "g_ffn", "g_final"))}
    out["g_q_lat"] = p[5, 0:256].reshape(like["g_q_lat"].shape)
    out["g_kv_lat"] = p[5, 256:384].reshape(like["g_kv_lat"].shape)
    return out


def kernel(x, mem, positions, g_mix, w_in, b_gate, g_q_lat, w_uq, g_kv_lat, w_ukv, w_a_proj, w_b_proj, w_o, g_x, g_mem, w_xq, w_xkv, w_xo, g_ffn, w_gate, w_up, w_down, g_final, loss_target, m_g_mix, m_w_in, m_b_gate, m_g_q_lat, m_w_uq, m_g_kv_lat, m_w_ukv, m_w_a_proj, m_w_b_proj, m_w_o, m_g_x, m_g_mem, m_w_xq, m_w_xkv, m_w_xo, m_g_ffn, m_w_gate, m_w_up, m_w_down, m_g_final, v_g_mix, v_w_in, v_b_gate, v_g_q_lat, v_w_uq, v_g_kv_lat, v_w_ukv, v_w_a_proj, v_w_b_proj, v_w_o, v_g_x, v_g_mem, v_w_xq, v_w_xkv, v_w_xo, v_g_ffn, v_w_gate, v_w_up, v_w_down, v_g_final):
    given = dict(locals())
    names = [n for n, _, _ in MATS] + ["b_gate"] + list(SMALL)
    wts = {n: given[n] for n in names}
    mom = {n: given["m_" + n] for n in names}
    var = {n: given["v_" + n] for n in names}
    shard2d = {n: shp for n, shp, _ in MATS}
    shard2d["b_gate"] = B_GATE_SHARD
    cx, cy, cc = lax.axis_index("x"), lax.axis_index("y"), lax.axis_index("c")
    me = 2 * cx + cy
    place = jnp.stack([me, cc]).astype(jnp.int32)
    bcol = me * B_GATE_SHARD[1]

    gathered = _all_gather_weights([wts[n].reshape(shard2d[n]).astype(BF16) for n, _, _ in MATS])
    full = {}
    for (n, shp, ax), g4 in zip(MATS, gathered):
        full[n] = g4.reshape(N_CHIPS * shp[0], shp[1]) if ax == 0 else jnp.concatenate(list(g4), axis=1)
    bias_rows = jnp.pad(wts["b_gate"].reshape(B_GATE_SHARD), ((0, SMALL_ROWS - 2), (0, 0)))
    bias_rows = lax.dynamic_update_slice(jnp.zeros((SMALL_ROWS, D_MODEL), F32), bias_rows, (0, bcol))
    full["b_gate"] = _all_reduce_small(jnp.where(cc == 0, bias_rows, 0.0), "all_gather_bias")[0:2]
    for n in SMALL:
        full[n] = wts[n].reshape(1, -1) if n != "g_final" else wts[n]

    sq, grad_x, grads = _local_step(x[0], mem[0], positions[0], loss_target[0], full, t_mla=512, t_sb=256)

    stacked = [jnp.stack([_shard_of(grads[n], d, ax) for d in range(N_CHIPS)]) for n, _, ax in MATS]
    recv = _pair_exchange_grads(stacked)
    parts = [_pair_sum(gs, rv, place, "pair_sum_" + n) for (n, _, _), gs, rv in zip(MATS, stacked, recv)]
    got = _chip_scatter(parts)
    halves = [_chip_sum(gs, rv, gt, place, "chip_sum_" + n) for (n, _, _), gs, rv, gt in zip(MATS, stacked, recv, got)]
    g_shard = dict(zip([n for n, _, _ in MATS], _pair_exchange_halves(halves)))

    small_rows = _small_pack({n: grads[n] for n in SMALL}) + [sq.reshape(-1), grads["b_gate"][0], grads["b_gate"][1]]
    small_rows += [jnp.zeros((D_MODEL,), F32)] * (SMALL_ROWS - len(small_rows))
    small = _all_reduce_small(jnp.stack(small_rows), "all_reduce_small")
    loss = (0.5 / D_MODEL) * jnp.sum(small[6])
    g_shard["b_gate"] = lax.dynamic_slice(small[7:9], (0, bcol), B_GATE_SHARD)

    out = {"grad": {}, "delta": {}, "m": {}, "v": {}}
    for n in [n for n, _, _ in MATS] + ["b_gate"]:
        shape = wts[n].shape
        r2 = lambda a: a.reshape(shard2d[n])
        d_n, m_n, v_n = _adamw(r2(wts[n]), g_shard[n], r2(mom[n]), r2(var[n]), "adamw_" + n)
        for key, a in (("grad", g_shard[n]), ("delta", d_n), ("m", m_n), ("v", v_n)):
            out[key][n] = a.reshape(shape)
    sp = lambda d: jnp.stack(_small_pack(d) + [jnp.zeros((D_MODEL,), F32)] * 2)
    delta_s, m_s, v_s = _adamw(sp(wts), small[0:8].at[6:8].set(0.0), sp(mom), sp(var), "adamw_small")
    for key, p in (("grad", small), ("delta", delta_s), ("m", m_s), ("v", v_s)):
        out[key].update(_small_unpack(p, wts))

    order = ["g_mix", "w_in", "b_gate", "g_q_lat", "w_uq", "g_kv_lat", "w_ukv", "w_a_proj", "w_b_proj", "w_o", "g_x",
             "g_mem", "w_xq", "w_xkv", "w_xo", "g_ffn", "w_gate", "w_up", "w_down", "g_final"]
    return (loss, grad_x[None], *[out[key][n] for key in ("grad", "delta", "m", "v") for n in order])
```

```python
import functools
import math

import jax
import jax.numpy as jnp
from jax import lax
from jax.experimental import pallas as pl
from jax.experimental.pallas import tpu as pltpu

F32 = jnp.float32
BF16 = jnp.bfloat16
MESH = pl.DeviceIdType.MESH

D_MODEL = 1024
MLA_HEADS = 8
MLA_Q_RANK = 256
MLA_KV_RANK = 128
MLA_NOPE = 64
MLA_ROPE = 32
MLA_V = 64
ROPE_THETA = 10000.0
SB_HEADS = 8
SB_HEAD_DIM = 64
X_HEADS = 4
X_HEAD_DIM = 128
D_FF = 2816
EPS = 1e-6
ADAM_LR = 0.001
ADAM_B1 = 0.9
ADAM_B2 = 0.999
ADAM_EPS = 1e-08
ADAM_WD = 0.01
ADAM_STEP = 10

LANE = 128
N_CHIPS = 4
VMEM_BYTES = 64 * 1024 * 1024

C_GA, C_GB, C_SBQ, C_SBK, C_SBV, C_CQ, C_CKV, C_KR = 0, 1024, 2048, 3072, 4096, 5120, 5376, 5504
D_INP = 5632
ROPE_LO = MLA_NOPE
HALF = MLA_ROPE // 2

SB_ZERO_LOG = -104.0

MATS = (
    ("w_in", (1024, 1000), 1), ("w_uq", (256, 192), 1), ("w_ukv", (128, 256), 1), ("w_a_proj", (512, 256), 1),
    ("w_b_proj", (512, 256), 1), ("w_o", (256, 1024), 0), ("w_xq", (256, 512), 0), ("w_xkv", (256, 1024), 0),
    ("w_xo", (512, 256), 1), ("w_gate", (1024, 704), 1), ("w_up", (1024, 704), 1), ("w_down", (704, 1024), 0),
)
B_GATE_SHARD = (2, 256)
SMALL = ("g_mix", "g_x", "g_mem", "g_ffn", "g_final", "g_q_lat", "g_kv_lat")
SMALL_ROWS = 16


def _vmem_limit(block_bytes, temp_bytes):
    est = 2 * block_bytes + temp_bytes + (4 << 20)
    return int(min(max(est, 16 << 20), VMEM_BYTES - (6 << 20)))


def _nbytes(shape, dtype):
    return math.prod(shape) * jnp.dtype(dtype).itemsize


def _row_tile(rows, cap):
    if rows <= cap:
        return rows
    return max(t for t in range(8, cap + 1, 8) if rows % t == 0)


def _tile(n, cap):
    if n <= cap:
        return n
    best = None
    for t in range(LANE, cap + 1, LANE):
        if n % t == 0:
            best = t
    assert best is not None, (n, cap)
    return best


def _mm(a, bs, *, name, ta=False, tb=False, extras=(), epilogue=None, out_dtypes=(F32,), tm=1024, tn=1024, tk=1024):
    bs = tuple(bs)
    m, k = (a.shape[1], a.shape[0]) if ta else a.shape
    n = bs[0].shape[0] if tb else bs[0].shape[1]
    tm, tn, tk = _tile(m, tm), _tile(n, tn), _tile(k, tk)
    assert m % tm == 0 and n % tn == 0 and k % tk == 0
    nk = k // tk
    nb, ne, no = len(bs), len(extras), len(out_dtypes)
    dims = (((0,) if ta else (1,)), ((1,) if tb else (0,))), ((), ())
    if epilogue is None:
        epilogue = lambda accs, ex: (accs[0],)

    def body(*refs):
        a_ref, b_refs, e_refs = refs[0], refs[1:1 + nb], refs[1 + nb:1 + nb + ne]
        o_refs, acc_refs = refs[1 + nb + ne:1 + nb + ne + no], refs[1 + nb + ne + no:]
        kk = pl.program_id(2)

        @pl.when(kk == 0)
        def _():
            for acc in acc_refs:
                acc[...] = jnp.zeros_like(acc)

        av = a_ref[...].astype(BF16)
        for b_ref, acc in zip(b_refs, acc_refs):
            acc[...] += lax.dot_general(av, b_ref[...].astype(BF16), dims, preferred_element_type=F32)

        @pl.when(kk == nk - 1)
        def _():
            outs = epilogue([acc[...] for acc in acc_refs], [e[...] for e in e_refs])
            for o_ref, v in zip(o_refs, outs):
                o_ref[...] = v.astype(o_ref.dtype)

    a_spec = pl.BlockSpec((tk, tm), lambda i, j, kk: (kk, i)) if ta else pl.BlockSpec((tm, tk), lambda i, j, kk: (i, kk))
    b_spec = pl.BlockSpec((tn, tk), lambda i, j, kk: (j, kk)) if tb else pl.BlockSpec((tk, tn), lambda i, j, kk: (kk, j))
    mn_spec = pl.BlockSpec((tm, tn), lambda i, j, kk: (i, j))
    blocks = (_nbytes((tm, tk), a.dtype) + sum(_nbytes((tk, tn), b.dtype) for b in bs)
              + sum(_nbytes((tm, tn), e.dtype) for e in extras) + sum(_nbytes((tm, tn), d) for d in out_dtypes))
    temps = (nb + 4) * _nbytes((tm, tn), F32)
    outs = pl.pallas_call(
        body, name=name, grid=(m // tm, n // tn, nk),
        in_specs=[a_spec] + [b_spec] * nb + [mn_spec] * ne,
        out_specs=[mn_spec] * no,
        out_shape=[jax.ShapeDtypeStruct((m, n), d) for d in out_dtypes],
        scratch_shapes=[pltpu.VMEM((tm, tn), F32) for _ in range(nb)],
        compiler_params=pltpu.CompilerParams(
            dimension_semantics=("parallel", "parallel", "arbitrary"),
            vmem_limit_bytes=_vmem_limit(blocks, temps)),
    )(a, *bs, *extras)
    return outs[0] if no == 1 else outs


def _rowwise(body, *, name, rows, tr, row_ins, full_ins=(), row_outs=(), acc_outs=()):
    tr = min(tr, rows)
    assert rows % tr == 0
    n_ri, n_fi, n_ro = len(row_ins), len(full_ins), len(row_outs)

    def kern(*refs):
        body(pl.program_id(0), refs[:n_ri], refs[n_ri:n_ri + n_fi], refs[n_ri + n_fi:n_ri + n_fi + n_ro],
             refs[n_ri + n_fi + n_ro:])

    in_specs = [pl.BlockSpec((tr, w), functools.partial(lambda i, c: (i, c), c=ci)) for _, w, ci in row_ins]
    in_specs += [pl.BlockSpec(f.shape, lambda i: (0, 0)) for f in full_ins]
    out_specs = [pl.BlockSpec((tr, w), lambda i: (i, 0)) for w, _ in row_outs]
    out_specs += [pl.BlockSpec(s, lambda i: (0, 0)) for s, _ in acc_outs]
    out_shape = [jax.ShapeDtypeStruct((rows, w), d) for w, d in row_outs]
    out_shape += [jax.ShapeDtypeStruct(s, d) for s, d in acc_outs]
    blocks = (sum(_nbytes((tr, w), a.dtype) for a, w, _ in row_ins) + sum(_nbytes(f.shape, f.dtype) for f in full_ins)
              + sum(_nbytes((tr, w), d) for w, d in row_outs) + sum(_nbytes(s, d) for s, d in acc_outs))
    widest = max([w for _, w, _ in row_ins] + [w for w, _ in row_outs])
    outs = pl.pallas_call(
        kern, name=name, grid=(rows // tr,), in_specs=in_specs, out_specs=out_specs, out_shape=out_shape,
        compiler_params=pltpu.CompilerParams(
            dimension_semantics=("arbitrary",) if acc_outs else ("parallel",),
            vmem_limit_bytes=_vmem_limit(blocks, 8 * _nbytes((tr, widest), F32))),
    )(*[a for a, _, _ in row_ins], *full_ins)
    return outs


def _rms(x, g):
    r = lax.rsqrt(jnp.mean(x * x, axis=-1, keepdims=True) + EPS)
    return x * r * g


def _rms_bwd(x, g, dy):
    r = lax.rsqrt(jnp.mean(x * x, axis=-1, keepdims=True) + EPS)
    xh = x * r
    dxh = dy * g
    dx = r * (dxh - xh * jnp.mean(dxh * xh, axis=-1, keepdims=True))
    return dx, jnp.sum(dy * xh, axis=0, keepdims=True)


def _sigmoid(x):
    return 1.0 / (1.0 + jnp.exp(-x))


def _acc_init(i, refs):
    @pl.when(i == 0)
    def _():
        for r in refs:
            r[...] = jnp.zeros_like(r)


def _rms_fwd_call(x, g, name):
    rows, c = x.shape

    def body(i, ins, fulls, outs, accs):
        outs[0][...] = _rms(ins[0][...], fulls[0][...]).astype(BF16)

    return _rowwise(body, name=name, rows=rows, tr=512, row_ins=[(x, c, 0)], full_ins=[g], row_outs=[(c, BF16)])[0]


def _rms_bwd_call(x, g, dy, res, name):
    rows, c = x.shape
    row_ins = [(x, c, 0), (dy, c, 0)] + ([(res, c, 0)] if res is not None else [])

    def body(i, ins, fulls, outs, accs):
        _acc_init(i, accs)
        dx, dg = _rms_bwd(ins[0][...], fulls[0][...], ins[1][...].astype(F32))
        if res is not None:
            dx = dx + ins[2][...]
        outs[0][...] = dx
        accs[0][...] += dg

    return _rowwise(body, name=name, rows=rows, tr=512, row_ins=row_ins, full_ins=[g], row_outs=[(c, F32)],
                    acc_outs=[((1, c), F32)])


def _rope_tables(pos_col, freq_lane):
    rows = pos_col.shape[0]

    def body(i, ins, fulls, outs, accs):
        ang = ins[0][...].astype(F32) * fulls[0][...]
        lane = lax.broadcasted_iota(jnp.int32, ang.shape, 1)
        cos, sin = jnp.cos(ang), jnp.sin(ang)
        first = (lane >= ROPE_LO) & (lane < ROPE_LO + HALF)
        second = (lane >= ROPE_LO + HALF) & (lane < ROPE_LO + MLA_ROPE)
        outs[0][:, 0:LANE] = jnp.where(first | second, cos, 1.0)
        outs[0][:, LANE:2 * LANE] = jnp.where(first, -sin, 0.0)
        outs[0][:, 2 * LANE:3 * LANE] = jnp.where(second, sin, 0.0)

    return _rowwise(body, name="rope_tables", rows=rows, tr=1024, row_ins=[(pos_col, 1, 0)], full_ins=[freq_lane],
                    row_outs=[(3 * LANE, F32)])[0]


def _rope(x, tab):
    return (x * tab[:, 0:LANE] + pltpu.roll(x, LANE - HALF, 1) * tab[:, LANE:2 * LANE]
            + pltpu.roll(x, HALF, 1) * tab[:, 2 * LANE:3 * LANE])


def _rope_t(dy, tab):
    return (dy * tab[:, 0:LANE] + pltpu.roll(dy * tab[:, LANE:2 * LANE], HALF, 1)
            + pltpu.roll(dy * tab[:, 2 * LANE:3 * LANE], LANE - HALF, 1))


def _mla_prep_fwd(proj, tab, g_q, g_kv):
    rows = proj.shape[0]

    def body(i, ins, fulls, outs, accs):
        outs[0][...] = _rms(ins[0][...], fulls[0][...]).astype(BF16)
        outs[1][...] = _rms(ins[1][...], fulls[1][...]).astype(BF16)
        outs[2][...] = _rope(ins[2][...], ins[3][...])

    return _rowwise(body, name="mla_prep_fwd", rows=rows, tr=512,
                    row_ins=[(proj, MLA_Q_RANK, C_CQ // MLA_Q_RANK), (proj, LANE, C_CKV // LANE),
                             (proj, LANE, C_KR // LANE), (tab, 3 * LANE, 0)],
                    full_ins=[g_q, g_kv], row_outs=[(MLA_Q_RANK, BF16), (MLA_KV_RANK, BF16), (LANE, F32)])


def _mla_prep_bwd(proj, g_q, g_kv, dcqn, dckvn):
    rows = proj.shape[0]

    def body(i, ins, fulls, outs, accs):
        _acc_init(i, accs)
        dcq, dgq = _rms_bwd(ins[0][...], fulls[0][...], ins[2][...])
        dckv, dgkv = _rms_bwd(ins[1][...], fulls[1][...], ins[3][...])
        outs[0][...] = dcq.astype(BF16)
        outs[1][...] = dckv.astype(BF16)
        accs[0][...] += dgq
        accs[1][...] += dgkv

    return _rowwise(body, name="mla_prep_bwd", rows=rows, tr=512,
                    row_ins=[(proj, MLA_Q_RANK, C_CQ // MLA_Q_RANK), (proj, LANE, C_CKV // LANE),
                             (dcqn, MLA_Q_RANK, 0), (dckvn, MLA_KV_RANK, 0)],
                    full_ins=[g_q, g_kv], row_outs=[(MLA_Q_RANK, BF16), (MLA_KV_RANK, BF16)],
                    acc_outs=[((1, MLA_Q_RANK), F32), ((1, MLA_KV_RANK), F32)])


def _mla_rope_fwd(qp, kvp, krope, tab):
    rows = qp.shape[0]
    hw = MLA_HEADS * LANE

    def body(i, ins, fulls, outs, accs):
        t = ins[3][...]
        kr = ins[2][...]
        for h in range(MLA_HEADS):
            sl = slice(h * LANE, (h + 1) * LANE)
            outs[0][:, sl] = _rope(ins[0][:, sl], t).astype(BF16)
            outs[1][:, sl] = (ins[1][:, sl] + kr).astype(BF16)
        outs[2][...] = ins[1][:, hw:2 * hw].astype(BF16)

    return _rowwise(body, name="mla_rope_fwd", rows=rows, tr=512,
                    row_ins=[(qp, hw, 0), (kvp, 2 * hw, 0), (krope, LANE, 0), (tab, 3 * LANE, 0)],
                    row_outs=[(hw, BF16), (hw, BF16), (hw, BF16)])


def _mla_rope_bwd(dq, dk, dv, tab):
    rows = dq.shape[0]
    hw = MLA_HEADS * LANE

    def body(i, ins, fulls, outs, accs):
        t = ins[3][...]
        dkr = jnp.zeros((ins[0].shape[0], LANE), F32)
        for h in range(MLA_HEADS):
            sl = slice(h * LANE, (h + 1) * LANE)
            outs[0][:, sl] = _rope_t(ins[0][:, sl], t).astype(BF16)
            dkr = dkr + ins[1][:, sl]
        outs[1][:, 0:hw] = ins[1][...].astype(BF16)
        outs[1][:, hw:2 * hw] = ins[2][...].astype(BF16)
        lane = lax.broadcasted_iota(jnp.int32, dkr.shape, 1)
        dkr = jnp.where((lane >= ROPE_LO) & (lane < ROPE_LO + MLA_ROPE), dkr, 0.0)
        outs[2][...] = _rope_t(dkr, t).astype(BF16)

    return _rowwise(body, name="mla_rope_bwd", rows=rows, tr=512,
                    row_ins=[(dq, hw, 0), (dk, hw, 0), (dv, hw, 0), (tab, 3 * LANE, 0)],
                    row_outs=[(hw, BF16), (2 * hw, BF16), (LANE, BF16)])


def _dot_nt(a, b):
    return lax.dot_general(a, b, (((1,), (1,)), ((), ())), preferred_element_type=F32)


def _dot_tn(a, b):
    return lax.dot_general(a, b, (((0,), (0,)), ((), ())), preferred_element_type=F32)


def _dot(a, b):
    return jnp.dot(a, b, preferred_element_type=F32)


def _attn_params(s, t, n_res_f32, n_res_bf16):
    blocks = n_res_f32 * _nbytes((s, LANE), F32) + n_res_bf16 * _nbytes((s, LANE), BF16) + 6 * _nbytes((t, LANE), F32)
    return pltpu.CompilerParams(dimension_semantics=("parallel", "arbitrary"),
                                vmem_limit_bytes=_vmem_limit(blocks, 12 * _nbytes((t, t), F32)))


def _mla_fwd(q, k, v, t):
    s, hw = q.shape
    heads, nq = hw // LANE, s // t
    scale = 1.0 / math.sqrt(MLA_NOPE + MLA_ROPE)

    def body(q_ref, k_ref, v_ref, o_ref, l_ref):
        i = pl.program_id(1)
        qv = q_ref[...]

        def step(j, carry, masked):
            m, l, acc = carry
            sl = pl.ds(pl.multiple_of(j * t, t), t)
            sc = _dot_nt(qv, k_ref[sl, :]) * scale
            if masked:
                row = lax.broadcasted_iota(jnp.int32, (t, t), 0)
                col = lax.broadcasted_iota(jnp.int32, (t, t), 1)
                sc = jnp.where(col <= row, sc, -1e30)
            m_new = jnp.maximum(m, jnp.max(sc, axis=1, keepdims=True))
            p = jnp.exp(sc - m_new)
            alpha = jnp.exp(m - m_new)
            l = alpha * l + jnp.sum(p, axis=1, keepdims=True)
            acc = alpha * acc + _dot(p.astype(BF16), v_ref[sl, :])
            return m_new, l, acc

        init = (jnp.full((t, 1), -1e30, F32), jnp.zeros((t, 1), F32), jnp.zeros((t, LANE), F32))
        carry = lax.fori_loop(0, i, lambda j, c: step(j, c, False), init)
        m, l, acc = step(i, carry, True)
        o_ref[...] = (acc / l).astype(o_ref.dtype)
        l_ref[0] = m + jnp.log(l)

    blk = pl.BlockSpec((t, LANE), lambda h, i: (i, h))
    res = pl.BlockSpec((s, LANE), lambda h, i: (0, h))
    return pl.pallas_call(
        body, name="mla_fwd", grid=(heads, nq), in_specs=[blk, res, res],
        out_specs=[blk, pl.BlockSpec((1, t, 1), lambda h, i: (h, i, 0))],
        out_shape=[jax.ShapeDtypeStruct((s, hw), BF16), jax.ShapeDtypeStruct((heads, s, 1), F32)],
        compiler_params=_attn_params(s, t, 0, 2),
    )(q, k, v)


def _mla_bwd(q, k, v, o, do, lse, t):
    s, hw = q.shape
    heads, nq = hw // LANE, s // t
    scale = 1.0 / math.sqrt(MLA_NOPE + MLA_ROPE)

    def body(q_ref, k_ref, v_ref, o_ref, do_ref, l_ref, dq_ref, dk_ref, dv_ref):
        i = pl.program_id(1)

        @pl.when(i == 0)
        def _():
            dk_ref[...] = jnp.zeros_like(dk_ref)
            dv_ref[...] = jnp.zeros_like(dv_ref)

        qv, dov, lv = q_ref[...], do_ref[...], l_ref[0]
        dlt = jnp.sum(dov.astype(F32) * o_ref[...].astype(F32), axis=1, keepdims=True)

        def step(j, dq, masked):
            sl = pl.ds(pl.multiple_of(j * t, t), t)
            kv, vv = k_ref[sl, :], v_ref[sl, :]
            p = jnp.exp(_dot_nt(qv, kv) * scale - lv)
            if masked:
                row = lax.broadcasted_iota(jnp.int32, (t, t), 0)
                col = lax.broadcasted_iota(jnp.int32, (t, t), 1)
                p = jnp.where(col <= row, p, 0.0)
            ds = (p * (_dot_nt(dov, vv) - dlt) * scale).astype(BF16)
            dk_ref[sl, :] += _dot_tn(ds, qv)
            dv_ref[sl, :] += _dot_tn(p.astype(BF16), dov)
            return dq + _dot(ds, kv)

        dq = lax.fori_loop(0, i, lambda j, c: step(j, c, False), jnp.zeros((t, LANE), F32))
        dq_ref[...] = step(i, dq, True)

    blk = pl.BlockSpec((t, LANE), lambda h, i: (i, h))
    res = pl.BlockSpec((s, LANE), lambda h, i: (0, h))
    full = jax.ShapeDtypeStruct((s, hw), F32)
    return pl.pallas_call(
        body, name="mla_bwd", grid=(heads, nq),
        in_specs=[blk, res, res, blk, blk, pl.BlockSpec((1, t, 1), lambda h, i: (h, i, 0))],
        out_specs=[blk, res, res], out_shape=[full, full, full],
        compiler_params=_attn_params(s, t, 2, 2),
    )(q, k, v, o, do, lse)


def _sb_logits(qv, kv, scale, masked, t, upper):
    z = _dot_nt(qv, kv) * scale
    e = jnp.exp(-jnp.abs(z))
    l1p = jnp.log1p(e)
    lb = jnp.minimum(z, 0.0) - l1p
    lo = -jnp.maximum(z, 0.0) - l1p
    keep = None
    if masked:
        row = lax.broadcasted_iota(jnp.int32, (t, t), 0)
        col = lax.broadcasted_iota(jnp.int32, (t, t), 1)
        keep = col < row
        lo = jnp.where(keep, lo, 0.0)
    hi = lo.astype(BF16)
    rem = (lo - hi.astype(F32)).astype(BF16)
    suf = _dot(hi, upper) + _dot(rem, upper)
    return z, e, lb, lo, suf, keep


def _tri(t, inclusive):
    row = lax.broadcasted_iota(jnp.int32, (t, t), 0)
    col = lax.broadcasted_iota(jnp.int32, (t, t), 1)
    return jnp.where((row >= col) if inclusive else (row > col), 1.0, 0.0).astype(BF16)


def _sb_fwd(proj, t):
    s = proj.shape[0]
    heads, nq = SB_HEADS, s // t
    scale = 1.0 / math.sqrt(SB_HEAD_DIM)

    def body(q_ref, k_ref, v_ref, o_ref):
        i = pl.program_id(1)
        qv = q_ref[...].astype(BF16)
        upper = _tri(t, False)

        def step(jj, carry, masked):
            c, acc = carry
            sl = pl.ds(pl.multiple_of((i - jj) * t, t), t)
            _, _, lb, lo, suf, keep = _sb_logits(qv, k_ref[sl, :].astype(BF16), scale, masked, t, upper)
            a = jnp.exp(lb + suf + c)
            if masked:
                a = jnp.where(keep, a, 0.0)
            acc = acc + _dot(a.astype(BF16), v_ref[sl, :].astype(BF16))
            return c + jnp.sum(lo, axis=1, keepdims=True), acc

        carry = step(0, (jnp.zeros((t, 1), F32), jnp.zeros((t, LANE), F32)), True)
        def live(st):
            return (st[0] <= i) & (jnp.max(st[1]) >= SB_ZERO_LOG)

        def more(st):
            return (st[0] + 1,) + step(st[0], st[1:], False)

        o_ref[...] = lax.while_loop(live, more, (jnp.int32(1),) + carry)[2]

    return pl.pallas_call(
        body, name="sb_fwd", grid=(heads, nq),
        in_specs=[pl.BlockSpec((t, LANE), lambda h, i: (i, C_SBQ // LANE + h)),
                  pl.BlockSpec((s, LANE), lambda h, i: (0, C_SBK // LANE + h)),
                  pl.BlockSpec((s, LANE), lambda h, i: (0, C_SBV // LANE + h))],
        out_specs=pl.BlockSpec((t, LANE), lambda h, i: (i, h)),
        out_shape=jax.ShapeDtypeStruct((s, heads * LANE), F32),
        compiler_params=_attn_params(s, t, 2, 0),
    )(proj, proj, proj)


def _sb_bwd(proj, o, do, t):
    s = proj.shape[0]
    heads, nq = SB_HEADS, s // t
    scale = 1.0 / math.sqrt(SB_HEAD_DIM)

    def body(q_ref, k_ref, v_ref, o_ref, do_ref, dq_ref, dk_ref, dv_ref):
        i = pl.program_id(1)

        @pl.when(i == 0)
        def _():
            dk_ref[...] = jnp.zeros_like(dk_ref)
            dv_ref[...] = jnp.zeros_like(dv_ref)

        qv, dov = q_ref[...].astype(BF16), do_ref[...]
        total = jnp.sum(dov.astype(F32) * o_ref[...], axis=1, keepdims=True)
        upper, upper_incl = _tri(t, False), _tri(t, True)

        def step(jj, carry, masked):
            c, g, dq = carry
            sl = pl.ds(pl.multiple_of((i - jj) * t, t), t)
            kv, vv = k_ref[sl, :].astype(BF16), v_ref[sl, :].astype(BF16)
            z, e, lb, lo, suf, keep = _sb_logits(qv, kv, scale, masked, t, upper)
            tail = suf + c
            a = jnp.exp(lb + tail)
            if masked:
                a = jnp.where(keep, a, 0.0)
            ab = a.astype(BF16)
            gr = ab.astype(F32) * _dot_nt(dov, vv)
            ghi = gr.astype(BF16)
            grem = (gr - ghi.astype(F32)).astype(BF16)
            before = total - g - (_dot(ghi, upper_incl) + _dot(grem, upper_incl))
            before = jnp.where(tail < SB_ZERO_LOG, 0.0, before)
            r = 1.0 / (1.0 + e)
            pos = z >= 0.0
            dz = r * (gr * jnp.where(pos, e, 1.0) - before * jnp.where(pos, 1.0, e))
            if masked:
                dz = jnp.where(keep, dz, 0.0)
            dzb = (dz * scale).astype(BF16)
            dk_ref[sl, :] += _dot_tn(dzb, qv)
            dv_ref[sl, :] += _dot_tn(ab, dov)
            return (c + jnp.sum(lo, axis=1, keepdims=True), g + jnp.sum(gr, axis=1, keepdims=True),
                    dq + _dot(dzb, kv))

        zero = jnp.zeros((t, 1), F32)
        carry = step(0, (zero, zero, jnp.zeros((t, LANE), F32)), True)
        def live(st):
            return (st[0] <= i) & (jnp.max(st[1]) >= SB_ZERO_LOG)

        def more(st):
            return (st[0] + 1,) + step(st[0], st[1:], False)

        dq_ref[...] = lax.while_loop(live, more, (jnp.int32(1),) + carry)[3]

    blk = pl.BlockSpec((t, LANE), lambda h, i: (i, h))
    res = pl.BlockSpec((s, LANE), lambda h, i: (0, h))
    full = jax.ShapeDtypeStruct((s, heads * LANE), F32)
    return pl.pallas_call(
        body, name="sb_bwd", grid=(heads, nq),
        in_specs=[pl.BlockSpec((t, LANE), lambda h, i: (i, C_SBQ // LANE + h)),
                  pl.BlockSpec((s, LANE), lambda h, i: (0, C_SBK // LANE + h)),
                  pl.BlockSpec((s, LANE), lambda h, i: (0, C_SBV // LANE + h)), blk, blk],
        out_specs=[blk, res, res], out_shape=[full, full, full],
        compiler_params=_attn_params(s, t, 4, 0),
    )(proj, proj, proj, o, do)


def _xattn_probs(qh, kh):
    sc = _dot_nt(qh, kh) * (1.0 / math.sqrt(X_HEAD_DIM))
    p = jnp.exp(sc - jnp.max(sc, axis=1, keepdims=True))
    return p / jnp.sum(p, axis=1, keepdims=True)


def _xattn_fwd(xq, xkv):
    rows = xq.shape[0]
    w = X_HEADS * X_HEAD_DIM

    def body(i, ins, fulls, outs, accs):
        for h in range(X_HEADS):
            sl = slice(h * LANE, (h + 1) * LANE)
            p = _xattn_probs(ins[0][:, sl], fulls[0][:, sl])
            outs[0][:, sl] = _dot(p.astype(BF16), fulls[0][:, w + h * LANE:w + (h + 1) * LANE]).astype(BF16)

    return _rowwise(body, name="xattn_fwd", rows=rows, tr=512, row_ins=[(xq, w, 0)], full_ins=[xkv],
                    row_outs=[(w, BF16)])[0]


def _xattn_bwd(xq, xkv, dxo):
    rows = xq.shape[0]
    w = X_HEADS * X_HEAD_DIM

    def body(i, ins, fulls, outs, accs):
        _acc_init(i, accs)
        for h in range(X_HEADS):
            sl = slice(h * LANE, (h + 1) * LANE)
            slv = slice(w + h * LANE, w + (h + 1) * LANE)
            qh, kh, vh, doh = ins[0][:, sl], fulls[0][:, sl], fulls[0][:, slv], ins[1][:, sl]
            p = _xattn_probs(qh, kh)
            dp = _dot_nt(doh, vh)
            ds = (p * (dp - jnp.sum(p * dp, axis=1, keepdims=True)) * (1.0 / math.sqrt(X_HEAD_DIM))).astype(BF16)
            outs[0][:, sl] = _dot(ds, kh).astype(BF16)
            accs[0][:, sl] += _dot_tn(ds, qh)
            accs[0][:, slv] += _dot_tn(p.astype(BF16), doh)

    return _rowwise(body, name="xattn_bwd", rows=rows, tr=512, row_ins=[(xq, w, 0), (dxo, w, 0)], full_ins=[xkv],
                    row_outs=[(w, BF16)], acc_outs=[(xkv.shape, F32)])


def _gate_fwd(proj, pa, pb, b_gate):
    rows = proj.shape[0]

    def body(i, ins, fulls, outs, accs):
        sa = _sigmoid(ins[0][...] + fulls[0][0:1, :])
        sb = _sigmoid(ins[1][...] + fulls[0][1:2, :])
        outs[0][...] = (sa * ins[2][...] + sb * ins[3][...]).astype(BF16)

    return _rowwise(body, name="gate_fwd", rows=rows, tr=512,
                    row_ins=[(proj, D_MODEL, C_GA // D_MODEL), (proj, D_MODEL, C_GB // D_MODEL), (pa, D_MODEL, 0),
                             (pb, D_MODEL, 0)],
                    full_ins=[b_gate], row_outs=[(D_MODEL, BF16)])[0]


def _gate_bwd(proj, pa, pb, b_gate, dm):
    rows = proj.shape[0]

    def body(i, ins, fulls, outs, accs):
        _acc_init(i, accs)
        d = ins[4][...]
        sa = _sigmoid(ins[0][...] + fulls[0][0:1, :])
        sb = _sigmoid(ins[1][...] + fulls[0][1:2, :])
        dga = d * ins[2][...] * sa * (1.0 - sa)
        dgb = d * ins[3][...] * sb * (1.0 - sb)
        outs[0][...] = (d * sa).astype(BF16)
        outs[1][...] = (d * sb).astype(BF16)
        outs[2][...] = dga.astype(BF16)
        outs[3][...] = dgb.astype(BF16)
        accs[0][0:1, :] += jnp.sum(dga, axis=0, keepdims=True)
        accs[0][1:2, :] += jnp.sum(dgb, axis=0, keepdims=True)

    return _rowwise(body, name="gate_bwd", rows=rows, tr=512,
                    row_ins=[(proj, D_MODEL, C_GA // D_MODEL), (proj, D_MODEL, C_GB // D_MODEL), (pa, D_MODEL, 0),
                             (pb, D_MODEL, 0), (dm, D_MODEL, 0)],
                    full_ins=[b_gate], row_outs=[(D_MODEL, BF16)] * 4, acc_outs=[((2, D_MODEL), F32)])


def _loss_head(x3, target, g_final):
    rows = x3.shape[0]

    def body(i, ins, fulls, outs, accs):
        _acc_init(i, accs)
        xv, g = ins[0][...], fulls[0][...]
        d = _rms(xv, g) - ins[1][...]
        dx, dg = _rms_bwd(xv, g, d * (1.0 / D_MODEL))
        outs[0][...] = dx
        accs[0][...] += dg
        accs[1][...] += jnp.sum(d * d, axis=0, keepdims=True)

    return _rowwise(body, name="loss_head", rows=rows, tr=512, row_ins=[(x3, D_MODEL, 0), (target, D_MODEL, 0)],
                    full_ins=[g_final], row_outs=[(D_MODEL, F32)], acc_outs=[((1, D_MODEL), F32), ((1, D_MODEL), F32)])


def _adamw(w, g, m, v, name):
    rows, c = w.shape

    def body(i, ins, fulls, outs, accs):
        wv, gv = ins[0][...], ins[1][...]
        mn = ADAM_B1 * ins[2][...] + (1.0 - ADAM_B1) * gv
        vn = ADAM_B2 * ins[3][...] + (1.0 - ADAM_B2) * jnp.square(gv)
        m_hat = mn / (1.0 - ADAM_B1 ** ADAM_STEP)
        v_hat = vn / (1.0 - ADAM_B2 ** ADAM_STEP)
        outs[0][...] = -ADAM_LR * (m_hat / (jnp.sqrt(v_hat) + ADAM_EPS) + ADAM_WD * wv)
        outs[1][...] = mn
        outs[2][...] = vn

    return _rowwise(body, name=name, rows=rows, tr=_row_tile(rows, 256), row_ins=[(a, c, 0) for a in (w, g, m, v)],
                    row_outs=[(c, F32)] * 3)


def _place():
    x, y, c = lax.axis_index("x"), lax.axis_index("y"), lax.axis_index("c")
    chips = [(1 - x, y), (x, 1 - y), (1 - x, 1 - y)]
    return x, y, c, chips


ANY = pl.BlockSpec(memory_space=pl.ANY)


def _remote(src, dst, send_sem, recv_sem, to):
    return pltpu.make_async_remote_copy(src_ref=src, dst_ref=dst, send_sem=send_sem, recv_sem=recv_sem,
                                        device_id=to, device_id_type=MESH)


def _dma_sems(n):
    return [pltpu.SemaphoreType.DMA((n,)), pltpu.SemaphoreType.DMA((n,))]


def _all_gather_weights(shards):
    n = len(shards)

    def body(*refs):
        ins, outs = refs[:n], refs[n:2 * n]
        send_sems, recv_sems = refs[2 * n:]
        x, y, c, chips = _place()
        me = 2 * x + y

        def copy(t, j, chip_idx, hlf, to, src=None):
            h = ins[t].shape[0] // 2
            dst = outs[t].at[chip_idx, pl.ds(hlf * h, h), :]
            return _remote(dst if src is None else src, dst, send_sems.at[6 * t + j], recv_sems.at[6 * t + j], to)

        first = []
        for t in range(n):
            h = ins[t].shape[0] // 2
            for j, chip in enumerate(chips):
                first.append(copy(t, j, me, c, (*chip, c), src=ins[t].at[pl.ds(c * h, h), :]))
                first[-1].start()
        passed = []
        for t in range(n):
            for j, chip in enumerate(chips):
                idx = 2 * chip[0] + chip[1]
                copy(t, j, idx, c, (x, y, c)).wait_recv()
                passed.append(copy(t, 3 + j, idx, c, (x, y, 1 - c)))
                passed[-1].start()
        for t in range(n):
            for j, chip in enumerate(chips):
                copy(t, 3 + j, 2 * chip[0] + chip[1], 1 - c, (x, y, c)).wait_recv()
        for cp in first + passed:
            cp.wait_send()

    return pl.pallas_call(
        body, name="all_gather_weights", in_specs=[ANY] * n, out_specs=[ANY] * n,
        out_shape=[jax.ShapeDtypeStruct((N_CHIPS,) + s.shape, s.dtype) for s in shards],
        scratch_shapes=_dma_sems(6 * n),
    )(*shards)


def _pair_exchange_grads(stacked):
    n = len(stacked)

    def body(*refs):
        ins, outs = refs[:n], refs[n:2 * n]
        send_sems, recv_sems = refs[2 * n:]
        x, y, c, _ = _place()
        cps = []
        for t in range(n):
            h = ins[t].shape[1] // 2
            cps.append(_remote(ins[t].at[:, pl.ds((1 - c) * h, h), :], outs[t], send_sems.at[t], recv_sems.at[t],
                               (x, y, 1 - c)))
            cps[-1].start()
        for cp in cps:
            cp.wait()

    return pl.pallas_call(
        body, name="pair_exchange_grads", in_specs=[ANY] * n, out_specs=[ANY] * n,
        out_shape=[jax.ShapeDtypeStruct((N_CHIPS, s.shape[1] // 2, s.shape[2]), s.dtype) for s in stacked],
        scratch_shapes=_dma_sems(n),
    )(*stacked)


def _chip_scatter(parts):
    n = len(parts)

    def body(*refs):
        ins, outs = refs[:n], refs[n:2 * n]
        send_sems, recv_sems = refs[2 * n:]
        x, y, c, chips = _place()
        cps = []
        for t in range(n):
            for j, chip in enumerate(chips):
                cps.append(_remote(ins[t].at[2 * chip[0] + chip[1]], outs[t].at[j], send_sems.at[3 * t + j],
                                   recv_sems.at[3 * t + j], (*chip, c)))
                cps[-1].start()
        for cp in cps:
            cp.wait()

    return pl.pallas_call(
        body, name="chip_scatter", in_specs=[ANY] * n, out_specs=[ANY] * n,
        out_shape=[jax.ShapeDtypeStruct((N_CHIPS - 1,) + s.shape[1:], s.dtype) for s in parts],
        scratch_shapes=_dma_sems(3 * n),
    )(*parts)


def _pair_exchange_halves(shards):
    n = len(shards)

    def body(*refs):
        bufs = refs[n:2 * n]
        send_sems, recv_sems = refs[2 * n:]
        x, y, c, _ = _place()
        cps = []
        for t in range(n):
            h = bufs[t].shape[0] // 2
            rows = bufs[t].at[pl.ds(c * h, h), :]
            cps.append(_remote(rows, rows, send_sems.at[t], recv_sems.at[t], (x, y, 1 - c)))
            cps[-1].start()
        for cp in cps:
            cp.wait()

    return pl.pallas_call(
        body, name="pair_exchange_halves", in_specs=[ANY] * n, out_specs=[ANY] * n,
        out_shape=[jax.ShapeDtypeStruct(s.shape, s.dtype) for s in shards],
        input_output_aliases={t: t for t in range(n)},
        scratch_shapes=_dma_sems(n),
    )(*shards)


def _pair_sum(gs, recv, place, name):
    _, r, cols = gs.shape
    h = r // 2

    def kern(p_ref, a_ref, b_ref, o_ref):
        o_ref[...] = (a_ref[...] + b_ref[...]).astype(BF16)

    blk = lambda f: pl.BlockSpec((1, h, cols), f)
    return pl.pallas_call(
        kern, name=name,
        grid_spec=pltpu.PrefetchScalarGridSpec(
            num_scalar_prefetch=1, grid=(N_CHIPS,),
            in_specs=[blk(lambda d, p: (d, p[1], 0)), blk(lambda d, p: (d, 0, 0))],
            out_specs=blk(lambda d, p: (d, 0, 0))),
        out_shape=jax.ShapeDtypeStruct((N_CHIPS, h, cols), BF16),
        compiler_params=pltpu.CompilerParams(dimension_semantics=("arbitrary",),
                                             vmem_limit_bytes=_vmem_limit(3 * _nbytes((h, cols), F32), 0)),
    )(place, gs, recv)


def _chip_sum(gs, recv, got, place, name):
    _, r, cols = gs.shape
    h = r // 2

    def kern(p_ref, a_ref, b_ref, g0, g1, g2, o_ref):
        own = a_ref[0] + b_ref[0]
        o_ref[...] = ((own + g0[0].astype(F32)) + g1[0].astype(F32)) + g2[0].astype(F32)

    blk = lambda f: pl.BlockSpec((1, h, cols), f)
    return pl.pallas_call(
        kern, name=name,
        grid_spec=pltpu.PrefetchScalarGridSpec(
            num_scalar_prefetch=1, grid=(1,),
            in_specs=[blk(lambda i, p: (p[0], p[1], 0)), blk(lambda i, p: (p[0], 0, 0)), blk(lambda i, p: (0, 0, 0)),
                      blk(lambda i, p: (1, 0, 0)), blk(lambda i, p: (2, 0, 0))],
            out_specs=pl.BlockSpec((h, cols), lambda i, p: (p[1], 0))),
        out_shape=jax.ShapeDtypeStruct((r, cols), F32),
        compiler_params=pltpu.CompilerParams(dimension_semantics=("arbitrary",),
                                             vmem_limit_bytes=_vmem_limit(5 * _nbytes((h, cols), F32), 0)),
    )(place, gs, recv, got, got, got)


def _all_reduce_small(vec, name):
    r, cols = vec.shape

    def body(in_ref, out_ref, gath, send_sems, recv_sems):
        x, y, c, _ = _place()
        me = 4 * x + 2 * y + c
        gath[me] = in_ref[...]
        sends = []
        for k in range(1, 8):
            to = (x ^ (k >> 2), y ^ ((k >> 1) & 1), c ^ (k & 1))
            cp = pltpu.make_async_remote_copy(src_ref=in_ref, dst_ref=gath.at[me], send_sem=send_sems.at[k - 1],
                                              recv_sem=recv_sems.at[k - 1], device_id=to, device_id_type=MESH)
            cp.start()
            sends.append(cp)
        for k in range(1, 8):
            peer = me ^ k
            pltpu.make_async_remote_copy(src_ref=in_ref, dst_ref=gath.at[peer], send_sem=send_sems.at[k - 1],
                                         recv_sem=recv_sems.at[k - 1], device_id=(x, y, c),
                                         device_id_type=MESH).wait_recv()
        for cp in sends:
            cp.wait_send()
        acc = gath[0]
        for d in range(1, 8):
            acc = acc + gath[d]
        out_ref[...] = acc

    vm = pl.BlockSpec(memory_space=pltpu.VMEM)
    return pl.pallas_call(
        body, name=name, in_specs=[vm], out_specs=vm,
        out_shape=jax.ShapeDtypeStruct((r, cols), F32),
        scratch_shapes=[pltpu.VMEM((8, r, cols), F32), pltpu.SemaphoreType.DMA((7,)), pltpu.SemaphoreType.DMA((7,))],
    )(vec)


def _pad_heads(w, heads, dim, axis):
    shp = w.shape[:axis] + (heads, dim) + w.shape[axis + 1:]
    pad = [(0, 0)] * len(shp)
    pad[axis + 1] = (0, LANE - dim)
    w = jnp.pad(w.reshape(shp), pad)
    return w.reshape(w.shape[:axis] + (heads * LANE,) + w.shape[axis + 2:])


def _unpad_heads(w, heads, dim, axis):
    shp = w.shape[:axis] + (heads, LANE) + w.shape[axis + 1:]
    w = lax.slice_in_dim(w.reshape(shp), 0, dim, axis=axis + 1)
    return w.reshape(w.shape[:axis] + (heads * dim,) + w.shape[axis + 2:])


def _w_in_layout(w_in):
    kr = jnp.pad(w_in[:, 384:416], ((0, 0), (ROPE_LO, LANE - ROPE_LO - MLA_ROPE)))
    sb = lambda lo: _pad_heads(w_in[:, lo:lo + 512], SB_HEADS, SB_HEAD_DIM, 1)
    return jnp.concatenate([w_in[:, 1952:2976], w_in[:, 2976:4000], sb(416), sb(928), sb(1440), w_in[:, 0:256],
                            w_in[:, 256:384], kr], axis=1)


def _w_in_unlayout(d):
    sb = lambda lo: _unpad_heads(d[:, lo:lo + 1024], SB_HEADS, SB_HEAD_DIM, 1)
    return jnp.concatenate([d[:, C_CQ:C_CQ + 256], d[:, C_CKV:C_CKV + 128], d[:, C_KR + ROPE_LO:C_KR + ROPE_LO + MLA_ROPE],
                            sb(C_SBQ), sb(C_SBK), sb(C_SBV), d[:, C_GA:C_GA + 1024], d[:, C_GB:C_GB + 1024]], axis=1)


def _w_ukv_layout(w):
    w3 = w.reshape(MLA_KV_RANK, MLA_HEADS, MLA_NOPE + MLA_V)
    pad = lambda part: jnp.pad(part, ((0, 0), (0, 0), (0, LANE - part.shape[2]))).reshape(MLA_KV_RANK, MLA_HEADS * LANE)
    return jnp.concatenate([pad(w3[:, :, :MLA_NOPE]), pad(w3[:, :, MLA_NOPE:])], axis=1)


def _w_ukv_unlayout(d):
    hw = MLA_HEADS * LANE
    kpart = d[:, :hw].reshape(MLA_KV_RANK, MLA_HEADS, LANE)[:, :, :MLA_NOPE]
    vpart = d[:, hw:].reshape(MLA_KV_RANK, MLA_HEADS, LANE)[:, :, :MLA_V]
    return jnp.concatenate([kpart, vpart], axis=2).reshape(MLA_KV_RANK, MLA_HEADS * (MLA_NOPE + MLA_V))


def _shard_of(full, d, axis):
    n = full.shape[axis] // N_CHIPS
    return lax.slice_in_dim(full, d * n, (d + 1) * n, axis=axis)


def _local_step(x, mem, pos, target, w, t_mla, t_sb):
    s = x.shape[0]
    win = _w_in_layout(w["w_in"])
    wuq = _pad_heads(w["w_uq"], MLA_HEADS, MLA_NOPE + MLA_ROPE, 1)
    wkv = _w_ukv_layout(w["w_ukv"])
    wa = _pad_heads(w["w_a_proj"], MLA_HEADS, MLA_V, 0)
    wb = _pad_heads(w["w_b_proj"], SB_HEADS, SB_HEAD_DIM, 0)
    inv_freq = ROPE_THETA ** (-jnp.arange(0, MLA_ROPE, 2, dtype=F32) / MLA_ROPE)
    freq_lane = jnp.pad(jnp.concatenate([inv_freq, inv_freq]), (ROPE_LO, LANE - ROPE_LO - MLA_ROPE)).reshape(1, LANE)
    add = lambda accs, ex: (accs[0] + ex[0],)

    tab = _rope_tables(pos.reshape(s, 1), freq_lane)
    h = _rms_fwd_call(x, w["g_mix"], "rms_mix")
    proj = _mm(h, [win], name="proj_in", tn=1408)
    cqn, ckvn, krope = _mla_prep_fwd(proj, tab, w["g_q_lat"], w["g_kv_lat"])
    qp = _mm(cqn, [wuq], name="q_up")
    kvp = _mm(ckvn, [wkv], name="kv_up")
    qa, ka, va = _mla_rope_fwd(qp, kvp, krope, tab)
    o_a, lse = _mla_fwd(qa, ka, va, t_mla)
    o_b = _sb_fwd(proj, t_sb)
    pa = _mm(o_a, [wa], name="proj_a")
    pb = _mm(o_b, [wb], name="proj_b")
    merged = _gate_fwd(proj, pa, pb, w["b_gate"])
    x1 = _mm(merged, [w["w_o"]], name="proj_o", extras=(x,), epilogue=add)
    hx = _rms_fwd_call(x1, w["g_x"], "rms_x")
    mn = _rms_fwd_call(mem, w["g_mem"], "rms_mem")
    xq = _mm(hx, [w["w_xq"]], name="xq", out_dtypes=(BF16,))
    xkv = _mm(mn, [w["w_xkv"]], name="xkv", out_dtypes=(BF16,))
    xo = _xattn_fwd(xq, xkv)
    x2 = _mm(xo, [w["w_xo"]], name="proj_xo", extras=(x1,), epilogue=add)
    hf = _rms_fwd_call(x2, w["g_ffn"], "rms_ffn")

    def swiglu(accs, ex):
        a, b = accs
        return a, b, a * _sigmoid(a) * b

    ga, gu, hmid = _mm(hf, [w["w_gate"], w["w_up"]], name="ffn_up", epilogue=swiglu, out_dtypes=(BF16, BF16, BF16),
                       tm=512, tn=1408)
    x3 = _mm(hmid, [w["w_down"]], name="ffn_down", extras=(x2,), epilogue=add, tk=1408)

    dx3, dg_final, sq = _loss_head(x3, target, w["g_final"].reshape(1, D_MODEL))
    g = {"g_final": dg_final.reshape(D_MODEL)}

    def swiglu_bwd(accs, ex):
        dh, a, b = accs[0], ex[0].astype(F32), ex[1].astype(F32)
        sg = _sigmoid(a)
        return dh * b * sg * (1.0 + a * (1.0 - sg)), dh * a * sg

    da, db = _mm(dx3, [w["w_down"]], name="ffn_down_dx", tb=True, extras=(ga, gu), epilogue=swiglu_bwd,
                 out_dtypes=(BF16, BF16), tm=512, tn=1408)
    g["w_down"] = _mm(hmid, [dx3], name="ffn_down_dw", ta=True, tm=1408)
    g["w_gate"] = _mm(hf, [da], name="ffn_gate_dw", ta=True, tn=1408, tk=512)
    g["w_up"] = _mm(hf, [db], name="ffn_up_dw", ta=True, tn=1408)
    dhf = _mm(da, [w["w_gate"]], name="ffn_gate_dx", tb=True, tk=256)
    dhf = _mm(db, [w["w_up"]], name="ffn_up_dx", tb=True, extras=(dhf,), epilogue=add, tk=1408)
    dx2, g["g_ffn"] = _rms_bwd_call(x2, w["g_ffn"], dhf, dx3, "rms_ffn_bwd")

    dxo = _mm(dx2, [w["w_xo"]], name="proj_xo_dx", tb=True, out_dtypes=(BF16,))
    g["w_xo"] = _mm(xo, [dx2], name="proj_xo_dw", ta=True)
    dxq, dxkv = _xattn_bwd(xq, xkv, dxo)
    dhx = _mm(dxq, [w["w_xq"]], name="xq_dx", tb=True)
    g["w_xq"] = _mm(hx, [dxq], name="xq_dw", ta=True)
    dmn = _mm(dxkv, [w["w_xkv"]], name="xkv_dx", tb=True)
    g["w_xkv"] = _mm(mn, [dxkv], name="xkv_dw", ta=True)
    dx1, g["g_x"] = _rms_bwd_call(x1, w["g_x"], dhx, dx2, "rms_x_bwd")
    _, g["g_mem"] = _rms_bwd_call(mem, w["g_mem"], dmn, None, "rms_mem_bwd")

    dmerged = _mm(dx1, [w["w_o"]], name="proj_o_dx", tb=True)
    g["w_o"] = _mm(merged, [dx1], name="proj_o_dw", ta=True)
    dpa, dpb, dga, dgb, g["b_gate"] = _gate_bwd(proj, pa, pb, w["b_gate"], dmerged)
    do_a = _mm(dpa, [wa], name="proj_a_dx", tb=True, out_dtypes=(BF16,))
    do_b = _mm(dpb, [wb], name="proj_b_dx", tb=True, out_dtypes=(BF16,))
    g["w_a_proj"] = _unpad_heads(_mm(o_a, [dpa], name="proj_a_dw", ta=True), MLA_HEADS, MLA_V, 0)
    g["w_b_proj"] = _unpad_heads(_mm(o_b, [dpb], name="proj_b_dw", ta=True), SB_HEADS, SB_HEAD_DIM, 0)

    dsq, dsk, dsv = _sb_bwd(proj, o_b, do_b, t_sb)
    dqa, dka, dva = _mla_bwd(qa, ka, va, o_a, do_a, lse, t_mla)
    dqp, dkvp, dkr = _mla_rope_bwd(dqa, dka, dva, tab)
    g["w_uq"] = _unpad_heads(_mm(cqn, [dqp], name="q_up_dw", ta=True), MLA_HEADS, MLA_NOPE + MLA_ROPE, 1)
    g["w_ukv"] = _w_ukv_unlayout(_mm(ckvn, [dkvp], name="kv_up_dw", ta=True))
    dcqn = _mm(dqp, [wuq], name="q_up_dx", tb=True)
    dckvn = _mm(dkvp, [wkv], name="kv_up_dx", tb=True)
    dcq, dckv, g["g_q_lat"], g["g_kv_lat"] = _mla_prep_bwd(proj, w["g_q_lat"], w["g_kv_lat"], dcqn, dckvn)

    dproj = jnp.concatenate([dga, dgb, dsq.astype(BF16), dsk.astype(BF16), dsv.astype(BF16), dcq, dckv, dkr], axis=1)
    g["w_in"] = _w_in_unlayout(_mm(h, [dproj], name="proj_in_dw", ta=True, tn=1408))
    dh = _mm(dproj, [win], name="proj_in_dx", tb=True, tk=1408)
    grad_x, g["g_mix"] = _rms_bwd_call(x, w["g_mix"], dh, dx1, "rms_mix_bwd")
    return sq, grad_x, g


def _small_pack(d):
    row5 = jnp.concatenate([d["g_q_lat"].reshape(-1), d["g_kv_lat"].reshape(-1), jnp.zeros((640,), F32)])
    rows = [d[n].reshape(-1) for n in ("g_mix", "g_x", "g_mem", "g_ffn", "g_final")] + [row5]
    return rows


def _small_unpack(p, like):
    out = {n: p[i].reshape(like[n].shape) for i, n in enumerate(("g_mix", "g_x", "g_mem", "g_ffn", "g_final"))}
    out["g_q_lat"] = p[5, 0:256].reshape(like["g_q_lat"].shape)
    out["g_kv_lat"] = p[5, 256:384].reshape(like["g_kv_lat"].shape)
    return out


def kernel(x, mem, positions, g_mix, w_in, b_gate, g_q_lat, w_uq, g_kv_lat, w_ukv, w_a_proj, w_b_proj, w_o, g_x, g_mem, w_xq, w_xkv, w_xo, g_ffn, w_gate, w_up, w_down, g_final, loss_target, m_g_mix, m_w_in, m_b_gate, m_g_q_lat, m_w_uq, m_g_kv_lat, m_w_ukv, m_w_a_proj, m_w_b_proj, m_w_o, m_g_x, m_g_mem, m_w_xq, m_w_xkv, m_w_xo, m_g_ffn, m_w_gate, m_w_up, m_w_down, m_g_final, v_g_mix, v_w_in, v_b_gate, v_g_q_lat, v_w_uq, v_g_kv_lat, v_w_ukv, v_w_a_proj, v_w_b_proj, v_w_o, v_g_x, v_g_mem, v_w_xq, v_w_xkv, v_w_xo, v_g_ffn, v_w_gate, v_w_up, v_w_down, v_g_final):
    given = dict(locals())
    names = [n for n, _, _ in MATS] + ["b_gate"] + list(SMALL)
    wts = {n: given[n] for n in names}
    mom = {n: given["m_" + n] for n in names}
    var = {n: given["v_" + n] for n in names}
    shard2d = {n: shp for n, shp, _ in MATS}
    shard2d["b_gate"] = B_GATE_SHARD
    cx, cy, cc = lax.axis_index("x"), lax.axis_index("y"), lax.axis_index("c")
    me = 2 * cx + cy
    place = jnp.stack([me, cc]).astype(jnp.int32)
    bcol = me * B_GATE_SHARD[1]

    own = [wts[n].reshape(shard2d[n]).astype(BF16) for n, _, _ in MATS]
    full = {}
    for (n, shp, ax), g4, mine in zip(MATS, _all_gather_weights(own), own):
        g4 = lax.dynamic_update_slice(g4, mine[None], (me, 0, 0))
        full[n] = g4.reshape(N_CHIPS * shp[0], shp[1]) if ax == 0 else jnp.concatenate(list(g4), axis=1)
    bias_rows = jnp.pad(wts["b_gate"].reshape(B_GATE_SHARD), ((0, SMALL_ROWS - 2), (0, 0)))
    bias_rows = lax.dynamic_update_slice(jnp.zeros((SMALL_ROWS, D_MODEL), F32), bias_rows, (0, bcol))
    full["b_gate"] = _all_reduce_small(jnp.where(cc == 0, bias_rows, 0.0), "all_gather_bias")[0:2]
    for n in SMALL:
        full[n] = wts[n].reshape(1, -1) if n != "g_final" else wts[n]

    sq, grad_x, grads = _local_step(x[0], mem[0], positions[0], loss_target[0], full, t_mla=512, t_sb=256)

    stacked = [jnp.stack([_shard_of(grads[n], d, ax) for d in range(N_CHIPS)]) for n, _, ax in MATS]
    recv = _pair_exchange_grads(stacked)
    parts = [_pair_sum(gs, rv, place, "pair_sum_" + n) for (n, _, _), gs, rv in zip(MATS, stacked, recv)]
    got = _chip_scatter(parts)
    halves = [_chip_sum(gs, rv, gt, place, "chip_sum_" + n) for (n, _, _), gs, rv, gt in zip(MATS, stacked, recv, got)]
    g_shard = dict(zip([n for n, _, _ in MATS], _pair_exchange_halves(halves)))

    small_rows = _small_pack({n: grads[n] for n in SMALL}) + [sq.reshape(-1), grads["b_gate"][0], grads["b_gate"][1]]
    small_rows += [jnp.zeros((D_MODEL,), F32)] * (SMALL_ROWS - len(small_rows))
    small = _all_reduce_small(jnp.stack(small_rows), "all_reduce_small")
    loss = (0.5 / D_MODEL) * jnp.sum(small[6])
    g_shard["b_gate"] = lax.dynamic_slice(small[7:9], (0, bcol), B_GATE_SHARD)

    out = {"grad": {}, "delta": {}, "m": {}, "v": {}}
    for n in [n for n, _, _ in MATS] + ["b_gate"]:
        shape = wts[n].shape
        r2 = lambda a: a.reshape(shard2d[n])
        d_n, m_n, v_n = _adamw(r2(wts[n]), g_shard[n], r2(mom[n]), r2(var[n]), "adamw_" + n)
        for key, a in (("grad", g_shard[n]), ("delta", d_n), ("m", m_n), ("v", v_n)):
            out[key][n] = a.reshape(shape)
    sp = lambda d: jnp.stack(_small_pack(d) + [jnp.zeros((D_MODEL,), F32)] * 2)
    delta_s, m_s, v_s = _adamw(sp(wts), small[0:8].at[6:8].set(0.0), sp(mom), sp(var), "adamw_small")
    for key, p in (("grad", small), ("delta", delta_s), ("m", m_s), ("v", v_s)):
        out[key].update(_small_unpack(p, wts))

    order = ["g_mix", "w_in", "b_gate", "g_q_lat", "w_uq", "g_kv_lat", "w_ukv", "w_a_proj", "w_b_proj", "w_o", "g_x",
             "g_mem", "w_xq", "w_xkv", "w_xo", "g_ffn", "w_gate", "w_up", "w_down", "g_final"]
    return (loss, grad_x[None], *[out[key][n] for key in ("grad", "delta", "m", "v") for n in order])
```

```python
import functools
import math

import jax
import jax.numpy as jnp
from jax import lax
from jax.experimental import pallas as pl
from jax.experimental.pallas import tpu as pltpu

F32 = jnp.float32
BF16 = jnp.bfloat16
MESH = pl.DeviceIdType.MESH

D_MODEL = 1024
MLA_HEADS = 8
MLA_Q_RANK = 256
MLA_KV_RANK = 128
MLA_NOPE = 64
MLA_ROPE = 32
MLA_V = 64
ROPE_THETA = 10000.0
SB_HEADS = 8
SB_HEAD_DIM = 64
X_HEADS = 4
X_HEAD_DIM = 128
D_FF = 2816
EPS = 1e-6
ADAM_LR = 0.001
ADAM_B1 = 0.9
ADAM_B2 = 0.999
ADAM_EPS = 1e-08
ADAM_WD = 0.01
ADAM_STEP = 10

LANE = 128
LOG2E = 1.4426950408889634
N_CHIPS = 4
VMEM_BYTES = 64 * 1024 * 1024

C_GA, C_GB, C_SBQ, C_SBK, C_SBV, C_CQ, C_CKV, C_KR = 0, 1024, 2048, 3072, 4096, 5120, 5376, 5504
D_INP = 5632
ROPE_LO = MLA_NOPE
HALF = MLA_ROPE // 2

SB_ZERO_LOG = -104.0

MATS = (
    ("w_in", (1024, 1000), 1), ("w_uq", (256, 192), 1), ("w_ukv", (128, 256), 1), ("w_a_proj", (512, 256), 1),
    ("w_b_proj", (512, 256), 1), ("w_o", (256, 1024), 0), ("w_xq", (256, 512), 0), ("w_xkv", (256, 1024), 0),
    ("w_xo", (512, 256), 1), ("w_gate", (1024, 704), 1), ("w_up", (1024, 704), 1), ("w_down", (704, 1024), 0),
)
B_GATE_SHARD = (2, 256)
SMALL = ("g_mix", "g_x", "g_mem", "g_ffn", "g_final", "g_q_lat", "g_kv_lat")
SMALL_ROWS = 16


def _vmem_limit(block_bytes, temp_bytes):
    est = 2 * block_bytes + temp_bytes + (4 << 20)
    return int(min(max(est, 16 << 20), VMEM_BYTES - (6 << 20)))


def _nbytes(shape, dtype):
    return math.prod(shape) * jnp.dtype(dtype).itemsize


def _row_tile(rows, cap):
    if rows <= cap:
        return rows
    return max(t for t in range(8, cap + 1, 8) if rows % t == 0)


def _tile(n, cap):
    if n <= cap:
        return n
    best = None
    for t in range(LANE, cap + 1, LANE):
        if n % t == 0:
            best = t
    assert best is not None, (n, cap)
    return best


def _mm(a, bs, *, name, ta=False, tb=False, extras=(), epilogue=None, out_dtypes=(F32,), tm=1024, tn=1024, tk=1024):
    bs = tuple(bs)
    m, k = (a.shape[1], a.shape[0]) if ta else a.shape
    n = bs[0].shape[0] if tb else bs[0].shape[1]
    tm, tn, tk = _tile(m, tm), _tile(n, tn), _tile(k, tk)
    assert m % tm == 0 and n % tn == 0 and k % tk == 0
    nk = k // tk
    nb, ne, no = len(bs), len(extras), len(out_dtypes)
    dims = (((0,) if ta else (1,)), ((1,) if tb else (0,))), ((), ())
    if epilogue is None:
        epilogue = lambda accs, ex: (accs[0],)

    def body(*refs):
        a_ref, b_refs, e_refs = refs[0], refs[1:1 + nb], refs[1 + nb:1 + nb + ne]
        o_refs, acc_refs = refs[1 + nb + ne:1 + nb + ne + no], refs[1 + nb + ne + no:]
        kk = pl.program_id(2)

        @pl.when(kk == 0)
        def _():
            for acc in acc_refs:
                acc[...] = jnp.zeros_like(acc)

        av = a_ref[...].astype(BF16)
        for b_ref, acc in zip(b_refs, acc_refs):
            acc[...] += lax.dot_general(av, b_ref[...].astype(BF16), dims, preferred_element_type=F32)

        @pl.when(kk == nk - 1)
        def _():
            outs = epilogue([acc[...] for acc in acc_refs], [e[...] for e in e_refs])
            for o_ref, v in zip(o_refs, outs):
                o_ref[...] = v.astype(o_ref.dtype)

    a_spec = pl.BlockSpec((tk, tm), lambda i, j, kk: (kk, i)) if ta else pl.BlockSpec((tm, tk), lambda i, j, kk: (i, kk))
    b_spec = pl.BlockSpec((tn, tk), lambda i, j, kk: (j, kk)) if tb else pl.BlockSpec((tk, tn), lambda i, j, kk: (kk, j))
    mn_spec = pl.BlockSpec((tm, tn), lambda i, j, kk: (i, j))
    blocks = (_nbytes((tm, tk), a.dtype) + sum(_nbytes((tk, tn), b.dtype) for b in bs)
              + sum(_nbytes((tm, tn), e.dtype) for e in extras) + sum(_nbytes((tm, tn), d) for d in out_dtypes))
    temps = (nb + 4) * _nbytes((tm, tn), F32)
    outs = pl.pallas_call(
        body, name=name, grid=(m // tm, n // tn, nk),
        in_specs=[a_spec] + [b_spec] * nb + [mn_spec] * ne,
        out_specs=[mn_spec] * no,
        out_shape=[jax.ShapeDtypeStruct((m, n), d) for d in out_dtypes],
        scratch_shapes=[pltpu.VMEM((tm, tn), F32) for _ in range(nb)],
        compiler_params=pltpu.CompilerParams(
            dimension_semantics=("parallel", "parallel", "arbitrary"),
            vmem_limit_bytes=_vmem_limit(blocks, temps)),
    )(a, *bs, *extras)
    return outs[0] if no == 1 else outs


def _rowwise(body, *, name, rows, tr, row_ins, full_ins=(), row_outs=(), acc_outs=()):
    tr = min(tr, rows)
    assert rows % tr == 0
    n_ri, n_fi, n_ro = len(row_ins), len(full_ins), len(row_outs)

    def kern(*refs):
        body(pl.program_id(0), refs[:n_ri], refs[n_ri:n_ri + n_fi], refs[n_ri + n_fi:n_ri + n_fi + n_ro],
             refs[n_ri + n_fi + n_ro:])

    in_specs = [pl.BlockSpec((tr, w), functools.partial(lambda i, c: (i, c), c=ci)) for _, w, ci in row_ins]
    in_specs += [pl.BlockSpec(f.shape, lambda i: (0, 0)) for f in full_ins]
    out_specs = [pl.BlockSpec((tr, w), lambda i: (i, 0)) for w, _ in row_outs]
    out_specs += [pl.BlockSpec(s, lambda i: (0, 0)) for s, _ in acc_outs]
    out_shape = [jax.ShapeDtypeStruct((rows, w), d) for w, d in row_outs]
    out_shape += [jax.ShapeDtypeStruct(s, d) for s, d in acc_outs]
    blocks = (sum(_nbytes((tr, w), a.dtype) for a, w, _ in row_ins) + sum(_nbytes(f.shape, f.dtype) for f in full_ins)
              + sum(_nbytes((tr, w), d) for w, d in row_outs) + sum(_nbytes(s, d) for s, d in acc_outs))
    widest = max([w for _, w, _ in row_ins] + [w for w, _ in row_outs])
    outs = pl.pallas_call(
        kern, name=name, grid=(rows // tr,), in_specs=in_specs, out_specs=out_specs, out_shape=out_shape,
        compiler_params=pltpu.CompilerParams(
            dimension_semantics=("arbitrary",) if acc_outs else ("parallel",),
            vmem_limit_bytes=_vmem_limit(blocks, 8 * _nbytes((tr, widest), F32))),
    )(*[a for a, _, _ in row_ins], *full_ins)
    return outs


def _rms(x, g):
    r = lax.rsqrt(jnp.mean(x * x, axis=-1, keepdims=True) + EPS)
    return x * r * g


def _rms_bwd(x, g, dy):
    r = lax.rsqrt(jnp.mean(x * x, axis=-1, keepdims=True) + EPS)
    xh = x * r
    dxh = dy * g
    dx = r * (dxh - xh * jnp.mean(dxh * xh, axis=-1, keepdims=True))
    return dx, jnp.sum(dy * xh, axis=0, keepdims=True)


def _sigmoid(x):
    return 1.0 / (1.0 + jnp.exp(-x))


def _acc_init(i, refs):
    @pl.when(i == 0)
    def _():
        for r in refs:
            r[...] = jnp.zeros_like(r)


def _rms_fwd_call(x, g, name):
    rows, c = x.shape

    def body(i, ins, fulls, outs, accs):
        outs[0][...] = _rms(ins[0][...], fulls[0][...]).astype(BF16)

    return _rowwise(body, name=name, rows=rows, tr=512, row_ins=[(x, c, 0)], full_ins=[g], row_outs=[(c, BF16)])[0]


def _rms_bwd_call(x, g, dy, res, name):
    rows, c = x.shape
    row_ins = [(x, c, 0), (dy, c, 0)] + ([(res, c, 0)] if res is not None else [])

    def body(i, ins, fulls, outs, accs):
        _acc_init(i, accs)
        dx, dg = _rms_bwd(ins[0][...], fulls[0][...], ins[1][...].astype(F32))
        if res is not None:
            dx = dx + ins[2][...]
        outs[0][...] = dx
        accs[0][...] += dg

    return _rowwise(body, name=name, rows=rows, tr=512, row_ins=row_ins, full_ins=[g], row_outs=[(c, F32)],
                    acc_outs=[((1, c), F32)])


def _rope_tables(pos_col, freq_lane):
    rows = pos_col.shape[0]

    def body(i, ins, fulls, outs, accs):
        ang = ins[0][...].astype(F32) * fulls[0][...]
        lane = lax.broadcasted_iota(jnp.int32, ang.shape, 1)
        cos, sin = jnp.cos(ang), jnp.sin(ang)
        first = (lane >= ROPE_LO) & (lane < ROPE_LO + HALF)
        second = (lane >= ROPE_LO + HALF) & (lane < ROPE_LO + MLA_ROPE)
        outs[0][:, 0:LANE] = jnp.where(first | second, cos, 1.0)
        outs[0][:, LANE:2 * LANE] = jnp.where(first, -sin, 0.0)
        outs[0][:, 2 * LANE:3 * LANE] = jnp.where(second, sin, 0.0)

    return _rowwise(body, name="rope_tables", rows=rows, tr=1024, row_ins=[(pos_col, 1, 0)], full_ins=[freq_lane],
                    row_outs=[(3 * LANE, F32)])[0]


def _rope(x, tab):
    return (x * tab[:, 0:LANE] + pltpu.roll(x, LANE - HALF, 1) * tab[:, LANE:2 * LANE]
            + pltpu.roll(x, HALF, 1) * tab[:, 2 * LANE:3 * LANE])


def _rope_t(dy, tab):
    return (dy * tab[:, 0:LANE] + pltpu.roll(dy * tab[:, LANE:2 * LANE], HALF, 1)
            + pltpu.roll(dy * tab[:, 2 * LANE:3 * LANE], LANE - HALF, 1))


def _mla_prep_fwd(proj, tab, g_q, g_kv):
    rows = proj.shape[0]

    def body(i, ins, fulls, outs, accs):
        outs[0][...] = _rms(ins[0][...], fulls[0][...]).astype(BF16)
        outs[1][...] = _rms(ins[1][...], fulls[1][...]).astype(BF16)
        outs[2][...] = _rope(ins[2][...], ins[3][...])

    return _rowwise(body, name="mla_prep_fwd", rows=rows, tr=512,
                    row_ins=[(proj, MLA_Q_RANK, C_CQ // MLA_Q_RANK), (proj, LANE, C_CKV // LANE),
                             (proj, LANE, C_KR // LANE), (tab, 3 * LANE, 0)],
                    full_ins=[g_q, g_kv], row_outs=[(MLA_Q_RANK, BF16), (MLA_KV_RANK, BF16), (LANE, F32)])


def _mla_prep_bwd(proj, g_q, g_kv, dcqn, dckvn):
    rows = proj.shape[0]

    def body(i, ins, fulls, outs, accs):
        _acc_init(i, accs)
        dcq, dgq = _rms_bwd(ins[0][...], fulls[0][...], ins[2][...])
        dckv, dgkv = _rms_bwd(ins[1][...], fulls[1][...], ins[3][...])
        outs[0][...] = dcq.astype(BF16)
        outs[1][...] = dckv.astype(BF16)
        accs[0][...] += dgq
        accs[1][...] += dgkv

    return _rowwise(body, name="mla_prep_bwd", rows=rows, tr=512,
                    row_ins=[(proj, MLA_Q_RANK, C_CQ // MLA_Q_RANK), (proj, LANE, C_CKV // LANE),
                             (dcqn, MLA_Q_RANK, 0), (dckvn, MLA_KV_RANK, 0)],
                    full_ins=[g_q, g_kv], row_outs=[(MLA_Q_RANK, BF16), (MLA_KV_RANK, BF16)],
                    acc_outs=[((1, MLA_Q_RANK), F32), ((1, MLA_KV_RANK), F32)])


def _mla_rope_fwd(qp, kvp, krope, tab):
    rows = qp.shape[0]
    hw = MLA_HEADS * LANE

    def body(i, ins, fulls, outs, accs):
        t = ins[3][...]
        kr = ins[2][...]
        for h in range(MLA_HEADS):
            sl = slice(h * LANE, (h + 1) * LANE)
            outs[0][:, sl] = _rope(ins[0][:, sl], t).astype(BF16)
            outs[1][:, sl] = (ins[1][:, sl] + kr).astype(BF16)
        outs[2][...] = ins[1][:, hw:2 * hw].astype(BF16)

    return _rowwise(body, name="mla_rope_fwd", rows=rows, tr=512,
                    row_ins=[(qp, hw, 0), (kvp, 2 * hw, 0), (krope, LANE, 0), (tab, 3 * LANE, 0)],
                    row_outs=[(hw, BF16), (hw, BF16), (hw, BF16)])


def _mla_rope_bwd(dq, dk, dv, tab):
    rows = dq.shape[0]
    hw = MLA_HEADS * LANE

    def body(i, ins, fulls, outs, accs):
        t = ins[3][...]
        dkr = jnp.zeros((ins[0].shape[0], LANE), F32)
        for h in range(MLA_HEADS):
            sl = slice(h * LANE, (h + 1) * LANE)
            outs[0][:, sl] = _rope_t(ins[0][:, sl], t).astype(BF16)
            dkr = dkr + ins[1][:, sl]
        outs[1][:, 0:hw] = ins[1][...].astype(BF16)
        outs[1][:, hw:2 * hw] = ins[2][...].astype(BF16)
        lane = lax.broadcasted_iota(jnp.int32, dkr.shape, 1)
        dkr = jnp.where((lane >= ROPE_LO) & (lane < ROPE_LO + MLA_ROPE), dkr, 0.0)
        outs[2][...] = _rope_t(dkr, t).astype(BF16)

    return _rowwise(body, name="mla_rope_bwd", rows=rows, tr=512,
                    row_ins=[(dq, hw, 0), (dk, hw, 0), (dv, hw, 0), (tab, 3 * LANE, 0)],
                    row_outs=[(hw, BF16), (2 * hw, BF16), (LANE, BF16)])


def _dot_nt(a, b):
    return lax.dot_general(a, b, (((1,), (1,)), ((), ())), preferred_element_type=F32)


def _dot_tn(a, b):
    return lax.dot_general(a, b, (((0,), (0,)), ((), ())), preferred_element_type=F32)


def _dot(a, b):
    return jnp.dot(a, b, preferred_element_type=F32)


def _attn_params(s, t, n_res_f32, n_res_bf16):
    blocks = n_res_f32 * _nbytes((s, LANE), F32) + n_res_bf16 * _nbytes((s, LANE), BF16) + 6 * _nbytes((t, LANE), F32)
    return pltpu.CompilerParams(dimension_semantics=("parallel", "arbitrary"),
                                vmem_limit_bytes=_vmem_limit(blocks, 12 * _nbytes((t, t), F32)))


def _mla_fwd(q, k, v, t):
    s, hw = q.shape
    heads, nq = hw // LANE, s // t
    scale = 1.0 / math.sqrt(MLA_NOPE + MLA_ROPE)
    scale2 = scale * LOG2E

    def body(q_ref, k_ref, v_ref, o_ref, l_ref):
        i = pl.program_id(1)
        qv = q_ref[...]

        def step(j, carry, masked):
            m, l, acc = carry
            sl = pl.ds(pl.multiple_of(j * t, t), t)
            sc = _dot_nt(qv, k_ref[sl, :])
            if masked:
                row = lax.broadcasted_iota(jnp.int32, (t, t), 0)
                col = lax.broadcasted_iota(jnp.int32, (t, t), 1)
                sc = jnp.where(col <= row, sc, -1e30)
            m_new = jnp.maximum(m, jnp.max(sc, axis=1, keepdims=True))
            p = jnp.exp2((sc - m_new) * scale2)
            alpha = jnp.exp2((m - m_new) * scale2)
            l = alpha * l + jnp.sum(p, axis=1, keepdims=True)
            acc = alpha * acc + _dot(p.astype(BF16), v_ref[sl, :])
            return m_new, l, acc

        init = (jnp.full((t, 1), -1e30, F32), jnp.zeros((t, 1), F32), jnp.zeros((t, LANE), F32))
        carry = lax.fori_loop(0, i, lambda j, c: step(j, c, False), init)
        m, l, acc = step(i, carry, True)
        o_ref[...] = (acc / l).astype(o_ref.dtype)
        l_ref[0] = m * scale2 + jnp.log2(l)

    blk = pl.BlockSpec((t, LANE), lambda h, i: (i, h))
    res = pl.BlockSpec((s, LANE), lambda h, i: (0, h))
    return pl.pallas_call(
        body, name="mla_fwd", grid=(heads, nq), in_specs=[blk, res, res],
        out_specs=[blk, pl.BlockSpec((1, t, 1), lambda h, i: (h, i, 0))],
        out_shape=[jax.ShapeDtypeStruct((s, hw), BF16), jax.ShapeDtypeStruct((heads, s, 1), F32)],
        compiler_params=_attn_params(s, t, 0, 2),
    )(q, k, v)


def _mla_bwd(q, k, v, o, do, lse, t):
    s, hw = q.shape
    heads, nq = hw // LANE, s // t
    scale = 1.0 / math.sqrt(MLA_NOPE + MLA_ROPE)
    scale2 = scale * LOG2E

    def body(q_ref, k_ref, v_ref, o_ref, do_ref, l_ref, dq_ref, dk_ref, dv_ref):
        i = pl.program_id(1)

        @pl.when(i == 0)
        def _():
            dk_ref[...] = jnp.zeros_like(dk_ref)
            dv_ref[...] = jnp.zeros_like(dv_ref)

        qv, dov, lv = q_ref[...], do_ref[...], l_ref[0]
        dlt = jnp.sum(dov.astype(F32) * o_ref[...].astype(F32), axis=1, keepdims=True)

        def step(j, dq, masked):
            sl = pl.ds(pl.multiple_of(j * t, t), t)
            kv, vv = k_ref[sl, :], v_ref[sl, :]
            p = jnp.exp2(_dot_nt(qv, kv) * scale2 - lv)
            if masked:
                row = lax.broadcasted_iota(jnp.int32, (t, t), 0)
                col = lax.broadcasted_iota(jnp.int32, (t, t), 1)
                p = jnp.where(col <= row, p, 0.0)
            ds = (p * (_dot_nt(dov, vv) - dlt)).astype(BF16)
            dk_ref[sl, :] += _dot_tn(ds, qv) * scale
            dv_ref[sl, :] += _dot_tn(p.astype(BF16), dov)
            return dq + _dot(ds, kv)

        dq = lax.fori_loop(0, i, lambda j, c: step(j, c, False), jnp.zeros((t, LANE), F32))
        dq_ref[...] = step(i, dq, True) * scale

    blk = pl.BlockSpec((t, LANE), lambda h, i: (i, h))
    res = pl.BlockSpec((s, LANE), lambda h, i: (0, h))
    full = jax.ShapeDtypeStruct((s, hw), F32)
    return pl.pallas_call(
        body, name="mla_bwd", grid=(heads, nq),
        in_specs=[blk, res, res, blk, blk, pl.BlockSpec((1, t, 1), lambda h, i: (h, i, 0))],
        out_specs=[blk, res, res], out_shape=[full, full, full],
        compiler_params=_attn_params(s, t, 2, 2),
    )(q, k, v, o, do, lse)


def _sb_logits(qv, kv, scale, masked, t, upper):
    z = _dot_nt(qv, kv) * scale
    e = jnp.exp(-jnp.abs(z))
    l1p = jnp.log(1.0 + e)
    lb = jnp.minimum(z, 0.0) - l1p
    lo = -jnp.maximum(z, 0.0) - l1p
    keep = None
    if masked:
        row = lax.broadcasted_iota(jnp.int32, (t, t), 0)
        col = lax.broadcasted_iota(jnp.int32, (t, t), 1)
        keep = col < row
        lo = jnp.where(keep, lo, 0.0)
    hi = lo.astype(BF16)
    rem = (lo - hi.astype(F32)).astype(BF16)
    suf = _dot(hi, upper) + _dot(rem, upper)
    return z, e, lb, lo, suf, keep


def _tri(t, inclusive):
    row = lax.broadcasted_iota(jnp.int32, (t, t), 0)
    col = lax.broadcasted_iota(jnp.int32, (t, t), 1)
    return jnp.where((row >= col) if inclusive else (row > col), 1.0, 0.0).astype(BF16)


def _sb_fwd(proj, t):
    s = proj.shape[0]
    heads, nq = SB_HEADS, s // t
    scale = 1.0 / math.sqrt(SB_HEAD_DIM)

    def body(q_ref, k_ref, v_ref, o_ref):
        i = pl.program_id(1)
        qv = q_ref[...].astype(BF16)
        upper = _tri(t, False)

        def step(jj, carry, masked):
            c, acc = carry
            sl = pl.ds(pl.multiple_of((i - jj) * t, t), t)
            _, _, lb, lo, suf, keep = _sb_logits(qv, k_ref[sl, :].astype(BF16), scale, masked, t, upper)
            a = jnp.exp(lb + suf + c)
            if masked:
                a = jnp.where(keep, a, 0.0)
            acc = acc + _dot(a.astype(BF16), v_ref[sl, :].astype(BF16))
            return c + jnp.sum(lo, axis=1, keepdims=True), acc

        carry = step(0, (jnp.zeros((t, 1), F32), jnp.zeros((t, LANE), F32)), True)
        def live(st):
            return (st[0] <= i) & (jnp.max(st[1]) >= SB_ZERO_LOG)

        def more(st):
            return (st[0] + 1,) + step(st[0], st[1:], False)

        o_ref[...] = lax.while_loop(live, more, (jnp.int32(1),) + carry)[2]

    return pl.pallas_call(
        body, name="sb_fwd", grid=(heads, nq),
        in_specs=[pl.BlockSpec((t, LANE), lambda h, i: (i, C_SBQ // LANE + h)),
                  pl.BlockSpec((s, LANE), lambda h, i: (0, C_SBK // LANE + h)),
                  pl.BlockSpec((s, LANE), lambda h, i: (0, C_SBV // LANE + h))],
        out_specs=pl.BlockSpec((t, LANE), lambda h, i: (i, h)),
        out_shape=jax.ShapeDtypeStruct((s, heads * LANE), F32),
        compiler_params=_attn_params(s, t, 2, 0),
    )(proj, proj, proj)


def _sb_bwd(proj, o, do, t):
    s = proj.shape[0]
    heads, nq = SB_HEADS, s // t
    scale = 1.0 / math.sqrt(SB_HEAD_DIM)

    def body(q_ref, k_ref, v_ref, o_ref, do_ref, dq_ref, dk_ref, dv_ref):
        i = pl.program_id(1)

        @pl.when(i == 0)
        def _():
            dk_ref[...] = jnp.zeros_like(dk_ref)
            dv_ref[...] = jnp.zeros_like(dv_ref)

        qv, dov = q_ref[...].astype(BF16), do_ref[...]
        total = jnp.sum(dov.astype(F32) * o_ref[...], axis=1, keepdims=True)
        upper, upper_incl = _tri(t, False), _tri(t, True)

        def step(jj, carry, masked):
            c, g, dq = carry
            sl = pl.ds(pl.multiple_of((i - jj) * t, t), t)
            kv, vv = k_ref[sl, :].astype(BF16), v_ref[sl, :].astype(BF16)
            z, e, lb, lo, suf, keep = _sb_logits(qv, kv, scale, masked, t, upper)
            tail = suf + c
            a = jnp.exp(lb + tail)
            if masked:
                a = jnp.where(keep, a, 0.0)
            ab = a.astype(BF16)
            gr = ab.astype(F32) * _dot_nt(dov, vv)
            ghi = gr.astype(BF16)
            grem = (gr - ghi.astype(F32)).astype(BF16)
            before = total - g - (_dot(ghi, upper_incl) + _dot(grem, upper_incl))
            before = jnp.where(tail < SB_ZERO_LOG, 0.0, before)
            r = 1.0 / (1.0 + e)
            pos = z >= 0.0
            dz = r * (gr * jnp.where(pos, e, 1.0) - before * jnp.where(pos, 1.0, e))
            if masked:
                dz = jnp.where(keep, dz, 0.0)
            dzb = (dz * scale).astype(BF16)
            dk_ref[sl, :] += _dot_tn(dzb, qv)
            dv_ref[sl, :] += _dot_tn(ab, dov)
            return (c + jnp.sum(lo, axis=1, keepdims=True), g + jnp.sum(gr, axis=1, keepdims=True),
                    dq + _dot(dzb, kv))

        zero = jnp.zeros((t, 1), F32)
        carry = step(0, (zero, zero, jnp.zeros((t, LANE), F32)), True)
        def live(st):
            return (st[0] <= i) & (jnp.max(st[1]) >= SB_ZERO_LOG)

        def more(st):
            return (st[0] + 1,) + step(st[0], st[1:], False)

        dq_ref[...] = lax.while_loop(live, more, (jnp.int32(1),) + carry)[3]

    blk = pl.BlockSpec((t, LANE), lambda h, i: (i, h))
    res = pl.BlockSpec((s, LANE), lambda h, i: (0, h))
    full = jax.ShapeDtypeStruct((s, heads * LANE), F32)
    return pl.pallas_call(
        body, name="sb_bwd", grid=(heads, nq),
        in_specs=[pl.BlockSpec((t, LANE), lambda h, i: (i, C_SBQ // LANE + h)),
                  pl.BlockSpec((s, LANE), lambda h, i: (0, C_SBK // LANE + h)),
                  pl.BlockSpec((s, LANE), lambda h, i: (0, C_SBV // LANE + h)), blk, blk],
        out_specs=[blk, res, res], out_shape=[full, full, full],
        compiler_params=_attn_params(s, t, 4, 0),
    )(proj, proj, proj, o, do)


def _xattn_probs(qh, kh):
    sc = _dot_nt(qh, kh) * (1.0 / math.sqrt(X_HEAD_DIM))
    p = jnp.exp(sc - jnp.max(sc, axis=1, keepdims=True))
    return p / jnp.sum(p, axis=1, keepdims=True)


def _xattn_fwd(xq, xkv):
    rows = xq.shape[0]
    w = X_HEADS * X_HEAD_DIM

    def body(i, ins, fulls, outs, accs):
        for h in range(X_HEADS):
            sl = slice(h * LANE, (h + 1) * LANE)
            p = _xattn_probs(ins[0][:, sl], fulls[0][:, sl])
            outs[0][:, sl] = _dot(p.astype(BF16), fulls[0][:, w + h * LANE:w + (h + 1) * LANE]).astype(BF16)

    return _rowwise(body, name="xattn_fwd", rows=rows, tr=512, row_ins=[(xq, w, 0)], full_ins=[xkv],
                    row_outs=[(w, BF16)])[0]


def _xattn_bwd(xq, xkv, dxo):
    rows = xq.shape[0]
    w = X_HEADS * X_HEAD_DIM

    def body(i, ins, fulls, outs, accs):
        _acc_init(i, accs)
        for h in range(X_HEADS):
            sl = slice(h * LANE, (h + 1) * LANE)
            slv = slice(w + h * LANE, w + (h + 1) * LANE)
            qh, kh, vh, doh = ins[0][:, sl], fulls[0][:, sl], fulls[0][:, slv], ins[1][:, sl]
            p = _xattn_probs(qh, kh)
            dp = _dot_nt(doh, vh)
            ds = (p * (dp - jnp.sum(p * dp, axis=1, keepdims=True)) * (1.0 / math.sqrt(X_HEAD_DIM))).astype(BF16)
            outs[0][:, sl] = _dot(ds, kh).astype(BF16)
            accs[0][:, sl] += _dot_tn(ds, qh)
            accs[0][:, slv] += _dot_tn(p.astype(BF16), doh)

    return _rowwise(body, name="xattn_bwd", rows=rows, tr=512, row_ins=[(xq, w, 0), (dxo, w, 0)], full_ins=[xkv],
                    row_outs=[(w, BF16)], acc_outs=[(xkv.shape, F32)])


def _gate_fwd(proj, pa, pb, b_gate):
    rows = proj.shape[0]

    def body(i, ins, fulls, outs, accs):
        sa = _sigmoid(ins[0][...] + fulls[0][0:1, :])
        sb = _sigmoid(ins[1][...] + fulls[0][1:2, :])
        outs[0][...] = (sa * ins[2][...] + sb * ins[3][...]).astype(BF16)

    return _rowwise(body, name="gate_fwd", rows=rows, tr=512,
                    row_ins=[(proj, D_MODEL, C_GA // D_MODEL), (proj, D_MODEL, C_GB // D_MODEL), (pa, D_MODEL, 0),
                             (pb, D_MODEL, 0)],
                    full_ins=[b_gate], row_outs=[(D_MODEL, BF16)])[0]


def _gate_bwd(proj, pa, pb, b_gate, dm):
    rows = proj.shape[0]

    def body(i, ins, fulls, outs, accs):
        _acc_init(i, accs)
        d = ins[4][...]
        sa = _sigmoid(ins[0][...] + fulls[0][0:1, :])
        sb = _sigmoid(ins[1][...] + fulls[0][1:2, :])
        dga = d * ins[2][...] * sa * (1.0 - sa)
        dgb = d * ins[3][...] * sb * (1.0 - sb)
        outs[0][...] = (d * sa).astype(BF16)
        outs[1][...] = (d * sb).astype(BF16)
        outs[2][...] = dga.astype(BF16)
        outs[3][...] = dgb.astype(BF16)
        accs[0][0:1, :] += jnp.sum(dga, axis=0, keepdims=True)
        accs[0][1:2, :] += jnp.sum(dgb, axis=0, keepdims=True)

    return _rowwise(body, name="gate_bwd", rows=rows, tr=512,
                    row_ins=[(proj, D_MODEL, C_GA // D_MODEL), (proj, D_MODEL, C_GB // D_MODEL), (pa, D_MODEL, 0),
                             (pb, D_MODEL, 0), (dm, D_MODEL, 0)],
                    full_ins=[b_gate], row_outs=[(D_MODEL, BF16)] * 4, acc_outs=[((2, D_MODEL), F32)])


def _loss_head(x3, target, g_final):
    rows = x3.shape[0]

    def body(i, ins, fulls, outs, accs):
        _acc_init(i, accs)
        xv, g = ins[0][...], fulls[0][...]
        d = _rms(xv, g) - ins[1][...]
        dx, dg = _rms_bwd(xv, g, d * (1.0 / D_MODEL))
        outs[0][...] = dx
        accs[0][...] += dg
        accs[1][...] += jnp.sum(d * d, axis=0, keepdims=True)

    return _rowwise(body, name="loss_head", rows=rows, tr=512, row_ins=[(x3, D_MODEL, 0), (target, D_MODEL, 0)],
                    full_ins=[g_final], row_outs=[(D_MODEL, F32)], acc_outs=[((1, D_MODEL), F32), ((1, D_MODEL), F32)])


def _adamw(w, g, m, v, name):
    rows, c = w.shape

    def body(i, ins, fulls, outs, accs):
        wv, gv = ins[0][...], ins[1][...]
        mn = ADAM_B1 * ins[2][...] + (1.0 - ADAM_B1) * gv
        vn = ADAM_B2 * ins[3][...] + (1.0 - ADAM_B2) * jnp.square(gv)
        m_hat = mn / (1.0 - ADAM_B1 ** ADAM_STEP)
        v_hat = vn / (1.0 - ADAM_B2 ** ADAM_STEP)
        outs[0][...] = -ADAM_LR * (m_hat / (jnp.sqrt(v_hat) + ADAM_EPS) + ADAM_WD * wv)
        outs[1][...] = mn
        outs[2][...] = vn

    return _rowwise(body, name=name, rows=rows, tr=_row_tile(rows, 256), row_ins=[(a, c, 0) for a in (w, g, m, v)],
                    row_outs=[(c, F32)] * 3)


def _place():
    x, y, c = lax.axis_index("x"), lax.axis_index("y"), lax.axis_index("c")
    chips = [(1 - x, y), (x, 1 - y), (1 - x, 1 - y)]
    return x, y, c, chips


ANY = pl.BlockSpec(memory_space=pl.ANY)


def _remote(src, dst, send_sem, recv_sem, to):
    return pltpu.make_async_remote_copy(src_ref=src, dst_ref=dst, send_sem=send_sem, recv_sem=recv_sem,
                                        device_id=to, device_id_type=MESH)


def _dma_sems(n):
    return [pltpu.SemaphoreType.DMA((n,)), pltpu.SemaphoreType.DMA((n,))]


def _all_gather_weights(shards):
    n = len(shards)

    def body(*refs):
        ins, outs = refs[:n], refs[n:2 * n]
        send_sems, recv_sems = refs[2 * n:]
        x, y, c, chips = _place()
        me = 2 * x + y

        def copy(t, j, chip_idx, hlf, to, src=None):
            h = ins[t].shape[0] // 2
            dst = outs[t].at[chip_idx, pl.ds(hlf * h, h), :]
            return _remote(dst if src is None else src, dst, send_sems.at[6 * t + j], recv_sems.at[6 * t + j], to)

        first = []
        for t in range(n):
            h = ins[t].shape[0] // 2
            for j, chip in enumerate(chips):
                first.append(copy(t, j, me, c, (*chip, c), src=ins[t].at[pl.ds(c * h, h), :]))
                first[-1].start()
        passed = []
        for t in range(n):
            for j, chip in enumerate(chips):
                idx = 2 * chip[0] + chip[1]
                copy(t, j, idx, c, (x, y, c)).wait_recv()
                passed.append(copy(t, 3 + j, idx, c, (x, y, 1 - c)))
                passed[-1].start()
        for t in range(n):
            for j, chip in enumerate(chips):
                copy(t, 3 + j, 2 * chip[0] + chip[1], 1 - c, (x, y, c)).wait_recv()
        for cp in first + passed:
            cp.wait_send()

    return pl.pallas_call(
        body, name="all_gather_weights", in_specs=[ANY] * n, out_specs=[ANY] * n,
        out_shape=[jax.ShapeDtypeStruct((N_CHIPS,) + s.shape, s.dtype) for s in shards],
        scratch_shapes=_dma_sems(6 * n),
    )(*shards)


def _pair_exchange_grads(stacked):
    n = len(stacked)

    def body(*refs):
        ins, outs = refs[:n], refs[n:2 * n]
        send_sems, recv_sems = refs[2 * n:]
        x, y, c, _ = _place()
        cps = []
        for t in range(n):
            h = ins[t].shape[1] // 2
            cps.append(_remote(ins[t].at[:, pl.ds((1 - c) * h, h), :], outs[t], send_sems.at[t], recv_sems.at[t],
                               (x, y, 1 - c)))
            cps[-1].start()
        for cp in cps:
            cp.wait()

    return pl.pallas_call(
        body, name="pair_exchange_grads", in_specs=[ANY] * n, out_specs=[ANY] * n,
        out_shape=[jax.ShapeDtypeStruct((N_CHIPS, s.shape[1] // 2, s.shape[2]), s.dtype) for s in stacked],
        scratch_shapes=_dma_sems(n),
    )(*stacked)


def _chip_scatter(parts):
    n = len(parts)

    def body(*refs):
        ins, outs = refs[:n], refs[n:2 * n]
        send_sems, recv_sems = refs[2 * n:]
        x, y, c, chips = _place()
        cps = []
        for t in range(n):
            for j, chip in enumerate(chips):
                cps.append(_remote(ins[t].at[2 * chip[0] + chip[1]], outs[t].at[j], send_sems.at[3 * t + j],
                                   recv_sems.at[3 * t + j], (*chip, c)))
                cps[-1].start()
        for cp in cps:
            cp.wait()

    return pl.pallas_call(
        body, name="chip_scatter", in_specs=[ANY] * n, out_specs=[ANY] * n,
        out_shape=[jax.ShapeDtypeStruct((N_CHIPS - 1,) + s.shape[1:], s.dtype) for s in parts],
        scratch_shapes=_dma_sems(3 * n),
    )(*parts)


def _pair_exchange_halves(shards):
    n = len(shards)

    def body(*refs):
        bufs = refs[n:2 * n]
        send_sems, recv_sems = refs[2 * n:]
        x, y, c, _ = _place()
        cps = []
        for t in range(n):
            h = bufs[t].shape[0] // 2
            rows = bufs[t].at[pl.ds(c * h, h), :]
            cps.append(_remote(rows, rows, send_sems.at[t], recv_sems.at[t], (x, y, 1 - c)))
            cps[-1].start()
        for cp in cps:
            cp.wait()

    return pl.pallas_call(
        body, name="pair_exchange_halves", in_specs=[ANY] * n, out_specs=[ANY] * n,
        out_shape=[jax.ShapeDtypeStruct(s.shape, s.dtype) for s in shards],
        input_output_aliases={t: t for t in range(n)},
        scratch_shapes=_dma_sems(n),
    )(*shards)


def _pair_sum(gs, recv, place, name):
    _, r, cols = gs.shape
    h = r // 2

    def kern(p_ref, a_ref, b_ref, o_ref):
        o_ref[...] = (a_ref[...] + b_ref[...]).astype(BF16)

    blk = lambda f: pl.BlockSpec((1, h, cols), f)
    return pl.pallas_call(
        kern, name=name,
        grid_spec=pltpu.PrefetchScalarGridSpec(
            num_scalar_prefetch=1, grid=(N_CHIPS,),
            in_specs=[blk(lambda d, p: (d, p[1], 0)), blk(lambda d, p: (d, 0, 0))],
            out_specs=blk(lambda d, p: (d, 0, 0))),
        out_shape=jax.ShapeDtypeStruct((N_CHIPS, h, cols), BF16),
        compiler_params=pltpu.CompilerParams(dimension_semantics=("arbitrary",),
                                             vmem_limit_bytes=_vmem_limit(3 * _nbytes((h, cols), F32), 0)),
    )(place, gs, recv)


def _chip_sum(gs, recv, got, place, name):
    _, r, cols = gs.shape
    h = r // 2

    def kern(p_ref, a_ref, b_ref, g0, g1, g2, o_ref):
        own = a_ref[0] + b_ref[0]
        o_ref[...] = ((own + g0[0].astype(F32)) + g1[0].astype(F32)) + g2[0].astype(F32)

    blk = lambda f: pl.BlockSpec((1, h, cols), f)
    return pl.pallas_call(
        kern, name=name,
        grid_spec=pltpu.PrefetchScalarGridSpec(
            num_scalar_prefetch=1, grid=(1,),
            in_specs=[blk(lambda i, p: (p[0], p[1], 0)), blk(lambda i, p: (p[0], 0, 0)), blk(lambda i, p: (0, 0, 0)),
                      blk(lambda i, p: (1, 0, 0)), blk(lambda i, p: (2, 0, 0))],
            out_specs=pl.BlockSpec((h, cols), lambda i, p: (p[1], 0))),
        out_shape=jax.ShapeDtypeStruct((r, cols), F32),
        compiler_params=pltpu.CompilerParams(dimension_semantics=("arbitrary",),
                                             vmem_limit_bytes=_vmem_limit(5 * _nbytes((h, cols), F32), 0)),
    )(place, gs, recv, got, got, got)


def _all_reduce_small(vec, name):
    r, cols = vec.shape

    def body(in_ref, out_ref, gath, send_sems, recv_sems):
        x, y, c, _ = _place()
        me = 4 * x + 2 * y + c
        gath[me] = in_ref[...]
        sends = []
        for k in range(1, 8):
            to = (x ^ (k >> 2), y ^ ((k >> 1) & 1), c ^ (k & 1))
            cp = pltpu.make_async_remote_copy(src_ref=in_ref, dst_ref=gath.at[me], send_sem=send_sems.at[k - 1],
                                              recv_sem=recv_sems.at[k - 1], device_id=to, device_id_type=MESH)
            cp.start()
            sends.append(cp)
        for k in range(1, 8):
            peer = me ^ k
            pltpu.make_async_remote_copy(src_ref=in_ref, dst_ref=gath.at[peer], send_sem=send_sems.at[k - 1],
                                         recv_sem=recv_sems.at[k - 1], device_id=(x, y, c),
                                         device_id_type=MESH).wait_recv()
        for cp in sends:
            cp.wait_send()
        acc = gath[0]
        for d in range(1, 8):
            acc = acc + gath[d]
        out_ref[...] = acc

    vm = pl.BlockSpec(memory_space=pltpu.VMEM)
    return pl.pallas_call(
        body, name=name, in_specs=[vm], out_specs=vm,
        out_shape=jax.ShapeDtypeStruct((r, cols), F32),
        scratch_shapes=[pltpu.VMEM((8, r, cols), F32), pltpu.SemaphoreType.DMA((7,)), pltpu.SemaphoreType.DMA((7,))],
    )(vec)


def _pad_heads(w, heads, dim, axis):
    shp = w.shape[:axis] + (heads, dim) + w.shape[axis + 1:]
    pad = [(0, 0)] * len(shp)
    pad[axis + 1] = (0, LANE - dim)
    w = jnp.pad(w.reshape(shp), pad)
    return w.reshape(w.shape[:axis] + (heads * LANE,) + w.shape[axis + 2:])


def _unpad_heads(w, heads, dim, axis):
    shp = w.shape[:axis] + (heads, LANE) + w.shape[axis + 1:]
    w = lax.slice_in_dim(w.reshape(shp), 0, dim, axis=axis + 1)
    return w.reshape(w.shape[:axis] + (heads * dim,) + w.shape[axis + 2:])


def _w_in_layout(w_in):
    kr = jnp.pad(w_in[:, 384:416], ((0, 0), (ROPE_LO, LANE - ROPE_LO - MLA_ROPE)))
    sb = lambda lo: _pad_heads(w_in[:, lo:lo + 512], SB_HEADS, SB_HEAD_DIM, 1)
    return jnp.concatenate([w_in[:, 1952:2976], w_in[:, 2976:4000], sb(416), sb(928), sb(1440), w_in[:, 0:256],
                            w_in[:, 256:384], kr], axis=1)


def _w_in_unlayout(d):
    sb = lambda lo: _unpad_heads(d[:, lo:lo + 1024], SB_HEADS, SB_HEAD_DIM, 1)
    return jnp.concatenate([d[:, C_CQ:C_CQ + 256], d[:, C_CKV:C_CKV + 128], d[:, C_KR + ROPE_LO:C_KR + ROPE_LO + MLA_ROPE],
                            sb(C_SBQ), sb(C_SBK), sb(C_SBV), d[:, C_GA:C_GA + 1024], d[:, C_GB:C_GB + 1024]], axis=1)


def _w_ukv_layout(w):
    w3 = w.reshape(MLA_KV_RANK, MLA_HEADS, MLA_NOPE + MLA_V)
    pad = lambda part: jnp.pad(part, ((0, 0), (0, 0), (0, LANE - part.shape[2]))).reshape(MLA_KV_RANK, MLA_HEADS * LANE)
    return jnp.concatenate([pad(w3[:, :, :MLA_NOPE]), pad(w3[:, :, MLA_NOPE:])], axis=1)


def _w_ukv_unlayout(d):
    hw = MLA_HEADS * LANE
    kpart = d[:, :hw].reshape(MLA_KV_RANK, MLA_HEADS, LANE)[:, :, :MLA_NOPE]
    vpart = d[:, hw:].reshape(MLA_KV_RANK, MLA_HEADS, LANE)[:, :, :MLA_V]
    return jnp.concatenate([kpart, vpart], axis=2).reshape(MLA_KV_RANK, MLA_HEADS * (MLA_NOPE + MLA_V))


def _shard_of(full, d, axis):
    n = full.shape[axis] // N_CHIPS
    return lax.slice_in_dim(full, d * n, (d + 1) * n, axis=axis)


def _local_step(x, mem, pos, target, w, t_mla, t_sb):
    s = x.shape[0]
    win = _w_in_layout(w["w_in"])
    wuq = _pad_heads(w["w_uq"], MLA_HEADS, MLA_NOPE + MLA_ROPE, 1)
    wkv = _w_ukv_layout(w["w_ukv"])
    wa = _pad_heads(w["w_a_proj"], MLA_HEADS, MLA_V, 0)
    wb = _pad_heads(w["w_b_proj"], SB_HEADS, SB_HEAD_DIM, 0)
    inv_freq = ROPE_THETA ** (-jnp.arange(0, MLA_ROPE, 2, dtype=F32) / MLA_ROPE)
    freq_lane = jnp.pad(jnp.concatenate([inv_freq, inv_freq]), (ROPE_LO, LANE - ROPE_LO - MLA_ROPE)).reshape(1, LANE)
    add = lambda accs, ex: (accs[0] + ex[0],)

    tab = _rope_tables(pos.reshape(s, 1), freq_lane)
    h = _rms_fwd_call(x, w["g_mix"], "rms_mix")
    proj = _mm(h, [win], name="proj_in", tn=1408)
    cqn, ckvn, krope = _mla_prep_fwd(proj, tab, w["g_q_lat"], w["g_kv_lat"])
    qp = _mm(cqn, [wuq], name="q_up")
    kvp = _mm(ckvn, [wkv], name="kv_up")
    qa, ka, va = _mla_rope_fwd(qp, kvp, krope, tab)
    o_a, lse = _mla_fwd(qa, ka, va, t_mla)
    o_b = _sb_fwd(proj, t_sb)
    pa = _mm(o_a, [wa], name="proj_a")
    pb = _mm(o_b, [wb], name="proj_b")
    merged = _gate_fwd(proj, pa, pb, w["b_gate"])
    x1 = _mm(merged, [w["w_o"]], name="proj_o", extras=(x,), epilogue=add)
    hx = _rms_fwd_call(x1, w["g_x"], "rms_x")
    mn = _rms_fwd_call(mem, w["g_mem"], "rms_mem")
    xq = _mm(hx, [w["w_xq"]], name="xq", out_dtypes=(BF16,))
    xkv = _mm(mn, [w["w_xkv"]], name="xkv", out_dtypes=(BF16,))
    xo = _xattn_fwd(xq, xkv)
    x2 = _mm(xo, [w["w_xo"]], name="proj_xo", extras=(x1,), epilogue=add)
    hf = _rms_fwd_call(x2, w["g_ffn"], "rms_ffn")

    def swiglu(accs, ex):
        a, b = accs
        return a, b, a * _sigmoid(a) * b

    ga, gu, hmid = _mm(hf, [w["w_gate"], w["w_up"]], name="ffn_up", epilogue=swiglu, out_dtypes=(BF16, BF16, BF16),
                       tm=512, tn=1408)
    x3 = _mm(hmid, [w["w_down"]], name="ffn_down", extras=(x2,), epilogue=add, tk=1408)

    dx3, dg_final, sq = _loss_head(x3, target, w["g_final"].reshape(1, D_MODEL))
    g = {"g_final": dg_final.reshape(D_MODEL)}

    def swiglu_bwd(accs, ex):
        dh, a, b = accs[0], ex[0].astype(F32), ex[1].astype(F32)
        sg = _sigmoid(a)
        return dh * b * sg * (1.0 + a * (1.0 - sg)), dh * a * sg

    da, db = _mm(dx3, [w["w_down"]], name="ffn_down_dx", tb=True, extras=(ga, gu), epilogue=swiglu_bwd,
                 out_dtypes=(BF16, BF16), tm=512, tn=1408)
    g["w_down"] = _mm(hmid, [dx3], name="ffn_down_dw", ta=True, tm=1408)
    g["w_gate"] = _mm(hf, [da], name="ffn_gate_dw", ta=True, tn=1408, tk=2048)
    g["w_up"] = _mm(hf, [db], name="ffn_up_dw", ta=True, tn=1408)
    dhf = _mm(da, [w["w_gate"]], name="ffn_gate_dx", tb=True, tk=1408)
    dhf = _mm(db, [w["w_up"]], name="ffn_up_dx", tb=True, extras=(dhf,), epilogue=add, tk=1408)
    dx2, g["g_ffn"] = _rms_bwd_call(x2, w["g_ffn"], dhf, dx3, "rms_ffn_bwd")

    dxo = _mm(dx2, [w["w_xo"]], name="proj_xo_dx", tb=True, out_dtypes=(BF16,))
    g["w_xo"] = _mm(xo, [dx2], name="proj_xo_dw", ta=True)
    dxq, dxkv = _xattn_bwd(xq, xkv, dxo)
    dhx = _mm(dxq, [w["w_xq"]], name="xq_dx", tb=True)
    g["w_xq"] = _mm(hx, [dxq], name="xq_dw", ta=True)
    dmn = _mm(dxkv, [w["w_xkv"]], name="xkv_dx", tb=True)
    g["w_xkv"] = _mm(mn, [dxkv], name="xkv_dw", ta=True)
    dx1, g["g_x"] = _rms_bwd_call(x1, w["g_x"], dhx, dx2, "rms_x_bwd")
    _, g["g_mem"] = _rms_bwd_call(mem, w["g_mem"], dmn, None, "rms_mem_bwd")

    dmerged = _mm(dx1, [w["w_o"]], name="proj_o_dx", tb=True)
    g["w_o"] = _mm(merged, [dx1], name="proj_o_dw", ta=True)
    dpa, dpb, dga, dgb, g["b_gate"] = _gate_bwd(proj, pa, pb, w["b_gate"], dmerged)
    do_a = _mm(dpa, [wa], name="proj_a_dx", tb=True, out_dtypes=(BF16,))
    do_b = _mm(dpb, [wb], name="proj_b_dx", tb=True, out_dtypes=(BF16,))
    g["w_a_proj"] = _unpad_heads(_mm(o_a, [dpa], name="proj_a_dw", ta=True), MLA_HEADS, MLA_V, 0)
    g["w_b_proj"] = _unpad_heads(_mm(o_b, [dpb], name="proj_b_dw", ta=True), SB_HEADS, SB_HEAD_DIM, 0)

    dsq, dsk, dsv = _sb_bwd(proj, o_b, do_b, t_sb)
    dqa, dka, dva = _mla_bwd(qa, ka, va, o_a, do_a, lse, t_mla)
    dqp, dkvp, dkr = _mla_rope_bwd(dqa, dka, dva, tab)
    g["w_uq"] = _unpad_heads(_mm(cqn, [dqp], name="q_up_dw", ta=True), MLA_HEADS, MLA_NOPE + MLA_ROPE, 1)
    g["w_ukv"] = _w_ukv_unlayout(_mm(ckvn, [dkvp], name="kv_up_dw", ta=True))
    dcqn = _mm(dqp, [wuq], name="q_up_dx", tb=True)
    dckvn = _mm(dkvp, [wkv], name="kv_up_dx", tb=True)
    dcq, dckv, g["g_q_lat"], g["g_kv_lat"] = _mla_prep_bwd(proj, w["g_q_lat"], w["g_kv_lat"], dcqn, dckvn)

    dproj = jnp.concatenate([dga, dgb, dsq.astype(BF16), dsk.astype(BF16), dsv.astype(BF16), dcq, dckv, dkr], axis=1)
    g["w_in"] = _w_in_unlayout(_mm(h, [dproj], name="proj_in_dw", ta=True, tn=1408))
    dh = _mm(dproj, [win], name="proj_in_dx", tb=True, tk=1408)
    grad_x, g["g_mix"] = _rms_bwd_call(x, w["g_mix"], dh, dx1, "rms_mix_bwd")
    return sq, grad_x, g


def _small_pack(d):
    row5 = jnp.concatenate([d["g_q_lat"].reshape(-1), d["g_kv_lat"].reshape(-1), jnp.zeros((640,), F32)])
    rows = [d[n].reshape(-1) for n in ("g_mix", "g_x", "g_mem", "g_ffn", "g_final")] + [row5]
    return rows


def _small_unpack(p, like):
    out = {n: p[i].reshape(like[n].shape) for i, n in enumerate(("g_mix", "g_x", "g_mem", "g_ffn", "g_final"))}
    out["g_q_lat"] = p[5, 0:256].reshape(like["g_q_lat"].shape)
    out["g_kv_lat"] = p[5, 256:384].reshape(like["g_kv_lat"].shape)
    return out


def kernel(x, mem, positions, g_mix, w_in, b_gate, g_q_lat, w_uq, g_kv_lat, w_ukv, w_a_proj, w_b_proj, w_o, g_x, g_mem, w_xq, w_xkv, w_xo, g_ffn, w_gate, w_up, w_down, g_final, loss_target, m_g_mix, m_w_in, m_b_gate, m_g_q_lat, m_w_uq, m_g_kv_lat, m_w_ukv, m_w_a_proj, m_w_b_proj, m_w_o, m_g_x, m_g_mem, m_w_xq, m_w_xkv, m_w_xo, m_g_ffn, m_w_gate, m_w_up, m_w_down, m_g_final, v_g_mix, v_w_in, v_b_gate, v_g_q_lat, v_w_uq, v_g_kv_lat, v_w_ukv, v_w_a_proj, v_w_b_proj, v_w_o, v_g_x, v_g_mem, v_w_xq, v_w_xkv, v_w_xo, v_g_ffn, v_w_gate, v_w_up, v_w_down, v_g_final):
    given = dict(locals())
    names = [n for n, _, _ in MATS] + ["b_gate"] + list(SMALL)
    wts = {n: given[n] for n in names}
    mom = {n: given["m_" + n] for n in names}
    var = {n: given["v_" + n] for n in names}
    shard2d = {n: shp for n, shp, _ in MATS}
    shard2d["b_gate"] = B_GATE_SHARD
    cx, cy, cc = lax.axis_index("x"), lax.axis_index("y"), lax.axis_index("c")
    me = 2 * cx + cy
    place = jnp.stack([me, cc]).astype(jnp.int32)
    bcol = me * B_GATE_SHARD[1]

    own = [wts[n].reshape(shard2d[n]).astype(BF16) for n, _, _ in MATS]
    full = {}
    for (n, shp, ax), g4, mine in zip(MATS, _all_gather_weights(own), own):
        g4 = lax.dynamic_update_slice(g4, mine[None], (me, 0, 0))
        full[n] = g4.reshape(N_CHIPS * shp[0], shp[1]) if ax == 0 else jnp.concatenate(list(g4), axis=1)
    bias_rows = jnp.pad(wts["b_gate"].reshape(B_GATE_SHARD), ((0, SMALL_ROWS - 2), (0, 0)))
    bias_rows = lax.dynamic_update_slice(jnp.zeros((SMALL_ROWS, D_MODEL), F32), bias_rows, (0, bcol))
    full["b_gate"] = _all_reduce_small(jnp.where(cc == 0, bias_rows, 0.0), "all_gather_bias")[0:2]
    for n in SMALL:
        full[n] = wts[n].reshape(1, -1) if n != "g_final" else wts[n]

    sq, grad_x, grads = _local_step(x[0], mem[0], positions[0], loss_target[0], full, t_mla=1024, t_sb=256)

    stacked = [jnp.stack([_shard_of(grads[n], d, ax) for d in range(N_CHIPS)]) for n, _, ax in MATS]
    recv = _pair_exchange_grads(stacked)
    parts = [_pair_sum(gs, rv, place, "pair_sum_" + n) for (n, _, _), gs, rv in zip(MATS, stacked, recv)]
    got = _chip_scatter(parts)
    halves = [_chip_sum(gs, rv, gt, place, "chip_sum_" + n) for (n, _, _), gs, rv, gt in zip(MATS, stacked, recv, got)]
    g_shard = dict(zip([n for n, _, _ in MATS], _pair_exchange_halves(halves)))

    small_rows = _small_pack({n: grads[n] for n in SMALL}) + [sq.reshape(-1), grads["b_gate"][0], grads["b_gate"][1]]
    small_rows += [jnp.zeros((D_MODEL,), F32)] * (SMALL_ROWS - len(small_rows))
    small = _all_reduce_small(jnp.stack(small_rows), "all_reduce_small")
    loss = (0.5 / D_MODEL) * jnp.sum(small[6])
    g_shard["b_gate"] = lax.dynamic_slice(small[7:9], (0, bcol), B_GATE_SHARD)

    out = {"grad": {}, "delta": {}, "m": {}, "v": {}}
    for n in [n for n, _, _ in MATS] + ["b_gate"]:
        shape = wts[n].shape
        r2 = lambda a: a.reshape(shard2d[n])
        d_n, m_n, v_n = _adamw(r2(wts[n]), g_shard[n], r2(mom[n]), r2(var[n]), "adamw_" + n)
        for key, a in (("grad", g_shard[n]), ("delta", d_n), ("m", m_n), ("v", v_n)):
            out[key][n] = a.reshape(shape)
    sp = lambda d: jnp.stack(_small_pack(d) + [jnp.zeros((D_MODEL,), F32)] * 2)
    delta_s, m_s, v_s = _adamw(sp(wts), small[0:8].at[6:8].set(0.0), sp(mom), sp(var), "adamw_small")
    for key, p in (("grad", small), ("delta", delta_s), ("m", m_s), ("v", v_s)):
        out[key].update(_small_unpack(p, wts))

    order = ["g_mix", "w_in", "b_gate", "g_q_lat", "w_uq", "g_kv_lat", "w_ukv", "w_a_proj", "w_b_proj", "w_o", "g_x",
             "g_mem", "w_xq", "w_xkv", "w_xo", "g_ffn", "w_gate", "w_up", "w_down", "g_final"]
    return (loss, grad_x[None], *[out[key][n] for key in ("grad", "delta", "m", "v") for n in order])
```

```python
import functools
import math

import jax
import jax.numpy as jnp
from jax import lax
from jax.experimental import pallas as pl
from jax.experimental.pallas import tpu as pltpu

F32 = jnp.float32
BF16 = jnp.bfloat16
MESH = pl.DeviceIdType.MESH

D_MODEL = 1024
MLA_HEADS = 8
MLA_Q_RANK = 256
MLA_KV_RANK = 128
MLA_NOPE = 64
MLA_ROPE = 32
MLA_V = 64
ROPE_THETA = 10000.0
SB_HEADS = 8
SB_HEAD_DIM = 64
X_HEADS = 4
X_HEAD_DIM = 128
D_FF = 2816
EPS = 1e-6
ADAM_LR = 0.001
ADAM_B1 = 0.9
ADAM_B2 = 0.999
ADAM_EPS = 1e-08
ADAM_WD = 0.01
ADAM_STEP = 10

LANE = 128
LOG2E = 1.4426950408889634
N_CHIPS = 4
VMEM_BYTES = 64 * 1024 * 1024

C_GA, C_GB, C_SBQ, C_SBK, C_SBV, C_CQ, C_CKV, C_KR = 0, 1024, 2048, 3072, 4096, 5120, 5376, 5504
D_INP = 5632
ROPE_LO = MLA_NOPE
HALF = MLA_ROPE // 2

SB_ZERO_LOG = -104.0

MATS = (
    ("w_in", (1024, 1000), 1), ("w_uq", (256, 192), 1), ("w_ukv", (128, 256), 1), ("w_a_proj", (512, 256), 1),
    ("w_b_proj", (512, 256), 1), ("w_o", (256, 1024), 0), ("w_xq", (256, 512), 0), ("w_xkv", (256, 1024), 0),
    ("w_xo", (512, 256), 1), ("w_gate", (1024, 704), 1), ("w_up", (1024, 704), 1), ("w_down", (704, 1024), 0),
)
B_GATE_SHARD = (2, 256)
SMALL = ("g_mix", "g_x", "g_mem", "g_ffn", "g_final", "g_q_lat", "g_kv_lat")
SMALL_ROWS = 16


def _vmem_limit(block_bytes, temp_bytes):
    est = 2 * block_bytes + temp_bytes + (4 << 20)
    return int(min(max(est, 16 << 20), VMEM_BYTES - (6 << 20)))


def _nbytes(shape, dtype):
    return math.prod(shape) * jnp.dtype(dtype).itemsize


def _row_tile(rows, cap):
    if rows <= cap:
        return rows
    return max(t for t in range(8, cap + 1, 8) if rows % t == 0)


def _tile(n, cap):
    if n <= cap:
        return n
    best = None
    for t in range(LANE, cap + 1, LANE):
        if n % t == 0:
            best = t
    assert best is not None, (n, cap)
    return best


def _mm(a, bs, *, name, ta=False, tb=False, extras=(), epilogue=None, out_dtypes=(F32,), tm=1024, tn=1024, tk=1024):
    bs = tuple(bs)
    m, k = (a.shape[1], a.shape[0]) if ta else a.shape
    n = bs[0].shape[0] if tb else bs[0].shape[1]
    tm, tn, tk = _tile(m, tm), _tile(n, tn), _tile(k, tk)
    assert m % tm == 0 and n % tn == 0 and k % tk == 0
    nk = k // tk
    nb, ne, no = len(bs), len(extras), len(out_dtypes)
    dims = (((0,) if ta else (1,)), ((1,) if tb else (0,))), ((), ())
    if epilogue is None:
        epilogue = lambda accs, ex: (accs[0],)

    def body(*refs):
        a_ref, b_refs, e_refs = refs[0], refs[1:1 + nb], refs[1 + nb:1 + nb + ne]
        o_refs, acc_refs = refs[1 + nb + ne:1 + nb + ne + no], refs[1 + nb + ne + no:]
        if nk == 1:
            av = a_ref[...].astype(BF16)
            accs = [lax.dot_general(av, b_ref[...].astype(BF16), dims, preferred_element_type=F32) for b_ref in b_refs]
            for o_ref, v in zip(o_refs, epilogue(accs, [e[...] for e in e_refs])):
                o_ref[...] = v.astype(o_ref.dtype)
            return
        kk = pl.program_id(2)

        @pl.when(kk == 0)
        def _():
            for acc in acc_refs:
                acc[...] = jnp.zeros_like(acc)

        av = a_ref[...].astype(BF16)
        for b_ref, acc in zip(b_refs, acc_refs):
            acc[...] += lax.dot_general(av, b_ref[...].astype(BF16), dims, preferred_element_type=F32)

        @pl.when(kk == nk - 1)
        def _():
            outs = epilogue([acc[...] for acc in acc_refs], [e[...] for e in e_refs])
            for o_ref, v in zip(o_refs, outs):
                o_ref[...] = v.astype(o_ref.dtype)

    a_spec = pl.BlockSpec((tk, tm), lambda i, j, kk: (kk, i)) if ta else pl.BlockSpec((tm, tk), lambda i, j, kk: (i, kk))
    b_spec = pl.BlockSpec((tn, tk), lambda i, j, kk: (j, kk)) if tb else pl.BlockSpec((tk, tn), lambda i, j, kk: (kk, j))
    mn_spec = pl.BlockSpec((tm, tn), lambda i, j, kk: (i, j))
    blocks = (_nbytes((tm, tk), a.dtype) + sum(_nbytes((tk, tn), b.dtype) for b in bs)
              + sum(_nbytes((tm, tn), e.dtype) for e in extras) + sum(_nbytes((tm, tn), d) for d in out_dtypes))
    temps = (nb + 4) * _nbytes((tm, tn), F32)
    outs = pl.pallas_call(
        body, name=name, grid=(m // tm, n // tn, nk),
        in_specs=[a_spec] + [b_spec] * nb + [mn_spec] * ne,
        out_specs=[mn_spec] * no,
        out_shape=[jax.ShapeDtypeStruct((m, n), d) for d in out_dtypes],
        scratch_shapes=[pltpu.VMEM((tm, tn), F32) for _ in range(nb if nk > 1 else 0)],
        compiler_params=pltpu.CompilerParams(
            dimension_semantics=("parallel", "parallel", "arbitrary"),
            vmem_limit_bytes=_vmem_limit(blocks, temps)),
    )(a, *bs, *extras)
    return outs[0] if no == 1 else outs


def _rowwise(body, *, name, rows, tr, row_ins, full_ins=(), row_outs=(), acc_outs=()):
    tr = min(tr, rows)
    assert rows % tr == 0
    n_ri, n_fi, n_ro = len(row_ins), len(full_ins), len(row_outs)

    def kern(*refs):
        body(pl.program_id(0), refs[:n_ri], refs[n_ri:n_ri + n_fi], refs[n_ri + n_fi:n_ri + n_fi + n_ro],
             refs[n_ri + n_fi + n_ro:])

    in_specs = [pl.BlockSpec((tr, w), functools.partial(lambda i, c: (i, c), c=ci)) for _, w, ci in row_ins]
    in_specs += [pl.BlockSpec(f.shape, lambda i: (0, 0)) for f in full_ins]
    out_specs = [pl.BlockSpec((tr, w), lambda i: (i, 0)) for w, _ in row_outs]
    out_specs += [pl.BlockSpec(s, lambda i: (0, 0)) for s, _ in acc_outs]
    out_shape = [jax.ShapeDtypeStruct((rows, w), d) for w, d in row_outs]
    out_shape += [jax.ShapeDtypeStruct(s, d) for s, d in acc_outs]
    blocks = (sum(_nbytes((tr, w), a.dtype) for a, w, _ in row_ins) + sum(_nbytes(f.shape, f.dtype) for f in full_ins)
              + sum(_nbytes((tr, w), d) for w, d in row_outs) + sum(_nbytes(s, d) for s, d in acc_outs))
    widest = max([w for _, w, _ in row_ins] + [w for w, _ in row_outs])
    outs = pl.pallas_call(
        kern, name=name, grid=(rows // tr,), in_specs=in_specs, out_specs=out_specs, out_shape=out_shape,
        compiler_params=pltpu.CompilerParams(
            dimension_semantics=("arbitrary",) if acc_outs else ("parallel",),
            vmem_limit_bytes=_vmem_limit(blocks, 8 * _nbytes((tr, widest), F32))),
    )(*[a for a, _, _ in row_ins], *full_ins)
    return outs


def _rms(x, g):
    r = lax.rsqrt(jnp.mean(x * x, axis=-1, keepdims=True) + EPS)
    return x * r * g


def _rms_bwd(x, g, dy):
    r = lax.rsqrt(jnp.mean(x * x, axis=-1, keepdims=True) + EPS)
    xh = x * r
    dxh = dy * g
    dx = r * (dxh - xh * jnp.mean(dxh * xh, axis=-1, keepdims=True))
    return dx, jnp.sum(dy * xh, axis=0, keepdims=True)


def _sigmoid(x):
    return 1.0 / (1.0 + jnp.exp(-x))


def _acc_init(i, refs):
    @pl.when(i == 0)
    def _():
        for r in refs:
            r[...] = jnp.zeros_like(r)


def _rms_fwd_call(x, g, name):
    rows, c = x.shape

    def body(i, ins, fulls, outs, accs):
        outs[0][...] = _rms(ins[0][...], fulls[0][...]).astype(BF16)

    return _rowwise(body, name=name, rows=rows, tr=512, row_ins=[(x, c, 0)], full_ins=[g], row_outs=[(c, BF16)])[0]


def _rms_bwd_call(x, g, dy, res, name):
    rows, c = x.shape
    row_ins = [(x, c, 0), (dy, c, 0)] + ([(res, c, 0)] if res is not None else [])

    def body(i, ins, fulls, outs, accs):
        _acc_init(i, accs)
        dx, dg = _rms_bwd(ins[0][...], fulls[0][...], ins[1][...].astype(F32))
        if res is not None:
            dx = dx + ins[2][...]
        outs[0][...] = dx
        accs[0][...] += dg

    return _rowwise(body, name=name, rows=rows, tr=512, row_ins=row_ins, full_ins=[g], row_outs=[(c, F32)],
                    acc_outs=[((1, c), F32)])


def _rope_tables(pos_col, freq_lane):
    rows = pos_col.shape[0]

    def body(i, ins, fulls, outs, accs):
        ang = ins[0][...].astype(F32) * fulls[0][...]
        lane = lax.broadcasted_iota(jnp.int32, ang.shape, 1)
        cos, sin = jnp.cos(ang), jnp.sin(ang)
        first = (lane >= ROPE_LO) & (lane < ROPE_LO + HALF)
        second = (lane >= ROPE_LO + HALF) & (lane < ROPE_LO + MLA_ROPE)
        outs[0][:, 0:LANE] = jnp.where(first | second, cos, 1.0)
        outs[0][:, LANE:2 * LANE] = jnp.where(first, -sin, 0.0)
        outs[0][:, 2 * LANE:3 * LANE] = jnp.where(second, sin, 0.0)

    return _rowwise(body, name="rope_tables", rows=rows, tr=1024, row_ins=[(pos_col, 1, 0)], full_ins=[freq_lane],
                    row_outs=[(3 * LANE, F32)])[0]


def _rope(x, tab):
    return (x * tab[:, 0:LANE] + pltpu.roll(x, LANE - HALF, 1) * tab[:, LANE:2 * LANE]
            + pltpu.roll(x, HALF, 1) * tab[:, 2 * LANE:3 * LANE])


def _rope_t(dy, tab):
    return (dy * tab[:, 0:LANE] + pltpu.roll(dy * tab[:, LANE:2 * LANE], HALF, 1)
            + pltpu.roll(dy * tab[:, 2 * LANE:3 * LANE], LANE - HALF, 1))


def _mla_prep_fwd(proj, tab, g_q, g_kv):
    rows = proj.shape[0]

    def body(i, ins, fulls, outs, accs):
        outs[0][...] = _rms(ins[0][...], fulls[0][...]).astype(BF16)
        outs[1][...] = _rms(ins[1][...], fulls[1][...]).astype(BF16)
        outs[2][...] = _rope(ins[2][...], ins[3][...])

    return _rowwise(body, name="mla_prep_fwd", rows=rows, tr=512,
                    row_ins=[(proj, MLA_Q_RANK, C_CQ // MLA_Q_RANK), (proj, LANE, C_CKV // LANE),
                             (proj, LANE, C_KR // LANE), (tab, 3 * LANE, 0)],
                    full_ins=[g_q, g_kv], row_outs=[(MLA_Q_RANK, BF16), (MLA_KV_RANK, BF16), (LANE, F32)])


def _mla_prep_bwd(proj, g_q, g_kv, dcqn, dckvn):
    rows = proj.shape[0]

    def body(i, ins, fulls, outs, accs):
        _acc_init(i, accs)
        dcq, dgq = _rms_bwd(ins[0][...], fulls[0][...], ins[2][...])
        dckv, dgkv = _rms_bwd(ins[1][...], fulls[1][...], ins[3][...])
        outs[0][...] = dcq.astype(BF16)
        outs[1][...] = dckv.astype(BF16)
        accs[0][...] += dgq
        accs[1][...] += dgkv

    return _rowwise(body, name="mla_prep_bwd", rows=rows, tr=512,
                    row_ins=[(proj, MLA_Q_RANK, C_CQ // MLA_Q_RANK), (proj, LANE, C_CKV // LANE),
                             (dcqn, MLA_Q_RANK, 0), (dckvn, MLA_KV_RANK, 0)],
                    full_ins=[g_q, g_kv], row_outs=[(MLA_Q_RANK, BF16), (MLA_KV_RANK, BF16)],
                    acc_outs=[((1, MLA_Q_RANK), F32), ((1, MLA_KV_RANK), F32)])


def _mla_rope_fwd(qp, kvp, krope, tab):
    rows = qp.shape[0]
    hw = MLA_HEADS * LANE

    def body(i, ins, fulls, outs, accs):
        t = ins[3][...]
        kr = ins[2][...]
        for h in range(MLA_HEADS):
            sl = slice(h * LANE, (h + 1) * LANE)
            outs[0][:, sl] = _rope(ins[0][:, sl], t).astype(BF16)
            outs[1][:, sl] = (ins[1][:, sl] + kr).astype(BF16)
        outs[2][...] = ins[1][:, hw:2 * hw].astype(BF16)

    return _rowwise(body, name="mla_rope_fwd", rows=rows, tr=512,
                    row_ins=[(qp, hw, 0), (kvp, 2 * hw, 0), (krope, LANE, 0), (tab, 3 * LANE, 0)],
                    row_outs=[(hw, BF16), (hw, BF16), (hw, BF16)])


def _mla_rope_bwd(dq, dk, dv, tab):
    rows = dq.shape[0]
    hw = MLA_HEADS * LANE

    def body(i, ins, fulls, outs, accs):
        t = ins[3][...]
        dkr = jnp.zeros((ins[0].shape[0], LANE), F32)
        for h in range(MLA_HEADS):
            sl = slice(h * LANE, (h + 1) * LANE)
            outs[0][:, sl] = _rope_t(ins[0][:, sl], t).astype(BF16)
            dkr = dkr + ins[1][:, sl]
        outs[1][:, 0:hw] = ins[1][...].astype(BF16)
        outs[1][:, hw:2 * hw] = ins[2][...].astype(BF16)
        lane = lax.broadcasted_iota(jnp.int32, dkr.shape, 1)
        dkr = jnp.where((lane >= ROPE_LO) & (lane < ROPE_LO + MLA_ROPE), dkr, 0.0)
        outs[2][...] = _rope_t(dkr, t).astype(BF16)

    return _rowwise(body, name="mla_rope_bwd", rows=rows, tr=512,
                    row_ins=[(dq, hw, 0), (dk, hw, 0), (dv, hw, 0), (tab, 3 * LANE, 0)],
                    row_outs=[(hw, BF16), (2 * hw, BF16), (LANE, BF16)])


def _dot_nt(a, b):
    return lax.dot_general(a, b, (((1,), (1,)), ((), ())), preferred_element_type=F32)


def _dot_tn(a, b):
    return lax.dot_general(a, b, (((0,), (0,)), ((), ())), preferred_element_type=F32)


def _dot(a, b):
    return jnp.dot(a, b, preferred_element_type=F32)


def _attn_params(s, t, n_res_f32, n_res_bf16):
    blocks = n_res_f32 * _nbytes((s, LANE), F32) + n_res_bf16 * _nbytes((s, LANE), BF16) + 6 * _nbytes((t, LANE), F32)
    return pltpu.CompilerParams(dimension_semantics=("parallel", "arbitrary"),
                                vmem_limit_bytes=_vmem_limit(blocks, 12 * _nbytes((t, t), F32)))


def _mla_fwd(q, k, v, t):
    s, hw = q.shape
    heads, nq = hw // LANE, s // t
    scale = 1.0 / math.sqrt(MLA_NOPE + MLA_ROPE)
    scale2 = scale * LOG2E

    def body(q_ref, k_ref, v_ref, o_ref, l_ref):
        i = pl.program_id(1)
        qv = q_ref[...]

        def step(j, carry, masked):
            m, l, acc = carry
            sl = pl.ds(pl.multiple_of(j * t, t), t)
            sc = _dot_nt(qv, k_ref[sl, :])
            if masked:
                row = lax.broadcasted_iota(jnp.int32, (t, t), 0)
                col = lax.broadcasted_iota(jnp.int32, (t, t), 1)
                sc = jnp.where(col <= row, sc, -1e30)
            m_new = jnp.maximum(m, jnp.max(sc, axis=1, keepdims=True))
            p = jnp.exp2((sc - m_new) * scale2)
            alpha = jnp.exp2((m - m_new) * scale2)
            l = alpha * l + jnp.sum(p, axis=1, keepdims=True)
            acc = alpha * acc + _dot(p.astype(BF16), v_ref[sl, :])
            return m_new, l, acc

        init = (jnp.full((t, 1), -1e30, F32), jnp.zeros((t, 1), F32), jnp.zeros((t, LANE), F32))
        carry = lax.fori_loop(0, i, lambda j, c: step(j, c, False), init)
        m, l, acc = step(i, carry, True)
        o_ref[...] = (acc / l).astype(o_ref.dtype)
        l_ref[0] = m * scale2 + jnp.log2(l)

    blk = pl.BlockSpec((t, LANE), lambda h, i: (i, h))
    res = pl.BlockSpec((s, LANE), lambda h, i: (0, h))
    return pl.pallas_call(
        body, name="mla_fwd", grid=(heads, nq), in_specs=[blk, res, res],
        out_specs=[blk, pl.BlockSpec((1, t, 1), lambda h, i: (h, i, 0))],
        out_shape=[jax.ShapeDtypeStruct((s, hw), BF16), jax.ShapeDtypeStruct((heads, s, 1), F32)],
        compiler_params=_attn_params(s, t, 0, 2),
    )(q, k, v)


def _mla_bwd(q, k, v, o, do, lse, t):
    s, hw = q.shape
    heads, nq = hw // LANE, s // t
    scale = 1.0 / math.sqrt(MLA_NOPE + MLA_ROPE)
    scale2 = scale * LOG2E

    def body(q_ref, k_ref, v_ref, o_ref, do_ref, l_ref, dq_ref, dk_ref, dv_ref):
        i = pl.program_id(1)

        @pl.when(i == 0)
        def _():
            dk_ref[...] = jnp.zeros_like(dk_ref)
            dv_ref[...] = jnp.zeros_like(dv_ref)

        qv, dov, lv = q_ref[...], do_ref[...], l_ref[0]
        dlt = jnp.sum(dov.astype(F32) * o_ref[...].astype(F32), axis=1, keepdims=True)

        def step(j, dq, masked):
            sl = pl.ds(pl.multiple_of(j * t, t), t)
            kv, vv = k_ref[sl, :], v_ref[sl, :]
            p = jnp.exp2(_dot_nt(qv, kv) * scale2 - lv)
            if masked:
                row = lax.broadcasted_iota(jnp.int32, (t, t), 0)
                col = lax.broadcasted_iota(jnp.int32, (t, t), 1)
                p = jnp.where(col <= row, p, 0.0)
            ds = (p * (_dot_nt(dov, vv) - dlt)).astype(BF16)
            dk_ref[sl, :] += _dot_tn(ds, qv) * scale
            dv_ref[sl, :] += _dot_tn(p.astype(BF16), dov)
            return dq + _dot(ds, kv)

        dq = lax.fori_loop(0, i, lambda j, c: step(j, c, False), jnp.zeros((t, LANE), F32))
        dq_ref[...] = step(i, dq, True) * scale

    blk = pl.BlockSpec((t, LANE), lambda h, i: (i, h))
    res = pl.BlockSpec((s, LANE), lambda h, i: (0, h))
    full = jax.ShapeDtypeStruct((s, hw), F32)
    return pl.pallas_call(
        body, name="mla_bwd", grid=(heads, nq),
        in_specs=[blk, res, res, blk, blk, pl.BlockSpec((1, t, 1), lambda h, i: (h, i, 0))],
        out_specs=[blk, res, res], out_shape=[full, full, full],
        compiler_params=_attn_params(s, t, 2, 2),
    )(q, k, v, o, do, lse)


def _sb_logits(qv, kv, scale, masked, t, upper):
    z = _dot_nt(qv, kv) * scale
    e = jnp.exp(-jnp.abs(z))
    l1p = jnp.log(1.0 + e)
    lb = jnp.minimum(z, 0.0) - l1p
    lo = -jnp.maximum(z, 0.0) - l1p
    keep = None
    if masked:
        row = lax.broadcasted_iota(jnp.int32, (t, t), 0)
        col = lax.broadcasted_iota(jnp.int32, (t, t), 1)
        keep = col < row
        lo = jnp.where(keep, lo, 0.0)
    hi = lo.astype(BF16)
    rem = (lo - hi.astype(F32)).astype(BF16)
    suf = _dot(hi, upper) + _dot(rem, upper)
    return z, e, lb, lo, suf, keep


def _tri(t, inclusive):
    row = lax.broadcasted_iota(jnp.int32, (t, t), 0)
    col = lax.broadcasted_iota(jnp.int32, (t, t), 1)
    return jnp.where((row >= col) if inclusive else (row > col), 1.0, 0.0).astype(BF16)


SB_CHAINS = 2


def _sb_walk(i, first, carries_of):
    n = SB_CHAINS
    carries = [first(c) for c in range(n)]
    width = len(carries[0])

    def alive(carry):
        return jnp.max(carry[0]) >= SB_ZERO_LOG

    def split(st):
        return [tuple(st[1 + c * width:1 + (c + 1) * width]) for c in range(n)]

    def live(st):
        any_alive = alive(split(st)[0])
        for cr in split(st)[1:]:
            any_alive = any_alive | alive(cr)
        return (st[0] <= n * i) & any_alive

    def more(st):
        out = (st[0] + 1,)
        for c, cr in enumerate(split(st)):
            out += tuple(carries_of(c, st[0], cr))
        return out

    st = lax.while_loop(live, more, (jnp.int32(1),) + tuple(x for cr in carries for x in cr))
    jj, carries = st[0], split(st)
    for c in range(1, n):
        def live_c(s2, c=c):
            return (s2[0] <= n * i + c) & alive(s2[1:])

        def more_c(s2, c=c):
            return (s2[0] + 1,) + tuple(carries_of(c, s2[0], s2[1:]))

        carries[c] = lax.while_loop(live_c, more_c, (jj,) + tuple(carries[c]))[1:]
    return carries


def _sb_fwd(proj, t):
    s = proj.shape[0]
    heads, nq, n = SB_HEADS, s // t, SB_CHAINS
    scale = 1.0 / math.sqrt(SB_HEAD_DIM)

    def body(q_ref, k_ref, v_ref, o_ref):
        i = pl.program_id(1)
        upper = _tri(t, False)
        qs = [q_ref[c * t:(c + 1) * t, :].astype(BF16) for c in range(n)]

        def step(c, jj, carry, masked):
            run, acc = carry
            sl = pl.ds(pl.multiple_of((n * i + c - jj) * t, t), t)
            _, _, lb, lo, suf, keep = _sb_logits(qs[c], k_ref[sl, :].astype(BF16), scale, masked, t, upper)
            a = jnp.exp(lb + suf + run)
            if masked:
                a = jnp.where(keep, a, 0.0)
            acc = acc + _dot(a.astype(BF16), v_ref[sl, :].astype(BF16))
            return run + jnp.sum(lo, axis=1, keepdims=True), acc

        init = (jnp.zeros((t, 1), F32), jnp.zeros((t, LANE), F32))
        carries = _sb_walk(i, lambda c: step(c, 0, init, True), lambda c, jj, cr: step(c, jj, cr, False))
        for c in range(n):
            o_ref[c * t:(c + 1) * t, :] = carries[c][1]

    return pl.pallas_call(
        body, name="sb_fwd", grid=(heads, nq // n),
        in_specs=[pl.BlockSpec((n * t, LANE), lambda h, i: (i, C_SBQ // LANE + h)),
                  pl.BlockSpec((s, LANE), lambda h, i: (0, C_SBK // LANE + h)),
                  pl.BlockSpec((s, LANE), lambda h, i: (0, C_SBV // LANE + h))],
        out_specs=pl.BlockSpec((n * t, LANE), lambda h, i: (i, h)),
        out_shape=jax.ShapeDtypeStruct((s, heads * LANE), F32),
        compiler_params=_attn_params(s, n * t, 2, 0),
    )(proj, proj, proj)


def _sb_bwd(proj, o, do, t):
    s = proj.shape[0]
    heads, nq, n = SB_HEADS, s // t, SB_CHAINS
    scale = 1.0 / math.sqrt(SB_HEAD_DIM)

    def body(q_ref, k_ref, v_ref, o_ref, do_ref, dq_ref, dk_ref, dv_ref):
        i = pl.program_id(1)

        @pl.when(i == 0)
        def _():
            dk_ref[...] = jnp.zeros_like(dk_ref)
            dv_ref[...] = jnp.zeros_like(dv_ref)

        rows = [slice(c * t, (c + 1) * t) for c in range(n)]
        qs = [q_ref[r, :].astype(BF16) for r in rows]
        dos = [do_ref[r, :] for r in rows]
        totals = [jnp.sum(dos[c].astype(F32) * o_ref[rows[c], :], axis=1, keepdims=True) for c in range(n)]
        upper, upper_incl = _tri(t, False), _tri(t, True)

        def step(c, jj, carry, masked):
            run, g, dq = carry
            qv, dov = qs[c], dos[c]
            sl = pl.ds(pl.multiple_of((n * i + c - jj) * t, t), t)
            kv, vv = k_ref[sl, :].astype(BF16), v_ref[sl, :].astype(BF16)
            z, e, lb, lo, suf, keep = _sb_logits(qv, kv, scale, masked, t, upper)
            tail = suf + run
            a = jnp.exp(lb + tail)
            if masked:
                a = jnp.where(keep, a, 0.0)
            ab = a.astype(BF16)
            gr = ab.astype(F32) * _dot_nt(dov, vv)
            ghi = gr.astype(BF16)
            grem = (gr - ghi.astype(F32)).astype(BF16)
            before = totals[c] - g - (_dot(ghi, upper_incl) + _dot(grem, upper_incl))
            before = jnp.where(tail < SB_ZERO_LOG, 0.0, before)
            r = 1.0 / (1.0 + e)
            pos = z >= 0.0
            dz = r * (gr * jnp.where(pos, e, 1.0) - before * jnp.where(pos, 1.0, e))
            if masked:
                dz = jnp.where(keep, dz, 0.0)
            dzb = (dz * scale).astype(BF16)
            dk_ref[sl, :] += _dot_tn(dzb, qv)
            dv_ref[sl, :] += _dot_tn(ab, dov)
            return (run + jnp.sum(lo, axis=1, keepdims=True), g + jnp.sum(gr, axis=1, keepdims=True),
                    dq + _dot(dzb, kv))

        zero = jnp.zeros((t, 1), F32)
        init = (zero, zero, jnp.zeros((t, LANE), F32))
        carries = _sb_walk(i, lambda c: step(c, 0, init, True), lambda c, jj, cr: step(c, jj, cr, False))
        for c in range(n):
            dq_ref[rows[c], :] = carries[c][2]

    blk = pl.BlockSpec((n * t, LANE), lambda h, i: (i, h))
    res = pl.BlockSpec((s, LANE), lambda h, i: (0, h))
    full = jax.ShapeDtypeStruct((s, heads * LANE), F32)
    return pl.pallas_call(
        body, name="sb_bwd", grid=(heads, nq // n),
        in_specs=[pl.BlockSpec((n * t, LANE), lambda h, i: (i, C_SBQ // LANE + h)),
                  pl.BlockSpec((s, LANE), lambda h, i: (0, C_SBK // LANE + h)),
                  pl.BlockSpec((s, LANE), lambda h, i: (0, C_SBV // LANE + h)), blk, blk],
        out_specs=[blk, res, res], out_shape=[full, full, full],
        compiler_params=_attn_params(s, n * t, 4, 0),
    )(proj, proj, proj, o, do)


def _xattn_probs(qh, kh):
    sc = _dot_nt(qh, kh) * (1.0 / math.sqrt(X_HEAD_DIM))
    p = jnp.exp(sc - jnp.max(sc, axis=1, keepdims=True))
    return p / jnp.sum(p, axis=1, keepdims=True)


def _xattn_fwd(xq, xkv):
    rows = xq.shape[0]
    w = X_HEADS * X_HEAD_DIM

    def body(i, ins, fulls, outs, accs):
        for h in range(X_HEADS):
            sl = slice(h * LANE, (h + 1) * LANE)
            p = _xattn_probs(ins[0][:, sl], fulls[0][:, sl])
            outs[0][:, sl] = _dot(p.astype(BF16), fulls[0][:, w + h * LANE:w + (h + 1) * LANE]).astype(BF16)

    return _rowwise(body, name="xattn_fwd", rows=rows, tr=512, row_ins=[(xq, w, 0)], full_ins=[xkv],
                    row_outs=[(w, BF16)])[0]


def _xattn_bwd(xq, xkv, dxo):
    rows = xq.shape[0]
    w = X_HEADS * X_HEAD_DIM

    def body(i, ins, fulls, outs, accs):
        _acc_init(i, accs)
        for h in range(X_HEADS):
            sl = slice(h * LANE, (h + 1) * LANE)
            slv = slice(w + h * LANE, w + (h + 1) * LANE)
            qh, kh, vh, doh = ins[0][:, sl], fulls[0][:, sl], fulls[0][:, slv], ins[1][:, sl]
            p = _xattn_probs(qh, kh)
            dp = _dot_nt(doh, vh)
            ds = (p * (dp - jnp.sum(p * dp, axis=1, keepdims=True)) * (1.0 / math.sqrt(X_HEAD_DIM))).astype(BF16)
            outs[0][:, sl] = _dot(ds, kh).astype(BF16)
            accs[0][:, sl] += _dot_tn(ds, qh)
            accs[0][:, slv] += _dot_tn(p.astype(BF16), doh)

    return _rowwise(body, name="xattn_bwd", rows=rows, tr=512, row_ins=[(xq, w, 0), (dxo, w, 0)], full_ins=[xkv],
                    row_outs=[(w, BF16)], acc_outs=[(xkv.shape, F32)])


def _gate_fwd(proj, pa, pb, b_gate):
    rows = proj.shape[0]

    def body(i, ins, fulls, outs, accs):
        sa = _sigmoid(ins[0][...] + fulls[0][0:1, :])
        sb = _sigmoid(ins[1][...] + fulls[0][1:2, :])
        outs[0][...] = (sa * ins[2][...] + sb * ins[3][...]).astype(BF16)

    return _rowwise(body, name="gate_fwd", rows=rows, tr=512,
                    row_ins=[(proj, D_MODEL, C_GA // D_MODEL), (proj, D_MODEL, C_GB // D_MODEL), (pa, D_MODEL, 0),
                             (pb, D_MODEL, 0)],
                    full_ins=[b_gate], row_outs=[(D_MODEL, BF16)])[0]


def _gate_bwd(proj, pa, pb, b_gate, dm):
    rows = proj.shape[0]

    def body(i, ins, fulls, outs, accs):
        _acc_init(i, accs)
        d = ins[4][...]
        sa = _sigmoid(ins[0][...] + fulls[0][0:1, :])
        sb = _sigmoid(ins[1][...] + fulls[0][1:2, :])
        dga = d * ins[2][...] * sa * (1.0 - sa)
        dgb = d * ins[3][...] * sb * (1.0 - sb)
        outs[0][...] = (d * sa).astype(BF16)
        outs[1][...] = (d * sb).astype(BF16)
        outs[2][...] = dga.astype(BF16)
        outs[3][...] = dgb.astype(BF16)
        accs[0][0:1, :] += jnp.sum(dga, axis=0, keepdims=True)
        accs[0][1:2, :] += jnp.sum(dgb, axis=0, keepdims=True)

    return _rowwise(body, name="gate_bwd", rows=rows, tr=512,
                    row_ins=[(proj, D_MODEL, C_GA // D_MODEL), (proj, D_MODEL, C_GB // D_MODEL), (pa, D_MODEL, 0),
                             (pb, D_MODEL, 0), (dm, D_MODEL, 0)],
                    full_ins=[b_gate], row_outs=[(D_MODEL, BF16)] * 4, acc_outs=[((2, D_MODEL), F32)])


def _loss_head(x3, target, g_final):
    rows = x3.shape[0]

    def body(i, ins, fulls, outs, accs):
        _acc_init(i, accs)
        xv, g = ins[0][...], fulls[0][...]
        d = _rms(xv, g) - ins[1][...]
        dx, dg = _rms_bwd(xv, g, d * (1.0 / D_MODEL))
        outs[0][...] = dx
        accs[0][...] += dg
        accs[1][...] += jnp.sum(d * d, axis=0, keepdims=True)

    return _rowwise(body, name="loss_head", rows=rows, tr=512, row_ins=[(x3, D_MODEL, 0), (target, D_MODEL, 0)],
                    full_ins=[g_final], row_outs=[(D_MODEL, F32)], acc_outs=[((1, D_MODEL), F32), ((1, D_MODEL), F32)])


def _adamw(w, g, m, v, name):
    rows, c = w.shape

    def body(i, ins, fulls, outs, accs):
        wv, gv = ins[0][...], ins[1][...]
        mn = ADAM_B1 * ins[2][...] + (1.0 - ADAM_B1) * gv
        vn = ADAM_B2 * ins[3][...] + (1.0 - ADAM_B2) * jnp.square(gv)
        m_hat = mn / (1.0 - ADAM_B1 ** ADAM_STEP)
        v_hat = vn / (1.0 - ADAM_B2 ** ADAM_STEP)
        outs[0][...] = -ADAM_LR * (m_hat / (jnp.sqrt(v_hat) + ADAM_EPS) + ADAM_WD * wv)
        outs[1][...] = mn
        outs[2][...] = vn

    return _rowwise(body, name=name, rows=rows, tr=_row_tile(rows, 256), row_ins=[(a, c, 0) for a in (w, g, m, v)],
                    row_outs=[(c, F32)] * 3)


def _place():
    x, y, c = lax.axis_index("x"), lax.axis_index("y"), lax.axis_index("c")
    chips = [(1 - x, y), (x, 1 - y), (1 - x, 1 - y)]
    return x, y, c, chips


ANY = pl.BlockSpec(memory_space=pl.ANY)


def _remote(src, dst, send_sem, recv_sem, to):
    return pltpu.make_async_remote_copy(src_ref=src, dst_ref=dst, send_sem=send_sem, recv_sem=recv_sem,
                                        device_id=to, device_id_type=MESH)


def _dma_sems(n):
    return [pltpu.SemaphoreType.DMA((n,)), pltpu.SemaphoreType.DMA((n,))]


def _all_gather_weights(shards):
    n = len(shards)

    def body(*refs):
        ins, outs = refs[:n], refs[n:2 * n]
        send_sems, recv_sems = refs[2 * n:]
        x, y, c, chips = _place()
        me = 2 * x + y

        def copy(t, j, chip_idx, hlf, to, src=None):
            h = ins[t].shape[0] // 2
            dst = outs[t].at[chip_idx, pl.ds(hlf * h, h), :]
            return _remote(dst if src is None else src, dst, send_sems.at[6 * t + j], recv_sems.at[6 * t + j], to)

        first = []
        for t in range(n):
            h = ins[t].shape[0] // 2
            for j, chip in enumerate(chips):
                first.append(copy(t, j, me, c, (*chip, c), src=ins[t].at[pl.ds(c * h, h), :]))
                first[-1].start()
        passed = []
        for t in range(n):
            for j, chip in enumerate(chips):
                idx = 2 * chip[0] + chip[1]
                copy(t, j, idx, c, (x, y, c)).wait_recv()
                passed.append(copy(t, 3 + j, idx, c, (x, y, 1 - c)))
                passed[-1].start()
        for t in range(n):
            for j, chip in enumerate(chips):
                copy(t, 3 + j, 2 * chip[0] + chip[1], 1 - c, (x, y, c)).wait_recv()
        for cp in first + passed:
            cp.wait_send()

    return pl.pallas_call(
        body, name="all_gather_weights", in_specs=[ANY] * n, out_specs=[ANY] * n,
        out_shape=[jax.ShapeDtypeStruct((N_CHIPS,) + s.shape, s.dtype) for s in shards],
        scratch_shapes=_dma_sems(6 * n),
    )(*shards)


def _pair_exchange_grads(stacked):
    n = len(stacked)

    def body(*refs):
        ins, outs = refs[:n], refs[n:2 * n]
        send_sems, recv_sems = refs[2 * n:]
        x, y, c, _ = _place()
        cps = []
        for t in range(n):
            h = ins[t].shape[1] // 2
            cps.append(_remote(ins[t].at[:, pl.ds((1 - c) * h, h), :], outs[t], send_sems.at[t], recv_sems.at[t],
                               (x, y, 1 - c)))
            cps[-1].start()
        for cp in cps:
            cp.wait()

    return pl.pallas_call(
        body, name="pair_exchange_grads", in_specs=[ANY] * n, out_specs=[ANY] * n,
        out_shape=[jax.ShapeDtypeStruct((N_CHIPS, s.shape[1] // 2, s.shape[2]), s.dtype) for s in stacked],
        scratch_shapes=_dma_sems(n),
    )(*stacked)


def _chip_scatter(parts):
    n = len(parts)

    def body(*refs):
        ins, outs = refs[:n], refs[n:2 * n]
        send_sems, recv_sems = refs[2 * n:]
        x, y, c, chips = _place()
        cps = []
        for t in range(n):
            for j, chip in enumerate(chips):
                cps.append(_remote(ins[t].at[2 * chip[0] + chip[1]], outs[t].at[j], send_sems.at[3 * t + j],
                                   recv_sems.at[3 * t + j], (*chip, c)))
                cps[-1].start()
        for cp in cps:
            cp.wait()

    return pl.pallas_call(
        body, name="chip_scatter", in_specs=[ANY] * n, out_specs=[ANY] * n,
        out_shape=[jax.ShapeDtypeStruct((N_CHIPS - 1,) + s.shape[1:], s.dtype) for s in parts],
        scratch_shapes=_dma_sems(3 * n),
    )(*parts)


def _pair_exchange_halves(shards):
    n = len(shards)

    def body(*refs):
        bufs = refs[n:2 * n]
        send_sems, recv_sems = refs[2 * n:]
        x, y, c, _ = _place()
        cps = []
        for t in range(n):
            h = bufs[t].shape[0] // 2
            rows = bufs[t].at[pl.ds(c * h, h), :]
            cps.append(_remote(rows, rows, send_sems.at[t], recv_sems.at[t], (x, y, 1 - c)))
            cps[-1].start()
        for cp in cps:
            cp.wait()

    return pl.pallas_call(
        body, name="pair_exchange_halves", in_specs=[ANY] * n, out_specs=[ANY] * n,
        out_shape=[jax.ShapeDtypeStruct(s.shape, s.dtype) for s in shards],
        input_output_aliases={t: t for t in range(n)},
        scratch_shapes=_dma_sems(n),
    )(*shards)


def _pair_sum(gs, recv, place, name):
    _, r, cols = gs.shape
    h = r // 2

    def kern(p_ref, a_ref, b_ref, o_ref):
        o_ref[...] = (a_ref[...] + b_ref[...]).astype(BF16)

    blk = lambda f: pl.BlockSpec((1, h, cols), f)
    return pl.pallas_call(
        kern, name=name,
        grid_spec=pltpu.PrefetchScalarGridSpec(
            num_scalar_prefetch=1, grid=(N_CHIPS,),
            in_specs=[blk(lambda d, p: (d, p[1], 0)), blk(lambda d, p: (d, 0, 0))],
            out_specs=blk(lambda d, p: (d, 0, 0))),
        out_shape=jax.ShapeDtypeStruct((N_CHIPS, h, cols), BF16),
        compiler_params=pltpu.CompilerParams(dimension_semantics=("arbitrary",),
                                             vmem_limit_bytes=_vmem_limit(3 * _nbytes((h, cols), F32), 0)),
    )(place, gs, recv)


def _chip_sum(gs, recv, got, place, name):
    _, r, cols = gs.shape
    h = r // 2

    def kern(p_ref, a_ref, b_ref, g0, g1, g2, o_ref):
        own = a_ref[0] + b_ref[0]
        o_ref[...] = ((own + g0[0].astype(F32)) + g1[0].astype(F32)) + g2[0].astype(F32)

    blk = lambda f: pl.BlockSpec((1, h, cols), f)
    return pl.pallas_call(
        kern, name=name,
        grid_spec=pltpu.PrefetchScalarGridSpec(
            num_scalar_prefetch=1, grid=(1,),
            in_specs=[blk(lambda i, p: (p[0], p[1], 0)), blk(lambda i, p: (p[0], 0, 0)), blk(lambda i, p: (0, 0, 0)),
                      blk(lambda i, p: (1, 0, 0)), blk(lambda i, p: (2, 0, 0))],
            out_specs=pl.BlockSpec((h, cols), lambda i, p: (p[1], 0))),
        out_shape=jax.ShapeDtypeStruct((r, cols), F32),
        compiler_params=pltpu.CompilerParams(dimension_semantics=("arbitrary",),
                                             vmem_limit_bytes=_vmem_limit(5 * _nbytes((h, cols), F32), 0)),
    )(place, gs, recv, got, got, got)


def _all_reduce_small(vec, name):
    r, cols = vec.shape

    def body(in_ref, out_ref, gath, send_sems, recv_sems):
        x, y, c, _ = _place()
        me = 4 * x + 2 * y + c
        gath[me] = in_ref[...]
        sends = []
        for k in range(1, 8):
            to = (x ^ (k >> 2), y ^ ((k >> 1) & 1), c ^ (k & 1))
            cp = pltpu.make_async_remote_copy(src_ref=in_ref, dst_ref=gath.at[me], send_sem=send_sems.at[k - 1],
                                              recv_sem=recv_sems.at[k - 1], device_id=to, device_id_type=MESH)
            cp.start()
            sends.append(cp)
        for k in range(1, 8):
            peer = me ^ k
            pltpu.make_async_remote_copy(src_ref=in_ref, dst_ref=gath.at[peer], send_sem=send_sems.at[k - 1],
                                         recv_sem=recv_sems.at[k - 1], device_id=(x, y, c),
                                         device_id_type=MESH).wait_recv()
        for cp in sends:
            cp.wait_send()
        acc = gath[0]
        for d in range(1, 8):
            acc = acc + gath[d]
        out_ref[...] = acc

    vm = pl.BlockSpec(memory_space=pltpu.VMEM)
    return pl.pallas_call(
        body, name=name, in_specs=[vm], out_specs=vm,
        out_shape=jax.ShapeDtypeStruct((r, cols), F32),
        scratch_shapes=[pltpu.VMEM((8, r, cols), F32), pltpu.SemaphoreType.DMA((7,)), pltpu.SemaphoreType.DMA((7,))],
    )(vec)


def _pad_heads(w, heads, dim, axis):
    shp = w.shape[:axis] + (heads, dim) + w.shape[axis + 1:]
    pad = [(0, 0)] * len(shp)
    pad[axis + 1] = (0, LANE - dim)
    w = jnp.pad(w.reshape(shp), pad)
    return w.reshape(w.shape[:axis] + (heads * LANE,) + w.shape[axis + 2:])


def _unpad_heads(w, heads, dim, axis):
    shp = w.shape[:axis] + (heads, LANE) + w.shape[axis + 1:]
    w = lax.slice_in_dim(w.reshape(shp), 0, dim, axis=axis + 1)
    return w.reshape(w.shape[:axis] + (heads * dim,) + w.shape[axis + 2:])


def _w_in_layout(w_in):
    kr = jnp.pad(w_in[:, 384:416], ((0, 0), (ROPE_LO, LANE - ROPE_LO - MLA_ROPE)))
    sb = lambda lo: _pad_heads(w_in[:, lo:lo + 512], SB_HEADS, SB_HEAD_DIM, 1)
    return jnp.concatenate([w_in[:, 1952:2976], w_in[:, 2976:4000], sb(416), sb(928), sb(1440), w_in[:, 0:256],
                            w_in[:, 256:384], kr], axis=1)


def _w_in_unlayout(d):
    sb = lambda lo: _unpad_heads(d[:, lo:lo + 1024], SB_HEADS, SB_HEAD_DIM, 1)
    return jnp.concatenate([d[:, C_CQ:C_CQ + 256], d[:, C_CKV:C_CKV + 128], d[:, C_KR + ROPE_LO:C_KR + ROPE_LO + MLA_ROPE],
                            sb(C_SBQ), sb(C_SBK), sb(C_SBV), d[:, C_GA:C_GA + 1024], d[:, C_GB:C_GB + 1024]], axis=1)


def _w_ukv_layout(w):
    w3 = w.reshape(MLA_KV_RANK, MLA_HEADS, MLA_NOPE + MLA_V)
    pad = lambda part: jnp.pad(part, ((0, 0), (0, 0), (0, LANE - part.shape[2]))).reshape(MLA_KV_RANK, MLA_HEADS * LANE)
    return jnp.concatenate([pad(w3[:, :, :MLA_NOPE]), pad(w3[:, :, MLA_NOPE:])], axis=1)


def _w_ukv_unlayout(d):
    hw = MLA_HEADS * LANE
    kpart = d[:, :hw].reshape(MLA_KV_RANK, MLA_HEADS, LANE)[:, :, :MLA_NOPE]
    vpart = d[:, hw:].reshape(MLA_KV_RANK, MLA_HEADS, LANE)[:, :, :MLA_V]
    return jnp.concatenate([kpart, vpart], axis=2).reshape(MLA_KV_RANK, MLA_HEADS * (MLA_NOPE + MLA_V))


def _shard_of(full, d, axis):
    n = full.shape[axis] // N_CHIPS
    return lax.slice_in_dim(full, d * n, (d + 1) * n, axis=axis)


def _local_step(x, mem, pos, target, w, t_mla, t_sb):
    s = x.shape[0]
    win = _w_in_layout(w["w_in"])
    wuq = _pad_heads(w["w_uq"], MLA_HEADS, MLA_NOPE + MLA_ROPE, 1)
    wkv = _w_ukv_layout(w["w_ukv"])
    wa = _pad_heads(w["w_a_proj"], MLA_HEADS, MLA_V, 0)
    wb = _pad_heads(w["w_b_proj"], SB_HEADS, SB_HEAD_DIM, 0)
    inv_freq = ROPE_THETA ** (-jnp.arange(0, MLA_ROPE, 2, dtype=F32) / MLA_ROPE)
    freq_lane = jnp.pad(jnp.concatenate([inv_freq, inv_freq]), (ROPE_LO, LANE - ROPE_LO - MLA_ROPE)).reshape(1, LANE)
    add = lambda accs, ex: (accs[0] + ex[0],)

    tab = _rope_tables(pos.reshape(s, 1), freq_lane)
    h = _rms_fwd_call(x, w["g_mix"], "rms_mix")
    proj = _mm(h, [win], name="proj_in", tn=1408)
    cqn, ckvn, krope = _mla_prep_fwd(proj, tab, w["g_q_lat"], w["g_kv_lat"])
    qp = _mm(cqn, [wuq], name="q_up")
    kvp = _mm(ckvn, [wkv], name="kv_up")
    qa, ka, va = _mla_rope_fwd(qp, kvp, krope, tab)
    o_a, lse = _mla_fwd(qa, ka, va, t_mla)
    o_b = _sb_fwd(proj, t_sb)
    pa = _mm(o_a, [wa], name="proj_a")
    pb = _mm(o_b, [wb], name="proj_b")
    merged = _gate_fwd(proj, pa, pb, w["b_gate"])
    x1 = _mm(merged, [w["w_o"]], name="proj_o", extras=(x,), epilogue=add)
    hx = _rms_fwd_call(x1, w["g_x"], "rms_x")
    mn = _rms_fwd_call(mem, w["g_mem"], "rms_mem")
    xq = _mm(hx, [w["w_xq"]], name="xq", out_dtypes=(BF16,))
    xkv = _mm(mn, [w["w_xkv"]], name="xkv", out_dtypes=(BF16,))
    xo = _xattn_fwd(xq, xkv)
    x2 = _mm(xo, [w["w_xo"]], name="proj_xo", extras=(x1,), epilogue=add)
    hf = _rms_fwd_call(x2, w["g_ffn"], "rms_ffn")

    def swiglu(accs, ex):
        a, b = accs
        return a, b, a * _sigmoid(a) * b

    ga, gu, hmid = _mm(hf, [w["w_gate"], w["w_up"]], name="ffn_up", epilogue=swiglu, out_dtypes=(BF16, BF16, BF16),
                       tm=512, tn=1408)
    x3 = _mm(hmid, [w["w_down"]], name="ffn_down", extras=(x2,), epilogue=add, tk=2816)

    dx3, dg_final, sq = _loss_head(x3, target, w["g_final"].reshape(1, D_MODEL))
    g = {"g_final": dg_final.reshape(D_MODEL)}

    def swiglu_bwd(accs, ex):
        dh, a, b = accs[0], ex[0].astype(F32), ex[1].astype(F32)
        sg = _sigmoid(a)
        return dh * b * sg * (1.0 + a * (1.0 - sg)), dh * a * sg

    da, db = _mm(dx3, [w["w_down"]], name="ffn_down_dx", tb=True, extras=(ga, gu), epilogue=swiglu_bwd,
                 out_dtypes=(BF16, BF16), tm=512, tn=1408)
    g["w_down"] = _mm(hmid, [dx3], name="ffn_down_dw", ta=True, tm=1408)
    g["w_gate"] = _mm(hf, [da], name="ffn_gate_dw", ta=True, tn=1408, tk=2048)
    g["w_up"] = _mm(hf, [db], name="ffn_up_dw", ta=True, tn=1408)
    dhf = _mm(da, [w["w_gate"]], name="ffn_gate_dx", tb=True, tk=2816)
    dhf = _mm(db, [w["w_up"]], name="ffn_up_dx", tb=True, extras=(dhf,), epilogue=add, tk=1408)
    dx2, g["g_ffn"] = _rms_bwd_call(x2, w["g_ffn"], dhf, dx3, "rms_ffn_bwd")

    dxo = _mm(dx2, [w["w_xo"]], name="proj_xo_dx", tb=True, out_dtypes=(BF16,))
    g["w_xo"] = _mm(xo, [dx2], name="proj_xo_dw", ta=True)
    dxq, dxkv = _xattn_bwd(xq, xkv, dxo)
    dhx = _mm(dxq, [w["w_xq"]], name="xq_dx", tb=True)
    g["w_xq"] = _mm(hx, [dxq], name="xq_dw", ta=True)
    dmn = _mm(dxkv, [w["w_xkv"]], name="xkv_dx", tb=True)
    g["w_xkv"] = _mm(mn, [dxkv], name="xkv_dw", ta=True)
    dx1, g["g_x"] = _rms_bwd_call(x1, w["g_x"], dhx, dx2, "rms_x_bwd")
    _, g["g_mem"] = _rms_bwd_call(mem, w["g_mem"], dmn, None, "rms_mem_bwd")

    dmerged = _mm(dx1, [w["w_o"]], name="proj_o_dx", tb=True)
    g["w_o"] = _mm(merged, [dx1], name="proj_o_dw", ta=True)
    dpa, dpb, dga, dgb, g["b_gate"] = _gate_bwd(proj, pa, pb, w["b_gate"], dmerged)
    do_a = _mm(dpa, [wa], name="proj_a_dx", tb=True, out_dtypes=(BF16,))
    do_b = _mm(dpb, [wb], name="proj_b_dx", tb=True, out_dtypes=(BF16,))
    g["w_a_proj"] = _unpad_heads(_mm(o_a, [dpa], name="proj_a_dw", ta=True), MLA_HEADS, MLA_V, 0)
    g["w_b_proj"] = _unpad_heads(_mm(o_b, [dpb], name="proj_b_dw", ta=True), SB_HEADS, SB_HEAD_DIM, 0)

    dsq, dsk, dsv = _sb_bwd(proj, o_b, do_b, t_sb)
    dqa, dka, dva = _mla_bwd(qa, ka, va, o_a, do_a, lse, t_mla)
    dqp, dkvp, dkr = _mla_rope_bwd(dqa, dka, dva, tab)
    g["w_uq"] = _unpad_heads(_mm(cqn, [dqp], name="q_up_dw", ta=True), MLA_HEADS, MLA_NOPE + MLA_ROPE, 1)
    g["w_ukv"] = _w_ukv_unlayout(_mm(ckvn, [dkvp], name="kv_up_dw", ta=True))
    dcqn = _mm(dqp, [wuq], name="q_up_dx", tb=True)
    dckvn = _mm(dkvp, [wkv], name="kv_up_dx", tb=True)
    dcq, dckv, g["g_q_lat"], g["g_kv_lat"] = _mla_prep_bwd(proj, w["g_q_lat"], w["g_kv_lat"], dcqn, dckvn)

    dproj = jnp.concatenate([dga, dgb, dsq.astype(BF16), dsk.astype(BF16), dsv.astype(BF16), dcq, dckv, dkr], axis=1)
    g["w_in"] = _w_in_unlayout(_mm(h, [dproj], name="proj_in_dw", ta=True, tn=1408))
    dh = _mm(dproj, [win], name="proj_in_dx", tb=True, tk=1408)
    grad_x, g["g_mix"] = _rms_bwd_call(x, w["g_mix"], dh, dx1, "rms_mix_bwd")
    return sq, grad_x, g


def _small_pack(d):
    row5 = jnp.concatenate([d["g_q_lat"].reshape(-1), d["g_kv_lat"].reshape(-1), jnp.zeros((640,), F32)])
    rows = [d[n].reshape(-1) for n in ("g_mix", "g_x", "g_mem", "g_ffn", "g_final")] + [row5]
    return rows


def _small_unpack(p, like):
    out = {n: p[i].reshape(like[n].shape) for i, n in enumerate(("g_mix", "g_x", "g_mem", "g_ffn", "g_final"))}
    out["g_q_lat"] = p[5, 0:256].reshape(like["g_q_lat"].shape)
    out["g_kv_lat"] = p[5, 256:384].reshape(like["g_kv_lat"].shape)
    return out


def kernel(x, mem, positions, g_mix, w_in, b_gate, g_q_lat, w_uq, g_kv_lat, w_ukv, w_a_proj, w_b_proj, w_o, g_x, g_mem, w_xq, w_xkv, w_xo, g_ffn, w_gate, w_up, w_down, g_final, loss_target, m_g_mix, m_w_in, m_b_gate, m_g_q_lat, m_w_uq, m_g_kv_lat, m_w_ukv, m_w_a_proj, m_w_b_proj, m_w_o, m_g_x, m_g_mem, m_w_xq, m_w_xkv, m_w_xo, m_g_ffn, m_w_gate, m_w_up, m_w_down, m_g_final, v_g_mix, v_w_in, v_b_gate, v_g_q_lat, v_w_uq, v_g_kv_lat, v_w_ukv, v_w_a_proj, v_w_b_proj, v_w_o, v_g_x, v_g_mem, v_w_xq, v_w_xkv, v_w_xo, v_g_ffn, v_w_gate, v_w_up, v_w_down, v_g_final):
    given = dict(locals())
    names = [n for n, _, _ in MATS] + ["b_gate"] + list(SMALL)
    wts = {n: given[n] for n in names}
    mom = {n: given["m_" + n] for n in names}
    var = {n: given["v_" + n] for n in names}
    shard2d = {n: shp for n, shp, _ in MATS}
    shard2d["b_gate"] = B_GATE_SHARD
    cx, cy, cc = lax.axis_index("x"), lax.axis_index("y"), lax.axis_index("c")
    me = 2 * cx + cy
    place = jnp.stack([me, cc]).astype(jnp.int32)
    bcol = me * B_GATE_SHARD[1]

    own = [wts[n].reshape(shard2d[n]).astype(BF16) for n, _, _ in MATS]
    full = {}
    for (n, shp, ax), g4, mine in zip(MATS, _all_gather_weights(own), own):
        g4 = lax.dynamic_update_slice(g4, mine[None], (me, 0, 0))
        full[n] = g4.reshape(N_CHIPS * shp[0], shp[1]) if ax == 0 else jnp.concatenate(list(g4), axis=1)
    bias_rows = jnp.pad(wts["b_gate"].reshape(B_GATE_SHARD), ((0, SMALL_ROWS - 2), (0, 0)))
    bias_rows = lax.dynamic_update_slice(jnp.zeros((SMALL_ROWS, D_MODEL), F32), bias_rows, (0, bcol))
    full["b_gate"] = _all_reduce_small(jnp.where(cc == 0, bias_rows, 0.0), "all_gather_bias")[0:2]
    for n in SMALL:
        full[n] = wts[n].reshape(1, -1) if n != "g_final" else wts[n]

    sq, grad_x, grads = _local_step(x[0], mem[0], positions[0], loss_target[0], full, t_mla=1024, t_sb=256)

    stacked = [jnp.stack([_shard_of(grads[n], d, ax) for d in range(N_CHIPS)]) for n, _, ax in MATS]
    recv = _pair_exchange_grads(stacked)
    parts = [_pair_sum(gs, rv, place, "pair_sum_" + n) for (n, _, _), gs, rv in zip(MATS, stacked, recv)]
    got = _chip_scatter(parts)
    halves = [_chip_sum(gs, rv, gt, place, "chip_sum_" + n) for (n, _, _), gs, rv, gt in zip(MATS, stacked, recv, got)]
    g_shard = dict(zip([n for n, _, _ in MATS], _pair_exchange_halves(halves)))

    small_rows = _small_pack({n: grads[n] for n in SMALL}) + [sq.reshape(-1), grads["b_gate"][0], grads["b_gate"][1]]
    small_rows += [jnp.zeros((D_MODEL,), F32)] * (SMALL_ROWS - len(small_rows))
    small = _all_reduce_small(jnp.stack(small_rows), "all_reduce_small")
    loss = (0.5 / D_MODEL) * jnp.sum(small[6])
    g_shard["b_gate"] = lax.dynamic_slice(small[7:9], (0, bcol), B_GATE_SHARD)

    out = {"grad": {}, "delta": {}, "m": {}, "v": {}}
    for n in [n for n, _, _ in MATS] + ["b_gate"]:
        shape = wts[n].shape
        r2 = lambda a: a.reshape(shard2d[n])
        d_n, m_n, v_n = _adamw(r2(wts[n]), g_shard[n], r2(mom[n]), r2(var[n]), "adamw_" + n)
        for key, a in (("grad", g_shard[n]), ("delta", d_n), ("m", m_n), ("v", v_n)):
            out[key][n] = a.reshape(shape)
    sp = lambda d: jnp.stack(_small_pack(d) + [jnp.zeros((D_MODEL,), F32)] * 2)
    delta_s, m_s, v_s = _adamw(sp(wts), small[0:8].at[6:8].set(0.0), sp(mom), sp(var), "adamw_small")
    for key, p in (("grad", small), ("delta", delta_s), ("m", m_s), ("v", v_s)):
        out[key].update(_small_unpack(p, wts))

    order = ["g_mix", "w_in", "b_gate", "g_q_lat", "w_uq", "g_kv_lat", "w_ukv", "w_a_proj", "w_b_proj", "w_o", "g_x",
             "g_mem", "w_xq", "w_xkv", "w_xo", "g_ffn", "w_gate", "w_up", "w_down", "g_final"]
    return (loss, grad_x[None], *[out[key][n] for key in ("grad", "delta", "m", "v") for n in order])
```

```python
import functools
import math

import jax
import jax.numpy as jnp
from jax import lax
from jax.experimental import pallas as pl
from jax.experimental.pallas import tpu as pltpu

F32 = jnp.float32
BF16 = jnp.bfloat16
MESH = pl.DeviceIdType.MESH

D_MODEL = 1024
MLA_HEADS = 8
MLA_Q_RANK = 256
MLA_KV_RANK = 128
MLA_NOPE = 64
MLA_ROPE = 32
MLA_V = 64
ROPE_THETA = 10000.0
SB_HEADS = 8
SB_HEAD_DIM = 64
X_HEADS = 4
X_HEAD_DIM = 128
D_FF = 2816
EPS = 1e-6
ADAM_LR = 0.001
ADAM_B1 = 0.9
ADAM_B2 = 0.999
ADAM_EPS = 1e-08
ADAM_WD = 0.01
ADAM_STEP = 10

LANE = 128
LOG2E = 1.4426950408889634
N_CHIPS = 4
VMEM_BYTES = 64 * 1024 * 1024

C_GA, C_GB, C_SBQ, C_SBK, C_SBV, C_CQ, C_CKV, C_KR = 0, 1024, 2048, 3072, 4096, 5120, 5376, 5504
D_INP = 5632
ROPE_LO = MLA_NOPE
HALF = MLA_ROPE // 2

SB_ZERO_LOG = -104.0

MATS = (
    ("w_in", (1024, 1000), 1), ("w_uq", (256, 192), 1), ("w_ukv", (128, 256), 1), ("w_a_proj", (512, 256), 1),
    ("w_b_proj", (512, 256), 1), ("w_o", (256, 1024), 0), ("w_xq", (256, 512), 0), ("w_xkv", (256, 1024), 0),
    ("w_xo", (512, 256), 1), ("w_gate", (1024, 704), 1), ("w_up", (1024, 704), 1), ("w_down", (704, 1024), 0),
)
B_GATE_SHARD = (2, 256)
SMALL = ("g_mix", "g_x", "g_mem", "g_ffn", "g_final", "g_q_lat", "g_kv_lat")
SMALL_ROWS = 16


def _vmem_limit(block_bytes, temp_bytes):
    est = 2 * block_bytes + temp_bytes + (4 << 20)
    return int(min(max(est, 16 << 20), VMEM_BYTES - (6 << 20)))


def _nbytes(shape, dtype):
    return math.prod(shape) * jnp.dtype(dtype).itemsize


def _row_tile(rows, cap):
    if rows <= cap:
        return rows
    return max(t for t in range(8, cap + 1, 8) if rows % t == 0)


def _tile(n, cap):
    if n <= cap:
        return n
    best = None
    for t in range(LANE, cap + 1, LANE):
        if n % t == 0:
            best = t
    assert best is not None, (n, cap)
    return best


def _mm(a, bs, *, name, ta=False, tb=False, extras=(), epilogue=None, out_dtypes=(F32,), tm=1024, tn=1024, tk=1024):
    bs = tuple(bs)
    m, k = (a.shape[1], a.shape[0]) if ta else a.shape
    n = bs[0].shape[0] if tb else bs[0].shape[1]
    tm, tn, tk = _tile(m, tm), _tile(n, tn), _tile(k, tk)
    assert m % tm == 0 and n % tn == 0 and k % tk == 0
    nk = k // tk
    nb, ne, no = len(bs), len(extras), len(out_dtypes)
    dims = (((0,) if ta else (1,)), ((1,) if tb else (0,))), ((), ())
    if epilogue is None:
        epilogue = lambda accs, ex: (accs[0],)

    def body(*refs):
        a_ref, b_refs, e_refs = refs[0], refs[1:1 + nb], refs[1 + nb:1 + nb + ne]
        o_refs, acc_refs = refs[1 + nb + ne:1 + nb + ne + no], refs[1 + nb + ne + no:]
        if nk == 1:
            av = a_ref[...].astype(BF16)
            accs = [lax.dot_general(av, b_ref[...].astype(BF16), dims, preferred_element_type=F32) for b_ref in b_refs]
            for o_ref, v in zip(o_refs, epilogue(accs, [e[...] for e in e_refs])):
                o_ref[...] = v.astype(o_ref.dtype)
            return
        kk = pl.program_id(2)

        @pl.when(kk == 0)
        def _():
            for acc in acc_refs:
                acc[...] = jnp.zeros_like(acc)

        av = a_ref[...].astype(BF16)
        for b_ref, acc in zip(b_refs, acc_refs):
            acc[...] += lax.dot_general(av, b_ref[...].astype(BF16), dims, preferred_element_type=F32)

        @pl.when(kk == nk - 1)
        def _():
            outs = epilogue([acc[...] for acc in acc_refs], [e[...] for e in e_refs])
            for o_ref, v in zip(o_refs, outs):
                o_ref[...] = v.astype(o_ref.dtype)

    a_spec = pl.BlockSpec((tk, tm), lambda i, j, kk: (kk, i)) if ta else pl.BlockSpec((tm, tk), lambda i, j, kk: (i, kk))
    b_spec = pl.BlockSpec((tn, tk), lambda i, j, kk: (j, kk)) if tb else pl.BlockSpec((tk, tn), lambda i, j, kk: (kk, j))
    mn_spec = pl.BlockSpec((tm, tn), lambda i, j, kk: (i, j))
    blocks = (_nbytes((tm, tk), a.dtype) + sum(_nbytes((tk, tn), b.dtype) for b in bs)
              + sum(_nbytes((tm, tn), e.dtype) for e in extras) + sum(_nbytes((tm, tn), d) for d in out_dtypes))
    temps = (nb + 4) * _nbytes((tm, tn), F32)
    outs = pl.pallas_call(
        body, name=name, grid=(m // tm, n // tn, nk),
        in_specs=[a_spec] + [b_spec] * nb + [mn_spec] * ne,
        out_specs=[mn_spec] * no,
        out_shape=[pltpu.HBM((m, n), d) for d in out_dtypes],
        scratch_shapes=[pltpu.VMEM((tm, tn), F32) for _ in range(nb if nk > 1 else 0)],
        compiler_params=pltpu.CompilerParams(
            dimension_semantics=("parallel", "parallel", "arbitrary"),
            vmem_limit_bytes=_vmem_limit(blocks, temps)),
    )(a, *bs, *extras)
    return outs[0] if no == 1 else outs


def _rowwise(body, *, name, rows, tr, row_ins, full_ins=(), row_outs=(), acc_outs=()):
    tr = min(tr, rows)
    assert rows % tr == 0
    n_ri, n_fi, n_ro = len(row_ins), len(full_ins), len(row_outs)

    def kern(*refs):
        body(pl.program_id(0), refs[:n_ri], refs[n_ri:n_ri + n_fi], refs[n_ri + n_fi:n_ri + n_fi + n_ro],
             refs[n_ri + n_fi + n_ro:])

    in_specs = [pl.BlockSpec((tr, w), functools.partial(lambda i, c: (i, c), c=ci)) for _, w, ci in row_ins]
    in_specs += [pl.BlockSpec(f.shape, lambda i: (0, 0)) for f in full_ins]
    out_specs = [pl.BlockSpec((tr, w), lambda i: (i, 0)) for w, _ in row_outs]
    out_specs += [pl.BlockSpec(s, lambda i: (0, 0)) for s, _ in acc_outs]
    out_shape = [pltpu.HBM((rows, w), d) for w, d in row_outs]
    out_shape += [pltpu.HBM(s, d) for s, d in acc_outs]
    blocks = (sum(_nbytes((tr, w), a.dtype) for a, w, _ in row_ins) + sum(_nbytes(f.shape, f.dtype) for f in full_ins)
              + sum(_nbytes((tr, w), d) for w, d in row_outs) + sum(_nbytes(s, d) for s, d in acc_outs))
    widest = max([w for _, w, _ in row_ins] + [w for w, _ in row_outs])
    outs = pl.pallas_call(
        kern, name=name, grid=(rows // tr,), in_specs=in_specs, out_specs=out_specs, out_shape=out_shape,
        compiler_params=pltpu.CompilerParams(
            dimension_semantics=("arbitrary",) if acc_outs else ("parallel",),
            vmem_limit_bytes=_vmem_limit(blocks, 8 * _nbytes((tr, widest), F32))),
    )(*[a for a, _, _ in row_ins], *full_ins)
    return outs


def _rms(x, g):
    r = lax.rsqrt(jnp.mean(x * x, axis=-1, keepdims=True) + EPS)
    return x * r * g


def _rms_bwd(x, g, dy):
    r = lax.rsqrt(jnp.mean(x * x, axis=-1, keepdims=True) + EPS)
    xh = x * r
    dxh = dy * g
    dx = r * (dxh - xh * jnp.mean(dxh * xh, axis=-1, keepdims=True))
    return dx, jnp.sum(dy * xh, axis=0, keepdims=True)


def _sigmoid(x):
    return 1.0 / (1.0 + jnp.exp(-x))


def _acc_init(i, refs):
    @pl.when(i == 0)
    def _():
        for r in refs:
            r[...] = jnp.zeros_like(r)


def _rms_fwd_call(x, g, name):
    rows, c = x.shape

    def body(i, ins, fulls, outs, accs):
        outs[0][...] = _rms(ins[0][...], fulls[0][...]).astype(BF16)

    return _rowwise(body, name=name, rows=rows, tr=512, row_ins=[(x, c, 0)], full_ins=[g], row_outs=[(c, BF16)])[0]


def _rms_bwd_call(x, g, dy, res, name):
    rows, c = x.shape
    row_ins = [(x, c, 0), (dy, c, 0)] + ([(res, c, 0)] if res is not None else [])

    def body(i, ins, fulls, outs, accs):
        _acc_init(i, accs)
        dx, dg = _rms_bwd(ins[0][...], fulls[0][...], ins[1][...].astype(F32))
        if res is not None:
            dx = dx + ins[2][...]
        outs[0][...] = dx
        accs[0][...] += dg

    return _rowwise(body, name=name, rows=rows, tr=512, row_ins=row_ins, full_ins=[g], row_outs=[(c, F32)],
                    acc_outs=[((1, c), F32)])


def _rope_tables(pos_col, freq_lane):
    rows = pos_col.shape[0]

    def body(i, ins, fulls, outs, accs):
        ang = ins[0][...].astype(F32) * fulls[0][...]
        lane = lax.broadcasted_iota(jnp.int32, ang.shape, 1)
        cos, sin = jnp.cos(ang), jnp.sin(ang)
        first = (lane >= ROPE_LO) & (lane < ROPE_LO + HALF)
        second = (lane >= ROPE_LO + HALF) & (lane < ROPE_LO + MLA_ROPE)
        outs[0][:, 0:LANE] = jnp.where(first | second, cos, 1.0)
        outs[0][:, LANE:2 * LANE] = jnp.where(first, -sin, 0.0)
        outs[0][:, 2 * LANE:3 * LANE] = jnp.where(second, sin, 0.0)

    return _rowwise(body, name="rope_tables", rows=rows, tr=1024, row_ins=[(pos_col, 1, 0)], full_ins=[freq_lane],
                    row_outs=[(3 * LANE, F32)])[0]


def _rope(x, tab):
    return (x * tab[:, 0:LANE] + pltpu.roll(x, LANE - HALF, 1) * tab[:, LANE:2 * LANE]
            + pltpu.roll(x, HALF, 1) * tab[:, 2 * LANE:3 * LANE])


def _rope_t(dy, tab):
    return (dy * tab[:, 0:LANE] + pltpu.roll(dy * tab[:, LANE:2 * LANE], HALF, 1)
            + pltpu.roll(dy * tab[:, 2 * LANE:3 * LANE], LANE - HALF, 1))


def _mla_prep_fwd(proj, tab, g_q, g_kv):
    rows = proj.shape[0]

    def body(i, ins, fulls, outs, accs):
        outs[0][...] = _rms(ins[0][...], fulls[0][...]).astype(BF16)
        outs[1][...] = _rms(ins[1][...], fulls[1][...]).astype(BF16)
        outs[2][...] = _rope(ins[2][...], ins[3][...])

    return _rowwise(body, name="mla_prep_fwd", rows=rows, tr=512,
                    row_ins=[(proj, MLA_Q_RANK, C_CQ // MLA_Q_RANK), (proj, LANE, C_CKV // LANE),
                             (proj, LANE, C_KR // LANE), (tab, 3 * LANE, 0)],
                    full_ins=[g_q, g_kv], row_outs=[(MLA_Q_RANK, BF16), (MLA_KV_RANK, BF16), (LANE, F32)])


def _mla_prep_bwd(proj, g_q, g_kv, dcqn, dckvn):
    rows = proj.shape[0]

    def body(i, ins, fulls, outs, accs):
        _acc_init(i, accs)
        dcq, dgq = _rms_bwd(ins[0][...], fulls[0][...], ins[2][...])
        dckv, dgkv = _rms_bwd(ins[1][...], fulls[1][...], ins[3][...])
        outs[0][...] = dcq.astype(BF16)
        outs[1][...] = dckv.astype(BF16)
        accs[0][...] += dgq
        accs[1][...] += dgkv

    return _rowwise(body, name="mla_prep_bwd", rows=rows, tr=512,
                    row_ins=[(proj, MLA_Q_RANK, C_CQ // MLA_Q_RANK), (proj, LANE, C_CKV // LANE),
                             (dcqn, MLA_Q_RANK, 0), (dckvn, MLA_KV_RANK, 0)],
                    full_ins=[g_q, g_kv], row_outs=[(MLA_Q_RANK, BF16), (MLA_KV_RANK, BF16)],
                    acc_outs=[((1, MLA_Q_RANK), F32), ((1, MLA_KV_RANK), F32)])


def _mla_rope_fwd(qp, kvp, krope, tab):
    rows = qp.shape[0]
    hw = MLA_HEADS * LANE

    def body(i, ins, fulls, outs, accs):
        t = ins[3][...]
        kr = ins[2][...]
        for h in range(MLA_HEADS):
            sl = slice(h * LANE, (h + 1) * LANE)
            outs[0][:, sl] = _rope(ins[0][:, sl], t).astype(BF16)
            outs[1][:, sl] = (ins[1][:, sl] + kr).astype(BF16)
        outs[2][...] = ins[1][:, hw:2 * hw].astype(BF16)

    return _rowwise(body, name="mla_rope_fwd", rows=rows, tr=512,
                    row_ins=[(qp, hw, 0), (kvp, 2 * hw, 0), (krope, LANE, 0), (tab, 3 * LANE, 0)],
                    row_outs=[(hw, BF16), (hw, BF16), (hw, BF16)])


def _mla_rope_bwd(dq, dk, dv, tab):
    rows = dq.shape[0]
    hw = MLA_HEADS * LANE

    def body(i, ins, fulls, outs, accs):
        t = ins[3][...]
        dkr = jnp.zeros((ins[0].shape[0], LANE), F32)
        for h in range(MLA_HEADS):
            sl = slice(h * LANE, (h + 1) * LANE)
            outs[0][:, sl] = _rope_t(ins[0][:, sl], t).astype(BF16)
            dkr = dkr + ins[1][:, sl]
        outs[1][:, 0:hw] = ins[1][...].astype(BF16)
        outs[1][:, hw:2 * hw] = ins[2][...].astype(BF16)
        lane = lax.broadcasted_iota(jnp.int32, dkr.shape, 1)
        dkr = jnp.where((lane >= ROPE_LO) & (lane < ROPE_LO + MLA_ROPE), dkr, 0.0)
        outs[2][...] = _rope_t(dkr, t).astype(BF16)

    return _rowwise(body, name="mla_rope_bwd", rows=rows, tr=512,
                    row_ins=[(dq, hw, 0), (dk, hw, 0), (dv, hw, 0), (tab, 3 * LANE, 0)],
                    row_outs=[(hw, BF16), (2 * hw, BF16), (LANE, BF16)])


def _dot_nt(a, b):
    return lax.dot_general(a, b, (((1,), (1,)), ((), ())), preferred_element_type=F32)


def _dot_tn(a, b):
    return lax.dot_general(a, b, (((0,), (0,)), ((), ())), preferred_element_type=F32)


def _dot(a, b):
    return jnp.dot(a, b, preferred_element_type=F32)


def _attn_params(s, t, n_res_f32, n_res_bf16):
    blocks = n_res_f32 * _nbytes((s, LANE), F32) + n_res_bf16 * _nbytes((s, LANE), BF16) + 6 * _nbytes((t, LANE), F32)
    return pltpu.CompilerParams(dimension_semantics=("parallel", "arbitrary"),
                                vmem_limit_bytes=_vmem_limit(blocks, 12 * _nbytes((t, t), F32)))


def _mla_fwd(q, k, v, t):
    s, hw = q.shape
    heads, nq = hw // LANE, s // t
    scale = 1.0 / math.sqrt(MLA_NOPE + MLA_ROPE)
    scale2 = scale * LOG2E

    def body(q_ref, k_ref, v_ref, o_ref, l_ref):
        i = pl.program_id(1)
        qv = q_ref[...]

        def step(j, carry, masked):
            m, l, acc = carry
            sl = pl.ds(pl.multiple_of(j * t, t), t)
            sc = _dot_nt(qv, k_ref[sl, :])
            if masked:
                row = lax.broadcasted_iota(jnp.int32, (t, t), 0)
                col = lax.broadcasted_iota(jnp.int32, (t, t), 1)
                sc = jnp.where(col <= row, sc, -1e30)
            m_new = jnp.maximum(m, jnp.max(sc, axis=1, keepdims=True))
            p = jnp.exp2((sc - m_new) * scale2)
            alpha = jnp.exp2((m - m_new) * scale2)
            l = alpha * l + jnp.sum(p, axis=1, keepdims=True)
            acc = alpha * acc + _dot(p.astype(BF16), v_ref[sl, :])
            return m_new, l, acc

        init = (jnp.full((t, 1), -1e30, F32), jnp.zeros((t, 1), F32), jnp.zeros((t, LANE), F32))
        carry = lax.fori_loop(0, i, lambda j, c: step(j, c, False), init)
        m, l, acc = step(i, carry, True)
        o_ref[...] = (acc / l).astype(o_ref.dtype)
        l_ref[0] = m * scale2 + jnp.log2(l)

    blk = pl.BlockSpec((t, LANE), lambda h, i: (i, h))
    res = pl.BlockSpec((s, LANE), lambda h, i: (0, h))
    return pl.pallas_call(
        body, name="mla_fwd", grid=(heads, nq), in_specs=[blk, res, res],
        out_specs=[blk, pl.BlockSpec((1, t, 1), lambda h, i: (h, i, 0))],
        out_shape=[pltpu.HBM((s, hw), BF16), pltpu.HBM((heads, s, 1), F32)],
        compiler_params=_attn_params(s, t, 0, 2),
    )(q, k, v)


def _mla_bwd(q, k, v, o, do, lse, t):
    s, hw = q.shape
    heads, nq = hw // LANE, s // t
    scale = 1.0 / math.sqrt(MLA_NOPE + MLA_ROPE)
    scale2 = scale * LOG2E

    def body(q_ref, k_ref, v_ref, o_ref, do_ref, l_ref, dq_ref, dk_ref, dv_ref):
        i = pl.program_id(1)

        @pl.when(i == 0)
        def _():
            dk_ref[...] = jnp.zeros_like(dk_ref)
            dv_ref[...] = jnp.zeros_like(dv_ref)

        qv, dov, lv = q_ref[...], do_ref[...], l_ref[0]
        dlt = jnp.sum(dov.astype(F32) * o_ref[...].astype(F32), axis=1, keepdims=True)

        def step(j, dq, masked):
            sl = pl.ds(pl.multiple_of(j * t, t), t)
            kv, vv = k_ref[sl, :], v_ref[sl, :]
            p = jnp.exp2(_dot_nt(qv, kv) * scale2 - lv)
            if masked:
                row = lax.broadcasted_iota(jnp.int32, (t, t), 0)
                col = lax.broadcasted_iota(jnp.int32, (t, t), 1)
                p = jnp.where(col <= row, p, 0.0)
            ds = (p * (_dot_nt(dov, vv) - dlt)).astype(BF16)
            dk_ref[sl, :] += _dot_tn(ds, qv) * scale
            dv_ref[sl, :] += _dot_tn(p.astype(BF16), dov)
            return dq + _dot(ds, kv)

        dq = lax.fori_loop(0, i, lambda j, c: step(j, c, False), jnp.zeros((t, LANE), F32))
        dq_ref[...] = step(i, dq, True) * scale

    blk = pl.BlockSpec((t, LANE), lambda h, i: (i, h))
    res = pl.BlockSpec((s, LANE), lambda h, i: (0, h))
    full = pltpu.HBM((s, hw), F32)
    return pl.pallas_call(
        body, name="mla_bwd", grid=(heads, nq),
        in_specs=[blk, res, res, blk, blk, pl.BlockSpec((1, t, 1), lambda h, i: (h, i, 0))],
        out_specs=[blk, res, res], out_shape=[full, full, full],
        compiler_params=_attn_params(s, t, 2, 2),
    )(q, k, v, o, do, lse)


def _sb_logits(qv, kv, scale, masked, t, upper):
    z = _dot_nt(qv, kv) * scale
    e = jnp.exp(-jnp.abs(z))
    l1p = jnp.log(1.0 + e)
    lb = jnp.minimum(z, 0.0) - l1p
    lo = -jnp.maximum(z, 0.0) - l1p
    keep = None
    if masked:
        row = lax.broadcasted_iota(jnp.int32, (t, t), 0)
        col = lax.broadcasted_iota(jnp.int32, (t, t), 1)
        keep = col < row
        lo = jnp.where(keep, lo, 0.0)
    hi = lo.astype(BF16)
    rem = (lo - hi.astype(F32)).astype(BF16)
    suf = _dot(hi, upper) + _dot(rem, upper)
    return z, e, lb, lo, suf, keep


def _tri(t, inclusive):
    row = lax.broadcasted_iota(jnp.int32, (t, t), 0)
    col = lax.broadcasted_iota(jnp.int32, (t, t), 1)
    return jnp.where((row >= col) if inclusive else (row > col), 1.0, 0.0).astype(BF16)


SB_CHAINS = 2


def _sb_walk(i, first, carries_of):
    n = SB_CHAINS
    carries = [first(c) for c in range(n)]
    width = len(carries[0])

    def alive(carry):
        return jnp.max(carry[0]) >= SB_ZERO_LOG

    def split(st):
        return [tuple(st[1 + c * width:1 + (c + 1) * width]) for c in range(n)]

    def live(st):
        any_alive = alive(split(st)[0])
        for cr in split(st)[1:]:
            any_alive = any_alive | alive(cr)
        return (st[0] <= n * i) & any_alive

    def more(st):
        out = (st[0] + 1,)
        for c, cr in enumerate(split(st)):
            out += tuple(carries_of(c, st[0], cr))
        return out

    st = lax.while_loop(live, more, (jnp.int32(1),) + tuple(x for cr in carries for x in cr))
    jj, carries = st[0], split(st)
    for c in range(1, n):
        def live_c(s2, c=c):
            return (s2[0] <= n * i + c) & alive(s2[1:])

        def more_c(s2, c=c):
            return (s2[0] + 1,) + tuple(carries_of(c, s2[0], s2[1:]))

        carries[c] = lax.while_loop(live_c, more_c, (jj,) + tuple(carries[c]))[1:]
    return carries


def _sb_fwd(proj, t):
    s = proj.shape[0]
    heads, nq, n = SB_HEADS, s // t, SB_CHAINS
    scale = 1.0 / math.sqrt(SB_HEAD_DIM)

    def body(q_ref, k_ref, v_ref, o_ref):
        i = pl.program_id(1)
        upper = _tri(t, False)
        qs = [q_ref[c * t:(c + 1) * t, :].astype(BF16) for c in range(n)]

        def step(c, jj, carry, masked):
            run, acc = carry
            sl = pl.ds(pl.multiple_of((n * i + c - jj) * t, t), t)
            _, _, lb, lo, suf, keep = _sb_logits(qs[c], k_ref[sl, :].astype(BF16), scale, masked, t, upper)
            a = jnp.exp(lb + suf + run)
            if masked:
                a = jnp.where(keep, a, 0.0)
            acc = acc + _dot(a.astype(BF16), v_ref[sl, :].astype(BF16))
            return run + jnp.sum(lo, axis=1, keepdims=True), acc

        init = (jnp.zeros((t, 1), F32), jnp.zeros((t, LANE), F32))
        carries = _sb_walk(i, lambda c: step(c, 0, init, True), lambda c, jj, cr: step(c, jj, cr, False))
        for c in range(n):
            o_ref[c * t:(c + 1) * t, :] = carries[c][1]

    return pl.pallas_call(
        body, name="sb_fwd", grid=(heads, nq // n),
        in_specs=[pl.BlockSpec((n * t, LANE), lambda h, i: (i, C_SBQ // LANE + h)),
                  pl.BlockSpec((s, LANE), lambda h, i: (0, C_SBK // LANE + h)),
                  pl.BlockSpec((s, LANE), lambda h, i: (0, C_SBV // LANE + h))],
        out_specs=pl.BlockSpec((n * t, LANE), lambda h, i: (i, h)),
        out_shape=pltpu.HBM((s, heads * LANE), F32),
        compiler_params=_attn_params(s, n * t, 2, 0),
    )(proj, proj, proj)


def _sb_bwd(proj, o, do, t):
    s = proj.shape[0]
    heads, nq, n = SB_HEADS, s // t, SB_CHAINS
    scale = 1.0 / math.sqrt(SB_HEAD_DIM)

    def body(q_ref, k_ref, v_ref, o_ref, do_ref, dq_ref, dk_ref, dv_ref):
        i = pl.program_id(1)

        @pl.when(i == 0)
        def _():
            dk_ref[...] = jnp.zeros_like(dk_ref)
            dv_ref[...] = jnp.zeros_like(dv_ref)

        rows = [slice(c * t, (c + 1) * t) for c in range(n)]
        qs = [q_ref[r, :].astype(BF16) for r in rows]
        dos = [do_ref[r, :] for r in rows]
        totals = [jnp.sum(dos[c].astype(F32) * o_ref[rows[c], :], axis=1, keepdims=True) for c in range(n)]
        upper, upper_incl = _tri(t, False), _tri(t, True)

        def step(c, jj, carry, masked):
            run, g, dq = carry
            qv, dov = qs[c], dos[c]
            sl = pl.ds(pl.multiple_of((n * i + c - jj) * t, t), t)
            kv, vv = k_ref[sl, :].astype(BF16), v_ref[sl, :].astype(BF16)
            z, e, lb, lo, suf, keep = _sb_logits(qv, kv, scale, masked, t, upper)
            tail = suf + run
            a = jnp.exp(lb + tail)
            if masked:
                a = jnp.where(keep, a, 0.0)
            ab = a.astype(BF16)
            gr = ab.astype(F32) * _dot_nt(dov, vv)
            ghi = gr.astype(BF16)
            grem = (gr - ghi.astype(F32)).astype(BF16)
            before = totals[c] - g - (_dot(ghi, upper_incl) + _dot(grem, upper_incl))
            before = jnp.where(tail < SB_ZERO_LOG, 0.0, before)
            r = 1.0 / (1.0 + e)
            pos = z >= 0.0
            dz = r * (gr * jnp.where(pos, e, 1.0) - before * jnp.where(pos, 1.0, e))
            if masked:
                dz = jnp.where(keep, dz, 0.0)
            dzb = (dz * scale).astype(BF16)
            dk_ref[sl, :] += _dot_tn(dzb, qv)
            dv_ref[sl, :] += _dot_tn(ab, dov)
            return (run + jnp.sum(lo, axis=1, keepdims=True), g + jnp.sum(gr, axis=1, keepdims=True),
                    dq + _dot(dzb, kv))

        zero = jnp.zeros((t, 1), F32)
        init = (zero, zero, jnp.zeros((t, LANE), F32))
        carries = _sb_walk(i, lambda c: step(c, 0, init, True), lambda c, jj, cr: step(c, jj, cr, False))
        for c in range(n):
            dq_ref[rows[c], :] = carries[c][2]

    blk = pl.BlockSpec((n * t, LANE), lambda h, i: (i, h))
    res = pl.BlockSpec((s, LANE), lambda h, i: (0, h))
    full = pltpu.HBM((s, heads * LANE), F32)
    return pl.pallas_call(
        body, name="sb_bwd", grid=(heads, nq // n),
        in_specs=[pl.BlockSpec((n * t, LANE), lambda h, i: (i, C_SBQ // LANE + h)),
                  pl.BlockSpec((s, LANE), lambda h, i: (0, C_SBK // LANE + h)),
                  pl.BlockSpec((s, LANE), lambda h, i: (0, C_SBV // LANE + h)), blk, blk],
        out_specs=[blk, res, res], out_shape=[full, full, full],
        compiler_params=_attn_params(s, n * t, 4, 0),
    )(proj, proj, proj, o, do)


def _xattn_probs(qh, kh):
    sc = _dot_nt(qh, kh) * (1.0 / math.sqrt(X_HEAD_DIM))
    p = jnp.exp(sc - jnp.max(sc, axis=1, keepdims=True))
    return p / jnp.sum(p, axis=1, keepdims=True)


def _xattn_fwd(xq, xkv):
    rows = xq.shape[0]
    w = X_HEADS * X_HEAD_DIM

    def body(i, ins, fulls, outs, accs):
        for h in range(X_HEADS):
            sl = slice(h * LANE, (h + 1) * LANE)
            p = _xattn_probs(ins[0][:, sl], fulls[0][:, sl])
            outs[0][:, sl] = _dot(p.astype(BF16), fulls[0][:, w + h * LANE:w + (h + 1) * LANE]).astype(BF16)

    return _rowwise(body, name="xattn_fwd", rows=rows, tr=512, row_ins=[(xq, w, 0)], full_ins=[xkv],
                    row_outs=[(w, BF16)])[0]


def _xattn_bwd(xq, xkv, dxo):
    rows = xq.shape[0]
    w = X_HEADS * X_HEAD_DIM

    def body(i, ins, fulls, outs, accs):
        _acc_init(i, accs)
        for h in range(X_HEADS):
            sl = slice(h * LANE, (h + 1) * LANE)
            slv = slice(w + h * LANE, w + (h + 1) * LANE)
            qh, kh, vh, doh = ins[0][:, sl], fulls[0][:, sl], fulls[0][:, slv], ins[1][:, sl]
            p = _xattn_probs(qh, kh)
            dp = _dot_nt(doh, vh)
            ds = (p * (dp - jnp.sum(p * dp, axis=1, keepdims=True)) * (1.0 / math.sqrt(X_HEAD_DIM))).astype(BF16)
            outs[0][:, sl] = _dot(ds, kh).astype(BF16)
            accs[0][:, sl] += _dot_tn(ds, qh)
            accs[0][:, slv] += _dot_tn(p.astype(BF16), doh)

    return _rowwise(body, name="xattn_bwd", rows=rows, tr=512, row_ins=[(xq, w, 0), (dxo, w, 0)], full_ins=[xkv],
                    row_outs=[(w, BF16)], acc_outs=[(xkv.shape, F32)])


def _gate_fwd(proj, pa, pb, b_gate):
    rows = proj.shape[0]

    def body(i, ins, fulls, outs, accs):
        sa = _sigmoid(ins[0][...] + fulls[0][0:1, :])
        sb = _sigmoid(ins[1][...] + fulls[0][1:2, :])
        outs[0][...] = (sa * ins[2][...] + sb * ins[3][...]).astype(BF16)

    return _rowwise(body, name="gate_fwd", rows=rows, tr=512,
                    row_ins=[(proj, D_MODEL, C_GA // D_MODEL), (proj, D_MODEL, C_GB // D_MODEL), (pa, D_MODEL, 0),
                             (pb, D_MODEL, 0)],
                    full_ins=[b_gate], row_outs=[(D_MODEL, BF16)])[0]


def _gate_bwd(proj, pa, pb, b_gate, dm):
    rows = proj.shape[0]

    def body(i, ins, fulls, outs, accs):
        _acc_init(i, accs)
        d = ins[4][...]
        sa = _sigmoid(ins[0][...] + fulls[0][0:1, :])
        sb = _sigmoid(ins[1][...] + fulls[0][1:2, :])
        dga = d * ins[2][...] * sa * (1.0 - sa)
        dgb = d * ins[3][...] * sb * (1.0 - sb)
        outs[0][...] = (d * sa).astype(BF16)
        outs[1][...] = (d * sb).astype(BF16)
        outs[2][...] = dga.astype(BF16)
        outs[3][...] = dgb.astype(BF16)
        accs[0][0:1, :] += jnp.sum(dga, axis=0, keepdims=True)
        accs[0][1:2, :] += jnp.sum(dgb, axis=0, keepdims=True)

    return _rowwise(body, name="gate_bwd", rows=rows, tr=512,
                    row_ins=[(proj, D_MODEL, C_GA // D_MODEL), (proj, D_MODEL, C_GB // D_MODEL), (pa, D_MODEL, 0),
                             (pb, D_MODEL, 0), (dm, D_MODEL, 0)],
                    full_ins=[b_gate], row_outs=[(D_MODEL, BF16)] * 4, acc_outs=[((2, D_MODEL), F32)])


def _loss_head(x3, target, g_final):
    rows = x3.shape[0]

    def body(i, ins, fulls, outs, accs):
        _acc_init(i, accs)
        xv, g = ins[0][...], fulls[0][...]
        d = _rms(xv, g) - ins[1][...]
        dx, dg = _rms_bwd(xv, g, d * (1.0 / D_MODEL))
        outs[0][...] = dx
        accs[0][...] += dg
        accs[1][...] += jnp.sum(d * d, axis=0, keepdims=True)

    return _rowwise(body, name="loss_head", rows=rows, tr=512, row_ins=[(x3, D_MODEL, 0), (target, D_MODEL, 0)],
                    full_ins=[g_final], row_outs=[(D_MODEL, F32)], acc_outs=[((1, D_MODEL), F32), ((1, D_MODEL), F32)])


def _adamw(w, g, m, v, name):
    rows, c = w.shape

    def body(i, ins, fulls, outs, accs):
        wv, gv = ins[0][...], ins[1][...]
        mn = ADAM_B1 * ins[2][...] + (1.0 - ADAM_B1) * gv
        vn = ADAM_B2 * ins[3][...] + (1.0 - ADAM_B2) * jnp.square(gv)
        m_hat = mn / (1.0 - ADAM_B1 ** ADAM_STEP)
        v_hat = vn / (1.0 - ADAM_B2 ** ADAM_STEP)
        outs[0][...] = -ADAM_LR * (m_hat / (jnp.sqrt(v_hat) + ADAM_EPS) + ADAM_WD * wv)
        outs[1][...] = mn
        outs[2][...] = vn

    return _rowwise(body, name=name, rows=rows, tr=_row_tile(rows, 256), row_ins=[(a, c, 0) for a in (w, g, m, v)],
                    row_outs=[(c, F32)] * 3)


def _place():
    x, y, c = lax.axis_index("x"), lax.axis_index("y"), lax.axis_index("c")
    chips = [(1 - x, y), (x, 1 - y), (1 - x, 1 - y)]
    return x, y, c, chips


ANY = pl.BlockSpec(memory_space=pl.ANY)


def _remote(src, dst, send_sem, recv_sem, to):
    return pltpu.make_async_remote_copy(src_ref=src, dst_ref=dst, send_sem=send_sem, recv_sem=recv_sem,
                                        device_id=to, device_id_type=MESH)


def _dma_sems(n):
    return [pltpu.SemaphoreType.DMA((n,)), pltpu.SemaphoreType.DMA((n,))]


def _all_gather_weights(shards):
    n = len(shards)

    def body(*refs):
        ins, outs = refs[:n], refs[n:2 * n]
        send_sems, recv_sems = refs[2 * n:]
        x, y, c, chips = _place()
        me = 2 * x + y

        def copy(t, j, chip_idx, hlf, to, src=None):
            h = ins[t].shape[0] // 2
            dst = outs[t].at[chip_idx, pl.ds(hlf * h, h), :]
            return _remote(dst if src is None else src, dst, send_sems.at[6 * t + j], recv_sems.at[6 * t + j], to)

        first = []
        for t in range(n):
            h = ins[t].shape[0] // 2
            for j, chip in enumerate(chips):
                first.append(copy(t, j, me, c, (*chip, c), src=ins[t].at[pl.ds(c * h, h), :]))
                first[-1].start()
        passed = []
        for t in range(n):
            for j, chip in enumerate(chips):
                idx = 2 * chip[0] + chip[1]
                copy(t, j, idx, c, (x, y, c)).wait_recv()
                passed.append(copy(t, 3 + j, idx, c, (x, y, 1 - c)))
                passed[-1].start()
        for t in range(n):
            for j, chip in enumerate(chips):
                copy(t, 3 + j, 2 * chip[0] + chip[1], 1 - c, (x, y, c)).wait_recv()
        for cp in first + passed:
            cp.wait_send()

    return pl.pallas_call(
        body, name="all_gather_weights", in_specs=[ANY] * n, out_specs=[ANY] * n,
        out_shape=[pltpu.HBM((N_CHIPS,) + s.shape, s.dtype) for s in shards],
        scratch_shapes=_dma_sems(6 * n),
    )(*shards)


def _pair_exchange_grads(stacked):
    n = len(stacked)

    def body(*refs):
        ins, outs = refs[:n], refs[n:2 * n]
        send_sems, recv_sems = refs[2 * n:]
        x, y, c, _ = _place()
        cps = []
        for t in range(n):
            h = ins[t].shape[1] // 2
            cps.append(_remote(ins[t].at[:, pl.ds((1 - c) * h, h), :], outs[t], send_sems.at[t], recv_sems.at[t],
                               (x, y, 1 - c)))
            cps[-1].start()
        for cp in cps:
            cp.wait()

    return pl.pallas_call(
        body, name="pair_exchange_grads", in_specs=[ANY] * n, out_specs=[ANY] * n,
        out_shape=[pltpu.HBM((N_CHIPS, s.shape[1] // 2, s.shape[2]), s.dtype) for s in stacked],
        scratch_shapes=_dma_sems(n),
    )(*stacked)


def _chip_scatter(parts):
    n = len(parts)

    def body(*refs):
        ins, outs = refs[:n], refs[n:2 * n]
        send_sems, recv_sems = refs[2 * n:]
        x, y, c, chips = _place()
        cps = []
        for t in range(n):
            for j, chip in enumerate(chips):
                cps.append(_remote(ins[t].at[2 * chip[0] + chip[1]], outs[t].at[j], send_sems.at[3 * t + j],
                                   recv_sems.at[3 * t + j], (*chip, c)))
                cps[-1].start()
        for cp in cps:
            cp.wait()

    return pl.pallas_call(
        body, name="chip_scatter", in_specs=[ANY] * n, out_specs=[ANY] * n,
        out_shape=[pltpu.HBM((N_CHIPS - 1,) + s.shape[1:], s.dtype) for s in parts],
        scratch_shapes=_dma_sems(3 * n),
    )(*parts)


def _pair_exchange_halves(shards):
    n = len(shards)

    def body(*refs):
        bufs = refs[n:2 * n]
        send_sems, recv_sems = refs[2 * n:]
        x, y, c, _ = _place()
        cps = []
        for t in range(n):
            h = bufs[t].shape[0] // 2
            rows = bufs[t].at[pl.ds(c * h, h), :]
            cps.append(_remote(rows, rows, send_sems.at[t], recv_sems.at[t], (x, y, 1 - c)))
            cps[-1].start()
        for cp in cps:
            cp.wait()

    return pl.pallas_call(
        body, name="pair_exchange_halves", in_specs=[ANY] * n, out_specs=[ANY] * n,
        out_shape=[pltpu.HBM(s.shape, s.dtype) for s in shards],
        input_output_aliases={t: t for t in range(n)},
        scratch_shapes=_dma_sems(n),
    )(*shards)


def _pair_sum(gs, recv, place, name):
    _, r, cols = gs.shape
    h = r // 2

    def kern(p_ref, a_ref, b_ref, o_ref):
        o_ref[...] = (a_ref[...] + b_ref[...]).astype(BF16)

    blk = lambda f: pl.BlockSpec((1, h, cols), f)
    return pl.pallas_call(
        kern, name=name,
        grid_spec=pltpu.PrefetchScalarGridSpec(
            num_scalar_prefetch=1, grid=(N_CHIPS,),
            in_specs=[blk(lambda d, p: (d, p[1], 0)), blk(lambda d, p: (d, 0, 0))],
            out_specs=blk(lambda d, p: (d, 0, 0))),
        out_shape=pltpu.HBM((N_CHIPS, h, cols), BF16),
        compiler_params=pltpu.CompilerParams(dimension_semantics=("arbitrary",),
                                             vmem_limit_bytes=_vmem_limit(3 * _nbytes((h, cols), F32), 0)),
    )(place, gs, recv)


def _chip_sum(gs, recv, got, place, name):
    _, r, cols = gs.shape
    h = r // 2

    def kern(p_ref, a_ref, b_ref, g0, g1, g2, o_ref):
        own = a_ref[0] + b_ref[0]
        o_ref[...] = ((own + g0[0].astype(F32)) + g1[0].astype(F32)) + g2[0].astype(F32)

    blk = lambda f: pl.BlockSpec((1, h, cols), f)
    return pl.pallas_call(
        kern, name=name,
        grid_spec=pltpu.PrefetchScalarGridSpec(
            num_scalar_prefetch=1, grid=(1,),
            in_specs=[blk(lambda i, p: (p[0], p[1], 0)), blk(lambda i, p: (p[0], 0, 0)), blk(lambda i, p: (0, 0, 0)),
                      blk(lambda i, p: (1, 0, 0)), blk(lambda i, p: (2, 0, 0))],
            out_specs=pl.BlockSpec((h, cols), lambda i, p: (p[1], 0))),
        out_shape=pltpu.HBM((r, cols), F32),
        compiler_params=pltpu.CompilerParams(dimension_semantics=("arbitrary",),
                                             vmem_limit_bytes=_vmem_limit(5 * _nbytes((h, cols), F32), 0)),
    )(place, gs, recv, got, got, got)


def _all_reduce_small(vec, name):
    r, cols = vec.shape

    def body(in_ref, out_ref, gath, send_sems, recv_sems):
        x, y, c, _ = _place()
        me = 4 * x + 2 * y + c
        gath[me] = in_ref[...]
        sends = []
        for k in range(1, 8):
            to = (x ^ (k >> 2), y ^ ((k >> 1) & 1), c ^ (k & 1))
            cp = pltpu.make_async_remote_copy(src_ref=in_ref, dst_ref=gath.at[me], send_sem=send_sems.at[k - 1],
                                              recv_sem=recv_sems.at[k - 1], device_id=to, device_id_type=MESH)
            cp.start()
            sends.append(cp)
        for k in range(1, 8):
            peer = me ^ k
            pltpu.make_async_remote_copy(src_ref=in_ref, dst_ref=gath.at[peer], send_sem=send_sems.at[k - 1],
                                         recv_sem=recv_sems.at[k - 1], device_id=(x, y, c),
                                         device_id_type=MESH).wait_recv()
        for cp in sends:
            cp.wait_send()
        acc = gath[0]
        for d in range(1, 8):
            acc = acc + gath[d]
        out_ref[...] = acc

    vm = pl.BlockSpec(memory_space=pltpu.VMEM)
    return pl.pallas_call(
        body, name=name, in_specs=[vm], out_specs=vm,
        out_shape=jax.ShapeDtypeStruct((r, cols), F32),
        scratch_shapes=[pltpu.VMEM((8, r, cols), F32), pltpu.SemaphoreType.DMA((7,)), pltpu.SemaphoreType.DMA((7,))],
    )(vec)


def _pad_heads(w, heads, dim, axis):
    shp = w.shape[:axis] + (heads, dim) + w.shape[axis + 1:]
    pad = [(0, 0)] * len(shp)
    pad[axis + 1] = (0, LANE - dim)
    w = jnp.pad(w.reshape(shp), pad)
    return w.reshape(w.shape[:axis] + (heads * LANE,) + w.shape[axis + 2:])


def _unpad_heads(w, heads, dim, axis):
    shp = w.shape[:axis] + (heads, LANE) + w.shape[axis + 1:]
    w = lax.slice_in_dim(w.reshape(shp), 0, dim, axis=axis + 1)
    return w.reshape(w.shape[:axis] + (heads * dim,) + w.shape[axis + 2:])


def _w_in_layout(w_in):
    kr = jnp.pad(w_in[:, 384:416], ((0, 0), (ROPE_LO, LANE - ROPE_LO - MLA_ROPE)))
    sb = lambda lo: _pad_heads(w_in[:, lo:lo + 512], SB_HEADS, SB_HEAD_DIM, 1)
    return jnp.concatenate([w_in[:, 1952:2976], w_in[:, 2976:4000], sb(416), sb(928), sb(1440), w_in[:, 0:256],
                            w_in[:, 256:384], kr], axis=1)


def _w_in_unlayout(d):
    sb = lambda lo: _unpad_heads(d[:, lo:lo + 1024], SB_HEADS, SB_HEAD_DIM, 1)
    return jnp.concatenate([d[:, C_CQ:C_CQ + 256], d[:, C_CKV:C_CKV + 128], d[:, C_KR + ROPE_LO:C_KR + ROPE_LO + MLA_ROPE],
                            sb(C_SBQ), sb(C_SBK), sb(C_SBV), d[:, C_GA:C_GA + 1024], d[:, C_GB:C_GB + 1024]], axis=1)


def _w_ukv_layout(w):
    w3 = w.reshape(MLA_KV_RANK, MLA_HEADS, MLA_NOPE + MLA_V)
    pad = lambda part: jnp.pad(part, ((0, 0), (0, 0), (0, LANE - part.shape[2]))).reshape(MLA_KV_RANK, MLA_HEADS * LANE)
    return jnp.concatenate([pad(w3[:, :, :MLA_NOPE]), pad(w3[:, :, MLA_NOPE:])], axis=1)


def _w_ukv_unlayout(d):
    hw = MLA_HEADS * LANE
    kpart = d[:, :hw].reshape(MLA_KV_RANK, MLA_HEADS, LANE)[:, :, :MLA_NOPE]
    vpart = d[:, hw:].reshape(MLA_KV_RANK, MLA_HEADS, LANE)[:, :, :MLA_V]
    return jnp.concatenate([kpart, vpart], axis=2).reshape(MLA_KV_RANK, MLA_HEADS * (MLA_NOPE + MLA_V))


def _shard_of(full, d, axis):
    n = full.shape[axis] // N_CHIPS
    return lax.slice_in_dim(full, d * n, (d + 1) * n, axis=axis)


def _local_step(x, mem, pos, target, w, t_mla, t_sb):
    s = x.shape[0]
    win = _w_in_layout(w["w_in"])
    wuq = _pad_heads(w["w_uq"], MLA_HEADS, MLA_NOPE + MLA_ROPE, 1)
    wkv = _w_ukv_layout(w["w_ukv"])
    wa = _pad_heads(w["w_a_proj"], MLA_HEADS, MLA_V, 0)
    wb = _pad_heads(w["w_b_proj"], SB_HEADS, SB_HEAD_DIM, 0)
    inv_freq = ROPE_THETA ** (-jnp.arange(0, MLA_ROPE, 2, dtype=F32) / MLA_ROPE)
    freq_lane = jnp.pad(jnp.concatenate([inv_freq, inv_freq]), (ROPE_LO, LANE - ROPE_LO - MLA_ROPE)).reshape(1, LANE)
    add = lambda accs, ex: (accs[0] + ex[0],)

    tab = _rope_tables(pos.reshape(s, 1), freq_lane)
    h = _rms_fwd_call(x, w["g_mix"], "rms_mix")
    proj = _mm(h, [win], name="proj_in", tn=1408)
    cqn, ckvn, krope = _mla_prep_fwd(proj, tab, w["g_q_lat"], w["g_kv_lat"])
    qp = _mm(cqn, [wuq], name="q_up")
    kvp = _mm(ckvn, [wkv], name="kv_up")
    qa, ka, va = _mla_rope_fwd(qp, kvp, krope, tab)
    o_a, lse = _mla_fwd(qa, ka, va, t_mla)
    o_b = _sb_fwd(proj, t_sb)
    pa = _mm(o_a, [wa], name="proj_a")
    pb = _mm(o_b, [wb], name="proj_b")
    merged = _gate_fwd(proj, pa, pb, w["b_gate"])
    x1 = _mm(merged, [w["w_o"]], name="proj_o", extras=(x,), epilogue=add)
    hx = _rms_fwd_call(x1, w["g_x"], "rms_x")
    mn = _rms_fwd_call(mem, w["g_mem"], "rms_mem")
    xq = _mm(hx, [w["w_xq"]], name="xq", out_dtypes=(BF16,))
    xkv = _mm(mn, [w["w_xkv"]], name="xkv", out_dtypes=(BF16,))
    xo = _xattn_fwd(xq, xkv)
    x2 = _mm(xo, [w["w_xo"]], name="proj_xo", extras=(x1,), epilogue=add)
    hf = _rms_fwd_call(x2, w["g_ffn"], "rms_ffn")

    def swiglu(accs, ex):
        a, b = accs
        return a, b, a * _sigmoid(a) * b

    ga, gu, hmid = _mm(hf, [w["w_gate"], w["w_up"]], name="ffn_up", epilogue=swiglu, out_dtypes=(BF16, BF16, BF16),
                       tm=512, tn=1408)
    x3 = _mm(hmid, [w["w_down"]], name="ffn_down", extras=(x2,), epilogue=add, tk=2816)

    dx3, dg_final, sq = _loss_head(x3, target, w["g_final"].reshape(1, D_MODEL))
    g = {"g_final": dg_final.reshape(D_MODEL)}

    def swiglu_bwd(accs, ex):
        dh, a, b = accs[0], ex[0].astype(F32), ex[1].astype(F32)
        sg = _sigmoid(a)
        return dh * b * sg * (1.0 + a * (1.0 - sg)), dh * a * sg

    da, db = _mm(dx3, [w["w_down"]], name="ffn_down_dx", tb=True, extras=(ga, gu), epilogue=swiglu_bwd,
                 out_dtypes=(BF16, BF16), tm=512, tn=1408)
    g["w_down"] = _mm(hmid, [dx3], name="ffn_down_dw", ta=True, tm=1408)
    g["w_gate"] = _mm(hf, [da], name="ffn_gate_dw", ta=True, tn=1408, tk=2048)
    g["w_up"] = _mm(hf, [db], name="ffn_up_dw", ta=True, tn=1408)
    dhf = _mm(da, [w["w_gate"]], name="ffn_gate_dx", tb=True, tk=2816)
    dhf = _mm(db, [w["w_up"]], name="ffn_up_dx", tb=True, extras=(dhf,), epilogue=add, tk=2816)
    dx2, g["g_ffn"] = _rms_bwd_call(x2, w["g_ffn"], dhf, dx3, "rms_ffn_bwd")

    dxo = _mm(dx2, [w["w_xo"]], name="proj_xo_dx", tb=True, out_dtypes=(BF16,))
    g["w_xo"] = _mm(xo, [dx2], name="proj_xo_dw", ta=True)
    dxq, dxkv = _xattn_bwd(xq, xkv, dxo)
    dhx = _mm(dxq, [w["w_xq"]], name="xq_dx", tb=True)
    g["w_xq"] = _mm(hx, [dxq], name="xq_dw", ta=True)
    dmn = _mm(dxkv, [w["w_xkv"]], name="xkv_dx", tb=True)
    g["w_xkv"] = _mm(mn, [dxkv], name="xkv_dw", ta=True)
    dx1, g["g_x"] = _rms_bwd_call(x1, w["g_x"], dhx, dx2, "rms_x_bwd")
    _, g["g_mem"] = _rms_bwd_call(mem, w["g_mem"], dmn, None, "rms_mem_bwd")

    dmerged = _mm(dx1, [w["w_o"]], name="proj_o_dx", tb=True)
    g["w_o"] = _mm(merged, [dx1], name="proj_o_dw", ta=True)
    dpa, dpb, dga, dgb, g["b_gate"] = _gate_bwd(proj, pa, pb, w["b_gate"], dmerged)
    do_a = _mm(dpa, [wa], name="proj_a_dx", tb=True, out_dtypes=(BF16,))
    do_b = _mm(dpb, [wb], name="proj_b_dx", tb=True, out_dtypes=(BF16,))
    g["w_a_proj"] = _unpad_heads(_mm(o_a, [dpa], name="proj_a_dw", ta=True), MLA_HEADS, MLA_V, 0)
    g["w_b_proj"] = _unpad_heads(_mm(o_b, [dpb], name="proj_b_dw", ta=True), SB_HEADS, SB_HEAD_DIM, 0)

    dsq, dsk, dsv = _sb_bwd(proj, o_b, do_b, t_sb)
    dqa, dka, dva = _mla_bwd(qa, ka, va, o_a, do_a, lse, t_mla)
    dqp, dkvp, dkr = _mla_rope_bwd(dqa, dka, dva, tab)
    g["w_uq"] = _unpad_heads(_mm(cqn, [dqp], name="q_up_dw", ta=True), MLA_HEADS, MLA_NOPE + MLA_ROPE, 1)
    g["w_ukv"] = _w_ukv_unlayout(_mm(ckvn, [dkvp], name="kv_up_dw", ta=True))
    dcqn = _mm(dqp, [wuq], name="q_up_dx", tb=True)
    dckvn = _mm(dkvp, [wkv], name="kv_up_dx", tb=True)
    dcq, dckv, g["g_q_lat"], g["g_kv_lat"] = _mla_prep_bwd(proj, w["g_q_lat"], w["g_kv_lat"], dcqn, dckvn)

    dproj = jnp.concatenate([dga, dgb, dsq.astype(BF16), dsk.astype(BF16), dsv.astype(BF16), dcq, dckv, dkr], axis=1)
    g["w_in"] = _w_in_unlayout(_mm(h, [dproj], name="proj_in_dw", ta=True, tn=1408))
    dh = _mm(dproj, [win], name="proj_in_dx", tb=True, tk=2816)
    grad_x, g["g_mix"] = _rms_bwd_call(x, w["g_mix"], dh, dx1, "rms_mix_bwd")
    return sq, grad_x, g


def _small_pack(d):
    row5 = jnp.concatenate([d["g_q_lat"].reshape(-1), d["g_kv_lat"].reshape(-1), jnp.zeros((640,), F32)])
    rows = [d[n].reshape(-1) for n in ("g_mix", "g_x", "g_mem", "g_ffn", "g_final")] + [row5]
    return rows


def _small_unpack(p, like):
    out = {n: p[i].reshape(like[n].shape) for i, n in enumerate(("g_mix", "g_x", "g_mem", "g_ffn", "g_final"))}
    out["g_q_lat"] = p[5, 0:256].reshape(like["g_q_lat"].shape)
    out["g_kv_lat"] = p[5, 256:384].reshape(like["g_kv_lat"].shape)
    return out


def kernel(x, mem, positions, g_mix, w_in, b_gate, g_q_lat, w_uq, g_kv_lat, w_ukv, w_a_proj, w_b_proj, w_o, g_x, g_mem, w_xq, w_xkv, w_xo, g_ffn, w_gate, w_up, w_down, g_final, loss_target, m_g_mix, m_w_in, m_b_gate, m_g_q_lat, m_w_uq, m_g_kv_lat, m_w_ukv, m_w_a_proj, m_w_b_proj, m_w_o, m_g_x, m_g_mem, m_w_xq, m_w_xkv, m_w_xo, m_g_ffn, m_w_gate, m_w_up, m_w_down, m_g_final, v_g_mix, v_w_in, v_b_gate, v_g_q_lat, v_w_uq, v_g_kv_lat, v_w_ukv, v_w_a_proj, v_w_b_proj, v_w_o, v_g_x, v_g_mem, v_w_xq, v_w_xkv, v_w_xo, v_g_ffn, v_w_gate, v_w_up, v_w_down, v_g_final):
    given = dict(locals())
    names = [n for n, _, _ in MATS] + ["b_gate"] + list(SMALL)
    wts = {n: given[n] for n in names}
    mom = {n: given["m_" + n] for n in names}
    var = {n: given["v_" + n] for n in names}
    shard2d = {n: shp for n, shp, _ in MATS}
    shard2d["b_gate"] = B_GATE_SHARD
    cx, cy, cc = lax.axis_index("x"), lax.axis_index("y"), lax.axis_index("c")
    me = 2 * cx + cy
    place = jnp.stack([me, cc]).astype(jnp.int32)
    bcol = me * B_GATE_SHARD[1]

    own = [wts[n].reshape(shard2d[n]).astype(BF16) for n, _, _ in MATS]
    full = {}
    for (n, shp, ax), g4, mine in zip(MATS, _all_gather_weights(own), own):
        g4 = lax.dynamic_update_slice(g4, mine[None], (me, 0, 0))
        full[n] = g4.reshape(N_CHIPS * shp[0], shp[1]) if ax == 0 else jnp.concatenate(list(g4), axis=1)
    bias_rows = jnp.pad(wts["b_gate"].reshape(B_GATE_SHARD), ((0, SMALL_ROWS - 2), (0, 0)))
    bias_rows = lax.dynamic_update_slice(jnp.zeros((SMALL_ROWS, D_MODEL), F32), bias_rows, (0, bcol))
    full["b_gate"] = _all_reduce_small(jnp.where(cc == 0, bias_rows, 0.0), "all_gather_bias")[0:2]
    for n in SMALL:
        full[n] = wts[n].reshape(1, -1) if n != "g_final" else wts[n]

    sq, grad_x, grads = _local_step(x[0], mem[0], positions[0], loss_target[0], full, t_mla=1024, t_sb=256)

    stacked = [jnp.stack([_shard_of(grads[n], d, ax) for d in range(N_CHIPS)]) for n, _, ax in MATS]
    recv = _pair_exchange_grads(stacked)
    parts = [_pair_sum(gs, rv, place, "pair_sum_" + n) for (n, _, _), gs, rv in zip(MATS, stacked, recv)]
    got = _chip_scatter(parts)
    halves = [_chip_sum(gs, rv, gt, place, "chip_sum_" + n) for (n, _, _), gs, rv, gt in zip(MATS, stacked, recv, got)]
    g_shard = dict(zip([n for n, _, _ in MATS], _pair_exchange_halves(halves)))

    small_rows = _small_pack({n: grads[n] for n in SMALL}) + [sq.reshape(-1), grads["b_gate"][0], grads["b_gate"][1]]
    small_rows += [jnp.zeros((D_MODEL,), F32)] * (SMALL_ROWS - len(small_rows))
    small = _all_reduce_small(jnp.stack(small_rows), "all_reduce_small")
    loss = (0.5 / D_MODEL) * jnp.sum(small[6])
    g_shard["b_gate"] = lax.dynamic_slice(small[7:9], (0, bcol), B_GATE_SHARD)

    out = {"grad": {}, "delta": {}, "m": {}, "v": {}}
    for n in [n for n, _, _ in MATS] + ["b_gate"]:
        shape = wts[n].shape
        r2 = lambda a: a.reshape(shard2d[n])
        d_n, m_n, v_n = _adamw(r2(wts[n]), g_shard[n], r2(mom[n]), r2(var[n]), "adamw_" + n)
        for key, a in (("grad", g_shard[n]), ("delta", d_n), ("m", m_n), ("v", v_n)):
            out[key][n] = a.reshape(shape)
    sp = lambda d: jnp.stack(_small_pack(d) + [jnp.zeros((D_MODEL,), F32)] * 2)
    delta_s, m_s, v_s = _adamw(sp(wts), small[0:8].at[6:8].set(0.0), sp(mom), sp(var), "adamw_small")
    for key, p in (("grad", small), ("delta", delta_s), ("m", m_s), ("v", v_s)):
        out[key].update(_small_unpack(p, wts))

    order = ["g_mix", "w_in", "b_gate", "g_q_lat", "w_uq", "g_kv_lat", "w_ukv", "w_a_proj", "w_b_proj", "w_o", "g_x",
             "g_mem", "w_xq", "w_xkv", "w_xo", "g_ffn", "w_gate", "w_up", "w_down", "g_final"]
    return (loss, grad_x[None], *[out[key][n] for key in ("grad", "delta", "m", "v") for n in order])
```

```python
import functools
import math

import jax
import jax.numpy as jnp
from jax import lax
from jax.experimental import pallas as pl
from jax.experimental.pallas import tpu as pltpu

F32 = jnp.float32
BF16 = jnp.bfloat16
MESH = pl.DeviceIdType.MESH

D_MODEL = 1024
MLA_HEADS = 8
MLA_Q_RANK = 256
MLA_KV_RANK = 128
MLA_NOPE = 64
MLA_ROPE = 32
MLA_V = 64
ROPE_THETA = 10000.0
SB_HEADS = 8
SB_HEAD_DIM = 64
X_HEADS = 4
X_HEAD_DIM = 128
D_FF = 2816
EPS = 1e-6
ADAM_LR = 0.001
ADAM_B1 = 0.9
ADAM_B2 = 0.999
ADAM_EPS = 1e-08
ADAM_WD = 0.01
ADAM_STEP = 10

LANE = 128
LOG2E = 1.4426950408889634
N_CHIPS = 4
VMEM_BYTES = 64 * 1024 * 1024

C_GA, C_GB, C_SBQ, C_SBK, C_SBV, C_CQ, C_CKV, C_KR = 0, 1024, 2048, 3072, 4096, 5120, 5376, 5504
D_INP = 5632
ROPE_LO = MLA_NOPE
HALF = MLA_ROPE // 2

SB_ZERO_LOG = -104.0

MATS = (
    ("w_in", (1024, 1000), 1), ("w_uq", (256, 192), 1), ("w_ukv", (128, 256), 1), ("w_a_proj", (512, 256), 1),
    ("w_b_proj", (512, 256), 1), ("w_o", (256, 1024), 0), ("w_xq", (256, 512), 0), ("w_xkv", (256, 1024), 0),
    ("w_xo", (512, 256), 1), ("w_gate", (1024, 704), 1), ("w_up", (1024, 704), 1), ("w_down", (704, 1024), 0),
)
N_EARLY = 3
B_GATE_SHARD = (2, 256)
SMALL = ("g_mix", "g_x", "g_mem", "g_ffn", "g_final", "g_q_lat", "g_kv_lat")
SMALL_ROWS = 16


def _vmem_limit(block_bytes, temp_bytes):
    est = 2 * block_bytes + temp_bytes + (4 << 20)
    return int(min(max(est, 16 << 20), VMEM_BYTES - (6 << 20)))


def _nbytes(shape, dtype):
    return math.prod(shape) * jnp.dtype(dtype).itemsize


def _row_tile(rows, cap):
    if rows <= cap:
        return rows
    return max(t for t in range(8, cap + 1, 8) if rows % t == 0)


def _tile(n, cap):
    if n <= cap:
        return n
    best = None
    for t in range(LANE, cap + 1, LANE):
        if n % t == 0:
            best = t
    assert best is not None, (n, cap)
    return best


def _mm(a, bs, *, name, ta=False, tb=False, extras=(), epilogue=None, out_dtypes=(F32,), tm=1024, tn=1024, tk=1024):
    bs = tuple(bs)
    m, k = (a.shape[1], a.shape[0]) if ta else a.shape
    n = bs[0].shape[0] if tb else bs[0].shape[1]
    tm, tn, tk = _tile(m, tm), _tile(n, tn), _tile(k, tk)
    assert m % tm == 0 and n % tn == 0 and k % tk == 0
    nk = k // tk
    nb, ne, no = len(bs), len(extras), len(out_dtypes)
    dims = (((0,) if ta else (1,)), ((1,) if tb else (0,))), ((), ())
    if epilogue is None:
        epilogue = lambda accs, ex: (accs[0],)

    def body(*refs):
        a_ref, b_refs, e_refs = refs[0], refs[1:1 + nb], refs[1 + nb:1 + nb + ne]
        o_refs, acc_refs = refs[1 + nb + ne:1 + nb + ne + no], refs[1 + nb + ne + no:]
        if nk == 1:
            av = a_ref[...].astype(BF16)
            accs = [lax.dot_general(av, b_ref[...].astype(BF16), dims, preferred_element_type=F32) for b_ref in b_refs]
            for o_ref, v in zip(o_refs, epilogue(accs, [e[...] for e in e_refs])):
                o_ref[...] = v.astype(o_ref.dtype)
            return
        kk = pl.program_id(2)

        @pl.when(kk == 0)
        def _():
            for acc in acc_refs:
                acc[...] = jnp.zeros_like(acc)

        av = a_ref[...].astype(BF16)
        for b_ref, acc in zip(b_refs, acc_refs):
            acc[...] += lax.dot_general(av, b_ref[...].astype(BF16), dims, preferred_element_type=F32)

        @pl.when(kk == nk - 1)
        def _():
            outs = epilogue([acc[...] for acc in acc_refs], [e[...] for e in e_refs])
            for o_ref, v in zip(o_refs, outs):
                o_ref[...] = v.astype(o_ref.dtype)

    a_spec = pl.BlockSpec((tk, tm), lambda i, j, kk: (kk, i)) if ta else pl.BlockSpec((tm, tk), lambda i, j, kk: (i, kk))
    b_spec = pl.BlockSpec((tn, tk), lambda i, j, kk: (j, kk)) if tb else pl.BlockSpec((tk, tn), lambda i, j, kk: (kk, j))
    mn_spec = pl.BlockSpec((tm, tn), lambda i, j, kk: (i, j))
    blocks = (_nbytes((tm, tk), a.dtype) + sum(_nbytes((tk, tn), b.dtype) for b in bs)
              + sum(_nbytes((tm, tn), e.dtype) for e in extras) + sum(_nbytes((tm, tn), d) for d in out_dtypes))
    temps = (nb + 4) * _nbytes((tm, tn), F32)
    outs = pl.pallas_call(
        body, name=name, grid=(m // tm, n // tn, nk),
        in_specs=[a_spec] + [b_spec] * nb + [mn_spec] * ne,
        out_specs=[mn_spec] * no,
        out_shape=[jax.ShapeDtypeStruct((m, n), d) for d in out_dtypes],
        scratch_shapes=[pltpu.VMEM((tm, tn), F32) for _ in range(nb if nk > 1 else 0)],
        compiler_params=pltpu.CompilerParams(
            dimension_semantics=("parallel", "parallel", "arbitrary"),
            vmem_limit_bytes=_vmem_limit(blocks, temps)),
    )(a, *bs, *extras)
    return outs[0] if no == 1 else outs


def _rowwise(body, *, name, rows, tr, row_ins, full_ins=(), row_outs=(), acc_outs=()):
    tr = min(tr, rows)
    assert rows % tr == 0
    n_ri, n_fi, n_ro = len(row_ins), len(full_ins), len(row_outs)

    def kern(*refs):
        body(pl.program_id(0), refs[:n_ri], refs[n_ri:n_ri + n_fi], refs[n_ri + n_fi:n_ri + n_fi + n_ro],
             refs[n_ri + n_fi + n_ro:])

    in_specs = [pl.BlockSpec((tr, w), functools.partial(lambda i, c: (i, c), c=ci)) for _, w, ci in row_ins]
    in_specs += [pl.BlockSpec(f.shape, lambda i: (0, 0)) for f in full_ins]
    out_specs = [pl.BlockSpec((tr, w), lambda i: (i, 0)) for w, _ in row_outs]
    out_specs += [pl.BlockSpec(s, lambda i: (0, 0)) for s, _ in acc_outs]
    out_shape = [jax.ShapeDtypeStruct((rows, w), d) for w, d in row_outs]
    out_shape += [jax.ShapeDtypeStruct(s, d) for s, d in acc_outs]
    blocks = (sum(_nbytes((tr, w), a.dtype) for a, w, _ in row_ins) + sum(_nbytes(f.shape, f.dtype) for f in full_ins)
              + sum(_nbytes((tr, w), d) for w, d in row_outs) + sum(_nbytes(s, d) for s, d in acc_outs))
    widest = max([w for _, w, _ in row_ins] + [w for w, _ in row_outs])
    outs = pl.pallas_call(
        kern, name=name, grid=(rows // tr,), in_specs=in_specs, out_specs=out_specs, out_shape=out_shape,
        compiler_params=pltpu.CompilerParams(
            dimension_semantics=("arbitrary",) if acc_outs else ("parallel",),
            vmem_limit_bytes=_vmem_limit(blocks, 8 * _nbytes((tr, widest), F32))),
    )(*[a for a, _, _ in row_ins], *full_ins)
    return outs


def _rms(x, g):
    r = lax.rsqrt(jnp.mean(x * x, axis=-1, keepdims=True) + EPS)
    return x * r * g


def _rms_bwd(x, g, dy):
    r = lax.rsqrt(jnp.mean(x * x, axis=-1, keepdims=True) + EPS)
    xh = x * r
    dxh = dy * g
    dx = r * (dxh - xh * jnp.mean(dxh * xh, axis=-1, keepdims=True))
    return dx, jnp.sum(dy * xh, axis=0, keepdims=True)


def _sigmoid(x):
    return 1.0 / (1.0 + jnp.exp(-x))


def _acc_init(i, refs):
    @pl.when(i == 0)
    def _():
        for r in refs:
            r[...] = jnp.zeros_like(r)


def _rms_fwd_call(x, g, name):
    rows, c = x.shape

    def body(i, ins, fulls, outs, accs):
        outs[0][...] = _rms(ins[0][...], fulls[0][...]).astype(BF16)

    return _rowwise(body, name=name, rows=rows, tr=512, row_ins=[(x, c, 0)], full_ins=[g], row_outs=[(c, BF16)])[0]


def _rms_bwd_call(x, g, dy, res, name):
    rows, c = x.shape
    row_ins = [(x, c, 0), (dy, c, 0)] + ([(res, c, 0)] if res is not None else [])

    def body(i, ins, fulls, outs, accs):
        _acc_init(i, accs)
        dx, dg = _rms_bwd(ins[0][...], fulls[0][...], ins[1][...].astype(F32))
        if res is not None:
            dx = dx + ins[2][...]
        outs[0][...] = dx
        accs[0][...] += dg

    return _rowwise(body, name=name, rows=rows, tr=512, row_ins=row_ins, full_ins=[g], row_outs=[(c, F32)],
                    acc_outs=[((1, c), F32)])


def _rope_tables(pos_col, freq_lane):
    rows = pos_col.shape[0]

    def body(i, ins, fulls, outs, accs):
        ang = ins[0][...].astype(F32) * fulls[0][...]
        lane = lax.broadcasted_iota(jnp.int32, ang.shape, 1)
        cos, sin = jnp.cos(ang), jnp.sin(ang)
        first = (lane >= ROPE_LO) & (lane < ROPE_LO + HALF)
        second = (lane >= ROPE_LO + HALF) & (lane < ROPE_LO + MLA_ROPE)
        outs[0][:, 0:LANE] = jnp.where(first | second, cos, 1.0)
        outs[0][:, LANE:2 * LANE] = jnp.where(first, -sin, 0.0)
        outs[0][:, 2 * LANE:3 * LANE] = jnp.where(second, sin, 0.0)

    return _rowwise(body, name="rope_tables", rows=rows, tr=1024, row_ins=[(pos_col, 1, 0)], full_ins=[freq_lane],
                    row_outs=[(3 * LANE, F32)])[0]


def _rope(x, tab):
    return (x * tab[:, 0:LANE] + pltpu.roll(x, LANE - HALF, 1) * tab[:, LANE:2 * LANE]
            + pltpu.roll(x, HALF, 1) * tab[:, 2 * LANE:3 * LANE])


def _rope_t(dy, tab):
    return (dy * tab[:, 0:LANE] + pltpu.roll(dy * tab[:, LANE:2 * LANE], HALF, 1)
            + pltpu.roll(dy * tab[:, 2 * LANE:3 * LANE], LANE - HALF, 1))


def _mla_prep_fwd(proj, tab, g_q, g_kv):
    rows = proj.shape[0]

    def body(i, ins, fulls, outs, accs):
        outs[0][...] = _rms(ins[0][...], fulls[0][...]).astype(BF16)
        outs[1][...] = _rms(ins[1][...], fulls[1][...]).astype(BF16)
        outs[2][...] = _rope(ins[2][...], ins[3][...])

    return _rowwise(body, name="mla_prep_fwd", rows=rows, tr=512,
                    row_ins=[(proj, MLA_Q_RANK, C_CQ // MLA_Q_RANK), (proj, LANE, C_CKV // LANE),
                             (proj, LANE, C_KR // LANE), (tab, 3 * LANE, 0)],
                    full_ins=[g_q, g_kv], row_outs=[(MLA_Q_RANK, BF16), (MLA_KV_RANK, BF16), (LANE, F32)])


def _mla_prep_bwd(proj, g_q, g_kv, dcqn, dckvn):
    rows = proj.shape[0]

    def body(i, ins, fulls, outs, accs):
        _acc_init(i, accs)
        dcq, dgq = _rms_bwd(ins[0][...], fulls[0][...], ins[2][...])
        dckv, dgkv = _rms_bwd(ins[1][...], fulls[1][...], ins[3][...])
        outs[0][...] = dcq.astype(BF16)
        outs[1][...] = dckv.astype(BF16)
        accs[0][...] += dgq
        accs[1][...] += dgkv

    return _rowwise(body, name="mla_prep_bwd", rows=rows, tr=512,
                    row_ins=[(proj, MLA_Q_RANK, C_CQ // MLA_Q_RANK), (proj, LANE, C_CKV // LANE),
                             (dcqn, MLA_Q_RANK, 0), (dckvn, MLA_KV_RANK, 0)],
                    full_ins=[g_q, g_kv], row_outs=[(MLA_Q_RANK, BF16), (MLA_KV_RANK, BF16)],
                    acc_outs=[((1, MLA_Q_RANK), F32), ((1, MLA_KV_RANK), F32)])


def _mla_rope_fwd(qp, kvp, krope, tab):
    rows = qp.shape[0]
    hw = MLA_HEADS * LANE

    def body(i, ins, fulls, outs, accs):
        t = ins[3][...]
        kr = ins[2][...]
        for h in range(MLA_HEADS):
            sl = slice(h * LANE, (h + 1) * LANE)
            outs[0][:, sl] = _rope(ins[0][:, sl], t).astype(BF16)
            outs[1][:, sl] = (ins[1][:, sl] + kr).astype(BF16)
        outs[2][...] = ins[1][:, hw:2 * hw].astype(BF16)

    return _rowwise(body, name="mla_rope_fwd", rows=rows, tr=512,
                    row_ins=[(qp, hw, 0), (kvp, 2 * hw, 0), (krope, LANE, 0), (tab, 3 * LANE, 0)],
                    row_outs=[(hw, BF16), (hw, BF16), (hw, BF16)])


def _mla_rope_bwd(dq, dk, dv, tab):
    rows = dq.shape[0]
    hw = MLA_HEADS * LANE

    def body(i, ins, fulls, outs, accs):
        t = ins[3][...]
        dkr = jnp.zeros((ins[0].shape[0], LANE), F32)
        for h in range(MLA_HEADS):
            sl = slice(h * LANE, (h + 1) * LANE)
            outs[0][:, sl] = _rope_t(ins[0][:, sl], t).astype(BF16)
            dkr = dkr + ins[1][:, sl]
        outs[1][:, 0:hw] = ins[1][...].astype(BF16)
        outs[1][:, hw:2 * hw] = ins[2][...].astype(BF16)
        lane = lax.broadcasted_iota(jnp.int32, dkr.shape, 1)
        dkr = jnp.where((lane >= ROPE_LO) & (lane < ROPE_LO + MLA_ROPE), dkr, 0.0)
        outs[2][...] = _rope_t(dkr, t).astype(BF16)

    return _rowwise(body, name="mla_rope_bwd", rows=rows, tr=512,
                    row_ins=[(dq, hw, 0), (dk, hw, 0), (dv, hw, 0), (tab, 3 * LANE, 0)],
                    row_outs=[(hw, BF16), (2 * hw, BF16), (LANE, BF16)])


def _dot_nt(a, b):
    return lax.dot_general(a, b, (((1,), (1,)), ((), ())), preferred_element_type=F32)


def _dot_tn(a, b):
    return lax.dot_general(a, b, (((0,), (0,)), ((), ())), preferred_element_type=F32)


def _dot(a, b):
    return jnp.dot(a, b, preferred_element_type=F32)


def _attn_params(s, t, n_res_f32, n_res_bf16):
    blocks = n_res_f32 * _nbytes((s, LANE), F32) + n_res_bf16 * _nbytes((s, LANE), BF16) + 6 * _nbytes((t, LANE), F32)
    return pltpu.CompilerParams(dimension_semantics=("parallel", "arbitrary"),
                                vmem_limit_bytes=_vmem_limit(blocks, 12 * _nbytes((t, t), F32)))


def _mla_fwd(q, k, v, t, shards=()):
    s, hw = q.shape
    heads, nq = hw // LANE, s // t
    ng = len(shards)
    scale = 1.0 / math.sqrt(MLA_NOPE + MLA_ROPE)
    scale2 = scale * LOG2E

    def body(q_ref, k_ref, v_ref, *rest):
        o_ref, l_ref = rest[ng], rest[ng + 1]
        h, i = pl.program_id(0), pl.program_id(1)
        if ng:
            gather = _GatherPhases(rest[:ng], rest[ng + 2:2 * ng + 2], *rest[2 * ng + 2:])
            pl.when((h == 0) & (i == 0))(gather.send)
            pl.when((h == heads // 2) & (i == 0))(gather.forward)
        qv = q_ref[...]

        def step(j, carry, masked):
            m, l, acc = carry
            sl = pl.ds(pl.multiple_of(j * t, t), t)
            sc = _dot_nt(qv, k_ref[sl, :])
            if masked:
                row = lax.broadcasted_iota(jnp.int32, (t, t), 0)
                col = lax.broadcasted_iota(jnp.int32, (t, t), 1)
                sc = jnp.where(col <= row, sc, -1e30)
            m_new = jnp.maximum(m, jnp.max(sc, axis=1, keepdims=True))
            p = jnp.exp2((sc - m_new) * scale2)
            alpha = jnp.exp2((m - m_new) * scale2)
            l = alpha * l + jnp.sum(p, axis=1, keepdims=True)
            acc = alpha * acc + _dot(p.astype(BF16), v_ref[sl, :])
            return m_new, l, acc

        init = (jnp.full((t, 1), -1e30, F32), jnp.zeros((t, 1), F32), jnp.zeros((t, LANE), F32))
        carry = lax.fori_loop(0, i, lambda j, c: step(j, c, False), init)
        m, l, acc = step(i, carry, True)
        o_ref[...] = (acc / l).astype(o_ref.dtype)
        l_ref[0] = m * scale2 + jnp.log2(l)
        if ng:
            pl.when((h == heads - 1) & (i == nq - 1))(gather.finish)

    blk = pl.BlockSpec((t, LANE), lambda h, i: (i, h))
    res = pl.BlockSpec((s, LANE), lambda h, i: (0, h))
    params = _attn_params(s, t, 0, 2)
    outs = pl.pallas_call(
        body, name="mla_fwd", grid=(heads, nq), in_specs=[blk, res, res] + [ANY] * ng,
        out_specs=[blk, pl.BlockSpec((1, t, 1), lambda h, i: (h, i, 0))] + [ANY] * ng,
        out_shape=[jax.ShapeDtypeStruct((s, hw), BF16), jax.ShapeDtypeStruct((heads, s, 1), F32)]
        + [jax.ShapeDtypeStruct((N_CHIPS,) + sh.shape, sh.dtype) for sh in shards],
        scratch_shapes=_dma_sems(6 * ng) if ng else [],
        compiler_params=pltpu.CompilerParams(dimension_semantics=("arbitrary", "arbitrary") if ng else
                                             ("parallel", "arbitrary"), vmem_limit_bytes=params.vmem_limit_bytes),
    )(q, k, v, *shards)
    return outs[0], outs[1], list(outs[2:])


def _mla_bwd(q, k, v, o, do, lse, t):
    s, hw = q.shape
    heads, nq = hw // LANE, s // t
    scale = 1.0 / math.sqrt(MLA_NOPE + MLA_ROPE)
    scale2 = scale * LOG2E

    def body(q_ref, k_ref, v_ref, o_ref, do_ref, l_ref, dq_ref, dk_ref, dv_ref):
        i = pl.program_id(1)

        @pl.when(i == 0)
        def _():
            dk_ref[...] = jnp.zeros_like(dk_ref)
            dv_ref[...] = jnp.zeros_like(dv_ref)

        qv, dov, lv = q_ref[...], do_ref[...], l_ref[0]
        dlt = jnp.sum(dov.astype(F32) * o_ref[...].astype(F32), axis=1, keepdims=True)

        def step(j, dq, masked):
            sl = pl.ds(pl.multiple_of(j * t, t), t)
            kv, vv = k_ref[sl, :], v_ref[sl, :]
            p = jnp.exp2(_dot_nt(qv, kv) * scale2 - lv)
            if masked:
                row = lax.broadcasted_iota(jnp.int32, (t, t), 0)
                col = lax.broadcasted_iota(jnp.int32, (t, t), 1)
                p = jnp.where(col <= row, p, 0.0)
            ds = (p * (_dot_nt(dov, vv) - dlt)).astype(BF16)
            dk_ref[sl, :] += _dot_tn(ds, qv) * scale
            dv_ref[sl, :] += _dot_tn(p.astype(BF16), dov)
            return dq + _dot(ds, kv)

        dq = lax.fori_loop(0, i, lambda j, c: step(j, c, False), jnp.zeros((t, LANE), F32))
        dq_ref[...] = step(i, dq, True) * scale

    blk = pl.BlockSpec((t, LANE), lambda h, i: (i, h))
    res = pl.BlockSpec((s, LANE), lambda h, i: (0, h))
    full = jax.ShapeDtypeStruct((s, hw), F32)
    return pl.pallas_call(
        body, name="mla_bwd", grid=(heads, nq),
        in_specs=[blk, res, res, blk, blk, pl.BlockSpec((1, t, 1), lambda h, i: (h, i, 0))],
        out_specs=[blk, res, res], out_shape=[full, full, full],
        compiler_params=_attn_params(s, t, 2, 2),
    )(q, k, v, o, do, lse)


def _sb_logits(qv, kv, scale, masked, t, upper):
    z = _dot_nt(qv, kv) * scale
    e = jnp.exp(-jnp.abs(z))
    l1p = jnp.log(1.0 + e)
    lb = jnp.minimum(z, 0.0) - l1p
    lo = -jnp.maximum(z, 0.0) - l1p
    keep = None
    if masked:
        row = lax.broadcasted_iota(jnp.int32, (t, t), 0)
        col = lax.broadcasted_iota(jnp.int32, (t, t), 1)
        keep = col < row
        lo = jnp.where(keep, lo, 0.0)
    hi = lo.astype(BF16)
    rem = (lo - hi.astype(F32)).astype(BF16)
    suf = _dot(hi, upper) + _dot(rem, upper)
    return z, e, lb, lo, suf, keep


def _tri(t, inclusive):
    row = lax.broadcasted_iota(jnp.int32, (t, t), 0)
    col = lax.broadcasted_iota(jnp.int32, (t, t), 1)
    return jnp.where((row >= col) if inclusive else (row > col), 1.0, 0.0).astype(BF16)


SB_CHAINS = 2


def _sb_walk(i, first, carries_of):
    n = SB_CHAINS
    carries = [first(c) for c in range(n)]
    width = len(carries[0])

    def alive(carry):
        return jnp.max(carry[0]) >= SB_ZERO_LOG

    def split(st):
        return [tuple(st[1 + c * width:1 + (c + 1) * width]) for c in range(n)]

    def live(st):
        any_alive = alive(split(st)[0])
        for cr in split(st)[1:]:
            any_alive = any_alive | alive(cr)
        return (st[0] <= n * i) & any_alive

    def more(st):
        out = (st[0] + 1,)
        for c, cr in enumerate(split(st)):
            out += tuple(carries_of(c, st[0], cr))
        return out

    st = lax.while_loop(live, more, (jnp.int32(1),) + tuple(x for cr in carries for x in cr))
    jj, carries = st[0], split(st)
    for c in range(1, n):
        def live_c(s2, c=c):
            return (s2[0] <= n * i + c) & alive(s2[1:])

        def more_c(s2, c=c):
            return (s2[0] + 1,) + tuple(carries_of(c, s2[0], s2[1:]))

        carries[c] = lax.while_loop(live_c, more_c, (jj,) + tuple(carries[c]))[1:]
    return carries


def _sb_fwd(proj, t):
    s = proj.shape[0]
    heads, nq, n = SB_HEADS, s // t, SB_CHAINS
    scale = 1.0 / math.sqrt(SB_HEAD_DIM)

    def body(q_ref, k_ref, v_ref, o_ref):
        i = pl.program_id(1)
        upper = _tri(t, False)
        qs = [q_ref[c * t:(c + 1) * t, :].astype(BF16) for c in range(n)]

        def step(c, jj, carry, masked):
            run, acc = carry
            sl = pl.ds(pl.multiple_of((n * i + c - jj) * t, t), t)
            _, _, lb, lo, suf, keep = _sb_logits(qs[c], k_ref[sl, :].astype(BF16), scale, masked, t, upper)
            a = jnp.exp(lb + suf + run)
            if masked:
                a = jnp.where(keep, a, 0.0)
            acc = acc + _dot(a.astype(BF16), v_ref[sl, :].astype(BF16))
            return run + jnp.sum(lo, axis=1, keepdims=True), acc

        init = (jnp.zeros((t, 1), F32), jnp.zeros((t, LANE), F32))
        carries = _sb_walk(i, lambda c: step(c, 0, init, True), lambda c, jj, cr: step(c, jj, cr, False))
        for c in range(n):
            o_ref[c * t:(c + 1) * t, :] = carries[c][1]

    return pl.pallas_call(
        body, name="sb_fwd", grid=(heads, nq // n),
        in_specs=[pl.BlockSpec((n * t, LANE), lambda h, i: (i, C_SBQ // LANE + h)),
                  pl.BlockSpec((s, LANE), lambda h, i: (0, C_SBK // LANE + h)),
                  pl.BlockSpec((s, LANE), lambda h, i: (0, C_SBV // LANE + h))],
        out_specs=pl.BlockSpec((n * t, LANE), lambda h, i: (i, h)),
        out_shape=jax.ShapeDtypeStruct((s, heads * LANE), F32),
        compiler_params=_attn_params(s, n * t, 2, 0),
    )(proj, proj, proj)


def _sb_bwd(proj, o, do, t):
    s = proj.shape[0]
    heads, nq, n = SB_HEADS, s // t, SB_CHAINS
    scale = 1.0 / math.sqrt(SB_HEAD_DIM)

    def body(q_ref, k_ref, v_ref, o_ref, do_ref, dq_ref, dk_ref, dv_ref):
        i = pl.program_id(1)

        @pl.when(i == 0)
        def _():
            dk_ref[...] = jnp.zeros_like(dk_ref)
            dv_ref[...] = jnp.zeros_like(dv_ref)

        rows = [slice(c * t, (c + 1) * t) for c in range(n)]
        qs = [q_ref[r, :].astype(BF16) for r in rows]
        dos = [do_ref[r, :] for r in rows]
        totals = [jnp.sum(dos[c].astype(F32) * o_ref[rows[c], :], axis=1, keepdims=True) for c in range(n)]
        upper, upper_incl = _tri(t, False), _tri(t, True)

        def step(c, jj, carry, masked):
            run, g, dq = carry
            qv, dov = qs[c], dos[c]
            sl = pl.ds(pl.multiple_of((n * i + c - jj) * t, t), t)
            kv, vv = k_ref[sl, :].astype(BF16), v_ref[sl, :].astype(BF16)
            z, e, lb, lo, suf, keep = _sb_logits(qv, kv, scale, masked, t, upper)
            tail = suf + run
            a = jnp.exp(lb + tail)
            if masked:
                a = jnp.where(keep, a, 0.0)
            ab = a.astype(BF16)
            gr = ab.astype(F32) * _dot_nt(dov, vv)
            ghi = gr.astype(BF16)
            grem = (gr - ghi.astype(F32)).astype(BF16)
            before = totals[c] - g - (_dot(ghi, upper_incl) + _dot(grem, upper_incl))
            before = jnp.where(tail < SB_ZERO_LOG, 0.0, before)
            r = 1.0 / (1.0 + e)
            pos = z >= 0.0
            dz = r * (gr * jnp.where(pos, e, 1.0) - before * jnp.where(pos, 1.0, e))
            if masked:
                dz = jnp.where(keep, dz, 0.0)
            dzb = (dz * scale).astype(BF16)
            dk_ref[sl, :] += _dot_tn(dzb, qv)
            dv_ref[sl, :] += _dot_tn(ab, dov)
            return (run + jnp.sum(lo, axis=1, keepdims=True), g + jnp.sum(gr, axis=1, keepdims=True),
                    dq + _dot(dzb, kv))

        zero = jnp.zeros((t, 1), F32)
        init = (zero, zero, jnp.zeros((t, LANE), F32))
        carries = _sb_walk(i, lambda c: step(c, 0, init, True), lambda c, jj, cr: step(c, jj, cr, False))
        for c in range(n):
            dq_ref[rows[c], :] = carries[c][2]

    blk = pl.BlockSpec((n * t, LANE), lambda h, i: (i, h))
    res = pl.BlockSpec((s, LANE), lambda h, i: (0, h))
    full = jax.ShapeDtypeStruct((s, heads * LANE), F32)
    return pl.pallas_call(
        body, name="sb_bwd", grid=(heads, nq // n),
        in_specs=[pl.BlockSpec((n * t, LANE), lambda h, i: (i, C_SBQ // LANE + h)),
                  pl.BlockSpec((s, LANE), lambda h, i: (0, C_SBK // LANE + h)),
                  pl.BlockSpec((s, LANE), lambda h, i: (0, C_SBV // LANE + h)), blk, blk],
        out_specs=[blk, res, res], out_shape=[full, full, full],
        compiler_params=_attn_params(s, n * t, 4, 0),
    )(proj, proj, proj, o, do)


def _xattn_probs(qh, kh):
    sc = _dot_nt(qh, kh) * (1.0 / math.sqrt(X_HEAD_DIM))
    p = jnp.exp(sc - jnp.max(sc, axis=1, keepdims=True))
    return p / jnp.sum(p, axis=1, keepdims=True)


def _xattn_fwd(xq, xkv):
    rows = xq.shape[0]
    w = X_HEADS * X_HEAD_DIM

    def body(i, ins, fulls, outs, accs):
        for h in range(X_HEADS):
            sl = slice(h * LANE, (h + 1) * LANE)
            p = _xattn_probs(ins[0][:, sl], fulls[0][:, sl])
            outs[0][:, sl] = _dot(p.astype(BF16), fulls[0][:, w + h * LANE:w + (h + 1) * LANE]).astype(BF16)

    return _rowwise(body, name="xattn_fwd", rows=rows, tr=512, row_ins=[(xq, w, 0)], full_ins=[xkv],
                    row_outs=[(w, BF16)])[0]


def _xattn_bwd(xq, xkv, dxo):
    rows = xq.shape[0]
    w = X_HEADS * X_HEAD_DIM

    def body(i, ins, fulls, outs, accs):
        _acc_init(i, accs)
        for h in range(X_HEADS):
            sl = slice(h * LANE, (h + 1) * LANE)
            slv = slice(w + h * LANE, w + (h + 1) * LANE)
            qh, kh, vh, doh = ins[0][:, sl], fulls[0][:, sl], fulls[0][:, slv], ins[1][:, sl]
            p = _xattn_probs(qh, kh)
            dp = _dot_nt(doh, vh)
            ds = (p * (dp - jnp.sum(p * dp, axis=1, keepdims=True)) * (1.0 / math.sqrt(X_HEAD_DIM))).astype(BF16)
            outs[0][:, sl] = _dot(ds, kh).astype(BF16)
            accs[0][:, sl] += _dot_tn(ds, qh)
            accs[0][:, slv] += _dot_tn(p.astype(BF16), doh)

    return _rowwise(body, name="xattn_bwd", rows=rows, tr=512, row_ins=[(xq, w, 0), (dxo, w, 0)], full_ins=[xkv],
                    row_outs=[(w, BF16)], acc_outs=[(xkv.shape, F32)])


def _gate_fwd(proj, pa, pb, b_gate):
    rows = proj.shape[0]

    def body(i, ins, fulls, outs, accs):
        sa = _sigmoid(ins[0][...] + fulls[0][0:1, :])
        sb = _sigmoid(ins[1][...] + fulls[0][1:2, :])
        outs[0][...] = (sa * ins[2][...] + sb * ins[3][...]).astype(BF16)

    return _rowwise(body, name="gate_fwd", rows=rows, tr=512,
                    row_ins=[(proj, D_MODEL, C_GA // D_MODEL), (proj, D_MODEL, C_GB // D_MODEL), (pa, D_MODEL, 0),
                             (pb, D_MODEL, 0)],
                    full_ins=[b_gate], row_outs=[(D_MODEL, BF16)])[0]


def _gate_bwd(proj, pa, pb, b_gate, dm):
    rows = proj.shape[0]

    def body(i, ins, fulls, outs, accs):
        _acc_init(i, accs)
        d = ins[4][...]
        sa = _sigmoid(ins[0][...] + fulls[0][0:1, :])
        sb = _sigmoid(ins[1][...] + fulls[0][1:2, :])
        dga = d * ins[2][...] * sa * (1.0 - sa)
        dgb = d * ins[3][...] * sb * (1.0 - sb)
        outs[0][...] = (d * sa).astype(BF16)
        outs[1][...] = (d * sb).astype(BF16)
        outs[2][...] = dga.astype(BF16)
        outs[3][...] = dgb.astype(BF16)
        accs[0][0:1, :] += jnp.sum(dga, axis=0, keepdims=True)
        accs[0][1:2, :] += jnp.sum(dgb, axis=0, keepdims=True)

    return _rowwise(body, name="gate_bwd", rows=rows, tr=512,
                    row_ins=[(proj, D_MODEL, C_GA // D_MODEL), (proj, D_MODEL, C_GB // D_MODEL), (pa, D_MODEL, 0),
                             (pb, D_MODEL, 0), (dm, D_MODEL, 0)],
                    full_ins=[b_gate], row_outs=[(D_MODEL, BF16)] * 4, acc_outs=[((2, D_MODEL), F32)])


def _loss_head(x3, target, g_final):
    rows = x3.shape[0]

    def body(i, ins, fulls, outs, accs):
        _acc_init(i, accs)
        xv, g = ins[0][...], fulls[0][...]
        d = _rms(xv, g) - ins[1][...]
        dx, dg = _rms_bwd(xv, g, d * (1.0 / D_MODEL))
        outs[0][...] = dx
        accs[0][...] += dg
        accs[1][...] += jnp.sum(d * d, axis=0, keepdims=True)

    return _rowwise(body, name="loss_head", rows=rows, tr=512, row_ins=[(x3, D_MODEL, 0), (target, D_MODEL, 0)],
                    full_ins=[g_final], row_outs=[(D_MODEL, F32)], acc_outs=[((1, D_MODEL), F32), ((1, D_MODEL), F32)])


def _adamw(w, g, m, v, name):
    rows, c = w.shape

    def body(i, ins, fulls, outs, accs):
        wv, gv = ins[0][...], ins[1][...]
        mn = ADAM_B1 * ins[2][...] + (1.0 - ADAM_B1) * gv
        vn = ADAM_B2 * ins[3][...] + (1.0 - ADAM_B2) * jnp.square(gv)
        m_hat = mn / (1.0 - ADAM_B1 ** ADAM_STEP)
        v_hat = vn / (1.0 - ADAM_B2 ** ADAM_STEP)
        outs[0][...] = -ADAM_LR * (m_hat / (jnp.sqrt(v_hat) + ADAM_EPS) + ADAM_WD * wv)
        outs[1][...] = mn
        outs[2][...] = vn

    return _rowwise(body, name=name, rows=rows, tr=_row_tile(rows, 256), row_ins=[(a, c, 0) for a in (w, g, m, v)],
                    row_outs=[(c, F32)] * 3)


def _place():
    x, y, c = lax.axis_index("x"), lax.axis_index("y"), lax.axis_index("c")
    chips = [(1 - x, y), (x, 1 - y), (1 - x, 1 - y)]
    return x, y, c, chips


ANY = pl.BlockSpec(memory_space=pl.ANY)


def _remote(src, dst, send_sem, recv_sem, to):
    return pltpu.make_async_remote_copy(src_ref=src, dst_ref=dst, send_sem=send_sem, recv_sem=recv_sem,
                                        device_id=to, device_id_type=MESH)


def _dma_sems(n):
    return [pltpu.SemaphoreType.DMA((n,)), pltpu.SemaphoreType.DMA((n,))]


class _GatherPhases:
    def __init__(self, ins, outs, send_sems, recv_sems):
        self.ins, self.outs, self.send_sems, self.recv_sems = ins, outs, send_sems, recv_sems
        self.x, self.y, self.c, self.chips = _place()
        self.me = 2 * self.x + self.y

    def _copy(self, t, j, chip_idx, hlf, to, src=None):
        h = self.ins[t].shape[0] // 2
        dst = self.outs[t].at[chip_idx, pl.ds(hlf * h, h), :]
        return _remote(dst if src is None else src, dst, self.send_sems.at[6 * t + j], self.recv_sems.at[6 * t + j], to)

    def _sends(self):
        out = []
        for t in range(len(self.ins)):
            h = self.ins[t].shape[0] // 2
            for j, chip in enumerate(self.chips):
                out.append(self._copy(t, j, self.me, self.c, (*chip, self.c), src=self.ins[t].at[pl.ds(self.c * h, h), :]))
        return out

    def _forwards(self):
        return [self._copy(t, 3 + j, 2 * chip[0] + chip[1], self.c, (self.x, self.y, 1 - self.c))
                for t in range(len(self.ins)) for j, chip in enumerate(self.chips)]

    def send(self):
        for cp in self._sends():
            cp.start()

    def forward(self):
        here = (self.x, self.y, self.c)
        landed = [self._copy(t, j, 2 * chip[0] + chip[1], self.c, here)
                  for t in range(len(self.ins)) for j, chip in enumerate(self.chips)]
        for arrival, fwd in zip(landed, self._forwards()):
            arrival.wait_recv()
            fwd.start()

    def finish(self):
        here = (self.x, self.y, self.c)
        for t in range(len(self.ins)):
            for j, chip in enumerate(self.chips):
                self._copy(t, 3 + j, 2 * chip[0] + chip[1], 1 - self.c, here).wait_recv()
        for cp in self._sends() + self._forwards():
            cp.wait_send()


def _all_gather_weights(shards):
    n = len(shards)

    def body(*refs):
        gather = _GatherPhases(refs[:n], refs[n:2 * n], *refs[2 * n:])
        gather.send()
        gather.forward()
        gather.finish()

    return pl.pallas_call(
        body, name="all_gather_weights", in_specs=[ANY] * n, out_specs=[ANY] * n,
        out_shape=[jax.ShapeDtypeStruct((N_CHIPS,) + s.shape, s.dtype) for s in shards],
        scratch_shapes=_dma_sems(6 * n),
    )(*shards)


def _pair_exchange_grads(stacked):
    n = len(stacked)

    def body(*refs):
        ins, outs = refs[:n], refs[n:2 * n]
        send_sems, recv_sems = refs[2 * n:]
        x, y, c, _ = _place()
        cps = []
        for t in range(n):
            h = ins[t].shape[1] // 2
            cps.append(_remote(ins[t].at[:, pl.ds((1 - c) * h, h), :], outs[t], send_sems.at[t], recv_sems.at[t],
                               (x, y, 1 - c)))
            cps[-1].start()
        for cp in cps:
            cp.wait()

    return pl.pallas_call(
        body, name="pair_exchange_grads", in_specs=[ANY] * n, out_specs=[ANY] * n,
        out_shape=[jax.ShapeDtypeStruct((N_CHIPS, s.shape[1] // 2, s.shape[2]), s.dtype) for s in stacked],
        scratch_shapes=_dma_sems(n),
    )(*stacked)


def _chip_scatter(parts):
    n = len(parts)

    def body(*refs):
        ins, outs = refs[:n], refs[n:2 * n]
        send_sems, recv_sems = refs[2 * n:]
        x, y, c, chips = _place()
        cps = []
        for t in range(n):
            for j, chip in enumerate(chips):
                cps.append(_remote(ins[t].at[2 * chip[0] + chip[1]], outs[t].at[j], send_sems.at[3 * t + j],
                                   recv_sems.at[3 * t + j], (*chip, c)))
                cps[-1].start()
        for cp in cps:
            cp.wait()

    return pl.pallas_call(
        body, name="chip_scatter", in_specs=[ANY] * n, out_specs=[ANY] * n,
        out_shape=[jax.ShapeDtypeStruct((N_CHIPS - 1,) + s.shape[1:], s.dtype) for s in parts],
        scratch_shapes=_dma_sems(3 * n),
    )(*parts)


def _pair_exchange_halves(shards):
    n = len(shards)

    def body(*refs):
        bufs = refs[n:2 * n]
        send_sems, recv_sems = refs[2 * n:]
        x, y, c, _ = _place()
        cps = []
        for t in range(n):
            h = bufs[t].shape[0] // 2
            rows = bufs[t].at[pl.ds(c * h, h), :]
            cps.append(_remote(rows, rows, send_sems.at[t], recv_sems.at[t], (x, y, 1 - c)))
            cps[-1].start()
        for cp in cps:
            cp.wait()

    return pl.pallas_call(
        body, name="pair_exchange_halves", in_specs=[ANY] * n, out_specs=[ANY] * n,
        out_shape=[jax.ShapeDtypeStruct(s.shape, s.dtype) for s in shards],
        input_output_aliases={t: t for t in range(n)},
        scratch_shapes=_dma_sems(n),
    )(*shards)


def _pair_sum(gs, recv, place, name):
    _, r, cols = gs.shape
    h = r // 2

    def kern(p_ref, a_ref, b_ref, o_ref):
        o_ref[...] = (a_ref[...] + b_ref[...]).astype(BF16)

    blk = lambda f: pl.BlockSpec((1, h, cols), f)
    return pl.pallas_call(
        kern, name=name,
        grid_spec=pltpu.PrefetchScalarGridSpec(
            num_scalar_prefetch=1, grid=(N_CHIPS,),
            in_specs=[blk(lambda d, p: (d, p[1], 0)), blk(lambda d, p: (d, 0, 0))],
            out_specs=blk(lambda d, p: (d, 0, 0))),
        out_shape=jax.ShapeDtypeStruct((N_CHIPS, h, cols), BF16),
        compiler_params=pltpu.CompilerParams(dimension_semantics=("arbitrary",),
                                             vmem_limit_bytes=_vmem_limit(3 * _nbytes((h, cols), F32), 0)),
    )(place, gs, recv)


def _chip_sum(gs, recv, got, place, name):
    _, r, cols = gs.shape
    h = r // 2

    def kern(p_ref, a_ref, b_ref, g0, g1, g2, o_ref):
        own = a_ref[0] + b_ref[0]
        o_ref[...] = ((own + g0[0].astype(F32)) + g1[0].astype(F32)) + g2[0].astype(F32)

    blk = lambda f: pl.BlockSpec((1, h, cols), f)
    return pl.pallas_call(
        kern, name=name,
        grid_spec=pltpu.PrefetchScalarGridSpec(
            num_scalar_prefetch=1, grid=(1,),
            in_specs=[blk(lambda i, p: (p[0], p[1], 0)), blk(lambda i, p: (p[0], 0, 0)), blk(lambda i, p: (0, 0, 0)),
                      blk(lambda i, p: (1, 0, 0)), blk(lambda i, p: (2, 0, 0))],
            out_specs=pl.BlockSpec((h, cols), lambda i, p: (p[1], 0))),
        out_shape=jax.ShapeDtypeStruct((r, cols), F32),
        compiler_params=pltpu.CompilerParams(dimension_semantics=("arbitrary",),
                                             vmem_limit_bytes=_vmem_limit(5 * _nbytes((h, cols), F32), 0)),
    )(place, gs, recv, got, got, got)


def _all_reduce_small(vec, name):
    r, cols = vec.shape

    def body(in_ref, out_ref, gath, send_sems, recv_sems):
        x, y, c, _ = _place()
        me = 4 * x + 2 * y + c
        gath[me] = in_ref[...]
        sends = []
        for k in range(1, 8):
            to = (x ^ (k >> 2), y ^ ((k >> 1) & 1), c ^ (k & 1))
            cp = pltpu.make_async_remote_copy(src_ref=in_ref, dst_ref=gath.at[me], send_sem=send_sems.at[k - 1],
                                              recv_sem=recv_sems.at[k - 1], device_id=to, device_id_type=MESH)
            cp.start()
            sends.append(cp)
        for k in range(1, 8):
            peer = me ^ k
            pltpu.make_async_remote_copy(src_ref=in_ref, dst_ref=gath.at[peer], send_sem=send_sems.at[k - 1],
                                         recv_sem=recv_sems.at[k - 1], device_id=(x, y, c),
                                         device_id_type=MESH).wait_recv()
        for cp in sends:
            cp.wait_send()
        acc = gath[0]
        for d in range(1, 8):
            acc = acc + gath[d]
        out_ref[...] = acc

    vm = pl.BlockSpec(memory_space=pltpu.VMEM)
    return pl.pallas_call(
        body, name=name, in_specs=[vm], out_specs=vm,
        out_shape=jax.ShapeDtypeStruct((r, cols), F32),
        scratch_shapes=[pltpu.VMEM((8, r, cols), F32), pltpu.SemaphoreType.DMA((7,)), pltpu.SemaphoreType.DMA((7,))],
    )(vec)


def _pad_heads(w, heads, dim, axis):
    shp = w.shape[:axis] + (heads, dim) + w.shape[axis + 1:]
    pad = [(0, 0)] * len(shp)
    pad[axis + 1] = (0, LANE - dim)
    w = jnp.pad(w.reshape(shp), pad)
    return w.reshape(w.shape[:axis] + (heads * LANE,) + w.shape[axis + 2:])


def _unpad_heads(w, heads, dim, axis):
    shp = w.shape[:axis] + (heads, LANE) + w.shape[axis + 1:]
    w = lax.slice_in_dim(w.reshape(shp), 0, dim, axis=axis + 1)
    return w.reshape(w.shape[:axis] + (heads * dim,) + w.shape[axis + 2:])


def _w_in_layout(w_in):
    kr = jnp.pad(w_in[:, 384:416], ((0, 0), (ROPE_LO, LANE - ROPE_LO - MLA_ROPE)))
    sb = lambda lo: _pad_heads(w_in[:, lo:lo + 512], SB_HEADS, SB_HEAD_DIM, 1)
    return jnp.concatenate([w_in[:, 1952:2976], w_in[:, 2976:4000], sb(416), sb(928), sb(1440), w_in[:, 0:256],
                            w_in[:, 256:384], kr], axis=1)


def _w_in_unlayout(d):
    sb = lambda lo: _unpad_heads(d[:, lo:lo + 1024], SB_HEADS, SB_HEAD_DIM, 1)
    return jnp.concatenate([d[:, C_CQ:C_CQ + 256], d[:, C_CKV:C_CKV + 128], d[:, C_KR + ROPE_LO:C_KR + ROPE_LO + MLA_ROPE],
                            sb(C_SBQ), sb(C_SBK), sb(C_SBV), d[:, C_GA:C_GA + 1024], d[:, C_GB:C_GB + 1024]], axis=1)


def _w_ukv_layout(w):
    w3 = w.reshape(MLA_KV_RANK, MLA_HEADS, MLA_NOPE + MLA_V)
    pad = lambda part: jnp.pad(part, ((0, 0), (0, 0), (0, LANE - part.shape[2]))).reshape(MLA_KV_RANK, MLA_HEADS * LANE)
    return jnp.concatenate([pad(w3[:, :, :MLA_NOPE]), pad(w3[:, :, MLA_NOPE:])], axis=1)


def _w_ukv_unlayout(d):
    hw = MLA_HEADS * LANE
    kpart = d[:, :hw].reshape(MLA_KV_RANK, MLA_HEADS, LANE)[:, :, :MLA_NOPE]
    vpart = d[:, hw:].reshape(MLA_KV_RANK, MLA_HEADS, LANE)[:, :, :MLA_V]
    return jnp.concatenate([kpart, vpart], axis=2).reshape(MLA_KV_RANK, MLA_HEADS * (MLA_NOPE + MLA_V))


def _shard_of(full, d, axis):
    n = full.shape[axis] // N_CHIPS
    return lax.slice_in_dim(full, d * n, (d + 1) * n, axis=axis)


def _local_step(x, mem, pos, target, w, t_mla, t_sb, late=None):
    s = x.shape[0]
    w = dict(w)
    win = _w_in_layout(w["w_in"])
    wuq = _pad_heads(w["w_uq"], MLA_HEADS, MLA_NOPE + MLA_ROPE, 1)
    wkv = _w_ukv_layout(w["w_ukv"])
    inv_freq = ROPE_THETA ** (-jnp.arange(0, MLA_ROPE, 2, dtype=F32) / MLA_ROPE)
    freq_lane = jnp.pad(jnp.concatenate([inv_freq, inv_freq]), (ROPE_LO, LANE - ROPE_LO - MLA_ROPE)).reshape(1, LANE)
    add = lambda accs, ex: (accs[0] + ex[0],)

    tab = _rope_tables(pos.reshape(s, 1), freq_lane)
    h = _rms_fwd_call(x, w["g_mix"], "rms_mix")
    proj = _mm(h, [win], name="proj_in", tn=1408)
    cqn, ckvn, krope = _mla_prep_fwd(proj, tab, w["g_q_lat"], w["g_kv_lat"])
    qp = _mm(cqn, [wuq], name="q_up")
    kvp = _mm(ckvn, [wkv], name="kv_up")
    qa, ka, va = _mla_rope_fwd(qp, kvp, krope, tab)
    o_a, lse, gathered = _mla_fwd(qa, ka, va, t_mla, late[0] if late else ())
    if late:
        w.update(late[1](gathered))
    wa = _pad_heads(w["w_a_proj"], MLA_HEADS, MLA_V, 0)
    wb = _pad_heads(w["w_b_proj"], SB_HEADS, SB_HEAD_DIM, 0)
    o_b = _sb_fwd(proj, t_sb)
    pa = _mm(o_a, [wa], name="proj_a")
    pb = _mm(o_b, [wb], name="proj_b")
    merged = _gate_fwd(proj, pa, pb, w["b_gate"])
    x1 = _mm(merged, [w["w_o"]], name="proj_o", extras=(x,), epilogue=add)
    hx = _rms_fwd_call(x1, w["g_x"], "rms_x")
    mn = _rms_fwd_call(mem, w["g_mem"], "rms_mem")
    xq = _mm(hx, [w["w_xq"]], name="xq", out_dtypes=(BF16,))
    xkv = _mm(mn, [w["w_xkv"]], name="xkv", out_dtypes=(BF16,))
    xo = _xattn_fwd(xq, xkv)
    x2 = _mm(xo, [w["w_xo"]], name="proj_xo", extras=(x1,), epilogue=add)
    hf = _rms_fwd_call(x2, w["g_ffn"], "rms_ffn")

    def swiglu(accs, ex):
        a, b = accs
        return a, b, a * _sigmoid(a) * b

    ga, gu, hmid = _mm(hf, [w["w_gate"], w["w_up"]], name="ffn_up", epilogue=swiglu, out_dtypes=(BF16, BF16, BF16),
                       tm=512, tn=1408)
    x3 = _mm(hmid, [w["w_down"]], name="ffn_down", extras=(x2,), epilogue=add, tk=2816)

    dx3, dg_final, sq = _loss_head(x3, target, w["g_final"].reshape(1, D_MODEL))
    g = {"g_final": dg_final.reshape(D_MODEL)}

    def swiglu_bwd(accs, ex):
        dh, a, b = accs[0], ex[0].astype(F32), ex[1].astype(F32)
        sg = _sigmoid(a)
        return dh * b * sg * (1.0 + a * (1.0 - sg)), dh * a * sg

    da, db = _mm(dx3, [w["w_down"]], name="ffn_down_dx", tb=True, extras=(ga, gu), epilogue=swiglu_bwd,
                 out_dtypes=(BF16, BF16), tm=512, tn=1408)
    g["w_down"] = _mm(hmid, [dx3], name="ffn_down_dw", ta=True, tm=1408)
    g["w_gate"] = _mm(hf, [da], name="ffn_gate_dw", ta=True, tn=1408, tk=2048)
    g["w_up"] = _mm(hf, [db], name="ffn_up_dw", ta=True, tn=1408)
    dhf = _mm(da, [w["w_gate"]], name="ffn_gate_dx", tb=True, tk=2816)
    dhf = _mm(db, [w["w_up"]], name="ffn_up_dx", tb=True, extras=(dhf,), epilogue=add, tk=2816)
    dx2, g["g_ffn"] = _rms_bwd_call(x2, w["g_ffn"], dhf, dx3, "rms_ffn_bwd")

    dxo = _mm(dx2, [w["w_xo"]], name="proj_xo_dx", tb=True, out_dtypes=(BF16,))
    g["w_xo"] = _mm(xo, [dx2], name="proj_xo_dw", ta=True)
    dxq, dxkv = _xattn_bwd(xq, xkv, dxo)
    dhx = _mm(dxq, [w["w_xq"]], name="xq_dx", tb=True)
    g["w_xq"] = _mm(hx, [dxq], name="xq_dw", ta=True)
    dmn = _mm(dxkv, [w["w_xkv"]], name="xkv_dx", tb=True)
    g["w_xkv"] = _mm(mn, [dxkv], name="xkv_dw", ta=True)
    dx1, g["g_x"] = _rms_bwd_call(x1, w["g_x"], dhx, dx2, "rms_x_bwd")
    _, g["g_mem"] = _rms_bwd_call(mem, w["g_mem"], dmn, None, "rms_mem_bwd")

    dmerged = _mm(dx1, [w["w_o"]], name="proj_o_dx", tb=True)
    g["w_o"] = _mm(merged, [dx1], name="proj_o_dw", ta=True)
    dpa, dpb, dga, dgb, g["b_gate"] = _gate_bwd(proj, pa, pb, w["b_gate"], dmerged)
    do_a = _mm(dpa, [wa], name="proj_a_dx", tb=True, out_dtypes=(BF16,))
    do_b = _mm(dpb, [wb], name="proj_b_dx", tb=True, out_dtypes=(BF16,))
    g["w_a_proj"] = _unpad_heads(_mm(o_a, [dpa], name="proj_a_dw", ta=True), MLA_HEADS, MLA_V, 0)
    g["w_b_proj"] = _unpad_heads(_mm(o_b, [dpb], name="proj_b_dw", ta=True), SB_HEADS, SB_HEAD_DIM, 0)

    dsq, dsk, dsv = _sb_bwd(proj, o_b, do_b, t_sb)
    dqa, dka, dva = _mla_bwd(qa, ka, va, o_a, do_a, lse, t_mla)
    dqp, dkvp, dkr = _mla_rope_bwd(dqa, dka, dva, tab)
    g["w_uq"] = _unpad_heads(_mm(cqn, [dqp], name="q_up_dw", ta=True), MLA_HEADS, MLA_NOPE + MLA_ROPE, 1)
    g["w_ukv"] = _w_ukv_unlayout(_mm(ckvn, [dkvp], name="kv_up_dw", ta=True))
    dcqn = _mm(dqp, [wuq], name="q_up_dx", tb=True)
    dckvn = _mm(dkvp, [wkv], name="kv_up_dx", tb=True)
    dcq, dckv, g["g_q_lat"], g["g_kv_lat"] = _mla_prep_bwd(proj, w["g_q_lat"], w["g_kv_lat"], dcqn, dckvn)

    dproj = jnp.concatenate([dga, dgb, dsq.astype(BF16), dsk.astype(BF16), dsv.astype(BF16), dcq, dckv, dkr], axis=1)
    g["w_in"] = _w_in_unlayout(_mm(h, [dproj], name="proj_in_dw", ta=True, tn=1408))
    dh = _mm(dproj, [win], name="proj_in_dx", tb=True, tk=2816)
    grad_x, g["g_mix"] = _rms_bwd_call(x, w["g_mix"], dh, dx1, "rms_mix_bwd")
    return sq, grad_x, g


def _small_pack(d):
    row5 = jnp.concatenate([d["g_q_lat"].reshape(-1), d["g_kv_lat"].reshape(-1), jnp.zeros((640,), F32)])
    rows = [d[n].reshape(-1) for n in ("g_mix", "g_x", "g_mem", "g_ffn", "g_final")] + [row5]
    return rows


def _small_unpack(p, like):
    out = {n: p[i].reshape(like[n].shape) for i, n in enumerate(("g_mix", "g_x", "g_mem", "g_ffn", "g_final"))}
    out["g_q_lat"] = p[5, 0:256].reshape(like["g_q_lat"].shape)
    out["g_kv_lat"] = p[5, 256:384].reshape(like["g_kv_lat"].shape)
    return out


def kernel(x, mem, positions, g_mix, w_in, b_gate, g_q_lat, w_uq, g_kv_lat, w_ukv, w_a_proj, w_b_proj, w_o, g_x, g_mem, w_xq, w_xkv, w_xo, g_ffn, w_gate, w_up, w_down, g_final, loss_target, m_g_mix, m_w_in, m_b_gate, m_g_q_lat, m_w_uq, m_g_kv_lat, m_w_ukv, m_w_a_proj, m_w_b_proj, m_w_o, m_g_x, m_g_mem, m_w_xq, m_w_xkv, m_w_xo, m_g_ffn, m_w_gate, m_w_up, m_w_down, m_g_final, v_g_mix, v_w_in, v_b_gate, v_g_q_lat, v_w_uq, v_g_kv_lat, v_w_ukv, v_w_a_proj, v_w_b_proj, v_w_o, v_g_x, v_g_mem, v_w_xq, v_w_xkv, v_w_xo, v_g_ffn, v_w_gate, v_w_up, v_w_down, v_g_final):
    given = dict(locals())
    names = [n for n, _, _ in MATS] + ["b_gate"] + list(SMALL)
    wts = {n: given[n] for n in names}
    mom = {n: given["m_" + n] for n in names}
    var = {n: given["v_" + n] for n in names}
    shard2d = {n: shp for n, shp, _ in MATS}
    shard2d["b_gate"] = B_GATE_SHARD
    cx, cy, cc = lax.axis_index("x"), lax.axis_index("y"), lax.axis_index("c")
    me = 2 * cx + cy
    place = jnp.stack([me, cc]).astype(jnp.int32)
    bcol = me * B_GATE_SHARD[1]

    own = [wts[n].reshape(shard2d[n]).astype(BF16) for n, _, _ in MATS]

    def assemble(mats, gathered, mine):
        out = {}
        for (n, shp, ax), g4, shard in zip(mats, gathered, mine):
            g4 = lax.dynamic_update_slice(g4, shard[None], (me, 0, 0))
            out[n] = g4.reshape(N_CHIPS * shp[0], shp[1]) if ax == 0 else jnp.concatenate(list(g4), axis=1)
        return out

    full = assemble(MATS[:N_EARLY], _all_gather_weights(own[:N_EARLY]), own[:N_EARLY])
    late = (own[N_EARLY:], lambda gathered: assemble(MATS[N_EARLY:], gathered, own[N_EARLY:]))
    bias_rows = jnp.pad(wts["b_gate"].reshape(B_GATE_SHARD), ((0, SMALL_ROWS - 2), (0, 0)))
    bias_rows = lax.dynamic_update_slice(jnp.zeros((SMALL_ROWS, D_MODEL), F32), bias_rows, (0, bcol))
    full["b_gate"] = _all_reduce_small(jnp.where(cc == 0, bias_rows, 0.0), "all_gather_bias")[0:2]
    for n in SMALL:
        full[n] = wts[n].reshape(1, -1) if n != "g_final" else wts[n]

    sq, grad_x, grads = _local_step(x[0], mem[0], positions[0], loss_target[0], full, t_mla=1024, t_sb=256,
                                     late=late)

    stacked = [jnp.stack([_shard_of(grads[n], d, ax) for d in range(N_CHIPS)]) for n, _, ax in MATS]
    recv = _pair_exchange_grads(stacked)
    parts = [_pair_sum(gs, rv, place, "pair_sum_" + n) for (n, _, _), gs, rv in zip(MATS, stacked, recv)]
    got = _chip_scatter(parts)
    halves = [_chip_sum(gs, rv, gt, place, "chip_sum_" + n) for (n, _, _), gs, rv, gt in zip(MATS, stacked, recv, got)]
    g_shard = dict(zip([n for n, _, _ in MATS], _pair_exchange_halves(halves)))

    small_rows = _small_pack({n: grads[n] for n in SMALL}) + [sq.reshape(-1), grads["b_gate"][0], grads["b_gate"][1]]
    small_rows += [jnp.zeros((D_MODEL,), F32)] * (SMALL_ROWS - len(small_rows))
    small = _all_reduce_small(jnp.stack(small_rows), "all_reduce_small")
    loss = (0.5 / D_MODEL) * jnp.sum(small[6])
    g_shard["b_gate"] = lax.dynamic_slice(small[7:9], (0, bcol), B_GATE_SHARD)

    out = {"grad": {}, "delta": {}, "m": {}, "v": {}}
    for n in [n for n, _, _ in MATS] + ["b_gate"]:
        shape = wts[n].shape
        r2 = lambda a: a.reshape(shard2d[n])
        d_n, m_n, v_n = _adamw(r2(wts[n]), g_shard[n], r2(mom[n]), r2(var[n]), "adamw_" + n)
        for key, a in (("grad", g_shard[n]), ("delta", d_n), ("m", m_n), ("v", v_n)):
            out[key][n] = a.reshape(shape)
    sp = lambda d: jnp.stack(_small_pack(d) + [jnp.zeros((D_MODEL,), F32)] * 2)
    delta_s, m_s, v_s = _adamw(sp(wts), small[0:8].at[6:8].set(0.0), sp(mom), sp(var), "adamw_small")
    for key, p in (("grad", small), ("delta", delta_s), ("m", m_s), ("v", v_s)):
        out[key].update(_small_unpack(p, wts))

    order = ["g_mix", "w_in", "b_gate", "g_q_lat", "w_uq", "g_kv_lat", "w_ukv", "w_a_proj", "w_b_proj", "w_o", "g_x",
             "g_mem", "w_xq", "w_xkv", "w_xo", "g_ffn", "w_gate", "w_up", "w_down", "g_final"]
    return (loss, grad_x[None], *[out[key][n] for key in ("grad", "delta", "m", "v") for n in order])
```

```python
import functools
import math

import jax
import jax.numpy as jnp
from jax import lax
from jax.experimental import pallas as pl
from jax.experimental.pallas import tpu as pltpu

F32 = jnp.float32
BF16 = jnp.bfloat16
MESH = pl.DeviceIdType.MESH

D_MODEL = 1024
MLA_HEADS = 8
MLA_Q_RANK = 256
MLA_KV_RANK = 128
MLA_NOPE = 64
MLA_ROPE = 32
MLA_V = 64
ROPE_THETA = 10000.0
SB_HEADS = 8
SB_HEAD_DIM = 64
X_HEADS = 4
X_HEAD_DIM = 128
D_FF = 2816
EPS = 1e-6
ADAM_LR = 0.001
ADAM_B1 = 0.9
ADAM_B2 = 0.999
ADAM_EPS = 1e-08
ADAM_WD = 0.01
ADAM_STEP = 10

LANE = 128
LOG2E = 1.4426950408889634
N_CHIPS = 4
VMEM_BYTES = 64 * 1024 * 1024

C_GA, C_GB, C_SBQ, C_SBK, C_SBV, C_CQ, C_CKV, C_KR = 0, 1024, 2048, 3072, 4096, 5120, 5376, 5504
D_INP = 5632
ROPE_LO = MLA_NOPE
HALF = MLA_ROPE // 2

SB_ZERO_LOG = -104.0

MATS = (
    ("w_in", (1024, 1000), 1), ("w_uq", (256, 192), 1), ("w_ukv", (128, 256), 1), ("w_a_proj", (512, 256), 1),
    ("w_b_proj", (512, 256), 1), ("w_o", (256, 1024), 0), ("w_xq", (256, 512), 0), ("w_xkv", (256, 1024), 0),
    ("w_xo", (512, 256), 1), ("w_gate", (1024, 704), 1), ("w_up", (1024, 704), 1), ("w_down", (704, 1024), 0),
)
N_EARLY = 3
B_GATE_SHARD = (2, 256)
SMALL = ("g_mix", "g_x", "g_mem", "g_ffn", "g_final", "g_q_lat", "g_kv_lat")
SMALL_ROWS = 16


def _vmem_limit(block_bytes, temp_bytes):
    est = 2 * block_bytes + temp_bytes + (4 << 20)
    return int(min(max(est, 16 << 20), VMEM_BYTES - (6 << 20)))


def _nbytes(shape, dtype):
    return math.prod(shape) * jnp.dtype(dtype).itemsize


def _row_tile(rows, cap):
    if rows <= cap:
        return rows
    return max(t for t in range(8, cap + 1, 8) if rows % t == 0)


def _tile(n, cap):
    if n <= cap:
        return n
    best = None
    for t in range(LANE, cap + 1, LANE):
        if n % t == 0:
            best = t
    assert best is not None, (n, cap)
    return best


def _mm(a, bs, *, name, ta=False, tb=False, extras=(), epilogue=None, out_dtypes=(F32,), tm=1024, tn=1024, tk=1024):
    bs = tuple(bs)
    m, k = (a.shape[1], a.shape[0]) if ta else a.shape
    n = bs[0].shape[0] if tb else bs[0].shape[1]
    tm, tn, tk = _tile(m, tm), _tile(n, tn), _tile(k, tk)
    assert m % tm == 0 and n % tn == 0 and k % tk == 0
    nk = k // tk
    nb, ne, no = len(bs), len(extras), len(out_dtypes)
    dims = (((0,) if ta else (1,)), ((1,) if tb else (0,))), ((), ())
    if epilogue is None:
        epilogue = lambda accs, ex: (accs[0],)

    def body(*refs):
        a_ref, b_refs, e_refs = refs[0], refs[1:1 + nb], refs[1 + nb:1 + nb + ne]
        o_refs, acc_refs = refs[1 + nb + ne:1 + nb + ne + no], refs[1 + nb + ne + no:]
        if nk == 1:
            av = a_ref[...].astype(BF16)
            accs = [lax.dot_general(av, b_ref[...].astype(BF16), dims, preferred_element_type=F32) for b_ref in b_refs]
            for o_ref, v in zip(o_refs, epilogue(accs, [e[...] for e in e_refs])):
                o_ref[...] = v.astype(o_ref.dtype)
            return
        kk = pl.program_id(2)

        @pl.when(kk == 0)
        def _():
            for acc in acc_refs:
                acc[...] = jnp.zeros_like(acc)

        av = a_ref[...].astype(BF16)
        for b_ref, acc in zip(b_refs, acc_refs):
            acc[...] += lax.dot_general(av, b_ref[...].astype(BF16), dims, preferred_element_type=F32)

        @pl.when(kk == nk - 1)
        def _():
            outs = epilogue([acc[...] for acc in acc_refs], [e[...] for e in e_refs])
            for o_ref, v in zip(o_refs, outs):
                o_ref[...] = v.astype(o_ref.dtype)

    a_spec = pl.BlockSpec((tk, tm), lambda i, j, kk: (kk, i)) if ta else pl.BlockSpec((tm, tk), lambda i, j, kk: (i, kk))
    b_spec = pl.BlockSpec((tn, tk), lambda i, j, kk: (j, kk)) if tb else pl.BlockSpec((tk, tn), lambda i, j, kk: (kk, j))
    mn_spec = pl.BlockSpec((tm, tn), lambda i, j, kk: (i, j))
    blocks = (_nbytes((tm, tk), a.dtype) + sum(_nbytes((tk, tn), b.dtype) for b in bs)
              + sum(_nbytes((tm, tn), e.dtype) for e in extras) + sum(_nbytes((tm, tn), d) for d in out_dtypes))
    temps = (nb + 4) * _nbytes((tm, tn), F32)
    outs = pl.pallas_call(
        body, name=name, grid=(m // tm, n // tn, nk),
        in_specs=[a_spec] + [b_spec] * nb + [mn_spec] * ne,
        out_specs=[mn_spec] * no,
        out_shape=[jax.ShapeDtypeStruct((m, n), d) for d in out_dtypes],
        scratch_shapes=[pltpu.VMEM((tm, tn), F32) for _ in range(nb if nk > 1 else 0)],
        compiler_params=pltpu.CompilerParams(
            dimension_semantics=("parallel", "parallel", "arbitrary"),
            vmem_limit_bytes=_vmem_limit(blocks, temps)),
    )(a, *bs, *extras)
    return outs[0] if no == 1 else outs


def _rowwise(body, *, name, rows, tr, row_ins, full_ins=(), row_outs=(), acc_outs=()):
    tr = min(tr, rows)
    assert rows % tr == 0
    n_ri, n_fi, n_ro = len(row_ins), len(full_ins), len(row_outs)

    def kern(*refs):
        body(pl.program_id(0), refs[:n_ri], refs[n_ri:n_ri + n_fi], refs[n_ri + n_fi:n_ri + n_fi + n_ro],
             refs[n_ri + n_fi + n_ro:])

    in_specs = [pl.BlockSpec((tr, w), functools.partial(lambda i, c: (i, c), c=ci)) for _, w, ci in row_ins]
    in_specs += [pl.BlockSpec(f.shape, lambda i: (0, 0)) for f in full_ins]
    out_specs = [pl.BlockSpec((tr, w), lambda i: (i, 0)) for w, _ in row_outs]
    out_specs += [pl.BlockSpec(s, lambda i: (0, 0)) for s, _ in acc_outs]
    out_shape = [jax.ShapeDtypeStruct((rows, w), d) for w, d in row_outs]
    out_shape += [jax.ShapeDtypeStruct(s, d) for s, d in acc_outs]
    blocks = (sum(_nbytes((tr, w), a.dtype) for a, w, _ in row_ins) + sum(_nbytes(f.shape, f.dtype) for f in full_ins)
              + sum(_nbytes((tr, w), d) for w, d in row_outs) + sum(_nbytes(s, d) for s, d in acc_outs))
    widest = max([w for _, w, _ in row_ins] + [w for w, _ in row_outs])
    outs = pl.pallas_call(
        kern, name=name, grid=(rows // tr,), in_specs=in_specs, out_specs=out_specs, out_shape=out_shape,
        compiler_params=pltpu.CompilerParams(
            dimension_semantics=("arbitrary",) if acc_outs else ("parallel",),
            vmem_limit_bytes=_vmem_limit(blocks, 8 * _nbytes((tr, widest), F32))),
    )(*[a for a, _, _ in row_ins], *full_ins)
    return outs


def _rms(x, g):
    r = lax.rsqrt(jnp.mean(x * x, axis=-1, keepdims=True) + EPS)
    return x * r * g


def _rms_bwd(x, g, dy):
    r = lax.rsqrt(jnp.mean(x * x, axis=-1, keepdims=True) + EPS)
    xh = x * r
    dxh = dy * g
    dx = r * (dxh - xh * jnp.mean(dxh * xh, axis=-1, keepdims=True))
    return dx, jnp.sum(dy * xh, axis=0, keepdims=True)


def _sigmoid(x):
    return 1.0 / (1.0 + jnp.exp(-x))


def _acc_init(i, refs):
    @pl.when(i == 0)
    def _():
        for r in refs:
            r[...] = jnp.zeros_like(r)


def _rms_fwd_call(x, g, name):
    rows, c = x.shape

    def body(i, ins, fulls, outs, accs):
        outs[0][...] = _rms(ins[0][...], fulls[0][...]).astype(BF16)

    return _rowwise(body, name=name, rows=rows, tr=512, row_ins=[(x, c, 0)], full_ins=[g], row_outs=[(c, BF16)])[0]


def _rms_bwd_call(x, g, dy, res, name):
    rows, c = x.shape
    row_ins = [(x, c, 0), (dy, c, 0)] + ([(res, c, 0)] if res is not None else [])

    def body(i, ins, fulls, outs, accs):
        _acc_init(i, accs)
        dx, dg = _rms_bwd(ins[0][...], fulls[0][...], ins[1][...].astype(F32))
        if res is not None:
            dx = dx + ins[2][...]
        outs[0][...] = dx
        accs[0][...] += dg

    return _rowwise(body, name=name, rows=rows, tr=512, row_ins=row_ins, full_ins=[g], row_outs=[(c, F32)],
                    acc_outs=[((1, c), F32)])


def _rope_tables(pos_col, freq_lane):
    rows = pos_col.shape[0]

    def body(i, ins, fulls, outs, accs):
        ang = ins[0][...].astype(F32) * fulls[0][...]
        lane = lax.broadcasted_iota(jnp.int32, ang.shape, 1)
        cos, sin = jnp.cos(ang), jnp.sin(ang)
        first = (lane >= ROPE_LO) & (lane < ROPE_LO + HALF)
        second = (lane >= ROPE_LO + HALF) & (lane < ROPE_LO + MLA_ROPE)
        outs[0][:, 0:LANE] = jnp.where(first | second, cos, 1.0)
        outs[0][:, LANE:2 * LANE] = jnp.where(first, -sin, 0.0)
        outs[0][:, 2 * LANE:3 * LANE] = jnp.where(second, sin, 0.0)

    return _rowwise(body, name="rope_tables", rows=rows, tr=1024, row_ins=[(pos_col, 1, 0)], full_ins=[freq_lane],
                    row_outs=[(3 * LANE, F32)])[0]


def _rope(x, tab):
    return (x * tab[:, 0:LANE] + pltpu.roll(x, LANE - HALF, 1) * tab[:, LANE:2 * LANE]
            + pltpu.roll(x, HALF, 1) * tab[:, 2 * LANE:3 * LANE])


def _rope_t(dy, tab):
    return (dy * tab[:, 0:LANE] + pltpu.roll(dy * tab[:, LANE:2 * LANE], HALF, 1)
            + pltpu.roll(dy * tab[:, 2 * LANE:3 * LANE], LANE - HALF, 1))


def _mla_prep_fwd(proj, tab, g_q, g_kv):
    rows = proj.shape[0]

    def body(i, ins, fulls, outs, accs):
        outs[0][...] = _rms(ins[0][...], fulls[0][...]).astype(BF16)
        outs[1][...] = _rms(ins[1][...], fulls[1][...]).astype(BF16)
        outs[2][...] = _rope(ins[2][...], ins[3][...])

    return _rowwise(body, name="mla_prep_fwd", rows=rows, tr=512,
                    row_ins=[(proj, MLA_Q_RANK, C_CQ // MLA_Q_RANK), (proj, LANE, C_CKV // LANE),
                             (proj, LANE, C_KR // LANE), (tab, 3 * LANE, 0)],
                    full_ins=[g_q, g_kv], row_outs=[(MLA_Q_RANK, BF16), (MLA_KV_RANK, BF16), (LANE, F32)])


def _mla_prep_bwd(proj, g_q, g_kv, dcqn, dckvn):
    rows = proj.shape[0]

    def body(i, ins, fulls, outs, accs):
        _acc_init(i, accs)
        dcq, dgq = _rms_bwd(ins[0][...], fulls[0][...], ins[2][...])
        dckv, dgkv = _rms_bwd(ins[1][...], fulls[1][...], ins[3][...])
        outs[0][...] = dcq.astype(BF16)
        outs[1][...] = dckv.astype(BF16)
        accs[0][...] += dgq
        accs[1][...] += dgkv

    return _rowwise(body, name="mla_prep_bwd", rows=rows, tr=512,
                    row_ins=[(proj, MLA_Q_RANK, C_CQ // MLA_Q_RANK), (proj, LANE, C_CKV // LANE),
                             (dcqn, MLA_Q_RANK, 0), (dckvn, MLA_KV_RANK, 0)],
                    full_ins=[g_q, g_kv], row_outs=[(MLA_Q_RANK, BF16), (MLA_KV_RANK, BF16)],
                    acc_outs=[((1, MLA_Q_RANK), F32), ((1, MLA_KV_RANK), F32)])


def _mla_rope_fwd(qp, kvp, krope, tab):
    rows = qp.shape[0]
    hw = MLA_HEADS * LANE

    def body(i, ins, fulls, outs, accs):
        t = ins[3][...]
        kr = ins[2][...]
        for h in range(MLA_HEADS):
            sl = slice(h * LANE, (h + 1) * LANE)
            outs[0][:, sl] = _rope(ins[0][:, sl], t).astype(BF16)
            outs[1][:, sl] = (ins[1][:, sl] + kr).astype(BF16)
        outs[2][...] = ins[1][:, hw:2 * hw].astype(BF16)

    return _rowwise(body, name="mla_rope_fwd", rows=rows, tr=512,
                    row_ins=[(qp, hw, 0), (kvp, 2 * hw, 0), (krope, LANE, 0), (tab, 3 * LANE, 0)],
                    row_outs=[(hw, BF16), (hw, BF16), (hw, BF16)])


def _mla_rope_bwd(dq, dk, dv, tab):
    rows = dq.shape[0]
    hw = MLA_HEADS * LANE

    def body(i, ins, fulls, outs, accs):
        t = ins[3][...]
        dkr = jnp.zeros((ins[0].shape[0], LANE), F32)
        for h in range(MLA_HEADS):
            sl = slice(h * LANE, (h + 1) * LANE)
            outs[0][:, sl] = _rope_t(ins[0][:, sl], t).astype(BF16)
            dkr = dkr + ins[1][:, sl]
        outs[1][:, 0:hw] = ins[1][...].astype(BF16)
        outs[1][:, hw:2 * hw] = ins[2][...].astype(BF16)
        lane = lax.broadcasted_iota(jnp.int32, dkr.shape, 1)
        dkr = jnp.where((lane >= ROPE_LO) & (lane < ROPE_LO + MLA_ROPE), dkr, 0.0)
        outs[2][...] = _rope_t(dkr, t).astype(BF16)

    return _rowwise(body, name="mla_rope_bwd", rows=rows, tr=512,
                    row_ins=[(dq, hw, 0), (dk, hw, 0), (dv, hw, 0), (tab, 3 * LANE, 0)],
                    row_outs=[(hw, BF16), (2 * hw, BF16), (LANE, BF16)])


def _dot_nt(a, b):
    return lax.dot_general(a, b, (((1,), (1,)), ((), ())), preferred_element_type=F32)


def _dot_tn(a, b):
    return lax.dot_general(a, b, (((0,), (0,)), ((), ())), preferred_element_type=F32)


def _dot(a, b):
    return jnp.dot(a, b, preferred_element_type=F32)


def _attn_params(s, t, n_res_f32, n_res_bf16, ride=False):
    blocks = n_res_f32 * _nbytes((s, LANE), F32) + n_res_bf16 * _nbytes((s, LANE), BF16) + 6 * _nbytes((t, LANE), F32)
    return pltpu.CompilerParams(dimension_semantics=("arbitrary" if ride else "parallel", "arbitrary"),
                                vmem_limit_bytes=_vmem_limit(blocks, 12 * _nbytes((t, t), F32)))


def _mla_fwd(q, k, v, t, shards=()):
    s, hw = q.shape
    heads, nq = hw // LANE, s // t
    ng = len(shards)
    scale = 1.0 / math.sqrt(MLA_NOPE + MLA_ROPE)
    scale2 = scale * LOG2E

    def body(q_ref, k_ref, v_ref, *rest):
        o_ref, l_ref = rest[ng], rest[ng + 1]
        h, i = pl.program_id(0), pl.program_id(1)
        if ng:
            gather = _GatherPhases(rest[:ng], rest[ng + 2:2 * ng + 2], *rest[2 * ng + 2:])
            pl.when((h == 0) & (i == 0))(gather.send)
            pl.when((h == heads // 2) & (i == 0))(gather.forward)
        qv = q_ref[...]

        def step(j, carry, masked):
            m, l, acc = carry
            sl = pl.ds(pl.multiple_of(j * t, t), t)
            sc = _dot_nt(qv, k_ref[sl, :])
            if masked:
                row = lax.broadcasted_iota(jnp.int32, (t, t), 0)
                col = lax.broadcasted_iota(jnp.int32, (t, t), 1)
                sc = jnp.where(col <= row, sc, -1e30)
            m_new = jnp.maximum(m, jnp.max(sc, axis=1, keepdims=True))
            p = jnp.exp2((sc - m_new) * scale2)
            alpha = jnp.exp2((m - m_new) * scale2)
            l = alpha * l + jnp.sum(p, axis=1, keepdims=True)
            acc = alpha * acc + _dot(p.astype(BF16), v_ref[sl, :])
            return m_new, l, acc

        init = (jnp.full((t, 1), -1e30, F32), jnp.zeros((t, 1), F32), jnp.zeros((t, LANE), F32))
        carry = lax.fori_loop(0, i, lambda j, c: step(j, c, False), init)
        m, l, acc = step(i, carry, True)
        o_ref[...] = (acc / l).astype(o_ref.dtype)
        l_ref[0] = m * scale2 + jnp.log2(l)
        if ng:
            pl.when((h == heads - 1) & (i == nq - 1))(gather.finish)

    blk = pl.BlockSpec((t, LANE), lambda h, i: (i, h))
    res = pl.BlockSpec((s, LANE), lambda h, i: (0, h))
    outs = pl.pallas_call(
        body, name="mla_fwd", grid=(heads, nq), in_specs=[blk, res, res] + [ANY] * ng,
        out_specs=[blk, pl.BlockSpec((1, t, 1), lambda h, i: (h, i, 0))] + [ANY] * ng,
        out_shape=[jax.ShapeDtypeStruct((s, hw), BF16), jax.ShapeDtypeStruct((heads, s, 1), F32)]
        + [jax.ShapeDtypeStruct((N_CHIPS,) + sh.shape, sh.dtype) for sh in shards],
        scratch_shapes=_dma_sems(6 * ng) if ng else [],
        compiler_params=_attn_params(s, t, 0, 2, ride=ng > 0),
    )(q, k, v, *shards)
    return outs[0], outs[1], list(outs[2:])


def _mla_bwd(q, k, v, o, do, lse, t, parts=()):
    s, hw = q.shape
    heads, nq = hw // LANE, s // t
    ns = len(parts)
    scale = 1.0 / math.sqrt(MLA_NOPE + MLA_ROPE)
    scale2 = scale * LOG2E

    def body(q_ref, k_ref, v_ref, o_ref, do_ref, l_ref, *rest):
        dq_ref, dk_ref, dv_ref = rest[ns:ns + 3]
        h, i = pl.program_id(0), pl.program_id(1)
        finish = _ride_along(_scatter_copies, ns, (rest[:ns], rest[ns + 3:2 * ns + 3], *rest[2 * ns + 3:]),
                             (h == 0) & (i == 0), (h == heads - 1) & (i == nq - 1))

        @pl.when(i == 0)
        def _():
            dk_ref[...] = jnp.zeros_like(dk_ref)
            dv_ref[...] = jnp.zeros_like(dv_ref)

        qv, dov, lv = q_ref[...], do_ref[...], l_ref[0]
        dlt = jnp.sum(dov.astype(F32) * o_ref[...].astype(F32), axis=1, keepdims=True)

        def step(j, dq, masked):
            sl = pl.ds(pl.multiple_of(j * t, t), t)
            kv, vv = k_ref[sl, :], v_ref[sl, :]
            p = jnp.exp2(_dot_nt(qv, kv) * scale2 - lv)
            if masked:
                row = lax.broadcasted_iota(jnp.int32, (t, t), 0)
                col = lax.broadcasted_iota(jnp.int32, (t, t), 1)
                p = jnp.where(col <= row, p, 0.0)
            ds = (p * (_dot_nt(dov, vv) - dlt)).astype(BF16)
            dk_ref[sl, :] += _dot_tn(ds, qv) * scale
            dv_ref[sl, :] += _dot_tn(p.astype(BF16), dov)
            return dq + _dot(ds, kv)

        dq = lax.fori_loop(0, i, lambda j, c: step(j, c, False), jnp.zeros((t, LANE), F32))
        dq_ref[...] = step(i, dq, True) * scale
        finish()

    blk = pl.BlockSpec((t, LANE), lambda h, i: (i, h))
    res = pl.BlockSpec((s, LANE), lambda h, i: (0, h))
    full = jax.ShapeDtypeStruct((s, hw), F32)
    outs = pl.pallas_call(
        body, name="mla_bwd", grid=(heads, nq),
        in_specs=[blk, res, res, blk, blk, pl.BlockSpec((1, t, 1), lambda h, i: (h, i, 0))] + [ANY] * ns,
        out_specs=[blk, res, res] + [ANY] * ns, out_shape=[full, full, full] + _scatter_shapes(parts),
        scratch_shapes=_dma_sems(3 * ns) if ns else [],
        compiler_params=_attn_params(s, t, 2, 2, ride=ns > 0),
    )(q, k, v, o, do, lse, *parts)
    return outs[0], outs[1], outs[2], list(outs[3:])


def _sb_logits(qv, kv, scale, masked, t, upper):
    z = _dot_nt(qv, kv) * scale
    e = jnp.exp(-jnp.abs(z))
    l1p = jnp.log(1.0 + e)
    lb = jnp.minimum(z, 0.0) - l1p
    lo = -jnp.maximum(z, 0.0) - l1p
    keep = None
    if masked:
        row = lax.broadcasted_iota(jnp.int32, (t, t), 0)
        col = lax.broadcasted_iota(jnp.int32, (t, t), 1)
        keep = col < row
        lo = jnp.where(keep, lo, 0.0)
    hi = lo.astype(BF16)
    rem = (lo - hi.astype(F32)).astype(BF16)
    suf = _dot(hi, upper) + _dot(rem, upper)
    return z, e, lb, lo, suf, keep


def _tri(t, inclusive):
    row = lax.broadcasted_iota(jnp.int32, (t, t), 0)
    col = lax.broadcasted_iota(jnp.int32, (t, t), 1)
    return jnp.where((row >= col) if inclusive else (row > col), 1.0, 0.0).astype(BF16)


SB_CHAINS = 2


def _sb_walk(i, first, carries_of):
    n = SB_CHAINS
    carries = [first(c) for c in range(n)]
    width = len(carries[0])

    def alive(carry):
        return jnp.max(carry[0]) >= SB_ZERO_LOG

    def split(st):
        return [tuple(st[1 + c * width:1 + (c + 1) * width]) for c in range(n)]

    def live(st):
        any_alive = alive(split(st)[0])
        for cr in split(st)[1:]:
            any_alive = any_alive | alive(cr)
        return (st[0] <= n * i) & any_alive

    def more(st):
        out = (st[0] + 1,)
        for c, cr in enumerate(split(st)):
            out += tuple(carries_of(c, st[0], cr))
        return out

    st = lax.while_loop(live, more, (jnp.int32(1),) + tuple(x for cr in carries for x in cr))
    jj, carries = st[0], split(st)
    for c in range(1, n):
        def live_c(s2, c=c):
            return (s2[0] <= n * i + c) & alive(s2[1:])

        def more_c(s2, c=c):
            return (s2[0] + 1,) + tuple(carries_of(c, s2[0], s2[1:]))

        carries[c] = lax.while_loop(live_c, more_c, (jj,) + tuple(carries[c]))[1:]
    return carries


def _sb_fwd(proj, t):
    s = proj.shape[0]
    heads, nq, n = SB_HEADS, s // t, SB_CHAINS
    scale = 1.0 / math.sqrt(SB_HEAD_DIM)

    def body(q_ref, k_ref, v_ref, o_ref):
        i = pl.program_id(1)
        upper = _tri(t, False)
        qs = [q_ref[c * t:(c + 1) * t, :].astype(BF16) for c in range(n)]

        def step(c, jj, carry, masked):
            run, acc = carry
            sl = pl.ds(pl.multiple_of((n * i + c - jj) * t, t), t)
            _, _, lb, lo, suf, keep = _sb_logits(qs[c], k_ref[sl, :].astype(BF16), scale, masked, t, upper)
            a = jnp.exp(lb + suf + run)
            if masked:
                a = jnp.where(keep, a, 0.0)
            acc = acc + _dot(a.astype(BF16), v_ref[sl, :].astype(BF16))
            return run + jnp.sum(lo, axis=1, keepdims=True), acc

        init = (jnp.zeros((t, 1), F32), jnp.zeros((t, LANE), F32))
        carries = _sb_walk(i, lambda c: step(c, 0, init, True), lambda c, jj, cr: step(c, jj, cr, False))
        for c in range(n):
            o_ref[c * t:(c + 1) * t, :] = carries[c][1]

    return pl.pallas_call(
        body, name="sb_fwd", grid=(heads, nq // n),
        in_specs=[pl.BlockSpec((n * t, LANE), lambda h, i: (i, C_SBQ // LANE + h)),
                  pl.BlockSpec((s, LANE), lambda h, i: (0, C_SBK // LANE + h)),
                  pl.BlockSpec((s, LANE), lambda h, i: (0, C_SBV // LANE + h))],
        out_specs=pl.BlockSpec((n * t, LANE), lambda h, i: (i, h)),
        out_shape=jax.ShapeDtypeStruct((s, heads * LANE), F32),
        compiler_params=_attn_params(s, n * t, 2, 0),
    )(proj, proj, proj)


def _sb_bwd(proj, o, do, t, stacked=()):
    s = proj.shape[0]
    heads, nq, n = SB_HEADS, s // t, SB_CHAINS
    nx = len(stacked)
    scale = 1.0 / math.sqrt(SB_HEAD_DIM)

    def body(q_ref, k_ref, v_ref, o_ref, do_ref, *rest):
        dq_ref, dk_ref, dv_ref = rest[nx:nx + 3]
        hd, i = pl.program_id(0), pl.program_id(1)
        finish = _ride_along(_exchange_copies, nx, (rest[:nx], rest[nx + 3:2 * nx + 3], *rest[2 * nx + 3:]),
                             (hd == 0) & (i == 0), (hd == heads - 1) & (i == nq // n - 1))

        @pl.when(i == 0)
        def _():
            dk_ref[...] = jnp.zeros_like(dk_ref)
            dv_ref[...] = jnp.zeros_like(dv_ref)

        rows = [slice(c * t, (c + 1) * t) for c in range(n)]
        qs = [q_ref[r, :].astype(BF16) for r in rows]
        dos = [do_ref[r, :] for r in rows]
        totals = [jnp.sum(dos[c].astype(F32) * o_ref[rows[c], :], axis=1, keepdims=True) for c in range(n)]
        upper, upper_incl = _tri(t, False), _tri(t, True)

        def step(c, jj, carry, masked):
            run, g, dq = carry
            qv, dov = qs[c], dos[c]
            sl = pl.ds(pl.multiple_of((n * i + c - jj) * t, t), t)
            kv, vv = k_ref[sl, :].astype(BF16), v_ref[sl, :].astype(BF16)
            z, e, lb, lo, suf, keep = _sb_logits(qv, kv, scale, masked, t, upper)
            tail = suf + run
            a = jnp.exp(lb + tail)
            if masked:
                a = jnp.where(keep, a, 0.0)
            ab = a.astype(BF16)
            gr = ab.astype(F32) * _dot_nt(dov, vv)
            ghi = gr.astype(BF16)
            grem = (gr - ghi.astype(F32)).astype(BF16)
            before = totals[c] - g - (_dot(ghi, upper_incl) + _dot(grem, upper_incl))
            before = jnp.where(tail < SB_ZERO_LOG, 0.0, before)
            r = 1.0 / (1.0 + e)
            pos = z >= 0.0
            dz = r * (gr * jnp.where(pos, e, 1.0) - before * jnp.where(pos, 1.0, e))
            if masked:
                dz = jnp.where(keep, dz, 0.0)
            dzb = (dz * scale).astype(BF16)
            dk_ref[sl, :] += _dot_tn(dzb, qv)
            dv_ref[sl, :] += _dot_tn(ab, dov)
            return (run + jnp.sum(lo, axis=1, keepdims=True), g + jnp.sum(gr, axis=1, keepdims=True),
                    dq + _dot(dzb, kv))

        zero = jnp.zeros((t, 1), F32)
        init = (zero, zero, jnp.zeros((t, LANE), F32))
        carries = _sb_walk(i, lambda c: step(c, 0, init, True), lambda c, jj, cr: step(c, jj, cr, False))
        for c in range(n):
            dq_ref[rows[c], :] = carries[c][2]
        finish()

    blk = pl.BlockSpec((n * t, LANE), lambda h, i: (i, h))
    res = pl.BlockSpec((s, LANE), lambda h, i: (0, h))
    full = jax.ShapeDtypeStruct((s, heads * LANE), F32)
    outs = pl.pallas_call(
        body, name="sb_bwd", grid=(heads, nq // n),
        in_specs=[pl.BlockSpec((n * t, LANE), lambda h, i: (i, C_SBQ // LANE + h)),
                  pl.BlockSpec((s, LANE), lambda h, i: (0, C_SBK // LANE + h)),
                  pl.BlockSpec((s, LANE), lambda h, i: (0, C_SBV // LANE + h)), blk, blk] + [ANY] * nx,
        out_specs=[blk, res, res] + [ANY] * nx, out_shape=[full, full, full] + _exchange_shapes(stacked),
        scratch_shapes=_dma_sems(nx) if nx else [],
        compiler_params=_attn_params(s, n * t, 4, 0, ride=nx > 0),
    )(proj, proj, proj, o, do, *stacked)
    return outs[0], outs[1], outs[2], list(outs[3:])


def _xattn_probs(qh, kh):
    sc = _dot_nt(qh, kh) * (1.0 / math.sqrt(X_HEAD_DIM))
    p = jnp.exp(sc - jnp.max(sc, axis=1, keepdims=True))
    return p / jnp.sum(p, axis=1, keepdims=True)


def _xattn_fwd(xq, xkv):
    rows = xq.shape[0]
    w = X_HEADS * X_HEAD_DIM

    def body(i, ins, fulls, outs, accs):
        for h in range(X_HEADS):
            sl = slice(h * LANE, (h + 1) * LANE)
            p = _xattn_probs(ins[0][:, sl], fulls[0][:, sl])
            outs[0][:, sl] = _dot(p.astype(BF16), fulls[0][:, w + h * LANE:w + (h + 1) * LANE]).astype(BF16)

    return _rowwise(body, name="xattn_fwd", rows=rows, tr=512, row_ins=[(xq, w, 0)], full_ins=[xkv],
                    row_outs=[(w, BF16)])[0]


def _xattn_bwd(xq, xkv, dxo):
    rows = xq.shape[0]
    w = X_HEADS * X_HEAD_DIM

    def body(i, ins, fulls, outs, accs):
        _acc_init(i, accs)
        for h in range(X_HEADS):
            sl = slice(h * LANE, (h + 1) * LANE)
            slv = slice(w + h * LANE, w + (h + 1) * LANE)
            qh, kh, vh, doh = ins[0][:, sl], fulls[0][:, sl], fulls[0][:, slv], ins[1][:, sl]
            p = _xattn_probs(qh, kh)
            dp = _dot_nt(doh, vh)
            ds = (p * (dp - jnp.sum(p * dp, axis=1, keepdims=True)) * (1.0 / math.sqrt(X_HEAD_DIM))).astype(BF16)
            outs[0][:, sl] = _dot(ds, kh).astype(BF16)
            accs[0][:, sl] += _dot_tn(ds, qh)
            accs[0][:, slv] += _dot_tn(p.astype(BF16), doh)

    return _rowwise(body, name="xattn_bwd", rows=rows, tr=512, row_ins=[(xq, w, 0), (dxo, w, 0)], full_ins=[xkv],
                    row_outs=[(w, BF16)], acc_outs=[(xkv.shape, F32)])


def _gate_fwd(proj, pa, pb, b_gate):
    rows = proj.shape[0]

    def body(i, ins, fulls, outs, accs):
        sa = _sigmoid(ins[0][...] + fulls[0][0:1, :])
        sb = _sigmoid(ins[1][...] + fulls[0][1:2, :])
        outs[0][...] = (sa * ins[2][...] + sb * ins[3][...]).astype(BF16)

    return _rowwise(body, name="gate_fwd", rows=rows, tr=512,
                    row_ins=[(proj, D_MODEL, C_GA // D_MODEL), (proj, D_MODEL, C_GB // D_MODEL), (pa, D_MODEL, 0),
                             (pb, D_MODEL, 0)],
                    full_ins=[b_gate], row_outs=[(D_MODEL, BF16)])[0]


def _gate_bwd(proj, pa, pb, b_gate, dm):
    rows = proj.shape[0]

    def body(i, ins, fulls, outs, accs):
        _acc_init(i, accs)
        d = ins[4][...]
        sa = _sigmoid(ins[0][...] + fulls[0][0:1, :])
        sb = _sigmoid(ins[1][...] + fulls[0][1:2, :])
        dga = d * ins[2][...] * sa * (1.0 - sa)
        dgb = d * ins[3][...] * sb * (1.0 - sb)
        outs[0][...] = (d * sa).astype(BF16)
        outs[1][...] = (d * sb).astype(BF16)
        outs[2][...] = dga.astype(BF16)
        outs[3][...] = dgb.astype(BF16)
        accs[0][0:1, :] += jnp.sum(dga, axis=0, keepdims=True)
        accs[0][1:2, :] += jnp.sum(dgb, axis=0, keepdims=True)

    return _rowwise(body, name="gate_bwd", rows=rows, tr=512,
                    row_ins=[(proj, D_MODEL, C_GA // D_MODEL), (proj, D_MODEL, C_GB // D_MODEL), (pa, D_MODEL, 0),
                             (pb, D_MODEL, 0), (dm, D_MODEL, 0)],
                    full_ins=[b_gate], row_outs=[(D_MODEL, BF16)] * 4, acc_outs=[((2, D_MODEL), F32)])


def _loss_head(x3, target, g_final):
    rows = x3.shape[0]

    def body(i, ins, fulls, outs, accs):
        _acc_init(i, accs)
        xv, g = ins[0][...], fulls[0][...]
        d = _rms(xv, g) - ins[1][...]
        dx, dg = _rms_bwd(xv, g, d * (1.0 / D_MODEL))
        outs[0][...] = dx
        accs[0][...] += dg
        accs[1][...] += jnp.sum(d * d, axis=0, keepdims=True)

    return _rowwise(body, name="loss_head", rows=rows, tr=512, row_ins=[(x3, D_MODEL, 0), (target, D_MODEL, 0)],
                    full_ins=[g_final], row_outs=[(D_MODEL, F32)], acc_outs=[((1, D_MODEL), F32), ((1, D_MODEL), F32)])


def _adamw(w, g, m, v, name):
    rows, c = w.shape

    def body(i, ins, fulls, outs, accs):
        wv, gv = ins[0][...], ins[1][...]
        mn = ADAM_B1 * ins[2][...] + (1.0 - ADAM_B1) * gv
        vn = ADAM_B2 * ins[3][...] + (1.0 - ADAM_B2) * jnp.square(gv)
        m_hat = mn / (1.0 - ADAM_B1 ** ADAM_STEP)
        v_hat = vn / (1.0 - ADAM_B2 ** ADAM_STEP)
        outs[0][...] = -ADAM_LR * (m_hat / (jnp.sqrt(v_hat) + ADAM_EPS) + ADAM_WD * wv)
        outs[1][...] = mn
        outs[2][...] = vn

    return _rowwise(body, name=name, rows=rows, tr=_row_tile(rows, 256), row_ins=[(a, c, 0) for a in (w, g, m, v)],
                    row_outs=[(c, F32)] * 3)


def _place():
    x, y, c = lax.axis_index("x"), lax.axis_index("y"), lax.axis_index("c")
    chips = [(1 - x, y), (x, 1 - y), (1 - x, 1 - y)]
    return x, y, c, chips


ANY = pl.BlockSpec(memory_space=pl.ANY)


def _remote(src, dst, send_sem, recv_sem, to):
    return pltpu.make_async_remote_copy(src_ref=src, dst_ref=dst, send_sem=send_sem, recv_sem=recv_sem,
                                        device_id=to, device_id_type=MESH)


def _dma_sems(n):
    return [pltpu.SemaphoreType.DMA((n,)), pltpu.SemaphoreType.DMA((n,))]


class _GatherPhases:
    def __init__(self, ins, outs, send_sems, recv_sems):
        self.ins, self.outs, self.send_sems, self.recv_sems = ins, outs, send_sems, recv_sems
        self.x, self.y, self.c, self.chips = _place()
        self.me = 2 * self.x + self.y

    def _copy(self, t, j, chip_idx, hlf, to, src=None):
        h = self.ins[t].shape[0] // 2
        dst = self.outs[t].at[chip_idx, pl.ds(hlf * h, h), :]
        return _remote(dst if src is None else src, dst, self.send_sems.at[6 * t + j], self.recv_sems.at[6 * t + j], to)

    def _sends(self):
        out = []
        for t in range(len(self.ins)):
            h = self.ins[t].shape[0] // 2
            for j, chip in enumerate(self.chips):
                out.append(self._copy(t, j, self.me, self.c, (*chip, self.c), src=self.ins[t].at[pl.ds(self.c * h, h), :]))
        return out

    def _forwards(self):
        return [self._copy(t, 3 + j, 2 * chip[0] + chip[1], self.c, (self.x, self.y, 1 - self.c))
                for t in range(len(self.ins)) for j, chip in enumerate(self.chips)]

    def send(self):
        for cp in self._sends():
            cp.start()

    def forward(self):
        here = (self.x, self.y, self.c)
        landed = [self._copy(t, j, 2 * chip[0] + chip[1], self.c, here)
                  for t in range(len(self.ins)) for j, chip in enumerate(self.chips)]
        for arrival, fwd in zip(landed, self._forwards()):
            arrival.wait_recv()
            fwd.start()

    def finish(self):
        here = (self.x, self.y, self.c)
        for t in range(len(self.ins)):
            for j, chip in enumerate(self.chips):
                self._copy(t, 3 + j, 2 * chip[0] + chip[1], 1 - self.c, here).wait_recv()
        for cp in self._sends() + self._forwards():
            cp.wait_send()


def _all_gather_weights(shards):
    n = len(shards)

    def body(*refs):
        gather = _GatherPhases(refs[:n], refs[n:2 * n], *refs[2 * n:])
        gather.send()
        gather.forward()
        gather.finish()

    return pl.pallas_call(
        body, name="all_gather_weights", in_specs=[ANY] * n, out_specs=[ANY] * n,
        out_shape=[jax.ShapeDtypeStruct((N_CHIPS,) + s.shape, s.dtype) for s in shards],
        scratch_shapes=_dma_sems(6 * n),
    )(*shards)


def _exchange_copies(ins, outs, send_sems, recv_sems):
    x, y, c, _ = _place()
    cps = []
    for t in range(len(ins)):
        h = ins[t].shape[1] // 2
        cps.append(_remote(ins[t].at[:, pl.ds((1 - c) * h, h), :], outs[t], send_sems.at[t], recv_sems.at[t],
                           (x, y, 1 - c)))
    return cps


def _exchange_shapes(stacked):
    return [jax.ShapeDtypeStruct((N_CHIPS, s.shape[1] // 2, s.shape[2]), s.dtype) for s in stacked]


def _scatter_copies(ins, outs, send_sems, recv_sems):
    x, y, c, chips = _place()
    return [_remote(ins[t].at[2 * chip[0] + chip[1]], outs[t].at[j], send_sems.at[3 * t + j], recv_sems.at[3 * t + j],
                    (*chip, c)) for t in range(len(ins)) for j, chip in enumerate(chips)]


def _scatter_shapes(parts):
    return [jax.ShapeDtypeStruct((N_CHIPS - 1,) + s.shape[1:], s.dtype) for s in parts]


def _ride_along(copies_of, n, refs, first, last):
    if not n:
        return lambda: None

    def start():
        for cp in copies_of(*refs):
            cp.start()

    def wait():
        for cp in copies_of(*refs):
            cp.wait()

    pl.when(first)(start)
    return lambda: pl.when(last)(wait)


def _copy_call(copies_of, arrs, out_shape, sems_per, name):
    n = len(arrs)

    def body(*refs):
        cps = copies_of(refs[:n], refs[n:2 * n], *refs[2 * n:])
        for cp in cps:
            cp.start()
        for cp in cps:
            cp.wait()

    return pl.pallas_call(body, name=name, in_specs=[ANY] * n, out_specs=[ANY] * n, out_shape=out_shape,
                          scratch_shapes=_dma_sems(sems_per * n))(*arrs)


def _pair_exchange_grads(stacked, name):
    return _copy_call(_exchange_copies, stacked, _exchange_shapes(stacked), 1, name)


def _chip_scatter(parts, name):
    return _copy_call(_scatter_copies, parts, _scatter_shapes(parts), 3, name)


def _pair_exchange_halves(shards):
    n = len(shards)

    def body(*refs):
        bufs = refs[n:2 * n]
        send_sems, recv_sems = refs[2 * n:]
        x, y, c, _ = _place()
        cps = []
        for t in range(n):
            h = bufs[t].shape[0] // 2
            rows = bufs[t].at[pl.ds(c * h, h), :]
            cps.append(_remote(rows, rows, send_sems.at[t], recv_sems.at[t], (x, y, 1 - c)))
            cps[-1].start()
        for cp in cps:
            cp.wait()

    return pl.pallas_call(
        body, name="pair_exchange_halves", in_specs=[ANY] * n, out_specs=[ANY] * n,
        out_shape=[jax.ShapeDtypeStruct(s.shape, s.dtype) for s in shards],
        input_output_aliases={t: t for t in range(n)},
        scratch_shapes=_dma_sems(n),
    )(*shards)


def _pair_sum(gs, recv, place, name):
    _, r, cols = gs.shape
    h = r // 2

    def kern(p_ref, a_ref, b_ref, o_ref):
        o_ref[...] = (a_ref[...] + b_ref[...]).astype(BF16)

    blk = lambda f: pl.BlockSpec((1, h, cols), f)
    return pl.pallas_call(
        kern, name=name,
        grid_spec=pltpu.PrefetchScalarGridSpec(
            num_scalar_prefetch=1, grid=(N_CHIPS,),
            in_specs=[blk(lambda d, p: (d, p[1], 0)), blk(lambda d, p: (d, 0, 0))],
            out_specs=blk(lambda d, p: (d, 0, 0))),
        out_shape=jax.ShapeDtypeStruct((N_CHIPS, h, cols), BF16),
        compiler_params=pltpu.CompilerParams(dimension_semantics=("arbitrary",),
                                             vmem_limit_bytes=_vmem_limit(3 * _nbytes((h, cols), F32), 0)),
    )(place, gs, recv)


def _chip_sum(gs, recv, got, place, name):
    _, r, cols = gs.shape
    h = r // 2

    def kern(p_ref, a_ref, b_ref, g0, g1, g2, o_ref):
        own = a_ref[0] + b_ref[0]
        o_ref[...] = ((own + g0[0].astype(F32)) + g1[0].astype(F32)) + g2[0].astype(F32)

    blk = lambda f: pl.BlockSpec((1, h, cols), f)
    return pl.pallas_call(
        kern, name=name,
        grid_spec=pltpu.PrefetchScalarGridSpec(
            num_scalar_prefetch=1, grid=(1,),
            in_specs=[blk(lambda i, p: (p[0], p[1], 0)), blk(lambda i, p: (p[0], 0, 0)), blk(lambda i, p: (0, 0, 0)),
                      blk(lambda i, p: (1, 0, 0)), blk(lambda i, p: (2, 0, 0))],
            out_specs=pl.BlockSpec((h, cols), lambda i, p: (p[1], 0))),
        out_shape=jax.ShapeDtypeStruct((r, cols), F32),
        compiler_params=pltpu.CompilerParams(dimension_semantics=("arbitrary",),
                                             vmem_limit_bytes=_vmem_limit(5 * _nbytes((h, cols), F32), 0)),
    )(place, gs, recv, got, got, got)


def _all_reduce_small(vec, name):
    r, cols = vec.shape

    def body(in_ref, out_ref, gath, send_sems, recv_sems):
        x, y, c, _ = _place()
        me = 4 * x + 2 * y + c
        gath[me] = in_ref[...]
        sends = []
        for k in range(1, 8):
            to = (x ^ (k >> 2), y ^ ((k >> 1) & 1), c ^ (k & 1))
            cp = pltpu.make_async_remote_copy(src_ref=in_ref, dst_ref=gath.at[me], send_sem=send_sems.at[k - 1],
                                              recv_sem=recv_sems.at[k - 1], device_id=to, device_id_type=MESH)
            cp.start()
            sends.append(cp)
        for k in range(1, 8):
            peer = me ^ k
            pltpu.make_async_remote_copy(src_ref=in_ref, dst_ref=gath.at[peer], send_sem=send_sems.at[k - 1],
                                         recv_sem=recv_sems.at[k - 1], device_id=(x, y, c),
                                         device_id_type=MESH).wait_recv()
        for cp in sends:
            cp.wait_send()
        acc = gath[0]
        for d in range(1, 8):
            acc = acc + gath[d]
        out_ref[...] = acc

    vm = pl.BlockSpec(memory_space=pltpu.VMEM)
    return pl.pallas_call(
        body, name=name, in_specs=[vm], out_specs=vm,
        out_shape=jax.ShapeDtypeStruct((r, cols), F32),
        scratch_shapes=[pltpu.VMEM((8, r, cols), F32), pltpu.SemaphoreType.DMA((7,)), pltpu.SemaphoreType.DMA((7,))],
    )(vec)


def _pad_heads(w, heads, dim, axis):
    shp = w.shape[:axis] + (heads, dim) + w.shape[axis + 1:]
    pad = [(0, 0)] * len(shp)
    pad[axis + 1] = (0, LANE - dim)
    w = jnp.pad(w.reshape(shp), pad)
    return w.reshape(w.shape[:axis] + (heads * LANE,) + w.shape[axis + 2:])


def _unpad_heads(w, heads, dim, axis):
    shp = w.shape[:axis] + (heads, LANE) + w.shape[axis + 1:]
    w = lax.slice_in_dim(w.reshape(shp), 0, dim, axis=axis + 1)
    return w.reshape(w.shape[:axis] + (heads * dim,) + w.shape[axis + 2:])


def _w_in_layout(w_in):
    kr = jnp.pad(w_in[:, 384:416], ((0, 0), (ROPE_LO, LANE - ROPE_LO - MLA_ROPE)))
    sb = lambda lo: _pad_heads(w_in[:, lo:lo + 512], SB_HEADS, SB_HEAD_DIM, 1)
    return jnp.concatenate([w_in[:, 1952:2976], w_in[:, 2976:4000], sb(416), sb(928), sb(1440), w_in[:, 0:256],
                            w_in[:, 256:384], kr], axis=1)


def _w_in_unlayout(d):
    sb = lambda lo: _unpad_heads(d[:, lo:lo + 1024], SB_HEADS, SB_HEAD_DIM, 1)
    return jnp.concatenate([d[:, C_CQ:C_CQ + 256], d[:, C_CKV:C_CKV + 128], d[:, C_KR + ROPE_LO:C_KR + ROPE_LO + MLA_ROPE],
                            sb(C_SBQ), sb(C_SBK), sb(C_SBV), d[:, C_GA:C_GA + 1024], d[:, C_GB:C_GB + 1024]], axis=1)


def _w_ukv_layout(w):
    w3 = w.reshape(MLA_KV_RANK, MLA_HEADS, MLA_NOPE + MLA_V)
    pad = lambda part: jnp.pad(part, ((0, 0), (0, 0), (0, LANE - part.shape[2]))).reshape(MLA_KV_RANK, MLA_HEADS * LANE)
    return jnp.concatenate([pad(w3[:, :, :MLA_NOPE]), pad(w3[:, :, MLA_NOPE:])], axis=1)


def _w_ukv_unlayout(d):
    hw = MLA_HEADS * LANE
    kpart = d[:, :hw].reshape(MLA_KV_RANK, MLA_HEADS, LANE)[:, :, :MLA_NOPE]
    vpart = d[:, hw:].reshape(MLA_KV_RANK, MLA_HEADS, LANE)[:, :, :MLA_V]
    return jnp.concatenate([kpart, vpart], axis=2).reshape(MLA_KV_RANK, MLA_HEADS * (MLA_NOPE + MLA_V))


def _shard_of(full, d, axis):
    n = full.shape[axis] // N_CHIPS
    return lax.slice_in_dim(full, d * n, (d + 1) * n, axis=axis)


def _local_step(x, mem, pos, target, w, t_mla, t_sb, late=None, reduce=None):
    s = x.shape[0]
    w = dict(w)
    win = _w_in_layout(w["w_in"])
    wuq = _pad_heads(w["w_uq"], MLA_HEADS, MLA_NOPE + MLA_ROPE, 1)
    wkv = _w_ukv_layout(w["w_ukv"])
    inv_freq = ROPE_THETA ** (-jnp.arange(0, MLA_ROPE, 2, dtype=F32) / MLA_ROPE)
    freq_lane = jnp.pad(jnp.concatenate([inv_freq, inv_freq]), (ROPE_LO, LANE - ROPE_LO - MLA_ROPE)).reshape(1, LANE)
    add = lambda accs, ex: (accs[0] + ex[0],)

    tab = _rope_tables(pos.reshape(s, 1), freq_lane)
    h = _rms_fwd_call(x, w["g_mix"], "rms_mix")
    proj = _mm(h, [win], name="proj_in", tn=1408)
    cqn, ckvn, krope = _mla_prep_fwd(proj, tab, w["g_q_lat"], w["g_kv_lat"])
    qp = _mm(cqn, [wuq], name="q_up")
    kvp = _mm(ckvn, [wkv], name="kv_up")
    qa, ka, va = _mla_rope_fwd(qp, kvp, krope, tab)
    o_a, lse, gathered = _mla_fwd(qa, ka, va, t_mla, late[0] if late else ())
    if late:
        w.update(late[1](gathered))
    wa = _pad_heads(w["w_a_proj"], MLA_HEADS, MLA_V, 0)
    wb = _pad_heads(w["w_b_proj"], SB_HEADS, SB_HEAD_DIM, 0)
    o_b = _sb_fwd(proj, t_sb)
    pa = _mm(o_a, [wa], name="proj_a")
    pb = _mm(o_b, [wb], name="proj_b")
    merged = _gate_fwd(proj, pa, pb, w["b_gate"])
    x1 = _mm(merged, [w["w_o"]], name="proj_o", extras=(x,), epilogue=add)
    hx = _rms_fwd_call(x1, w["g_x"], "rms_x")
    mn = _rms_fwd_call(mem, w["g_mem"], "rms_mem")
    xq = _mm(hx, [w["w_xq"]], name="xq", out_dtypes=(BF16,))
    xkv = _mm(mn, [w["w_xkv"]], name="xkv", out_dtypes=(BF16,))
    xo = _xattn_fwd(xq, xkv)
    x2 = _mm(xo, [w["w_xo"]], name="proj_xo", extras=(x1,), epilogue=add)
    hf = _rms_fwd_call(x2, w["g_ffn"], "rms_ffn")

    def swiglu(accs, ex):
        a, b = accs
        return a, b, a * _sigmoid(a) * b

    ga, gu, hmid = _mm(hf, [w["w_gate"], w["w_up"]], name="ffn_up", epilogue=swiglu, out_dtypes=(BF16, BF16, BF16),
                       tm=512, tn=1408)
    x3 = _mm(hmid, [w["w_down"]], name="ffn_down", extras=(x2,), epilogue=add, tk=2816)

    dx3, dg_final, sq = _loss_head(x3, target, w["g_final"].reshape(1, D_MODEL))
    g = {"g_final": dg_final.reshape(D_MODEL)}

    def swiglu_bwd(accs, ex):
        dh, a, b = accs[0], ex[0].astype(F32), ex[1].astype(F32)
        sg = _sigmoid(a)
        return dh * b * sg * (1.0 + a * (1.0 - sg)), dh * a * sg

    da, db = _mm(dx3, [w["w_down"]], name="ffn_down_dx", tb=True, extras=(ga, gu), epilogue=swiglu_bwd,
                 out_dtypes=(BF16, BF16), tm=512, tn=1408)
    g["w_down"] = _mm(hmid, [dx3], name="ffn_down_dw", ta=True, tm=1408)
    g["w_gate"] = _mm(hf, [da], name="ffn_gate_dw", ta=True, tn=1408, tk=2048)
    g["w_up"] = _mm(hf, [db], name="ffn_up_dw", ta=True, tn=1408)
    dhf = _mm(da, [w["w_gate"]], name="ffn_gate_dx", tb=True, tk=2816)
    dhf = _mm(db, [w["w_up"]], name="ffn_up_dx", tb=True, extras=(dhf,), epilogue=add, tk=2816)
    dx2, g["g_ffn"] = _rms_bwd_call(x2, w["g_ffn"], dhf, dx3, "rms_ffn_bwd")

    dxo = _mm(dx2, [w["w_xo"]], name="proj_xo_dx", tb=True, out_dtypes=(BF16,))
    g["w_xo"] = _mm(xo, [dx2], name="proj_xo_dw", ta=True)
    dxq, dxkv = _xattn_bwd(xq, xkv, dxo)
    dhx = _mm(dxq, [w["w_xq"]], name="xq_dx", tb=True)
    g["w_xq"] = _mm(hx, [dxq], name="xq_dw", ta=True)
    dmn = _mm(dxkv, [w["w_xkv"]], name="xkv_dx", tb=True)
    g["w_xkv"] = _mm(mn, [dxkv], name="xkv_dw", ta=True)
    dx1, g["g_x"] = _rms_bwd_call(x1, w["g_x"], dhx, dx2, "rms_x_bwd")
    _, g["g_mem"] = _rms_bwd_call(mem, w["g_mem"], dmn, None, "rms_mem_bwd")

    dmerged = _mm(dx1, [w["w_o"]], name="proj_o_dx", tb=True)
    g["w_o"] = _mm(merged, [dx1], name="proj_o_dw", ta=True)
    dpa, dpb, dga, dgb, g["b_gate"] = _gate_bwd(proj, pa, pb, w["b_gate"], dmerged)
    do_a = _mm(dpa, [wa], name="proj_a_dx", tb=True, out_dtypes=(BF16,))
    do_b = _mm(dpb, [wb], name="proj_b_dx", tb=True, out_dtypes=(BF16,))
    g["w_a_proj"] = _unpad_heads(_mm(o_a, [dpa], name="proj_a_dw", ta=True), MLA_HEADS, MLA_V, 0)
    g["w_b_proj"] = _unpad_heads(_mm(o_b, [dpb], name="proj_b_dw", ta=True), SB_HEADS, SB_HEAD_DIM, 0)

    stacked = [reduce[1](n, g[n]) for n in reduce[0]] if reduce else []
    dsq, dsk, dsv, recv = _sb_bwd(proj, o_b, do_b, t_sb, stacked)
    parts = [reduce[2](n, gs, rv) for n, gs, rv in zip(reduce[0], stacked, recv)] if reduce else []
    dqa, dka, dva, got = _mla_bwd(qa, ka, va, o_a, do_a, lse, t_mla, parts)
    riding = dict(zip(reduce[0], zip(stacked, recv, got))) if reduce else {}
    dqp, dkvp, dkr = _mla_rope_bwd(dqa, dka, dva, tab)
    g["w_uq"] = _unpad_heads(_mm(cqn, [dqp], name="q_up_dw", ta=True), MLA_HEADS, MLA_NOPE + MLA_ROPE, 1)
    g["w_ukv"] = _w_ukv_unlayout(_mm(ckvn, [dkvp], name="kv_up_dw", ta=True))
    dcqn = _mm(dqp, [wuq], name="q_up_dx", tb=True)
    dckvn = _mm(dkvp, [wkv], name="kv_up_dx", tb=True)
    dcq, dckv, g["g_q_lat"], g["g_kv_lat"] = _mla_prep_bwd(proj, w["g_q_lat"], w["g_kv_lat"], dcqn, dckvn)

    dproj = jnp.concatenate([dga, dgb, dsq.astype(BF16), dsk.astype(BF16), dsv.astype(BF16), dcq, dckv, dkr], axis=1)
    g["w_in"] = _w_in_unlayout(_mm(h, [dproj], name="proj_in_dw", ta=True, tn=1408))
    dh = _mm(dproj, [win], name="proj_in_dx", tb=True, tk=2816)
    grad_x, g["g_mix"] = _rms_bwd_call(x, w["g_mix"], dh, dx1, "rms_mix_bwd")
    return sq, grad_x, g, riding


def _small_pack(d):
    row5 = jnp.concatenate([d["g_q_lat"].reshape(-1), d["g_kv_lat"].reshape(-1), jnp.zeros((640,), F32)])
    rows = [d[n].reshape(-1) for n in ("g_mix", "g_x", "g_mem", "g_ffn", "g_final")] + [row5]
    return rows


def _small_unpack(p, like):
    out = {n: p[i].reshape(like[n].shape) for i, n in enumerate(("g_mix", "g_x", "g_mem", "g_ffn", "g_final"))}
    out["g_q_lat"] = p[5, 0:256].reshape(like["g_q_lat"].shape)
    out["g_kv_lat"] = p[5, 256:384].reshape(like["g_kv_lat"].shape)
    return out


def kernel(x, mem, positions, g_mix, w_in, b_gate, g_q_lat, w_uq, g_kv_lat, w_ukv, w_a_proj, w_b_proj, w_o, g_x, g_mem, w_xq, w_xkv, w_xo, g_ffn, w_gate, w_up, w_down, g_final, loss_target, m_g_mix, m_w_in, m_b_gate, m_g_q_lat, m_w_uq, m_g_kv_lat, m_w_ukv, m_w_a_proj, m_w_b_proj, m_w_o, m_g_x, m_g_mem, m_w_xq, m_w_xkv, m_w_xo, m_g_ffn, m_w_gate, m_w_up, m_w_down, m_g_final, v_g_mix, v_w_in, v_b_gate, v_g_q_lat, v_w_uq, v_g_kv_lat, v_w_ukv, v_w_a_proj, v_w_b_proj, v_w_o, v_g_x, v_g_mem, v_w_xq, v_w_xkv, v_w_xo, v_g_ffn, v_w_gate, v_w_up, v_w_down, v_g_final):
    given = dict(locals())
    names = [n for n, _, _ in MATS] + ["b_gate"] + list(SMALL)
    wts = {n: given[n] for n in names}
    mom = {n: given["m_" + n] for n in names}
    var = {n: given["v_" + n] for n in names}
    shard2d = {n: shp for n, shp, _ in MATS}
    shard2d["b_gate"] = B_GATE_SHARD
    cx, cy, cc = lax.axis_index("x"), lax.axis_index("y"), lax.axis_index("c")
    me = 2 * cx + cy
    place = jnp.stack([me, cc]).astype(jnp.int32)
    bcol = me * B_GATE_SHARD[1]

    own = [wts[n].reshape(shard2d[n]).astype(BF16) for n, _, _ in MATS]

    def assemble(mats, gathered, mine):
        out = {}
        for (n, shp, ax), g4, shard in zip(mats, gathered, mine):
            g4 = lax.dynamic_update_slice(g4, shard[None], (me, 0, 0))
            out[n] = g4.reshape(N_CHIPS * shp[0], shp[1]) if ax == 0 else jnp.concatenate(list(g4), axis=1)
        return out

    full = assemble(MATS[:N_EARLY], _all_gather_weights(own[:N_EARLY]), own[:N_EARLY])
    late = (own[N_EARLY:], lambda gathered: assemble(MATS[N_EARLY:], gathered, own[N_EARLY:]))
    bias_rows = jnp.pad(wts["b_gate"].reshape(B_GATE_SHARD), ((0, SMALL_ROWS - 2), (0, 0)))
    bias_rows = lax.dynamic_update_slice(jnp.zeros((SMALL_ROWS, D_MODEL), F32), bias_rows, (0, bcol))
    full["b_gate"] = _all_reduce_small(jnp.where(cc == 0, bias_rows, 0.0), "all_gather_bias")[0:2]
    for n in SMALL:
        full[n] = wts[n].reshape(1, -1) if n != "g_final" else wts[n]

    axis_of = {n: ax for n, _, ax in MATS}
    stack = lambda n, g: jnp.stack([_shard_of(g, d, axis_of[n]) for d in range(N_CHIPS)])
    pair_sum = lambda n, gs, rv: _pair_sum(gs, rv, place, "pair_sum_" + n)
    behind = [n for n, _, _ in MATS[N_EARLY:]]
    sq, grad_x, grads, riding = _local_step(x[0], mem[0], positions[0], loss_target[0], full, t_mla=1024, t_sb=256,
                                            late=late, reduce=(behind, stack, pair_sum))

    last = [n for n, _, _ in MATS[:N_EARLY]]
    stacked = [stack(n, grads[n]) for n in last]
    recv = _pair_exchange_grads(stacked, "pair_exchange_grads")
    got = _chip_scatter([pair_sum(n, gs, rv) for n, gs, rv in zip(last, stacked, recv)], "chip_scatter")
    riding.update(zip(last, zip(stacked, recv, got)))
    halves = [_chip_sum(*riding[n], place, "chip_sum_" + n) for n, _, _ in MATS]
    g_shard = dict(zip([n for n, _, _ in MATS], _pair_exchange_halves(halves)))

    small_rows = _small_pack({n: grads[n] for n in SMALL}) + [sq.reshape(-1), grads["b_gate"][0], grads["b_gate"][1]]
    small_rows += [jnp.zeros((D_MODEL,), F32)] * (SMALL_ROWS - len(small_rows))
    small = _all_reduce_small(jnp.stack(small_rows), "all_reduce_small")
    loss = (0.5 / D_MODEL) * jnp.sum(small[6])
    g_shard["b_gate"] = lax.dynamic_slice(small[7:9], (0, bcol), B_GATE_SHARD)

    out = {"grad": {}, "delta": {}, "m": {}, "v": {}}
    for n in [n for n, _, _ in MATS] + ["b_gate"]:
        shape = wts[n].shape
        r2 = lambda a: a.reshape(shard2d[n])
        d_n, m_n, v_n = _adamw(r2(wts[n]), g_shard[n], r2(mom[n]), r2(var[n]), "adamw_" + n)
        for key, a in (("grad", g_shard[n]), ("delta", d_n), ("m", m_n), ("v", v_n)):
            out[key][n] = a.reshape(shape)
    sp = lambda d: jnp.stack(_small_pack(d) + [jnp.zeros((D_MODEL,), F32)] * 2)
    delta_s, m_s, v_s = _adamw(sp(wts), small[0:8].at[6:8].set(0.0), sp(mom), sp(var), "adamw_small")
    for key, p in (("grad", small), ("delta", delta_s), ("m", m_s), ("v", v_s)):
        out[key].update(_small_unpack(p, wts))

    order = ["g_mix", "w_in", "b_gate", "g_q_lat", "w_uq", "g_kv_lat", "w_ukv", "w_a_proj", "w_b_proj", "w_o", "g_x",
             "g_mem", "w_xq", "w_xkv", "w_xo", "g_ffn", "w_gate", "w_up", "w_down", "g_final"]
    return (loss, grad_x[None], *[out[key][n] for key in ("grad", "delta", "m", "v") for n in order])
```

```python
import functools
import math

import jax
import jax.numpy as jnp
from jax import lax
from jax.experimental import pallas as pl
from jax.experimental.pallas import tpu as pltpu

F32 = jnp.float32
BF16 = jnp.bfloat16
MESH = pl.DeviceIdType.MESH

D_MODEL = 1024
MLA_HEADS = 8
MLA_Q_RANK = 256
MLA_KV_RANK = 128
MLA_NOPE = 64
MLA_ROPE = 32
MLA_V = 64
ROPE_THETA = 10000.0
SB_HEADS = 8
SB_HEAD_DIM = 64
X_HEADS = 4
X_HEAD_DIM = 128
D_FF = 2816
EPS = 1e-6
ADAM_LR = 0.001
ADAM_B1 = 0.9
ADAM_B2 = 0.999
ADAM_EPS = 1e-08
ADAM_WD = 0.01
ADAM_STEP = 10

LANE = 128
LOG2E = 1.4426950408889634
N_CHIPS = 4
VMEM_BYTES = 64 * 1024 * 1024

C_GA, C_GB, C_SBQ, C_SBK, C_SBV, C_CQ, C_CKV, C_KR = 0, 1024, 2048, 3072, 4096, 5120, 5376, 5504
D_INP = 5632
ROPE_LO = MLA_NOPE
HALF = MLA_ROPE // 2

SB_ZERO_LOG = -104.0

MATS = (
    ("w_in", (1024, 1000), 1), ("w_uq", (256, 192), 1), ("w_ukv", (128, 256), 1), ("w_a_proj", (512, 256), 1),
    ("w_b_proj", (512, 256), 1), ("w_o", (256, 1024), 0), ("w_xq", (256, 512), 0), ("w_xkv", (256, 1024), 0),
    ("w_xo", (512, 256), 1), ("w_gate", (1024, 704), 1), ("w_up", (1024, 704), 1), ("w_down", (704, 1024), 0),
)
N_EARLY = 3
B_GATE_SHARD = (2, 256)
SMALL = ("g_mix", "g_x", "g_mem", "g_ffn", "g_final", "g_q_lat", "g_kv_lat")
SMALL_ROWS = 16


def _vmem_limit(block_bytes, temp_bytes):
    est = 2 * block_bytes + temp_bytes + (4 << 20)
    return int(min(max(est, 16 << 20), VMEM_BYTES - (6 << 20)))


def _nbytes(shape, dtype):
    return math.prod(shape) * jnp.dtype(dtype).itemsize


def _row_tile(rows, cap):
    if rows <= cap:
        return rows
    return max(t for t in range(8, cap + 1, 8) if rows % t == 0)


def _tile(n, cap):
    if n <= cap:
        return n
    best = None
    for t in range(LANE, cap + 1, LANE):
        if n % t == 0:
            best = t
    assert best is not None, (n, cap)
    return best


def _mm(a, bs, *, name, ta=False, tb=False, extras=(), row_extras=(), consts=(), epilogue=None, out_dtypes=(F32,),
        tm=1024, tn=1024, tk=1024):
    bs = tuple(bs)
    m, k = (a.shape[1], a.shape[0]) if ta else a.shape
    n = bs[0].shape[0] if tb else bs[0].shape[1]
    tm, tn, tk = _tile(m, tm), _tile(n, tn), _tile(k, tk)
    assert m % tm == 0 and n % tn == 0 and k % tk == 0
    nk = k // tk
    nb, ne, no = len(bs), len(extras) + len(row_extras) + len(consts), len(out_dtypes)
    dims = (((0,) if ta else (1,)), ((1,) if tb else (0,))), ((), ())
    if epilogue is None:
        epilogue = lambda accs, ex: (accs[0],)

    def body(*refs):
        a_ref, b_refs, e_refs = refs[0], refs[1:1 + nb], refs[1 + nb:1 + nb + ne]
        o_refs, acc_refs = refs[1 + nb + ne:1 + nb + ne + no], refs[1 + nb + ne + no:]
        if nk == 1:
            av = a_ref[...].astype(BF16)
            accs = [lax.dot_general(av, b_ref[...].astype(BF16), dims, preferred_element_type=F32) for b_ref in b_refs]
            for o_ref, v in zip(o_refs, epilogue(accs, [e[...] for e in e_refs])):
                o_ref[...] = v.astype(o_ref.dtype)
            return
        kk = pl.program_id(2)

        @pl.when(kk == 0)
        def _():
            for acc in acc_refs:
                acc[...] = jnp.zeros_like(acc)

        av = a_ref[...].astype(BF16)
        for b_ref, acc in zip(b_refs, acc_refs):
            acc[...] += lax.dot_general(av, b_ref[...].astype(BF16), dims, preferred_element_type=F32)

        @pl.when(kk == nk - 1)
        def _():
            outs = epilogue([acc[...] for acc in acc_refs], [e[...] for e in e_refs])
            for o_ref, v in zip(o_refs, outs):
                o_ref[...] = v.astype(o_ref.dtype)

    a_spec = pl.BlockSpec((tk, tm), lambda i, j, kk: (kk, i)) if ta else pl.BlockSpec((tm, tk), lambda i, j, kk: (i, kk))
    b_spec = pl.BlockSpec((tn, tk), lambda i, j, kk: (j, kk)) if tb else pl.BlockSpec((tk, tn), lambda i, j, kk: (kk, j))
    mn_spec = pl.BlockSpec((tm, tn), lambda i, j, kk: (i, j))
    blocks = (_nbytes((tm, tk), a.dtype) + sum(_nbytes((tk, tn), b.dtype) for b in bs)
              + sum(_nbytes((tm, tn), e.dtype) for e in extras) + sum(_nbytes((tm, tn), d) for d in out_dtypes)
              + sum(_nbytes((tm, e.shape[1]), e.dtype) for e in row_extras))
    temps = (nb + 4) * _nbytes((tm, tn), F32)
    outs = pl.pallas_call(
        body, name=name, grid=(m // tm, n // tn, nk),
        in_specs=[a_spec] + [b_spec] * nb + [mn_spec] * len(extras)
        + [pl.BlockSpec((tm, e.shape[1]), lambda i, j, kk: (i, 0)) for e in row_extras]
        + [pl.BlockSpec(e.shape, lambda i, j, kk: (0, 0)) for e in consts],
        out_specs=[mn_spec] * no,
        out_shape=[jax.ShapeDtypeStruct((m, n), d) for d in out_dtypes],
        scratch_shapes=[pltpu.VMEM((tm, tn), F32) for _ in range(nb if nk > 1 else 0)],
        compiler_params=pltpu.CompilerParams(
            dimension_semantics=("parallel", "parallel", "arbitrary"),
            vmem_limit_bytes=_vmem_limit(blocks, temps)),
    )(a, *bs, *extras, *row_extras, *consts)
    return outs[0] if no == 1 else outs


def _rowwise(body, *, name, rows, tr, row_ins, full_ins=(), row_outs=(), acc_outs=()):
    tr = min(tr, rows)
    assert rows % tr == 0
    n_ri, n_fi, n_ro = len(row_ins), len(full_ins), len(row_outs)

    def kern(*refs):
        body(pl.program_id(0), refs[:n_ri], refs[n_ri:n_ri + n_fi], refs[n_ri + n_fi:n_ri + n_fi + n_ro],
             refs[n_ri + n_fi + n_ro:])

    in_specs = [pl.BlockSpec((tr, w), functools.partial(lambda i, c: (i, c), c=ci)) for _, w, ci in row_ins]
    in_specs += [pl.BlockSpec(f.shape, lambda i: (0, 0)) for f in full_ins]
    out_specs = [pl.BlockSpec((tr, w), lambda i: (i, 0)) for w, _ in row_outs]
    out_specs += [pl.BlockSpec(s, lambda i: (0, 0)) for s, _ in acc_outs]
    out_shape = [jax.ShapeDtypeStruct((rows, w), d) for w, d in row_outs]
    out_shape += [jax.ShapeDtypeStruct(s, d) for s, d in acc_outs]
    blocks = (sum(_nbytes((tr, w), a.dtype) for a, w, _ in row_ins) + sum(_nbytes(f.shape, f.dtype) for f in full_ins)
              + sum(_nbytes((tr, w), d) for w, d in row_outs) + sum(_nbytes(s, d) for s, d in acc_outs))
    widest = max([w for _, w, _ in row_ins] + [w for w, _ in row_outs])
    outs = pl.pallas_call(
        kern, name=name, grid=(rows // tr,), in_specs=in_specs, out_specs=out_specs, out_shape=out_shape,
        compiler_params=pltpu.CompilerParams(
            dimension_semantics=("arbitrary",) if acc_outs else ("parallel",),
            vmem_limit_bytes=_vmem_limit(blocks, 8 * _nbytes((tr, widest), F32))),
    )(*[a for a, _, _ in row_ins], *full_ins)
    return outs


def _rms(x, g):
    r = lax.rsqrt(jnp.mean(x * x, axis=-1, keepdims=True) + EPS)
    return x * r * g


def _rms_bwd(x, g, dy):
    r = lax.rsqrt(jnp.mean(x * x, axis=-1, keepdims=True) + EPS)
    xh = x * r
    dxh = dy * g
    dx = r * (dxh - xh * jnp.mean(dxh * xh, axis=-1, keepdims=True))
    return dx, jnp.sum(dy * xh, axis=0, keepdims=True)


def _sigmoid(x):
    return 1.0 / (1.0 + jnp.exp(-x))


def _acc_init(i, refs):
    @pl.when(i == 0)
    def _():
        for r in refs:
            r[...] = jnp.zeros_like(r)


def _rms_fwd_call(x, g, name):
    rows, c = x.shape

    def body(i, ins, fulls, outs, accs):
        outs[0][...] = _rms(ins[0][...], fulls[0][...]).astype(BF16)

    return _rowwise(body, name=name, rows=rows, tr=512, row_ins=[(x, c, 0)], full_ins=[g], row_outs=[(c, BF16)])[0]


def _rms_bwd_call(x, g, dy, res, name):
    rows, c = x.shape
    row_ins = [(x, c, 0), (dy, c, 0)] + ([(res, c, 0)] if res is not None else [])

    def body(i, ins, fulls, outs, accs):
        _acc_init(i, accs)
        dx, dg = _rms_bwd(ins[0][...], fulls[0][...], ins[1][...].astype(F32))
        if res is not None:
            dx = dx + ins[2][...]
        outs[0][...] = dx
        accs[0][...] += dg

    return _rowwise(body, name=name, rows=rows, tr=512, row_ins=row_ins, full_ins=[g], row_outs=[(c, F32)],
                    acc_outs=[((1, c), F32)])


def _rope_tables(pos_col, freq_lane):
    rows = pos_col.shape[0]

    def body(i, ins, fulls, outs, accs):
        ang = ins[0][...].astype(F32) * fulls[0][...]
        lane = lax.broadcasted_iota(jnp.int32, ang.shape, 1)
        cos, sin = jnp.cos(ang), jnp.sin(ang)
        first = (lane >= ROPE_LO) & (lane < ROPE_LO + HALF)
        second = (lane >= ROPE_LO + HALF) & (lane < ROPE_LO + MLA_ROPE)
        outs[0][:, 0:LANE] = jnp.where(first | second, cos, 1.0)
        outs[0][:, LANE:2 * LANE] = jnp.where(first, -sin, 0.0)
        outs[0][:, 2 * LANE:3 * LANE] = jnp.where(second, sin, 0.0)

    return _rowwise(body, name="rope_tables", rows=rows, tr=1024, row_ins=[(pos_col, 1, 0)], full_ins=[freq_lane],
                    row_outs=[(3 * LANE, F32)])[0]


def _rope(x, tab):
    return (x * tab[:, 0:LANE] + pltpu.roll(x, LANE - HALF, 1) * tab[:, LANE:2 * LANE]
            + pltpu.roll(x, HALF, 1) * tab[:, 2 * LANE:3 * LANE])


def _rope_t(dy, tab):
    return (dy * tab[:, 0:LANE] + pltpu.roll(dy * tab[:, LANE:2 * LANE], HALF, 1)
            + pltpu.roll(dy * tab[:, 2 * LANE:3 * LANE], LANE - HALF, 1))


def _mla_prep_fwd(proj, tab, g_q, g_kv):
    rows = proj.shape[0]

    def body(i, ins, fulls, outs, accs):
        outs[0][...] = _rms(ins[0][...].astype(F32), fulls[0][...]).astype(BF16)
        outs[1][...] = _rms(ins[1][...].astype(F32), fulls[1][...]).astype(BF16)
        outs[2][...] = _rope(ins[2][...].astype(F32), ins[3][...])

    return _rowwise(body, name="mla_prep_fwd", rows=rows, tr=512,
                    row_ins=[(proj, MLA_Q_RANK, C_CQ // MLA_Q_RANK), (proj, LANE, C_CKV // LANE),
                             (proj, LANE, C_KR // LANE), (tab, 3 * LANE, 0)],
                    full_ins=[g_q, g_kv], row_outs=[(MLA_Q_RANK, BF16), (MLA_KV_RANK, BF16), (LANE, F32)])


def _mla_prep_bwd(proj, g_q, g_kv, dcqn, dckvn):
    rows = proj.shape[0]

    def body(i, ins, fulls, outs, accs):
        _acc_init(i, accs)
        dcq, dgq = _rms_bwd(ins[0][...].astype(F32), fulls[0][...], ins[2][...])
        dckv, dgkv = _rms_bwd(ins[1][...].astype(F32), fulls[1][...], ins[3][...])
        outs[0][...] = dcq.astype(BF16)
        outs[1][...] = dckv.astype(BF16)
        accs[0][...] += dgq
        accs[1][...] += dgkv

    return _rowwise(body, name="mla_prep_bwd", rows=rows, tr=512,
                    row_ins=[(proj, MLA_Q_RANK, C_CQ // MLA_Q_RANK), (proj, LANE, C_CKV // LANE),
                             (dcqn, MLA_Q_RANK, 0), (dckvn, MLA_KV_RANK, 0)],
                    full_ins=[g_q, g_kv], row_outs=[(MLA_Q_RANK, BF16), (MLA_KV_RANK, BF16)],
                    acc_outs=[((1, MLA_Q_RANK), F32), ((1, MLA_KV_RANK), F32)])


def _per_head(fn, x):
    return jnp.concatenate([fn(x[:, h * LANE:(h + 1) * LANE]) for h in range(x.shape[1] // LANE)], axis=1)


def _mla_rope_bwd(dq, dk, dv, tab):
    rows = dq.shape[0]
    hw = MLA_HEADS * LANE

    def body(i, ins, fulls, outs, accs):
        t = ins[3][...]
        dkr = jnp.zeros((ins[0].shape[0], LANE), F32)
        for h in range(MLA_HEADS):
            sl = slice(h * LANE, (h + 1) * LANE)
            outs[0][:, sl] = _rope_t(ins[0][:, sl], t).astype(BF16)
            dkr = dkr + ins[1][:, sl]
        outs[1][:, 0:hw] = ins[1][...].astype(BF16)
        outs[1][:, hw:2 * hw] = ins[2][...].astype(BF16)
        lane = lax.broadcasted_iota(jnp.int32, dkr.shape, 1)
        dkr = jnp.where((lane >= ROPE_LO) & (lane < ROPE_LO + MLA_ROPE), dkr, 0.0)
        outs[2][...] = _rope_t(dkr, t).astype(BF16)

    return _rowwise(body, name="mla_rope_bwd", rows=rows, tr=512,
                    row_ins=[(dq, hw, 0), (dk, hw, 0), (dv, hw, 0), (tab, 3 * LANE, 0)],
                    row_outs=[(hw, BF16), (2 * hw, BF16), (LANE, BF16)])


def _dot_nt(a, b):
    return lax.dot_general(a, b, (((1,), (1,)), ((), ())), preferred_element_type=F32)


def _dot_tn(a, b):
    return lax.dot_general(a, b, (((0,), (0,)), ((), ())), preferred_element_type=F32)


def _dot(a, b):
    return jnp.dot(a, b, preferred_element_type=F32)


def _attn_params(s, t, n_res_f32, n_res_bf16, ride=False):
    blocks = n_res_f32 * _nbytes((s, LANE), F32) + n_res_bf16 * _nbytes((s, LANE), BF16) + 6 * _nbytes((t, LANE), F32)
    return pltpu.CompilerParams(dimension_semantics=("arbitrary" if ride else "parallel", "arbitrary"),
                                vmem_limit_bytes=_vmem_limit(blocks, 12 * _nbytes((t, t), F32)))


def _mla_fwd(q, k, v, t, tk, shards=()):
    s, hw = q.shape
    heads, nq, r = hw // LANE, s // t, t // tk
    ng = len(shards)
    scale = 1.0 / math.sqrt(MLA_NOPE + MLA_ROPE)
    scale2 = scale * LOG2E

    def body(q_ref, k_ref, v_ref, *rest):
        o_ref, l_ref = rest[ng], rest[ng + 1]
        h, i = pl.program_id(0), pl.program_id(1)
        if ng:
            gather = _GatherPhases(rest[:ng], rest[ng + 2:2 * ng + 2], *rest[2 * ng + 2:])
            pl.when((h == 0) & (i == 0))(gather.send)
            pl.when((h == heads // 2) & (i == 0))(gather.forward)
        qv = q_ref[...]

        def step(j, carry, off):
            m, l, acc = carry
            sl = pl.ds(pl.multiple_of(j * tk, tk), tk)
            sc = _dot_nt(qv, k_ref[sl, :])
            if off is not None:
                row = lax.broadcasted_iota(jnp.int32, (t, tk), 0)
                col = lax.broadcasted_iota(jnp.int32, (t, tk), 1)
                sc = jnp.where(col + off <= row, sc, -1e30)
            m_new = jnp.maximum(m, jnp.max(sc, axis=1, keepdims=True))
            p = jnp.exp2((sc - m_new) * scale2)
            alpha = jnp.exp2((m - m_new) * scale2)
            l = alpha * l + jnp.sum(p, axis=1, keepdims=True)
            acc = alpha * acc + _dot(p.astype(BF16), v_ref[sl, :])
            return m_new, l, acc

        init = (jnp.full((t, 1), -1e30, F32), jnp.zeros((t, 1), F32), jnp.zeros((t, LANE), F32))
        carry = lax.fori_loop(0, i * r, lambda j, c: step(j, c, None), init)
        for jj in range(r):
            carry = step(i * r + jj, carry, jj * tk)
        m, l, acc = carry
        o_ref[...] = (acc / l).astype(o_ref.dtype)
        l_ref[0] = m * scale2 + jnp.log2(l)
        if ng:
            pl.when((h == heads - 1) & (i == nq - 1))(gather.finish)

    blk = pl.BlockSpec((t, LANE), lambda h, i: (i, h))
    res = pl.BlockSpec((s, LANE), lambda h, i: (0, h))
    outs = pl.pallas_call(
        body, name="mla_fwd", grid=(heads, nq), in_specs=[blk, res, res] + [ANY] * ng,
        out_specs=[blk, pl.BlockSpec((1, t, 1), lambda h, i: (h, i, 0))] + [ANY] * ng,
        out_shape=[jax.ShapeDtypeStruct((s, hw), BF16), jax.ShapeDtypeStruct((heads, s, 1), F32)]
        + [jax.ShapeDtypeStruct((N_CHIPS,) + sh.shape, sh.dtype) for sh in shards],
        scratch_shapes=_dma_sems(6 * ng) if ng else [],
        compiler_params=_attn_params(s, t, 0, 2, ride=ng > 0),
    )(q, k, v, *shards)
    return outs[0], outs[1], list(outs[2:])


def _mla_bwd(q, k, v, o, do, lse, t, tk, parts=()):
    s, hw = q.shape
    heads, nq, r = hw // LANE, s // t, t // tk
    ns = len(parts)
    scale = 1.0 / math.sqrt(MLA_NOPE + MLA_ROPE)
    scale2 = scale * LOG2E

    def body(q_ref, k_ref, v_ref, o_ref, do_ref, l_ref, *rest):
        dq_ref, dk_ref, dv_ref = rest[ns:ns + 3]
        h, i = pl.program_id(0), pl.program_id(1)
        finish = _ride_along(_scatter_copies, ns, (rest[:ns], rest[ns + 3:2 * ns + 3], *rest[2 * ns + 3:]),
                             (h == 0) & (i == 0), (h == heads - 1) & (i == nq - 1))

        @pl.when(i == 0)
        def _():
            dk_ref[...] = jnp.zeros_like(dk_ref)
            dv_ref[...] = jnp.zeros_like(dv_ref)

        qv, dov, lv = q_ref[...], do_ref[...], l_ref[0]
        dlt = jnp.sum(dov.astype(F32) * o_ref[...].astype(F32), axis=1, keepdims=True)

        def step(j, dq, off):
            sl = pl.ds(pl.multiple_of(j * tk, tk), tk)
            kv, vv = k_ref[sl, :], v_ref[sl, :]
            p = jnp.exp2(_dot_nt(qv, kv) * scale2 - lv)
            if off is not None:
                row = lax.broadcasted_iota(jnp.int32, (t, tk), 0)
                col = lax.broadcasted_iota(jnp.int32, (t, tk), 1)
                p = jnp.where(col + off <= row, p, 0.0)
            ds = (p * (_dot_nt(dov, vv) - dlt)).astype(BF16)
            dk_ref[sl, :] += _dot_tn(ds, qv) * scale
            dv_ref[sl, :] += _dot_tn(p.astype(BF16), dov)
            return dq + _dot(ds, kv)

        dq = lax.fori_loop(0, i * r, lambda j, c: step(j, c, None), jnp.zeros((t, LANE), F32))
        for jj in range(r):
            dq = step(i * r + jj, dq, jj * tk)
        dq_ref[...] = dq * scale
        finish()

    blk = pl.BlockSpec((t, LANE), lambda h, i: (i, h))
    res = pl.BlockSpec((s, LANE), lambda h, i: (0, h))
    full = jax.ShapeDtypeStruct((s, hw), F32)
    outs = pl.pallas_call(
        body, name="mla_bwd", grid=(heads, nq),
        in_specs=[blk, res, res, blk, blk, pl.BlockSpec((1, t, 1), lambda h, i: (h, i, 0))] + [ANY] * ns,
        out_specs=[blk, res, res] + [ANY] * ns, out_shape=[full, full, full] + _scatter_shapes(parts),
        scratch_shapes=_dma_sems(3 * ns) if ns else [],
        compiler_params=_attn_params(s, t, 2, 2, ride=ns > 0),
    )(q, k, v, o, do, lse, *parts)
    return outs[0], outs[1], outs[2], list(outs[3:])


def _sb_logits(qv, kv, scale, masked, t, upper):
    z = _dot_nt(qv, kv) * scale
    e = jnp.exp(-jnp.abs(z))
    l1p = jnp.log(1.0 + e)
    lb = jnp.minimum(z, 0.0) - l1p
    lo = -jnp.maximum(z, 0.0) - l1p
    keep = None
    if masked:
        row = lax.broadcasted_iota(jnp.int32, (t, t), 0)
        col = lax.broadcasted_iota(jnp.int32, (t, t), 1)
        keep = col < row
        lo = jnp.where(keep, lo, 0.0)
    hi = lo.astype(BF16)
    rem = (lo - hi.astype(F32)).astype(BF16)
    suf = _dot(hi, upper) + _dot(rem, upper)
    return z, e, lb, lo, suf, keep


def _tri(t, inclusive):
    row = lax.broadcasted_iota(jnp.int32, (t, t), 0)
    col = lax.broadcasted_iota(jnp.int32, (t, t), 1)
    return jnp.where((row >= col) if inclusive else (row > col), 1.0, 0.0).astype(BF16)


SB_CHAINS = 4


def _sb_walk(i, first, carries_of):
    n = SB_CHAINS
    carries = [first(c) for c in range(n)]
    width = len(carries[0])

    def alive(carry):
        return jnp.max(carry[0]) >= SB_ZERO_LOG

    def split(st):
        return [tuple(st[1 + c * width:1 + (c + 1) * width]) for c in range(n)]

    def live(st):
        any_alive = alive(split(st)[0])
        for cr in split(st)[1:]:
            any_alive = any_alive | alive(cr)
        return (st[0] <= n * i) & any_alive

    def more(st):
        out = (st[0] + 1,)
        for c, cr in enumerate(split(st)):
            out += tuple(carries_of(c, st[0], cr))
        return out

    st = lax.while_loop(live, more, (jnp.int32(1),) + tuple(x for cr in carries for x in cr))
    jj, carries = st[0], split(st)
    for c in range(1, n):
        def live_c(s2, c=c):
            return (s2[0] <= n * i + c) & alive(s2[1:])

        def more_c(s2, c=c):
            return (s2[0] + 1,) + tuple(carries_of(c, s2[0], s2[1:]))

        carries[c] = lax.while_loop(live_c, more_c, (jj,) + tuple(carries[c]))[1:]
    return carries


def _sb_fwd(proj, t):
    s = proj.shape[0]
    heads, nq, n = SB_HEADS, s // t, SB_CHAINS
    scale = 1.0 / math.sqrt(SB_HEAD_DIM)

    def body(q_ref, k_ref, v_ref, o_ref):
        i = pl.program_id(1)
        upper = _tri(t, False)
        qs = [q_ref[c * t:(c + 1) * t, :].astype(BF16) for c in range(n)]

        def step(c, jj, carry, masked):
            run, acc = carry
            sl = pl.ds(pl.multiple_of((n * i + c - jj) * t, t), t)
            _, _, lb, lo, suf, keep = _sb_logits(qs[c], k_ref[sl, :].astype(BF16), scale, masked, t, upper)
            a = jnp.exp(lb + suf + run)
            if masked:
                a = jnp.where(keep, a, 0.0)
            acc = acc + _dot(a.astype(BF16), v_ref[sl, :].astype(BF16))
            return run + jnp.sum(lo, axis=1, keepdims=True), acc

        init = (jnp.zeros((t, 1), F32), jnp.zeros((t, LANE), F32))
        carries = _sb_walk(i, lambda c: step(c, 0, init, True), lambda c, jj, cr: step(c, jj, cr, False))
        for c in range(n):
            o_ref[c * t:(c + 1) * t, :] = carries[c][1]

    return pl.pallas_call(
        body, name="sb_fwd", grid=(heads, nq // n),
        in_specs=[pl.BlockSpec((n * t, LANE), lambda h, i: (i, C_SBQ // LANE + h)),
                  pl.BlockSpec((s, LANE), lambda h, i: (0, C_SBK // LANE + h)),
                  pl.BlockSpec((s, LANE), lambda h, i: (0, C_SBV // LANE + h))],
        out_specs=pl.BlockSpec((n * t, LANE), lambda h, i: (i, h)),
        out_shape=jax.ShapeDtypeStruct((s, heads * LANE), F32),
        compiler_params=_attn_params(s, n * t, 0, 2),
    )(proj, proj, proj)


def _sb_bwd(proj, o, do, t, stacked=()):
    s = proj.shape[0]
    heads, nq, n = SB_HEADS, s // t, SB_CHAINS
    nx = len(stacked)
    scale = 1.0 / math.sqrt(SB_HEAD_DIM)

    def body(q_ref, k_ref, v_ref, o_ref, do_ref, *rest):
        dq_ref, dk_ref, dv_ref = rest[nx:nx + 3]
        hd, i = pl.program_id(0), pl.program_id(1)
        finish = _ride_along(_exchange_copies, nx, (rest[:nx], rest[nx + 3:2 * nx + 3], *rest[2 * nx + 3:]),
                             (hd == 0) & (i == 0), (hd == heads - 1) & (i == nq // n - 1))

        @pl.when(i == 0)
        def _():
            dk_ref[...] = jnp.zeros_like(dk_ref)
            dv_ref[...] = jnp.zeros_like(dv_ref)

        rows = [slice(c * t, (c + 1) * t) for c in range(n)]
        qs = [q_ref[r, :].astype(BF16) for r in rows]
        dos = [do_ref[r, :] for r in rows]
        totals = [jnp.sum(dos[c].astype(F32) * o_ref[rows[c], :], axis=1, keepdims=True) for c in range(n)]
        upper, upper_incl = _tri(t, False), _tri(t, True)

        def step(c, jj, carry, masked):
            run, g, dq = carry
            qv, dov = qs[c], dos[c]
            sl = pl.ds(pl.multiple_of((n * i + c - jj) * t, t), t)
            kv, vv = k_ref[sl, :].astype(BF16), v_ref[sl, :].astype(BF16)
            z, e, lb, lo, suf, keep = _sb_logits(qv, kv, scale, masked, t, upper)
            tail = suf + run
            a = jnp.exp(lb + tail)
            if masked:
                a = jnp.where(keep, a, 0.0)
            ab = a.astype(BF16)
            gr = ab.astype(F32) * _dot_nt(dov, vv)
            ghi = gr.astype(BF16)
            grem = (gr - ghi.astype(F32)).astype(BF16)
            before = totals[c] - g - (_dot(ghi, upper_incl) + _dot(grem, upper_incl))
            before = jnp.where(tail < SB_ZERO_LOG, 0.0, before)
            r = 1.0 / (1.0 + e)
            pos = z >= 0.0
            dz = r * (gr * jnp.where(pos, e, 1.0) - before * jnp.where(pos, 1.0, e))
            if masked:
                dz = jnp.where(keep, dz, 0.0)
            dzb = (dz * scale).astype(BF16)
            dk_ref[sl, :] += _dot_tn(dzb, qv)
            dv_ref[sl, :] += _dot_tn(ab, dov)
            return (run + jnp.sum(lo, axis=1, keepdims=True), g + jnp.sum(gr, axis=1, keepdims=True),
                    dq + _dot(dzb, kv))

        zero = jnp.zeros((t, 1), F32)
        init = (zero, zero, jnp.zeros((t, LANE), F32))
        carries = _sb_walk(i, lambda c: step(c, 0, init, True), lambda c, jj, cr: step(c, jj, cr, False))
        for c in range(n):
            dq_ref[rows[c], :] = carries[c][2]
        finish()

    blk = pl.BlockSpec((n * t, LANE), lambda h, i: (i, h))
    res = pl.BlockSpec((s, LANE), lambda h, i: (0, h))
    full = jax.ShapeDtypeStruct((s, heads * LANE), F32)
    outs = pl.pallas_call(
        body, name="sb_bwd", grid=(heads, nq // n),
        in_specs=[pl.BlockSpec((n * t, LANE), lambda h, i: (i, C_SBQ // LANE + h)),
                  pl.BlockSpec((s, LANE), lambda h, i: (0, C_SBK // LANE + h)),
                  pl.BlockSpec((s, LANE), lambda h, i: (0, C_SBV // LANE + h)), blk, blk] + [ANY] * nx,
        out_specs=[blk, res, res] + [ANY] * nx, out_shape=[full, full, full] + _exchange_shapes(stacked),
        scratch_shapes=_dma_sems(nx) if nx else [],
        compiler_params=_attn_params(s, n * t, 2, 2, ride=nx > 0),
    )(proj, proj, proj, o, do, *stacked)
    return outs[0], outs[1], outs[2], list(outs[3:])


def _xattn_probs(qh, kh):
    sc = _dot_nt(qh, kh) * (1.0 / math.sqrt(X_HEAD_DIM))
    p = jnp.exp(sc - jnp.max(sc, axis=1, keepdims=True))
    return p / jnp.sum(p, axis=1, keepdims=True)


def _xattn_fwd(xq, xkv):
    rows = xq.shape[0]
    w = X_HEADS * X_HEAD_DIM

    def body(i, ins, fulls, outs, accs):
        for h in range(X_HEADS):
            sl = slice(h * LANE, (h + 1) * LANE)
            p = _xattn_probs(ins[0][:, sl], fulls[0][:, sl])
            outs[0][:, sl] = _dot(p.astype(BF16), fulls[0][:, w + h * LANE:w + (h + 1) * LANE]).astype(BF16)

    return _rowwise(body, name="xattn_fwd", rows=rows, tr=512, row_ins=[(xq, w, 0)], full_ins=[xkv],
                    row_outs=[(w, BF16)])[0]


def _xattn_bwd(xq, xkv, dxo):
    rows = xq.shape[0]
    w = X_HEADS * X_HEAD_DIM

    def body(i, ins, fulls, outs, accs):
        _acc_init(i, accs)
        for h in range(X_HEADS):
            sl = slice(h * LANE, (h + 1) * LANE)
            slv = slice(w + h * LANE, w + (h + 1) * LANE)
            qh, kh, vh, doh = ins[0][:, sl], fulls[0][:, sl], fulls[0][:, slv], ins[1][:, sl]
            p = _xattn_probs(qh, kh)
            dp = _dot_nt(doh, vh)
            ds = (p * (dp - jnp.sum(p * dp, axis=1, keepdims=True)) * (1.0 / math.sqrt(X_HEAD_DIM))).astype(BF16)
            outs[0][:, sl] = _dot(ds, kh).astype(BF16)
            accs[0][:, sl] += _dot_tn(ds, qh)
            accs[0][:, slv] += _dot_tn(p.astype(BF16), doh)

    return _rowwise(body, name="xattn_bwd", rows=rows, tr=512, row_ins=[(xq, w, 0), (dxo, w, 0)], full_ins=[xkv],
                    row_outs=[(w, BF16)], acc_outs=[(xkv.shape, F32)])


def _gate_fwd(proj, pa, pb, b_gate):
    rows = proj.shape[0]

    def body(i, ins, fulls, outs, accs):
        sa = _sigmoid(ins[0][...].astype(F32) + fulls[0][0:1, :])
        sb = _sigmoid(ins[1][...].astype(F32) + fulls[0][1:2, :])
        outs[0][...] = (sa * ins[2][...] + sb * ins[3][...]).astype(BF16)

    return _rowwise(body, name="gate_fwd", rows=rows, tr=512,
                    row_ins=[(proj, D_MODEL, C_GA // D_MODEL), (proj, D_MODEL, C_GB // D_MODEL), (pa, D_MODEL, 0),
                             (pb, D_MODEL, 0)],
                    full_ins=[b_gate], row_outs=[(D_MODEL, BF16)])[0]


def _gate_bwd(proj, pa, pb, b_gate, dm):
    rows = proj.shape[0]

    def body(i, ins, fulls, outs, accs):
        _acc_init(i, accs)
        d = ins[4][...]
        sa = _sigmoid(ins[0][...].astype(F32) + fulls[0][0:1, :])
        sb = _sigmoid(ins[1][...].astype(F32) + fulls[0][1:2, :])
        dga = d * ins[2][...] * sa * (1.0 - sa)
        dgb = d * ins[3][...] * sb * (1.0 - sb)
        outs[0][...] = (d * sa).astype(BF16)
        outs[1][...] = (d * sb).astype(BF16)
        outs[2][...] = dga.astype(BF16)
        outs[3][...] = dgb.astype(BF16)
        accs[0][0:1, :] += jnp.sum(dga, axis=0, keepdims=True)
        accs[0][1:2, :] += jnp.sum(dgb, axis=0, keepdims=True)

    return _rowwise(body, name="gate_bwd", rows=rows, tr=512,
                    row_ins=[(proj, D_MODEL, C_GA // D_MODEL), (proj, D_MODEL, C_GB // D_MODEL), (pa, D_MODEL, 0),
                             (pb, D_MODEL, 0), (dm, D_MODEL, 0)],
                    full_ins=[b_gate], row_outs=[(D_MODEL, BF16)] * 4, acc_outs=[((2, D_MODEL), F32)])


def _loss_head(x3, target, g_final):
    rows = x3.shape[0]

    def body(i, ins, fulls, outs, accs):
        _acc_init(i, accs)
        xv, g = ins[0][...], fulls[0][...]
        d = _rms(xv, g) - ins[1][...]
        dx, dg = _rms_bwd(xv, g, d * (1.0 / D_MODEL))
        outs[0][...] = dx
        accs[0][...] += dg
        accs[1][...] += jnp.sum(d * d, axis=0, keepdims=True)

    return _rowwise(body, name="loss_head", rows=rows, tr=512, row_ins=[(x3, D_MODEL, 0), (target, D_MODEL, 0)],
                    full_ins=[g_final], row_outs=[(D_MODEL, F32)], acc_outs=[((1, D_MODEL), F32), ((1, D_MODEL), F32)])


def _adamw(w, g, m, v, name):
    rows, c = w.shape

    def body(i, ins, fulls, outs, accs):
        wv, gv = ins[0][...], ins[1][...]
        mn = ADAM_B1 * ins[2][...] + (1.0 - ADAM_B1) * gv
        vn = ADAM_B2 * ins[3][...] + (1.0 - ADAM_B2) * jnp.square(gv)
        m_hat = mn / (1.0 - ADAM_B1 ** ADAM_STEP)
        v_hat = vn / (1.0 - ADAM_B2 ** ADAM_STEP)
        outs[0][...] = -ADAM_LR * (m_hat / (jnp.sqrt(v_hat) + ADAM_EPS) + ADAM_WD * wv)
        outs[1][...] = mn
        outs[2][...] = vn

    return _rowwise(body, name=name, rows=rows, tr=_row_tile(rows, 256), row_ins=[(a, c, 0) for a in (w, g, m, v)],
                    row_outs=[(c, F32)] * 3)


def _place():
    x, y, c = lax.axis_index("x"), lax.axis_index("y"), lax.axis_index("c")
    chips = [(1 - x, y), (x, 1 - y), (1 - x, 1 - y)]
    return x, y, c, chips


ANY = pl.BlockSpec(memory_space=pl.ANY)


def _remote(src, dst, send_sem, recv_sem, to):
    return pltpu.make_async_remote_copy(src_ref=src, dst_ref=dst, send_sem=send_sem, recv_sem=recv_sem,
                                        device_id=to, device_id_type=MESH)


def _dma_sems(n):
    return [pltpu.SemaphoreType.DMA((n,)), pltpu.SemaphoreType.DMA((n,))]


class _GatherPhases:
    def __init__(self, ins, outs, send_sems, recv_sems):
        self.ins, self.outs, self.send_sems, self.recv_sems = ins, outs, send_sems, recv_sems
        self.x, self.y, self.c, self.chips = _place()
        self.me = 2 * self.x + self.y

    def _copy(self, t, j, chip_idx, hlf, to, src=None):
        h = self.ins[t].shape[0] // 2
        dst = self.outs[t].at[chip_idx, pl.ds(hlf * h, h), :]
        return _remote(dst if src is None else src, dst, self.send_sems.at[6 * t + j], self.recv_sems.at[6 * t + j], to)

    def _sends(self):
        out = []
        for t in range(len(self.ins)):
            h = self.ins[t].shape[0] // 2
            for j, chip in enumerate(self.chips):
                out.append(self._copy(t, j, self.me, self.c, (*chip, self.c), src=self.ins[t].at[pl.ds(self.c * h, h), :]))
        return out

    def _forwards(self):
        return [self._copy(t, 3 + j, 2 * chip[0] + chip[1], self.c, (self.x, self.y, 1 - self.c))
                for t in range(len(self.ins)) for j, chip in enumerate(self.chips)]

    def send(self):
        for cp in self._sends():
            cp.start()

    def forward(self):
        here = (self.x, self.y, self.c)
        landed = [self._copy(t, j, 2 * chip[0] + chip[1], self.c, here)
                  for t in range(len(self.ins)) for j, chip in enumerate(self.chips)]
        for arrival, fwd in zip(landed, self._forwards()):
            arrival.wait_recv()
            fwd.start()

    def finish(self):
        here = (self.x, self.y, self.c)
        for t in range(len(self.ins)):
            for j, chip in enumerate(self.chips):
                self._copy(t, 3 + j, 2 * chip[0] + chip[1], 1 - self.c, here).wait_recv()
        for cp in self._sends() + self._forwards():
            cp.wait_send()


def _all_gather_weights(shards):
    n = len(shards)

    def body(*refs):
        gather = _GatherPhases(refs[:n], refs[n:2 * n], *refs[2 * n:])
        gather.send()
        gather.forward()
        gather.finish()

    return pl.pallas_call(
        body, name="all_gather_weights", in_specs=[ANY] * n, out_specs=[ANY] * n,
        out_shape=[jax.ShapeDtypeStruct((N_CHIPS,) + s.shape, s.dtype) for s in shards],
        scratch_shapes=_dma_sems(6 * n),
    )(*shards)


def _exchange_copies(ins, outs, send_sems, recv_sems):
    x, y, c, _ = _place()
    cps = []
    for t in range(len(ins)):
        h = ins[t].shape[1] // 2
        cps.append(_remote(ins[t].at[:, pl.ds((1 - c) * h, h), :], outs[t], send_sems.at[t], recv_sems.at[t],
                           (x, y, 1 - c)))
    return cps


def _exchange_shapes(stacked):
    return [jax.ShapeDtypeStruct((N_CHIPS, s.shape[1] // 2, s.shape[2]), s.dtype) for s in stacked]


def _scatter_copies(ins, outs, send_sems, recv_sems):
    x, y, c, chips = _place()
    return [_remote(ins[t].at[2 * chip[0] + chip[1]], outs[t].at[j], send_sems.at[3 * t + j], recv_sems.at[3 * t + j],
                    (*chip, c)) for t in range(len(ins)) for j, chip in enumerate(chips)]


def _scatter_shapes(parts):
    return [jax.ShapeDtypeStruct((N_CHIPS - 1,) + s.shape[1:], s.dtype) for s in parts]


def _ride_along(copies_of, n, refs, first, last):
    if not n:
        return lambda: None

    def start():
        for cp in copies_of(*refs):
            cp.start()

    def wait():
        for cp in copies_of(*refs):
            cp.wait()

    pl.when(first)(start)
    return lambda: pl.when(last)(wait)


def _copy_call(copies_of, arrs, out_shape, sems_per, name):
    n = len(arrs)

    def body(*refs):
        cps = copies_of(refs[:n], refs[n:2 * n], *refs[2 * n:])
        for cp in cps:
            cp.start()
        for cp in cps:
            cp.wait()

    return pl.pallas_call(body, name=name, in_specs=[ANY] * n, out_specs=[ANY] * n, out_shape=out_shape,
                          scratch_shapes=_dma_sems(sems_per * n))(*arrs)


def _pair_exchange_grads(stacked, name):
    return _copy_call(_exchange_copies, stacked, _exchange_shapes(stacked), 1, name)


def _chip_scatter(parts, name):
    return _copy_call(_scatter_copies, parts, _scatter_shapes(parts), 3, name)


def _pair_exchange_halves(shards):
    n = len(shards)

    def body(*refs):
        bufs = refs[n:2 * n]
        send_sems, recv_sems = refs[2 * n:]
        x, y, c, _ = _place()
        cps = []
        for t in range(n):
            h = bufs[t].shape[0] // 2
            rows = bufs[t].at[pl.ds(c * h, h), :]
            cps.append(_remote(rows, rows, send_sems.at[t], recv_sems.at[t], (x, y, 1 - c)))
            cps[-1].start()
        for cp in cps:
            cp.wait()

    return pl.pallas_call(
        body, name="pair_exchange_halves", in_specs=[ANY] * n, out_specs=[ANY] * n,
        out_shape=[jax.ShapeDtypeStruct(s.shape, s.dtype) for s in shards],
        input_output_aliases={t: t for t in range(n)},
        scratch_shapes=_dma_sems(n),
    )(*shards)


def _pair_sum(gs, recv, place, name):
    _, r, cols = gs.shape
    h = r // 2

    def kern(p_ref, a_ref, b_ref, o_ref):
        o_ref[...] = (a_ref[...] + b_ref[...]).astype(BF16)

    blk = lambda f: pl.BlockSpec((1, h, cols), f)
    return pl.pallas_call(
        kern, name=name,
        grid_spec=pltpu.PrefetchScalarGridSpec(
            num_scalar_prefetch=1, grid=(N_CHIPS,),
            in_specs=[blk(lambda d, p: (d, p[1], 0)), blk(lambda d, p: (d, 0, 0))],
            out_specs=blk(lambda d, p: (d, 0, 0))),
        out_shape=jax.ShapeDtypeStruct((N_CHIPS, h, cols), BF16),
        compiler_params=pltpu.CompilerParams(dimension_semantics=("arbitrary",),
                                             vmem_limit_bytes=_vmem_limit(3 * _nbytes((h, cols), F32), 0)),
    )(place, gs, recv)


def _chip_sum(gs, recv, got, place, name):
    _, r, cols = gs.shape
    h = r // 2

    def kern(p_ref, a_ref, b_ref, g0, g1, g2, o_ref):
        own = a_ref[0] + b_ref[0]
        o_ref[...] = ((own + g0[0].astype(F32)) + g1[0].astype(F32)) + g2[0].astype(F32)

    blk = lambda f: pl.BlockSpec((1, h, cols), f)
    return pl.pallas_call(
        kern, name=name,
        grid_spec=pltpu.PrefetchScalarGridSpec(
            num_scalar_prefetch=1, grid=(1,),
            in_specs=[blk(lambda i, p: (p[0], p[1], 0)), blk(lambda i, p: (p[0], 0, 0)), blk(lambda i, p: (0, 0, 0)),
                      blk(lambda i, p: (1, 0, 0)), blk(lambda i, p: (2, 0, 0))],
            out_specs=pl.BlockSpec((h, cols), lambda i, p: (p[1], 0))),
        out_shape=jax.ShapeDtypeStruct((r, cols), F32),
        compiler_params=pltpu.CompilerParams(dimension_semantics=("arbitrary",),
                                             vmem_limit_bytes=_vmem_limit(5 * _nbytes((h, cols), F32), 0)),
    )(place, gs, recv, got, got, got)


def _all_reduce_small(vec, name):
    r, cols = vec.shape

    def body(in_ref, out_ref, gath, send_sems, recv_sems):
        x, y, c, _ = _place()
        me = 4 * x + 2 * y + c
        gath[me] = in_ref[...]
        sends = []
        for k in range(1, 8):
            to = (x ^ (k >> 2), y ^ ((k >> 1) & 1), c ^ (k & 1))
            cp = pltpu.make_async_remote_copy(src_ref=in_ref, dst_ref=gath.at[me], send_sem=send_sems.at[k - 1],
                                              recv_sem=recv_sems.at[k - 1], device_id=to, device_id_type=MESH)
            cp.start()
            sends.append(cp)
        for k in range(1, 8):
            peer = me ^ k
            pltpu.make_async_remote_copy(src_ref=in_ref, dst_ref=gath.at[peer], send_sem=send_sems.at[k - 1],
                                         recv_sem=recv_sems.at[k - 1], device_id=(x, y, c),
                                         device_id_type=MESH).wait_recv()
        for cp in sends:
            cp.wait_send()
        acc = gath[0]
        for d in range(1, 8):
            acc = acc + gath[d]
        out_ref[...] = acc

    vm = pl.BlockSpec(memory_space=pltpu.VMEM)
    return pl.pallas_call(
        body, name=name, in_specs=[vm], out_specs=vm,
        out_shape=jax.ShapeDtypeStruct((r, cols), F32),
        scratch_shapes=[pltpu.VMEM((8, r, cols), F32), pltpu.SemaphoreType.DMA((7,)), pltpu.SemaphoreType.DMA((7,))],
    )(vec)


def _pad_heads(w, heads, dim, axis):
    shp = w.shape[:axis] + (heads, dim) + w.shape[axis + 1:]
    pad = [(0, 0)] * len(shp)
    pad[axis + 1] = (0, LANE - dim)
    w = jnp.pad(w.reshape(shp), pad)
    return w.reshape(w.shape[:axis] + (heads * LANE,) + w.shape[axis + 2:])


def _unpad_heads(w, heads, dim, axis):
    shp = w.shape[:axis] + (heads, LANE) + w.shape[axis + 1:]
    w = lax.slice_in_dim(w.reshape(shp), 0, dim, axis=axis + 1)
    return w.reshape(w.shape[:axis] + (heads * dim,) + w.shape[axis + 2:])


def _w_in_layout(w_in):
    kr = jnp.pad(w_in[:, 384:416], ((0, 0), (ROPE_LO, LANE - ROPE_LO - MLA_ROPE)))
    sb = lambda lo: _pad_heads(w_in[:, lo:lo + 512], SB_HEADS, SB_HEAD_DIM, 1)
    return jnp.concatenate([w_in[:, 1952:2976], w_in[:, 2976:4000], sb(416), sb(928), sb(1440), w_in[:, 0:256],
                            w_in[:, 256:384], kr], axis=1)


def _w_in_unlayout(d):
    sb = lambda lo: _unpad_heads(d[:, lo:lo + 1024], SB_HEADS, SB_HEAD_DIM, 1)
    return jnp.concatenate([d[:, C_CQ:C_CQ + 256], d[:, C_CKV:C_CKV + 128], d[:, C_KR + ROPE_LO:C_KR + ROPE_LO + MLA_ROPE],
                            sb(C_SBQ), sb(C_SBK), sb(C_SBV), d[:, C_GA:C_GA + 1024], d[:, C_GB:C_GB + 1024]], axis=1)


def _w_ukv_layout(w):
    w3 = w.reshape(MLA_KV_RANK, MLA_HEADS, MLA_NOPE + MLA_V)
    pad = lambda part: jnp.pad(part, ((0, 0), (0, 0), (0, LANE - part.shape[2]))).reshape(MLA_KV_RANK, MLA_HEADS * LANE)
    return jnp.concatenate([pad(w3[:, :, :MLA_NOPE]), pad(w3[:, :, MLA_NOPE:])], axis=1)


def _w_ukv_unlayout(d):
    hw = MLA_HEADS * LANE
    kpart = d[:, :hw].reshape(MLA_KV_RANK, MLA_HEADS, LANE)[:, :, :MLA_NOPE]
    vpart = d[:, hw:].reshape(MLA_KV_RANK, MLA_HEADS, LANE)[:, :, :MLA_V]
    return jnp.concatenate([kpart, vpart], axis=2).reshape(MLA_KV_RANK, MLA_HEADS * (MLA_NOPE + MLA_V))


def _shard_of(full, d, axis):
    n = full.shape[axis] // N_CHIPS
    return lax.slice_in_dim(full, d * n, (d + 1) * n, axis=axis)


def _local_step(x, mem, pos, target, w, t_mla, t_sb, late=None, reduce=None):
    s = x.shape[0]
    w = dict(w)
    win = _w_in_layout(w["w_in"])
    wuq = _pad_heads(w["w_uq"], MLA_HEADS, MLA_NOPE + MLA_ROPE, 1)
    wkv = _w_ukv_layout(w["w_ukv"])
    inv_freq = ROPE_THETA ** (-jnp.arange(0, MLA_ROPE, 2, dtype=F32) / MLA_ROPE)
    freq_lane = jnp.pad(jnp.concatenate([inv_freq, inv_freq]), (ROPE_LO, LANE - ROPE_LO - MLA_ROPE)).reshape(1, LANE)
    add = lambda accs, ex: (accs[0] + ex[0],)

    def add_norm(accs, ex):
        y = accs[0] + ex[0]
        return y, _rms(y, ex[1])

    tab = _rope_tables(pos.reshape(s, 1), freq_lane)
    h = _rms_fwd_call(x, w["g_mix"], "rms_mix")
    proj = _mm(h, [win], name="proj_in", tn=1408, out_dtypes=(BF16,))
    cqn, ckvn, krope = _mla_prep_fwd(proj, tab, w["g_q_lat"], w["g_kv_lat"])
    hw = MLA_HEADS * LANE
    qa = _mm(cqn, [wuq], name="q_up", row_extras=(tab,), out_dtypes=(BF16,),
             epilogue=lambda accs, ex: (_per_head(lambda t: _rope(t, ex[0]), accs[0]),))
    ka = _mm(ckvn, [wkv[:, :hw]], name="k_up", row_extras=(krope,), out_dtypes=(BF16,),
             epilogue=lambda accs, ex: (_per_head(lambda t: t + ex[0], accs[0]),))
    va = _mm(ckvn, [wkv[:, hw:]], name="v_up", out_dtypes=(BF16,))
    o_a, lse, gathered = _mla_fwd(qa, ka, va, t_mla[0], t_mla[1], late[0] if late else ())
    if late:
        w.update(late[1](gathered))
    wa = _pad_heads(w["w_a_proj"], MLA_HEADS, MLA_V, 0)
    wb = _pad_heads(w["w_b_proj"], SB_HEADS, SB_HEAD_DIM, 0)
    o_b = _sb_fwd(proj, t_sb)
    pa = _mm(o_a, [wa], name="proj_a")
    pb = _mm(o_b, [wb], name="proj_b")
    merged = _gate_fwd(proj, pa, pb, w["b_gate"])
    x1, hx = _mm(merged, [w["w_o"]], name="proj_o", extras=(x,), consts=(w["g_x"],), epilogue=add_norm,
                 out_dtypes=(F32, BF16))
    mn = _rms_fwd_call(mem, w["g_mem"], "rms_mem")
    xq = _mm(hx, [w["w_xq"]], name="xq", out_dtypes=(BF16,))
    xkv = _mm(mn, [w["w_xkv"]], name="xkv", out_dtypes=(BF16,))
    xo = _xattn_fwd(xq, xkv)
    x2, hf = _mm(xo, [w["w_xo"]], name="proj_xo", extras=(x1,), consts=(w["g_ffn"],), epilogue=add_norm,
                 out_dtypes=(F32, BF16))

    def swiglu(accs, ex):
        a, b = accs
        return a, b, a * _sigmoid(a) * b

    ga, gu, hmid = _mm(hf, [w["w_gate"], w["w_up"]], name="ffn_up", epilogue=swiglu, out_dtypes=(BF16, BF16, BF16),
                       tm=512, tn=1408)
    x3 = _mm(hmid, [w["w_down"]], name="ffn_down", extras=(x2,), epilogue=add, tk=2816)

    dx3, dg_final, sq = _loss_head(x3, target, w["g_final"].reshape(1, D_MODEL))
    g = {"g_final": dg_final.reshape(D_MODEL)}

    def swiglu_bwd(accs, ex):
        dh, a, b = accs[0], ex[0].astype(F32), ex[1].astype(F32)
        sg = _sigmoid(a)
        return dh * b * sg * (1.0 + a * (1.0 - sg)), dh * a * sg

    da, db = _mm(dx3, [w["w_down"]], name="ffn_down_dx", tb=True, extras=(ga, gu), epilogue=swiglu_bwd,
                 out_dtypes=(BF16, BF16), tm=512, tn=1408)
    g["w_down"] = _mm(hmid, [dx3], name="ffn_down_dw", ta=True, tm=1408)
    g["w_gate"] = _mm(hf, [da], name="ffn_gate_dw", ta=True, tn=1408, tk=2048)
    g["w_up"] = _mm(hf, [db], name="ffn_up_dw", ta=True, tn=1408)
    dhf = _mm(da, [w["w_gate"]], name="ffn_gate_dx", tb=True, tk=2816)
    dhf = _mm(db, [w["w_up"]], name="ffn_up_dx", tb=True, extras=(dhf,), epilogue=add, tk=2816)
    dx2, g["g_ffn"] = _rms_bwd_call(x2, w["g_ffn"], dhf, dx3, "rms_ffn_bwd")

    dxo = _mm(dx2, [w["w_xo"]], name="proj_xo_dx", tb=True, out_dtypes=(BF16,))
    g["w_xo"] = _mm(xo, [dx2], name="proj_xo_dw", ta=True)
    dxq, dxkv = _xattn_bwd(xq, xkv, dxo)
    dhx = _mm(dxq, [w["w_xq"]], name="xq_dx", tb=True)
    g["w_xq"] = _mm(hx, [dxq], name="xq_dw", ta=True)
    dmn = _mm(dxkv, [w["w_xkv"]], name="xkv_dx", tb=True)
    g["w_xkv"] = _mm(mn, [dxkv], name="xkv_dw", ta=True)
    dx1, g["g_x"] = _rms_bwd_call(x1, w["g_x"], dhx, dx2, "rms_x_bwd")
    _, g["g_mem"] = _rms_bwd_call(mem, w["g_mem"], dmn, None, "rms_mem_bwd")

    dmerged = _mm(dx1, [w["w_o"]], name="proj_o_dx", tb=True)
    g["w_o"] = _mm(merged, [dx1], name="proj_o_dw", ta=True)
    dpa, dpb, dga, dgb, g["b_gate"] = _gate_bwd(proj, pa, pb, w["b_gate"], dmerged)
    do_a = _mm(dpa, [wa], name="proj_a_dx", tb=True, out_dtypes=(BF16,))
    do_b = _mm(dpb, [wb], name="proj_b_dx", tb=True, out_dtypes=(BF16,))
    g["w_a_proj"] = _unpad_heads(_mm(o_a, [dpa], name="proj_a_dw", ta=True), MLA_HEADS, MLA_V, 0)
    g["w_b_proj"] = _unpad_heads(_mm(o_b, [dpb], name="proj_b_dw", ta=True), SB_HEADS, SB_HEAD_DIM, 0)

    stacked = [reduce[1](n, g[n]) for n in reduce[0]] if reduce else []
    dsq, dsk, dsv, recv = _sb_bwd(proj, o_b, do_b, t_sb, stacked)
    parts = [reduce[2](n, gs, rv) for n, gs, rv in zip(reduce[0], stacked, recv)] if reduce else []
    dqa, dka, dva, got = _mla_bwd(qa, ka, va, o_a, do_a, lse, t_mla[0], t_mla[2], parts)
    riding = dict(zip(reduce[0], zip(stacked, recv, got))) if reduce else {}
    dqp, dkvp, dkr = _mla_rope_bwd(dqa, dka, dva, tab)
    g["w_uq"] = _unpad_heads(_mm(cqn, [dqp], name="q_up_dw", ta=True), MLA_HEADS, MLA_NOPE + MLA_ROPE, 1)
    g["w_ukv"] = _w_ukv_unlayout(_mm(ckvn, [dkvp], name="kv_up_dw", ta=True))
    dcqn = _mm(dqp, [wuq], name="q_up_dx", tb=True)
    dckvn = _mm(dkvp, [wkv], name="kv_up_dx", tb=True)
    dcq, dckv, g["g_q_lat"], g["g_kv_lat"] = _mla_prep_bwd(proj, w["g_q_lat"], w["g_kv_lat"], dcqn, dckvn)

    dproj = jnp.concatenate([dga, dgb, dsq.astype(BF16), dsk.astype(BF16), dsv.astype(BF16), dcq, dckv, dkr], axis=1)
    g["w_in"] = _w_in_unlayout(_mm(h, [dproj], name="proj_in_dw", ta=True, tn=1408))
    dh = _mm(dproj, [win], name="proj_in_dx", tb=True, tk=2816)
    grad_x, g["g_mix"] = _rms_bwd_call(x, w["g_mix"], dh, dx1, "rms_mix_bwd")
    return sq, grad_x, g, riding


def _small_pack(d):
    row5 = jnp.concatenate([d["g_q_lat"].reshape(-1), d["g_kv_lat"].reshape(-1), jnp.zeros((640,), F32)])
    rows = [d[n].reshape(-1) for n in ("g_mix", "g_x", "g_mem", "g_ffn", "g_final")] + [row5]
    return rows


def _small_unpack(p, like):
    out = {n: p[i].reshape(like[n].shape) for i, n in enumerate(("g_mix", "g_x", "g_mem", "g_ffn", "g_final"))}
    out["g_q_lat"] = p[5, 0:256].reshape(like["g_q_lat"].shape)
    out["g_kv_lat"] = p[5, 256:384].reshape(like["g_kv_lat"].shape)
    return out


def kernel(x, mem, positions, g_mix, w_in, b_gate, g_q_lat, w_uq, g_kv_lat, w_ukv, w_a_proj, w_b_proj, w_o, g_x, g_mem, w_xq, w_xkv, w_xo, g_ffn, w_gate, w_up, w_down, g_final, loss_target, m_g_mix, m_w_in, m_b_gate, m_g_q_lat, m_w_uq, m_g_kv_lat, m_w_ukv, m_w_a_proj, m_w_b_proj, m_w_o, m_g_x, m_g_mem, m_w_xq, m_w_xkv, m_w_xo, m_g_ffn, m_w_gate, m_w_up, m_w_down, m_g_final, v_g_mix, v_w_in, v_b_gate, v_g_q_lat, v_w_uq, v_g_kv_lat, v_w_ukv, v_w_a_proj, v_w_b_proj, v_w_o, v_g_x, v_g_mem, v_w_xq, v_w_xkv, v_w_xo, v_g_ffn, v_w_gate, v_w_up, v_w_down, v_g_final):
    given = dict(locals())
    names = [n for n, _, _ in MATS] + ["b_gate"] + list(SMALL)
    wts = {n: given[n] for n in names}
    mom = {n: given["m_" + n] for n in names}
    var = {n: given["v_" + n] for n in names}
    shard2d = {n: shp for n, shp, _ in MATS}
    shard2d["b_gate"] = B_GATE_SHARD
    cx, cy, cc = lax.axis_index("x"), lax.axis_index("y"), lax.axis_index("c")
    me = 2 * cx + cy
    place = jnp.stack([me, cc]).astype(jnp.int32)
    bcol = me * B_GATE_SHARD[1]

    own = [wts[n].reshape(shard2d[n]).astype(BF16) for n, _, _ in MATS]

    def assemble(mats, gathered, mine):
        out = {}
        for (n, shp, ax), g4, shard in zip(mats, gathered, mine):
            g4 = lax.dynamic_update_slice(g4, shard[None], (me, 0, 0))
            out[n] = g4.reshape(N_CHIPS * shp[0], shp[1]) if ax == 0 else jnp.concatenate(list(g4), axis=1)
        return out

    full = assemble(MATS[:N_EARLY], _all_gather_weights(own[:N_EARLY]), own[:N_EARLY])
    late = (own[N_EARLY:], lambda gathered: assemble(MATS[N_EARLY:], gathered, own[N_EARLY:]))
    bias_rows = jnp.pad(wts["b_gate"].reshape(B_GATE_SHARD), ((0, SMALL_ROWS - 2), (0, 0)))
    bias_rows = lax.dynamic_update_slice(jnp.zeros((SMALL_ROWS, D_MODEL), F32), bias_rows, (0, bcol))
    full["b_gate"] = _all_reduce_small(jnp.where(cc == 0, bias_rows, 0.0), "all_gather_bias")[0:2]
    for n in SMALL:
        full[n] = wts[n].reshape(1, -1) if n != "g_final" else wts[n]

    axis_of = {n: ax for n, _, ax in MATS}
    stack = lambda n, g: jnp.stack([_shard_of(g, d, axis_of[n]) for d in range(N_CHIPS)])
    pair_sum = lambda n, gs, rv: _pair_sum(gs, rv, place, "pair_sum_" + n)
    behind = [n for n, _, _ in MATS[N_EARLY:]]
    sq, grad_x, grads, riding = _local_step(x[0], mem[0], positions[0], loss_target[0], full, t_mla=(1024, 1024, 1024), t_sb=256,
                                            late=late, reduce=(behind, stack, pair_sum))

    last = [n for n, _, _ in MATS[:N_EARLY]]
    stacked = [stack(n, grads[n]) for n in last]
    recv = _pair_exchange_grads(stacked, "pair_exchange_grads")
    got = _chip_scatter([pair_sum(n, gs, rv) for n, gs, rv in zip(last, stacked, recv)], "chip_scatter")
    riding.update(zip(last, zip(stacked, recv, got)))
    halves = [_chip_sum(*riding[n], place, "chip_sum_" + n) for n, _, _ in MATS]
    g_shard = dict(zip([n for n, _, _ in MATS], _pair_exchange_halves(halves)))

    small_rows = _small_pack({n: grads[n] for n in SMALL}) + [sq.reshape(-1), grads["b_gate"][0], grads["b_gate"][1]]
    small_rows += [jnp.zeros((D_MODEL,), F32)] * (SMALL_ROWS - len(small_rows))
    small = _all_reduce_small(jnp.stack(small_rows), "all_reduce_small")
    loss = (0.5 / D_MODEL) * jnp.sum(small[6])
    g_shard["b_gate"] = lax.dynamic_slice(small[7:9], (0, bcol), B_GATE_SHARD)

    out = {"grad": {}, "delta": {}, "m": {}, "v": {}}
    for n in [n for n, _, _ in MATS] + ["b_gate"]:
        shape = wts[n].shape
        r2 = lambda a: a.reshape(shard2d[n])
        d_n, m_n, v_n = _adamw(r2(wts[n]), g_shard[n], r2(mom[n]), r2(var[n]), "adamw_" + n)
        for key, a in (("grad", g_shard[n]), ("delta", d_n), ("m", m_n), ("v", v_n)):
            out[key][n] = a.reshape(shape)
    sp = lambda d: jnp.stack(_small_pack(d) + [jnp.zeros((D_MODEL,), F32)] * 2)
    delta_s, m_s, v_s = _adamw(sp(wts), small[0:8].at[6:8].set(0.0), sp(mom), sp(var), "adamw_small")
    for key, p in (("grad", small), ("delta", delta_s), ("m", m_s), ("v", v_s)):
        out[key].update(_small_unpack(p, wts))

    order = ["g_mix", "w_in", "b_gate", "g_q_lat", "w_uq", "g_kv_lat", "w_ukv", "w_a_proj", "w_b_proj", "w_o", "g_x",
             "g_mem", "w_xq", "w_xkv", "w_xo", "g_ffn", "w_gate", "w_up", "w_down", "g_final"]
    return (loss, grad_x[None], *[out[key][n] for key in ("grad", "delta", "m", "v") for n in order])
```

```python
import functools
import math

import jax
import jax.numpy as jnp
from jax import lax
from jax.experimental import pallas as pl
from jax.experimental.pallas import tpu as pltpu

F32 = jnp.float32
BF16 = jnp.bfloat16
MESH = pl.DeviceIdType.MESH

D_MODEL = 1024
MLA_HEADS = 8
MLA_Q_RANK = 256
MLA_KV_RANK = 128
MLA_NOPE = 64
MLA_ROPE = 32
MLA_V = 64
ROPE_THETA = 10000.0
SB_HEADS = 8
SB_HEAD_DIM = 64
X_HEADS = 4
X_HEAD_DIM = 128
EPS = 1e-6
ADAM_LR = 0.001
ADAM_B1 = 0.9
ADAM_B2 = 0.999
ADAM_EPS = 1e-08
ADAM_WD = 0.01
ADAM_STEP = 10

LANE = 128
LOG2E = 1.4426950408889634
MM_CHUNK = 256
N_CHIPS = 4
VMEM_BYTES = 64 * 1024 * 1024

C_GA, C_GB, C_SBQ, C_SBK, C_SBV, C_CQ, C_CKV, C_KR = 0, 1024, 2048, 3072, 4096, 5120, 5376, 5504
ROPE_LO = MLA_NOPE
HALF = MLA_ROPE // 2

SB_ZERO_LOG = -104.0

MATS = (
    ("w_in", (1024, 1000), 1), ("w_uq", (256, 192), 1), ("w_ukv", (128, 256), 1), ("w_a_proj", (512, 256), 1),
    ("w_b_proj", (512, 256), 1), ("w_o", (256, 1024), 0), ("w_xq", (256, 512), 0), ("w_xkv", (256, 1024), 0),
    ("w_xo", (512, 256), 1), ("w_gate", (1024, 704), 1), ("w_up", (1024, 704), 1), ("w_down", (704, 1024), 0),
)
N_EARLY = 3
B_GATE_SHARD = (2, 256)
BIAS_ROWS = 16
SMALL = ("g_mix", "g_x", "g_mem", "g_ffn", "g_final", "g_q_lat", "g_kv_lat")
SMALL_ROWS = 16


def _vmem_limit(block_bytes, temp_bytes):
    est = 2 * block_bytes + temp_bytes + (4 << 20)
    return int(min(max(est, 16 << 20), VMEM_BYTES - (6 << 20)))


def _nbytes(shape, dtype):
    return math.prod(shape) * jnp.dtype(dtype).itemsize


def _row_tile(rows, cap):
    if rows <= cap:
        return rows
    return max(t for t in range(8, cap + 1, 8) if rows % t == 0)


def _tile(n, cap):
    if n <= cap:
        return n
    best = None
    for t in range(LANE, cap + 1, LANE):
        if n % t == 0:
            best = t
    assert best is not None, (n, cap)
    return best


def _mm(a, bs, *, name, ta=False, tb=False, extras=(), row_extras=(), consts=(), epilogue=None, out_dtypes=(F32,),
        tm=1024, tn=1024, tk=1024, chunk=None):
    bs = tuple(bs)
    m, k = (a.shape[1], a.shape[0]) if ta else a.shape
    n = bs[0].shape[0] if tb else bs[0].shape[1]
    tm, tn, tk = _tile(m, tm), _tile(n, tn), _tile(k, tk)
    assert m % tm == 0 and n % tn == 0 and k % tk == 0
    nk = k // tk
    nb, ne, no = len(bs), len(extras) + len(row_extras) + len(consts), len(out_dtypes)
    dims = (((0,) if ta else (1,)), ((1,) if tb else (0,))), ((), ())
    if epilogue is None:
        epilogue = lambda accs, ex: (accs[0],)

    def body(*refs):
        a_ref, b_refs, e_refs = refs[0], refs[1:1 + nb], refs[1 + nb:1 + nb + ne]
        o_refs, acc_refs = refs[1 + nb + ne:1 + nb + ne + no], refs[1 + nb + ne + no:]
        if nk == 1:
            ch = chunk or tm
            bvs = [b_ref[...].astype(BF16) for b_ref in b_refs]
            for r0 in range(0, tm, ch):
                rows = slice(r0, r0 + ch)
                av = (a_ref[:, rows] if ta else a_ref[rows, :]).astype(BF16)
                accs = [lax.dot_general(av, bv, dims, preferred_element_type=F32) for bv in bvs]
                ex = [e[rows, :] for e in e_refs[:ne - len(consts)]] + [e[...] for e in e_refs[ne - len(consts):]]
                for o_ref, v in zip(o_refs, epilogue(accs, ex)):
                    o_ref[rows, :] = v.astype(o_ref.dtype)
            return
        kk = pl.program_id(2)

        @pl.when(kk == 0)
        def _():
            for acc in acc_refs:
                acc[...] = jnp.zeros_like(acc)

        av = a_ref[...].astype(BF16)
        for b_ref, acc in zip(b_refs, acc_refs):
            acc[...] += lax.dot_general(av, b_ref[...].astype(BF16), dims, preferred_element_type=F32)

        @pl.when(kk == nk - 1)
        def _():
            outs = epilogue([acc[...] for acc in acc_refs], [e[...] for e in e_refs])
            for o_ref, v in zip(o_refs, outs):
                o_ref[...] = v.astype(o_ref.dtype)

    a_spec = pl.BlockSpec((tk, tm), lambda i, j, kk: (kk, i)) if ta else pl.BlockSpec((tm, tk), lambda i, j, kk: (i, kk))
    b_spec = pl.BlockSpec((tn, tk), lambda i, j, kk: (j, kk)) if tb else pl.BlockSpec((tk, tn), lambda i, j, kk: (kk, j))
    mn_spec = pl.BlockSpec((tm, tn), lambda i, j, kk: (i, j))
    blocks = (_nbytes((tm, tk), a.dtype) + sum(_nbytes((tk, tn), b.dtype) for b in bs)
              + sum(_nbytes((tm, tn), e.dtype) for e in extras) + sum(_nbytes((tm, tn), d) for d in out_dtypes)
              + sum(_nbytes((tm, e.shape[1]), e.dtype) for e in row_extras))
    temps = (nb + 4) * _nbytes((tm, tn), F32)
    outs = pl.pallas_call(
        body, name=name, grid=(m // tm, n // tn, nk),
        in_specs=[a_spec] + [b_spec] * nb + [mn_spec] * len(extras)
        + [pl.BlockSpec((tm, e.shape[1]), lambda i, j, kk: (i, 0)) for e in row_extras]
        + [pl.BlockSpec(e.shape, lambda i, j, kk: (0, 0)) for e in consts],
        out_specs=[mn_spec] * no,
        out_shape=[jax.ShapeDtypeStruct((m, n), d) for d in out_dtypes],
        scratch_shapes=[pltpu.VMEM((tm, tn), F32) for _ in range(nb if nk > 1 else 0)],
        compiler_params=pltpu.CompilerParams(
            dimension_semantics=("parallel", "parallel", "arbitrary"),
            vmem_limit_bytes=_vmem_limit(blocks, temps)),
    )(a, *bs, *extras, *row_extras, *consts)
    return outs[0] if no == 1 else outs


def _rowwise(body, *, name, rows, tr, row_ins, full_ins=(), row_outs=(), acc_outs=()):
    tr = min(tr, rows)
    assert rows % tr == 0
    n_ri, n_fi, n_ro = len(row_ins), len(full_ins), len(row_outs)

    def kern(*refs):
        body(pl.program_id(0), refs[:n_ri], refs[n_ri:n_ri + n_fi], refs[n_ri + n_fi:n_ri + n_fi + n_ro],
             refs[n_ri + n_fi + n_ro:])

    in_specs = [pl.BlockSpec((tr, w), functools.partial(lambda i, c: (i, c), c=ci)) for _, w, ci in row_ins]
    in_specs += [pl.BlockSpec(f.shape, lambda i: (0, 0)) for f in full_ins]
    out_specs = [pl.BlockSpec((tr, w), lambda i: (i, 0)) for w, _ in row_outs]
    out_specs += [pl.BlockSpec(s, lambda i: (0, 0)) for s, _ in acc_outs]
    out_shape = [jax.ShapeDtypeStruct((rows, w), d) for w, d in row_outs]
    out_shape += [jax.ShapeDtypeStruct(s, d) for s, d in acc_outs]
    blocks = (sum(_nbytes((tr, w), a.dtype) for a, w, _ in row_ins) + sum(_nbytes(f.shape, f.dtype) for f in full_ins)
              + sum(_nbytes((tr, w), d) for w, d in row_outs) + sum(_nbytes(s, d) for s, d in acc_outs))
    widest = max([w for _, w, _ in row_ins] + [w for w, _ in row_outs])
    outs = pl.pallas_call(
        kern, name=name, grid=(rows // tr,), in_specs=in_specs, out_specs=out_specs, out_shape=out_shape,
        compiler_params=pltpu.CompilerParams(
            dimension_semantics=("arbitrary",) if acc_outs else ("parallel",),
            vmem_limit_bytes=_vmem_limit(blocks, 8 * _nbytes((tr, widest), F32))),
    )(*[a for a, _, _ in row_ins], *full_ins)
    return outs


def _rms(x, g):
    r = lax.rsqrt(jnp.mean(x * x, axis=-1, keepdims=True) + EPS)
    return x * r * g


def _rms_bwd(x, g, dy):
    r = lax.rsqrt(jnp.mean(x * x, axis=-1, keepdims=True) + EPS)
    xh = x * r
    dxh = dy * g
    dx = r * (dxh - xh * jnp.mean(dxh * xh, axis=-1, keepdims=True))
    return dx, jnp.sum(dy * xh, axis=0, keepdims=True)


def _sigmoid(x):
    return 1.0 / (1.0 + jnp.exp(-x))


def _acc_init(i, refs):
    @pl.when(i == 0)
    def _():
        for r in refs:
            r[...] = jnp.zeros_like(r)


def _rms_fwd_call(x, g, name):
    rows, c = x.shape

    def body(i, ins, fulls, outs, accs):
        outs[0][...] = _rms(ins[0][...], fulls[0][...]).astype(BF16)

    return _rowwise(body, name=name, rows=rows, tr=512, row_ins=[(x, c, 0)], full_ins=[g], row_outs=[(c, BF16)])[0]


def _rms_bwd_call(x, g, dy, res, name):
    rows, c = x.shape
    row_ins = [(x, c, 0), (dy, c, 0)] + ([(res, c, 0)] if res is not None else [])

    def body(i, ins, fulls, outs, accs):
        _acc_init(i, accs)
        dx, dg = _rms_bwd(ins[0][...], fulls[0][...], ins[1][...].astype(F32))
        if res is not None:
            dx = dx + ins[2][...]
        outs[0][...] = dx
        accs[0][...] += dg

    return _rowwise(body, name=name, rows=rows, tr=512, row_ins=row_ins, full_ins=[g], row_outs=[(c, F32)],
                    acc_outs=[((1, c), F32)])


def _rope_tables(pos_col, freq_lane):
    rows = pos_col.shape[0]

    def body(i, ins, fulls, outs, accs):
        ang = ins[0][...].astype(F32) * fulls[0][...]
        lane = lax.broadcasted_iota(jnp.int32, ang.shape, 1)
        cos, sin = jnp.cos(ang), jnp.sin(ang)
        first = (lane >= ROPE_LO) & (lane < ROPE_LO + HALF)
        second = (lane >= ROPE_LO + HALF) & (lane < ROPE_LO + MLA_ROPE)
        outs[0][:, 0:LANE] = jnp.where(first | second, cos, 1.0)
        outs[0][:, LANE:2 * LANE] = jnp.where(first, -sin, 0.0)
        outs[0][:, 2 * LANE:3 * LANE] = jnp.where(second, sin, 0.0)

    return _rowwise(body, name="rope_tables", rows=rows, tr=1024, row_ins=[(pos_col, 1, 0)], full_ins=[freq_lane],
                    row_outs=[(3 * LANE, F32)])[0]


def _rope(x, tab):
    return (x * tab[:, 0:LANE] + pltpu.roll(x, LANE - HALF, 1) * tab[:, LANE:2 * LANE]
            + pltpu.roll(x, HALF, 1) * tab[:, 2 * LANE:3 * LANE])


def _rope_t(dy, tab):
    return (dy * tab[:, 0:LANE] + pltpu.roll(dy * tab[:, LANE:2 * LANE], HALF, 1)
            + pltpu.roll(dy * tab[:, 2 * LANE:3 * LANE], LANE - HALF, 1))


def _mla_prep_fwd(proj, tab, g_q, g_kv):
    rows = proj.shape[0]

    def body(i, ins, fulls, outs, accs):
        outs[0][...] = _rms(ins[0][...].astype(F32), fulls[0][...]).astype(BF16)
        outs[1][...] = _rms(ins[1][...].astype(F32), fulls[1][...]).astype(BF16)
        outs[2][...] = _rope(ins[2][...].astype(F32), ins[3][...])

    return _rowwise(body, name="mla_prep_fwd", rows=rows, tr=512,
                    row_ins=[(proj, MLA_Q_RANK, C_CQ // MLA_Q_RANK), (proj, LANE, C_CKV // LANE),
                             (proj, LANE, C_KR // LANE), (tab, 3 * LANE, 0)],
                    full_ins=[g_q, g_kv], row_outs=[(MLA_Q_RANK, BF16), (MLA_KV_RANK, BF16), (LANE, F32)])


def _mla_prep_bwd(proj, g_q, g_kv, dcqn, dckvn):
    rows = proj.shape[0]

    def body(i, ins, fulls, outs, accs):
        _acc_init(i, accs)
        dcq, dgq = _rms_bwd(ins[0][...].astype(F32), fulls[0][...], ins[2][...])
        dckv, dgkv = _rms_bwd(ins[1][...].astype(F32), fulls[1][...], ins[3][...])
        outs[0][...] = dcq.astype(BF16)
        outs[1][...] = dckv.astype(BF16)
        accs[0][...] += dgq
        accs[1][...] += dgkv

    return _rowwise(body, name="mla_prep_bwd", rows=rows, tr=512,
                    row_ins=[(proj, MLA_Q_RANK, C_CQ // MLA_Q_RANK), (proj, LANE, C_CKV // LANE),
                             (dcqn, MLA_Q_RANK, 0), (dckvn, MLA_KV_RANK, 0)],
                    full_ins=[g_q, g_kv], row_outs=[(MLA_Q_RANK, BF16), (MLA_KV_RANK, BF16)],
                    acc_outs=[((1, MLA_Q_RANK), F32), ((1, MLA_KV_RANK), F32)])


def _per_head(fn, x):
    return jnp.concatenate([fn(x[:, h * LANE:(h + 1) * LANE]) for h in range(x.shape[1] // LANE)], axis=1)


def _mla_rope_bwd(dq, dk, dv, tab):
    rows = dq.shape[0]
    hw = MLA_HEADS * LANE

    def body(i, ins, fulls, outs, accs):
        t = ins[3][...]
        dkr = jnp.zeros((ins[0].shape[0], LANE), F32)
        for h in range(MLA_HEADS):
            sl = slice(h * LANE, (h + 1) * LANE)
            outs[0][:, sl] = _rope_t(ins[0][:, sl], t).astype(BF16)
            dkr = dkr + ins[1][:, sl]
        outs[1][:, 0:hw] = ins[1][...].astype(BF16)
        outs[1][:, hw:2 * hw] = ins[2][...].astype(BF16)
        lane = lax.broadcasted_iota(jnp.int32, dkr.shape, 1)
        dkr = jnp.where((lane >= ROPE_LO) & (lane < ROPE_LO + MLA_ROPE), dkr, 0.0)
        outs[2][...] = _rope_t(dkr, t).astype(BF16)

    return _rowwise(body, name="mla_rope_bwd", rows=rows, tr=512,
                    row_ins=[(dq, hw, 0), (dk, hw, 0), (dv, hw, 0), (tab, 3 * LANE, 0)],
                    row_outs=[(hw, BF16), (2 * hw, BF16), (LANE, BF16)])


def _dot_nt(a, b):
    return lax.dot_general(a, b, (((1,), (1,)), ((), ())), preferred_element_type=F32)


def _dot_tn(a, b):
    return lax.dot_general(a, b, (((0,), (0,)), ((), ())), preferred_element_type=F32)


def _dot(a, b):
    return jnp.dot(a, b, preferred_element_type=F32)


def _attn_params(s, t, n_res_f32, n_res_bf16, ride=False):
    blocks = n_res_f32 * _nbytes((s, LANE), F32) + n_res_bf16 * _nbytes((s, LANE), BF16) + 6 * _nbytes((t, LANE), F32)
    return pltpu.CompilerParams(dimension_semantics=("arbitrary" if ride else "parallel", "arbitrary"),
                                vmem_limit_bytes=_vmem_limit(blocks, 12 * _nbytes((t, t), F32)))


def _mla_fwd(q, k, v, t, tk, shards=()):
    s, hw = q.shape
    heads, nq, r = hw // LANE, s // t, t // tk
    ng = len(shards)
    scale = 1.0 / math.sqrt(MLA_NOPE + MLA_ROPE)
    scale2 = scale * LOG2E

    def body(q_ref, k_ref, v_ref, *rest):
        o_ref, l_ref = rest[ng], rest[ng + 1]
        h, i = pl.program_id(0), pl.program_id(1)
        if ng:
            gather = _GatherPhases(rest[:ng], rest[ng + 2:2 * ng + 2], *rest[2 * ng + 2:])
            pl.when((h == 0) & (i == 0))(gather.send)
            pl.when((h == heads // 2) & (i == 0))(gather.forward)
        qv = q_ref[...]

        def step(j, carry, off):
            m, l, acc = carry
            sl = pl.ds(pl.multiple_of(j * tk, tk), tk)
            sc = _dot_nt(qv, k_ref[sl, :])
            if off is not None:
                row = lax.broadcasted_iota(jnp.int32, (t, tk), 0)
                col = lax.broadcasted_iota(jnp.int32, (t, tk), 1)
                sc = jnp.where(col + off <= row, sc, -1e30)
            m_new = jnp.maximum(m, jnp.max(sc, axis=1, keepdims=True))
            p = jnp.exp2((sc - m_new) * scale2)
            alpha = jnp.exp2((m - m_new) * scale2)
            l = alpha * l + jnp.sum(p, axis=1, keepdims=True)
            acc = alpha * acc + _dot(p.astype(BF16), v_ref[sl, :])
            return m_new, l, acc

        init = (jnp.full((t, 1), -1e30, F32), jnp.zeros((t, 1), F32), jnp.zeros((t, LANE), F32))
        carry = lax.fori_loop(0, i * r, lambda j, c: step(j, c, None), init)
        for jj in range(r):
            carry = step(i * r + jj, carry, jj * tk)
        m, l, acc = carry
        o_ref[...] = (acc / l).astype(o_ref.dtype)
        l_ref[0] = m * scale2 + jnp.log2(l)
        if ng:
            pl.when((h == heads - 1) & (i == nq - 1))(gather.finish)

    blk = pl.BlockSpec((t, LANE), lambda h, i: (i, h))
    res = pl.BlockSpec((s, LANE), lambda h, i: (0, h))
    outs = pl.pallas_call(
        body, name="mla_fwd", grid=(heads, nq), in_specs=[blk, res, res] + [ANY] * ng,
        out_specs=[blk, pl.BlockSpec((1, t, 1), lambda h, i: (h, i, 0))] + [ANY] * ng,
        out_shape=[jax.ShapeDtypeStruct((s, hw), BF16), jax.ShapeDtypeStruct((heads, s, 1), F32)]
        + [jax.ShapeDtypeStruct((N_CHIPS,) + sh.shape, sh.dtype) for sh in shards],
        scratch_shapes=_dma_sems(6 * ng) if ng else [],
        compiler_params=_attn_params(s, t, 0, 2, ride=ng > 0),
    )(q, k, v, *shards)
    return outs[0], outs[1], list(outs[2:])


def _mla_bwd(q, k, v, o, do, lse, t, tk, parts=()):
    s, hw = q.shape
    heads, nq, r = hw // LANE, s // t, t // tk
    ns = len(parts)
    scale = 1.0 / math.sqrt(MLA_NOPE + MLA_ROPE)
    scale2 = scale * LOG2E

    def body(q_ref, k_ref, v_ref, o_ref, do_ref, l_ref, *rest):
        dq_ref, dk_ref, dv_ref = rest[ns:ns + 3]
        h, i = pl.program_id(0), pl.program_id(1)
        finish = _ride_along(_scatter_copies, ns, (rest[:ns], rest[ns + 3:2 * ns + 3], *rest[2 * ns + 3:]),
                             (h == 0) & (i == 0), (h == heads - 1) & (i == nq - 1))

        @pl.when(i == 0)
        def _():
            dk_ref[...] = jnp.zeros_like(dk_ref)
            dv_ref[...] = jnp.zeros_like(dv_ref)

        qv, dov, lv = q_ref[...], do_ref[...], l_ref[0]
        dlt = jnp.sum(dov.astype(F32) * o_ref[...].astype(F32), axis=1, keepdims=True)

        def step(j, dq, off):
            sl = pl.ds(pl.multiple_of(j * tk, tk), tk)
            kv, vv = k_ref[sl, :], v_ref[sl, :]
            p = jnp.exp2(_dot_nt(qv, kv) * scale2 - lv)
            if off is not None:
                row = lax.broadcasted_iota(jnp.int32, (t, tk), 0)
                col = lax.broadcasted_iota(jnp.int32, (t, tk), 1)
                p = jnp.where(col + off <= row, p, 0.0)
            ds = (p * (_dot_nt(dov, vv) - dlt)).astype(BF16)
            dk_ref[sl, :] += _dot_tn(ds, qv) * scale
            dv_ref[sl, :] += _dot_tn(p.astype(BF16), dov)
            return dq + _dot(ds, kv)

        dq = lax.fori_loop(0, i * r, lambda j, c: step(j, c, None), jnp.zeros((t, LANE), F32))
        for jj in range(r):
            dq = step(i * r + jj, dq, jj * tk)
        dq_ref[...] = dq * scale
        finish()

    blk = pl.BlockSpec((t, LANE), lambda h, i: (i, h))
    res = pl.BlockSpec((s, LANE), lambda h, i: (0, h))
    full = jax.ShapeDtypeStruct((s, hw), F32)
    outs = pl.pallas_call(
        body, name="mla_bwd", grid=(heads, nq),
        in_specs=[blk, res, res, blk, blk, pl.BlockSpec((1, t, 1), lambda h, i: (h, i, 0))] + [ANY] * ns,
        out_specs=[blk, res, res] + [ANY] * ns, out_shape=[full, full, full] + _scatter_shapes(parts),
        scratch_shapes=_dma_sems(3 * ns) if ns else [],
        compiler_params=_attn_params(s, t, 2, 2, ride=ns > 0),
    )(q, k, v, o, do, lse, *parts)
    return outs[0], outs[1], outs[2], list(outs[3:])


def _sb_logits(qv, kv, scale, masked, t, upper):
    z = _dot_nt(qv, kv) * scale
    e = jnp.exp(-jnp.abs(z))
    l1p = jnp.log(1.0 + e)
    lb = jnp.minimum(z, 0.0) - l1p
    lo = -jnp.maximum(z, 0.0) - l1p
    keep = None
    if masked:
        row = lax.broadcasted_iota(jnp.int32, (t, t), 0)
        col = lax.broadcasted_iota(jnp.int32, (t, t), 1)
        keep = col < row
        lo = jnp.where(keep, lo, 0.0)
    hi = lo.astype(BF16)
    rem = (lo - hi.astype(F32)).astype(BF16)
    suf = _dot(hi, upper) + _dot(rem, upper)
    return z, e, lb, lo, suf, keep


def _tri(t, inclusive):
    row = lax.broadcasted_iota(jnp.int32, (t, t), 0)
    col = lax.broadcasted_iota(jnp.int32, (t, t), 1)
    return jnp.where((row >= col) if inclusive else (row > col), 1.0, 0.0).astype(BF16)


SB_CHAINS = 4


def _sb_walk(i, first, carries_of):
    n = SB_CHAINS
    carries = [first(c) for c in range(n)]
    width = len(carries[0])

    def alive(carry):
        return jnp.max(carry[0]) >= SB_ZERO_LOG

    def split(st):
        return [tuple(st[1 + c * width:1 + (c + 1) * width]) for c in range(n)]

    def live(st):
        any_alive = alive(split(st)[0])
        for cr in split(st)[1:]:
            any_alive = any_alive | alive(cr)
        return (st[0] <= n * i) & any_alive

    def more(st):
        out = (st[0] + 1,)
        for c, cr in enumerate(split(st)):
            out += tuple(carries_of(c, st[0], cr))
        return out

    st = lax.while_loop(live, more, (jnp.int32(1),) + tuple(x for cr in carries for x in cr))
    jj, carries = st[0], split(st)
    for c in range(1, n):
        def live_c(s2, c=c):
            return (s2[0] <= n * i + c) & alive(s2[1:])

        def more_c(s2, c=c):
            return (s2[0] + 1,) + tuple(carries_of(c, s2[0], s2[1:]))

        carries[c] = lax.while_loop(live_c, more_c, (jj,) + tuple(carries[c]))[1:]
    return carries


def _sb_fwd(proj, t):
    s = proj.shape[0]
    heads, nq, n = SB_HEADS, s // t, SB_CHAINS
    scale = 1.0 / math.sqrt(SB_HEAD_DIM)

    def body(q_ref, k_ref, v_ref, o_ref):
        i = pl.program_id(1)
        upper = _tri(t, False)
        qs = [q_ref[c * t:(c + 1) * t, :].astype(BF16) for c in range(n)]

        def step(c, jj, carry, masked):
            run, acc = carry
            sl = pl.ds(pl.multiple_of((n * i + c - jj) * t, t), t)
            _, _, lb, lo, suf, keep = _sb_logits(qs[c], k_ref[sl, :].astype(BF16), scale, masked, t, upper)
            a = jnp.exp(lb + suf + run)
            if masked:
                a = jnp.where(keep, a, 0.0)
            acc = acc + _dot(a.astype(BF16), v_ref[sl, :].astype(BF16))
            return run + jnp.sum(lo, axis=1, keepdims=True), acc

        init = (jnp.zeros((t, 1), F32), jnp.zeros((t, LANE), F32))
        carries = _sb_walk(i, lambda c: step(c, 0, init, True), lambda c, jj, cr: step(c, jj, cr, False))
        for c in range(n):
            o_ref[c * t:(c + 1) * t, :] = carries[c][1]

    return pl.pallas_call(
        body, name="sb_fwd", grid=(heads, nq // n),
        in_specs=[pl.BlockSpec((n * t, LANE), lambda h, i: (i, C_SBQ // LANE + h)),
                  pl.BlockSpec((s, LANE), lambda h, i: (0, C_SBK // LANE + h)),
                  pl.BlockSpec((s, LANE), lambda h, i: (0, C_SBV // LANE + h))],
        out_specs=pl.BlockSpec((n * t, LANE), lambda h, i: (i, h)),
        out_shape=jax.ShapeDtypeStruct((s, heads * LANE), F32),
        compiler_params=_attn_params(s, n * t, 0, 2),
    )(proj, proj, proj)


def _sb_bwd(proj, o, do, t, stacked=()):
    s = proj.shape[0]
    heads, nq, n = SB_HEADS, s // t, SB_CHAINS
    nx = len(stacked)
    scale = 1.0 / math.sqrt(SB_HEAD_DIM)

    def body(q_ref, k_ref, v_ref, o_ref, do_ref, *rest):
        dq_ref, dk_ref, dv_ref = rest[nx:nx + 3]
        hd, i = pl.program_id(0), pl.program_id(1)
        finish = _ride_along(_exchange_copies, nx, (rest[:nx], rest[nx + 3:2 * nx + 3], *rest[2 * nx + 3:]),
                             (hd == 0) & (i == 0), (hd == heads - 1) & (i == nq // n - 1))

        @pl.when(i == 0)
        def _():
            dk_ref[...] = jnp.zeros_like(dk_ref)
            dv_ref[...] = jnp.zeros_like(dv_ref)

        rows = [slice(c * t, (c + 1) * t) for c in range(n)]
        qs = [q_ref[r, :].astype(BF16) for r in rows]
        dos = [do_ref[r, :] for r in rows]
        totals = [jnp.sum(dos[c].astype(F32) * o_ref[rows[c], :], axis=1, keepdims=True) for c in range(n)]
        upper, upper_incl = _tri(t, False), _tri(t, True)

        def step(c, jj, carry, masked):
            run, g, dq = carry
            qv, dov = qs[c], dos[c]
            sl = pl.ds(pl.multiple_of((n * i + c - jj) * t, t), t)
            kv, vv = k_ref[sl, :].astype(BF16), v_ref[sl, :].astype(BF16)
            z, e, lb, lo, suf, keep = _sb_logits(qv, kv, scale, masked, t, upper)
            tail = suf + run
            a = jnp.exp(lb + tail)
            if masked:
                a = jnp.where(keep, a, 0.0)
            ab = a.astype(BF16)
            gr = ab.astype(F32) * _dot_nt(dov, vv)
            ghi = gr.astype(BF16)
            grem = (gr - ghi.astype(F32)).astype(BF16)
            before = totals[c] - g - (_dot(ghi, upper_incl) + _dot(grem, upper_incl))
            before = jnp.where(tail < SB_ZERO_LOG, 0.0, before)
            r = 1.0 / (1.0 + e)
            pos = z >= 0.0
            dz = r * (gr * jnp.where(pos, e, 1.0) - before * jnp.where(pos, 1.0, e))
            if masked:
                dz = jnp.where(keep, dz, 0.0)
            dzb = (dz * scale).astype(BF16)
            dk_ref[sl, :] += _dot_tn(dzb, qv)
            dv_ref[sl, :] += _dot_tn(ab, dov)
            return (run + jnp.sum(lo, axis=1, keepdims=True), g + jnp.sum(gr, axis=1, keepdims=True),
                    dq + _dot(dzb, kv))

        zero = jnp.zeros((t, 1), F32)
        init = (zero, zero, jnp.zeros((t, LANE), F32))
        carries = _sb_walk(i, lambda c: step(c, 0, init, True), lambda c, jj, cr: step(c, jj, cr, False))
        for c in range(n):
            dq_ref[rows[c], :] = carries[c][2]
        finish()

    blk = pl.BlockSpec((n * t, LANE), lambda h, i: (i, h))
    res = pl.BlockSpec((s, LANE), lambda h, i: (0, h))
    full = jax.ShapeDtypeStruct((s, heads * LANE), F32)
    outs = pl.pallas_call(
        body, name="sb_bwd", grid=(heads, nq // n),
        in_specs=[pl.BlockSpec((n * t, LANE), lambda h, i: (i, C_SBQ // LANE + h)),
                  pl.BlockSpec((s, LANE), lambda h, i: (0, C_SBK // LANE + h)),
                  pl.BlockSpec((s, LANE), lambda h, i: (0, C_SBV // LANE + h)), blk, blk] + [ANY] * nx,
        out_specs=[blk, res, res] + [ANY] * nx, out_shape=[full, full, full] + _exchange_shapes(stacked),
        scratch_shapes=_dma_sems(nx) if nx else [],
        compiler_params=_attn_params(s, n * t, 2, 2, ride=nx > 0),
    )(proj, proj, proj, o, do, *stacked)
    return outs[0], outs[1], outs[2], list(outs[3:])


def _xattn_probs(qh, kh):
    sc = _dot_nt(qh, kh) * (1.0 / math.sqrt(X_HEAD_DIM))
    p = jnp.exp(sc - jnp.max(sc, axis=1, keepdims=True))
    return p / jnp.sum(p, axis=1, keepdims=True)


def _xattn_fwd(xq, xkv):
    rows = xq.shape[0]
    w = X_HEADS * X_HEAD_DIM

    def body(i, ins, fulls, outs, accs):
        for h in range(X_HEADS):
            sl = slice(h * LANE, (h + 1) * LANE)
            p = _xattn_probs(ins[0][:, sl], fulls[0][:, sl])
            outs[0][:, sl] = _dot(p.astype(BF16), fulls[0][:, w + h * LANE:w + (h + 1) * LANE]).astype(BF16)

    return _rowwise(body, name="xattn_fwd", rows=rows, tr=512, row_ins=[(xq, w, 0)], full_ins=[xkv],
                    row_outs=[(w, BF16)])[0]


def _xattn_bwd(xq, xkv, dxo):
    rows = xq.shape[0]
    w = X_HEADS * X_HEAD_DIM

    def body(i, ins, fulls, outs, accs):
        _acc_init(i, accs)
        for h in range(X_HEADS):
            sl = slice(h * LANE, (h + 1) * LANE)
            slv = slice(w + h * LANE, w + (h + 1) * LANE)
            qh, kh, vh, doh = ins[0][:, sl], fulls[0][:, sl], fulls[0][:, slv], ins[1][:, sl]
            p = _xattn_probs(qh, kh)
            dp = _dot_nt(doh, vh)
            ds = (p * (dp - jnp.sum(p * dp, axis=1, keepdims=True)) * (1.0 / math.sqrt(X_HEAD_DIM))).astype(BF16)
            outs[0][:, sl] = _dot(ds, kh).astype(BF16)
            accs[0][:, sl] += _dot_tn(ds, qh)
            accs[0][:, slv] += _dot_tn(p.astype(BF16), doh)

    return _rowwise(body, name="xattn_bwd", rows=rows, tr=512, row_ins=[(xq, w, 0), (dxo, w, 0)], full_ins=[xkv],
                    row_outs=[(w, BF16)], acc_outs=[(xkv.shape, F32)])


def _gate_fwd(proj, pa, pb, b_gate):
    rows = proj.shape[0]

    def body(i, ins, fulls, outs, accs):
        sa = _sigmoid(ins[0][...].astype(F32) + fulls[0][0:1, :])
        sb = _sigmoid(ins[1][...].astype(F32) + fulls[0][1:2, :])
        outs[0][...] = (sa * ins[2][...] + sb * ins[3][...]).astype(BF16)

    return _rowwise(body, name="gate_fwd", rows=rows, tr=512,
                    row_ins=[(proj, D_MODEL, C_GA // D_MODEL), (proj, D_MODEL, C_GB // D_MODEL), (pa, D_MODEL, 0),
                             (pb, D_MODEL, 0)],
                    full_ins=[b_gate], row_outs=[(D_MODEL, BF16)])[0]


def _gate_bwd(proj, pa, pb, b_gate, dm):
    rows = proj.shape[0]

    def body(i, ins, fulls, outs, accs):
        _acc_init(i, accs)
        d = ins[4][...]
        sa = _sigmoid(ins[0][...].astype(F32) + fulls[0][0:1, :])
        sb = _sigmoid(ins[1][...].astype(F32) + fulls[0][1:2, :])
        dga = d * ins[2][...] * sa * (1.0 - sa)
        dgb = d * ins[3][...] * sb * (1.0 - sb)
        outs[0][...] = (d * sa).astype(BF16)
        outs[1][...] = (d * sb).astype(BF16)
        outs[2][...] = dga.astype(BF16)
        outs[3][...] = dgb.astype(BF16)
        accs[0][0:1, :] += jnp.sum(dga, axis=0, keepdims=True)
        accs[0][1:2, :] += jnp.sum(dgb, axis=0, keepdims=True)

    return _rowwise(body, name="gate_bwd", rows=rows, tr=512,
                    row_ins=[(proj, D_MODEL, C_GA // D_MODEL), (proj, D_MODEL, C_GB // D_MODEL), (pa, D_MODEL, 0),
                             (pb, D_MODEL, 0), (dm, D_MODEL, 0)],
                    full_ins=[b_gate], row_outs=[(D_MODEL, BF16)] * 4, acc_outs=[((2, D_MODEL), F32)])


def _loss_head(x3, target, g_final):
    rows = x3.shape[0]

    def body(i, ins, fulls, outs, accs):
        _acc_init(i, accs)
        xv, g = ins[0][...], fulls[0][...]
        d = _rms(xv, g) - ins[1][...]
        dx, dg = _rms_bwd(xv, g, d * (1.0 / D_MODEL))
        outs[0][...] = dx
        accs[0][...] += dg
        accs[1][...] += jnp.sum(d * d, axis=0, keepdims=True)

    return _rowwise(body, name="loss_head", rows=rows, tr=512, row_ins=[(x3, D_MODEL, 0), (target, D_MODEL, 0)],
                    full_ins=[g_final], row_outs=[(D_MODEL, F32)], acc_outs=[((1, D_MODEL), F32), ((1, D_MODEL), F32)])


def _adamw(w, g, m, v, name):
    rows, c = w.shape

    def body(i, ins, fulls, outs, accs):
        wv, gv = ins[0][...], ins[1][...]
        mn = ADAM_B1 * ins[2][...] + (1.0 - ADAM_B1) * gv
        vn = ADAM_B2 * ins[3][...] + (1.0 - ADAM_B2) * jnp.square(gv)
        m_hat = mn / (1.0 - ADAM_B1 ** ADAM_STEP)
        v_hat = vn / (1.0 - ADAM_B2 ** ADAM_STEP)
        outs[0][...] = -ADAM_LR * (m_hat / (jnp.sqrt(v_hat) + ADAM_EPS) + ADAM_WD * wv)
        outs[1][...] = mn
        outs[2][...] = vn

    return _rowwise(body, name=name, rows=rows, tr=_row_tile(rows, 256), row_ins=[(a, c, 0) for a in (w, g, m, v)],
                    row_outs=[(c, F32)] * 3)


def _place():
    x, y, c = lax.axis_index("x"), lax.axis_index("y"), lax.axis_index("c")
    chips = [(1 - x, y), (x, 1 - y), (1 - x, 1 - y)]
    return x, y, c, chips


ANY = pl.BlockSpec(memory_space=pl.ANY)


def _remote(src, dst, send_sem, recv_sem, to):
    return pltpu.make_async_remote_copy(src_ref=src, dst_ref=dst, send_sem=send_sem, recv_sem=recv_sem,
                                        device_id=to, device_id_type=MESH)


def _dma_sems(n):
    return [pltpu.SemaphoreType.DMA((n,)), pltpu.SemaphoreType.DMA((n,))]


class _GatherPhases:
    def __init__(self, ins, outs, send_sems, recv_sems):
        self.ins, self.outs, self.send_sems, self.recv_sems = ins, outs, send_sems, recv_sems
        self.x, self.y, self.c, self.chips = _place()
        self.me = 2 * self.x + self.y

    def _copy(self, t, j, chip_idx, hlf, to, src=None):
        h = self.ins[t].shape[0] // 2
        dst = self.outs[t].at[chip_idx, pl.ds(hlf * h, h), :]
        return _remote(dst if src is None else src, dst, self.send_sems.at[6 * t + j], self.recv_sems.at[6 * t + j], to)

    def _sends(self):
        out = []
        for t in range(len(self.ins)):
            h = self.ins[t].shape[0] // 2
            for j, chip in enumerate(self.chips):
                out.append(self._copy(t, j, self.me, self.c, (*chip, self.c), src=self.ins[t].at[pl.ds(self.c * h, h), :]))
        return out

    def _forwards(self):
        return [self._copy(t, 3 + j, 2 * chip[0] + chip[1], self.c, (self.x, self.y, 1 - self.c))
                for t in range(len(self.ins)) for j, chip in enumerate(self.chips)]

    def send(self):
        for cp in self._sends():
            cp.start()

    def forward(self):
        here = (self.x, self.y, self.c)
        landed = [self._copy(t, j, 2 * chip[0] + chip[1], self.c, here)
                  for t in range(len(self.ins)) for j, chip in enumerate(self.chips)]
        for arrival, fwd in zip(landed, self._forwards()):
            arrival.wait_recv()
            fwd.start()

    def finish(self):
        here = (self.x, self.y, self.c)
        for t in range(len(self.ins)):
            for j, chip in enumerate(self.chips):
                self._copy(t, 3 + j, 2 * chip[0] + chip[1], 1 - self.c, here).wait_recv()
        for cp in self._sends() + self._forwards():
            cp.wait_send()


def _all_gather_weights(shards):
    n = len(shards)

    def body(*refs):
        gather = _GatherPhases(refs[:n], refs[n:2 * n], *refs[2 * n:])
        gather.send()
        gather.forward()
        gather.finish()

    return pl.pallas_call(
        body, name="all_gather_weights", in_specs=[ANY] * n, out_specs=[ANY] * n,
        out_shape=[jax.ShapeDtypeStruct((N_CHIPS,) + s.shape, s.dtype) for s in shards],
        scratch_shapes=_dma_sems(6 * n),
    )(*shards)


def _exchange_copies(ins, outs, send_sems, recv_sems):
    x, y, c, _ = _place()
    cps = []
    for t in range(len(ins)):
        h = ins[t].shape[1] // 2
        cps.append(_remote(ins[t].at[:, pl.ds((1 - c) * h, h), :], outs[t], send_sems.at[t], recv_sems.at[t],
                           (x, y, 1 - c)))
    return cps


def _exchange_shapes(stacked):
    return [jax.ShapeDtypeStruct((N_CHIPS, s.shape[1] // 2, s.shape[2]), s.dtype) for s in stacked]


def _scatter_copies(ins, outs, send_sems, recv_sems):
    x, y, c, chips = _place()
    return [_remote(ins[t].at[2 * chip[0] + chip[1]], outs[t].at[j], send_sems.at[3 * t + j], recv_sems.at[3 * t + j],
                    (*chip, c)) for t in range(len(ins)) for j, chip in enumerate(chips)]


def _scatter_shapes(parts):
    return [jax.ShapeDtypeStruct((N_CHIPS - 1,) + s.shape[1:], s.dtype) for s in parts]


def _ride_along(copies_of, n, refs, first, last):
    if not n:
        return lambda: None

    def start():
        for cp in copies_of(*refs):
            cp.start()

    def wait():
        for cp in copies_of(*refs):
            cp.wait()

    pl.when(first)(start)
    return lambda: pl.when(last)(wait)


def _copy_call(copies_of, arrs, out_shape, sems_per, name):
    n = len(arrs)

    def body(*refs):
        cps = copies_of(refs[:n], refs[n:2 * n], *refs[2 * n:])
        for cp in cps:
            cp.start()
        for cp in cps:
            cp.wait()

    return pl.pallas_call(body, name=name, in_specs=[ANY] * n, out_specs=[ANY] * n, out_shape=out_shape,
                          scratch_shapes=_dma_sems(sems_per * n))(*arrs)


def _pair_exchange_grads(stacked, name):
    return _copy_call(_exchange_copies, stacked, _exchange_shapes(stacked), 1, name)


def _chip_scatter(parts, name):
    return _copy_call(_scatter_copies, parts, _scatter_shapes(parts), 3, name)


def _pair_exchange_halves(shards):
    n = len(shards)

    def body(*refs):
        bufs = refs[n:2 * n]
        send_sems, recv_sems = refs[2 * n:]
        x, y, c, _ = _place()
        cps = []
        for t in range(n):
            h = bufs[t].shape[0] // 2
            rows = bufs[t].at[pl.ds(c * h, h), :]
            cps.append(_remote(rows, rows, send_sems.at[t], recv_sems.at[t], (x, y, 1 - c)))
            cps[-1].start()
        for cp in cps:
            cp.wait()

    return pl.pallas_call(
        body, name="pair_exchange_halves", in_specs=[ANY] * n, out_specs=[ANY] * n,
        out_shape=[jax.ShapeDtypeStruct(s.shape, s.dtype) for s in shards],
        input_output_aliases={t: t for t in range(n)},
        scratch_shapes=_dma_sems(n),
    )(*shards)


def _pair_sum(gs, recv, place, name):
    _, r, cols = gs.shape
    h = r // 2

    def kern(p_ref, a_ref, b_ref, o_ref):
        o_ref[...] = (a_ref[...] + b_ref[...]).astype(BF16)

    blk = lambda f: pl.BlockSpec((1, h, cols), f)
    return pl.pallas_call(
        kern, name=name,
        grid_spec=pltpu.PrefetchScalarGridSpec(
            num_scalar_prefetch=1, grid=(N_CHIPS,),
            in_specs=[blk(lambda d, p: (d, p[1], 0)), blk(lambda d, p: (d, 0, 0))],
            out_specs=blk(lambda d, p: (d, 0, 0))),
        out_shape=jax.ShapeDtypeStruct((N_CHIPS, h, cols), BF16),
        compiler_params=pltpu.CompilerParams(dimension_semantics=("arbitrary",),
                                             vmem_limit_bytes=_vmem_limit(3 * _nbytes((h, cols), F32), 0)),
    )(place, gs, recv)


def _chip_sum(gs, recv, got, place, name):
    _, r, cols = gs.shape
    h = r // 2

    def kern(p_ref, a_ref, b_ref, g0, g1, g2, o_ref):
        own = a_ref[0] + b_ref[0]
        o_ref[...] = ((own + g0[0].astype(F32)) + g1[0].astype(F32)) + g2[0].astype(F32)

    blk = lambda f: pl.BlockSpec((1, h, cols), f)
    return pl.pallas_call(
        kern, name=name,
        grid_spec=pltpu.PrefetchScalarGridSpec(
            num_scalar_prefetch=1, grid=(1,),
            in_specs=[blk(lambda i, p: (p[0], p[1], 0)), blk(lambda i, p: (p[0], 0, 0)), blk(lambda i, p: (0, 0, 0)),
                      blk(lambda i, p: (1, 0, 0)), blk(lambda i, p: (2, 0, 0))],
            out_specs=pl.BlockSpec((h, cols), lambda i, p: (p[1], 0))),
        out_shape=jax.ShapeDtypeStruct((r, cols), F32),
        compiler_params=pltpu.CompilerParams(dimension_semantics=("arbitrary",),
                                             vmem_limit_bytes=_vmem_limit(5 * _nbytes((h, cols), F32), 0)),
    )(place, gs, recv, got, got, got)


def _all_reduce_small(vec, name):
    r, cols = vec.shape

    def body(in_ref, out_ref, gath, send_sems, recv_sems):
        x, y, c, _ = _place()
        me = 4 * x + 2 * y + c
        gath[me] = in_ref[...]
        sends = []
        for k in range(1, 8):
            to = (x ^ (k >> 2), y ^ ((k >> 1) & 1), c ^ (k & 1))
            cp = pltpu.make_async_remote_copy(src_ref=in_ref, dst_ref=gath.at[me], send_sem=send_sems.at[k - 1],
                                              recv_sem=recv_sems.at[k - 1], device_id=to, device_id_type=MESH)
            cp.start()
            sends.append(cp)
        for k in range(1, 8):
            peer = me ^ k
            pltpu.make_async_remote_copy(src_ref=in_ref, dst_ref=gath.at[peer], send_sem=send_sems.at[k - 1],
                                         recv_sem=recv_sems.at[k - 1], device_id=(x, y, c),
                                         device_id_type=MESH).wait_recv()
        for cp in sends:
            cp.wait_send()
        acc = gath[0]
        for d in range(1, 8):
            acc = acc + gath[d]
        out_ref[...] = acc

    vm = pl.BlockSpec(memory_space=pltpu.VMEM)
    return pl.pallas_call(
        body, name=name, in_specs=[vm], out_specs=vm,
        out_shape=jax.ShapeDtypeStruct((r, cols), F32),
        scratch_shapes=[pltpu.VMEM((8, r, cols), F32), pltpu.SemaphoreType.DMA((7,)), pltpu.SemaphoreType.DMA((7,))],
    )(vec)


def _pad_heads(w, heads, dim, axis):
    shp = w.shape[:axis] + (heads, dim) + w.shape[axis + 1:]
    pad = [(0, 0)] * len(shp)
    pad[axis + 1] = (0, LANE - dim)
    w = jnp.pad(w.reshape(shp), pad)
    return w.reshape(w.shape[:axis] + (heads * LANE,) + w.shape[axis + 2:])


def _unpad_heads(w, heads, dim, axis):
    shp = w.shape[:axis] + (heads, LANE) + w.shape[axis + 1:]
    w = lax.slice_in_dim(w.reshape(shp), 0, dim, axis=axis + 1)
    return w.reshape(w.shape[:axis] + (heads * dim,) + w.shape[axis + 2:])


def _w_in_layout(w_in):
    kr = jnp.pad(w_in[:, 384:416], ((0, 0), (ROPE_LO, LANE - ROPE_LO - MLA_ROPE)))
    sb = lambda lo: _pad_heads(w_in[:, lo:lo + 512], SB_HEADS, SB_HEAD_DIM, 1)
    return jnp.concatenate([w_in[:, 1952:2976], w_in[:, 2976:4000], sb(416), sb(928), sb(1440), w_in[:, 0:256],
                            w_in[:, 256:384], kr], axis=1)


def _w_in_unlayout(d):
    sb = lambda lo: _unpad_heads(d[:, lo:lo + 1024], SB_HEADS, SB_HEAD_DIM, 1)
    return jnp.concatenate([d[:, C_CQ:C_CQ + 256], d[:, C_CKV:C_CKV + 128], d[:, C_KR + ROPE_LO:C_KR + ROPE_LO + MLA_ROPE],
                            sb(C_SBQ), sb(C_SBK), sb(C_SBV), d[:, C_GA:C_GA + 1024], d[:, C_GB:C_GB + 1024]], axis=1)


def _w_ukv_layout(w):
    w3 = w.reshape(MLA_KV_RANK, MLA_HEADS, MLA_NOPE + MLA_V)
    pad = lambda part: jnp.pad(part, ((0, 0), (0, 0), (0, LANE - part.shape[2]))).reshape(MLA_KV_RANK, MLA_HEADS * LANE)
    return jnp.concatenate([pad(w3[:, :, :MLA_NOPE]), pad(w3[:, :, MLA_NOPE:])], axis=1)


def _w_ukv_unlayout(d):
    hw = MLA_HEADS * LANE
    kpart = d[:, :hw].reshape(MLA_KV_RANK, MLA_HEADS, LANE)[:, :, :MLA_NOPE]
    vpart = d[:, hw:].reshape(MLA_KV_RANK, MLA_HEADS, LANE)[:, :, :MLA_V]
    return jnp.concatenate([kpart, vpart], axis=2).reshape(MLA_KV_RANK, MLA_HEADS * (MLA_NOPE + MLA_V))


def _shard_of(full, d, axis):
    n = full.shape[axis] // N_CHIPS
    return lax.slice_in_dim(full, d * n, (d + 1) * n, axis=axis)


def _local_step(x, mem, pos, target, w, t_mla, t_sb, late=None, reduce=None):
    s = x.shape[0]
    w = dict(w)
    win = _w_in_layout(w["w_in"])
    wuq = _pad_heads(w["w_uq"], MLA_HEADS, MLA_NOPE + MLA_ROPE, 1)
    wkv = _w_ukv_layout(w["w_ukv"])
    inv_freq = ROPE_THETA ** (-jnp.arange(0, MLA_ROPE, 2, dtype=F32) / MLA_ROPE)
    freq_lane = jnp.pad(jnp.concatenate([inv_freq, inv_freq]), (ROPE_LO, LANE - ROPE_LO - MLA_ROPE)).reshape(1, LANE)
    add = lambda accs, ex: (accs[0] + ex[0],)

    def add_norm(accs, ex):
        y = accs[0] + ex[0]
        return y, _rms(y, ex[1])

    tab = _rope_tables(pos.reshape(s, 1), freq_lane)
    h = _rms_fwd_call(x, w["g_mix"], "rms_mix")
    proj = _mm(h, [win], name="proj_in", tn=1408, out_dtypes=(BF16,))
    cqn, ckvn, krope = _mla_prep_fwd(proj, tab, w["g_q_lat"], w["g_kv_lat"])
    hw = MLA_HEADS * LANE
    qa = _mm(cqn, [wuq], name="q_up", row_extras=(tab,), out_dtypes=(BF16,),
             epilogue=lambda accs, ex: (_per_head(lambda t: _rope(t, ex[0]), accs[0]),))
    ka = _mm(ckvn, [wkv[:, :hw]], name="k_up", row_extras=(krope,), out_dtypes=(BF16,),
             epilogue=lambda accs, ex: (_per_head(lambda t: t + ex[0], accs[0]),))
    va = _mm(ckvn, [wkv[:, hw:]], name="v_up", out_dtypes=(BF16,))
    o_a, lse, gathered = _mla_fwd(qa, ka, va, t_mla[0], t_mla[1], late[0] if late else ())
    if late:
        w.update(late[1](gathered))
    wa = _pad_heads(w["w_a_proj"], MLA_HEADS, MLA_V, 0)
    wb = _pad_heads(w["w_b_proj"], SB_HEADS, SB_HEAD_DIM, 0)
    o_b = _sb_fwd(proj, t_sb)
    pa = _mm(o_a, [wa], name="proj_a")
    pb = _mm(o_b, [wb], name="proj_b")
    merged = _gate_fwd(proj, pa, pb, w["b_gate"])
    x1, hx = _mm(merged, [w["w_o"]], name="proj_o", extras=(x,), consts=(w["g_x"],), epilogue=add_norm,
                 out_dtypes=(F32, BF16))
    mn = _rms_fwd_call(mem, w["g_mem"], "rms_mem")
    xq = _mm(hx, [w["w_xq"]], name="xq", out_dtypes=(BF16,))
    xkv = _mm(mn, [w["w_xkv"]], name="xkv", out_dtypes=(BF16,))
    xo = _xattn_fwd(xq, xkv)
    x2, hf = _mm(xo, [w["w_xo"]], name="proj_xo", extras=(x1,), consts=(w["g_ffn"],), epilogue=add_norm,
                 out_dtypes=(F32, BF16))

    def swiglu(accs, ex):
        a, b = accs
        return a, b, a * _sigmoid(a) * b

    ga, gu, hmid = _mm(hf, [w["w_gate"], w["w_up"]], name="ffn_up", epilogue=swiglu, out_dtypes=(BF16, BF16, BF16),
                       tm=512, tn=1408)
    x3 = _mm(hmid, [w["w_down"]], name="ffn_down", extras=(x2,), epilogue=add, tk=2816)

    dx3, dg_final, sq = _loss_head(x3, target, w["g_final"].reshape(1, D_MODEL))
    g = {"g_final": dg_final.reshape(D_MODEL)}

    def swiglu_bwd(accs, ex):
        dh, a, b = accs[0], ex[0].astype(F32), ex[1].astype(F32)
        sg = _sigmoid(a)
        return dh * b * sg * (1.0 + a * (1.0 - sg)), dh * a * sg

    da, db = _mm(dx3, [w["w_down"]], name="ffn_down_dx", tb=True, extras=(ga, gu), epilogue=swiglu_bwd,
                 out_dtypes=(BF16, BF16), tm=512, tn=1408, chunk=MM_CHUNK)
    g["w_down"] = _mm(hmid, [dx3], name="ffn_down_dw", ta=True, tm=1408)
    g["w_gate"] = _mm(hf, [da], name="ffn_gate_dw", ta=True, tn=1408)
    g["w_up"] = _mm(hf, [db], name="ffn_up_dw", ta=True, tn=1408)
    dhf = _mm(da, [w["w_gate"]], name="ffn_gate_dx", tb=True, tk=2816)
    dhf = _mm(db, [w["w_up"]], name="ffn_up_dx", tb=True, extras=(dhf,), epilogue=add, tk=2816,
              out_dtypes=(BF16,))
    dx2, g["g_ffn"] = _rms_bwd_call(x2, w["g_ffn"], dhf, dx3, "rms_ffn_bwd")

    dxo = _mm(dx2, [w["w_xo"]], name="proj_xo_dx", tb=True, out_dtypes=(BF16,))
    g["w_xo"] = _mm(xo, [dx2], name="proj_xo_dw", ta=True)
    dxq, dxkv = _xattn_bwd(xq, xkv, dxo)
    dhx = _mm(dxq, [w["w_xq"]], name="xq_dx", tb=True, out_dtypes=(BF16,))
    g["w_xq"] = _mm(hx, [dxq], name="xq_dw", ta=True)
    dmn = _mm(dxkv, [w["w_xkv"]], name="xkv_dx", tb=True)
    g["w_xkv"] = _mm(mn, [dxkv], name="xkv_dw", ta=True)
    dx1, g["g_x"] = _rms_bwd_call(x1, w["g_x"], dhx, dx2, "rms_x_bwd")
    _, g["g_mem"] = _rms_bwd_call(mem, w["g_mem"], dmn, None, "rms_mem_bwd")

    dmerged = _mm(dx1, [w["w_o"]], name="proj_o_dx", tb=True)
    g["w_o"] = _mm(merged, [dx1], name="proj_o_dw", ta=True)
    dpa, dpb, dga, dgb, g["b_gate"] = _gate_bwd(proj, pa, pb, w["b_gate"], dmerged)
    do_a = _mm(dpa, [wa], name="proj_a_dx", tb=True, out_dtypes=(BF16,))
    do_b = _mm(dpb, [wb], name="proj_b_dx", tb=True, out_dtypes=(BF16,))
    g["w_a_proj"] = _unpad_heads(_mm(o_a, [dpa], name="proj_a_dw", ta=True), MLA_HEADS, MLA_V, 0)
    g["w_b_proj"] = _unpad_heads(_mm(o_b, [dpb], name="proj_b_dw", ta=True), SB_HEADS, SB_HEAD_DIM, 0)

    stacked = [reduce[1](n, g[n]) for n in reduce[0]] if reduce else []
    dsq, dsk, dsv, recv = _sb_bwd(proj, o_b, do_b, t_sb, stacked)
    parts = [reduce[2](n, gs, rv) for n, gs, rv in zip(reduce[0], stacked, recv)] if reduce else []
    dqa, dka, dva, got = _mla_bwd(qa, ka, va, o_a, do_a, lse, t_mla[0], t_mla[2], parts)
    riding = dict(zip(reduce[0], zip(stacked, recv, got))) if reduce else {}
    dqp, dkvp, dkr = _mla_rope_bwd(dqa, dka, dva, tab)
    g["w_uq"] = _unpad_heads(_mm(cqn, [dqp], name="q_up_dw", ta=True), MLA_HEADS, MLA_NOPE + MLA_ROPE, 1)
    g["w_ukv"] = _w_ukv_unlayout(_mm(ckvn, [dkvp], name="kv_up_dw", ta=True))
    dcqn = _mm(dqp, [wuq], name="q_up_dx", tb=True)
    dckvn = _mm(dkvp, [wkv], name="kv_up_dx", tb=True)
    dcq, dckv, g["g_q_lat"], g["g_kv_lat"] = _mla_prep_bwd(proj, w["g_q_lat"], w["g_kv_lat"], dcqn, dckvn)

    dproj = jnp.concatenate([dga, dgb, dsq.astype(BF16), dsk.astype(BF16), dsv.astype(BF16), dcq, dckv, dkr], axis=1)
    g["w_in"] = _w_in_unlayout(_mm(h, [dproj], name="proj_in_dw", ta=True, tn=1408))
    dh = _mm(dproj, [win], name="proj_in_dx", tb=True, tk=2816, out_dtypes=(BF16,))
    grad_x, g["g_mix"] = _rms_bwd_call(x, w["g_mix"], dh, dx1, "rms_mix_bwd")
    return sq, grad_x, g, riding


def _small_pack(d):
    row5 = jnp.concatenate([d["g_q_lat"].reshape(-1), d["g_kv_lat"].reshape(-1), jnp.zeros((640,), F32)])
    rows = [d[n].reshape(-1) for n in ("g_mix", "g_x", "g_mem", "g_ffn", "g_final")] + [row5]
    return rows


def _small_unpack(p, like):
    out = {n: p[i].reshape(like[n].shape) for i, n in enumerate(("g_mix", "g_x", "g_mem", "g_ffn", "g_final"))}
    out["g_q_lat"] = p[5, 0:256].reshape(like["g_q_lat"].shape)
    out["g_kv_lat"] = p[5, 256:384].reshape(like["g_kv_lat"].shape)
    return out


def kernel(x, mem, positions, g_mix, w_in, b_gate, g_q_lat, w_uq, g_kv_lat, w_ukv, w_a_proj, w_b_proj, w_o, g_x, g_mem, w_xq, w_xkv, w_xo, g_ffn, w_gate, w_up, w_down, g_final, loss_target, m_g_mix, m_w_in, m_b_gate, m_g_q_lat, m_w_uq, m_g_kv_lat, m_w_ukv, m_w_a_proj, m_w_b_proj, m_w_o, m_g_x, m_g_mem, m_w_xq, m_w_xkv, m_w_xo, m_g_ffn, m_w_gate, m_w_up, m_w_down, m_g_final, v_g_mix, v_w_in, v_b_gate, v_g_q_lat, v_w_uq, v_g_kv_lat, v_w_ukv, v_w_a_proj, v_w_b_proj, v_w_o, v_g_x, v_g_mem, v_w_xq, v_w_xkv, v_w_xo, v_g_ffn, v_w_gate, v_w_up, v_w_down, v_g_final):
    given = dict(locals())
    names = [n for n, _, _ in MATS] + ["b_gate"] + list(SMALL)
    wts = {n: given[n] for n in names}
    mom = {n: given["m_" + n] for n in names}
    var = {n: given["v_" + n] for n in names}
    shard2d = {n: shp for n, shp, _ in MATS}
    shard2d["b_gate"] = B_GATE_SHARD
    cx, cy, cc = lax.axis_index("x"), lax.axis_index("y"), lax.axis_index("c")
    me = 2 * cx + cy
    place = jnp.stack([me, cc]).astype(jnp.int32)
    bcol = me * B_GATE_SHARD[1]

    own = [wts[n].reshape(shard2d[n]).astype(BF16) for n, _, _ in MATS]
    bias = (("b_gate", (BIAS_ROWS, B_GATE_SHARD[1]), 1),)
    own_bias = [jnp.pad(wts["b_gate"].reshape(B_GATE_SHARD), ((0, BIAS_ROWS - B_GATE_SHARD[0]), (0, 0)))]

    def assemble(mats, gathered, mine):
        out = {}
        for (n, shp, ax), g4, shard in zip(mats, gathered, mine):
            g4 = lax.dynamic_update_slice(g4, shard[None], (me, 0, 0))
            out[n] = g4.reshape(N_CHIPS * shp[0], shp[1]) if ax == 0 else jnp.concatenate(list(g4), axis=1)
        return out

    first = own[:N_EARLY] + own_bias
    full = assemble(MATS[:N_EARLY] + bias, _all_gather_weights(first), first)
    full["b_gate"] = full["b_gate"][0:B_GATE_SHARD[0]]
    late = (own[N_EARLY:], lambda gathered: assemble(MATS[N_EARLY:], gathered, own[N_EARLY:]))
    for n in SMALL:
        full[n] = wts[n].reshape(1, -1) if n != "g_final" else wts[n]

    axis_of = {n: ax for n, _, ax in MATS}
    stack = lambda n, g: jnp.stack([_shard_of(g, d, axis_of[n]) for d in range(N_CHIPS)])
    pair_sum = lambda n, gs, rv: _pair_sum(gs, rv, place, "pair_sum_" + n)
    behind = [n for n, _, _ in MATS[N_EARLY:]]
    sq, grad_x, grads, riding = _local_step(x[0], mem[0], positions[0], loss_target[0], full, t_mla=(1024, 1024, 1024), t_sb=256,
                                            late=late, reduce=(behind, stack, pair_sum))

    last = [n for n, _, _ in MATS[:N_EARLY]]
    stacked = [stack(n, grads[n]) for n in last]
    recv = _pair_exchange_grads(stacked, "pair_exchange_grads")
    got = _chip_scatter([pair_sum(n, gs, rv) for n, gs, rv in zip(last, stacked, recv)], "chip_scatter")
    riding.update(zip(last, zip(stacked, recv, got)))
    halves = [_chip_sum(*riding[n], place, "chip_sum_" + n) for n, _, _ in MATS]
    g_shard = dict(zip([n for n, _, _ in MATS], _pair_exchange_halves(halves)))

    small_rows = _small_pack({n: grads[n] for n in SMALL}) + [sq.reshape(-1), grads["b_gate"][0], grads["b_gate"][1]]
    small_rows += [jnp.zeros((D_MODEL,), F32)] * (SMALL_ROWS - len(small_rows))
    small = _all_reduce_small(jnp.stack(small_rows), "all_reduce_small")
    loss = (0.5 / D_MODEL) * jnp.sum(small[6])
    g_shard["b_gate"] = lax.dynamic_slice(small[7:9], (0, bcol), B_GATE_SHARD)

    out = {"grad": {}, "delta": {}, "m": {}, "v": {}}
    for n in [n for n, _, _ in MATS] + ["b_gate"]:
        shape = wts[n].shape
        r2 = lambda a: a.reshape(shard2d[n])
        d_n, m_n, v_n = _adamw(r2(wts[n]), g_shard[n], r2(mom[n]), r2(var[n]), "adamw_" + n)
        for key, a in (("grad", g_shard[n]), ("delta", d_n), ("m", m_n), ("v", v_n)):
            out[key][n] = a.reshape(shape)
    sp = lambda d: jnp.stack(_small_pack(d) + [jnp.zeros((D_MODEL,), F32)] * 2)
    delta_s, m_s, v_s = _adamw(sp(wts), small[0:8].at[6:8].set(0.0), sp(mom), sp(var), "adamw_small")
    for key, p in (("grad", small), ("delta", delta_s), ("m", m_s), ("v", v_s)):
        out[key].update(_small_unpack(p, wts))

    order = ["g_mix", "w_in", "b_gate", "g_q_lat", "w_uq", "g_kv_lat", "w_ukv", "w_a_proj", "w_b_proj", "w_o", "g_x",
             "g_mem", "w_xq", "w_xkv", "w_xo", "g_ffn", "w_gate", "w_up", "w_down", "g_final"]
    return (loss, grad_x[None], *[out[key][n] for key in ("grad", "delta", "m", "v") for n in order])
```

```python
import functools
import math

import jax
import jax.numpy as jnp
from jax import lax
from jax.experimental import pallas as pl
from jax.experimental.pallas import tpu as pltpu

F32 = jnp.float32
BF16 = jnp.bfloat16
MESH = pl.DeviceIdType.MESH

D_MODEL = 1024
MLA_HEADS = 8
MLA_Q_RANK = 256
MLA_KV_RANK = 128
MLA_NOPE = 64
MLA_ROPE = 32
MLA_V = 64
ROPE_THETA = 10000.0
SB_HEADS = 8
SB_HEAD_DIM = 64
X_HEADS = 4
X_HEAD_DIM = 128
EPS = 1e-6
ADAM_LR = 0.001
ADAM_B1 = 0.9
ADAM_B2 = 0.999
ADAM_EPS = 1e-08
ADAM_WD = 0.01
ADAM_STEP = 10

LANE = 128
LOG2E = 1.4426950408889634
MM_CHUNK = 256
N_CHIPS = 4
VMEM_BYTES = 64 * 1024 * 1024

C_GA, C_GB, C_SBQ, C_SBK, C_SBV, C_CQ, C_CKV, C_KR = 0, 1024, 2048, 3072, 4096, 5120, 5376, 5504
ROPE_LO = MLA_NOPE
HALF = MLA_ROPE // 2

SB_ZERO_LOG = -104.0

MATS = (
    ("w_in", (1024, 1000), 1), ("w_uq", (256, 192), 1), ("w_ukv", (128, 256), 1), ("w_a_proj", (512, 256), 1),
    ("w_b_proj", (512, 256), 1), ("w_o", (256, 1024), 0), ("w_xq", (256, 512), 0), ("w_xkv", (256, 1024), 0),
    ("w_xo", (512, 256), 1), ("w_gate", (1024, 704), 1), ("w_up", (1024, 704), 1), ("w_down", (704, 1024), 0),
)
N_EARLY = 3
B_GATE_SHARD = (2, 256)
BIAS_ROWS = 16
SMALL = ("g_mix", "g_x", "g_mem", "g_ffn", "g_final", "g_q_lat", "g_kv_lat")
SMALL_ROWS = 16


def _vmem_limit(block_bytes, temp_bytes):
    est = 2 * block_bytes + temp_bytes + (4 << 20)
    return int(min(max(est, 16 << 20), VMEM_BYTES - (6 << 20)))


def _nbytes(shape, dtype):
    return math.prod(shape) * jnp.dtype(dtype).itemsize


def _row_tile(rows, cap):
    if rows <= cap:
        return rows
    return max(t for t in range(8, cap + 1, 8) if rows % t == 0)


def _tile(n, cap):
    if n <= cap:
        return n
    best = None
    for t in range(LANE, cap + 1, LANE):
        if n % t == 0:
            best = t
    assert best is not None, (n, cap)
    return best


def _mm(a, bs, *, name, ta=False, tb=False, extras=(), row_extras=(), consts=(), epilogue=None, out_dtypes=(F32,),
        tm=1024, tn=1024, tk=1024, chunk=None):
    bs = tuple(bs)
    m, k = (a.shape[1], a.shape[0]) if ta else a.shape
    n = bs[0].shape[0] if tb else bs[0].shape[1]
    tm, tn, tk = _tile(m, tm), _tile(n, tn), _tile(k, tk)
    assert m % tm == 0 and n % tn == 0 and k % tk == 0
    nk = k // tk
    nb, ne, no = len(bs), len(extras) + len(row_extras) + len(consts), len(out_dtypes)
    dims = (((0,) if ta else (1,)), ((1,) if tb else (0,))), ((), ())
    if epilogue is None:
        epilogue = lambda accs, ex: (accs[0],)

    def body(*refs):
        a_ref, b_refs, e_refs = refs[0], refs[1:1 + nb], refs[1 + nb:1 + nb + ne]
        o_refs, acc_refs = refs[1 + nb + ne:1 + nb + ne + no], refs[1 + nb + ne + no:]
        if nk == 1:
            ch = chunk or tm
            bvs = [b_ref[...].astype(BF16) for b_ref in b_refs]
            for r0 in range(0, tm, ch):
                rows = slice(r0, r0 + ch)
                av = (a_ref[:, rows] if ta else a_ref[rows, :]).astype(BF16)
                accs = [lax.dot_general(av, bv, dims, preferred_element_type=F32) for bv in bvs]
                ex = [e[rows, :] for e in e_refs[:ne - len(consts)]] + [e[...] for e in e_refs[ne - len(consts):]]
                for o_ref, v in zip(o_refs, epilogue(accs, ex)):
                    o_ref[rows, :] = v.astype(o_ref.dtype)
            return
        kk = pl.program_id(2)

        @pl.when(kk == 0)
        def _():
            for acc in acc_refs:
                acc[...] = jnp.zeros_like(acc)

        av = a_ref[...].astype(BF16)
        for b_ref, acc in zip(b_refs, acc_refs):
            acc[...] += lax.dot_general(av, b_ref[...].astype(BF16), dims, preferred_element_type=F32)

        @pl.when(kk == nk - 1)
        def _():
            outs = epilogue([acc[...] for acc in acc_refs], [e[...] for e in e_refs])
            for o_ref, v in zip(o_refs, outs):
                o_ref[...] = v.astype(o_ref.dtype)

    a_spec = pl.BlockSpec((tk, tm), lambda i, j, kk: (kk, i)) if ta else pl.BlockSpec((tm, tk), lambda i, j, kk: (i, kk))
    b_spec = pl.BlockSpec((tn, tk), lambda i, j, kk: (j, kk)) if tb else pl.BlockSpec((tk, tn), lambda i, j, kk: (kk, j))
    mn_spec = pl.BlockSpec((tm, tn), lambda i, j, kk: (i, j))
    blocks = (_nbytes((tm, tk), a.dtype) + sum(_nbytes((tk, tn), b.dtype) for b in bs)
              + sum(_nbytes((tm, tn), e.dtype) for e in extras) + sum(_nbytes((tm, tn), d) for d in out_dtypes)
              + sum(_nbytes((tm, e.shape[1]), e.dtype) for e in row_extras))
    temps = (nb + 4) * _nbytes((tm, tn), F32)
    outs = pl.pallas_call(
        body, name=name, grid=(m // tm, n // tn, nk),
        in_specs=[a_spec] + [b_spec] * nb + [mn_spec] * len(extras)
        + [pl.BlockSpec((tm, e.shape[1]), lambda i, j, kk: (i, 0)) for e in row_extras]
        + [pl.BlockSpec(e.shape, lambda i, j, kk: (0, 0)) for e in consts],
        out_specs=[mn_spec] * no,
        out_shape=[jax.ShapeDtypeStruct((m, n), d) for d in out_dtypes],
        scratch_shapes=[pltpu.VMEM((tm, tn), F32) for _ in range(nb if nk > 1 else 0)],
        compiler_params=pltpu.CompilerParams(
            dimension_semantics=("parallel", "parallel", "arbitrary"),
            vmem_limit_bytes=_vmem_limit(blocks, temps)),
    )(a, *bs, *extras, *row_extras, *consts)
    return outs[0] if no == 1 else outs


def _rowwise(body, *, name, rows, tr, row_ins, full_ins=(), row_outs=(), acc_outs=()):
    tr = min(tr, rows)
    assert rows % tr == 0
    n_ri, n_fi, n_ro = len(row_ins), len(full_ins), len(row_outs)

    def kern(*refs):
        body(pl.program_id(0), refs[:n_ri], refs[n_ri:n_ri + n_fi], refs[n_ri + n_fi:n_ri + n_fi + n_ro],
             refs[n_ri + n_fi + n_ro:])

    in_specs = [pl.BlockSpec((tr, w), functools.partial(lambda i, c: (i, c), c=ci)) for _, w, ci in row_ins]
    in_specs += [pl.BlockSpec(f.shape, lambda i: (0, 0)) for f in full_ins]
    out_specs = [pl.BlockSpec((tr, w), lambda i: (i, 0)) for w, _ in row_outs]
    out_specs += [pl.BlockSpec(s, lambda i: (0, 0)) for s, _ in acc_outs]
    out_shape = [jax.ShapeDtypeStruct((rows, w), d) for w, d in row_outs]
    out_shape += [jax.ShapeDtypeStruct(s, d) for s, d in acc_outs]
    blocks = (sum(_nbytes((tr, w), a.dtype) for a, w, _ in row_ins) + sum(_nbytes(f.shape, f.dtype) for f in full_ins)
              + sum(_nbytes((tr, w), d) for w, d in row_outs) + sum(_nbytes(s, d) for s, d in acc_outs))
    widest = max([w for _, w, _ in row_ins] + [w for w, _ in row_outs])
    outs = pl.pallas_call(
        kern, name=name, grid=(rows // tr,), in_specs=in_specs, out_specs=out_specs, out_shape=out_shape,
        compiler_params=pltpu.CompilerParams(
            dimension_semantics=("arbitrary",) if acc_outs else ("parallel",),
            vmem_limit_bytes=_vmem_limit(blocks, 8 * _nbytes((tr, widest), F32))),
    )(*[a for a, _, _ in row_ins], *full_ins)
    return outs


def _rms(x, g):
    r = lax.rsqrt(jnp.mean(x * x, axis=-1, keepdims=True) + EPS)
    return x * r * g


def _rms_bwd(x, g, dy):
    r = lax.rsqrt(jnp.mean(x * x, axis=-1, keepdims=True) + EPS)
    xh = x * r
    dxh = dy * g
    dx = r * (dxh - xh * jnp.mean(dxh * xh, axis=-1, keepdims=True))
    return dx, jnp.sum(dy * xh, axis=0, keepdims=True)


def _sigmoid(x):
    return 1.0 / (1.0 + jnp.exp(-x))


def _acc_init(i, refs):
    @pl.when(i == 0)
    def _():
        for r in refs:
            r[...] = jnp.zeros_like(r)


def _rms_fwd_call(x, g, name):
    rows, c = x.shape

    def body(i, ins, fulls, outs, accs):
        outs[0][...] = _rms(ins[0][...], fulls[0][...]).astype(BF16)

    return _rowwise(body, name=name, rows=rows, tr=512, row_ins=[(x, c, 0)], full_ins=[g], row_outs=[(c, BF16)])[0]


def _rms_bwd_call(x, g, dy, res, name):
    rows, c = x.shape
    row_ins = [(x, c, 0), (dy, c, 0)] + ([(res, c, 0)] if res is not None else [])

    def body(i, ins, fulls, outs, accs):
        _acc_init(i, accs)
        dx, dg = _rms_bwd(ins[0][...], fulls[0][...], ins[1][...].astype(F32))
        if res is not None:
            dx = dx + ins[2][...]
        outs[0][...] = dx
        accs[0][...] += dg

    return _rowwise(body, name=name, rows=rows, tr=512, row_ins=row_ins, full_ins=[g], row_outs=[(c, F32)],
                    acc_outs=[((1, c), F32)])


def _rope_tables(pos_col, freq_lane):
    rows = pos_col.shape[0]

    def body(i, ins, fulls, outs, accs):
        ang = ins[0][...].astype(F32) * fulls[0][...]
        lane = lax.broadcasted_iota(jnp.int32, ang.shape, 1)
        cos, sin = jnp.cos(ang), jnp.sin(ang)
        first = (lane >= ROPE_LO) & (lane < ROPE_LO + HALF)
        second = (lane >= ROPE_LO + HALF) & (lane < ROPE_LO + MLA_ROPE)
        outs[0][:, 0:LANE] = jnp.where(first | second, cos, 1.0)
        outs[0][:, LANE:2 * LANE] = jnp.where(first, -sin, 0.0)
        outs[0][:, 2 * LANE:3 * LANE] = jnp.where(second, sin, 0.0)

    return _rowwise(body, name="rope_tables", rows=rows, tr=1024, row_ins=[(pos_col, 1, 0)], full_ins=[freq_lane],
                    row_outs=[(3 * LANE, F32)])[0]


def _rope(x, tab):
    return (x * tab[:, 0:LANE] + pltpu.roll(x, LANE - HALF, 1) * tab[:, LANE:2 * LANE]
            + pltpu.roll(x, HALF, 1) * tab[:, 2 * LANE:3 * LANE])


def _rope_t(dy, tab):
    return (dy * tab[:, 0:LANE] + pltpu.roll(dy * tab[:, LANE:2 * LANE], HALF, 1)
            + pltpu.roll(dy * tab[:, 2 * LANE:3 * LANE], LANE - HALF, 1))


def _mla_prep_fwd(proj, tab, g_q, g_kv):
    rows = proj.shape[0]

    def body(i, ins, fulls, outs, accs):
        outs[0][...] = _rms(ins[0][...].astype(F32), fulls[0][...]).astype(BF16)
        outs[1][...] = _rms(ins[1][...].astype(F32), fulls[1][...]).astype(BF16)
        outs[2][...] = _rope(ins[2][...].astype(F32), ins[3][...])

    return _rowwise(body, name="mla_prep_fwd", rows=rows, tr=512,
                    row_ins=[(proj, MLA_Q_RANK, C_CQ // MLA_Q_RANK), (proj, LANE, C_CKV // LANE),
                             (proj, LANE, C_KR // LANE), (tab, 3 * LANE, 0)],
                    full_ins=[g_q, g_kv], row_outs=[(MLA_Q_RANK, BF16), (MLA_KV_RANK, BF16), (LANE, F32)])


def _mla_prep_bwd(proj, g_q, g_kv, dcqn, dckvn):
    rows = proj.shape[0]

    def body(i, ins, fulls, outs, accs):
        _acc_init(i, accs)
        dcq, dgq = _rms_bwd(ins[0][...].astype(F32), fulls[0][...], ins[2][...])
        dckv, dgkv = _rms_bwd(ins[1][...].astype(F32), fulls[1][...], ins[3][...])
        outs[0][...] = dcq.astype(BF16)
        outs[1][...] = dckv.astype(BF16)
        accs[0][...] += dgq
        accs[1][...] += dgkv

    return _rowwise(body, name="mla_prep_bwd", rows=rows, tr=512,
                    row_ins=[(proj, MLA_Q_RANK, C_CQ // MLA_Q_RANK), (proj, LANE, C_CKV // LANE),
                             (dcqn, MLA_Q_RANK, 0), (dckvn, MLA_KV_RANK, 0)],
                    full_ins=[g_q, g_kv], row_outs=[(MLA_Q_RANK, BF16), (MLA_KV_RANK, BF16)],
                    acc_outs=[((1, MLA_Q_RANK), F32), ((1, MLA_KV_RANK), F32)])


def _per_head(fn, x):
    return jnp.concatenate([fn(x[:, h * LANE:(h + 1) * LANE]) for h in range(x.shape[1] // LANE)], axis=1)


def _mla_rope_bwd(dq, dk, dv, tab):
    rows = dq.shape[0]
    hw = MLA_HEADS * LANE

    def body(i, ins, fulls, outs, accs):
        t = ins[3][...]
        dkr = jnp.zeros((ins[0].shape[0], LANE), F32)
        for h in range(MLA_HEADS):
            sl = slice(h * LANE, (h + 1) * LANE)
            outs[0][:, sl] = _rope_t(ins[0][:, sl], t).astype(BF16)
            dkr = dkr + ins[1][:, sl]
        outs[1][:, 0:hw] = ins[1][...].astype(BF16)
        outs[1][:, hw:2 * hw] = ins[2][...].astype(BF16)
        lane = lax.broadcasted_iota(jnp.int32, dkr.shape, 1)
        dkr = jnp.where((lane >= ROPE_LO) & (lane < ROPE_LO + MLA_ROPE), dkr, 0.0)
        outs[2][...] = _rope_t(dkr, t).astype(BF16)

    return _rowwise(body, name="mla_rope_bwd", rows=rows, tr=512,
                    row_ins=[(dq, hw, 0), (dk, hw, 0), (dv, hw, 0), (tab, 3 * LANE, 0)],
                    row_outs=[(hw, BF16), (2 * hw, BF16), (LANE, BF16)])


def _dot_nt(a, b):
    return lax.dot_general(a, b, (((1,), (1,)), ((), ())), preferred_element_type=F32)


def _dot_tn(a, b):
    return lax.dot_general(a, b, (((0,), (0,)), ((), ())), preferred_element_type=F32)


def _dot(a, b):
    return jnp.dot(a, b, preferred_element_type=F32)


def _attn_params(s, t, n_res_f32, n_res_bf16, ride=False):
    blocks = n_res_f32 * _nbytes((s, LANE), F32) + n_res_bf16 * _nbytes((s, LANE), BF16) + 6 * _nbytes((t, LANE), F32)
    return pltpu.CompilerParams(dimension_semantics=("arbitrary" if ride else "parallel", "arbitrary"),
                                vmem_limit_bytes=_vmem_limit(blocks, 12 * _nbytes((t, t), F32)))


def _mla_fwd(q, k, v, t, tk, shards=()):
    s, hw = q.shape
    heads, nq, r = hw // LANE, s // t, t // tk
    ng = len(shards)
    scale = 1.0 / math.sqrt(MLA_NOPE + MLA_ROPE)
    scale2 = scale * LOG2E

    def body(q_ref, k_ref, v_ref, *rest):
        o_ref, l_ref = rest[ng], rest[ng + 1]
        h, i = pl.program_id(0), pl.program_id(1)
        if ng:
            gather = _GatherPhases(rest[:ng], rest[ng + 2:2 * ng + 2], *rest[2 * ng + 2:])
            pl.when((h == 0) & (i == 0))(gather.send)
            pl.when((h == heads // 2) & (i == 0))(gather.forward)
        qv = q_ref[...]

        def step(j, carry, off):
            m, l, acc = carry
            sl = pl.ds(pl.multiple_of(j * tk, tk), tk)
            sc = _dot_nt(qv, k_ref[sl, :])
            if off is not None:
                row = lax.broadcasted_iota(jnp.int32, (t, tk), 0)
                col = lax.broadcasted_iota(jnp.int32, (t, tk), 1)
                sc = jnp.where(col + off <= row, sc, -1e30)
            m_new = jnp.maximum(m, jnp.max(sc, axis=1, keepdims=True))
            p = jnp.exp2((sc - m_new) * scale2)
            alpha = jnp.exp2((m - m_new) * scale2)
            l = alpha * l + jnp.sum(p, axis=1, keepdims=True)
            acc = alpha * acc + _dot(p.astype(BF16), v_ref[sl, :])
            return m_new, l, acc

        init = (jnp.full((t, 1), -1e30, F32), jnp.zeros((t, 1), F32), jnp.zeros((t, LANE), F32))
        carry = lax.fori_loop(0, i * r, lambda j, c: step(j, c, None), init)
        for jj in range(r):
            carry = step(i * r + jj, carry, jj * tk)
        m, l, acc = carry
        o_ref[...] = (acc / l).astype(o_ref.dtype)
        l_ref[0] = m * scale2 + jnp.log2(l)
        if ng:
            pl.when((h == heads - 1) & (i == nq - 1))(gather.finish)

    blk = pl.BlockSpec((t, LANE), lambda h, i: (i, h))
    res = pl.BlockSpec((s, LANE), lambda h, i: (0, h))
    outs = pl.pallas_call(
        body, name="mla_fwd", grid=(heads, nq), in_specs=[blk, res, res] + [ANY] * ng,
        out_specs=[blk, pl.BlockSpec((1, t, 1), lambda h, i: (h, i, 0))] + [ANY] * ng,
        out_shape=[jax.ShapeDtypeStruct((s, hw), BF16), jax.ShapeDtypeStruct((heads, s, 1), F32)]
        + [jax.ShapeDtypeStruct((N_CHIPS,) + sh.shape, sh.dtype) for sh in shards],
        scratch_shapes=_dma_sems(6 * ng) if ng else [],
        compiler_params=_attn_params(s, t, 0, 2, ride=ng > 0),
    )(q, k, v, *shards)
    return outs[0], outs[1], list(outs[2:])


def _mla_bwd(q, k, v, o, do, lse, t, tk, parts=()):
    s, hw = q.shape
    heads, nq, r = hw // LANE, s // t, t // tk
    ns = len(parts)
    scale = 1.0 / math.sqrt(MLA_NOPE + MLA_ROPE)
    scale2 = scale * LOG2E

    def body(q_ref, k_ref, v_ref, o_ref, do_ref, l_ref, *rest):
        dq_ref, dk_ref, dv_ref = rest[ns:ns + 3]
        h, i = pl.program_id(0), pl.program_id(1)
        finish = _ride_along(_scatter_copies, ns, (rest[:ns], rest[ns + 3:2 * ns + 3], *rest[2 * ns + 3:]),
                             (h == 0) & (i == 0), (h == heads - 1) & (i == nq - 1))

        @pl.when(i == 0)
        def _():
            dk_ref[...] = jnp.zeros_like(dk_ref)
            dv_ref[...] = jnp.zeros_like(dv_ref)

        qv, dov, lv = q_ref[...], do_ref[...], l_ref[0]
        dlt = jnp.sum(dov.astype(F32) * o_ref[...].astype(F32), axis=1, keepdims=True)

        def step(j, dq, off):
            sl = pl.ds(pl.multiple_of(j * tk, tk), tk)
            kv, vv = k_ref[sl, :], v_ref[sl, :]
            p = jnp.exp2(_dot_nt(qv, kv) * scale2 - lv)
            if off is not None:
                row = lax.broadcasted_iota(jnp.int32, (t, tk), 0)
                col = lax.broadcasted_iota(jnp.int32, (t, tk), 1)
                p = jnp.where(col + off <= row, p, 0.0)
            ds = (p * (_dot_nt(dov, vv) - dlt)).astype(BF16)
            dk_ref[sl, :] += _dot_tn(ds, qv) * scale
            dv_ref[sl, :] += _dot_tn(p.astype(BF16), dov)
            return dq + _dot(ds, kv)

        dq = lax.fori_loop(0, i * r, lambda j, c: step(j, c, None), jnp.zeros((t, LANE), F32))
        for jj in range(r):
            dq = step(i * r + jj, dq, jj * tk)
        dq_ref[...] = dq * scale
        finish()

    blk = pl.BlockSpec((t, LANE), lambda h, i: (i, h))
    res = pl.BlockSpec((s, LANE), lambda h, i: (0, h))
    full = jax.ShapeDtypeStruct((s, hw), F32)
    outs = pl.pallas_call(
        body, name="mla_bwd", grid=(heads, nq),
        in_specs=[blk, res, res, blk, blk, pl.BlockSpec((1, t, 1), lambda h, i: (h, i, 0))] + [ANY] * ns,
        out_specs=[blk, res, res] + [ANY] * ns, out_shape=[full, full, full] + _scatter_shapes(parts),
        scratch_shapes=_dma_sems(3 * ns) if ns else [],
        compiler_params=_attn_params(s, t, 2, 2, ride=ns > 0),
    )(q, k, v, o, do, lse, *parts)
    return outs[0], outs[1], outs[2], list(outs[3:])


def _sb_logits(qv, kv, scale, keep, upper):
    z = _dot_nt(qv, kv) * scale
    e = jnp.exp(-jnp.abs(z))
    l1p = jnp.log(1.0 + e)
    lb = jnp.minimum(z, 0.0) - l1p
    lo = -jnp.maximum(z, 0.0) - l1p
    if keep is not None:
        lo = jnp.where(keep, lo, 0.0)
    hi = lo.astype(BF16)
    rem = (lo - hi.astype(F32)).astype(BF16)
    suf = _dot(hi, upper) + _dot(rem, upper)
    return z, e, lb, lo, suf


def _tri(t, inclusive):
    row = lax.broadcasted_iota(jnp.int32, (t, t), 0)
    col = lax.broadcasted_iota(jnp.int32, (t, t), 1)
    return jnp.where((row >= col) if inclusive else (row > col), 1.0, 0.0).astype(BF16)


SB_CHAINS = 4
SB_FIRST = 2


def _sb_first_tile(b, t):
    start = jnp.maximum(b - (SB_FIRST - 1), 0) * t
    row = lax.broadcasted_iota(jnp.int32, (t, SB_FIRST * t), 0)
    col = lax.broadcasted_iota(jnp.int32, (t, SB_FIRST * t), 1)
    return pl.ds(pl.multiple_of(start, t), SB_FIRST * t), col + start < row + b * t


def _sb_walk(i, first, carries_of):
    n = SB_CHAINS
    carries = [first(c) for c in range(n)]
    width = len(carries[0])

    def alive(carry):
        return jnp.max(carry[0]) >= SB_ZERO_LOG

    def split(st):
        return [tuple(st[1 + c * width:1 + (c + 1) * width]) for c in range(n)]

    def live(st):
        any_alive = alive(split(st)[0])
        for cr in split(st)[1:]:
            any_alive = any_alive | alive(cr)
        return (st[0] <= n * i) & any_alive

    def more(st):
        out = (st[0] + 1,)
        for c, cr in enumerate(split(st)):
            out += tuple(carries_of(c, st[0], cr))
        return out

    st = lax.while_loop(live, more, (jnp.int32(SB_FIRST),) + tuple(x for cr in carries for x in cr))
    jj, carries = st[0], split(st)
    for c in range(1, n):
        def live_c(s2, c=c):
            return (s2[0] <= n * i + c) & alive(s2[1:])

        def more_c(s2, c=c):
            return (s2[0] + 1,) + tuple(carries_of(c, s2[0], s2[1:]))

        carries[c] = lax.while_loop(live_c, more_c, (jj,) + tuple(carries[c]))[1:]
    return carries


def _sb_fwd(proj, t):
    s = proj.shape[0]
    heads, nq, n = SB_HEADS, s // t, SB_CHAINS
    scale = 1.0 / math.sqrt(SB_HEAD_DIM)

    def body(q_ref, k_ref, v_ref, o_ref):
        i = pl.program_id(1)
        upper, upper_first = _tri(t, False), _tri(SB_FIRST * t, False)
        qs = [q_ref[c * t:(c + 1) * t, :].astype(BF16) for c in range(n)]

        def first(c):
            sl, keep = _sb_first_tile(n * i + c, t)
            _, _, lb, lo, suf = _sb_logits(qs[c], k_ref[sl, :].astype(BF16), scale, keep, upper_first)
            a = jnp.where(keep, jnp.exp(lb + suf), 0.0)
            return jnp.sum(lo, axis=1, keepdims=True), _dot(a.astype(BF16), v_ref[sl, :].astype(BF16))

        def step(c, jj, carry):
            run, acc = carry
            sl = pl.ds(pl.multiple_of((n * i + c - jj) * t, t), t)
            _, _, lb, lo, suf = _sb_logits(qs[c], k_ref[sl, :].astype(BF16), scale, None, upper)
            a = jnp.exp(lb + suf + run)
            acc = acc + _dot(a.astype(BF16), v_ref[sl, :].astype(BF16))
            return run + jnp.sum(lo, axis=1, keepdims=True), acc

        carries = _sb_walk(i, first, step)
        for c in range(n):
            o_ref[c * t:(c + 1) * t, :] = carries[c][1]

    return pl.pallas_call(
        body, name="sb_fwd", grid=(heads, nq // n),
        in_specs=[pl.BlockSpec((n * t, LANE), lambda h, i: (i, C_SBQ // LANE + h)),
                  pl.BlockSpec((s, LANE), lambda h, i: (0, C_SBK // LANE + h)),
                  pl.BlockSpec((s, LANE), lambda h, i: (0, C_SBV // LANE + h))],
        out_specs=pl.BlockSpec((n * t, LANE), lambda h, i: (i, h)),
        out_shape=jax.ShapeDtypeStruct((s, heads * LANE), F32),
        compiler_params=_attn_params(s, n * t, 0, 2),
    )(proj, proj, proj)


def _sb_bwd(proj, o, do, t, stacked=()):
    s = proj.shape[0]
    heads, nq, n = SB_HEADS, s // t, SB_CHAINS
    nx = len(stacked)
    scale = 1.0 / math.sqrt(SB_HEAD_DIM)

    def body(q_ref, k_ref, v_ref, o_ref, do_ref, *rest):
        dq_ref, dk_ref, dv_ref = rest[nx:nx + 3]
        hd, i = pl.program_id(0), pl.program_id(1)
        finish = _ride_along(_exchange_copies, nx, (rest[:nx], rest[nx + 3:2 * nx + 3], *rest[2 * nx + 3:]),
                             (hd == 0) & (i == 0), (hd == heads - 1) & (i == nq // n - 1))

        @pl.when(i == 0)
        def _():
            dk_ref[...] = jnp.zeros_like(dk_ref)
            dv_ref[...] = jnp.zeros_like(dv_ref)

        rows = [slice(c * t, (c + 1) * t) for c in range(n)]
        qs = [q_ref[r, :].astype(BF16) for r in rows]
        dos = [do_ref[r, :] for r in rows]
        totals = [jnp.sum(dos[c].astype(F32) * o_ref[rows[c], :], axis=1, keepdims=True) for c in range(n)]
        tris = {1: (_tri(t, False), _tri(t, True)), SB_FIRST: (_tri(SB_FIRST * t, False), _tri(SB_FIRST * t, True))}

        def tile(c, sl, keep, blocks, carry):
            run, g, dq = carry
            qv, dov = qs[c], dos[c]
            upper, upper_incl = tris[blocks]
            kv, vv = k_ref[sl, :].astype(BF16), v_ref[sl, :].astype(BF16)
            z, e, lb, lo, suf = _sb_logits(qv, kv, scale, keep, upper)
            tail = suf + run
            a = jnp.exp(lb + tail)
            if keep is not None:
                a = jnp.where(keep, a, 0.0)
            ab = a.astype(BF16)
            gr = ab.astype(F32) * _dot_nt(dov, vv)
            ghi = gr.astype(BF16)
            grem = (gr - ghi.astype(F32)).astype(BF16)
            before = totals[c] - g - (_dot(ghi, upper_incl) + _dot(grem, upper_incl))
            before = jnp.where(tail < SB_ZERO_LOG, 0.0, before)
            r = 1.0 / (1.0 + e)
            pos = z >= 0.0
            dz = r * (gr * jnp.where(pos, e, 1.0) - before * jnp.where(pos, 1.0, e))
            if keep is not None:
                dz = jnp.where(keep, dz, 0.0)
            dzb = (dz * scale).astype(BF16)
            dk_ref[sl, :] += _dot_tn(dzb, qv)
            dv_ref[sl, :] += _dot_tn(ab, dov)
            return (run + jnp.sum(lo, axis=1, keepdims=True), g + jnp.sum(gr, axis=1, keepdims=True),
                    dq + _dot(dzb, kv))

        zero = jnp.zeros((t, 1), F32)
        init = (zero, zero, jnp.zeros((t, LANE), F32))

        def first(c):
            sl, keep = _sb_first_tile(n * i + c, t)
            return tile(c, sl, keep, SB_FIRST, init)

        def step(c, jj, carry):
            return tile(c, pl.ds(pl.multiple_of((n * i + c - jj) * t, t), t), None, 1, carry)

        carries = _sb_walk(i, first, step)
        for c in range(n):
            dq_ref[rows[c], :] = carries[c][2]
        finish()

    blk = pl.BlockSpec((n * t, LANE), lambda h, i: (i, h))
    res = pl.BlockSpec((s, LANE), lambda h, i: (0, h))
    full = jax.ShapeDtypeStruct((s, heads * LANE), F32)
    outs = pl.pallas_call(
        body, name="sb_bwd", grid=(heads, nq // n),
        in_specs=[pl.BlockSpec((n * t, LANE), lambda h, i: (i, C_SBQ // LANE + h)),
                  pl.BlockSpec((s, LANE), lambda h, i: (0, C_SBK // LANE + h)),
                  pl.BlockSpec((s, LANE), lambda h, i: (0, C_SBV // LANE + h)), blk, blk] + [ANY] * nx,
        out_specs=[blk, res, res] + [ANY] * nx, out_shape=[full, full, full] + _exchange_shapes(stacked),
        scratch_shapes=_dma_sems(nx) if nx else [],
        compiler_params=_attn_params(s, n * t, 2, 2, ride=nx > 0),
    )(proj, proj, proj, o, do, *stacked)
    return outs[0], outs[1], outs[2], list(outs[3:])


def _xattn_probs(qh, kh):
    sc = _dot_nt(qh, kh) * (1.0 / math.sqrt(X_HEAD_DIM))
    p = jnp.exp(sc - jnp.max(sc, axis=1, keepdims=True))
    return p / jnp.sum(p, axis=1, keepdims=True)


def _xattn_fwd(xq, xkv):
    rows = xq.shape[0]
    w = X_HEADS * X_HEAD_DIM

    def body(i, ins, fulls, outs, accs):
        for h in range(X_HEADS):
            sl = slice(h * LANE, (h + 1) * LANE)
            p = _xattn_probs(ins[0][:, sl], fulls[0][:, sl])
            outs[0][:, sl] = _dot(p.astype(BF16), fulls[0][:, w + h * LANE:w + (h + 1) * LANE]).astype(BF16)

    return _rowwise(body, name="xattn_fwd", rows=rows, tr=512, row_ins=[(xq, w, 0)], full_ins=[xkv],
                    row_outs=[(w, BF16)])[0]


def _xattn_bwd(xq, xkv, dxo):
    rows = xq.shape[0]
    w = X_HEADS * X_HEAD_DIM

    def body(i, ins, fulls, outs, accs):
        _acc_init(i, accs)
        for h in range(X_HEADS):
            sl = slice(h * LANE, (h + 1) * LANE)
            slv = slice(w + h * LANE, w + (h + 1) * LANE)
            qh, kh, vh, doh = ins[0][:, sl], fulls[0][:, sl], fulls[0][:, slv], ins[1][:, sl]
            p = _xattn_probs(qh, kh)
            dp = _dot_nt(doh, vh)
            ds = (p * (dp - jnp.sum(p * dp, axis=1, keepdims=True)) * (1.0 / math.sqrt(X_HEAD_DIM))).astype(BF16)
            outs[0][:, sl] = _dot(ds, kh).astype(BF16)
            accs[0][:, sl] += _dot_tn(ds, qh)
            accs[0][:, slv] += _dot_tn(p.astype(BF16), doh)

    return _rowwise(body, name="xattn_bwd", rows=rows, tr=512, row_ins=[(xq, w, 0), (dxo, w, 0)], full_ins=[xkv],
                    row_outs=[(w, BF16)], acc_outs=[(xkv.shape, F32)])


def _gate_fwd(proj, pa, pb, b_gate):
    rows = proj.shape[0]

    def body(i, ins, fulls, outs, accs):
        sa = _sigmoid(ins[0][...].astype(F32) + fulls[0][0:1, :])
        sb = _sigmoid(ins[1][...].astype(F32) + fulls[0][1:2, :])
        outs[0][...] = (sa * ins[2][...] + sb * ins[3][...]).astype(BF16)

    return _rowwise(body, name="gate_fwd", rows=rows, tr=512,
                    row_ins=[(proj, D_MODEL, C_GA // D_MODEL), (proj, D_MODEL, C_GB // D_MODEL), (pa, D_MODEL, 0),
                             (pb, D_MODEL, 0)],
                    full_ins=[b_gate], row_outs=[(D_MODEL, BF16)])[0]


def _gate_bwd(proj, pa, pb, b_gate, dm):
    rows = proj.shape[0]

    def body(i, ins, fulls, outs, accs):
        _acc_init(i, accs)
        d = ins[4][...]
        sa = _sigmoid(ins[0][...].astype(F32) + fulls[0][0:1, :])
        sb = _sigmoid(ins[1][...].astype(F32) + fulls[0][1:2, :])
        dga = d * ins[2][...] * sa * (1.0 - sa)
        dgb = d * ins[3][...] * sb * (1.0 - sb)
        outs[0][...] = (d * sa).astype(BF16)
        outs[1][...] = (d * sb).astype(BF16)
        outs[2][...] = dga.astype(BF16)
        outs[3][...] = dgb.astype(BF16)
        accs[0][0:1, :] += jnp.sum(dga, axis=0, keepdims=True)
        accs[0][1:2, :] += jnp.sum(dgb, axis=0, keepdims=True)

    return _rowwise(body, name="gate_bwd", rows=rows, tr=512,
                    row_ins=[(proj, D_MODEL, C_GA // D_MODEL), (proj, D_MODEL, C_GB // D_MODEL), (pa, D_MODEL, 0),
                             (pb, D_MODEL, 0), (dm, D_MODEL, 0)],
                    full_ins=[b_gate], row_outs=[(D_MODEL, BF16)] * 4, acc_outs=[((2, D_MODEL), F32)])


def _loss_head(x3, target, g_final):
    rows = x3.shape[0]

    def body(i, ins, fulls, outs, accs):
        _acc_init(i, accs)
        xv, g = ins[0][...], fulls[0][...]
        d = _rms(xv, g) - ins[1][...]
        dx, dg = _rms_bwd(xv, g, d * (1.0 / D_MODEL))
        outs[0][...] = dx
        accs[0][...] += dg
        accs[1][...] += jnp.sum(d * d, axis=0, keepdims=True)

    return _rowwise(body, name="loss_head", rows=rows, tr=512, row_ins=[(x3, D_MODEL, 0), (target, D_MODEL, 0)],
                    full_ins=[g_final], row_outs=[(D_MODEL, F32)], acc_outs=[((1, D_MODEL), F32), ((1, D_MODEL), F32)])


def _adamw(w, g, m, v, name):
    rows, c = w.shape

    def body(i, ins, fulls, outs, accs):
        wv, gv = ins[0][...], ins[1][...]
        mn = ADAM_B1 * ins[2][...] + (1.0 - ADAM_B1) * gv
        vn = ADAM_B2 * ins[3][...] + (1.0 - ADAM_B2) * jnp.square(gv)
        m_hat = mn / (1.0 - ADAM_B1 ** ADAM_STEP)
        v_hat = vn / (1.0 - ADAM_B2 ** ADAM_STEP)
        outs[0][...] = -ADAM_LR * (m_hat / (jnp.sqrt(v_hat) + ADAM_EPS) + ADAM_WD * wv)
        outs[1][...] = mn
        outs[2][...] = vn

    return _rowwise(body, name=name, rows=rows, tr=_row_tile(rows, 256), row_ins=[(a, c, 0) for a in (w, g, m, v)],
                    row_outs=[(c, F32)] * 3)


def _place():
    x, y, c = lax.axis_index("x"), lax.axis_index("y"), lax.axis_index("c")
    chips = [(1 - x, y), (x, 1 - y), (1 - x, 1 - y)]
    return x, y, c, chips


ANY = pl.BlockSpec(memory_space=pl.ANY)


def _remote(src, dst, send_sem, recv_sem, to):
    return pltpu.make_async_remote_copy(src_ref=src, dst_ref=dst, send_sem=send_sem, recv_sem=recv_sem,
                                        device_id=to, device_id_type=MESH)


def _dma_sems(n):
    return [pltpu.SemaphoreType.DMA((n,)), pltpu.SemaphoreType.DMA((n,))]


class _GatherPhases:
    def __init__(self, ins, outs, send_sems, recv_sems):
        self.ins, self.outs, self.send_sems, self.recv_sems = ins, outs, send_sems, recv_sems
        self.x, self.y, self.c, self.chips = _place()
        self.me = 2 * self.x + self.y

    def _copy(self, t, j, chip_idx, hlf, to, src=None):
        h = self.ins[t].shape[0] // 2
        dst = self.outs[t].at[chip_idx, pl.ds(hlf * h, h), :]
        return _remote(dst if src is None else src, dst, self.send_sems.at[6 * t + j], self.recv_sems.at[6 * t + j], to)

    def _sends(self):
        out = []
        for t in range(len(self.ins)):
            h = self.ins[t].shape[0] // 2
            for j, chip in enumerate(self.chips):
                out.append(self._copy(t, j, self.me, self.c, (*chip, self.c), src=self.ins[t].at[pl.ds(self.c * h, h), :]))
        return out

    def _forwards(self):
        return [self._copy(t, 3 + j, 2 * chip[0] + chip[1], self.c, (self.x, self.y, 1 - self.c))
                for t in range(len(self.ins)) for j, chip in enumerate(self.chips)]

    def send(self):
        for cp in self._sends():
            cp.start()

    def forward(self):
        here = (self.x, self.y, self.c)
        landed = [self._copy(t, j, 2 * chip[0] + chip[1], self.c, here)
                  for t in range(len(self.ins)) for j, chip in enumerate(self.chips)]
        for arrival, fwd in zip(landed, self._forwards()):
            arrival.wait_recv()
            fwd.start()

    def finish(self):
        here = (self.x, self.y, self.c)
        for t in range(len(self.ins)):
            for j, chip in enumerate(self.chips):
                self._copy(t, 3 + j, 2 * chip[0] + chip[1], 1 - self.c, here).wait_recv()
        for cp in self._sends() + self._forwards():
            cp.wait_send()


def _all_gather_weights(shards):
    n = len(shards)

    def body(*refs):
        gather = _GatherPhases(refs[:n], refs[n:2 * n], *refs[2 * n:])
        gather.send()
        gather.forward()
        gather.finish()

    return pl.pallas_call(
        body, name="all_gather_weights", in_specs=[ANY] * n, out_specs=[ANY] * n,
        out_shape=[jax.ShapeDtypeStruct((N_CHIPS,) + s.shape, s.dtype) for s in shards],
        scratch_shapes=_dma_sems(6 * n),
    )(*shards)


def _exchange_copies(ins, outs, send_sems, recv_sems):
    x, y, c, _ = _place()
    cps = []
    for t in range(len(ins)):
        h = ins[t].shape[1] // 2
        cps.append(_remote(ins[t].at[:, pl.ds((1 - c) * h, h), :], outs[t], send_sems.at[t], recv_sems.at[t],
                           (x, y, 1 - c)))
    return cps


def _exchange_shapes(stacked):
    return [jax.ShapeDtypeStruct((N_CHIPS, s.shape[1] // 2, s.shape[2]), s.dtype) for s in stacked]


def _scatter_copies(ins, outs, send_sems, recv_sems):
    x, y, c, chips = _place()
    return [_remote(ins[t].at[2 * chip[0] + chip[1]], outs[t].at[j], send_sems.at[3 * t + j], recv_sems.at[3 * t + j],
                    (*chip, c)) for t in range(len(ins)) for j, chip in enumerate(chips)]


def _scatter_shapes(parts):
    return [jax.ShapeDtypeStruct((N_CHIPS - 1,) + s.shape[1:], s.dtype) for s in parts]


def _ride_along(copies_of, n, refs, first, last):
    if not n:
        return lambda: None

    def start():
        for cp in copies_of(*refs):
            cp.start()

    def wait():
        for cp in copies_of(*refs):
            cp.wait()

    pl.when(first)(start)
    return lambda: pl.when(last)(wait)


def _copy_call(copies_of, arrs, out_shape, sems_per, name):
    n = len(arrs)

    def body(*refs):
        cps = copies_of(refs[:n], refs[n:2 * n], *refs[2 * n:])
        for cp in cps:
            cp.start()
        for cp in cps:
            cp.wait()

    return pl.pallas_call(body, name=name, in_specs=[ANY] * n, out_specs=[ANY] * n, out_shape=out_shape,
                          scratch_shapes=_dma_sems(sems_per * n))(*arrs)


def _pair_exchange_grads(stacked, name):
    return _copy_call(_exchange_copies, stacked, _exchange_shapes(stacked), 1, name)


def _chip_scatter(parts, name):
    return _copy_call(_scatter_copies, parts, _scatter_shapes(parts), 3, name)


def _pair_exchange_halves(shards):
    n = len(shards)

    def body(*refs):
        bufs = refs[n:2 * n]
        send_sems, recv_sems = refs[2 * n:]
        x, y, c, _ = _place()
        cps = []
        for t in range(n):
            h = bufs[t].shape[0] // 2
            rows = bufs[t].at[pl.ds(c * h, h), :]
            cps.append(_remote(rows, rows, send_sems.at[t], recv_sems.at[t], (x, y, 1 - c)))
            cps[-1].start()
        for cp in cps:
            cp.wait()

    return pl.pallas_call(
        body, name="pair_exchange_halves", in_specs=[ANY] * n, out_specs=[ANY] * n,
        out_shape=[jax.ShapeDtypeStruct(s.shape, s.dtype) for s in shards],
        input_output_aliases={t: t for t in range(n)},
        scratch_shapes=_dma_sems(n),
    )(*shards)


def _pair_sum(gs, recv, place, name):
    _, r, cols = gs.shape
    h = r // 2

    def kern(p_ref, a_ref, b_ref, o_ref):
        o_ref[...] = (a_ref[...] + b_ref[...]).astype(BF16)

    blk = lambda f: pl.BlockSpec((1, h, cols), f)
    return pl.pallas_call(
        kern, name=name,
        grid_spec=pltpu.PrefetchScalarGridSpec(
            num_scalar_prefetch=1, grid=(N_CHIPS,),
            in_specs=[blk(lambda d, p: (d, p[1], 0)), blk(lambda d, p: (d, 0, 0))],
            out_specs=blk(lambda d, p: (d, 0, 0))),
        out_shape=jax.ShapeDtypeStruct((N_CHIPS, h, cols), BF16),
        compiler_params=pltpu.CompilerParams(dimension_semantics=("arbitrary",),
                                             vmem_limit_bytes=_vmem_limit(3 * _nbytes((h, cols), F32), 0)),
    )(place, gs, recv)


def _chip_sum(gs, recv, got, place, name):
    _, r, cols = gs.shape
    h = r // 2

    def kern(p_ref, a_ref, b_ref, g0, g1, g2, o_ref):
        own = a_ref[0] + b_ref[0]
        o_ref[...] = ((own + g0[0].astype(F32)) + g1[0].astype(F32)) + g2[0].astype(F32)

    blk = lambda f: pl.BlockSpec((1, h, cols), f)
    return pl.pallas_call(
        kern, name=name,
        grid_spec=pltpu.PrefetchScalarGridSpec(
            num_scalar_prefetch=1, grid=(1,),
            in_specs=[blk(lambda i, p: (p[0], p[1], 0)), blk(lambda i, p: (p[0], 0, 0)), blk(lambda i, p: (0, 0, 0)),
                      blk(lambda i, p: (1, 0, 0)), blk(lambda i, p: (2, 0, 0))],
            out_specs=pl.BlockSpec((h, cols), lambda i, p: (p[1], 0))),
        out_shape=jax.ShapeDtypeStruct((r, cols), F32),
        compiler_params=pltpu.CompilerParams(dimension_semantics=("arbitrary",),
                                             vmem_limit_bytes=_vmem_limit(5 * _nbytes((h, cols), F32), 0)),
    )(place, gs, recv, got, got, got)


def _all_reduce_small(vec, name):
    r, cols = vec.shape

    def body(in_ref, out_ref, gath, send_sems, recv_sems):
        x, y, c, _ = _place()
        me = 4 * x + 2 * y + c
        gath[me] = in_ref[...]
        sends = []
        for k in range(1, 8):
            to = (x ^ (k >> 2), y ^ ((k >> 1) & 1), c ^ (k & 1))
            cp = pltpu.make_async_remote_copy(src_ref=in_ref, dst_ref=gath.at[me], send_sem=send_sems.at[k - 1],
                                              recv_sem=recv_sems.at[k - 1], device_id=to, device_id_type=MESH)
            cp.start()
            sends.append(cp)
        for k in range(1, 8):
            peer = me ^ k
            pltpu.make_async_remote_copy(src_ref=in_ref, dst_ref=gath.at[peer], send_sem=send_sems.at[k - 1],
                                         recv_sem=recv_sems.at[k - 1], device_id=(x, y, c),
                                         device_id_type=MESH).wait_recv()
        for cp in sends:
            cp.wait_send()
        acc = gath[0]
        for d in range(1, 8):
            acc = acc + gath[d]
        out_ref[...] = acc

    vm = pl.BlockSpec(memory_space=pltpu.VMEM)
    return pl.pallas_call(
        body, name=name, in_specs=[vm], out_specs=vm,
        out_shape=jax.ShapeDtypeStruct((r, cols), F32),
        scratch_shapes=[pltpu.VMEM((8, r, cols), F32), pltpu.SemaphoreType.DMA((7,)), pltpu.SemaphoreType.DMA((7,))],
    )(vec)


def _pad_heads(w, heads, dim, axis):
    shp = w.shape[:axis] + (heads, dim) + w.shape[axis + 1:]
    pad = [(0, 0)] * len(shp)
    pad[axis + 1] = (0, LANE - dim)
    w = jnp.pad(w.reshape(shp), pad)
    return w.reshape(w.shape[:axis] + (heads * LANE,) + w.shape[axis + 2:])


def _unpad_heads(w, heads, dim, axis):
    shp = w.shape[:axis] + (heads, LANE) + w.shape[axis + 1:]
    w = lax.slice_in_dim(w.reshape(shp), 0, dim, axis=axis + 1)
    return w.reshape(w.shape[:axis] + (heads * dim,) + w.shape[axis + 2:])


def _w_in_layout(w_in):
    kr = jnp.pad(w_in[:, 384:416], ((0, 0), (ROPE_LO, LANE - ROPE_LO - MLA_ROPE)))
    sb = lambda lo: _pad_heads(w_in[:, lo:lo + 512], SB_HEADS, SB_HEAD_DIM, 1)
    return jnp.concatenate([w_in[:, 1952:2976], w_in[:, 2976:4000], sb(416), sb(928), sb(1440), w_in[:, 0:256],
                            w_in[:, 256:384], kr], axis=1)


def _w_in_unlayout(d):
    sb = lambda lo: _unpad_heads(d[:, lo:lo + 1024], SB_HEADS, SB_HEAD_DIM, 1)
    return jnp.concatenate([d[:, C_CQ:C_CQ + 256], d[:, C_CKV:C_CKV + 128], d[:, C_KR + ROPE_LO:C_KR + ROPE_LO + MLA_ROPE],
                            sb(C_SBQ), sb(C_SBK), sb(C_SBV), d[:, C_GA:C_GA + 1024], d[:, C_GB:C_GB + 1024]], axis=1)


def _w_ukv_layout(w):
    w3 = w.reshape(MLA_KV_RANK, MLA_HEADS, MLA_NOPE + MLA_V)
    pad = lambda part: jnp.pad(part, ((0, 0), (0, 0), (0, LANE - part.shape[2]))).reshape(MLA_KV_RANK, MLA_HEADS * LANE)
    return jnp.concatenate([pad(w3[:, :, :MLA_NOPE]), pad(w3[:, :, MLA_NOPE:])], axis=1)


def _w_ukv_unlayout(d):
    hw = MLA_HEADS * LANE
    kpart = d[:, :hw].reshape(MLA_KV_RANK, MLA_HEADS, LANE)[:, :, :MLA_NOPE]
    vpart = d[:, hw:].reshape(MLA_KV_RANK, MLA_HEADS, LANE)[:, :, :MLA_V]
    return jnp.concatenate([kpart, vpart], axis=2).reshape(MLA_KV_RANK, MLA_HEADS * (MLA_NOPE + MLA_V))


def _shard_of(full, d, axis):
    n = full.shape[axis] // N_CHIPS
    return lax.slice_in_dim(full, d * n, (d + 1) * n, axis=axis)


def _local_step(x, mem, pos, target, w, t_mla, t_sb, late=None, reduce=None):
    s = x.shape[0]
    w = dict(w)
    win = _w_in_layout(w["w_in"])
    wuq = _pad_heads(w["w_uq"], MLA_HEADS, MLA_NOPE + MLA_ROPE, 1)
    wkv = _w_ukv_layout(w["w_ukv"])
    inv_freq = ROPE_THETA ** (-jnp.arange(0, MLA_ROPE, 2, dtype=F32) / MLA_ROPE)
    freq_lane = jnp.pad(jnp.concatenate([inv_freq, inv_freq]), (ROPE_LO, LANE - ROPE_LO - MLA_ROPE)).reshape(1, LANE)
    add = lambda accs, ex: (accs[0] + ex[0],)

    def add_norm(accs, ex):
        y = accs[0] + ex[0]
        return y, _rms(y, ex[1])

    tab = _rope_tables(pos.reshape(s, 1), freq_lane)
    h = _rms_fwd_call(x, w["g_mix"], "rms_mix")
    proj = _mm(h, [win], name="proj_in", tn=1408, out_dtypes=(BF16,))
    cqn, ckvn, krope = _mla_prep_fwd(proj, tab, w["g_q_lat"], w["g_kv_lat"])
    hw = MLA_HEADS * LANE
    qa = _mm(cqn, [wuq], name="q_up", row_extras=(tab,), out_dtypes=(BF16,),
             epilogue=lambda accs, ex: (_per_head(lambda t: _rope(t, ex[0]), accs[0]),))
    ka = _mm(ckvn, [wkv[:, :hw]], name="k_up", row_extras=(krope,), out_dtypes=(BF16,),
             epilogue=lambda accs, ex: (_per_head(lambda t: t + ex[0], accs[0]),))
    va = _mm(ckvn, [wkv[:, hw:]], name="v_up", out_dtypes=(BF16,))
    o_a, lse, gathered = _mla_fwd(qa, ka, va, t_mla[0], t_mla[1], late[0] if late else ())
    if late:
        w.update(late[1](gathered))
    wa = _pad_heads(w["w_a_proj"], MLA_HEADS, MLA_V, 0)
    wb = _pad_heads(w["w_b_proj"], SB_HEADS, SB_HEAD_DIM, 0)
    o_b = _sb_fwd(proj, t_sb)
    pa = _mm(o_a, [wa], name="proj_a")
    pb = _mm(o_b, [wb], name="proj_b")
    merged = _gate_fwd(proj, pa, pb, w["b_gate"])
    x1, hx = _mm(merged, [w["w_o"]], name="proj_o", extras=(x,), consts=(w["g_x"],), epilogue=add_norm,
                 out_dtypes=(F32, BF16))
    mn = _rms_fwd_call(mem, w["g_mem"], "rms_mem")
    xq = _mm(hx, [w["w_xq"]], name="xq", out_dtypes=(BF16,))
    xkv = _mm(mn, [w["w_xkv"]], name="xkv", out_dtypes=(BF16,))
    xo = _xattn_fwd(xq, xkv)
    x2, hf = _mm(xo, [w["w_xo"]], name="proj_xo", extras=(x1,), consts=(w["g_ffn"],), epilogue=add_norm,
                 out_dtypes=(F32, BF16))

    def swiglu(accs, ex):
        a, b = accs
        return a, b, a * _sigmoid(a) * b

    ga, gu, hmid = _mm(hf, [w["w_gate"], w["w_up"]], name="ffn_up", epilogue=swiglu, out_dtypes=(BF16, BF16, BF16),
                       tm=512, tn=1408)
    x3 = _mm(hmid, [w["w_down"]], name="ffn_down", extras=(x2,), epilogue=add, tk=2816)

    dx3, dg_final, sq = _loss_head(x3, target, w["g_final"].reshape(1, D_MODEL))
    g = {"g_final": dg_final.reshape(D_MODEL)}

    def swiglu_bwd(accs, ex):
        dh, a, b = accs[0], ex[0].astype(F32), ex[1].astype(F32)
        sg = _sigmoid(a)
        return dh * b * sg * (1.0 + a * (1.0 - sg)), dh * a * sg

    da, db = _mm(dx3, [w["w_down"]], name="ffn_down_dx", tb=True, extras=(ga, gu), epilogue=swiglu_bwd,
                 out_dtypes=(BF16, BF16), tm=512, tn=1408, chunk=MM_CHUNK)
    g["w_down"] = _mm(hmid, [dx3], name="ffn_down_dw", ta=True, tm=1408)
    g["w_gate"] = _mm(hf, [da], name="ffn_gate_dw", ta=True, tn=1408)
    g["w_up"] = _mm(hf, [db], name="ffn_up_dw", ta=True, tn=1408)
    dhf = _mm(da, [w["w_gate"]], name="ffn_gate_dx", tb=True, tk=2816)
    dhf = _mm(db, [w["w_up"]], name="ffn_up_dx", tb=True, extras=(dhf,), epilogue=add, tk=2816,
              out_dtypes=(BF16,))
    dx2, g["g_ffn"] = _rms_bwd_call(x2, w["g_ffn"], dhf, dx3, "rms_ffn_bwd")

    dxo = _mm(dx2, [w["w_xo"]], name="proj_xo_dx", tb=True, out_dtypes=(BF16,))
    g["w_xo"] = _mm(xo, [dx2], name="proj_xo_dw", ta=True)
    dxq, dxkv = _xattn_bwd(xq, xkv, dxo)
    dhx = _mm(dxq, [w["w_xq"]], name="xq_dx", tb=True, out_dtypes=(BF16,))
    g["w_xq"] = _mm(hx, [dxq], name="xq_dw", ta=True)
    dmn = _mm(dxkv, [w["w_xkv"]], name="xkv_dx", tb=True)
    g["w_xkv"] = _mm(mn, [dxkv], name="xkv_dw", ta=True)
    dx1, g["g_x"] = _rms_bwd_call(x1, w["g_x"], dhx, dx2, "rms_x_bwd")
    _, g["g_mem"] = _rms_bwd_call(mem, w["g_mem"], dmn, None, "rms_mem_bwd")

    dmerged = _mm(dx1, [w["w_o"]], name="proj_o_dx", tb=True)
    g["w_o"] = _mm(merged, [dx1], name="proj_o_dw", ta=True)
    dpa, dpb, dga, dgb, g["b_gate"] = _gate_bwd(proj, pa, pb, w["b_gate"], dmerged)
    do_a = _mm(dpa, [wa], name="proj_a_dx", tb=True, out_dtypes=(BF16,))
    do_b = _mm(dpb, [wb], name="proj_b_dx", tb=True, out_dtypes=(BF16,))
    g["w_a_proj"] = _unpad_heads(_mm(o_a, [dpa], name="proj_a_dw", ta=True), MLA_HEADS, MLA_V, 0)
    g["w_b_proj"] = _unpad_heads(_mm(o_b, [dpb], name="proj_b_dw", ta=True), SB_HEADS, SB_HEAD_DIM, 0)

    stacked = [reduce[1](n, g[n]) for n in reduce[0]] if reduce else []
    dsq, dsk, dsv, recv = _sb_bwd(proj, o_b, do_b, t_sb, stacked)
    parts = [reduce[2](n, gs, rv) for n, gs, rv in zip(reduce[0], stacked, recv)] if reduce else []
    dqa, dka, dva, got = _mla_bwd(qa, ka, va, o_a, do_a, lse, t_mla[0], t_mla[2], parts)
    riding = dict(zip(reduce[0], zip(stacked, recv, got))) if reduce else {}
    dqp, dkvp, dkr = _mla_rope_bwd(dqa, dka, dva, tab)
    g["w_uq"] = _unpad_heads(_mm(cqn, [dqp], name="q_up_dw", ta=True), MLA_HEADS, MLA_NOPE + MLA_ROPE, 1)
    g["w_ukv"] = _w_ukv_unlayout(_mm(ckvn, [dkvp], name="kv_up_dw", ta=True))
    dcqn = _mm(dqp, [wuq], name="q_up_dx", tb=True)
    dckvn = _mm(dkvp, [wkv], name="kv_up_dx", tb=True)
    dcq, dckv, g["g_q_lat"], g["g_kv_lat"] = _mla_prep_bwd(proj, w["g_q_lat"], w["g_kv_lat"], dcqn, dckvn)

    dproj = jnp.concatenate([dga, dgb, dsq.astype(BF16), dsk.astype(BF16), dsv.astype(BF16), dcq, dckv, dkr], axis=1)
    g["w_in"] = _w_in_unlayout(_mm(h, [dproj], name="proj_in_dw", ta=True, tn=1408))
    dh = _mm(dproj, [win], name="proj_in_dx", tb=True, tk=2816, out_dtypes=(BF16,))
    grad_x, g["g_mix"] = _rms_bwd_call(x, w["g_mix"], dh, dx1, "rms_mix_bwd")
    return sq, grad_x, g, riding


def _small_pack(d):
    row5 = jnp.concatenate([d["g_q_lat"].reshape(-1), d["g_kv_lat"].reshape(-1), jnp.zeros((640,), F32)])
    rows = [d[n].reshape(-1) for n in ("g_mix", "g_x", "g_mem", "g_ffn", "g_final")] + [row5]
    return rows


def _small_unpack(p, like):
    out = {n: p[i].reshape(like[n].shape) for i, n in enumerate(("g_mix", "g_x", "g_mem", "g_ffn", "g_final"))}
    out["g_q_lat"] = p[5, 0:256].reshape(like["g_q_lat"].shape)
    out["g_kv_lat"] = p[5, 256:384].reshape(like["g_kv_lat"].shape)
    return out


def kernel(x, mem, positions, g_mix, w_in, b_gate, g_q_lat, w_uq, g_kv_lat, w_ukv, w_a_proj, w_b_proj, w_o, g_x, g_mem, w_xq, w_xkv, w_xo, g_ffn, w_gate, w_up, w_down, g_final, loss_target, m_g_mix, m_w_in, m_b_gate, m_g_q_lat, m_w_uq, m_g_kv_lat, m_w_ukv, m_w_a_proj, m_w_b_proj, m_w_o, m_g_x, m_g_mem, m_w_xq, m_w_xkv, m_w_xo, m_g_ffn, m_w_gate, m_w_up, m_w_down, m_g_final, v_g_mix, v_w_in, v_b_gate, v_g_q_lat, v_w_uq, v_g_kv_lat, v_w_ukv, v_w_a_proj, v_w_b_proj, v_w_o, v_g_x, v_g_mem, v_w_xq, v_w_xkv, v_w_xo, v_g_ffn, v_w_gate, v_w_up, v_w_down, v_g_final):
    given = dict(locals())
    names = [n for n, _, _ in MATS] + ["b_gate"] + list(SMALL)
    wts = {n: given[n] for n in names}
    mom = {n: given["m_" + n] for n in names}
    var = {n: given["v_" + n] for n in names}
    shard2d = {n: shp for n, shp, _ in MATS}
    shard2d["b_gate"] = B_GATE_SHARD
    cx, cy, cc = lax.axis_index("x"), lax.axis_index("y"), lax.axis_index("c")
    me = 2 * cx + cy
    place = jnp.stack([me, cc]).astype(jnp.int32)
    bcol = me * B_GATE_SHARD[1]

    own = [wts[n].reshape(shard2d[n]).astype(BF16) for n, _, _ in MATS]
    bias = (("b_gate", (BIAS_ROWS, B_GATE_SHARD[1]), 1),)
    own_bias = [jnp.pad(wts["b_gate"].reshape(B_GATE_SHARD), ((0, BIAS_ROWS - B_GATE_SHARD[0]), (0, 0)))]

    def assemble(mats, gathered, mine):
        out = {}
        for (n, shp, ax), g4, shard in zip(mats, gathered, mine):
            g4 = lax.dynamic_update_slice(g4, shard[None], (me, 0, 0))
            out[n] = g4.reshape(N_CHIPS * shp[0], shp[1]) if ax == 0 else jnp.concatenate(list(g4), axis=1)
        return out

    first = own[:N_EARLY] + own_bias
    full = assemble(MATS[:N_EARLY] + bias, _all_gather_weights(first), first)
    full["b_gate"] = full["b_gate"][0:B_GATE_SHARD[0]]
    late = (own[N_EARLY:], lambda gathered: assemble(MATS[N_EARLY:], gathered, own[N_EARLY:]))
    for n in SMALL:
        full[n] = wts[n].reshape(1, -1) if n != "g_final" else wts[n]

    axis_of = {n: ax for n, _, ax in MATS}
    stack = lambda n, g: jnp.stack([_shard_of(g, d, axis_of[n]) for d in range(N_CHIPS)])
    pair_sum = lambda n, gs, rv: _pair_sum(gs, rv, place, "pair_sum_" + n)
    behind = [n for n, _, _ in MATS[N_EARLY:]]
    sq, grad_x, grads, riding = _local_step(x[0], mem[0], positions[0], loss_target[0], full, t_mla=(1024, 1024, 1024), t_sb=256,
                                            late=late, reduce=(behind, stack, pair_sum))

    last = [n for n, _, _ in MATS[:N_EARLY]]
    stacked = [stack(n, grads[n]) for n in last]
    recv = _pair_exchange_grads(stacked, "pair_exchange_grads")
    got = _chip_scatter([pair_sum(n, gs, rv) for n, gs, rv in zip(last, stacked, recv)], "chip_scatter")
    riding.update(zip(last, zip(stacked, recv, got)))
    halves = [_chip_sum(*riding[n], place, "chip_sum_" + n) for n, _, _ in MATS]
    g_shard = dict(zip([n for n, _, _ in MATS], _pair_exchange_halves(halves)))

    small_rows = _small_pack({n: grads[n] for n in SMALL}) + [sq.reshape(-1), grads["b_gate"][0], grads["b_gate"][1]]
    small_rows += [jnp.zeros((D_MODEL,), F32)] * (SMALL_ROWS - len(small_rows))
    small = _all_reduce_small(jnp.stack(small_rows), "all_reduce_small")
    loss = (0.5 / D_MODEL) * jnp.sum(small[6])
    g_shard["b_gate"] = lax.dynamic_slice(small[7:9], (0, bcol), B_GATE_SHARD)

    out = {"grad": {}, "delta": {}, "m": {}, "v": {}}
    for n in [n for n, _, _ in MATS] + ["b_gate"]:
        shape = wts[n].shape
        r2 = lambda a: a.reshape(shard2d[n])
        d_n, m_n, v_n = _adamw(r2(wts[n]), g_shard[n], r2(mom[n]), r2(var[n]), "adamw_" + n)
        for key, a in (("grad", g_shard[n]), ("delta", d_n), ("m", m_n), ("v", v_n)):
            out[key][n] = a.reshape(shape)
    sp = lambda d: jnp.stack(_small_pack(d) + [jnp.zeros((D_MODEL,), F32)] * 2)
    delta_s, m_s, v_s = _adamw(sp(wts), small[0:8].at[6:8].set(0.0), sp(mom), sp(var), "adamw_small")
    for key, p in (("grad", small), ("delta", delta_s), ("m", m_s), ("v", v_s)):
        out[key].update(_small_unpack(p, wts))

    order = ["g_mix", "w_in", "b_gate", "g_q_lat", "w_uq", "g_kv_lat", "w_ukv", "w_a_proj", "w_b_proj", "w_o", "g_x",
             "g_mem", "w_xq", "w_xkv", "w_xo", "g_ffn", "w_gate", "w_up", "w_down", "g_final"]
    return (loss, grad_x[None], *[out[key][n] for key in ("grad", "delta", "m", "v") for n in order])
```

```python
import functools
import math

import jax
import jax.numpy as jnp
from jax import lax
from jax.experimental import pallas as pl
from jax.experimental.pallas import tpu as pltpu

F32 = jnp.float32
BF16 = jnp.bfloat16
MESH = pl.DeviceIdType.MESH

D_MODEL = 1024
MLA_HEADS = 8
MLA_Q_RANK = 256
MLA_KV_RANK = 128
MLA_NOPE = 64
MLA_ROPE = 32
MLA_V = 64
ROPE_THETA = 10000.0
SB_HEADS = 8
SB_HEAD_DIM = 64
X_HEADS = 4
X_HEAD_DIM = 128
EPS = 1e-6
ADAM_LR = 0.001
ADAM_B1 = 0.9
ADAM_B2 = 0.999
ADAM_EPS = 1e-08
ADAM_WD = 0.01
ADAM_STEP = 10

LANE = 128
LOG2E = 1.4426950408889634
MLA_SCALE = 1.0 / math.sqrt(MLA_NOPE + MLA_ROPE)
SB_SCALE = 1.0 / math.sqrt(SB_HEAD_DIM)
assert math.log2(SB_SCALE) == round(math.log2(SB_SCALE))
MM_CHUNK = 256
N_CHIPS = 4
VMEM_BYTES = 64 * 1024 * 1024

C_GA, C_GB, C_SBQ, C_SBK, C_SBV, C_CQ, C_CKV, C_KR = 0, 1024, 2048, 3072, 4096, 5120, 5376, 5504
ROPE_LO = MLA_NOPE
HALF = MLA_ROPE // 2

SB_ZERO_LOG = -104.0

MATS = (
    ("w_in", (1024, 1000), 1), ("w_uq", (256, 192), 1), ("w_ukv", (128, 256), 1), ("w_a_proj", (512, 256), 1),
    ("w_b_proj", (512, 256), 1), ("w_o", (256, 1024), 0), ("w_xq", (256, 512), 0), ("w_xkv", (256, 1024), 0),
    ("w_xo", (512, 256), 1), ("w_gate", (1024, 704), 1), ("w_up", (1024, 704), 1), ("w_down", (704, 1024), 0),
)
N_EARLY = 3
B_GATE_SHARD = (2, 256)
BIAS_ROWS = 16
SMALL = ("g_mix", "g_x", "g_mem", "g_ffn", "g_final", "g_q_lat", "g_kv_lat")
SMALL_ROWS = 16


def _vmem_limit(block_bytes, temp_bytes):
    est = 2 * block_bytes + temp_bytes + (4 << 20)
    return int(min(max(est, 16 << 20), VMEM_BYTES - (6 << 20)))


def _nbytes(shape, dtype):
    return math.prod(shape) * jnp.dtype(dtype).itemsize


def _row_tile(rows, cap):
    if rows <= cap:
        return rows
    return max(t for t in range(8, cap + 1, 8) if rows % t == 0)


def _tile(n, cap):
    if n <= cap:
        return n
    best = None
    for t in range(LANE, cap + 1, LANE):
        if n % t == 0:
            best = t
    assert best is not None, (n, cap)
    return best


def _mm(a, bs, *, name, ta=False, tb=False, extras=(), row_extras=(), consts=(), epilogue=None, out_dtypes=(F32,),
        tm=1024, tn=1024, tk=1024, chunk=None):
    bs = tuple(bs)
    m, k = (a.shape[1], a.shape[0]) if ta else a.shape
    n = bs[0].shape[0] if tb else bs[0].shape[1]
    tm, tn, tk = _tile(m, tm), _tile(n, tn), _tile(k, tk)
    assert m % tm == 0 and n % tn == 0 and k % tk == 0
    nk = k // tk
    nb, ne, no = len(bs), len(extras) + len(row_extras) + len(consts), len(out_dtypes)
    dims = (((0,) if ta else (1,)), ((1,) if tb else (0,))), ((), ())
    if epilogue is None:
        epilogue = lambda accs, ex: (accs[0],)

    def body(*refs):
        a_ref, b_refs, e_refs = refs[0], refs[1:1 + nb], refs[1 + nb:1 + nb + ne]
        o_refs, acc_refs = refs[1 + nb + ne:1 + nb + ne + no], refs[1 + nb + ne + no:]
        if nk == 1:
            ch = chunk or tm
            bvs = [b_ref[...].astype(BF16) for b_ref in b_refs]
            for r0 in range(0, tm, ch):
                rows = slice(r0, r0 + ch)
                av = (a_ref[:, rows] if ta else a_ref[rows, :]).astype(BF16)
                accs = [lax.dot_general(av, bv, dims, preferred_element_type=F32) for bv in bvs]
                ex = [e[rows, :] for e in e_refs[:ne - len(consts)]] + [e[...] for e in e_refs[ne - len(consts):]]
                for o_ref, v in zip(o_refs, epilogue(accs, ex)):
                    o_ref[rows, :] = v.astype(o_ref.dtype)
            return
        kk = pl.program_id(2)

        @pl.when(kk == 0)
        def _():
            for acc in acc_refs:
                acc[...] = jnp.zeros_like(acc)

        av = a_ref[...].astype(BF16)
        for b_ref, acc in zip(b_refs, acc_refs):
            acc[...] += lax.dot_general(av, b_ref[...].astype(BF16), dims, preferred_element_type=F32)

        @pl.when(kk == nk - 1)
        def _():
            outs = epilogue([acc[...] for acc in acc_refs], [e[...] for e in e_refs])
            for o_ref, v in zip(o_refs, outs):
                o_ref[...] = v.astype(o_ref.dtype)

    a_spec = pl.BlockSpec((tk, tm), lambda i, j, kk: (kk, i)) if ta else pl.BlockSpec((tm, tk), lambda i, j, kk: (i, kk))
    b_spec = pl.BlockSpec((tn, tk), lambda i, j, kk: (j, kk)) if tb else pl.BlockSpec((tk, tn), lambda i, j, kk: (kk, j))
    mn_spec = pl.BlockSpec((tm, tn), lambda i, j, kk: (i, j))
    blocks = (_nbytes((tm, tk), a.dtype) + sum(_nbytes((tk, tn), b.dtype) for b in bs)
              + sum(_nbytes((tm, tn), e.dtype) for e in extras) + sum(_nbytes((tm, tn), d) for d in out_dtypes)
              + sum(_nbytes((tm, e.shape[1]), e.dtype) for e in row_extras))
    temps = (nb + 4) * _nbytes((tm, tn), F32)
    outs = pl.pallas_call(
        body, name=name, grid=(m // tm, n // tn, nk),
        in_specs=[a_spec] + [b_spec] * nb + [mn_spec] * len(extras)
        + [pl.BlockSpec((tm, e.shape[1]), lambda i, j, kk: (i, 0)) for e in row_extras]
        + [pl.BlockSpec(e.shape, lambda i, j, kk: (0, 0)) for e in consts],
        out_specs=[mn_spec] * no,
        out_shape=[jax.ShapeDtypeStruct((m, n), d) for d in out_dtypes],
        scratch_shapes=[pltpu.VMEM((tm, tn), F32) for _ in range(nb if nk > 1 else 0)],
        compiler_params=pltpu.CompilerParams(
            dimension_semantics=("parallel", "parallel", "arbitrary"),
            vmem_limit_bytes=_vmem_limit(blocks, temps)),
    )(a, *bs, *extras, *row_extras, *consts)
    return outs[0] if no == 1 else outs


def _rowwise(body, *, name, rows, tr, row_ins, full_ins=(), row_outs=(), acc_outs=()):
    tr = min(tr, rows)
    assert rows % tr == 0
    n_ri, n_fi, n_ro = len(row_ins), len(full_ins), len(row_outs)

    def kern(*refs):
        body(pl.program_id(0), refs[:n_ri], refs[n_ri:n_ri + n_fi], refs[n_ri + n_fi:n_ri + n_fi + n_ro],
             refs[n_ri + n_fi + n_ro:])

    in_specs = [pl.BlockSpec((tr, w), functools.partial(lambda i, c: (i, c), c=ci)) for _, w, ci in row_ins]
    in_specs += [pl.BlockSpec(f.shape, lambda i: (0, 0)) for f in full_ins]
    out_specs = [pl.BlockSpec((tr, w), lambda i: (i, 0)) for w, _ in row_outs]
    out_specs += [pl.BlockSpec(s, lambda i: (0, 0)) for s, _ in acc_outs]
    out_shape = [jax.ShapeDtypeStruct((rows, w), d) for w, d in row_outs]
    out_shape += [jax.ShapeDtypeStruct(s, d) for s, d in acc_outs]
    blocks = (sum(_nbytes((tr, w), a.dtype) for a, w, _ in row_ins) + sum(_nbytes(f.shape, f.dtype) for f in full_ins)
              + sum(_nbytes((tr, w), d) for w, d in row_outs) + sum(_nbytes(s, d) for s, d in acc_outs))
    widest = max([w for _, w, _ in row_ins] + [w for w, _ in row_outs])
    outs = pl.pallas_call(
        kern, name=name, grid=(rows // tr,), in_specs=in_specs, out_specs=out_specs, out_shape=out_shape,
        compiler_params=pltpu.CompilerParams(
            dimension_semantics=("arbitrary",) if acc_outs else ("parallel",),
            vmem_limit_bytes=_vmem_limit(blocks, 8 * _nbytes((tr, widest), F32))),
    )(*[a for a, _, _ in row_ins], *full_ins)
    return outs


def _rms(x, g):
    r = lax.rsqrt(jnp.mean(x * x, axis=-1, keepdims=True) + EPS)
    return x * r * g


def _rms_bwd(x, g, dy):
    r = lax.rsqrt(jnp.mean(x * x, axis=-1, keepdims=True) + EPS)
    xh = x * r
    dxh = dy * g
    dx = r * (dxh - xh * jnp.mean(dxh * xh, axis=-1, keepdims=True))
    return dx, jnp.sum(dy * xh, axis=0, keepdims=True)


def _sigmoid(x):
    return 1.0 / (1.0 + jnp.exp(-x))


def _acc_init(i, refs):
    @pl.when(i == 0)
    def _():
        for r in refs:
            r[...] = jnp.zeros_like(r)


def _rms_fwd_call(x, g, name):
    rows, c = x.shape

    def body(i, ins, fulls, outs, accs):
        outs[0][...] = _rms(ins[0][...], fulls[0][...]).astype(BF16)

    return _rowwise(body, name=name, rows=rows, tr=512, row_ins=[(x, c, 0)], full_ins=[g], row_outs=[(c, BF16)])[0]


def _rms_bwd_call(x, g, dy, res, name):
    rows, c = x.shape
    row_ins = [(x, c, 0), (dy, c, 0)] + ([(res, c, 0)] if res is not None else [])

    def body(i, ins, fulls, outs, accs):
        _acc_init(i, accs)
        dx, dg = _rms_bwd(ins[0][...], fulls[0][...], ins[1][...].astype(F32))
        if res is not None:
            dx = dx + ins[2][...]
        outs[0][...] = dx
        accs[0][...] += dg

    return _rowwise(body, name=name, rows=rows, tr=512, row_ins=row_ins, full_ins=[g], row_outs=[(c, F32)],
                    acc_outs=[((1, c), F32)])


def _rope_tables(pos_col, freq_lane):
    rows = pos_col.shape[0]

    def body(i, ins, fulls, outs, accs):
        ang = ins[0][...].astype(F32) * fulls[0][...]
        lane = lax.broadcasted_iota(jnp.int32, ang.shape, 1)
        cos, sin = jnp.cos(ang), jnp.sin(ang)
        first = (lane >= ROPE_LO) & (lane < ROPE_LO + HALF)
        second = (lane >= ROPE_LO + HALF) & (lane < ROPE_LO + MLA_ROPE)
        outs[0][:, 0:LANE] = jnp.where(first | second, cos, 1.0)
        outs[0][:, LANE:2 * LANE] = jnp.where(first, -sin, 0.0)
        outs[0][:, 2 * LANE:3 * LANE] = jnp.where(second, sin, 0.0)

    return _rowwise(body, name="rope_tables", rows=rows, tr=1024, row_ins=[(pos_col, 1, 0)], full_ins=[freq_lane],
                    row_outs=[(3 * LANE, F32)])[0]


def _rope(x, tab):
    return (x * tab[:, 0:LANE] + pltpu.roll(x, LANE - HALF, 1) * tab[:, LANE:2 * LANE]
            + pltpu.roll(x, HALF, 1) * tab[:, 2 * LANE:3 * LANE])


def _rope_t(dy, tab):
    return (dy * tab[:, 0:LANE] + pltpu.roll(dy * tab[:, LANE:2 * LANE], HALF, 1)
            + pltpu.roll(dy * tab[:, 2 * LANE:3 * LANE], LANE - HALF, 1))


def _mla_prep_fwd(proj, tab, g_q, g_kv):
    rows = proj.shape[0]

    def body(i, ins, fulls, outs, accs):
        outs[0][...] = _rms(ins[0][...].astype(F32), fulls[0][...]).astype(BF16)
        outs[1][...] = _rms(ins[1][...].astype(F32), fulls[1][...]).astype(BF16)
        outs[2][...] = _rope(ins[2][...].astype(F32), ins[3][...])

    return _rowwise(body, name="mla_prep_fwd", rows=rows, tr=512,
                    row_ins=[(proj, MLA_Q_RANK, C_CQ // MLA_Q_RANK), (proj, LANE, C_CKV // LANE),
                             (proj, LANE, C_KR // LANE), (tab, 3 * LANE, 0)],
                    full_ins=[g_q, g_kv], row_outs=[(MLA_Q_RANK, BF16), (MLA_KV_RANK, BF16), (LANE, F32)])


def _mla_prep_bwd(proj, g_q, g_kv, dcqn, dckvn):
    rows = proj.shape[0]

    def body(i, ins, fulls, outs, accs):
        _acc_init(i, accs)
        dcq, dgq = _rms_bwd(ins[0][...].astype(F32), fulls[0][...], ins[2][...])
        dckv, dgkv = _rms_bwd(ins[1][...].astype(F32), fulls[1][...], ins[3][...])
        outs[0][...] = dcq.astype(BF16)
        outs[1][...] = dckv.astype(BF16)
        accs[0][...] += dgq
        accs[1][...] += dgkv

    return _rowwise(body, name="mla_prep_bwd", rows=rows, tr=512,
                    row_ins=[(proj, MLA_Q_RANK, C_CQ // MLA_Q_RANK), (proj, LANE, C_CKV // LANE),
                             (dcqn, MLA_Q_RANK, 0), (dckvn, MLA_KV_RANK, 0)],
                    full_ins=[g_q, g_kv], row_outs=[(MLA_Q_RANK, BF16), (MLA_KV_RANK, BF16)],
                    acc_outs=[((1, MLA_Q_RANK), F32), ((1, MLA_KV_RANK), F32)])


def _per_head(fn, x):
    return jnp.concatenate([fn(x[:, h * LANE:(h + 1) * LANE]) for h in range(x.shape[1] // LANE)], axis=1)


def _mla_rope_bwd(dq, dk, dv, tab):
    rows = dq.shape[0]
    hw = MLA_HEADS * LANE

    def body(i, ins, fulls, outs, accs):
        t = ins[3][...]
        dkr = jnp.zeros((ins[0].shape[0], LANE), F32)
        for h in range(MLA_HEADS):
            sl = slice(h * LANE, (h + 1) * LANE)
            outs[0][:, sl] = _rope_t(ins[0][:, sl], t).astype(BF16)
            dkr = dkr + ins[1][:, sl]
        outs[1][:, 0:hw] = ins[1][...].astype(BF16)
        outs[1][:, hw:2 * hw] = ins[2][...].astype(BF16)
        lane = lax.broadcasted_iota(jnp.int32, dkr.shape, 1)
        dkr = jnp.where((lane >= ROPE_LO) & (lane < ROPE_LO + MLA_ROPE), dkr, 0.0)
        outs[2][...] = _rope_t(dkr, t).astype(BF16)

    return _rowwise(body, name="mla_rope_bwd", rows=rows, tr=512,
                    row_ins=[(dq, hw, 0), (dk, hw, 0), (dv, hw, 0), (tab, 3 * LANE, 0)],
                    row_outs=[(hw, BF16), (2 * hw, BF16), (LANE, BF16)])


def _dot_nt(a, b):
    return lax.dot_general(a, b, (((1,), (1,)), ((), ())), preferred_element_type=F32)


def _dot_tn(a, b):
    return lax.dot_general(a, b, (((0,), (0,)), ((), ())), preferred_element_type=F32)


def _dot(a, b):
    return jnp.dot(a, b, preferred_element_type=F32)


def _attn_params(s, t, n_res_f32, n_res_bf16, ride=False):
    blocks = n_res_f32 * _nbytes((s, LANE), F32) + n_res_bf16 * _nbytes((s, LANE), BF16) + 6 * _nbytes((t, LANE), F32)
    return pltpu.CompilerParams(dimension_semantics=("arbitrary" if ride else "parallel", "arbitrary"),
                                vmem_limit_bytes=_vmem_limit(blocks, 12 * _nbytes((t, t), F32)))


def _mla_fwd(q, k, v, t, tk, shards=()):
    s, hw = q.shape
    heads, nq, r = hw // LANE, s // t, t // tk
    ng = len(shards)

    def body(q_ref, k_ref, v_ref, *rest):
        o_ref, l_ref = rest[ng], rest[ng + 1]
        h, i = pl.program_id(0), pl.program_id(1)
        if ng:
            gather = _GatherPhases(rest[:ng], rest[ng + 2:2 * ng + 2], *rest[2 * ng + 2:])
            pl.when((h == 0) & (i == 0))(gather.send)
            pl.when((h == heads // 2) & (i == 0))(gather.forward)
        qv = q_ref[...]

        def step(j, carry, off):
            m, l, acc = carry
            sl = pl.ds(pl.multiple_of(j * tk, tk), tk)
            sc = _dot_nt(qv, k_ref[sl, :])
            if off is not None:
                row = lax.broadcasted_iota(jnp.int32, (t, tk), 0)
                col = lax.broadcasted_iota(jnp.int32, (t, tk), 1)
                sc = jnp.where(col + off <= row, sc, -1e30)
            m_new = jnp.maximum(m, jnp.max(sc, axis=1, keepdims=True))
            p = jnp.exp2(sc - m_new)
            alpha = jnp.exp2(m - m_new)
            l = alpha * l + jnp.sum(p, axis=1, keepdims=True)
            acc = alpha * acc + _dot(p.astype(BF16), v_ref[sl, :])
            return m_new, l, acc

        init = (jnp.full((t, 1), -1e30, F32), jnp.zeros((t, 1), F32), jnp.zeros((t, LANE), F32))
        carry = lax.fori_loop(0, i * r, lambda j, c: step(j, c, None), init)
        for jj in range(r):
            carry = step(i * r + jj, carry, jj * tk)
        m, l, acc = carry
        o_ref[...] = (acc / l).astype(o_ref.dtype)
        l_ref[0] = m + jnp.log2(l)
        if ng:
            pl.when((h == heads - 1) & (i == nq - 1))(gather.finish)

    blk = pl.BlockSpec((t, LANE), lambda h, i: (i, h))
    res = pl.BlockSpec((s, LANE), lambda h, i: (0, h))
    outs = pl.pallas_call(
        body, name="mla_fwd", grid=(heads, nq), in_specs=[blk, res, res] + [ANY] * ng,
        out_specs=[blk, pl.BlockSpec((1, t, 1), lambda h, i: (h, i, 0))] + [ANY] * ng,
        out_shape=[jax.ShapeDtypeStruct((s, hw), BF16), jax.ShapeDtypeStruct((heads, s, 1), F32)]
        + [jax.ShapeDtypeStruct((N_CHIPS,) + sh.shape, sh.dtype) for sh in shards],
        scratch_shapes=_dma_sems(6 * ng) if ng else [],
        compiler_params=_attn_params(s, t, 0, 2, ride=ng > 0),
    )(q, k, v, *shards)
    return outs[0], outs[1], list(outs[2:])


def _mla_bwd(q, k, v, o, do, lse, t, tk, parts=()):
    s, hw = q.shape
    heads, nq, r = hw // LANE, s // t, t // tk
    ns = len(parts)

    def body(q_ref, k_ref, v_ref, o_ref, do_ref, l_ref, *rest):
        dq_ref, dk_ref, dv_ref = rest[ns:ns + 3]
        h, i = pl.program_id(0), pl.program_id(1)
        finish = _ride_along(_scatter_copies, ns, (rest[:ns], rest[ns + 3:2 * ns + 3], *rest[2 * ns + 3:]),
                             (h == 0) & (i == 0), (h == heads - 1) & (i == nq - 1))

        @pl.when(i == 0)
        def _():
            dk_ref[...] = jnp.zeros_like(dk_ref)
            dv_ref[...] = jnp.zeros_like(dv_ref)

        qv, dov, lv = q_ref[...], do_ref[...], l_ref[0]
        dlt = jnp.sum(dov.astype(F32) * o_ref[...].astype(F32), axis=1, keepdims=True)

        def step(j, dq, off):
            sl = pl.ds(pl.multiple_of(j * tk, tk), tk)
            kv, vv = k_ref[sl, :], v_ref[sl, :]
            p = jnp.exp2(_dot_nt(qv, kv) - lv)
            if off is not None:
                row = lax.broadcasted_iota(jnp.int32, (t, tk), 0)
                col = lax.broadcasted_iota(jnp.int32, (t, tk), 1)
                p = jnp.where(col + off <= row, p, 0.0)
            ds = (p * (_dot_nt(dov, vv) - dlt)).astype(BF16)
            dk_ref[sl, :] += _dot_tn(ds, qv) * (1.0 / LOG2E)
            dv_ref[sl, :] += _dot_tn(p.astype(BF16), dov)
            return dq + _dot(ds, kv)

        dq = lax.fori_loop(0, i * r, lambda j, c: step(j, c, None), jnp.zeros((t, LANE), F32))
        for jj in range(r):
            dq = step(i * r + jj, dq, jj * tk)
        dq_ref[...] = dq * MLA_SCALE
        finish()

    blk = pl.BlockSpec((t, LANE), lambda h, i: (i, h))
    res = pl.BlockSpec((s, LANE), lambda h, i: (0, h))
    full = jax.ShapeDtypeStruct((s, hw), F32)
    outs = pl.pallas_call(
        body, name="mla_bwd", grid=(heads, nq),
        in_specs=[blk, res, res, blk, blk, pl.BlockSpec((1, t, 1), lambda h, i: (h, i, 0))] + [ANY] * ns,
        out_specs=[blk, res, res] + [ANY] * ns, out_shape=[full, full, full] + _scatter_shapes(parts),
        scratch_shapes=_dma_sems(3 * ns) if ns else [],
        compiler_params=_attn_params(s, t, 2, 2, ride=ns > 0),
    )(q, k, v, o, do, lse, *parts)
    return outs[0], outs[1], outs[2], list(outs[3:])


def _sb_logits(qv, kv, keep, upper):
    z = _dot_nt(qv, kv)
    e = jnp.exp(-jnp.abs(z))
    l1p = jnp.log(1.0 + e)
    lb = jnp.minimum(z, 0.0) - l1p
    lo = -jnp.maximum(z, 0.0) - l1p
    if keep is not None:
        lo = jnp.where(keep, lo, 0.0)
    hi = lo.astype(BF16)
    rem = (lo - hi.astype(F32)).astype(BF16)
    suf = _dot(hi, upper) + _dot(rem, upper)
    return z, e, lb, lo, suf


def _tri(t, inclusive):
    row = lax.broadcasted_iota(jnp.int32, (t, t), 0)
    col = lax.broadcasted_iota(jnp.int32, (t, t), 1)
    return jnp.where((row >= col) if inclusive else (row > col), 1.0, 0.0).astype(BF16)


SB_CHAINS = 4
SB_FIRST = 2


def _sb_first_tile(b, t):
    start = jnp.maximum(b - (SB_FIRST - 1), 0) * t
    row = lax.broadcasted_iota(jnp.int32, (t, SB_FIRST * t), 0)
    col = lax.broadcasted_iota(jnp.int32, (t, SB_FIRST * t), 1)
    return pl.ds(pl.multiple_of(start, t), SB_FIRST * t), col + start < row + b * t


def _sb_walk(i, first, carries_of):
    n = SB_CHAINS
    carries = [first(c) for c in range(n)]
    width = len(carries[0])

    def alive(carry):
        return jnp.max(carry[0]) >= SB_ZERO_LOG

    def split(st):
        return [tuple(st[1 + c * width:1 + (c + 1) * width]) for c in range(n)]

    def live(st):
        any_alive = alive(split(st)[0])
        for cr in split(st)[1:]:
            any_alive = any_alive | alive(cr)
        return (st[0] <= n * i) & any_alive

    def more(st):
        out = (st[0] + 1,)
        for c, cr in enumerate(split(st)):
            out += tuple(carries_of(c, st[0], cr))
        return out

    st = lax.while_loop(live, more, (jnp.int32(SB_FIRST),) + tuple(x for cr in carries for x in cr))
    jj, carries = st[0], split(st)
    for c in range(1, n):
        def live_c(s2, c=c):
            return (s2[0] <= n * i + c) & alive(s2[1:])

        def more_c(s2, c=c):
            return (s2[0] + 1,) + tuple(carries_of(c, s2[0], s2[1:]))

        carries[c] = lax.while_loop(live_c, more_c, (jj,) + tuple(carries[c]))[1:]
    return carries


def _sb_fwd(proj, t):
    s = proj.shape[0]
    heads, nq, n = SB_HEADS, s // t, SB_CHAINS

    def body(q_ref, k_ref, v_ref, o_ref):
        i = pl.program_id(1)
        upper, upper_first = _tri(t, False), _tri(SB_FIRST * t, False)
        qs = [(q_ref[c * t:(c + 1) * t, :] * SB_SCALE).astype(BF16) for c in range(n)]

        def first(c):
            sl, keep = _sb_first_tile(n * i + c, t)
            _, _, lb, lo, suf = _sb_logits(qs[c], k_ref[sl, :].astype(BF16), keep, upper_first)
            a = jnp.where(keep, jnp.exp(lb + suf), 0.0)
            return jnp.sum(lo, axis=1, keepdims=True), _dot(a.astype(BF16), v_ref[sl, :].astype(BF16))

        def step(c, jj, carry):
            run, acc = carry
            sl = pl.ds(pl.multiple_of((n * i + c - jj) * t, t), t)
            _, _, lb, lo, suf = _sb_logits(qs[c], k_ref[sl, :].astype(BF16), None, upper)
            a = jnp.exp(lb + suf + run)
            acc = acc + _dot(a.astype(BF16), v_ref[sl, :].astype(BF16))
            return run + jnp.sum(lo, axis=1, keepdims=True), acc

        carries = _sb_walk(i, first, step)
        for c in range(n):
            o_ref[c * t:(c + 1) * t, :] = carries[c][1]

    return pl.pallas_call(
        body, name="sb_fwd", grid=(heads, nq // n),
        in_specs=[pl.BlockSpec((n * t, LANE), lambda h, i: (i, C_SBQ // LANE + h)),
                  pl.BlockSpec((s, LANE), lambda h, i: (0, C_SBK // LANE + h)),
                  pl.BlockSpec((s, LANE), lambda h, i: (0, C_SBV // LANE + h))],
        out_specs=pl.BlockSpec((n * t, LANE), lambda h, i: (i, h)),
        out_shape=jax.ShapeDtypeStruct((s, heads * LANE), F32),
        compiler_params=_attn_params(s, n * t, 0, 2),
    )(proj, proj, proj)


def _sb_bwd(proj, o, do, t, stacked=()):
    s = proj.shape[0]
    heads, nq, n = SB_HEADS, s // t, SB_CHAINS
    nx = len(stacked)

    def body(q_ref, k_ref, v_ref, o_ref, do_ref, *rest):
        dq_ref, dk_ref, dv_ref = rest[nx:nx + 3]
        hd, i = pl.program_id(0), pl.program_id(1)
        finish = _ride_along(_exchange_copies, nx, (rest[:nx], rest[nx + 3:2 * nx + 3], *rest[2 * nx + 3:]),
                             (hd == 0) & (i == 0), (hd == heads - 1) & (i == nq // n - 1))

        @pl.when(i == 0)
        def _():
            dk_ref[...] = jnp.zeros_like(dk_ref)
            dv_ref[...] = jnp.zeros_like(dv_ref)

        rows = [slice(c * t, (c + 1) * t) for c in range(n)]
        qs = [(q_ref[r, :] * SB_SCALE).astype(BF16) for r in rows]
        dos = [do_ref[r, :] for r in rows]
        totals = [jnp.sum(dos[c].astype(F32) * o_ref[rows[c], :], axis=1, keepdims=True) for c in range(n)]
        tris = {1: (_tri(t, False), _tri(t, True)), SB_FIRST: (_tri(SB_FIRST * t, False), _tri(SB_FIRST * t, True))}

        def tile(c, sl, keep, blocks, carry):
            run, g, dq = carry
            qv, dov = qs[c], dos[c]
            upper, upper_incl = tris[blocks]
            kv, vv = k_ref[sl, :].astype(BF16), v_ref[sl, :].astype(BF16)
            z, e, lb, lo, suf = _sb_logits(qv, kv, keep, upper)
            tail = suf + run
            a = jnp.exp(lb + tail)
            if keep is not None:
                a = jnp.where(keep, a, 0.0)
            ab = a.astype(BF16)
            gr = ab.astype(F32) * _dot_nt(dov, vv)
            ghi = gr.astype(BF16)
            grem = (gr - ghi.astype(F32)).astype(BF16)
            before = totals[c] - g - (_dot(ghi, upper_incl) + _dot(grem, upper_incl))
            before = jnp.where(tail < SB_ZERO_LOG, 0.0, before)
            r = 1.0 / (1.0 + e)
            pos = z >= 0.0
            dz = r * (gr * jnp.where(pos, e, 1.0) - before * jnp.where(pos, 1.0, e))
            if keep is not None:
                dz = jnp.where(keep, dz, 0.0)
            dzb = dz.astype(BF16)
            dk_ref[sl, :] += _dot_tn(dzb, qv)
            dv_ref[sl, :] += _dot_tn(ab, dov)
            return (run + jnp.sum(lo, axis=1, keepdims=True), g + jnp.sum(gr, axis=1, keepdims=True),
                    dq + _dot(dzb, kv))

        zero = jnp.zeros((t, 1), F32)
        init = (zero, zero, jnp.zeros((t, LANE), F32))

        def first(c):
            sl, keep = _sb_first_tile(n * i + c, t)
            return tile(c, sl, keep, SB_FIRST, init)

        def step(c, jj, carry):
            return tile(c, pl.ds(pl.multiple_of((n * i + c - jj) * t, t), t), None, 1, carry)

        carries = _sb_walk(i, first, step)
        for c in range(n):
            dq_ref[rows[c], :] = carries[c][2] * SB_SCALE
        finish()

    blk = pl.BlockSpec((n * t, LANE), lambda h, i: (i, h))
    res = pl.BlockSpec((s, LANE), lambda h, i: (0, h))
    full = jax.ShapeDtypeStruct((s, heads * LANE), F32)
    outs = pl.pallas_call(
        body, name="sb_bwd", grid=(heads, nq // n),
        in_specs=[pl.BlockSpec((n * t, LANE), lambda h, i: (i, C_SBQ // LANE + h)),
                  pl.BlockSpec((s, LANE), lambda h, i: (0, C_SBK // LANE + h)),
                  pl.BlockSpec((s, LANE), lambda h, i: (0, C_SBV // LANE + h)), blk, blk] + [ANY] * nx,
        out_specs=[blk, res, res] + [ANY] * nx, out_shape=[full, full, full] + _exchange_shapes(stacked),
        scratch_shapes=_dma_sems(nx) if nx else [],
        compiler_params=_attn_params(s, n * t, 2, 2, ride=nx > 0),
    )(proj, proj, proj, o, do, *stacked)
    return outs[0], outs[1], outs[2], list(outs[3:])


def _xattn_probs(qh, kh):
    sc = _dot_nt(qh, kh) * (1.0 / math.sqrt(X_HEAD_DIM))
    p = jnp.exp(sc - jnp.max(sc, axis=1, keepdims=True))
    return p / jnp.sum(p, axis=1, keepdims=True)


def _xattn_fwd(xq, xkv):
    rows = xq.shape[0]
    w = X_HEADS * X_HEAD_DIM

    def body(i, ins, fulls, outs, accs):
        for h in range(X_HEADS):
            sl = slice(h * LANE, (h + 1) * LANE)
            p = _xattn_probs(ins[0][:, sl], fulls[0][:, sl])
            outs[0][:, sl] = _dot(p.astype(BF16), fulls[0][:, w + h * LANE:w + (h + 1) * LANE]).astype(BF16)

    return _rowwise(body, name="xattn_fwd", rows=rows, tr=512, row_ins=[(xq, w, 0)], full_ins=[xkv],
                    row_outs=[(w, BF16)])[0]


def _xattn_bwd(xq, xkv, dxo):
    rows = xq.shape[0]
    w = X_HEADS * X_HEAD_DIM

    def body(i, ins, fulls, outs, accs):
        _acc_init(i, accs)
        for h in range(X_HEADS):
            sl = slice(h * LANE, (h + 1) * LANE)
            slv = slice(w + h * LANE, w + (h + 1) * LANE)
            qh, kh, vh, doh = ins[0][:, sl], fulls[0][:, sl], fulls[0][:, slv], ins[1][:, sl]
            p = _xattn_probs(qh, kh)
            dp = _dot_nt(doh, vh)
            ds = (p * (dp - jnp.sum(p * dp, axis=1, keepdims=True)) * (1.0 / math.sqrt(X_HEAD_DIM))).astype(BF16)
            outs[0][:, sl] = _dot(ds, kh).astype(BF16)
            accs[0][:, sl] += _dot_tn(ds, qh)
            accs[0][:, slv] += _dot_tn(p.astype(BF16), doh)

    return _rowwise(body, name="xattn_bwd", rows=rows, tr=512, row_ins=[(xq, w, 0), (dxo, w, 0)], full_ins=[xkv],
                    row_outs=[(w, BF16)], acc_outs=[(xkv.shape, F32)])


def _gate_fwd(proj, pa, pb, b_gate):
    rows = proj.shape[0]

    def body(i, ins, fulls, outs, accs):
        sa = _sigmoid(ins[0][...].astype(F32) + fulls[0][0:1, :])
        sb = _sigmoid(ins[1][...].astype(F32) + fulls[0][1:2, :])
        outs[0][...] = (sa * ins[2][...] + sb * ins[3][...]).astype(BF16)

    return _rowwise(body, name="gate_fwd", rows=rows, tr=512,
                    row_ins=[(proj, D_MODEL, C_GA // D_MODEL), (proj, D_MODEL, C_GB // D_MODEL), (pa, D_MODEL, 0),
                             (pb, D_MODEL, 0)],
                    full_ins=[b_gate], row_outs=[(D_MODEL, BF16)])[0]


def _gate_bwd(proj, pa, pb, b_gate, dm):
    rows = proj.shape[0]

    def body(i, ins, fulls, outs, accs):
        _acc_init(i, accs)
        d = ins[4][...]
        sa = _sigmoid(ins[0][...].astype(F32) + fulls[0][0:1, :])
        sb = _sigmoid(ins[1][...].astype(F32) + fulls[0][1:2, :])
        dga = d * ins[2][...] * sa * (1.0 - sa)
        dgb = d * ins[3][...] * sb * (1.0 - sb)
        outs[0][...] = (d * sa).astype(BF16)
        outs[1][...] = (d * sb).astype(BF16)
        outs[2][...] = dga.astype(BF16)
        outs[3][...] = dgb.astype(BF16)
        accs[0][0:1, :] += jnp.sum(dga, axis=0, keepdims=True)
        accs[0][1:2, :] += jnp.sum(dgb, axis=0, keepdims=True)

    return _rowwise(body, name="gate_bwd", rows=rows, tr=512,
                    row_ins=[(proj, D_MODEL, C_GA // D_MODEL), (proj, D_MODEL, C_GB // D_MODEL), (pa, D_MODEL, 0),
                             (pb, D_MODEL, 0), (dm, D_MODEL, 0)],
                    full_ins=[b_gate], row_outs=[(D_MODEL, BF16)] * 4, acc_outs=[((2, D_MODEL), F32)])


def _loss_head(x3, target, g_final):
    rows = x3.shape[0]

    def body(i, ins, fulls, outs, accs):
        _acc_init(i, accs)
        xv, g = ins[0][...], fulls[0][...]
        d = _rms(xv, g) - ins[1][...]
        dx, dg = _rms_bwd(xv, g, d * (1.0 / D_MODEL))
        outs[0][...] = dx
        accs[0][...] += dg
        accs[1][...] += jnp.sum(d * d, axis=0, keepdims=True)

    return _rowwise(body, name="loss_head", rows=rows, tr=512, row_ins=[(x3, D_MODEL, 0), (target, D_MODEL, 0)],
                    full_ins=[g_final], row_outs=[(D_MODEL, F32)], acc_outs=[((1, D_MODEL), F32), ((1, D_MODEL), F32)])


def _adamw(w, g, m, v, name):
    rows, c = w.shape

    def body(i, ins, fulls, outs, accs):
        wv, gv = ins[0][...], ins[1][...]
        mn = ADAM_B1 * ins[2][...] + (1.0 - ADAM_B1) * gv
        vn = ADAM_B2 * ins[3][...] + (1.0 - ADAM_B2) * jnp.square(gv)
        m_hat = mn / (1.0 - ADAM_B1 ** ADAM_STEP)
        v_hat = vn / (1.0 - ADAM_B2 ** ADAM_STEP)
        outs[0][...] = -ADAM_LR * (m_hat / (jnp.sqrt(v_hat) + ADAM_EPS) + ADAM_WD * wv)
        outs[1][...] = mn
        outs[2][...] = vn

    return _rowwise(body, name=name, rows=rows, tr=_row_tile(rows, 256), row_ins=[(a, c, 0) for a in (w, g, m, v)],
                    row_outs=[(c, F32)] * 3)


def _place():
    x, y, c = lax.axis_index("x"), lax.axis_index("y"), lax.axis_index("c")
    chips = [(1 - x, y), (x, 1 - y), (1 - x, 1 - y)]
    return x, y, c, chips


ANY = pl.BlockSpec(memory_space=pl.ANY)


def _remote(src, dst, send_sem, recv_sem, to):
    return pltpu.make_async_remote_copy(src_ref=src, dst_ref=dst, send_sem=send_sem, recv_sem=recv_sem,
                                        device_id=to, device_id_type=MESH)


def _dma_sems(n):
    return [pltpu.SemaphoreType.DMA((n,)), pltpu.SemaphoreType.DMA((n,))]


class _GatherPhases:
    def __init__(self, ins, outs, send_sems, recv_sems):
        self.ins, self.outs, self.send_sems, self.recv_sems = ins, outs, send_sems, recv_sems
        self.x, self.y, self.c, self.chips = _place()
        self.me = 2 * self.x + self.y

    def _copy(self, t, j, chip_idx, hlf, to, src=None):
        h = self.ins[t].shape[0] // 2
        dst = self.outs[t].at[chip_idx, pl.ds(hlf * h, h), :]
        return _remote(dst if src is None else src, dst, self.send_sems.at[6 * t + j], self.recv_sems.at[6 * t + j], to)

    def _sends(self):
        out = []
        for t in range(len(self.ins)):
            h = self.ins[t].shape[0] // 2
            for j, chip in enumerate(self.chips):
                out.append(self._copy(t, j, self.me, self.c, (*chip, self.c), src=self.ins[t].at[pl.ds(self.c * h, h), :]))
        return out

    def _forwards(self):
        return [self._copy(t, 3 + j, 2 * chip[0] + chip[1], self.c, (self.x, self.y, 1 - self.c))
                for t in range(len(self.ins)) for j, chip in enumerate(self.chips)]

    def send(self):
        for cp in self._sends():
            cp.start()

    def forward(self):
        here = (self.x, self.y, self.c)
        landed = [self._copy(t, j, 2 * chip[0] + chip[1], self.c, here)
                  for t in range(len(self.ins)) for j, chip in enumerate(self.chips)]
        for arrival, fwd in zip(landed, self._forwards()):
            arrival.wait_recv()
            fwd.start()

    def finish(self):
        here = (self.x, self.y, self.c)
        for t in range(len(self.ins)):
            for j, chip in enumerate(self.chips):
                self._copy(t, 3 + j, 2 * chip[0] + chip[1], 1 - self.c, here).wait_recv()
        for cp in self._sends() + self._forwards():
            cp.wait_send()


def _all_gather_weights(shards):
    n = len(shards)

    def body(*refs):
        gather = _GatherPhases(refs[:n], refs[n:2 * n], *refs[2 * n:])
        gather.send()
        gather.forward()
        gather.finish()

    return pl.pallas_call(
        body, name="all_gather_weights", in_specs=[ANY] * n, out_specs=[ANY] * n,
        out_shape=[jax.ShapeDtypeStruct((N_CHIPS,) + s.shape, s.dtype) for s in shards],
        scratch_shapes=_dma_sems(6 * n),
    )(*shards)


def _exchange_copies(ins, outs, send_sems, recv_sems):
    x, y, c, _ = _place()
    cps = []
    for t in range(len(ins)):
        h = ins[t].shape[1] // 2
        cps.append(_remote(ins[t].at[:, pl.ds((1 - c) * h, h), :], outs[t], send_sems.at[t], recv_sems.at[t],
                           (x, y, 1 - c)))
    return cps


def _exchange_shapes(stacked):
    return [jax.ShapeDtypeStruct((N_CHIPS, s.shape[1] // 2, s.shape[2]), s.dtype) for s in stacked]


def _scatter_copies(ins, outs, send_sems, recv_sems):
    x, y, c, chips = _place()
    return [_remote(ins[t].at[2 * chip[0] + chip[1]], outs[t].at[j], send_sems.at[3 * t + j], recv_sems.at[3 * t + j],
                    (*chip, c)) for t in range(len(ins)) for j, chip in enumerate(chips)]


def _scatter_shapes(parts):
    return [jax.ShapeDtypeStruct((N_CHIPS - 1,) + s.shape[1:], s.dtype) for s in parts]


def _ride_along(copies_of, n, refs, first, last):
    if not n:
        return lambda: None

    def start():
        for cp in copies_of(*refs):
            cp.start()

    def wait():
        for cp in copies_of(*refs):
            cp.wait()

    pl.when(first)(start)
    return lambda: pl.when(last)(wait)


def _copy_call(copies_of, arrs, out_shape, sems_per, name):
    n = len(arrs)

    def body(*refs):
        cps = copies_of(refs[:n], refs[n:2 * n], *refs[2 * n:])
        for cp in cps:
            cp.start()
        for cp in cps:
            cp.wait()

    return pl.pallas_call(body, name=name, in_specs=[ANY] * n, out_specs=[ANY] * n, out_shape=out_shape,
                          scratch_shapes=_dma_sems(sems_per * n))(*arrs)


def _pair_exchange_grads(stacked, name):
    return _copy_call(_exchange_copies, stacked, _exchange_shapes(stacked), 1, name)


def _chip_scatter(parts, name):
    return _copy_call(_scatter_copies, parts, _scatter_shapes(parts), 3, name)


def _pair_exchange_halves(shards):
    n = len(shards)

    def body(*refs):
        bufs = refs[n:2 * n]
        send_sems, recv_sems = refs[2 * n:]
        x, y, c, _ = _place()
        cps = []
        for t in range(n):
            h = bufs[t].shape[0] // 2
            rows = bufs[t].at[pl.ds(c * h, h), :]
            cps.append(_remote(rows, rows, send_sems.at[t], recv_sems.at[t], (x, y, 1 - c)))
            cps[-1].start()
        for cp in cps:
            cp.wait()

    return pl.pallas_call(
        body, name="pair_exchange_halves", in_specs=[ANY] * n, out_specs=[ANY] * n,
        out_shape=[jax.ShapeDtypeStruct(s.shape, s.dtype) for s in shards],
        input_output_aliases={t: t for t in range(n)},
        scratch_shapes=_dma_sems(n),
    )(*shards)


def _pair_sum(gs, recv, place, name):
    _, r, cols = gs.shape
    h = r // 2

    def kern(p_ref, a_ref, b_ref, o_ref):
        o_ref[...] = (a_ref[...] + b_ref[...]).astype(BF16)

    blk = lambda f: pl.BlockSpec((1, h, cols), f)
    return pl.pallas_call(
        kern, name=name,
        grid_spec=pltpu.PrefetchScalarGridSpec(
            num_scalar_prefetch=1, grid=(N_CHIPS,),
            in_specs=[blk(lambda d, p: (d, p[1], 0)), blk(lambda d, p: (d, 0, 0))],
            out_specs=blk(lambda d, p: (d, 0, 0))),
        out_shape=jax.ShapeDtypeStruct((N_CHIPS, h, cols), BF16),
        compiler_params=pltpu.CompilerParams(dimension_semantics=("arbitrary",),
                                             vmem_limit_bytes=_vmem_limit(3 * _nbytes((h, cols), F32), 0)),
    )(place, gs, recv)


def _chip_sum(gs, recv, got, place, name):
    _, r, cols = gs.shape
    h = r // 2

    def kern(p_ref, a_ref, b_ref, g0, g1, g2, o_ref):
        own = a_ref[0] + b_ref[0]
        o_ref[...] = ((own + g0[0].astype(F32)) + g1[0].astype(F32)) + g2[0].astype(F32)

    blk = lambda f: pl.BlockSpec((1, h, cols), f)
    return pl.pallas_call(
        kern, name=name,
        grid_spec=pltpu.PrefetchScalarGridSpec(
            num_scalar_prefetch=1, grid=(1,),
            in_specs=[blk(lambda i, p: (p[0], p[1], 0)), blk(lambda i, p: (p[0], 0, 0)), blk(lambda i, p: (0, 0, 0)),
                      blk(lambda i, p: (1, 0, 0)), blk(lambda i, p: (2, 0, 0))],
            out_specs=pl.BlockSpec((h, cols), lambda i, p: (p[1], 0))),
        out_shape=jax.ShapeDtypeStruct((r, cols), F32),
        compiler_params=pltpu.CompilerParams(dimension_semantics=("arbitrary",),
                                             vmem_limit_bytes=_vmem_limit(5 * _nbytes((h, cols), F32), 0)),
    )(place, gs, recv, got, got, got)


def _all_reduce_small(vec, name):
    r, cols = vec.shape

    def body(in_ref, out_ref, gath, send_sems, recv_sems):
        x, y, c, _ = _place()
        me = 4 * x + 2 * y + c
        gath[me] = in_ref[...]
        sends = []
        for k in range(1, 8):
            to = (x ^ (k >> 2), y ^ ((k >> 1) & 1), c ^ (k & 1))
            cp = pltpu.make_async_remote_copy(src_ref=in_ref, dst_ref=gath.at[me], send_sem=send_sems.at[k - 1],
                                              recv_sem=recv_sems.at[k - 1], device_id=to, device_id_type=MESH)
            cp.start()
            sends.append(cp)
        for k in range(1, 8):
            peer = me ^ k
            pltpu.make_async_remote_copy(src_ref=in_ref, dst_ref=gath.at[peer], send_sem=send_sems.at[k - 1],
                                         recv_sem=recv_sems.at[k - 1], device_id=(x, y, c),
                                         device_id_type=MESH).wait_recv()
        for cp in sends:
            cp.wait_send()
        acc = gath[0]
        for d in range(1, 8):
            acc = acc + gath[d]
        out_ref[...] = acc

    vm = pl.BlockSpec(memory_space=pltpu.VMEM)
    return pl.pallas_call(
        body, name=name, in_specs=[vm], out_specs=vm,
        out_shape=jax.ShapeDtypeStruct((r, cols), F32),
        scratch_shapes=[pltpu.VMEM((8, r, cols), F32), pltpu.SemaphoreType.DMA((7,)), pltpu.SemaphoreType.DMA((7,))],
    )(vec)


def _pad_heads(w, heads, dim, axis):
    shp = w.shape[:axis] + (heads, dim) + w.shape[axis + 1:]
    pad = [(0, 0)] * len(shp)
    pad[axis + 1] = (0, LANE - dim)
    w = jnp.pad(w.reshape(shp), pad)
    return w.reshape(w.shape[:axis] + (heads * LANE,) + w.shape[axis + 2:])


def _unpad_heads(w, heads, dim, axis):
    shp = w.shape[:axis] + (heads, LANE) + w.shape[axis + 1:]
    w = lax.slice_in_dim(w.reshape(shp), 0, dim, axis=axis + 1)
    return w.reshape(w.shape[:axis] + (heads * dim,) + w.shape[axis + 2:])


def _w_in_layout(w_in):
    kr = jnp.pad(w_in[:, 384:416], ((0, 0), (ROPE_LO, LANE - ROPE_LO - MLA_ROPE)))
    sb = lambda lo: _pad_heads(w_in[:, lo:lo + 512], SB_HEADS, SB_HEAD_DIM, 1)
    return jnp.concatenate([w_in[:, 1952:2976], w_in[:, 2976:4000], sb(416), sb(928), sb(1440), w_in[:, 0:256],
                            w_in[:, 256:384], kr], axis=1)


def _w_in_unlayout(d):
    sb = lambda lo: _unpad_heads(d[:, lo:lo + 1024], SB_HEADS, SB_HEAD_DIM, 1)
    return jnp.concatenate([d[:, C_CQ:C_CQ + 256], d[:, C_CKV:C_CKV + 128], d[:, C_KR + ROPE_LO:C_KR + ROPE_LO + MLA_ROPE],
                            sb(C_SBQ), sb(C_SBK), sb(C_SBV), d[:, C_GA:C_GA + 1024], d[:, C_GB:C_GB + 1024]], axis=1)


def _w_ukv_layout(w):
    w3 = w.reshape(MLA_KV_RANK, MLA_HEADS, MLA_NOPE + MLA_V)
    pad = lambda part: jnp.pad(part, ((0, 0), (0, 0), (0, LANE - part.shape[2]))).reshape(MLA_KV_RANK, MLA_HEADS * LANE)
    return jnp.concatenate([pad(w3[:, :, :MLA_NOPE]), pad(w3[:, :, MLA_NOPE:])], axis=1)


def _w_ukv_unlayout(d):
    hw = MLA_HEADS * LANE
    kpart = d[:, :hw].reshape(MLA_KV_RANK, MLA_HEADS, LANE)[:, :, :MLA_NOPE]
    vpart = d[:, hw:].reshape(MLA_KV_RANK, MLA_HEADS, LANE)[:, :, :MLA_V]
    return jnp.concatenate([kpart, vpart], axis=2).reshape(MLA_KV_RANK, MLA_HEADS * (MLA_NOPE + MLA_V))


def _shard_of(full, d, axis):
    n = full.shape[axis] // N_CHIPS
    return lax.slice_in_dim(full, d * n, (d + 1) * n, axis=axis)


def _local_step(x, mem, pos, target, w, t_mla, t_sb, late=None, reduce=None):
    s = x.shape[0]
    w = dict(w)
    win = _w_in_layout(w["w_in"])
    wuq = _pad_heads(w["w_uq"], MLA_HEADS, MLA_NOPE + MLA_ROPE, 1)
    wkv = _w_ukv_layout(w["w_ukv"])
    inv_freq = ROPE_THETA ** (-jnp.arange(0, MLA_ROPE, 2, dtype=F32) / MLA_ROPE)
    freq_lane = jnp.pad(jnp.concatenate([inv_freq, inv_freq]), (ROPE_LO, LANE - ROPE_LO - MLA_ROPE)).reshape(1, LANE)
    add = lambda accs, ex: (accs[0] + ex[0],)

    def add_norm(accs, ex):
        y = accs[0] + ex[0]
        return y, _rms(y, ex[1])

    tab = _rope_tables(pos.reshape(s, 1), freq_lane)
    h = _rms_fwd_call(x, w["g_mix"], "rms_mix")
    proj = _mm(h, [win], name="proj_in", tn=1408, out_dtypes=(BF16,))
    cqn, ckvn, krope = _mla_prep_fwd(proj, tab, w["g_q_lat"], w["g_kv_lat"])
    hw = MLA_HEADS * LANE
    qa = _mm(cqn, [wuq], name="q_up", row_extras=(tab,), out_dtypes=(BF16,),
             epilogue=lambda accs, ex: (_per_head(lambda t: _rope(t, ex[0]) * (MLA_SCALE * LOG2E), accs[0]),))
    ka = _mm(ckvn, [wkv[:, :hw]], name="k_up", row_extras=(krope,), out_dtypes=(BF16,),
             epilogue=lambda accs, ex: (_per_head(lambda t: t + ex[0], accs[0]),))
    va = _mm(ckvn, [wkv[:, hw:]], name="v_up", out_dtypes=(BF16,))
    o_a, lse, gathered = _mla_fwd(qa, ka, va, t_mla[0], t_mla[1], late[0] if late else ())
    if late:
        w.update(late[1](gathered))
    wa = _pad_heads(w["w_a_proj"], MLA_HEADS, MLA_V, 0)
    wb = _pad_heads(w["w_b_proj"], SB_HEADS, SB_HEAD_DIM, 0)
    o_b = _sb_fwd(proj, t_sb)
    pa = _mm(o_a, [wa], name="proj_a")
    pb = _mm(o_b, [wb], name="proj_b")
    merged = _gate_fwd(proj, pa, pb, w["b_gate"])
    x1, hx = _mm(merged, [w["w_o"]], name="proj_o", extras=(x,), consts=(w["g_x"],), epilogue=add_norm,
                 out_dtypes=(F32, BF16))
    mn = _rms_fwd_call(mem, w["g_mem"], "rms_mem")
    xq = _mm(hx, [w["w_xq"]], name="xq", out_dtypes=(BF16,))
    xkv = _mm(mn, [w["w_xkv"]], name="xkv", out_dtypes=(BF16,))
    xo = _xattn_fwd(xq, xkv)
    x2, hf = _mm(xo, [w["w_xo"]], name="proj_xo", extras=(x1,), consts=(w["g_ffn"],), epilogue=add_norm,
                 out_dtypes=(F32, BF16))

    def swiglu(accs, ex):
        a, b = accs
        return a, b, a * _sigmoid(a) * b

    ga, gu, hmid = _mm(hf, [w["w_gate"], w["w_up"]], name="ffn_up", epilogue=swiglu, out_dtypes=(BF16, BF16, BF16),
                       tm=512, tn=1408)
    x3 = _mm(hmid, [w["w_down"]], name="ffn_down", extras=(x2,), epilogue=add, tk=2816)

    dx3, dg_final, sq = _loss_head(x3, target, w["g_final"].reshape(1, D_MODEL))
    g = {"g_final": dg_final.reshape(D_MODEL)}

    def swiglu_bwd(accs, ex):
        dh, a, b = accs[0], ex[0].astype(F32), ex[1].astype(F32)
        sg = _sigmoid(a)
        return dh * b * sg * (1.0 + a * (1.0 - sg)), dh * a * sg

    da, db = _mm(dx3, [w["w_down"]], name="ffn_down_dx", tb=True, extras=(ga, gu), epilogue=swiglu_bwd,
                 out_dtypes=(BF16, BF16), tm=512, tn=1408, chunk=MM_CHUNK)
    g["w_down"] = _mm(hmid, [dx3], name="ffn_down_dw", ta=True, tm=1408)
    g["w_gate"] = _mm(hf, [da], name="ffn_gate_dw", ta=True, tn=1408)
    g["w_up"] = _mm(hf, [db], name="ffn_up_dw", ta=True, tn=1408)
    dhf = _mm(da, [w["w_gate"]], name="ffn_gate_dx", tb=True, tk=2816)
    dhf = _mm(db, [w["w_up"]], name="ffn_up_dx", tb=True, extras=(dhf,), epilogue=add, tk=2816,
              out_dtypes=(BF16,))
    dx2, g["g_ffn"] = _rms_bwd_call(x2, w["g_ffn"], dhf, dx3, "rms_ffn_bwd")

    dxo = _mm(dx2, [w["w_xo"]], name="proj_xo_dx", tb=True, out_dtypes=(BF16,))
    g["w_xo"] = _mm(xo, [dx2], name="proj_xo_dw", ta=True)
    dxq, dxkv = _xattn_bwd(xq, xkv, dxo)
    dhx = _mm(dxq, [w["w_xq"]], name="xq_dx", tb=True, out_dtypes=(BF16,))
    g["w_xq"] = _mm(hx, [dxq], name="xq_dw", ta=True)
    dmn = _mm(dxkv, [w["w_xkv"]], name="xkv_dx", tb=True)
    g["w_xkv"] = _mm(mn, [dxkv], name="xkv_dw", ta=True)
    dx1, g["g_x"] = _rms_bwd_call(x1, w["g_x"], dhx, dx2, "rms_x_bwd")
    _, g["g_mem"] = _rms_bwd_call(mem, w["g_mem"], dmn, None, "rms_mem_bwd")

    dmerged = _mm(dx1, [w["w_o"]], name="proj_o_dx", tb=True)
    g["w_o"] = _mm(merged, [dx1], name="proj_o_dw", ta=True)
    dpa, dpb, dga, dgb, g["b_gate"] = _gate_bwd(proj, pa, pb, w["b_gate"], dmerged)
    do_a = _mm(dpa, [wa], name="proj_a_dx", tb=True, out_dtypes=(BF16,))
    do_b = _mm(dpb, [wb], name="proj_b_dx", tb=True, out_dtypes=(BF16,))
    g["w_a_proj"] = _unpad_heads(_mm(o_a, [dpa], name="proj_a_dw", ta=True), MLA_HEADS, MLA_V, 0)
    g["w_b_proj"] = _unpad_heads(_mm(o_b, [dpb], name="proj_b_dw", ta=True), SB_HEADS, SB_HEAD_DIM, 0)

    stacked = [reduce[1](n, g[n]) for n in reduce[0]] if reduce else []
    dsq, dsk, dsv, recv = _sb_bwd(proj, o_b, do_b, t_sb, stacked)
    parts = [reduce[2](n, gs, rv) for n, gs, rv in zip(reduce[0], stacked, recv)] if reduce else []
    dqa, dka, dva, got = _mla_bwd(qa, ka, va, o_a, do_a, lse, t_mla[0], t_mla[2], parts)
    riding = dict(zip(reduce[0], zip(stacked, recv, got))) if reduce else {}
    dqp, dkvp, dkr = _mla_rope_bwd(dqa, dka, dva, tab)
    g["w_uq"] = _unpad_heads(_mm(cqn, [dqp], name="q_up_dw", ta=True), MLA_HEADS, MLA_NOPE + MLA_ROPE, 1)
    g["w_ukv"] = _w_ukv_unlayout(_mm(ckvn, [dkvp], name="kv_up_dw", ta=True))
    dcqn = _mm(dqp, [wuq], name="q_up_dx", tb=True)
    dckvn = _mm(dkvp, [wkv], name="kv_up_dx", tb=True)
    dcq, dckv, g["g_q_lat"], g["g_kv_lat"] = _mla_prep_bwd(proj, w["g_q_lat"], w["g_kv_lat"], dcqn, dckvn)

    dproj = jnp.concatenate([dga, dgb, dsq.astype(BF16), dsk.astype(BF16), dsv.astype(BF16), dcq, dckv, dkr], axis=1)
    g["w_in"] = _w_in_unlayout(_mm(h, [dproj], name="proj_in_dw", ta=True, tn=1408))
    dh = _mm(dproj, [win], name="proj_in_dx", tb=True, tk=2816, out_dtypes=(BF16,))
    grad_x, g["g_mix"] = _rms_bwd_call(x, w["g_mix"], dh, dx1, "rms_mix_bwd")
    return sq, grad_x, g, riding


def _small_pack(d):
    row5 = jnp.concatenate([d["g_q_lat"].reshape(-1), d["g_kv_lat"].reshape(-1), jnp.zeros((640,), F32)])
    rows = [d[n].reshape(-1) for n in ("g_mix", "g_x", "g_mem", "g_ffn", "g_final")] + [row5]
    return rows


def _small_unpack(p, like):
    out = {n: p[i].reshape(like[n].shape) for i, n in enumerate(("g_mix", "g_x", "g_mem", "g_ffn", "g_final"))}
    out["g_q_lat"] = p[5, 0:256].reshape(like["g_q_lat"].shape)
    out["g_kv_lat"] = p[5, 256:384].reshape(like["g_kv_lat"].shape)
    return out


def kernel(x, mem, positions, g_mix, w_in, b_gate, g_q_lat, w_uq, g_kv_lat, w_ukv, w_a_proj, w_b_proj, w_o, g_x, g_mem, w_xq, w_xkv, w_xo, g_ffn, w_gate, w_up, w_down, g_final, loss_target, m_g_mix, m_w_in, m_b_gate, m_g_q_lat, m_w_uq, m_g_kv_lat, m_w_ukv, m_w_a_proj, m_w_b_proj, m_w_o, m_g_x, m_g_mem, m_w_xq, m_w_xkv, m_w_xo, m_g_ffn, m_w_gate, m_w_up, m_w_down, m_g_final, v_g_mix, v_w_in, v_b_gate, v_g_q_lat, v_w_uq, v_g_kv_lat, v_w_ukv, v_w_a_proj, v_w_b_proj, v_w_o, v_g_x, v_g_mem, v_w_xq, v_w_xkv, v_w_xo, v_g_ffn, v_w_gate, v_w_up, v_w_down, v_g_final):
    given = dict(locals())
    names = [n for n, _, _ in MATS] + ["b_gate"] + list(SMALL)
    wts = {n: given[n] for n in names}
    mom = {n: given["m_" + n] for n in names}
    var = {n: given["v_" + n] for n in names}
    shard2d = {n: shp for n, shp, _ in MATS}
    shard2d["b_gate"] = B_GATE_SHARD
    cx, cy, cc = lax.axis_index("x"), lax.axis_index("y"), lax.axis_index("c")
    me = 2 * cx + cy
    place = jnp.stack([me, cc]).astype(jnp.int32)
    bcol = me * B_GATE_SHARD[1]

    own = [wts[n].reshape(shard2d[n]).astype(BF16) for n, _, _ in MATS]
    bias = (("b_gate", (BIAS_ROWS, B_GATE_SHARD[1]), 1),)
    own_bias = [jnp.pad(wts["b_gate"].reshape(B_GATE_SHARD), ((0, BIAS_ROWS - B_GATE_SHARD[0]), (0, 0)))]

    def assemble(mats, gathered, mine):
        out = {}
        for (n, shp, ax), g4, shard in zip(mats, gathered, mine):
            g4 = lax.dynamic_update_slice(g4, shard[None], (me, 0, 0))
            out[n] = g4.reshape(N_CHIPS * shp[0], shp[1]) if ax == 0 else jnp.concatenate(list(g4), axis=1)
        return out

    first = own[:N_EARLY] + own_bias
    full = assemble(MATS[:N_EARLY] + bias, _all_gather_weights(first), first)
    full["b_gate"] = full["b_gate"][0:B_GATE_SHARD[0]]
    late = (own[N_EARLY:], lambda gathered: assemble(MATS[N_EARLY:], gathered, own[N_EARLY:]))
    for n in SMALL:
        full[n] = wts[n].reshape(1, -1) if n != "g_final" else wts[n]

    axis_of = {n: ax for n, _, ax in MATS}
    stack = lambda n, g: jnp.stack([_shard_of(g, d, axis_of[n]) for d in range(N_CHIPS)])
    pair_sum = lambda n, gs, rv: _pair_sum(gs, rv, place, "pair_sum_" + n)
    behind = [n for n, _, _ in MATS[N_EARLY:]]
    sq, grad_x, grads, riding = _local_step(x[0], mem[0], positions[0], loss_target[0], full, t_mla=(1024, 1024, 1024), t_sb=256,
                                            late=late, reduce=(behind, stack, pair_sum))

    last = [n for n, _, _ in MATS[:N_EARLY]]
    stacked = [stack(n, grads[n]) for n in last]
    recv = _pair_exchange_grads(stacked, "pair_exchange_grads")
    got = _chip_scatter([pair_sum(n, gs, rv) for n, gs, rv in zip(last, stacked, recv)], "chip_scatter")
    riding.update(zip(last, zip(stacked, recv, got)))
    halves = [_chip_sum(*riding[n], place, "chip_sum_" + n) for n, _, _ in MATS]
    g_shard = dict(zip([n for n, _, _ in MATS], _pair_exchange_halves(halves)))

    small_rows = _small_pack({n: grads[n] for n in SMALL}) + [sq.reshape(-1), grads["b_gate"][0], grads["b_gate"][1]]
    small_rows += [jnp.zeros((D_MODEL,), F32)] * (SMALL_ROWS - len(small_rows))
    small = _all_reduce_small(jnp.stack(small_rows), "all_reduce_small")
    loss = (0.5 / D_MODEL) * jnp.sum(small[6])
    g_shard["b_gate"] = lax.dynamic_slice(small[7:9], (0, bcol), B_GATE_SHARD)

    out = {"grad": {}, "delta": {}, "m": {}, "v": {}}
    for n in [n for n, _, _ in MATS] + ["b_gate"]:
        shape = wts[n].shape
        r2 = lambda a: a.reshape(shard2d[n])
        d_n, m_n, v_n = _adamw(r2(wts[n]), g_shard[n], r2(mom[n]), r2(var[n]), "adamw_" + n)
        for key, a in (("grad", g_shard[n]), ("delta", d_n), ("m", m_n), ("v", v_n)):
            out[key][n] = a.reshape(shape)
    sp = lambda d: jnp.stack(_small_pack(d) + [jnp.zeros((D_MODEL,), F32)] * 2)
    delta_s, m_s, v_s = _adamw(sp(wts), small[0:8].at[6:8].set(0.0), sp(mom), sp(var), "adamw_small")
    for key, p in (("grad", small), ("delta", delta_s), ("m", m_s), ("v", v_s)):
        out[key].update(_small_unpack(p, wts))

    order = ["g_mix", "w_in", "b_gate", "g_q_lat", "w_uq", "g_kv_lat", "w_ukv", "w_a_proj", "w_b_proj", "w_o", "g_x",
             "g_mem", "w_xq", "w_xkv", "w_xo", "g_ffn", "w_gate", "w_up", "w_down", "g_final"]
    return (loss, grad_x[None], *[out[key][n] for key in ("grad", "delta", "m", "v") for n in order])
```

```python
import functools
import math

import jax
import jax.numpy as jnp
from jax import lax
from jax.experimental import pallas as pl
from jax.experimental.pallas import tpu as pltpu

F32 = jnp.float32
BF16 = jnp.bfloat16
MESH = pl.DeviceIdType.MESH

D_MODEL = 1024
MLA_HEADS = 8
MLA_Q_RANK = 256
MLA_KV_RANK = 128
MLA_NOPE = 64
MLA_ROPE = 32
MLA_V = 64
ROPE_THETA = 10000.0
SB_HEADS = 8
SB_HEAD_DIM = 64
X_HEADS = 4
X_HEAD_DIM = 128
EPS = 1e-6
ADAM_LR = 0.001
ADAM_B1 = 0.9
ADAM_B2 = 0.999
ADAM_EPS = 1e-08
ADAM_WD = 0.01
ADAM_STEP = 10

LANE = 128
LOG2E = 1.4426950408889634
MLA_SCALE = 1.0 / math.sqrt(MLA_NOPE + MLA_ROPE)
SB_SCALE = 1.0 / math.sqrt(SB_HEAD_DIM)
assert math.log2(SB_SCALE) == round(math.log2(SB_SCALE))
MM_CHUNK = 256
N_CHIPS = 4
VMEM_BYTES = 64 * 1024 * 1024

C_GA, C_GB, C_SBQ, C_SBK, C_SBV, C_CQ, C_CKV, C_KR = 0, 1024, 2048, 3072, 4096, 5120, 5376, 5504
ROPE_LO = MLA_NOPE
HALF = MLA_ROPE // 2

SB_ZERO_LOG = -104.0

MATS = (
    ("w_in", (1024, 1000), 1), ("w_uq", (256, 192), 1), ("w_ukv", (128, 256), 1), ("w_a_proj", (512, 256), 1),
    ("w_b_proj", (512, 256), 1), ("w_o", (256, 1024), 0), ("w_xq", (256, 512), 0), ("w_xkv", (256, 1024), 0),
    ("w_xo", (512, 256), 1), ("w_gate", (1024, 704), 1), ("w_up", (1024, 704), 1), ("w_down", (704, 1024), 0),
)
N_EARLY = 3
B_GATE_SHARD = (2, 256)
BIAS_ROWS = 16
SMALL = ("g_mix", "g_x", "g_mem", "g_ffn", "g_final", "g_q_lat", "g_kv_lat")
SMALL_ROWS = 16


def _vmem_limit(block_bytes, temp_bytes):
    est = 2 * block_bytes + temp_bytes + (4 << 20)
    return int(min(max(est, 16 << 20), VMEM_BYTES - (6 << 20)))


def _nbytes(shape, dtype):
    return math.prod(shape) * jnp.dtype(dtype).itemsize


def _row_tile(rows, cap):
    if rows <= cap:
        return rows
    return max(t for t in range(8, cap + 1, 8) if rows % t == 0)


def _tile(n, cap):
    if n <= cap:
        return n
    best = None
    for t in range(LANE, cap + 1, LANE):
        if n % t == 0:
            best = t
    assert best is not None, (n, cap)
    return best


def _mm(a, bs, *, name, ta=False, tb=False, extras=(), row_extras=(), consts=(), epilogue=None, out_dtypes=(F32,),
        tm=1024, tn=1024, tk=1024, chunk=None, ride=None):
    bs = tuple(bs)
    m, k = (a.shape[1], a.shape[0]) if ta else a.shape
    n = bs[0].shape[0] if tb else bs[0].shape[1]
    tm, tn, tk = _tile(m, tm), _tile(n, tn), _tile(k, tk)
    assert m % tm == 0 and n % tn == 0 and k % tk == 0
    nk = k // tk
    nb, ne, no = len(bs), len(extras) + len(row_extras) + len(consts), len(out_dtypes)
    dims = (((0,) if ta else (1,)), ((1,) if tb else (0,))), ((), ())
    if epilogue is None:
        epilogue = lambda accs, ex: (accs[0],)

    rn = len(ride[1]) if ride else 0
    n_acc = nb if nk > 1 else 0
    gi, gj = m // tm, n // tn

    def body(*refs):
        a_ref, b_refs, e_refs = refs[0], refs[1:1 + nb], refs[1 + nb:1 + nb + ne]
        base = 1 + nb + ne + rn
        o_refs, acc_refs = refs[base:base + no], refs[base + no + rn:base + no + rn + n_acc]
        step = [pl.program_id(d) for d in range(3)]
        finish = _ride_along(ride[0] if ride else None, rn,
                             (refs[base - rn:base], refs[base + no:base + no + rn], *refs[base + no + rn + n_acc:]),
                             (step[0] == 0) & (step[1] == 0) & (step[2] == 0),
                             (step[0] == gi - 1) & (step[1] == gj - 1) & (step[2] == nk - 1))
        if nk == 1:
            ch = chunk or tm
            bvs = [b_ref[...].astype(BF16) for b_ref in b_refs]
            for r0 in range(0, tm, ch):
                rows = slice(r0, r0 + ch)
                av = (a_ref[:, rows] if ta else a_ref[rows, :]).astype(BF16)
                accs = [lax.dot_general(av, bv, dims, preferred_element_type=F32) for bv in bvs]
                ex = [e[rows, :] for e in e_refs[:ne - len(consts)]] + [e[...] for e in e_refs[ne - len(consts):]]
                for o_ref, v in zip(o_refs, epilogue(accs, ex)):
                    o_ref[rows, :] = v.astype(o_ref.dtype)
            finish()
            return
        kk = step[2]

        @pl.when(kk == 0)
        def _():
            for acc in acc_refs:
                acc[...] = jnp.zeros_like(acc)

        av = a_ref[...].astype(BF16)
        for b_ref, acc in zip(b_refs, acc_refs):
            acc[...] += lax.dot_general(av, b_ref[...].astype(BF16), dims, preferred_element_type=F32)

        @pl.when(kk == nk - 1)
        def _():
            outs = epilogue([acc[...] for acc in acc_refs], [e[...] for e in e_refs])
            for o_ref, v in zip(o_refs, outs):
                o_ref[...] = v.astype(o_ref.dtype)

        finish()

    a_spec = pl.BlockSpec((tk, tm), lambda i, j, kk: (kk, i)) if ta else pl.BlockSpec((tm, tk), lambda i, j, kk: (i, kk))
    b_spec = pl.BlockSpec((tn, tk), lambda i, j, kk: (j, kk)) if tb else pl.BlockSpec((tk, tn), lambda i, j, kk: (kk, j))
    mn_spec = pl.BlockSpec((tm, tn), lambda i, j, kk: (i, j))
    blocks = (_nbytes((tm, tk), a.dtype) + sum(_nbytes((tk, tn), b.dtype) for b in bs)
              + sum(_nbytes((tm, tn), e.dtype) for e in extras) + sum(_nbytes((tm, tn), d) for d in out_dtypes)
              + sum(_nbytes((tm, e.shape[1]), e.dtype) for e in row_extras))
    temps = (nb + 4) * _nbytes((tm, tn), F32)
    outs = pl.pallas_call(
        body, name=name, grid=(m // tm, n // tn, nk),
        in_specs=[a_spec] + [b_spec] * nb + [mn_spec] * len(extras)
        + [pl.BlockSpec((tm, e.shape[1]), lambda i, j, kk: (i, 0)) for e in row_extras]
        + [pl.BlockSpec(e.shape, lambda i, j, kk: (0, 0)) for e in consts] + [ANY] * rn,
        out_specs=[mn_spec] * no + [ANY] * rn,
        out_shape=[jax.ShapeDtypeStruct((m, n), d) for d in out_dtypes] + (list(ride[2]) if ride else []),
        scratch_shapes=[pltpu.VMEM((tm, tn), F32) for _ in range(n_acc)] + (_dma_sems(ride[3] * rn) if ride else []),
        compiler_params=pltpu.CompilerParams(
            dimension_semantics=("arbitrary",) * 3 if ride else ("parallel", "parallel", "arbitrary"),
            vmem_limit_bytes=_vmem_limit(blocks, temps)),
    )(a, *bs, *extras, *row_extras, *consts, *(ride[1] if ride else ()))
    if ride:
        return (outs[0] if no == 1 else outs[:no]), list(outs[no:])
    return outs[0] if no == 1 else outs


def _rowwise(body, *, name, rows, tr, row_ins, full_ins=(), row_outs=(), acc_outs=(), ride=None):
    tr = min(tr, rows)
    assert rows % tr == 0
    n_ri, n_fi, n_ro, n_ao = len(row_ins), len(full_ins), len(row_outs), len(acc_outs)
    rn = len(ride[1]) if ride else 0

    def kern(*refs):
        i = pl.program_id(0)
        n_in, n_out = n_ri + n_fi, n_ro + n_ao
        outs = refs[n_in + rn:n_in + rn + n_out]
        finish = _ride_along(ride[0] if ride else None, rn,
                             (refs[n_in:n_in + rn], refs[n_in + rn + n_out:n_in + 2 * rn + n_out],
                              *refs[n_in + 2 * rn + n_out:]), i == 0, i == rows // tr - 1)
        body(i, refs[:n_ri], refs[n_ri:n_in], outs[:n_ro], outs[n_ro:])
        finish()

    in_specs = [pl.BlockSpec((tr, w), functools.partial(lambda i, c: (i, c), c=ci)) for _, w, ci in row_ins]
    in_specs += [pl.BlockSpec(f.shape, lambda i: (0, 0)) for f in full_ins]
    in_specs += [ANY] * rn
    out_specs = [pl.BlockSpec((tr, w), lambda i: (i, 0)) for w, _ in row_outs]
    out_specs += [pl.BlockSpec(s, lambda i: (0, 0)) for s, _ in acc_outs] + [ANY] * rn
    out_shape = [jax.ShapeDtypeStruct((rows, w), d) for w, d in row_outs]
    out_shape += [jax.ShapeDtypeStruct(s, d) for s, d in acc_outs] + (list(ride[2]) if ride else [])
    blocks = (sum(_nbytes((tr, w), a.dtype) for a, w, _ in row_ins) + sum(_nbytes(f.shape, f.dtype) for f in full_ins)
              + sum(_nbytes((tr, w), d) for w, d in row_outs) + sum(_nbytes(s, d) for s, d in acc_outs))
    widest = max([w for _, w, _ in row_ins] + [w for w, _ in row_outs])
    outs = pl.pallas_call(
        kern, name=name, grid=(rows // tr,), in_specs=in_specs, out_specs=out_specs, out_shape=out_shape,
        scratch_shapes=_dma_sems(ride[3] * rn) if ride else [],
        compiler_params=pltpu.CompilerParams(
            dimension_semantics=("arbitrary",) if acc_outs or ride else ("parallel",),
            vmem_limit_bytes=_vmem_limit(blocks, 8 * _nbytes((tr, widest), F32))),
    )(*[a for a, _, _ in row_ins], *full_ins, *(ride[1] if ride else ()))
    return outs


def _rms(x, g):
    r = lax.rsqrt(jnp.mean(x * x, axis=-1, keepdims=True) + EPS)
    return x * r * g


def _rms_bwd(x, g, dy):
    r = lax.rsqrt(jnp.mean(x * x, axis=-1, keepdims=True) + EPS)
    xh = x * r
    dxh = dy * g
    dx = r * (dxh - xh * jnp.mean(dxh * xh, axis=-1, keepdims=True))
    return dx, jnp.sum(dy * xh, axis=0, keepdims=True)


def _sigmoid(x):
    return 1.0 / (1.0 + jnp.exp(-x))


def _acc_init(i, refs):
    @pl.when(i == 0)
    def _():
        for r in refs:
            r[...] = jnp.zeros_like(r)


def _rms_fwd_call(x, g, name):
    rows, c = x.shape

    def body(i, ins, fulls, outs, accs):
        outs[0][...] = _rms(ins[0][...], fulls[0][...]).astype(BF16)

    return _rowwise(body, name=name, rows=rows, tr=512, row_ins=[(x, c, 0)], full_ins=[g], row_outs=[(c, BF16)])[0]


def _rms_bwd_call(x, g, dy, res, name, ride=None):
    rows, c = x.shape
    row_ins = [(x, c, 0), (dy, c, 0)] + ([(res, c, 0)] if res is not None else [])

    def body(i, ins, fulls, outs, accs):
        _acc_init(i, accs)
        dx, dg = _rms_bwd(ins[0][...], fulls[0][...], ins[1][...].astype(F32))
        if res is not None:
            dx = dx + ins[2][...]
        outs[0][...] = dx
        accs[0][...] += dg

    return _rowwise(body, name=name, rows=rows, tr=512, row_ins=row_ins, full_ins=[g], row_outs=[(c, F32)],
                    acc_outs=[((1, c), F32)], ride=ride)


def _rope_tables(pos_col, freq_lane):
    rows = pos_col.shape[0]

    def body(i, ins, fulls, outs, accs):
        ang = ins[0][...].astype(F32) * fulls[0][...]
        lane = lax.broadcasted_iota(jnp.int32, ang.shape, 1)
        cos, sin = jnp.cos(ang), jnp.sin(ang)
        first = (lane >= ROPE_LO) & (lane < ROPE_LO + HALF)
        second = (lane >= ROPE_LO + HALF) & (lane < ROPE_LO + MLA_ROPE)
        outs[0][:, 0:LANE] = jnp.where(first | second, cos, 1.0)
        outs[0][:, LANE:2 * LANE] = jnp.where(first, -sin, 0.0)
        outs[0][:, 2 * LANE:3 * LANE] = jnp.where(second, sin, 0.0)

    return _rowwise(body, name="rope_tables", rows=rows, tr=1024, row_ins=[(pos_col, 1, 0)], full_ins=[freq_lane],
                    row_outs=[(3 * LANE, F32)])[0]


def _rope(x, tab):
    return (x * tab[:, 0:LANE] + pltpu.roll(x, LANE - HALF, 1) * tab[:, LANE:2 * LANE]
            + pltpu.roll(x, HALF, 1) * tab[:, 2 * LANE:3 * LANE])


def _rope_t(dy, tab):
    return (dy * tab[:, 0:LANE] + pltpu.roll(dy * tab[:, LANE:2 * LANE], HALF, 1)
            + pltpu.roll(dy * tab[:, 2 * LANE:3 * LANE], LANE - HALF, 1))


def _mla_prep_fwd(proj, tab, g_q, g_kv):
    rows = proj.shape[0]

    def body(i, ins, fulls, outs, accs):
        outs[0][...] = _rms(ins[0][...].astype(F32), fulls[0][...]).astype(BF16)
        outs[1][...] = _rms(ins[1][...].astype(F32), fulls[1][...]).astype(BF16)
        outs[2][...] = _rope(ins[2][...].astype(F32), ins[3][...])

    return _rowwise(body, name="mla_prep_fwd", rows=rows, tr=512,
                    row_ins=[(proj, MLA_Q_RANK, C_CQ // MLA_Q_RANK), (proj, LANE, C_CKV // LANE),
                             (proj, LANE, C_KR // LANE), (tab, 3 * LANE, 0)],
                    full_ins=[g_q, g_kv], row_outs=[(MLA_Q_RANK, BF16), (MLA_KV_RANK, BF16), (LANE, F32)])


def _mla_prep_bwd(proj, g_q, g_kv, dcqn, dckvn):
    rows = proj.shape[0]

    def body(i, ins, fulls, outs, accs):
        _acc_init(i, accs)
        dcq, dgq = _rms_bwd(ins[0][...].astype(F32), fulls[0][...], ins[2][...])
        dckv, dgkv = _rms_bwd(ins[1][...].astype(F32), fulls[1][...], ins[3][...])
        outs[0][...] = dcq.astype(BF16)
        outs[1][...] = dckv.astype(BF16)
        accs[0][...] += dgq
        accs[1][...] += dgkv

    return _rowwise(body, name="mla_prep_bwd", rows=rows, tr=512,
                    row_ins=[(proj, MLA_Q_RANK, C_CQ // MLA_Q_RANK), (proj, LANE, C_CKV // LANE),
                             (dcqn, MLA_Q_RANK, 0), (dckvn, MLA_KV_RANK, 0)],
                    full_ins=[g_q, g_kv], row_outs=[(MLA_Q_RANK, BF16), (MLA_KV_RANK, BF16)],
                    acc_outs=[((1, MLA_Q_RANK), F32), ((1, MLA_KV_RANK), F32)])


def _per_head(fn, x):
    return jnp.concatenate([fn(x[:, h * LANE:(h + 1) * LANE]) for h in range(x.shape[1] // LANE)], axis=1)


def _mla_rope_bwd(dq, dk, dv, tab):
    rows = dq.shape[0]
    hw = MLA_HEADS * LANE

    def body(i, ins, fulls, outs, accs):
        t = ins[3][...]
        dkr = jnp.zeros((ins[0].shape[0], LANE), F32)
        for h in range(MLA_HEADS):
            sl = slice(h * LANE, (h + 1) * LANE)
            outs[0][:, sl] = _rope_t(ins[0][:, sl], t).astype(BF16)
            dkr = dkr + ins[1][:, sl]
        outs[1][:, 0:hw] = ins[1][...].astype(BF16)
        outs[1][:, hw:2 * hw] = ins[2][...].astype(BF16)
        lane = lax.broadcasted_iota(jnp.int32, dkr.shape, 1)
        dkr = jnp.where((lane >= ROPE_LO) & (lane < ROPE_LO + MLA_ROPE), dkr, 0.0)
        outs[2][...] = _rope_t(dkr, t).astype(BF16)

    return _rowwise(body, name="mla_rope_bwd", rows=rows, tr=512,
                    row_ins=[(dq, hw, 0), (dk, hw, 0), (dv, hw, 0), (tab, 3 * LANE, 0)],
                    row_outs=[(hw, BF16), (2 * hw, BF16), (LANE, BF16)])


def _dot_nt(a, b):
    return lax.dot_general(a, b, (((1,), (1,)), ((), ())), preferred_element_type=F32)


def _dot_tn(a, b):
    return lax.dot_general(a, b, (((0,), (0,)), ((), ())), preferred_element_type=F32)


def _dot(a, b):
    return jnp.dot(a, b, preferred_element_type=F32)


def _attn_params(s, t, n_res_f32, n_res_bf16, ride=False):
    blocks = n_res_f32 * _nbytes((s, LANE), F32) + n_res_bf16 * _nbytes((s, LANE), BF16) + 6 * _nbytes((t, LANE), F32)
    return pltpu.CompilerParams(dimension_semantics=("arbitrary" if ride else "parallel", "arbitrary"),
                                vmem_limit_bytes=_vmem_limit(blocks, 12 * _nbytes((t, t), F32)))


def _mla_fwd(q, k, v, t, tk, shards=()):
    s, hw = q.shape
    heads, nq, r = hw // LANE, s // t, t // tk
    ng = len(shards)

    def body(q_ref, k_ref, v_ref, *rest):
        o_ref, l_ref = rest[ng], rest[ng + 1]
        h, i = pl.program_id(0), pl.program_id(1)
        if ng:
            gather = _GatherPhases(rest[:ng], rest[ng + 2:2 * ng + 2], *rest[2 * ng + 2:])
            pl.when((h == 0) & (i == 0))(gather.send)
            pl.when((h == heads // 2) & (i == 0))(gather.forward)
        qv = q_ref[...]

        def step(j, carry, off):
            m, l, acc = carry
            sl = pl.ds(pl.multiple_of(j * tk, tk), tk)
            sc = _dot_nt(qv, k_ref[sl, :])
            if off is not None:
                row = lax.broadcasted_iota(jnp.int32, (t, tk), 0)
                col = lax.broadcasted_iota(jnp.int32, (t, tk), 1)
                sc = jnp.where(col + off <= row, sc, -1e30)
            m_new = jnp.maximum(m, jnp.max(sc, axis=1, keepdims=True))
            p = jnp.exp2(sc - m_new)
            alpha = jnp.exp2(m - m_new)
            l = alpha * l + jnp.sum(p, axis=1, keepdims=True)
            acc = alpha * acc + _dot(p.astype(BF16), v_ref[sl, :])
            return m_new, l, acc

        init = (jnp.full((t, 1), -1e30, F32), jnp.zeros((t, 1), F32), jnp.zeros((t, LANE), F32))
        carry = lax.fori_loop(0, i * r, lambda j, c: step(j, c, None), init)
        for jj in range(r):
            carry = step(i * r + jj, carry, jj * tk)
        m, l, acc = carry
        o_ref[...] = (acc / l).astype(o_ref.dtype)
        l_ref[0] = m + jnp.log2(l)
        if ng:
            pl.when((h == heads - 1) & (i == nq - 1))(gather.finish)

    blk = pl.BlockSpec((t, LANE), lambda h, i: (i, h))
    res = pl.BlockSpec((s, LANE), lambda h, i: (0, h))
    outs = pl.pallas_call(
        body, name="mla_fwd", grid=(heads, nq), in_specs=[blk, res, res] + [ANY] * ng,
        out_specs=[blk, pl.BlockSpec((1, t, 1), lambda h, i: (h, i, 0))] + [ANY] * ng,
        out_shape=[jax.ShapeDtypeStruct((s, hw), BF16), jax.ShapeDtypeStruct((heads, s, 1), F32)]
        + [jax.ShapeDtypeStruct((N_CHIPS,) + sh.shape, sh.dtype) for sh in shards],
        scratch_shapes=_dma_sems(6 * ng) + [pltpu.SemaphoreType.DMA((ng,))] if ng else [],
        compiler_params=_attn_params(s, t, 0, 2, ride=ng > 0),
    )(q, k, v, *shards)
    return outs[0], outs[1], list(outs[2:])


def _mla_bwd(q, k, v, o, do, lse, t, tk, parts=()):
    s, hw = q.shape
    heads, nq, r = hw // LANE, s // t, t // tk
    ns = len(parts)

    def body(q_ref, k_ref, v_ref, o_ref, do_ref, l_ref, *rest):
        dq_ref, dk_ref, dv_ref = rest[ns:ns + 3]
        h, i = pl.program_id(0), pl.program_id(1)
        finish = _ride_along(_scatter_copies, ns, (rest[:ns], rest[ns + 3:2 * ns + 3], *rest[2 * ns + 3:]),
                             (h == 0) & (i == 0), (h == heads - 1) & (i == nq - 1))

        @pl.when(i == 0)
        def _():
            dk_ref[...] = jnp.zeros_like(dk_ref)
            dv_ref[...] = jnp.zeros_like(dv_ref)

        qv, dov, lv = q_ref[...], do_ref[...], l_ref[0]
        dlt = jnp.sum(dov.astype(F32) * o_ref[...].astype(F32), axis=1, keepdims=True)

        def step(j, dq, off):
            sl = pl.ds(pl.multiple_of(j * tk, tk), tk)
            kv, vv = k_ref[sl, :], v_ref[sl, :]
            p = jnp.exp2(_dot_nt(qv, kv) - lv)
            if off is not None:
                row = lax.broadcasted_iota(jnp.int32, (t, tk), 0)
                col = lax.broadcasted_iota(jnp.int32, (t, tk), 1)
                p = jnp.where(col + off <= row, p, 0.0)
            ds = (p * (_dot_nt(dov, vv) - dlt)).astype(BF16)
            dk_ref[sl, :] += _dot_tn(ds, qv) * (1.0 / LOG2E)
            dv_ref[sl, :] += _dot_tn(p.astype(BF16), dov)
            return dq + _dot(ds, kv)

        dq = lax.fori_loop(0, i * r, lambda j, c: step(j, c, None), jnp.zeros((t, LANE), F32))
        for jj in range(r):
            dq = step(i * r + jj, dq, jj * tk)
        dq_ref[...] = dq * MLA_SCALE
        finish()

    blk = pl.BlockSpec((t, LANE), lambda h, i: (i, h))
    res = pl.BlockSpec((s, LANE), lambda h, i: (0, h))
    full = jax.ShapeDtypeStruct((s, hw), F32)
    outs = pl.pallas_call(
        body, name="mla_bwd", grid=(heads, nq),
        in_specs=[blk, res, res, blk, blk, pl.BlockSpec((1, t, 1), lambda h, i: (h, i, 0))] + [ANY] * ns,
        out_specs=[blk, res, res] + [ANY] * ns, out_shape=[full, full, full] + _scatter_shapes(parts),
        scratch_shapes=_dma_sems(3 * ns) if ns else [],
        compiler_params=_attn_params(s, t, 2, 2, ride=ns > 0),
    )(q, k, v, o, do, lse, *parts)
    return outs[0], outs[1], outs[2], list(outs[3:])


def _sb_logits(qv, kv, keep, upper):
    z = _dot_nt(qv, kv)
    e = jnp.exp(-jnp.abs(z))
    l1p = jnp.log(1.0 + e)
    lb = jnp.minimum(z, 0.0) - l1p
    lo = -jnp.maximum(z, 0.0) - l1p
    if keep is not None:
        lo = jnp.where(keep, lo, 0.0)
    hi = lo.astype(BF16)
    rem = (lo - hi.astype(F32)).astype(BF16)
    suf = _dot(hi, upper) + _dot(rem, upper)
    return z, e, lb, lo, suf


def _tri(t, inclusive):
    row = lax.broadcasted_iota(jnp.int32, (t, t), 0)
    col = lax.broadcasted_iota(jnp.int32, (t, t), 1)
    return jnp.where((row >= col) if inclusive else (row > col), 1.0, 0.0).astype(BF16)


SB_CHAINS = 4
SB_FIRST = 2


def _sb_first_tile(b, t):
    start = jnp.maximum(b - (SB_FIRST - 1), 0) * t
    row = lax.broadcasted_iota(jnp.int32, (t, SB_FIRST * t), 0)
    col = lax.broadcasted_iota(jnp.int32, (t, SB_FIRST * t), 1)
    return pl.ds(pl.multiple_of(start, t), SB_FIRST * t), col + start < row + b * t


def _sb_walk(i, first, carries_of):
    n = SB_CHAINS
    carries = [first(c) for c in range(n)]
    width = len(carries[0])

    def alive(carry):
        return jnp.max(carry[0]) >= SB_ZERO_LOG

    def split(st):
        return [tuple(st[1 + c * width:1 + (c + 1) * width]) for c in range(n)]

    def live(st):
        any_alive = alive(split(st)[0])
        for cr in split(st)[1:]:
            any_alive = any_alive | alive(cr)
        return (st[0] <= n * i) & any_alive

    def more(st):
        out = (st[0] + 1,)
        for c, cr in enumerate(split(st)):
            out += tuple(carries_of(c, st[0], cr))
        return out

    st = lax.while_loop(live, more, (jnp.int32(SB_FIRST),) + tuple(x for cr in carries for x in cr))
    jj, carries = st[0], split(st)
    for c in range(1, n):
        def live_c(s2, c=c):
            return (s2[0] <= n * i + c) & alive(s2[1:])

        def more_c(s2, c=c):
            return (s2[0] + 1,) + tuple(carries_of(c, s2[0], s2[1:]))

        carries[c] = lax.while_loop(live_c, more_c, (jj,) + tuple(carries[c]))[1:]
    return carries


def _sb_fwd(proj, t):
    s = proj.shape[0]
    heads, nq, n = SB_HEADS, s // t, SB_CHAINS

    def body(q_ref, k_ref, v_ref, o_ref):
        i = pl.program_id(1)
        upper, upper_first = _tri(t, False), _tri(SB_FIRST * t, False)
        qs = [(q_ref[c * t:(c + 1) * t, :] * SB_SCALE).astype(BF16) for c in range(n)]

        def first(c):
            sl, keep = _sb_first_tile(n * i + c, t)
            _, _, lb, lo, suf = _sb_logits(qs[c], k_ref[sl, :].astype(BF16), keep, upper_first)
            a = jnp.where(keep, jnp.exp(lb + suf), 0.0)
            return jnp.sum(lo, axis=1, keepdims=True), _dot(a.astype(BF16), v_ref[sl, :].astype(BF16))

        def step(c, jj, carry):
            run, acc = carry
            sl = pl.ds(pl.multiple_of((n * i + c - jj) * t, t), t)
            _, _, lb, lo, suf = _sb_logits(qs[c], k_ref[sl, :].astype(BF16), None, upper)
            a = jnp.exp(lb + suf + run)
            acc = acc + _dot(a.astype(BF16), v_ref[sl, :].astype(BF16))
            return run + jnp.sum(lo, axis=1, keepdims=True), acc

        carries = _sb_walk(i, first, step)
        for c in range(n):
            o_ref[c * t:(c + 1) * t, :] = carries[c][1]

    return pl.pallas_call(
        body, name="sb_fwd", grid=(heads, nq // n),
        in_specs=[pl.BlockSpec((n * t, LANE), lambda h, i: (i, C_SBQ // LANE + h)),
                  pl.BlockSpec((s, LANE), lambda h, i: (0, C_SBK // LANE + h)),
                  pl.BlockSpec((s, LANE), lambda h, i: (0, C_SBV // LANE + h))],
        out_specs=pl.BlockSpec((n * t, LANE), lambda h, i: (i, h)),
        out_shape=jax.ShapeDtypeStruct((s, heads * LANE), F32),
        compiler_params=_attn_params(s, n * t, 0, 2),
    )(proj, proj, proj)


def _sb_bwd(proj, o, do, t, stacked=()):
    s = proj.shape[0]
    heads, nq, n = SB_HEADS, s // t, SB_CHAINS
    nx = len(stacked)

    def body(q_ref, k_ref, v_ref, o_ref, do_ref, *rest):
        dq_ref, dk_ref, dv_ref = rest[nx:nx + 3]
        hd, i = pl.program_id(0), pl.program_id(1)
        finish = _ride_along(_exchange_copies, nx, (rest[:nx], rest[nx + 3:2 * nx + 3], *rest[2 * nx + 3:]),
                             (hd == 0) & (i == 0), (hd == heads - 1) & (i == nq // n - 1))

        @pl.when(i == 0)
        def _():
            dk_ref[...] = jnp.zeros_like(dk_ref)
            dv_ref[...] = jnp.zeros_like(dv_ref)

        rows = [slice(c * t, (c + 1) * t) for c in range(n)]
        qs = [(q_ref[r, :] * SB_SCALE).astype(BF16) for r in rows]
        dos = [do_ref[r, :] for r in rows]
        totals = [jnp.sum(dos[c].astype(F32) * o_ref[rows[c], :], axis=1, keepdims=True) for c in range(n)]
        tris = {1: (_tri(t, False), _tri(t, True)), SB_FIRST: (_tri(SB_FIRST * t, False), _tri(SB_FIRST * t, True))}

        def tile(c, sl, keep, blocks, carry):
            run, g, dq = carry
            qv, dov = qs[c], dos[c]
            upper, upper_incl = tris[blocks]
            kv, vv = k_ref[sl, :].astype(BF16), v_ref[sl, :].astype(BF16)
            z, e, lb, lo, suf = _sb_logits(qv, kv, keep, upper)
            tail = suf + run
            a = jnp.exp(lb + tail)
            if keep is not None:
                a = jnp.where(keep, a, 0.0)
            ab = a.astype(BF16)
            gr = ab.astype(F32) * _dot_nt(dov, vv)
            ghi = gr.astype(BF16)
            grem = (gr - ghi.astype(F32)).astype(BF16)
            before = totals[c] - g - (_dot(ghi, upper_incl) + _dot(grem, upper_incl))
            before = jnp.where(tail < SB_ZERO_LOG, 0.0, before)
            r = 1.0 / (1.0 + e)
            pos = z >= 0.0
            dz = r * (gr * jnp.where(pos, e, 1.0) - before * jnp.where(pos, 1.0, e))
            if keep is not None:
                dz = jnp.where(keep, dz, 0.0)
            dzb = dz.astype(BF16)
            dk_ref[sl, :] += _dot_tn(dzb, qv)
            dv_ref[sl, :] += _dot_tn(ab, dov)
            return (run + jnp.sum(lo, axis=1, keepdims=True), g + jnp.sum(gr, axis=1, keepdims=True),
                    dq + _dot(dzb, kv))

        zero = jnp.zeros((t, 1), F32)
        init = (zero, zero, jnp.zeros((t, LANE), F32))

        def first(c):
            sl, keep = _sb_first_tile(n * i + c, t)
            return tile(c, sl, keep, SB_FIRST, init)

        def step(c, jj, carry):
            return tile(c, pl.ds(pl.multiple_of((n * i + c - jj) * t, t), t), None, 1, carry)

        carries = _sb_walk(i, first, step)
        for c in range(n):
            dq_ref[rows[c], :] = carries[c][2] * SB_SCALE
        finish()

    blk = pl.BlockSpec((n * t, LANE), lambda h, i: (i, h))
    res = pl.BlockSpec((s, LANE), lambda h, i: (0, h))
    full = jax.ShapeDtypeStruct((s, heads * LANE), F32)
    outs = pl.pallas_call(
        body, name="sb_bwd", grid=(heads, nq // n),
        in_specs=[pl.BlockSpec((n * t, LANE), lambda h, i: (i, C_SBQ // LANE + h)),
                  pl.BlockSpec((s, LANE), lambda h, i: (0, C_SBK // LANE + h)),
                  pl.BlockSpec((s, LANE), lambda h, i: (0, C_SBV // LANE + h)), blk, blk] + [ANY] * nx,
        out_specs=[blk, res, res] + [ANY] * nx, out_shape=[full, full, full] + _exchange_shapes(stacked),
        scratch_shapes=_dma_sems(nx) if nx else [],
        compiler_params=_attn_params(s, n * t, 2, 2, ride=nx > 0),
    )(proj, proj, proj, o, do, *stacked)
    return outs[0], outs[1], outs[2], list(outs[3:])


def _xattn_probs(qh, kh):
    sc = _dot_nt(qh, kh) * (1.0 / math.sqrt(X_HEAD_DIM))
    p = jnp.exp(sc - jnp.max(sc, axis=1, keepdims=True))
    return p / jnp.sum(p, axis=1, keepdims=True)


def _xattn_fwd(xq, xkv):
    rows = xq.shape[0]
    w = X_HEADS * X_HEAD_DIM

    def body(i, ins, fulls, outs, accs):
        for h in range(X_HEADS):
            sl = slice(h * LANE, (h + 1) * LANE)
            p = _xattn_probs(ins[0][:, sl], fulls[0][:, sl])
            outs[0][:, sl] = _dot(p.astype(BF16), fulls[0][:, w + h * LANE:w + (h + 1) * LANE]).astype(BF16)

    return _rowwise(body, name="xattn_fwd", rows=rows, tr=512, row_ins=[(xq, w, 0)], full_ins=[xkv],
                    row_outs=[(w, BF16)])[0]


def _xattn_bwd(xq, xkv, dxo):
    rows = xq.shape[0]
    w = X_HEADS * X_HEAD_DIM

    def body(i, ins, fulls, outs, accs):
        _acc_init(i, accs)
        for h in range(X_HEADS):
            sl = slice(h * LANE, (h + 1) * LANE)
            slv = slice(w + h * LANE, w + (h + 1) * LANE)
            qh, kh, vh, doh = ins[0][:, sl], fulls[0][:, sl], fulls[0][:, slv], ins[1][:, sl]
            p = _xattn_probs(qh, kh)
            dp = _dot_nt(doh, vh)
            ds = (p * (dp - jnp.sum(p * dp, axis=1, keepdims=True)) * (1.0 / math.sqrt(X_HEAD_DIM))).astype(BF16)
            outs[0][:, sl] = _dot(ds, kh).astype(BF16)
            accs[0][:, sl] += _dot_tn(ds, qh)
            accs[0][:, slv] += _dot_tn(p.astype(BF16), doh)

    return _rowwise(body, name="xattn_bwd", rows=rows, tr=512, row_ins=[(xq, w, 0), (dxo, w, 0)], full_ins=[xkv],
                    row_outs=[(w, BF16)], acc_outs=[(xkv.shape, F32)])


def _gate_fwd(proj, pa, pb, b_gate):
    rows = proj.shape[0]

    def body(i, ins, fulls, outs, accs):
        sa = _sigmoid(ins[0][...].astype(F32) + fulls[0][0:1, :])
        sb = _sigmoid(ins[1][...].astype(F32) + fulls[0][1:2, :])
        outs[0][...] = (sa * ins[2][...] + sb * ins[3][...]).astype(BF16)

    return _rowwise(body, name="gate_fwd", rows=rows, tr=512,
                    row_ins=[(proj, D_MODEL, C_GA // D_MODEL), (proj, D_MODEL, C_GB // D_MODEL), (pa, D_MODEL, 0),
                             (pb, D_MODEL, 0)],
                    full_ins=[b_gate], row_outs=[(D_MODEL, BF16)])[0]


def _gate_bwd(proj, pa, pb, b_gate, dm):
    rows = proj.shape[0]

    def body(i, ins, fulls, outs, accs):
        _acc_init(i, accs)
        d = ins[4][...]
        sa = _sigmoid(ins[0][...].astype(F32) + fulls[0][0:1, :])
        sb = _sigmoid(ins[1][...].astype(F32) + fulls[0][1:2, :])
        dga = d * ins[2][...] * sa * (1.0 - sa)
        dgb = d * ins[3][...] * sb * (1.0 - sb)
        outs[0][...] = (d * sa).astype(BF16)
        outs[1][...] = (d * sb).astype(BF16)
        outs[2][...] = dga.astype(BF16)
        outs[3][...] = dgb.astype(BF16)
        accs[0][0:1, :] += jnp.sum(dga, axis=0, keepdims=True)
        accs[0][1:2, :] += jnp.sum(dgb, axis=0, keepdims=True)

    return _rowwise(body, name="gate_bwd", rows=rows, tr=512,
                    row_ins=[(proj, D_MODEL, C_GA // D_MODEL), (proj, D_MODEL, C_GB // D_MODEL), (pa, D_MODEL, 0),
                             (pb, D_MODEL, 0), (dm, D_MODEL, 0)],
                    full_ins=[b_gate], row_outs=[(D_MODEL, BF16)] * 4, acc_outs=[((2, D_MODEL), F32)])


def _loss_head(x3, target, g_final):
    rows = x3.shape[0]

    def body(i, ins, fulls, outs, accs):
        _acc_init(i, accs)
        xv, g = ins[0][...], fulls[0][...]
        d = _rms(xv, g) - ins[1][...]
        dx, dg = _rms_bwd(xv, g, d * (1.0 / D_MODEL))
        outs[0][...] = dx
        accs[0][...] += dg
        accs[1][...] += jnp.sum(d * d, axis=0, keepdims=True)

    return _rowwise(body, name="loss_head", rows=rows, tr=512, row_ins=[(x3, D_MODEL, 0), (target, D_MODEL, 0)],
                    full_ins=[g_final], row_outs=[(D_MODEL, F32)], acc_outs=[((1, D_MODEL), F32), ((1, D_MODEL), F32)])


def _adamw(w, g, m, v, name):
    rows, c = w.shape

    def body(i, ins, fulls, outs, accs):
        wv, gv = ins[0][...], ins[1][...]
        mn = ADAM_B1 * ins[2][...] + (1.0 - ADAM_B1) * gv
        vn = ADAM_B2 * ins[3][...] + (1.0 - ADAM_B2) * jnp.square(gv)
        m_hat = mn / (1.0 - ADAM_B1 ** ADAM_STEP)
        v_hat = vn / (1.0 - ADAM_B2 ** ADAM_STEP)
        outs[0][...] = -ADAM_LR * (m_hat / (jnp.sqrt(v_hat) + ADAM_EPS) + ADAM_WD * wv)
        outs[1][...] = mn
        outs[2][...] = vn

    return _rowwise(body, name=name, rows=rows, tr=_row_tile(rows, 256), row_ins=[(a, c, 0) for a in (w, g, m, v)],
                    row_outs=[(c, F32)] * 3)


def _place():
    x, y, c = lax.axis_index("x"), lax.axis_index("y"), lax.axis_index("c")
    chips = [(1 - x, y), (x, 1 - y), (1 - x, 1 - y)]
    return x, y, c, chips


ANY = pl.BlockSpec(memory_space=pl.ANY)


def _remote(src, dst, send_sem, recv_sem, to):
    return pltpu.make_async_remote_copy(src_ref=src, dst_ref=dst, send_sem=send_sem, recv_sem=recv_sem,
                                        device_id=to, device_id_type=MESH)


def _dma_sems(n):
    return [pltpu.SemaphoreType.DMA((n,)), pltpu.SemaphoreType.DMA((n,))]


class _GatherPhases:
    def __init__(self, ins, outs, send_sems, recv_sems, local_sems=None):
        self.ins, self.outs, self.send_sems, self.recv_sems = ins, outs, send_sems, recv_sems
        self.local_sems = local_sems
        self.x, self.y, self.c, self.chips = _place()
        self.me = 2 * self.x + self.y

    def _locals(self):
        if self.local_sems is None:
            return []
        return [pltpu.make_async_copy(self.ins[t], self.outs[t].at[self.me], self.local_sems.at[t])
                for t in range(len(self.ins))]

    def _copy(self, t, j, chip_idx, hlf, to, src=None):
        h = self.ins[t].shape[0] // 2
        dst = self.outs[t].at[chip_idx, pl.ds(hlf * h, h), :]
        return _remote(dst if src is None else src, dst, self.send_sems.at[6 * t + j], self.recv_sems.at[6 * t + j], to)

    def _sends(self):
        out = []
        for t in range(len(self.ins)):
            h = self.ins[t].shape[0] // 2
            for j, chip in enumerate(self.chips):
                out.append(self._copy(t, j, self.me, self.c, (*chip, self.c), src=self.ins[t].at[pl.ds(self.c * h, h), :]))
        return out

    def _forwards(self):
        return [self._copy(t, 3 + j, 2 * chip[0] + chip[1], self.c, (self.x, self.y, 1 - self.c))
                for t in range(len(self.ins)) for j, chip in enumerate(self.chips)]

    def send(self):
        for cp in self._sends() + self._locals():
            cp.start()

    def forward(self):
        here = (self.x, self.y, self.c)
        landed = [self._copy(t, j, 2 * chip[0] + chip[1], self.c, here)
                  for t in range(len(self.ins)) for j, chip in enumerate(self.chips)]
        for arrival, fwd in zip(landed, self._forwards()):
            arrival.wait_recv()
            fwd.start()

    def finish(self):
        here = (self.x, self.y, self.c)
        for t in range(len(self.ins)):
            for j, chip in enumerate(self.chips):
                self._copy(t, 3 + j, 2 * chip[0] + chip[1], 1 - self.c, here).wait_recv()
        for cp in self._sends() + self._forwards():
            cp.wait_send()
        for cp in self._locals():
            cp.wait()


def _all_gather_weights(shards):
    n = len(shards)

    def body(*refs):
        gather = _GatherPhases(refs[:n], refs[n:2 * n], *refs[2 * n:])
        gather.send()
        gather.forward()
        gather.finish()

    return pl.pallas_call(
        body, name="all_gather_weights", in_specs=[ANY] * n, out_specs=[ANY] * n,
        out_shape=[jax.ShapeDtypeStruct((N_CHIPS,) + s.shape, s.dtype) for s in shards],
        scratch_shapes=_dma_sems(6 * n),
    )(*shards)


def _exchange_copies(ins, outs, send_sems, recv_sems):
    x, y, c, _ = _place()
    cps = []
    for t in range(len(ins)):
        h = ins[t].shape[1] // 2
        cps.append(_remote(ins[t].at[:, pl.ds((1 - c) * h, h), :], outs[t], send_sems.at[t], recv_sems.at[t],
                           (x, y, 1 - c)))
    return cps


def _exchange_shapes(stacked):
    return [jax.ShapeDtypeStruct((N_CHIPS, s.shape[1] // 2, s.shape[2]), s.dtype) for s in stacked]


def _scatter_copies(ins, outs, send_sems, recv_sems):
    x, y, c, chips = _place()
    return [_remote(ins[t].at[2 * chip[0] + chip[1]], outs[t].at[j], send_sems.at[3 * t + j], recv_sems.at[3 * t + j],
                    (*chip, c)) for t in range(len(ins)) for j, chip in enumerate(chips)]


def _scatter_shapes(parts):
    return [jax.ShapeDtypeStruct((N_CHIPS - 1,) + s.shape[1:], s.dtype) for s in parts]


def _ride_along(copies_of, n, refs, first, last):
    if not n:
        return lambda: None

    def start():
        for cp in copies_of(*refs):
            cp.start()

    def wait():
        for cp in copies_of(*refs):
            cp.wait()

    pl.when(first)(start)
    return lambda: pl.when(last)(wait)


def _pair_exchange_halves(shards):
    n = len(shards)

    def body(*refs):
        bufs = refs[n:2 * n]
        send_sems, recv_sems = refs[2 * n:]
        x, y, c, _ = _place()
        cps = []
        for t in range(n):
            h = bufs[t].shape[0] // 2
            rows = bufs[t].at[pl.ds(c * h, h), :]
            cps.append(_remote(rows, rows, send_sems.at[t], recv_sems.at[t], (x, y, 1 - c)))
            cps[-1].start()
        for cp in cps:
            cp.wait()

    return pl.pallas_call(
        body, name="pair_exchange_halves", in_specs=[ANY] * n, out_specs=[ANY] * n,
        out_shape=[jax.ShapeDtypeStruct(s.shape, s.dtype) for s in shards],
        input_output_aliases={t: t for t in range(n)},
        scratch_shapes=_dma_sems(n),
    )(*shards)


def _pair_sum(gs, recv, place, name):
    _, r, cols = gs.shape
    h = r // 2

    def kern(p_ref, a_ref, b_ref, o_ref):
        o_ref[...] = (a_ref[...] + b_ref[...]).astype(BF16)

    blk = lambda f: pl.BlockSpec((1, h, cols), f)
    return pl.pallas_call(
        kern, name=name,
        grid_spec=pltpu.PrefetchScalarGridSpec(
            num_scalar_prefetch=1, grid=(N_CHIPS,),
            in_specs=[blk(lambda d, p: (d, p[1], 0)), blk(lambda d, p: (d, 0, 0))],
            out_specs=blk(lambda d, p: (d, 0, 0))),
        out_shape=jax.ShapeDtypeStruct((N_CHIPS, h, cols), BF16),
        compiler_params=pltpu.CompilerParams(dimension_semantics=("arbitrary",),
                                             vmem_limit_bytes=_vmem_limit(3 * _nbytes((h, cols), F32), 0)),
    )(place, gs, recv)


def _chip_sum(gs, recv, got, place, name):
    _, r, cols = gs.shape
    h = r // 2

    def kern(p_ref, a_ref, b_ref, g0, g1, g2, o_ref):
        own = a_ref[0] + b_ref[0]
        o_ref[...] = ((own + g0[0].astype(F32)) + g1[0].astype(F32)) + g2[0].astype(F32)

    blk = lambda f: pl.BlockSpec((1, h, cols), f)
    return pl.pallas_call(
        kern, name=name,
        grid_spec=pltpu.PrefetchScalarGridSpec(
            num_scalar_prefetch=1, grid=(1,),
            in_specs=[blk(lambda i, p: (p[0], p[1], 0)), blk(lambda i, p: (p[0], 0, 0)), blk(lambda i, p: (0, 0, 0)),
                      blk(lambda i, p: (1, 0, 0)), blk(lambda i, p: (2, 0, 0))],
            out_specs=pl.BlockSpec((h, cols), lambda i, p: (p[1], 0))),
        out_shape=jax.ShapeDtypeStruct((r, cols), F32),
        compiler_params=pltpu.CompilerParams(dimension_semantics=("arbitrary",),
                                             vmem_limit_bytes=_vmem_limit(5 * _nbytes((h, cols), F32), 0)),
    )(place, gs, recv, got, got, got)


def _all_reduce_small(vec, name):
    r, cols = vec.shape

    def body(in_ref, out_ref, gath, send_sems, recv_sems):
        x, y, c, _ = _place()
        me = 4 * x + 2 * y + c
        gath[me] = in_ref[...]
        sends = []
        for k in range(1, 8):
            to = (x ^ (k >> 2), y ^ ((k >> 1) & 1), c ^ (k & 1))
            cp = pltpu.make_async_remote_copy(src_ref=in_ref, dst_ref=gath.at[me], send_sem=send_sems.at[k - 1],
                                              recv_sem=recv_sems.at[k - 1], device_id=to, device_id_type=MESH)
            cp.start()
            sends.append(cp)
        for k in range(1, 8):
            peer = me ^ k
            pltpu.make_async_remote_copy(src_ref=in_ref, dst_ref=gath.at[peer], send_sem=send_sems.at[k - 1],
                                         recv_sem=recv_sems.at[k - 1], device_id=(x, y, c),
                                         device_id_type=MESH).wait_recv()
        for cp in sends:
            cp.wait_send()
        acc = gath[0]
        for d in range(1, 8):
            acc = acc + gath[d]
        out_ref[...] = acc

    vm = pl.BlockSpec(memory_space=pltpu.VMEM)
    return pl.pallas_call(
        body, name=name, in_specs=[vm], out_specs=vm,
        out_shape=jax.ShapeDtypeStruct((r, cols), F32),
        scratch_shapes=[pltpu.VMEM((8, r, cols), F32), pltpu.SemaphoreType.DMA((7,)), pltpu.SemaphoreType.DMA((7,))],
    )(vec)


def _pad_heads(w, heads, dim, axis):
    shp = w.shape[:axis] + (heads, dim) + w.shape[axis + 1:]
    pad = [(0, 0)] * len(shp)
    pad[axis + 1] = (0, LANE - dim)
    w = jnp.pad(w.reshape(shp), pad)
    return w.reshape(w.shape[:axis] + (heads * LANE,) + w.shape[axis + 2:])


def _unpad_heads(w, heads, dim, axis):
    shp = w.shape[:axis] + (heads, LANE) + w.shape[axis + 1:]
    w = lax.slice_in_dim(w.reshape(shp), 0, dim, axis=axis + 1)
    return w.reshape(w.shape[:axis] + (heads * dim,) + w.shape[axis + 2:])


def _w_in_layout(w_in):
    kr = jnp.pad(w_in[:, 384:416], ((0, 0), (ROPE_LO, LANE - ROPE_LO - MLA_ROPE)))
    sb = lambda lo: _pad_heads(w_in[:, lo:lo + 512], SB_HEADS, SB_HEAD_DIM, 1)
    return jnp.concatenate([w_in[:, 1952:2976], w_in[:, 2976:4000], sb(416), sb(928), sb(1440), w_in[:, 0:256],
                            w_in[:, 256:384], kr], axis=1)


def _w_in_unlayout(d):
    sb = lambda lo: _unpad_heads(d[:, lo:lo + 1024], SB_HEADS, SB_HEAD_DIM, 1)
    return jnp.concatenate([d[:, C_CQ:C_CQ + 256], d[:, C_CKV:C_CKV + 128], d[:, C_KR + ROPE_LO:C_KR + ROPE_LO + MLA_ROPE],
                            sb(C_SBQ), sb(C_SBK), sb(C_SBV), d[:, C_GA:C_GA + 1024], d[:, C_GB:C_GB + 1024]], axis=1)


def _w_ukv_layout(w):
    w3 = w.reshape(MLA_KV_RANK, MLA_HEADS, MLA_NOPE + MLA_V)
    pad = lambda part: jnp.pad(part, ((0, 0), (0, 0), (0, LANE - part.shape[2]))).reshape(MLA_KV_RANK, MLA_HEADS * LANE)
    return jnp.concatenate([pad(w3[:, :, :MLA_NOPE]), pad(w3[:, :, MLA_NOPE:])], axis=1)


def _w_ukv_unlayout(d):
    hw = MLA_HEADS * LANE
    kpart = d[:, :hw].reshape(MLA_KV_RANK, MLA_HEADS, LANE)[:, :, :MLA_NOPE]
    vpart = d[:, hw:].reshape(MLA_KV_RANK, MLA_HEADS, LANE)[:, :, :MLA_V]
    return jnp.concatenate([kpart, vpart], axis=2).reshape(MLA_KV_RANK, MLA_HEADS * (MLA_NOPE + MLA_V))


def _shard_of(full, d, axis):
    n = full.shape[axis] // N_CHIPS
    return lax.slice_in_dim(full, d * n, (d + 1) * n, axis=axis)


def _local_step(x, mem, pos, target, w, t_mla, t_sb, late=None, reduce=None):
    s = x.shape[0]
    w = dict(w)
    win = _w_in_layout(w["w_in"])
    wuq = _pad_heads(w["w_uq"], MLA_HEADS, MLA_NOPE + MLA_ROPE, 1)
    wkv = _w_ukv_layout(w["w_ukv"])
    inv_freq = ROPE_THETA ** (-jnp.arange(0, MLA_ROPE, 2, dtype=F32) / MLA_ROPE)
    freq_lane = jnp.pad(jnp.concatenate([inv_freq, inv_freq]), (ROPE_LO, LANE - ROPE_LO - MLA_ROPE)).reshape(1, LANE)
    add = lambda accs, ex: (accs[0] + ex[0],)

    def add_norm(accs, ex):
        y = accs[0] + ex[0]
        return y, _rms(y, ex[1])

    tab = _rope_tables(pos.reshape(s, 1), freq_lane)
    h = _rms_fwd_call(x, w["g_mix"], "rms_mix")
    proj = _mm(h, [win], name="proj_in", tn=1408, out_dtypes=(BF16,))
    cqn, ckvn, krope = _mla_prep_fwd(proj, tab, w["g_q_lat"], w["g_kv_lat"])
    hw = MLA_HEADS * LANE
    qa = _mm(cqn, [wuq], name="q_up", row_extras=(tab,), out_dtypes=(BF16,),
             epilogue=lambda accs, ex: (_per_head(lambda t: _rope(t, ex[0]) * (MLA_SCALE * LOG2E), accs[0]),))
    ka = _mm(ckvn, [wkv[:, :hw]], name="k_up", row_extras=(krope,), out_dtypes=(BF16,),
             epilogue=lambda accs, ex: (_per_head(lambda t: t + ex[0], accs[0]),))
    va = _mm(ckvn, [wkv[:, hw:]], name="v_up", out_dtypes=(BF16,))
    o_a, lse, gathered = _mla_fwd(qa, ka, va, t_mla[0], t_mla[1], late[0] if late else ())
    if late:
        w.update(late[1](gathered))
    wa = _pad_heads(w["w_a_proj"], MLA_HEADS, MLA_V, 0)
    wb = _pad_heads(w["w_b_proj"], SB_HEADS, SB_HEAD_DIM, 0)
    o_b = _sb_fwd(proj, t_sb)
    pa = _mm(o_a, [wa], name="proj_a")
    pb = _mm(o_b, [wb], name="proj_b")
    merged = _gate_fwd(proj, pa, pb, w["b_gate"])
    x1, hx = _mm(merged, [w["w_o"]], name="proj_o", extras=(x,), consts=(w["g_x"],), epilogue=add_norm,
                 out_dtypes=(F32, BF16))
    mn = _rms_fwd_call(mem, w["g_mem"], "rms_mem")
    xq = _mm(hx, [w["w_xq"]], name="xq", out_dtypes=(BF16,))
    xkv = _mm(mn, [w["w_xkv"]], name="xkv", out_dtypes=(BF16,))
    xo = _xattn_fwd(xq, xkv)
    x2, hf = _mm(xo, [w["w_xo"]], name="proj_xo", extras=(x1,), consts=(w["g_ffn"],), epilogue=add_norm,
                 out_dtypes=(F32, BF16))

    def swiglu(accs, ex):
        a, b = accs
        return a, b, a * _sigmoid(a) * b

    ga, gu, hmid = _mm(hf, [w["w_gate"], w["w_up"]], name="ffn_up", epilogue=swiglu, out_dtypes=(BF16, BF16, BF16),
                       tm=512, tn=1408)
    x3 = _mm(hmid, [w["w_down"]], name="ffn_down", extras=(x2,), epilogue=add, tk=2816)

    dx3, dg_final, sq = _loss_head(x3, target, w["g_final"].reshape(1, D_MODEL))
    g = {"g_final": dg_final.reshape(D_MODEL)}

    def swiglu_bwd(accs, ex):
        dh, a, b = accs[0], ex[0].astype(F32), ex[1].astype(F32)
        sg = _sigmoid(a)
        return dh * b * sg * (1.0 + a * (1.0 - sg)), dh * a * sg

    da, db = _mm(dx3, [w["w_down"]], name="ffn_down_dx", tb=True, extras=(ga, gu), epilogue=swiglu_bwd,
                 out_dtypes=(BF16, BF16), tm=512, tn=1408, chunk=MM_CHUNK)
    g["w_down"] = _mm(hmid, [dx3], name="ffn_down_dw", ta=True, tm=1408)
    g["w_gate"] = _mm(hf, [da], name="ffn_gate_dw", ta=True, tn=1408)
    g["w_up"] = _mm(hf, [db], name="ffn_up_dw", ta=True, tn=1408)
    dhf = _mm(da, [w["w_gate"]], name="ffn_gate_dx", tb=True, tk=2816)
    dhf = _mm(db, [w["w_up"]], name="ffn_up_dx", tb=True, extras=(dhf,), epilogue=add, tk=2816,
              out_dtypes=(BF16,))
    dx2, g["g_ffn"] = _rms_bwd_call(x2, w["g_ffn"], dhf, dx3, "rms_ffn_bwd")

    dxo = _mm(dx2, [w["w_xo"]], name="proj_xo_dx", tb=True, out_dtypes=(BF16,))
    g["w_xo"] = _mm(xo, [dx2], name="proj_xo_dw", ta=True)
    dxq, dxkv = _xattn_bwd(xq, xkv, dxo)
    dhx = _mm(dxq, [w["w_xq"]], name="xq_dx", tb=True, out_dtypes=(BF16,))
    g["w_xq"] = _mm(hx, [dxq], name="xq_dw", ta=True)
    dmn = _mm(dxkv, [w["w_xkv"]], name="xkv_dx", tb=True)
    g["w_xkv"] = _mm(mn, [dxkv], name="xkv_dw", ta=True)
    dx1, g["g_x"] = _rms_bwd_call(x1, w["g_x"], dhx, dx2, "rms_x_bwd")
    _, g["g_mem"] = _rms_bwd_call(mem, w["g_mem"], dmn, None, "rms_mem_bwd")

    dmerged = _mm(dx1, [w["w_o"]], name="proj_o_dx", tb=True)
    g["w_o"] = _mm(merged, [dx1], name="proj_o_dw", ta=True)
    dpa, dpb, dga, dgb, g["b_gate"] = _gate_bwd(proj, pa, pb, w["b_gate"], dmerged)
    do_a = _mm(dpa, [wa], name="proj_a_dx", tb=True, out_dtypes=(BF16,))
    do_b = _mm(dpb, [wb], name="proj_b_dx", tb=True, out_dtypes=(BF16,))
    g["w_a_proj"] = _unpad_heads(_mm(o_a, [dpa], name="proj_a_dw", ta=True), MLA_HEADS, MLA_V, 0)
    g["w_b_proj"] = _unpad_heads(_mm(o_b, [dpb], name="proj_b_dw", ta=True), SB_HEADS, SB_HEAD_DIM, 0)

    stacked = [reduce[1](n, g[n]) for n in reduce[0]] if reduce else []
    dsq, dsk, dsv, recv = _sb_bwd(proj, o_b, do_b, t_sb, stacked)
    parts = [reduce[2](n, gs, rv) for n, gs, rv in zip(reduce[0], stacked, recv)] if reduce else []
    dqa, dka, dva, got = _mla_bwd(qa, ka, va, o_a, do_a, lse, t_mla[0], t_mla[2], parts)
    riding = dict(zip(reduce[0], zip(stacked, recv, got))) if reduce else {}
    dqp, dkvp, dkr = _mla_rope_bwd(dqa, dka, dva, tab)
    g["w_uq"] = _unpad_heads(_mm(cqn, [dqp], name="q_up_dw", ta=True), MLA_HEADS, MLA_NOPE + MLA_ROPE, 1)
    g["w_ukv"] = _w_ukv_unlayout(_mm(ckvn, [dkvp], name="kv_up_dw", ta=True))
    dcqn = _mm(dqp, [wuq], name="q_up_dx", tb=True)
    dckvn = _mm(dkvp, [wkv], name="kv_up_dx", tb=True)
    dcq, dckv, g["g_q_lat"], g["g_kv_lat"] = _mla_prep_bwd(proj, w["g_q_lat"], w["g_kv_lat"], dcqn, dckvn)

    dproj = jnp.concatenate([dga, dgb, dsq.astype(BF16), dsk.astype(BF16), dsv.astype(BF16), dcq, dckv, dkr], axis=1)
    g["w_in"] = _w_in_unlayout(_mm(h, [dproj], name="proj_in_dw", ta=True, tn=1408))
    if not reduce:
        dh = _mm(dproj, [win], name="proj_in_dx", tb=True, tk=2816, out_dtypes=(BF16,))
        grad_x, g["g_mix"] = _rms_bwd_call(x, w["g_mix"], dh, dx1, "rms_mix_bwd")
        return sq, grad_x, g, riding
    stacked = [reduce[1](n, g[n]) for n in reduce[3]]
    dh, recv = _mm(dproj, [win], name="proj_in_dx", tb=True, tk=2816, out_dtypes=(BF16,),
                   ride=(_exchange_copies, stacked, _exchange_shapes(stacked), 1))
    parts = [reduce[2](n, gs, rv) for n, gs, rv in zip(reduce[3], stacked, recv)]
    grad_x, g["g_mix"], *got = _rms_bwd_call(x, w["g_mix"], dh, dx1, "rms_mix_bwd",
                                             ride=(_scatter_copies, parts, _scatter_shapes(parts), 3))
    riding.update(zip(reduce[3], zip(stacked, recv, got)))
    return sq, grad_x, g, riding


def _small_pack(d):
    row5 = jnp.concatenate([d["g_q_lat"].reshape(-1), d["g_kv_lat"].reshape(-1), jnp.zeros((640,), F32)])
    rows = [d[n].reshape(-1) for n in ("g_mix", "g_x", "g_mem", "g_ffn", "g_final")] + [row5]
    return rows


def _small_unpack(p, like):
    out = {n: p[i].reshape(like[n].shape) for i, n in enumerate(("g_mix", "g_x", "g_mem", "g_ffn", "g_final"))}
    out["g_q_lat"] = p[5, 0:256].reshape(like["g_q_lat"].shape)
    out["g_kv_lat"] = p[5, 256:384].reshape(like["g_kv_lat"].shape)
    return out


def kernel(x, mem, positions, g_mix, w_in, b_gate, g_q_lat, w_uq, g_kv_lat, w_ukv, w_a_proj, w_b_proj, w_o, g_x, g_mem, w_xq, w_xkv, w_xo, g_ffn, w_gate, w_up, w_down, g_final, loss_target, m_g_mix, m_w_in, m_b_gate, m_g_q_lat, m_w_uq, m_g_kv_lat, m_w_ukv, m_w_a_proj, m_w_b_proj, m_w_o, m_g_x, m_g_mem, m_w_xq, m_w_xkv, m_w_xo, m_g_ffn, m_w_gate, m_w_up, m_w_down, m_g_final, v_g_mix, v_w_in, v_b_gate, v_g_q_lat, v_w_uq, v_g_kv_lat, v_w_ukv, v_w_a_proj, v_w_b_proj, v_w_o, v_g_x, v_g_mem, v_w_xq, v_w_xkv, v_w_xo, v_g_ffn, v_w_gate, v_w_up, v_w_down, v_g_final):
    given = dict(locals())
    names = [n for n, _, _ in MATS] + ["b_gate"] + list(SMALL)
    wts = {n: given[n] for n in names}
    mom = {n: given["m_" + n] for n in names}
    var = {n: given["v_" + n] for n in names}
    shard2d = {n: shp for n, shp, _ in MATS}
    shard2d["b_gate"] = B_GATE_SHARD
    cx, cy, cc = lax.axis_index("x"), lax.axis_index("y"), lax.axis_index("c")
    me = 2 * cx + cy
    place = jnp.stack([me, cc]).astype(jnp.int32)
    bcol = me * B_GATE_SHARD[1]

    own = [wts[n].reshape(shard2d[n]).astype(BF16) for n, _, _ in MATS]
    bias = (("b_gate", (BIAS_ROWS, B_GATE_SHARD[1]), 1),)
    own_bias = [jnp.pad(wts["b_gate"].reshape(B_GATE_SHARD), ((0, BIAS_ROWS - B_GATE_SHARD[0]), (0, 0)))]

    def assemble(mats, gathered, mine=None):
        out = {}
        for k, ((n, shp, ax), g4) in enumerate(zip(mats, gathered)):
            if mine is not None:
                g4 = lax.dynamic_update_slice(g4, mine[k][None], (me, 0, 0))
            out[n] = g4.reshape(N_CHIPS * shp[0], shp[1]) if ax == 0 else jnp.concatenate(list(g4), axis=1)
        return out

    first = own[:N_EARLY] + own_bias
    full = assemble(MATS[:N_EARLY] + bias, _all_gather_weights(first), first)
    full["b_gate"] = full["b_gate"][0:B_GATE_SHARD[0]]
    late = (own[N_EARLY:], lambda gathered: assemble(MATS[N_EARLY:], gathered))
    for n in SMALL:
        full[n] = wts[n].reshape(1, -1) if n != "g_final" else wts[n]

    axis_of = {n: ax for n, _, ax in MATS}
    stack = lambda n, g: jnp.stack([_shard_of(g, d, axis_of[n]) for d in range(N_CHIPS)])
    pair_sum = lambda n, gs, rv: _pair_sum(gs, rv, place, "pair_sum_" + n)
    behind = [n for n, _, _ in MATS[N_EARLY:]]
    last = [n for n, _, _ in MATS[:N_EARLY]]
    sq, grad_x, grads, riding = _local_step(x[0], mem[0], positions[0], loss_target[0], full, t_mla=(1024, 1024, 1024), t_sb=256,
                                            late=late, reduce=(behind, stack, pair_sum, last))

    halves = [_chip_sum(*riding[n], place, "chip_sum_" + n) for n, _, _ in MATS]
    g_shard = dict(zip([n for n, _, _ in MATS], _pair_exchange_halves(halves)))

    small_rows = _small_pack({n: grads[n] for n in SMALL}) + [sq.reshape(-1), grads["b_gate"][0], grads["b_gate"][1]]
    small_rows += [jnp.zeros((D_MODEL,), F32)] * (SMALL_ROWS - len(small_rows))
    small = _all_reduce_small(jnp.stack(small_rows), "all_reduce_small")
    loss = (0.5 / D_MODEL) * jnp.sum(small[6])
    g_shard["b_gate"] = lax.dynamic_slice(small[7:9], (0, bcol), B_GATE_SHARD)

    out = {"grad": {}, "delta": {}, "m": {}, "v": {}}
    for n in [n for n, _, _ in MATS] + ["b_gate"]:
        shape = wts[n].shape
        r2 = lambda a: a.reshape(shard2d[n])
        d_n, m_n, v_n = _adamw(r2(wts[n]), g_shard[n], r2(mom[n]), r2(var[n]), "adamw_" + n)
        for key, a in (("grad", g_shard[n]), ("delta", d_n), ("m", m_n), ("v", v_n)):
            out[key][n] = a.reshape(shape)
    sp = lambda d: jnp.stack(_small_pack(d) + [jnp.zeros((D_MODEL,), F32)] * 2)
    delta_s, m_s, v_s = _adamw(sp(wts), small[0:8].at[6:8].set(0.0), sp(mom), sp(var), "adamw_small")
    for key, p in (("grad", small), ("delta", delta_s), ("m", m_s), ("v", v_s)):
        out[key].update(_small_unpack(p, wts))

    order = ["g_mix", "w_in", "b_gate", "g_q_lat", "w_uq", "g_kv_lat", "w_ukv", "w_a_proj", "w_b_proj", "w_o", "g_x",
             "g_mem", "w_xq", "w_xkv", "w_xo", "g_ffn", "w_gate", "w_up", "w_down", "g_final"]
    return (loss, grad_x[None], *[out[key][n] for key in ("grad", "delta", "m", "v") for n in order])
```

```python
import functools
import math

import jax
import jax.numpy as jnp
from jax import lax
from jax.experimental import pallas as pl
from jax.experimental.pallas import tpu as pltpu

F32 = jnp.float32
BF16 = jnp.bfloat16
MESH = pl.DeviceIdType.MESH

D_MODEL = 1024
MLA_HEADS = 8
MLA_Q_RANK = 256
MLA_KV_RANK = 128
MLA_NOPE = 64
MLA_ROPE = 32
MLA_V = 64
ROPE_THETA = 10000.0
SB_HEADS = 8
SB_HEAD_DIM = 64
X_HEADS = 4
X_HEAD_DIM = 128
EPS = 1e-6
ADAM_LR = 0.001
ADAM_B1 = 0.9
ADAM_B2 = 0.999
ADAM_EPS = 1e-08
ADAM_WD = 0.01
ADAM_STEP = 10

LANE = 128
LOG2E = 1.4426950408889634
MLA_SCALE = 1.0 / math.sqrt(MLA_NOPE + MLA_ROPE)
SB_SCALE = 1.0 / math.sqrt(SB_HEAD_DIM)
assert math.log2(SB_SCALE) == round(math.log2(SB_SCALE))
MM_CHUNK = 256
N_CHIPS = 4
VMEM_BYTES = 64 * 1024 * 1024

C_GA, C_GB, C_SBQ, C_SBK, C_SBV, C_CQ, C_CKV, C_KR = 0, 1024, 2048, 2560, 3072, 3584, 3840, 3968
SB_WIDTH = SB_HEADS * SB_HEAD_DIM
ROPE_LO = MLA_NOPE
HALF = MLA_ROPE // 2

SB_ZERO_LOG = -104.0

MATS = (
    ("w_in", (1024, 1000), 1), ("w_uq", (256, 192), 1), ("w_ukv", (128, 256), 1), ("w_a_proj", (512, 256), 1),
    ("w_b_proj", (512, 256), 1), ("w_o", (256, 1024), 0), ("w_xq", (256, 512), 0), ("w_xkv", (256, 1024), 0),
    ("w_xo", (512, 256), 1), ("w_gate", (1024, 704), 1), ("w_up", (1024, 704), 1), ("w_down", (704, 1024), 0),
)
N_EARLY = 3
B_GATE_SHARD = (2, 256)
BIAS_ROWS = 16
SMALL = ("g_mix", "g_x", "g_mem", "g_ffn", "g_final", "g_q_lat", "g_kv_lat")
SMALL_ROWS = 16


def _vmem_limit(block_bytes, temp_bytes):
    est = 2 * block_bytes + temp_bytes + (4 << 20)
    return int(min(max(est, 16 << 20), VMEM_BYTES - (6 << 20)))


def _nbytes(shape, dtype):
    return math.prod(shape) * jnp.dtype(dtype).itemsize


def _row_tile(rows, cap):
    if rows <= cap:
        return rows
    return max(t for t in range(8, cap + 1, 8) if rows % t == 0)


def _tile(n, cap):
    if n <= cap:
        return n
    best = None
    for t in range(LANE, cap + 1, LANE):
        if n % t == 0:
            best = t
    assert best is not None, (n, cap)
    return best


def _mm(a, bs, *, name, ta=False, tb=False, extras=(), row_extras=(), consts=(), epilogue=None, out_dtypes=(F32,),
        tm=1024, tn=1024, tk=1024, chunk=None, ride=None):
    bs = tuple(bs)
    m, k = (a.shape[1], a.shape[0]) if ta else a.shape
    n = bs[0].shape[0] if tb else bs[0].shape[1]
    tm, tn, tk = _tile(m, tm), _tile(n, tn), _tile(k, tk)
    assert m % tm == 0 and n % tn == 0 and k % tk == 0
    nk = k // tk
    nb, ne, no = len(bs), len(extras) + len(row_extras) + len(consts), len(out_dtypes)
    dims = (((0,) if ta else (1,)), ((1,) if tb else (0,))), ((), ())
    if epilogue is None:
        epilogue = lambda accs, ex: (accs[0],)

    rn = len(ride[1]) if ride else 0
    n_acc = nb if nk > 1 else 0
    gi, gj = m // tm, n // tn

    def body(*refs):
        a_ref, b_refs, e_refs = refs[0], refs[1:1 + nb], refs[1 + nb:1 + nb + ne]
        base = 1 + nb + ne + rn
        o_refs, acc_refs = refs[base:base + no], refs[base + no + rn:base + no + rn + n_acc]
        step = [pl.program_id(d) for d in range(3)]
        finish = _ride_along(ride[0] if ride else None, rn,
                             (refs[base - rn:base], refs[base + no:base + no + rn], *refs[base + no + rn + n_acc:]),
                             (step[0] == 0) & (step[1] == 0) & (step[2] == 0),
                             (step[0] == gi - 1) & (step[1] == gj - 1) & (step[2] == nk - 1))
        if nk == 1:
            ch = chunk or tm
            bvs = [b_ref[...].astype(BF16) for b_ref in b_refs]
            for r0 in range(0, tm, ch):
                rows = slice(r0, r0 + ch)
                av = (a_ref[:, rows] if ta else a_ref[rows, :]).astype(BF16)
                accs = [lax.dot_general(av, bv, dims, preferred_element_type=F32) for bv in bvs]
                ex = [e[rows, :] for e in e_refs[:ne - len(consts)]] + [e[...] for e in e_refs[ne - len(consts):]]
                for o_ref, v in zip(o_refs, epilogue(accs, ex)):
                    o_ref[rows, :] = v.astype(o_ref.dtype)
            finish()
            return
        kk = step[2]

        @pl.when(kk == 0)
        def _():
            for acc in acc_refs:
                acc[...] = jnp.zeros_like(acc)

        av = a_ref[...].astype(BF16)
        for b_ref, acc in zip(b_refs, acc_refs):
            acc[...] += lax.dot_general(av, b_ref[...].astype(BF16), dims, preferred_element_type=F32)

        @pl.when(kk == nk - 1)
        def _():
            outs = epilogue([acc[...] for acc in acc_refs], [e[...] for e in e_refs])
            for o_ref, v in zip(o_refs, outs):
                o_ref[...] = v.astype(o_ref.dtype)

        finish()

    a_spec = pl.BlockSpec((tk, tm), lambda i, j, kk: (kk, i)) if ta else pl.BlockSpec((tm, tk), lambda i, j, kk: (i, kk))
    b_spec = pl.BlockSpec((tn, tk), lambda i, j, kk: (j, kk)) if tb else pl.BlockSpec((tk, tn), lambda i, j, kk: (kk, j))
    mn_spec = pl.BlockSpec((tm, tn), lambda i, j, kk: (i, j))
    blocks = (_nbytes((tm, tk), a.dtype) + sum(_nbytes((tk, tn), b.dtype) for b in bs)
              + sum(_nbytes((tm, tn), e.dtype) for e in extras) + sum(_nbytes((tm, tn), d) for d in out_dtypes)
              + sum(_nbytes((tm, e.shape[1]), e.dtype) for e in row_extras))
    temps = (nb + 4) * _nbytes((tm, tn), F32)
    outs = pl.pallas_call(
        body, name=name, grid=(m // tm, n // tn, nk),
        in_specs=[a_spec] + [b_spec] * nb + [mn_spec] * len(extras)
        + [pl.BlockSpec((tm, e.shape[1]), lambda i, j, kk: (i, 0)) for e in row_extras]
        + [pl.BlockSpec(e.shape, lambda i, j, kk: (0, 0)) for e in consts] + [ANY] * rn,
        out_specs=[mn_spec] * no + [ANY] * rn,
        out_shape=[jax.ShapeDtypeStruct((m, n), d) for d in out_dtypes] + (list(ride[2]) if ride else []),
        scratch_shapes=[pltpu.VMEM((tm, tn), F32) for _ in range(n_acc)] + (_dma_sems(ride[3] * rn) if ride else []),
        compiler_params=pltpu.CompilerParams(
            dimension_semantics=("arbitrary",) * 3 if ride else ("parallel", "parallel", "arbitrary"),
            vmem_limit_bytes=_vmem_limit(blocks, temps)),
    )(a, *bs, *extras, *row_extras, *consts, *(ride[1] if ride else ()))
    if ride:
        return (outs[0] if no == 1 else outs[:no]), list(outs[no:])
    return outs[0] if no == 1 else outs


def _rowwise(body, *, name, rows, tr, row_ins, full_ins=(), row_outs=(), acc_outs=(), ride=None):
    tr = min(tr, rows)
    assert rows % tr == 0
    n_ri, n_fi, n_ro, n_ao = len(row_ins), len(full_ins), len(row_outs), len(acc_outs)
    rn = len(ride[1]) if ride else 0

    def kern(*refs):
        i = pl.program_id(0)
        n_in, n_out = n_ri + n_fi, n_ro + n_ao
        outs = refs[n_in + rn:n_in + rn + n_out]
        finish = _ride_along(ride[0] if ride else None, rn,
                             (refs[n_in:n_in + rn], refs[n_in + rn + n_out:n_in + 2 * rn + n_out],
                              *refs[n_in + 2 * rn + n_out:]), i == 0, i == rows // tr - 1)
        body(i, refs[:n_ri], refs[n_ri:n_in], outs[:n_ro], outs[n_ro:])
        finish()

    in_specs = [pl.BlockSpec((tr, w), functools.partial(lambda i, c: (i, c), c=ci)) for _, w, ci in row_ins]
    in_specs += [pl.BlockSpec(f.shape, lambda i: (0, 0)) for f in full_ins]
    in_specs += [ANY] * rn
    out_specs = [pl.BlockSpec((tr, w), lambda i: (i, 0)) for w, _ in row_outs]
    out_specs += [pl.BlockSpec(s, lambda i: (0, 0)) for s, _ in acc_outs] + [ANY] * rn
    out_shape = [jax.ShapeDtypeStruct((rows, w), d) for w, d in row_outs]
    out_shape += [jax.ShapeDtypeStruct(s, d) for s, d in acc_outs] + (list(ride[2]) if ride else [])
    blocks = (sum(_nbytes((tr, w), a.dtype) for a, w, _ in row_ins) + sum(_nbytes(f.shape, f.dtype) for f in full_ins)
              + sum(_nbytes((tr, w), d) for w, d in row_outs) + sum(_nbytes(s, d) for s, d in acc_outs))
    widest = max([w for _, w, _ in row_ins] + [w for w, _ in row_outs])
    outs = pl.pallas_call(
        kern, name=name, grid=(rows // tr,), in_specs=in_specs, out_specs=out_specs, out_shape=out_shape,
        scratch_shapes=_dma_sems(ride[3] * rn) if ride else [],
        compiler_params=pltpu.CompilerParams(
            dimension_semantics=("arbitrary",) if acc_outs or ride else ("parallel",),
            vmem_limit_bytes=_vmem_limit(blocks, 8 * _nbytes((tr, widest), F32))),
    )(*[a for a, _, _ in row_ins], *full_ins, *(ride[1] if ride else ()))
    return outs


def _rms(x, g):
    r = lax.rsqrt(jnp.mean(x * x, axis=-1, keepdims=True) + EPS)
    return x * r * g


def _rms_bwd(x, g, dy):
    r = lax.rsqrt(jnp.mean(x * x, axis=-1, keepdims=True) + EPS)
    xh = x * r
    dxh = dy * g
    dx = r * (dxh - xh * jnp.mean(dxh * xh, axis=-1, keepdims=True))
    return dx, jnp.sum(dy * xh, axis=0, keepdims=True)


def _sigmoid(x):
    return 1.0 / (1.0 + jnp.exp(-x))


def _acc_init(i, refs):
    @pl.when(i == 0)
    def _():
        for r in refs:
            r[...] = jnp.zeros_like(r)


def _rms_fwd_call(x, g, name):
    rows, c = x.shape

    def body(i, ins, fulls, outs, accs):
        outs[0][...] = _rms(ins[0][...], fulls[0][...]).astype(BF16)

    return _rowwise(body, name=name, rows=rows, tr=512, row_ins=[(x, c, 0)], full_ins=[g], row_outs=[(c, BF16)])[0]


def _rms_bwd_call(x, g, dy, res, name, ride=None):
    rows, c = x.shape
    row_ins = [(x, c, 0), (dy, c, 0)] + ([(res, c, 0)] if res is not None else [])

    def body(i, ins, fulls, outs, accs):
        _acc_init(i, accs)
        dx, dg = _rms_bwd(ins[0][...], fulls[0][...], ins[1][...].astype(F32))
        if res is not None:
            dx = dx + ins[2][...]
        outs[0][...] = dx
        accs[0][...] += dg

    return _rowwise(body, name=name, rows=rows, tr=512, row_ins=row_ins, full_ins=[g], row_outs=[(c, F32)],
                    acc_outs=[((1, c), F32)], ride=ride)


def _rope_tables(pos_col, freq_lane):
    rows = pos_col.shape[0]

    def body(i, ins, fulls, outs, accs):
        ang = ins[0][...].astype(F32) * fulls[0][...]
        lane = lax.broadcasted_iota(jnp.int32, ang.shape, 1)
        cos, sin = jnp.cos(ang), jnp.sin(ang)
        first = (lane >= ROPE_LO) & (lane < ROPE_LO + HALF)
        second = (lane >= ROPE_LO + HALF) & (lane < ROPE_LO + MLA_ROPE)
        outs[0][:, 0:LANE] = jnp.where(first | second, cos, 1.0)
        outs[0][:, LANE:2 * LANE] = jnp.where(first, -sin, 0.0)
        outs[0][:, 2 * LANE:3 * LANE] = jnp.where(second, sin, 0.0)

    return _rowwise(body, name="rope_tables", rows=rows, tr=1024, row_ins=[(pos_col, 1, 0)], full_ins=[freq_lane],
                    row_outs=[(3 * LANE, F32)])[0]


def _rope(x, tab):
    return (x * tab[:, 0:LANE] + pltpu.roll(x, LANE - HALF, 1) * tab[:, LANE:2 * LANE]
            + pltpu.roll(x, HALF, 1) * tab[:, 2 * LANE:3 * LANE])


def _rope_t(dy, tab):
    return (dy * tab[:, 0:LANE] + pltpu.roll(dy * tab[:, LANE:2 * LANE], HALF, 1)
            + pltpu.roll(dy * tab[:, 2 * LANE:3 * LANE], LANE - HALF, 1))


def _mla_prep_fwd(proj, tab, g_q, g_kv):
    rows = proj.shape[0]

    def body(i, ins, fulls, outs, accs):
        outs[0][...] = _rms(ins[0][...].astype(F32), fulls[0][...]).astype(BF16)
        outs[1][...] = _rms(ins[1][...].astype(F32), fulls[1][...]).astype(BF16)
        outs[2][...] = _rope(ins[2][...].astype(F32), ins[3][...])

    return _rowwise(body, name="mla_prep_fwd", rows=rows, tr=512,
                    row_ins=[(proj, MLA_Q_RANK, C_CQ // MLA_Q_RANK), (proj, LANE, C_CKV // LANE),
                             (proj, LANE, C_KR // LANE), (tab, 3 * LANE, 0)],
                    full_ins=[g_q, g_kv], row_outs=[(MLA_Q_RANK, BF16), (MLA_KV_RANK, BF16), (LANE, F32)])


def _mla_prep_bwd(proj, g_q, g_kv, dcqn, dckvn):
    rows = proj.shape[0]

    def body(i, ins, fulls, outs, accs):
        _acc_init(i, accs)
        dcq, dgq = _rms_bwd(ins[0][...].astype(F32), fulls[0][...], ins[2][...])
        dckv, dgkv = _rms_bwd(ins[1][...].astype(F32), fulls[1][...], ins[3][...])
        outs[0][...] = dcq.astype(BF16)
        outs[1][...] = dckv.astype(BF16)
        accs[0][...] += dgq
        accs[1][...] += dgkv

    return _rowwise(body, name="mla_prep_bwd", rows=rows, tr=512,
                    row_ins=[(proj, MLA_Q_RANK, C_CQ // MLA_Q_RANK), (proj, LANE, C_CKV // LANE),
                             (dcqn, MLA_Q_RANK, 0), (dckvn, MLA_KV_RANK, 0)],
                    full_ins=[g_q, g_kv], row_outs=[(MLA_Q_RANK, BF16), (MLA_KV_RANK, BF16)],
                    acc_outs=[((1, MLA_Q_RANK), F32), ((1, MLA_KV_RANK), F32)])


def _per_head(fn, x):
    return jnp.concatenate([fn(x[:, h * LANE:(h + 1) * LANE]) for h in range(x.shape[1] // LANE)], axis=1)


def _mla_rope_bwd(dq, dk, dv, tab):
    rows = dq.shape[0]
    hw = MLA_HEADS * LANE

    def body(i, ins, fulls, outs, accs):
        t = ins[3][...]
        dkr = jnp.zeros((ins[0].shape[0], LANE), F32)
        for h in range(MLA_HEADS):
            sl = slice(h * LANE, (h + 1) * LANE)
            outs[0][:, sl] = _rope_t(ins[0][:, sl], t).astype(BF16)
            dkr = dkr + ins[1][:, sl]
        outs[1][:, 0:hw] = ins[1][...].astype(BF16)
        outs[1][:, hw:2 * hw] = ins[2][...].astype(BF16)
        lane = lax.broadcasted_iota(jnp.int32, dkr.shape, 1)
        dkr = jnp.where((lane >= ROPE_LO) & (lane < ROPE_LO + MLA_ROPE), dkr, 0.0)
        outs[2][...] = _rope_t(dkr, t).astype(BF16)

    return _rowwise(body, name="mla_rope_bwd", rows=rows, tr=512,
                    row_ins=[(dq, hw, 0), (dk, hw, 0), (dv, hw, 0), (tab, 3 * LANE, 0)],
                    row_outs=[(hw, BF16), (2 * hw, BF16), (LANE, BF16)])


def _dot_nt(a, b):
    return lax.dot_general(a, b, (((1,), (1,)), ((), ())), preferred_element_type=F32)


def _dot_tn(a, b):
    return lax.dot_general(a, b, (((0,), (0,)), ((), ())), preferred_element_type=F32)


def _dot(a, b):
    return jnp.dot(a, b, preferred_element_type=F32)


def _attn_params(s, t, n_res_f32, n_res_bf16, ride=False):
    blocks = n_res_f32 * _nbytes((s, LANE), F32) + n_res_bf16 * _nbytes((s, LANE), BF16) + 6 * _nbytes((t, LANE), F32)
    return pltpu.CompilerParams(dimension_semantics=("arbitrary" if ride else "parallel", "arbitrary"),
                                vmem_limit_bytes=_vmem_limit(blocks, 12 * _nbytes((t, t), F32)))


def _mla_fwd(q, k, v, t, tk, shards=()):
    s, hw = q.shape
    heads, nq, r = hw // LANE, s // t, t // tk
    ng = len(shards)

    def body(q_ref, k_ref, v_ref, *rest):
        o_ref, l_ref = rest[ng], rest[ng + 1]
        h, i = pl.program_id(0), pl.program_id(1)
        if ng:
            gather = _GatherPhases(rest[:ng], rest[ng + 2:2 * ng + 2], *rest[2 * ng + 2:])
            pl.when((h == 0) & (i == 0))(gather.send)
            pl.when((h == heads // 2) & (i == 0))(gather.forward)
        qv = q_ref[...]

        def step(j, carry, off):
            m, l, acc = carry
            sl = pl.ds(pl.multiple_of(j * tk, tk), tk)
            sc = _dot_nt(qv, k_ref[sl, :])
            if off is not None:
                row = lax.broadcasted_iota(jnp.int32, (t, tk), 0)
                col = lax.broadcasted_iota(jnp.int32, (t, tk), 1)
                sc = jnp.where(col + off <= row, sc, -1e30)
            m_new = jnp.maximum(m, jnp.max(sc, axis=1, keepdims=True))
            p = jnp.exp2(sc - m_new)
            alpha = jnp.exp2(m - m_new)
            l = alpha * l + jnp.sum(p, axis=1, keepdims=True)
            acc = alpha * acc + _dot(p.astype(BF16), v_ref[sl, :])
            return m_new, l, acc

        init = (jnp.full((t, 1), -1e30, F32), jnp.zeros((t, 1), F32), jnp.zeros((t, LANE), F32))
        carry = lax.fori_loop(0, i * r, lambda j, c: step(j, c, None), init)
        for jj in range(r):
            carry = step(i * r + jj, carry, jj * tk)
        m, l, acc = carry
        o_ref[...] = (acc / l).astype(o_ref.dtype)
        l_ref[0] = m + jnp.log2(l)
        if ng:
            pl.when((h == heads - 1) & (i == nq - 1))(gather.finish)

    blk = pl.BlockSpec((t, LANE), lambda h, i: (i, h))
    res = pl.BlockSpec((s, LANE), lambda h, i: (0, h))
    outs = pl.pallas_call(
        body, name="mla_fwd", grid=(heads, nq), in_specs=[blk, res, res] + [ANY] * ng,
        out_specs=[blk, pl.BlockSpec((1, t, 1), lambda h, i: (h, i, 0))] + [ANY] * ng,
        out_shape=[jax.ShapeDtypeStruct((s, hw), BF16), jax.ShapeDtypeStruct((heads, s, 1), F32)]
        + [jax.ShapeDtypeStruct((N_CHIPS,) + sh.shape, sh.dtype) for sh in shards],
        scratch_shapes=_dma_sems(6 * ng) + [pltpu.SemaphoreType.DMA((ng,))] if ng else [],
        compiler_params=_attn_params(s, t, 0, 2, ride=ng > 0),
    )(q, k, v, *shards)
    return outs[0], outs[1], list(outs[2:])


def _mla_bwd(q, k, v, o, do, lse, t, tk, parts=()):
    s, hw = q.shape
    heads, nq, r = hw // LANE, s // t, t // tk
    ns = len(parts)

    def body(q_ref, k_ref, v_ref, o_ref, do_ref, l_ref, *rest):
        dq_ref, dk_ref, dv_ref = rest[ns:ns + 3]
        h, i = pl.program_id(0), pl.program_id(1)
        finish = _ride_along(_scatter_copies, ns, (rest[:ns], rest[ns + 3:2 * ns + 3], *rest[2 * ns + 3:]),
                             (h == 0) & (i == 0), (h == heads - 1) & (i == nq - 1))

        @pl.when(i == 0)
        def _():
            dk_ref[...] = jnp.zeros_like(dk_ref)
            dv_ref[...] = jnp.zeros_like(dv_ref)

        qv, dov, lv = q_ref[...], do_ref[...], l_ref[0]
        dlt = jnp.sum(dov.astype(F32) * o_ref[...].astype(F32), axis=1, keepdims=True)

        def step(j, dq, off):
            sl = pl.ds(pl.multiple_of(j * tk, tk), tk)
            kv, vv = k_ref[sl, :], v_ref[sl, :]
            p = jnp.exp2(_dot_nt(qv, kv) - lv)
            if off is not None:
                row = lax.broadcasted_iota(jnp.int32, (t, tk), 0)
                col = lax.broadcasted_iota(jnp.int32, (t, tk), 1)
                p = jnp.where(col + off <= row, p, 0.0)
            ds = (p * (_dot_nt(dov, vv) - dlt)).astype(BF16)
            dk_ref[sl, :] += _dot_tn(ds, qv) * (1.0 / LOG2E)
            dv_ref[sl, :] += _dot_tn(p.astype(BF16), dov)
            return dq + _dot(ds, kv)

        dq = lax.fori_loop(0, i * r, lambda j, c: step(j, c, None), jnp.zeros((t, LANE), F32))
        for jj in range(r):
            dq = step(i * r + jj, dq, jj * tk)
        dq_ref[...] = dq * MLA_SCALE
        finish()

    blk = pl.BlockSpec((t, LANE), lambda h, i: (i, h))
    res = pl.BlockSpec((s, LANE), lambda h, i: (0, h))
    full = jax.ShapeDtypeStruct((s, hw), F32)
    outs = pl.pallas_call(
        body, name="mla_bwd", grid=(heads, nq),
        in_specs=[blk, res, res, blk, blk, pl.BlockSpec((1, t, 1), lambda h, i: (h, i, 0))] + [ANY] * ns,
        out_specs=[blk, res, res] + [ANY] * ns, out_shape=[full, full, full] + _scatter_shapes(parts),
        scratch_shapes=_dma_sems(3 * ns) if ns else [],
        compiler_params=_attn_params(s, t, 2, 2, ride=ns > 0),
    )(q, k, v, o, do, lse, *parts)
    return outs[0], outs[1], outs[2], list(outs[3:])


def _sb_logits(qv, kv, keep, upper):
    z = _dot_nt(qv, kv)
    e = jnp.exp(-jnp.abs(z))
    l1p = jnp.log(1.0 + e)
    lb = jnp.minimum(z, 0.0) - l1p
    lo = -jnp.maximum(z, 0.0) - l1p
    if keep is not None:
        lo = jnp.where(keep, lo, 0.0)
    hi = lo.astype(BF16)
    rem = (lo - hi.astype(F32)).astype(BF16)
    suf = _dot(hi, upper) + _dot(rem, upper)
    return z, e, lb, lo, suf


def _tri(t, inclusive):
    row = lax.broadcasted_iota(jnp.int32, (t, t), 0)
    col = lax.broadcasted_iota(jnp.int32, (t, t), 1)
    return jnp.where((row >= col) if inclusive else (row > col), 1.0, 0.0).astype(BF16)


SB_QBLOCKS = 2
SB_PAIR = LANE // SB_HEAD_DIM
SB_CHAINS = SB_QBLOCKS * SB_PAIR
SB_FIRST = 2


def _sb_first_tile(b, t):
    start = jnp.maximum(b - (SB_FIRST - 1), 0) * t
    row = lax.broadcasted_iota(jnp.int32, (t, SB_FIRST * t), 0)
    col = lax.broadcasted_iota(jnp.int32, (t, SB_FIRST * t), 1)
    return pl.ds(pl.multiple_of(start, t), SB_FIRST * t), col + start < row + b * t


def _sb_head(x, hh):
    lane = lax.broadcasted_iota(jnp.int32, x.shape, 1)
    return jnp.where(lane // SB_HEAD_DIM == hh, x, jnp.zeros_like(x))


def _sb_walk(i, first, carries_of):
    n = SB_CHAINS
    carries = [first(c) for c in range(n)]
    width = len(carries[0])

    def alive(carry):
        return jnp.max(carry[0]) >= SB_ZERO_LOG

    def split(st):
        return [tuple(st[1 + c * width:1 + (c + 1) * width]) for c in range(n)]

    def live(st):
        any_alive = alive(split(st)[0])
        for cr in split(st)[1:]:
            any_alive = any_alive | alive(cr)
        return (st[0] <= SB_QBLOCKS * i) & any_alive

    def more(st):
        out = (st[0] + 1,)
        for c, cr in enumerate(split(st)):
            out += tuple(carries_of(c, st[0], cr))
        return out

    st = lax.while_loop(live, more, (jnp.int32(SB_FIRST),) + tuple(x for cr in carries for x in cr))
    jj, carries = st[0], split(st)
    for c in range(SB_PAIR, n):
        def live_c(s2, c=c):
            return (s2[0] <= SB_QBLOCKS * i + c // SB_PAIR) & alive(s2[1:])

        def more_c(s2, c=c):
            return (s2[0] + 1,) + tuple(carries_of(c, s2[0], s2[1:]))

        carries[c] = lax.while_loop(live_c, more_c, (jj,) + tuple(carries[c]))[1:]
    return carries


def _sb_fwd(proj, t):
    s = proj.shape[0]
    pairs, nq, nb = SB_HEADS // SB_PAIR, s // t, SB_QBLOCKS

    def body(q_ref, k_ref, v_ref, o_ref):
        i = pl.program_id(1)
        upper, upper_first = _tri(t, False), _tri(SB_FIRST * t, False)
        rows = [slice((c // SB_PAIR) * t, (c // SB_PAIR + 1) * t) for c in range(SB_CHAINS)]
        qs = [_sb_head(q_ref[rows[c], :] * SB_SCALE, c % SB_PAIR).astype(BF16) for c in range(SB_CHAINS)]

        def first(c):
            sl, keep = _sb_first_tile(nb * i + c // SB_PAIR, t)
            _, _, lb, lo, suf = _sb_logits(qs[c], k_ref[sl, :].astype(BF16), keep, upper_first)
            a = jnp.where(keep, jnp.exp(lb + suf), 0.0)
            return (jnp.sum(lo, axis=1, keepdims=True),
                    _dot(a.astype(BF16), _sb_head(v_ref[sl, :], c % SB_PAIR).astype(BF16)))

        def step(c, jj, carry):
            run, acc = carry
            sl = pl.ds(pl.multiple_of((nb * i + c // SB_PAIR - jj) * t, t), t)
            _, _, lb, lo, suf = _sb_logits(qs[c], k_ref[sl, :].astype(BF16), None, upper)
            a = jnp.exp(lb + suf + run)
            acc = acc + _dot(a.astype(BF16), _sb_head(v_ref[sl, :], c % SB_PAIR).astype(BF16))
            return run + jnp.sum(lo, axis=1, keepdims=True), acc

        carries = _sb_walk(i, first, step)
        for qb in range(nb):
            o_ref[qb * t:(qb + 1) * t, :] = sum(carries[qb * SB_PAIR + hh][1] for hh in range(SB_PAIR))

    return pl.pallas_call(
        body, name="sb_fwd", grid=(pairs, nq // nb),
        in_specs=[pl.BlockSpec((nb * t, LANE), lambda h, i: (i, C_SBQ // LANE + h)),
                  pl.BlockSpec((s, LANE), lambda h, i: (0, C_SBK // LANE + h)),
                  pl.BlockSpec((s, LANE), lambda h, i: (0, C_SBV // LANE + h))],
        out_specs=pl.BlockSpec((nb * t, LANE), lambda h, i: (i, h)),
        out_shape=jax.ShapeDtypeStruct((s, pairs * LANE), F32),
        compiler_params=_attn_params(s, nb * t, 0, 2),
    )(proj, proj, proj)


def _sb_bwd(proj, o, do, t, stacked=()):
    s = proj.shape[0]
    pairs, nq, nb = SB_HEADS // SB_PAIR, s // t, SB_QBLOCKS
    nx = len(stacked)

    def body(q_ref, k_ref, v_ref, o_ref, do_ref, *rest):
        dq_ref, dk_ref, dv_ref = rest[nx:nx + 3]
        hd, i = pl.program_id(0), pl.program_id(1)
        finish = _ride_along(_exchange_copies, nx, (rest[:nx], rest[nx + 3:2 * nx + 3], *rest[2 * nx + 3:]),
                             (hd == 0) & (i == 0), (hd == pairs - 1) & (i == nq // nb - 1))

        @pl.when(i == 0)
        def _():
            dk_ref[...] = jnp.zeros_like(dk_ref)
            dv_ref[...] = jnp.zeros_like(dv_ref)

        rows = [slice((c // SB_PAIR) * t, (c // SB_PAIR + 1) * t) for c in range(SB_CHAINS)]
        qs = [_sb_head(q_ref[rows[c], :] * SB_SCALE, c % SB_PAIR).astype(BF16) for c in range(SB_CHAINS)]
        dos = [_sb_head(do_ref[rows[c], :], c % SB_PAIR) for c in range(SB_CHAINS)]
        totals = [jnp.sum(dos[c].astype(F32) * o_ref[rows[c], :], axis=1, keepdims=True) for c in range(SB_CHAINS)]
        tris = {1: (_tri(t, False), _tri(t, True)), SB_FIRST: (_tri(SB_FIRST * t, False), _tri(SB_FIRST * t, True))}

        def tile(c, sl, keep, blocks, carry):
            run, g, dq = carry
            qv, dov = qs[c], dos[c]
            upper, upper_incl = tris[blocks]
            kv, vv = k_ref[sl, :].astype(BF16), v_ref[sl, :].astype(BF16)
            z, e, lb, lo, suf = _sb_logits(qv, kv, keep, upper)
            tail = suf + run
            a = jnp.exp(lb + tail)
            if keep is not None:
                a = jnp.where(keep, a, 0.0)
            ab = a.astype(BF16)
            gr = ab.astype(F32) * _dot_nt(dov, vv)
            ghi = gr.astype(BF16)
            grem = (gr - ghi.astype(F32)).astype(BF16)
            before = totals[c] - g - (_dot(ghi, upper_incl) + _dot(grem, upper_incl))
            before = jnp.where(tail < SB_ZERO_LOG, 0.0, before)
            r = 1.0 / (1.0 + e)
            pos = z >= 0.0
            dz = r * (gr * jnp.where(pos, e, 1.0) - before * jnp.where(pos, 1.0, e))
            if keep is not None:
                dz = jnp.where(keep, dz, 0.0)
            dzb = dz.astype(BF16)
            dk_ref[sl, :] += _dot_tn(dzb, qv)
            dv_ref[sl, :] += _dot_tn(ab, dov)
            return (run + jnp.sum(lo, axis=1, keepdims=True), g + jnp.sum(gr, axis=1, keepdims=True),
                    dq + _dot(dzb, _sb_head(kv, c % SB_PAIR)))

        zero = jnp.zeros((t, 1), F32)
        init = (zero, zero, jnp.zeros((t, LANE), F32))

        def first(c):
            sl, keep = _sb_first_tile(nb * i + c // SB_PAIR, t)
            return tile(c, sl, keep, SB_FIRST, init)

        def step(c, jj, carry):
            return tile(c, pl.ds(pl.multiple_of((nb * i + c // SB_PAIR - jj) * t, t), t), None, 1, carry)

        carries = _sb_walk(i, first, step)
        for qb in range(nb):
            dq_ref[qb * t:(qb + 1) * t, :] = sum(carries[qb * SB_PAIR + hh][2] for hh in range(SB_PAIR)) * SB_SCALE
        finish()

    blk = pl.BlockSpec((nb * t, LANE), lambda h, i: (i, h))
    res = pl.BlockSpec((s, LANE), lambda h, i: (0, h))
    full = jax.ShapeDtypeStruct((s, pairs * LANE), F32)
    outs = pl.pallas_call(
        body, name="sb_bwd", grid=(pairs, nq // nb),
        in_specs=[pl.BlockSpec((nb * t, LANE), lambda h, i: (i, C_SBQ // LANE + h)),
                  pl.BlockSpec((s, LANE), lambda h, i: (0, C_SBK // LANE + h)),
                  pl.BlockSpec((s, LANE), lambda h, i: (0, C_SBV // LANE + h)), blk, blk] + [ANY] * nx,
        out_specs=[blk, res, res] + [ANY] * nx, out_shape=[full, full, full] + _exchange_shapes(stacked),
        scratch_shapes=_dma_sems(nx) if nx else [],
        compiler_params=_attn_params(s, nb * t, 2, 2, ride=nx > 0),
    )(proj, proj, proj, o, do, *stacked)
    return outs[0], outs[1], outs[2], list(outs[3:])


def _xattn_probs(qh, kh):
    sc = _dot_nt(qh, kh) * (1.0 / math.sqrt(X_HEAD_DIM))
    p = jnp.exp(sc - jnp.max(sc, axis=1, keepdims=True))
    return p / jnp.sum(p, axis=1, keepdims=True)


def _xattn_fwd(xq, xkv):
    rows = xq.shape[0]
    w = X_HEADS * X_HEAD_DIM

    def body(i, ins, fulls, outs, accs):
        for h in range(X_HEADS):
            sl = slice(h * LANE, (h + 1) * LANE)
            p = _xattn_probs(ins[0][:, sl], fulls[0][:, sl])
            outs[0][:, sl] = _dot(p.astype(BF16), fulls[0][:, w + h * LANE:w + (h + 1) * LANE]).astype(BF16)

    return _rowwise(body, name="xattn_fwd", rows=rows, tr=512, row_ins=[(xq, w, 0)], full_ins=[xkv],
                    row_outs=[(w, BF16)])[0]


def _xattn_bwd(xq, xkv, dxo):
    rows = xq.shape[0]
    w = X_HEADS * X_HEAD_DIM

    def body(i, ins, fulls, outs, accs):
        _acc_init(i, accs)
        for h in range(X_HEADS):
            sl = slice(h * LANE, (h + 1) * LANE)
            slv = slice(w + h * LANE, w + (h + 1) * LANE)
            qh, kh, vh, doh = ins[0][:, sl], fulls[0][:, sl], fulls[0][:, slv], ins[1][:, sl]
            p = _xattn_probs(qh, kh)
            dp = _dot_nt(doh, vh)
            ds = (p * (dp - jnp.sum(p * dp, axis=1, keepdims=True)) * (1.0 / math.sqrt(X_HEAD_DIM))).astype(BF16)
            outs[0][:, sl] = _dot(ds, kh).astype(BF16)
            accs[0][:, sl] += _dot_tn(ds, qh)
            accs[0][:, slv] += _dot_tn(p.astype(BF16), doh)

    return _rowwise(body, name="xattn_bwd", rows=rows, tr=512, row_ins=[(xq, w, 0), (dxo, w, 0)], full_ins=[xkv],
                    row_outs=[(w, BF16)], acc_outs=[(xkv.shape, F32)])


def _gate_fwd(proj, pa, pb, b_gate):
    rows = proj.shape[0]

    def body(i, ins, fulls, outs, accs):
        sa = _sigmoid(ins[0][...].astype(F32) + fulls[0][0:1, :])
        sb = _sigmoid(ins[1][...].astype(F32) + fulls[0][1:2, :])
        outs[0][...] = (sa * ins[2][...] + sb * ins[3][...]).astype(BF16)

    return _rowwise(body, name="gate_fwd", rows=rows, tr=512,
                    row_ins=[(proj, D_MODEL, C_GA // D_MODEL), (proj, D_MODEL, C_GB // D_MODEL), (pa, D_MODEL, 0),
                             (pb, D_MODEL, 0)],
                    full_ins=[b_gate], row_outs=[(D_MODEL, BF16)])[0]


def _gate_bwd(proj, pa, pb, b_gate, dm):
    rows = proj.shape[0]

    def body(i, ins, fulls, outs, accs):
        _acc_init(i, accs)
        d = ins[4][...]
        sa = _sigmoid(ins[0][...].astype(F32) + fulls[0][0:1, :])
        sb = _sigmoid(ins[1][...].astype(F32) + fulls[0][1:2, :])
        dga = d * ins[2][...] * sa * (1.0 - sa)
        dgb = d * ins[3][...] * sb * (1.0 - sb)
        outs[0][...] = (d * sa).astype(BF16)
        outs[1][...] = (d * sb).astype(BF16)
        outs[2][...] = dga.astype(BF16)
        outs[3][...] = dgb.astype(BF16)
        accs[0][0:1, :] += jnp.sum(dga, axis=0, keepdims=True)
        accs[0][1:2, :] += jnp.sum(dgb, axis=0, keepdims=True)

    return _rowwise(body, name="gate_bwd", rows=rows, tr=512,
                    row_ins=[(proj, D_MODEL, C_GA // D_MODEL), (proj, D_MODEL, C_GB // D_MODEL), (pa, D_MODEL, 0),
                             (pb, D_MODEL, 0), (dm, D_MODEL, 0)],
                    full_ins=[b_gate], row_outs=[(D_MODEL, BF16)] * 4, acc_outs=[((2, D_MODEL), F32)])


def _loss_head(x3, target, g_final):
    rows = x3.shape[0]

    def body(i, ins, fulls, outs, accs):
        _acc_init(i, accs)
        xv, g = ins[0][...], fulls[0][...]
        d = _rms(xv, g) - ins[1][...]
        dx, dg = _rms_bwd(xv, g, d * (1.0 / D_MODEL))
        outs[0][...] = dx
        accs[0][...] += dg
        accs[1][...] += jnp.sum(d * d, axis=0, keepdims=True)

    return _rowwise(body, name="loss_head", rows=rows, tr=512, row_ins=[(x3, D_MODEL, 0), (target, D_MODEL, 0)],
                    full_ins=[g_final], row_outs=[(D_MODEL, F32)], acc_outs=[((1, D_MODEL), F32), ((1, D_MODEL), F32)])


def _adamw(w, g, m, v, name):
    rows, c = w.shape

    def body(i, ins, fulls, outs, accs):
        wv, gv = ins[0][...], ins[1][...]
        mn = ADAM_B1 * ins[2][...] + (1.0 - ADAM_B1) * gv
        vn = ADAM_B2 * ins[3][...] + (1.0 - ADAM_B2) * jnp.square(gv)
        m_hat = mn / (1.0 - ADAM_B1 ** ADAM_STEP)
        v_hat = vn / (1.0 - ADAM_B2 ** ADAM_STEP)
        outs[0][...] = -ADAM_LR * (m_hat / (jnp.sqrt(v_hat) + ADAM_EPS) + ADAM_WD * wv)
        outs[1][...] = mn
        outs[2][...] = vn

    return _rowwise(body, name=name, rows=rows, tr=_row_tile(rows, 256), row_ins=[(a, c, 0) for a in (w, g, m, v)],
                    row_outs=[(c, F32)] * 3)


def _place():
    x, y, c = lax.axis_index("x"), lax.axis_index("y"), lax.axis_index("c")
    chips = [(1 - x, y), (x, 1 - y), (1 - x, 1 - y)]
    return x, y, c, chips


ANY = pl.BlockSpec(memory_space=pl.ANY)


def _remote(src, dst, send_sem, recv_sem, to):
    return pltpu.make_async_remote_copy(src_ref=src, dst_ref=dst, send_sem=send_sem, recv_sem=recv_sem,
                                        device_id=to, device_id_type=MESH)


def _dma_sems(n):
    return [pltpu.SemaphoreType.DMA((n,)), pltpu.SemaphoreType.DMA((n,))]


class _GatherPhases:
    def __init__(self, ins, outs, send_sems, recv_sems, local_sems=None):
        self.ins, self.outs, self.send_sems, self.recv_sems = ins, outs, send_sems, recv_sems
        self.local_sems = local_sems
        self.x, self.y, self.c, self.chips = _place()
        self.me = 2 * self.x + self.y

    def _locals(self):
        if self.local_sems is None:
            return []
        return [pltpu.make_async_copy(self.ins[t], self.outs[t].at[self.me], self.local_sems.at[t])
                for t in range(len(self.ins))]

    def _copy(self, t, j, chip_idx, hlf, to, src=None):
        h = self.ins[t].shape[0] // 2
        dst = self.outs[t].at[chip_idx, pl.ds(hlf * h, h), :]
        return _remote(dst if src is None else src, dst, self.send_sems.at[6 * t + j], self.recv_sems.at[6 * t + j], to)

    def _sends(self):
        out = []
        for t in range(len(self.ins)):
            h = self.ins[t].shape[0] // 2
            for j, chip in enumerate(self.chips):
                out.append(self._copy(t, j, self.me, self.c, (*chip, self.c), src=self.ins[t].at[pl.ds(self.c * h, h), :]))
        return out

    def _forwards(self):
        return [self._copy(t, 3 + j, 2 * chip[0] + chip[1], self.c, (self.x, self.y, 1 - self.c))
                for t in range(len(self.ins)) for j, chip in enumerate(self.chips)]

    def send(self):
        for cp in self._sends() + self._locals():
            cp.start()

    def forward(self):
        here = (self.x, self.y, self.c)
        landed = [self._copy(t, j, 2 * chip[0] + chip[1], self.c, here)
                  for t in range(len(self.ins)) for j, chip in enumerate(self.chips)]
        for arrival, fwd in zip(landed, self._forwards()):
            arrival.wait_recv()
            fwd.start()

    def finish(self):
        here = (self.x, self.y, self.c)
        for t in range(len(self.ins)):
            for j, chip in enumerate(self.chips):
                self._copy(t, 3 + j, 2 * chip[0] + chip[1], 1 - self.c, here).wait_recv()
        for cp in self._sends() + self._forwards():
            cp.wait_send()
        for cp in self._locals():
            cp.wait()


def _all_gather_weights(shards):
    n = len(shards)

    def body(*refs):
        gather = _GatherPhases(refs[:n], refs[n:2 * n], *refs[2 * n:])
        gather.send()
        gather.forward()
        gather.finish()

    return pl.pallas_call(
        body, name="all_gather_weights", in_specs=[ANY] * n, out_specs=[ANY] * n,
        out_shape=[jax.ShapeDtypeStruct((N_CHIPS,) + s.shape, s.dtype) for s in shards],
        scratch_shapes=_dma_sems(6 * n),
    )(*shards)


def _exchange_copies(ins, outs, send_sems, recv_sems):
    x, y, c, _ = _place()
    cps = []
    for t in range(len(ins)):
        h = ins[t].shape[1] // 2
        cps.append(_remote(ins[t].at[:, pl.ds((1 - c) * h, h), :], outs[t], send_sems.at[t], recv_sems.at[t],
                           (x, y, 1 - c)))
    return cps


def _exchange_shapes(stacked):
    return [jax.ShapeDtypeStruct((N_CHIPS, s.shape[1] // 2, s.shape[2]), s.dtype) for s in stacked]


def _scatter_copies(ins, outs, send_sems, recv_sems):
    x, y, c, chips = _place()
    return [_remote(ins[t].at[2 * chip[0] + chip[1]], outs[t].at[j], send_sems.at[3 * t + j], recv_sems.at[3 * t + j],
                    (*chip, c)) for t in range(len(ins)) for j, chip in enumerate(chips)]


def _scatter_shapes(parts):
    return [jax.ShapeDtypeStruct((N_CHIPS - 1,) + s.shape[1:], s.dtype) for s in parts]


def _ride_along(copies_of, n, refs, first, last):
    if not n:
        return lambda: None

    def start():
        for cp in copies_of(*refs):
            cp.start()

    def wait():
        for cp in copies_of(*refs):
            cp.wait()

    pl.when(first)(start)
    return lambda: pl.when(last)(wait)


def _pair_exchange_halves(shards):
    n = len(shards)

    def body(*refs):
        bufs = refs[n:2 * n]
        send_sems, recv_sems = refs[2 * n:]
        x, y, c, _ = _place()
        cps = []
        for t in range(n):
            h = bufs[t].shape[0] // 2
            rows = bufs[t].at[pl.ds(c * h, h), :]
            cps.append(_remote(rows, rows, send_sems.at[t], recv_sems.at[t], (x, y, 1 - c)))
            cps[-1].start()
        for cp in cps:
            cp.wait()

    return pl.pallas_call(
        body, name="pair_exchange_halves", in_specs=[ANY] * n, out_specs=[ANY] * n,
        out_shape=[jax.ShapeDtypeStruct(s.shape, s.dtype) for s in shards],
        input_output_aliases={t: t for t in range(n)},
        scratch_shapes=_dma_sems(n),
    )(*shards)


def _pair_sum(gs, recv, place, name):
    _, r, cols = gs.shape
    h = r // 2

    def kern(p_ref, a_ref, b_ref, o_ref):
        o_ref[...] = (a_ref[...] + b_ref[...]).astype(BF16)

    blk = lambda f: pl.BlockSpec((1, h, cols), f)
    return pl.pallas_call(
        kern, name=name,
        grid_spec=pltpu.PrefetchScalarGridSpec(
            num_scalar_prefetch=1, grid=(N_CHIPS,),
            in_specs=[blk(lambda d, p: (d, p[1], 0)), blk(lambda d, p: (d, 0, 0))],
            out_specs=blk(lambda d, p: (d, 0, 0))),
        out_shape=jax.ShapeDtypeStruct((N_CHIPS, h, cols), BF16),
        compiler_params=pltpu.CompilerParams(dimension_semantics=("arbitrary",),
                                             vmem_limit_bytes=_vmem_limit(3 * _nbytes((h, cols), F32), 0)),
    )(place, gs, recv)


def _chip_sum(gs, recv, got, place, name):
    _, r, cols = gs.shape
    h = r // 2

    def kern(p_ref, a_ref, b_ref, g0, g1, g2, o_ref):
        own = a_ref[0] + b_ref[0]
        o_ref[...] = ((own + g0[0].astype(F32)) + g1[0].astype(F32)) + g2[0].astype(F32)

    blk = lambda f: pl.BlockSpec((1, h, cols), f)
    return pl.pallas_call(
        kern, name=name,
        grid_spec=pltpu.PrefetchScalarGridSpec(
            num_scalar_prefetch=1, grid=(1,),
            in_specs=[blk(lambda i, p: (p[0], p[1], 0)), blk(lambda i, p: (p[0], 0, 0)), blk(lambda i, p: (0, 0, 0)),
                      blk(lambda i, p: (1, 0, 0)), blk(lambda i, p: (2, 0, 0))],
            out_specs=pl.BlockSpec((h, cols), lambda i, p: (p[1], 0))),
        out_shape=jax.ShapeDtypeStruct((r, cols), F32),
        compiler_params=pltpu.CompilerParams(dimension_semantics=("arbitrary",),
                                             vmem_limit_bytes=_vmem_limit(5 * _nbytes((h, cols), F32), 0)),
    )(place, gs, recv, got, got, got)


def _all_reduce_small(vec, name):
    r, cols = vec.shape

    def body(in_ref, out_ref, gath, send_sems, recv_sems):
        x, y, c, _ = _place()
        me = 4 * x + 2 * y + c
        gath[me] = in_ref[...]
        sends = []
        for k in range(1, 8):
            to = (x ^ (k >> 2), y ^ ((k >> 1) & 1), c ^ (k & 1))
            cp = pltpu.make_async_remote_copy(src_ref=in_ref, dst_ref=gath.at[me], send_sem=send_sems.at[k - 1],
                                              recv_sem=recv_sems.at[k - 1], device_id=to, device_id_type=MESH)
            cp.start()
            sends.append(cp)
        for k in range(1, 8):
            peer = me ^ k
            pltpu.make_async_remote_copy(src_ref=in_ref, dst_ref=gath.at[peer], send_sem=send_sems.at[k - 1],
                                         recv_sem=recv_sems.at[k - 1], device_id=(x, y, c),
                                         device_id_type=MESH).wait_recv()
        for cp in sends:
            cp.wait_send()
        acc = gath[0]
        for d in range(1, 8):
            acc = acc + gath[d]
        out_ref[...] = acc

    vm = pl.BlockSpec(memory_space=pltpu.VMEM)
    return pl.pallas_call(
        body, name=name, in_specs=[vm], out_specs=vm,
        out_shape=jax.ShapeDtypeStruct((r, cols), F32),
        scratch_shapes=[pltpu.VMEM((8, r, cols), F32), pltpu.SemaphoreType.DMA((7,)), pltpu.SemaphoreType.DMA((7,))],
    )(vec)


def _pad_heads(w, heads, dim, axis):
    shp = w.shape[:axis] + (heads, dim) + w.shape[axis + 1:]
    pad = [(0, 0)] * len(shp)
    pad[axis + 1] = (0, LANE - dim)
    w = jnp.pad(w.reshape(shp), pad)
    return w.reshape(w.shape[:axis] + (heads * LANE,) + w.shape[axis + 2:])


def _unpad_heads(w, heads, dim, axis):
    shp = w.shape[:axis] + (heads, LANE) + w.shape[axis + 1:]
    w = lax.slice_in_dim(w.reshape(shp), 0, dim, axis=axis + 1)
    return w.reshape(w.shape[:axis] + (heads * dim,) + w.shape[axis + 2:])


def _w_in_layout(w_in):
    kr = jnp.pad(w_in[:, 384:416], ((0, 0), (ROPE_LO, LANE - ROPE_LO - MLA_ROPE)))
    sb = lambda lo: w_in[:, lo:lo + SB_WIDTH]
    return jnp.concatenate([w_in[:, 1952:2976], w_in[:, 2976:4000], sb(416), sb(928), sb(1440), w_in[:, 0:256],
                            w_in[:, 256:384], kr], axis=1)


def _w_in_unlayout(d):
    sb = lambda lo: d[:, lo:lo + SB_WIDTH]
    return jnp.concatenate([d[:, C_CQ:C_CQ + 256], d[:, C_CKV:C_CKV + 128], d[:, C_KR + ROPE_LO:C_KR + ROPE_LO + MLA_ROPE],
                            sb(C_SBQ), sb(C_SBK), sb(C_SBV), d[:, C_GA:C_GA + 1024], d[:, C_GB:C_GB + 1024]], axis=1)


def _w_ukv_layout(w):
    w3 = w.reshape(MLA_KV_RANK, MLA_HEADS, MLA_NOPE + MLA_V)
    pad = lambda part: jnp.pad(part, ((0, 0), (0, 0), (0, LANE - part.shape[2]))).reshape(MLA_KV_RANK, MLA_HEADS * LANE)
    return jnp.concatenate([pad(w3[:, :, :MLA_NOPE]), pad(w3[:, :, MLA_NOPE:])], axis=1)


def _w_ukv_unlayout(d):
    hw = MLA_HEADS * LANE
    kpart = d[:, :hw].reshape(MLA_KV_RANK, MLA_HEADS, LANE)[:, :, :MLA_NOPE]
    vpart = d[:, hw:].reshape(MLA_KV_RANK, MLA_HEADS, LANE)[:, :, :MLA_V]
    return jnp.concatenate([kpart, vpart], axis=2).reshape(MLA_KV_RANK, MLA_HEADS * (MLA_NOPE + MLA_V))


def _shard_of(full, d, axis):
    n = full.shape[axis] // N_CHIPS
    return lax.slice_in_dim(full, d * n, (d + 1) * n, axis=axis)


def _local_step(x, mem, pos, target, w, t_mla, t_sb, late=None, reduce=None):
    s = x.shape[0]
    w = dict(w)
    win = _w_in_layout(w["w_in"])
    wuq = _pad_heads(w["w_uq"], MLA_HEADS, MLA_NOPE + MLA_ROPE, 1)
    wkv = _w_ukv_layout(w["w_ukv"])
    inv_freq = ROPE_THETA ** (-jnp.arange(0, MLA_ROPE, 2, dtype=F32) / MLA_ROPE)
    freq_lane = jnp.pad(jnp.concatenate([inv_freq, inv_freq]), (ROPE_LO, LANE - ROPE_LO - MLA_ROPE)).reshape(1, LANE)
    add = lambda accs, ex: (accs[0] + ex[0],)

    def add_norm(accs, ex):
        y = accs[0] + ex[0]
        return y, _rms(y, ex[1])

    tab = _rope_tables(pos.reshape(s, 1), freq_lane)
    h = _rms_fwd_call(x, w["g_mix"], "rms_mix")
    proj = _mm(h, [win], name="proj_in", out_dtypes=(BF16,))
    cqn, ckvn, krope = _mla_prep_fwd(proj, tab, w["g_q_lat"], w["g_kv_lat"])
    hw = MLA_HEADS * LANE
    qa = _mm(cqn, [wuq], name="q_up", row_extras=(tab,), out_dtypes=(BF16,),
             epilogue=lambda accs, ex: (_per_head(lambda t: _rope(t, ex[0]) * (MLA_SCALE * LOG2E), accs[0]),))
    ka = _mm(ckvn, [wkv[:, :hw]], name="k_up", row_extras=(krope,), out_dtypes=(BF16,),
             epilogue=lambda accs, ex: (_per_head(lambda t: t + ex[0], accs[0]),))
    va = _mm(ckvn, [wkv[:, hw:]], name="v_up", out_dtypes=(BF16,))
    o_a, lse, gathered = _mla_fwd(qa, ka, va, t_mla[0], t_mla[1], late[0] if late else ())
    if late:
        w.update(late[1](gathered))
    wa = _pad_heads(w["w_a_proj"], MLA_HEADS, MLA_V, 0)
    wb = w["w_b_proj"]
    o_b = _sb_fwd(proj, t_sb)
    pa = _mm(o_a, [wa], name="proj_a")
    pb = _mm(o_b, [wb], name="proj_b")
    merged = _gate_fwd(proj, pa, pb, w["b_gate"])
    x1, hx = _mm(merged, [w["w_o"]], name="proj_o", extras=(x,), consts=(w["g_x"],), epilogue=add_norm,
                 out_dtypes=(F32, BF16))
    mn = _rms_fwd_call(mem, w["g_mem"], "rms_mem")
    xq = _mm(hx, [w["w_xq"]], name="xq", out_dtypes=(BF16,))
    xkv = _mm(mn, [w["w_xkv"]], name="xkv", out_dtypes=(BF16,))
    xo = _xattn_fwd(xq, xkv)
    x2, hf = _mm(xo, [w["w_xo"]], name="proj_xo", extras=(x1,), consts=(w["g_ffn"],), epilogue=add_norm,
                 out_dtypes=(F32, BF16))

    def swiglu(accs, ex):
        a, b = accs
        return a, b, a * _sigmoid(a) * b

    ga, gu, hmid = _mm(hf, [w["w_gate"], w["w_up"]], name="ffn_up", epilogue=swiglu, out_dtypes=(BF16, BF16, BF16),
                       tm=512, tn=1408)
    x3 = _mm(hmid, [w["w_down"]], name="ffn_down", extras=(x2,), epilogue=add, tk=2816)

    dx3, dg_final, sq = _loss_head(x3, target, w["g_final"].reshape(1, D_MODEL))
    g = {"g_final": dg_final.reshape(D_MODEL)}

    def swiglu_bwd(accs, ex):
        dh, a, b = accs[0], ex[0].astype(F32), ex[1].astype(F32)
        sg = _sigmoid(a)
        return dh * b * sg * (1.0 + a * (1.0 - sg)), dh * a * sg

    da, db = _mm(dx3, [w["w_down"]], name="ffn_down_dx", tb=True, extras=(ga, gu), epilogue=swiglu_bwd,
                 out_dtypes=(BF16, BF16), tm=512, tn=1408, chunk=MM_CHUNK)
    g["w_down"] = _mm(hmid, [dx3], name="ffn_down_dw", ta=True, tm=1408)
    g["w_gate"] = _mm(hf, [da], name="ffn_gate_dw", ta=True, tn=1408)
    g["w_up"] = _mm(hf, [db], name="ffn_up_dw", ta=True, tn=1408)
    dhf = _mm(da, [w["w_gate"]], name="ffn_gate_dx", tb=True, tk=2816)
    dhf = _mm(db, [w["w_up"]], name="ffn_up_dx", tb=True, extras=(dhf,), epilogue=add, tk=2816,
              out_dtypes=(BF16,))
    dx2, g["g_ffn"] = _rms_bwd_call(x2, w["g_ffn"], dhf, dx3, "rms_ffn_bwd")

    dxo = _mm(dx2, [w["w_xo"]], name="proj_xo_dx", tb=True, out_dtypes=(BF16,))
    g["w_xo"] = _mm(xo, [dx2], name="proj_xo_dw", ta=True)
    dxq, dxkv = _xattn_bwd(xq, xkv, dxo)
    dhx = _mm(dxq, [w["w_xq"]], name="xq_dx", tb=True, out_dtypes=(BF16,))
    g["w_xq"] = _mm(hx, [dxq], name="xq_dw", ta=True)
    dmn = _mm(dxkv, [w["w_xkv"]], name="xkv_dx", tb=True)
    g["w_xkv"] = _mm(mn, [dxkv], name="xkv_dw", ta=True)
    dx1, g["g_x"] = _rms_bwd_call(x1, w["g_x"], dhx, dx2, "rms_x_bwd")
    _, g["g_mem"] = _rms_bwd_call(mem, w["g_mem"], dmn, None, "rms_mem_bwd")

    dmerged = _mm(dx1, [w["w_o"]], name="proj_o_dx", tb=True)
    g["w_o"] = _mm(merged, [dx1], name="proj_o_dw", ta=True)
    dpa, dpb, dga, dgb, g["b_gate"] = _gate_bwd(proj, pa, pb, w["b_gate"], dmerged)
    do_a = _mm(dpa, [wa], name="proj_a_dx", tb=True, out_dtypes=(BF16,))
    do_b = _mm(dpb, [wb], name="proj_b_dx", tb=True, out_dtypes=(BF16,))
    g["w_a_proj"] = _unpad_heads(_mm(o_a, [dpa], name="proj_a_dw", ta=True), MLA_HEADS, MLA_V, 0)
    g["w_b_proj"] = _mm(o_b, [dpb], name="proj_b_dw", ta=True)

    stacked = [reduce[1](n, g[n]) for n in reduce[0]] if reduce else []
    dsq, dsk, dsv, recv = _sb_bwd(proj, o_b, do_b, t_sb, stacked)
    parts = [reduce[2](n, gs, rv) for n, gs, rv in zip(reduce[0], stacked, recv)] if reduce else []
    dqa, dka, dva, got = _mla_bwd(qa, ka, va, o_a, do_a, lse, t_mla[0], t_mla[2], parts)
    riding = dict(zip(reduce[0], zip(stacked, recv, got))) if reduce else {}
    dqp, dkvp, dkr = _mla_rope_bwd(dqa, dka, dva, tab)
    g["w_uq"] = _unpad_heads(_mm(cqn, [dqp], name="q_up_dw", ta=True), MLA_HEADS, MLA_NOPE + MLA_ROPE, 1)
    g["w_ukv"] = _w_ukv_unlayout(_mm(ckvn, [dkvp], name="kv_up_dw", ta=True))
    dcqn = _mm(dqp, [wuq], name="q_up_dx", tb=True)
    dckvn = _mm(dkvp, [wkv], name="kv_up_dx", tb=True)
    dcq, dckv, g["g_q_lat"], g["g_kv_lat"] = _mla_prep_bwd(proj, w["g_q_lat"], w["g_kv_lat"], dcqn, dckvn)

    dproj = jnp.concatenate([dga, dgb, dsq.astype(BF16), dsk.astype(BF16), dsv.astype(BF16), dcq, dckv, dkr], axis=1)
    g["w_in"] = _w_in_unlayout(_mm(h, [dproj], name="proj_in_dw", ta=True))
    if not reduce:
        dh = _mm(dproj, [win], name="proj_in_dx", tb=True, tk=2048, out_dtypes=(BF16,))
        grad_x, g["g_mix"] = _rms_bwd_call(x, w["g_mix"], dh, dx1, "rms_mix_bwd")
        return sq, grad_x, g, riding
    stacked = [reduce[1](n, g[n]) for n in reduce[3]]
    dh, recv = _mm(dproj, [win], name="proj_in_dx", tb=True, tk=2048, out_dtypes=(BF16,),
                   ride=(_exchange_copies, stacked, _exchange_shapes(stacked), 1))
    parts = [reduce[2](n, gs, rv) for n, gs, rv in zip(reduce[3], stacked, recv)]
    grad_x, g["g_mix"], *got = _rms_bwd_call(x, w["g_mix"], dh, dx1, "rms_mix_bwd",
                                             ride=(_scatter_copies, parts, _scatter_shapes(parts), 3))
    riding.update(zip(reduce[3], zip(stacked, recv, got)))
    return sq, grad_x, g, riding


def _small_pack(d):
    row5 = jnp.concatenate([d["g_q_lat"].reshape(-1), d["g_kv_lat"].reshape(-1), jnp.zeros((640,), F32)])
    rows = [d[n].reshape(-1) for n in ("g_mix", "g_x", "g_mem", "g_ffn", "g_final")] + [row5]
    return rows


def _small_unpack(p, like):
    out = {n: p[i].reshape(like[n].shape) for i, n in enumerate(("g_mix", "g_x", "g_mem", "g_ffn", "g_final"))}
    out["g_q_lat"] = p[5, 0:256].reshape(like["g_q_lat"].shape)
    out["g_kv_lat"] = p[5, 256:384].reshape(like["g_kv_lat"].shape)
    return out


def kernel(x, mem, positions, g_mix, w_in, b_gate, g_q_lat, w_uq, g_kv_lat, w_ukv, w_a_proj, w_b_proj, w_o, g_x, g_mem, w_xq, w_xkv, w_xo, g_ffn, w_gate, w_up, w_down, g_final, loss_target, m_g_mix, m_w_in, m_b_gate, m_g_q_lat, m_w_uq, m_g_kv_lat, m_w_ukv, m_w_a_proj, m_w_b_proj, m_w_o, m_g_x, m_g_mem, m_w_xq, m_w_xkv, m_w_xo, m_g_ffn, m_w_gate, m_w_up, m_w_down, m_g_final, v_g_mix, v_w_in, v_b_gate, v_g_q_lat, v_w_uq, v_g_kv_lat, v_w_ukv, v_w_a_proj, v_w_b_proj, v_w_o, v_g_x, v_g_mem, v_w_xq, v_w_xkv, v_w_xo, v_g_ffn, v_w_gate, v_w_up, v_w_down, v_g_final):
    given = dict(locals())
    names = [n for n, _, _ in MATS] + ["b_gate"] + list(SMALL)
    wts = {n: given[n] for n in names}
    mom = {n: given["m_" + n] for n in names}
    var = {n: given["v_" + n] for n in names}
    shard2d = {n: shp for n, shp, _ in MATS}
    shard2d["b_gate"] = B_GATE_SHARD
    cx, cy, cc = lax.axis_index("x"), lax.axis_index("y"), lax.axis_index("c")
    me = 2 * cx + cy
    place = jnp.stack([me, cc]).astype(jnp.int32)
    bcol = me * B_GATE_SHARD[1]

    own = [wts[n].reshape(shard2d[n]).astype(BF16) for n, _, _ in MATS]
    bias = (("b_gate", (BIAS_ROWS, B_GATE_SHARD[1]), 1),)
    own_bias = [jnp.pad(wts["b_gate"].reshape(B_GATE_SHARD), ((0, BIAS_ROWS - B_GATE_SHARD[0]), (0, 0)))]

    def assemble(mats, gathered, mine=None):
        out = {}
        for k, ((n, shp, ax), g4) in enumerate(zip(mats, gathered)):
            if mine is not None:
                g4 = lax.dynamic_update_slice(g4, mine[k][None], (me, 0, 0))
            out[n] = g4.reshape(N_CHIPS * shp[0], shp[1]) if ax == 0 else jnp.concatenate(list(g4), axis=1)
        return out

    first = own[:N_EARLY] + own_bias
    full = assemble(MATS[:N_EARLY] + bias, _all_gather_weights(first), first)
    full["b_gate"] = full["b_gate"][0:B_GATE_SHARD[0]]
    late = (own[N_EARLY:], lambda gathered: assemble(MATS[N_EARLY:], gathered))
    for n in SMALL:
        full[n] = wts[n].reshape(1, -1) if n != "g_final" else wts[n]

    axis_of = {n: ax for n, _, ax in MATS}
    stack = lambda n, g: jnp.stack([_shard_of(g, d, axis_of[n]) for d in range(N_CHIPS)])
    pair_sum = lambda n, gs, rv: _pair_sum(gs, rv, place, "pair_sum_" + n)
    behind = [n for n, _, _ in MATS[N_EARLY:]]
    last = [n for n, _, _ in MATS[:N_EARLY]]
    sq, grad_x, grads, riding = _local_step(x[0], mem[0], positions[0], loss_target[0], full, t_mla=(1024, 1024, 1024), t_sb=256,
                                            late=late, reduce=(behind, stack, pair_sum, last))

    halves = [_chip_sum(*riding[n], place, "chip_sum_" + n) for n, _, _ in MATS]
    g_shard = dict(zip([n for n, _, _ in MATS], _pair_exchange_halves(halves)))

    small_rows = _small_pack({n: grads[n] for n in SMALL}) + [sq.reshape(-1), grads["b_gate"][0], grads["b_gate"][1]]
    small_rows += [jnp.zeros((D_MODEL,), F32)] * (SMALL_ROWS - len(small_rows))
    small = _all_reduce_small(jnp.stack(small_rows), "all_reduce_small")
    loss = (0.5 / D_MODEL) * jnp.sum(small[6])
    g_shard["b_gate"] = lax.dynamic_slice(small[7:9], (0, bcol), B_GATE_SHARD)

    out = {"grad": {}, "delta": {}, "m": {}, "v": {}}
    for n in [n for n, _, _ in MATS] + ["b_gate"]:
        shape = wts[n].shape
        r2 = lambda a: a.reshape(shard2d[n])
        d_n, m_n, v_n = _adamw(r2(wts[n]), g_shard[n], r2(mom[n]), r2(var[n]), "adamw_" + n)
        for key, a in (("grad", g_shard[n]), ("delta", d_n), ("m", m_n), ("v", v_n)):
            out[key][n] = a.reshape(shape)
    sp = lambda d: jnp.stack(_small_pack(d) + [jnp.zeros((D_MODEL,), F32)] * 2)
    delta_s, m_s, v_s = _adamw(sp(wts), small[0:8].at[6:8].set(0.0), sp(mom), sp(var), "adamw_small")
    for key, p in (("grad", small), ("delta", delta_s), ("m", m_s), ("v", v_s)):
        out[key].update(_small_unpack(p, wts))

    order = ["g_mix", "w_in", "b_gate", "g_q_lat", "w_uq", "g_kv_lat", "w_ukv", "w_a_proj", "w_b_proj", "w_o", "g_x",
             "g_mem", "w_xq", "w_xkv", "w_xo", "g_ffn", "w_gate", "w_up", "w_down", "g_final"]
    return (loss, grad_x[None], *[out[key][n] for key in ("grad", "delta", "m", "v") for n in order])
```

```python
import functools
import math

import jax
import jax.numpy as jnp
from jax import lax
from jax.experimental import pallas as pl
from jax.experimental.pallas import tpu as pltpu

F32 = jnp.float32
BF16 = jnp.bfloat16
MESH = pl.DeviceIdType.MESH

D_MODEL = 1024
MLA_HEADS = 8
MLA_Q_RANK = 256
MLA_KV_RANK = 128
MLA_NOPE = 64
MLA_ROPE = 32
MLA_V = 64
ROPE_THETA = 10000.0
SB_HEADS = 8
SB_HEAD_DIM = 64
X_HEADS = 4
X_HEAD_DIM = 128
EPS = 1e-6
ADAM_LR = 0.001
ADAM_B1 = 0.9
ADAM_B2 = 0.999
ADAM_EPS = 1e-08
ADAM_WD = 0.01
ADAM_STEP = 10

LANE = 128
LOG2E = 1.4426950408889634
MLA_SCALE = 1.0 / math.sqrt(MLA_NOPE + MLA_ROPE)
SB_SCALE = 1.0 / math.sqrt(SB_HEAD_DIM)
assert math.log2(SB_SCALE) == round(math.log2(SB_SCALE))
MM_CHUNK = 256
N_CHIPS = 4
VMEM_BYTES = 64 * 1024 * 1024

C_GA, C_GB, C_SBQ, C_SBK, C_SBV, C_CQ, C_CKV, C_KR = 0, 1024, 2048, 2560, 3072, 3584, 3840, 3968
SB_WIDTH = SB_HEADS * SB_HEAD_DIM
ROPE_LO = MLA_NOPE
HALF = MLA_ROPE // 2

SB_ZERO_LOG = -104.0

MATS = (
    ("w_in", (1024, 1000), 1), ("w_uq", (256, 192), 1), ("w_ukv", (128, 256), 1), ("w_a_proj", (512, 256), 1),
    ("w_b_proj", (512, 256), 1), ("w_o", (256, 1024), 0), ("w_xq", (256, 512), 0), ("w_xkv", (256, 1024), 0),
    ("w_xo", (512, 256), 1), ("w_gate", (1024, 704), 1), ("w_up", (1024, 704), 1), ("w_down", (704, 1024), 0),
)
N_EARLY = 3
B_GATE_SHARD = (2, 256)
BIAS_ROWS = 16
SMALL = ("g_mix", "g_x", "g_mem", "g_ffn", "g_final", "g_q_lat", "g_kv_lat")
SMALL_ROWS = 16


def _vmem_limit(block_bytes, temp_bytes):
    est = 2 * block_bytes + temp_bytes + (4 << 20)
    return int(min(max(est, 16 << 20), VMEM_BYTES - (6 << 20)))


def _nbytes(shape, dtype):
    return math.prod(shape) * jnp.dtype(dtype).itemsize


def _row_tile(rows, cap):
    if rows <= cap:
        return rows
    return max(t for t in range(8, cap + 1, 8) if rows % t == 0)


def _tile(n, cap):
    if n <= cap:
        return n
    best = None
    for t in range(LANE, cap + 1, LANE):
        if n % t == 0:
            best = t
    assert best is not None, (n, cap)
    return best


def _mm(a, bs, *, name, ta=False, tb=False, extras=(), row_extras=(), consts=(), epilogue=None, out_dtypes=(F32,),
        tm=1024, tn=1024, tk=1024, chunk=None, ride=None):
    bs = tuple(bs)
    m, k = (a.shape[1], a.shape[0]) if ta else a.shape
    n = bs[0].shape[0] if tb else bs[0].shape[1]
    tm, tn, tk = _tile(m, tm), _tile(n, tn), _tile(k, tk)
    assert m % tm == 0 and n % tn == 0 and k % tk == 0
    nk = k // tk
    nb, ne, no = len(bs), len(extras) + len(row_extras) + len(consts), len(out_dtypes)
    dims = (((0,) if ta else (1,)), ((1,) if tb else (0,))), ((), ())
    if epilogue is None:
        epilogue = lambda accs, ex: (accs[0],)

    rn = len(ride[1]) if ride else 0
    n_acc = nb if nk > 1 else 0
    gi, gj = m // tm, n // tn

    def body(*refs):
        a_ref, b_refs, e_refs = refs[0], refs[1:1 + nb], refs[1 + nb:1 + nb + ne]
        base = 1 + nb + ne + rn
        o_refs, acc_refs = refs[base:base + no], refs[base + no + rn:base + no + rn + n_acc]
        step = [pl.program_id(d) for d in range(3)]
        finish = _ride_along(ride[0] if ride else None, rn,
                             (refs[base - rn:base], refs[base + no:base + no + rn], *refs[base + no + rn + n_acc:]),
                             (step[0] == 0) & (step[1] == 0) & (step[2] == 0),
                             (step[0] == gi - 1) & (step[1] == gj - 1) & (step[2] == nk - 1))
        if nk == 1:
            ch = chunk or tm
            bvs = [b_ref[...].astype(BF16) for b_ref in b_refs]
            for r0 in range(0, tm, ch):
                rows = slice(r0, r0 + ch)
                av = (a_ref[:, rows] if ta else a_ref[rows, :]).astype(BF16)
                accs = [lax.dot_general(av, bv, dims, preferred_element_type=F32) for bv in bvs]
                ex = [e[rows, :] for e in e_refs[:ne - len(consts)]] + [e[...] for e in e_refs[ne - len(consts):]]
                for o_ref, v in zip(o_refs, epilogue(accs, ex)):
                    o_ref[rows, :] = v.astype(o_ref.dtype)
            finish()
            return
        kk = step[2]

        @pl.when(kk == 0)
        def _():
            for acc in acc_refs:
                acc[...] = jnp.zeros_like(acc)

        av = a_ref[...].astype(BF16)
        for b_ref, acc in zip(b_refs, acc_refs):
            acc[...] += lax.dot_general(av, b_ref[...].astype(BF16), dims, preferred_element_type=F32)

        @pl.when(kk == nk - 1)
        def _():
            outs = epilogue([acc[...] for acc in acc_refs], [e[...] for e in e_refs])
            for o_ref, v in zip(o_refs, outs):
                o_ref[...] = v.astype(o_ref.dtype)

        finish()

    a_spec = pl.BlockSpec((tk, tm), lambda i, j, kk: (kk, i)) if ta else pl.BlockSpec((tm, tk), lambda i, j, kk: (i, kk))
    b_spec = pl.BlockSpec((tn, tk), lambda i, j, kk: (j, kk)) if tb else pl.BlockSpec((tk, tn), lambda i, j, kk: (kk, j))
    mn_spec = pl.BlockSpec((tm, tn), lambda i, j, kk: (i, j))
    blocks = (_nbytes((tm, tk), a.dtype) + sum(_nbytes((tk, tn), b.dtype) for b in bs)
              + sum(_nbytes((tm, tn), e.dtype) for e in extras) + sum(_nbytes((tm, tn), d) for d in out_dtypes)
              + sum(_nbytes((tm, e.shape[1]), e.dtype) for e in row_extras))
    temps = (nb + 4) * _nbytes((tm, tn), F32)
    outs = pl.pallas_call(
        body, name=name, grid=(m // tm, n // tn, nk),
        in_specs=[a_spec] + [b_spec] * nb + [mn_spec] * len(extras)
        + [pl.BlockSpec((tm, e.shape[1]), lambda i, j, kk: (i, 0)) for e in row_extras]
        + [pl.BlockSpec(e.shape, lambda i, j, kk: (0, 0)) for e in consts] + [ANY] * rn,
        out_specs=[mn_spec] * no + [ANY] * rn,
        out_shape=[jax.ShapeDtypeStruct((m, n), d) for d in out_dtypes] + (list(ride[2]) if ride else []),
        scratch_shapes=[pltpu.VMEM((tm, tn), F32) for _ in range(n_acc)] + (_dma_sems(ride[3] * rn) if ride else []),
        compiler_params=pltpu.CompilerParams(
            dimension_semantics=("arbitrary",) * 3 if ride else ("parallel", "parallel", "arbitrary"),
            vmem_limit_bytes=_vmem_limit(blocks, temps)),
    )(a, *bs, *extras, *row_extras, *consts, *(ride[1] if ride else ()))
    if ride:
        return (outs[0] if no == 1 else outs[:no]), list(outs[no:])
    return outs[0] if no == 1 else outs


def _rowwise(body, *, name, rows, tr, row_ins, full_ins=(), row_outs=(), acc_outs=(), ride=None):
    tr = min(tr, rows)
    assert rows % tr == 0
    n_ri, n_fi, n_ro, n_ao = len(row_ins), len(full_ins), len(row_outs), len(acc_outs)
    rn = len(ride[1]) if ride else 0

    def kern(*refs):
        i = pl.program_id(0)
        n_in, n_out = n_ri + n_fi, n_ro + n_ao
        outs = refs[n_in + rn:n_in + rn + n_out]
        finish = _ride_along(ride[0] if ride else None, rn,
                             (refs[n_in:n_in + rn], refs[n_in + rn + n_out:n_in + 2 * rn + n_out],
                              *refs[n_in + 2 * rn + n_out:]), i == 0, i == rows // tr - 1)
        body(i, refs[:n_ri], refs[n_ri:n_in], outs[:n_ro], outs[n_ro:])
        finish()

    in_specs = [pl.BlockSpec((tr, w), functools.partial(lambda i, c: (i, c), c=ci)) for _, w, ci in row_ins]
    in_specs += [pl.BlockSpec(f.shape, lambda i: (0, 0)) for f in full_ins]
    in_specs += [ANY] * rn
    out_specs = [pl.BlockSpec((tr, w), lambda i: (i, 0)) for w, _ in row_outs]
    out_specs += [pl.BlockSpec(s, lambda i: (0, 0)) for s, _ in acc_outs] + [ANY] * rn
    out_shape = [jax.ShapeDtypeStruct((rows, w), d) for w, d in row_outs]
    out_shape += [jax.ShapeDtypeStruct(s, d) for s, d in acc_outs] + (list(ride[2]) if ride else [])
    blocks = (sum(_nbytes((tr, w), a.dtype) for a, w, _ in row_ins) + sum(_nbytes(f.shape, f.dtype) for f in full_ins)
              + sum(_nbytes((tr, w), d) for w, d in row_outs) + sum(_nbytes(s, d) for s, d in acc_outs))
    widest = max([w for _, w, _ in row_ins] + [w for w, _ in row_outs])
    outs = pl.pallas_call(
        kern, name=name, grid=(rows // tr,), in_specs=in_specs, out_specs=out_specs, out_shape=out_shape,
        scratch_shapes=_dma_sems(ride[3] * rn) if ride else [],
        compiler_params=pltpu.CompilerParams(
            dimension_semantics=("arbitrary",) if acc_outs or ride else ("parallel",),
            vmem_limit_bytes=_vmem_limit(blocks, 8 * _nbytes((tr, widest), F32))),
    )(*[a for a, _, _ in row_ins], *full_ins, *(ride[1] if ride else ()))
    return outs


def _rms(x, g):
    r = lax.rsqrt(jnp.mean(x * x, axis=-1, keepdims=True) + EPS)
    return x * r * g


def _rms_bwd(x, g, dy):
    r = lax.rsqrt(jnp.mean(x * x, axis=-1, keepdims=True) + EPS)
    xh = x * r
    dxh = dy * g
    dx = r * (dxh - xh * jnp.mean(dxh * xh, axis=-1, keepdims=True))
    return dx, jnp.sum(dy * xh, axis=0, keepdims=True)


def _sigmoid(x):
    return 1.0 / (1.0 + jnp.exp(-x))


def _acc_init(i, refs):
    @pl.when(i == 0)
    def _():
        for r in refs:
            r[...] = jnp.zeros_like(r)


def _rms_fwd_call(x, g, name):
    rows, c = x.shape

    def body(i, ins, fulls, outs, accs):
        outs[0][...] = _rms(ins[0][...], fulls[0][...]).astype(BF16)

    return _rowwise(body, name=name, rows=rows, tr=512, row_ins=[(x, c, 0)], full_ins=[g], row_outs=[(c, BF16)])[0]


def _rms_bwd_call(x, g, dy, res, name, ride=None):
    rows, c = x.shape
    row_ins = [(x, c, 0), (dy, c, 0)] + ([(res, c, 0)] if res is not None else [])

    def body(i, ins, fulls, outs, accs):
        _acc_init(i, accs)
        dx, dg = _rms_bwd(ins[0][...], fulls[0][...], ins[1][...].astype(F32))
        if res is not None:
            dx = dx + ins[2][...]
        outs[0][...] = dx
        accs[0][...] += dg

    return _rowwise(body, name=name, rows=rows, tr=512, row_ins=row_ins, full_ins=[g], row_outs=[(c, F32)],
                    acc_outs=[((1, c), F32)], ride=ride)


def _rope_tables(pos_col, freq_lane):
    rows = pos_col.shape[0]

    def body(i, ins, fulls, outs, accs):
        ang = ins[0][...].astype(F32) * fulls[0][...]
        lane = lax.broadcasted_iota(jnp.int32, ang.shape, 1)
        cos, sin = jnp.cos(ang), jnp.sin(ang)
        first = (lane >= ROPE_LO) & (lane < ROPE_LO + HALF)
        second = (lane >= ROPE_LO + HALF) & (lane < ROPE_LO + MLA_ROPE)
        outs[0][:, 0:LANE] = jnp.where(first | second, cos, 1.0)
        outs[0][:, LANE:2 * LANE] = jnp.where(first, -sin, 0.0)
        outs[0][:, 2 * LANE:3 * LANE] = jnp.where(second, sin, 0.0)

    return _rowwise(body, name="rope_tables", rows=rows, tr=1024, row_ins=[(pos_col, 1, 0)], full_ins=[freq_lane],
                    row_outs=[(3 * LANE, F32)])[0]


def _rope(x, tab):
    return (x * tab[:, 0:LANE] + pltpu.roll(x, LANE - HALF, 1) * tab[:, LANE:2 * LANE]
            + pltpu.roll(x, HALF, 1) * tab[:, 2 * LANE:3 * LANE])


def _rope_t(dy, tab):
    return (dy * tab[:, 0:LANE] + pltpu.roll(dy * tab[:, LANE:2 * LANE], HALF, 1)
            + pltpu.roll(dy * tab[:, 2 * LANE:3 * LANE], LANE - HALF, 1))


def _mla_prep_fwd(proj, tab, g_q, g_kv):
    rows = proj.shape[0]

    def body(i, ins, fulls, outs, accs):
        outs[0][...] = _rms(ins[0][...].astype(F32), fulls[0][...]).astype(BF16)
        outs[1][...] = _rms(ins[1][...].astype(F32), fulls[1][...]).astype(BF16)
        outs[2][...] = _rope(ins[2][...].astype(F32), ins[3][...])

    return _rowwise(body, name="mla_prep_fwd", rows=rows, tr=512,
                    row_ins=[(proj, MLA_Q_RANK, C_CQ // MLA_Q_RANK), (proj, LANE, C_CKV // LANE),
                             (proj, LANE, C_KR // LANE), (tab, 3 * LANE, 0)],
                    full_ins=[g_q, g_kv], row_outs=[(MLA_Q_RANK, BF16), (MLA_KV_RANK, BF16), (LANE, F32)])


def _mla_prep_bwd(proj, g_q, g_kv, dcqn, dckvn):
    rows = proj.shape[0]

    def body(i, ins, fulls, outs, accs):
        _acc_init(i, accs)
        dcq, dgq = _rms_bwd(ins[0][...].astype(F32), fulls[0][...], ins[2][...])
        dckv, dgkv = _rms_bwd(ins[1][...].astype(F32), fulls[1][...], ins[3][...])
        outs[0][...] = dcq.astype(BF16)
        outs[1][...] = dckv.astype(BF16)
        accs[0][...] += dgq
        accs[1][...] += dgkv

    return _rowwise(body, name="mla_prep_bwd", rows=rows, tr=512,
                    row_ins=[(proj, MLA_Q_RANK, C_CQ // MLA_Q_RANK), (proj, LANE, C_CKV // LANE),
                             (dcqn, MLA_Q_RANK, 0), (dckvn, MLA_KV_RANK, 0)],
                    full_ins=[g_q, g_kv], row_outs=[(MLA_Q_RANK, BF16), (MLA_KV_RANK, BF16)],
                    acc_outs=[((1, MLA_Q_RANK), F32), ((1, MLA_KV_RANK), F32)])


def _per_head(fn, x):
    return jnp.concatenate([fn(x[:, h * LANE:(h + 1) * LANE]) for h in range(x.shape[1] // LANE)], axis=1)


def _mla_rope_bwd(dq, dk, dv, tab):
    rows = dq.shape[0]
    hw = MLA_HEADS * LANE

    def body(i, ins, fulls, outs, accs):
        t = ins[3][...]
        dkr = jnp.zeros((ins[0].shape[0], LANE), F32)
        for h in range(MLA_HEADS):
            sl = slice(h * LANE, (h + 1) * LANE)
            outs[0][:, sl] = _rope_t(ins[0][:, sl], t).astype(BF16)
            dkr = dkr + ins[1][:, sl]
        outs[1][:, 0:hw] = ins[1][...].astype(BF16)
        outs[1][:, hw:2 * hw] = ins[2][...].astype(BF16)
        lane = lax.broadcasted_iota(jnp.int32, dkr.shape, 1)
        dkr = jnp.where((lane >= ROPE_LO) & (lane < ROPE_LO + MLA_ROPE), dkr, 0.0)
        outs[2][...] = _rope_t(dkr, t).astype(BF16)

    return _rowwise(body, name="mla_rope_bwd", rows=rows, tr=512,
                    row_ins=[(dq, hw, 0), (dk, hw, 0), (dv, hw, 0), (tab, 3 * LANE, 0)],
                    row_outs=[(hw, BF16), (2 * hw, BF16), (LANE, BF16)])


def _dot_nt(a, b):
    return lax.dot_general(a, b, (((1,), (1,)), ((), ())), preferred_element_type=F32)


def _dot_tn(a, b):
    return lax.dot_general(a, b, (((0,), (0,)), ((), ())), preferred_element_type=F32)


def _dot(a, b):
    return jnp.dot(a, b, preferred_element_type=F32)


def _attn_params(s, t, n_res_f32, n_res_bf16, ride=False):
    blocks = n_res_f32 * _nbytes((s, LANE), F32) + n_res_bf16 * _nbytes((s, LANE), BF16) + 6 * _nbytes((t, LANE), F32)
    return pltpu.CompilerParams(dimension_semantics=("arbitrary" if ride else "parallel", "arbitrary"),
                                vmem_limit_bytes=_vmem_limit(blocks, 12 * _nbytes((t, t), F32)))


def _mla_fwd(q, k, v, t, tk, shards=()):
    s, hw = q.shape
    heads, nq, r = hw // LANE, s // t, t // tk
    ng = len(shards)

    def body(q_ref, k_ref, v_ref, *rest):
        o_ref, l_ref = rest[ng], rest[ng + 1]
        h, i = pl.program_id(0), pl.program_id(1)
        if ng:
            gather = _GatherPhases(rest[:ng], rest[ng + 2:2 * ng + 2], *rest[2 * ng + 2:])
            pl.when((h == 0) & (i == 0))(gather.send)
            pl.when((h == heads // 2) & (i == 0))(gather.forward)
        qv = q_ref[...]

        def step(j, carry, off):
            m, l, acc = carry
            sl = pl.ds(pl.multiple_of(j * tk, tk), tk)
            sc = _dot_nt(qv, k_ref[sl, :])
            if off is not None:
                row = lax.broadcasted_iota(jnp.int32, (t, tk), 0)
                col = lax.broadcasted_iota(jnp.int32, (t, tk), 1)
                sc = jnp.where(col + off <= row, sc, -1e30)
            m_new = jnp.maximum(m, jnp.max(sc, axis=1, keepdims=True))
            p = jnp.exp2(sc - m_new)
            alpha = jnp.exp2(m - m_new)
            l = alpha * l + jnp.sum(p, axis=1, keepdims=True)
            acc = alpha * acc + _dot(p.astype(BF16), v_ref[sl, :])
            return m_new, l, acc

        init = (jnp.full((t, 1), -1e30, F32), jnp.zeros((t, 1), F32), jnp.zeros((t, LANE), F32))
        carry = lax.fori_loop(0, i * r, lambda j, c: step(j, c, None), init)
        for jj in range(r):
            carry = step(i * r + jj, carry, jj * tk)
        m, l, acc = carry
        o_ref[...] = (acc / l).astype(o_ref.dtype)
        l_ref[0] = m + jnp.log2(l)
        if ng:
            pl.when((h == heads - 1) & (i == nq - 1))(gather.finish)

    blk = pl.BlockSpec((t, LANE), lambda h, i: (i, h))
    res = pl.BlockSpec((s, LANE), lambda h, i: (0, h))
    outs = pl.pallas_call(
        body, name="mla_fwd", grid=(heads, nq), in_specs=[blk, res, res] + [ANY] * ng,
        out_specs=[blk, pl.BlockSpec((1, t, 1), lambda h, i: (h, i, 0))] + [ANY] * ng,
        out_shape=[jax.ShapeDtypeStruct((s, hw), BF16), jax.ShapeDtypeStruct((heads, s, 1), F32)]
        + [jax.ShapeDtypeStruct((N_CHIPS,) + sh.shape, sh.dtype) for sh in shards],
        scratch_shapes=_dma_sems(6 * ng) + [pltpu.SemaphoreType.DMA((ng,))] if ng else [],
        compiler_params=_attn_params(s, t, 0, 2, ride=ng > 0),
    )(q, k, v, *shards)
    return outs[0], outs[1], list(outs[2:])


def _mla_bwd(q, k, v, o, do, lse, t, tk, parts=()):
    s, hw = q.shape
    heads, nq, r = hw // LANE, s // t, t // tk
    ns = len(parts)

    def body(q_ref, k_ref, v_ref, o_ref, do_ref, l_ref, *rest):
        dq_ref, dk_ref, dv_ref = rest[ns:ns + 3]
        h, i = pl.program_id(0), pl.program_id(1)
        finish = _ride_along(_scatter_copies, ns, (rest[:ns], rest[ns + 3:2 * ns + 3], *rest[2 * ns + 3:]),
                             (h == 0) & (i == 0), (h == heads - 1) & (i == nq - 1))

        @pl.when(i == 0)
        def _():
            dk_ref[...] = jnp.zeros_like(dk_ref)
            dv_ref[...] = jnp.zeros_like(dv_ref)

        qv, dov, lv = q_ref[...], do_ref[...], l_ref[0]
        dlt = jnp.sum(dov.astype(F32) * o_ref[...].astype(F32), axis=1, keepdims=True)

        def step(j, dq, off):
            sl = pl.ds(pl.multiple_of(j * tk, tk), tk)
            kv, vv = k_ref[sl, :], v_ref[sl, :]
            p = jnp.exp2(_dot_nt(qv, kv) - lv)
            if off is not None:
                row = lax.broadcasted_iota(jnp.int32, (t, tk), 0)
                col = lax.broadcasted_iota(jnp.int32, (t, tk), 1)
                p = jnp.where(col + off <= row, p, 0.0)
            ds = (p * (_dot_nt(dov, vv) - dlt)).astype(BF16)
            dk_ref[sl, :] += _dot_tn(ds, qv) * (1.0 / LOG2E)
            dv_ref[sl, :] += _dot_tn(p.astype(BF16), dov)
            return dq + _dot(ds, kv)

        dq = lax.fori_loop(0, i * r, lambda j, c: step(j, c, None), jnp.zeros((t, LANE), F32))
        for jj in range(r):
            dq = step(i * r + jj, dq, jj * tk)
        dq_ref[...] = dq * MLA_SCALE
        finish()

    blk = pl.BlockSpec((t, LANE), lambda h, i: (i, h))
    res = pl.BlockSpec((s, LANE), lambda h, i: (0, h))
    full = jax.ShapeDtypeStruct((s, hw), F32)
    outs = pl.pallas_call(
        body, name="mla_bwd", grid=(heads, nq),
        in_specs=[blk, res, res, blk, blk, pl.BlockSpec((1, t, 1), lambda h, i: (h, i, 0))] + [ANY] * ns,
        out_specs=[blk, res, res] + [ANY] * ns, out_shape=[full, full, full] + _scatter_shapes(parts),
        scratch_shapes=_dma_sems(3 * ns) if ns else [],
        compiler_params=_attn_params(s, t, 2, 2, ride=ns > 0),
    )(q, k, v, o, do, lse, *parts)
    return outs[0], outs[1], outs[2], list(outs[3:])


def _sb_logits(qv, kv, keep, upper):
    z = _dot_nt(qv, kv)
    e = jnp.exp(-jnp.abs(z))
    l1p = jnp.log(1.0 + e)
    lb = jnp.minimum(z, 0.0) - l1p
    lo = -jnp.maximum(z, 0.0) - l1p
    if keep is not None:
        lo = jnp.where(keep, lo, 0.0)
    hi = lo.astype(BF16)
    rem = (lo - hi.astype(F32)).astype(BF16)
    suf = _dot(hi, upper) + _dot(rem, upper)
    return z, e, lb, lo, suf


def _tri(t, inclusive):
    row = lax.broadcasted_iota(jnp.int32, (t, t), 0)
    col = lax.broadcasted_iota(jnp.int32, (t, t), 1)
    return jnp.where((row >= col) if inclusive else (row > col), 1.0, 0.0).astype(BF16)


SB_QBLOCKS = 2
SB_PAIR = LANE // SB_HEAD_DIM
SB_CHAINS = SB_QBLOCKS * SB_PAIR
SB_FIRST = 2


def _sb_first_tile(b, t):
    start = jnp.maximum(b - (SB_FIRST - 1), 0) * t
    row = lax.broadcasted_iota(jnp.int32, (t, SB_FIRST * t), 0)
    col = lax.broadcasted_iota(jnp.int32, (t, SB_FIRST * t), 1)
    return pl.ds(pl.multiple_of(start, t), SB_FIRST * t), col + start < row + b * t


def _sb_head(x, hh):
    lane = lax.broadcasted_iota(jnp.int32, x.shape, 1)
    return jnp.where(lane // SB_HEAD_DIM == hh, x, jnp.zeros_like(x))


def _sb_walk(i, first, carries_of):
    n = SB_CHAINS
    carries = [first(c) for c in range(n)]
    width = len(carries[0])

    def alive(carry):
        return jnp.max(carry[0]) >= SB_ZERO_LOG

    def split(st):
        return [tuple(st[1 + c * width:1 + (c + 1) * width]) for c in range(n)]

    def live(st):
        any_alive = alive(split(st)[0])
        for cr in split(st)[1:]:
            any_alive = any_alive | alive(cr)
        return (st[0] <= SB_QBLOCKS * i) & any_alive

    def more(st):
        out = (st[0] + 1,)
        for c, cr in enumerate(split(st)):
            out += tuple(carries_of(c, st[0], cr))
        return out

    st = lax.while_loop(live, more, (jnp.int32(SB_FIRST),) + tuple(x for cr in carries for x in cr))
    jj, carries = st[0], split(st)
    for c in range(SB_PAIR, n):
        def live_c(s2, c=c):
            return (s2[0] <= SB_QBLOCKS * i + c // SB_PAIR) & alive(s2[1:])

        def more_c(s2, c=c):
            return (s2[0] + 1,) + tuple(carries_of(c, s2[0], s2[1:]))

        carries[c] = lax.while_loop(live_c, more_c, (jj,) + tuple(carries[c]))[1:]
    return carries


def _sb_fwd(proj, t):
    s = proj.shape[0]
    pairs, nq, nb = SB_HEADS // SB_PAIR, s // t, SB_QBLOCKS

    def body(q_ref, k_ref, v_ref, o_ref):
        i = pl.program_id(1)
        upper, upper_first = _tri(t, False), _tri(SB_FIRST * t, False)
        rows = [slice((c // SB_PAIR) * t, (c // SB_PAIR + 1) * t) for c in range(SB_CHAINS)]
        qs = [_sb_head(q_ref[rows[c], :] * SB_SCALE, c % SB_PAIR).astype(BF16) for c in range(SB_CHAINS)]

        def first(c):
            sl, keep = _sb_first_tile(nb * i + c // SB_PAIR, t)
            _, _, lb, lo, suf = _sb_logits(qs[c], k_ref[sl, :].astype(BF16), keep, upper_first)
            a = jnp.where(keep, jnp.exp(lb + suf), 0.0)
            return (jnp.sum(lo, axis=1, keepdims=True),
                    _dot(a.astype(BF16), _sb_head(v_ref[sl, :], c % SB_PAIR).astype(BF16)))

        def step(c, jj, carry):
            run, acc = carry
            sl = pl.ds(pl.multiple_of((nb * i + c // SB_PAIR - jj) * t, t), t)
            _, _, lb, lo, suf = _sb_logits(qs[c], k_ref[sl, :].astype(BF16), None, upper)
            a = jnp.exp(lb + suf + run)
            acc = acc + _dot(a.astype(BF16), _sb_head(v_ref[sl, :], c % SB_PAIR).astype(BF16))
            return run + jnp.sum(lo, axis=1, keepdims=True), acc

        carries = _sb_walk(i, first, step)
        for qb in range(nb):
            o_ref[qb * t:(qb + 1) * t, :] = sum(carries[qb * SB_PAIR + hh][1] for hh in range(SB_PAIR))

    return pl.pallas_call(
        body, name="sb_fwd", grid=(pairs, nq // nb),
        in_specs=[pl.BlockSpec((nb * t, LANE), lambda h, i: (i, C_SBQ // LANE + h)),
                  pl.BlockSpec((s, LANE), lambda h, i: (0, C_SBK // LANE + h)),
                  pl.BlockSpec((s, LANE), lambda h, i: (0, C_SBV // LANE + h))],
        out_specs=pl.BlockSpec((nb * t, LANE), lambda h, i: (i, h)),
        out_shape=jax.ShapeDtypeStruct((s, pairs * LANE), F32),
        compiler_params=_attn_params(s, nb * t, 0, 2),
    )(proj, proj, proj)


def _sb_bwd(proj, o, do, t, stacked=()):
    s = proj.shape[0]
    pairs, nq, nb = SB_HEADS // SB_PAIR, s // t, SB_QBLOCKS
    nx = len(stacked)

    def body(q_ref, k_ref, v_ref, o_ref, do_ref, *rest):
        dq_ref, dk_ref, dv_ref = rest[nx:nx + 3]
        hd, i = pl.program_id(0), pl.program_id(1)
        finish = _ride_along(_exchange_copies, nx, (rest[:nx], rest[nx + 3:2 * nx + 3], *rest[2 * nx + 3:]),
                             (hd == 0) & (i == 0), (hd == pairs - 1) & (i == nq // nb - 1))

        @pl.when(i == 0)
        def _():
            dk_ref[...] = jnp.zeros_like(dk_ref)
            dv_ref[...] = jnp.zeros_like(dv_ref)

        rows = [slice((c // SB_PAIR) * t, (c // SB_PAIR + 1) * t) for c in range(SB_CHAINS)]
        qs = [_sb_head(q_ref[rows[c], :] * SB_SCALE, c % SB_PAIR).astype(BF16) for c in range(SB_CHAINS)]
        dos = [_sb_head(do_ref[rows[c], :], c % SB_PAIR) for c in range(SB_CHAINS)]
        totals = [jnp.sum(dos[c].astype(F32) * o_ref[rows[c], :], axis=1, keepdims=True) for c in range(SB_CHAINS)]
        tris = {1: (_tri(t, False), _tri(t, True)), SB_FIRST: (_tri(SB_FIRST * t, False), _tri(SB_FIRST * t, True))}

        def tile(c, sl, keep, blocks, carry):
            run, g, dq = carry
            qv, dov = qs[c], dos[c]
            upper, upper_incl = tris[blocks]
            kv, vv = k_ref[sl, :].astype(BF16), v_ref[sl, :].astype(BF16)
            z, e, lb, lo, suf = _sb_logits(qv, kv, keep, upper)
            tail = suf + run
            a = jnp.exp(lb + tail)
            if keep is not None:
                a = jnp.where(keep, a, 0.0)
            ab = a.astype(BF16)
            gr = ab.astype(F32) * _dot_nt(dov, vv)
            ghi = gr.astype(BF16)
            grem = (gr - ghi.astype(F32)).astype(BF16)
            before = totals[c] - g - (_dot(ghi, upper_incl) + _dot(grem, upper_incl))
            before = jnp.where(tail < SB_ZERO_LOG, 0.0, before)
            r = 1.0 / (1.0 + e)
            pos = z >= 0.0
            dz = r * (gr * jnp.where(pos, e, 1.0) - before * jnp.where(pos, 1.0, e))
            if keep is not None:
                dz = jnp.where(keep, dz, 0.0)
            dzb = dz.astype(BF16)
            dk_ref[sl, :] += _dot_tn(dzb, qv)
            dv_ref[sl, :] += _dot_tn(ab, dov)
            return (run + jnp.sum(lo, axis=1, keepdims=True), g + jnp.sum(gr, axis=1, keepdims=True),
                    dq + _dot(dzb, _sb_head(kv, c % SB_PAIR)))

        zero = jnp.zeros((t, 1), F32)
        init = (zero, zero, jnp.zeros((t, LANE), F32))

        def first(c):
            sl, keep = _sb_first_tile(nb * i + c // SB_PAIR, t)
            return tile(c, sl, keep, SB_FIRST, init)

        def step(c, jj, carry):
            return tile(c, pl.ds(pl.multiple_of((nb * i + c // SB_PAIR - jj) * t, t), t), None, 1, carry)

        carries = _sb_walk(i, first, step)
        for qb in range(nb):
            dq_ref[qb * t:(qb + 1) * t, :] = sum(carries[qb * SB_PAIR + hh][2] for hh in range(SB_PAIR)) * SB_SCALE
        finish()

    blk = pl.BlockSpec((nb * t, LANE), lambda h, i: (i, h))
    res = pl.BlockSpec((s, LANE), lambda h, i: (0, h))
    full = jax.ShapeDtypeStruct((s, pairs * LANE), F32)
    outs = pl.pallas_call(
        body, name="sb_bwd", grid=(pairs, nq // nb),
        in_specs=[pl.BlockSpec((nb * t, LANE), lambda h, i: (i, C_SBQ // LANE + h)),
                  pl.BlockSpec((s, LANE), lambda h, i: (0, C_SBK // LANE + h)),
                  pl.BlockSpec((s, LANE), lambda h, i: (0, C_SBV // LANE + h)), blk, blk] + [ANY] * nx,
        out_specs=[blk, res, res] + [ANY] * nx, out_shape=[full, full, full] + _exchange_shapes(stacked),
        scratch_shapes=_dma_sems(nx) if nx else [],
        compiler_params=_attn_params(s, nb * t, 2, 2, ride=nx > 0),
    )(proj, proj, proj, o, do, *stacked)
    return outs[0], outs[1], outs[2], list(outs[3:])


def _xattn_probs(qh, kh):
    sc = _dot_nt(qh, kh) * (1.0 / math.sqrt(X_HEAD_DIM))
    p = jnp.exp(sc - jnp.max(sc, axis=1, keepdims=True))
    return p / jnp.sum(p, axis=1, keepdims=True)


def _xattn_fwd(xq, xkv):
    rows = xq.shape[0]
    w = X_HEADS * X_HEAD_DIM

    def body(i, ins, fulls, outs, accs):
        for h in range(X_HEADS):
            sl = slice(h * LANE, (h + 1) * LANE)
            p = _xattn_probs(ins[0][:, sl], fulls[0][:, sl])
            outs[0][:, sl] = _dot(p.astype(BF16), fulls[0][:, w + h * LANE:w + (h + 1) * LANE]).astype(BF16)

    return _rowwise(body, name="xattn_fwd", rows=rows, tr=512, row_ins=[(xq, w, 0)], full_ins=[xkv],
                    row_outs=[(w, BF16)])[0]


def _xattn_bwd(xq, xkv, dxo):
    rows = xq.shape[0]
    w = X_HEADS * X_HEAD_DIM

    def body(i, ins, fulls, outs, accs):
        _acc_init(i, accs)
        for h in range(X_HEADS):
            sl = slice(h * LANE, (h + 1) * LANE)
            slv = slice(w + h * LANE, w + (h + 1) * LANE)
            qh, kh, vh, doh = ins[0][:, sl], fulls[0][:, sl], fulls[0][:, slv], ins[1][:, sl]
            p = _xattn_probs(qh, kh)
            dp = _dot_nt(doh, vh)
            ds = (p * (dp - jnp.sum(p * dp, axis=1, keepdims=True)) * (1.0 / math.sqrt(X_HEAD_DIM))).astype(BF16)
            outs[0][:, sl] = _dot(ds, kh).astype(BF16)
            accs[0][:, sl] += _dot_tn(ds, qh)
            accs[0][:, slv] += _dot_tn(p.astype(BF16), doh)

    return _rowwise(body, name="xattn_bwd", rows=rows, tr=512, row_ins=[(xq, w, 0), (dxo, w, 0)], full_ins=[xkv],
                    row_outs=[(w, BF16)], acc_outs=[(xkv.shape, F32)])


def _gate_fwd(proj, pa, pb, b_gate):
    rows = proj.shape[0]

    def body(i, ins, fulls, outs, accs):
        sa = _sigmoid(ins[0][...].astype(F32) + fulls[0][0:1, :])
        sb = _sigmoid(ins[1][...].astype(F32) + fulls[0][1:2, :])
        outs[0][...] = (sa * ins[2][...].astype(F32) + sb * ins[3][...].astype(F32)).astype(BF16)

    return _rowwise(body, name="gate_fwd", rows=rows, tr=512,
                    row_ins=[(proj, D_MODEL, C_GA // D_MODEL), (proj, D_MODEL, C_GB // D_MODEL), (pa, D_MODEL, 0),
                             (pb, D_MODEL, 0)],
                    full_ins=[b_gate], row_outs=[(D_MODEL, BF16)])[0]


def _gate_bwd(proj, pa, pb, b_gate, dm):
    rows = proj.shape[0]

    def body(i, ins, fulls, outs, accs):
        _acc_init(i, accs)
        d = ins[4][...].astype(F32)
        sa = _sigmoid(ins[0][...].astype(F32) + fulls[0][0:1, :])
        sb = _sigmoid(ins[1][...].astype(F32) + fulls[0][1:2, :])
        dga = d * ins[2][...].astype(F32) * sa * (1.0 - sa)
        dgb = d * ins[3][...].astype(F32) * sb * (1.0 - sb)
        outs[0][...] = (d * sa).astype(BF16)
        outs[1][...] = (d * sb).astype(BF16)
        outs[2][...] = dga.astype(BF16)
        outs[3][...] = dgb.astype(BF16)
        accs[0][0:1, :] += jnp.sum(dga, axis=0, keepdims=True)
        accs[0][1:2, :] += jnp.sum(dgb, axis=0, keepdims=True)

    return _rowwise(body, name="gate_bwd", rows=rows, tr=512,
                    row_ins=[(proj, D_MODEL, C_GA // D_MODEL), (proj, D_MODEL, C_GB // D_MODEL), (pa, D_MODEL, 0),
                             (pb, D_MODEL, 0), (dm, D_MODEL, 0)],
                    full_ins=[b_gate], row_outs=[(D_MODEL, BF16)] * 4, acc_outs=[((2, D_MODEL), F32)])


def _loss_head(x3, target, g_final):
    rows = x3.shape[0]

    def body(i, ins, fulls, outs, accs):
        _acc_init(i, accs)
        xv, g = ins[0][...], fulls[0][...]
        d = _rms(xv, g) - ins[1][...]
        dx, dg = _rms_bwd(xv, g, d * (1.0 / D_MODEL))
        outs[0][...] = dx
        accs[0][...] += dg
        accs[1][...] += jnp.sum(d * d, axis=0, keepdims=True)

    return _rowwise(body, name="loss_head", rows=rows, tr=512, row_ins=[(x3, D_MODEL, 0), (target, D_MODEL, 0)],
                    full_ins=[g_final], row_outs=[(D_MODEL, F32)], acc_outs=[((1, D_MODEL), F32), ((1, D_MODEL), F32)])


def _adamw(w, g, m, v, name):
    rows, c = w.shape

    def body(i, ins, fulls, outs, accs):
        wv, gv = ins[0][...], ins[1][...]
        mn = ADAM_B1 * ins[2][...] + (1.0 - ADAM_B1) * gv
        vn = ADAM_B2 * ins[3][...] + (1.0 - ADAM_B2) * jnp.square(gv)
        m_hat = mn / (1.0 - ADAM_B1 ** ADAM_STEP)
        v_hat = vn / (1.0 - ADAM_B2 ** ADAM_STEP)
        outs[0][...] = -ADAM_LR * (m_hat / (jnp.sqrt(v_hat) + ADAM_EPS) + ADAM_WD * wv)
        outs[1][...] = mn
        outs[2][...] = vn

    return _rowwise(body, name=name, rows=rows, tr=_row_tile(rows, 256), row_ins=[(a, c, 0) for a in (w, g, m, v)],
                    row_outs=[(c, F32)] * 3)


def _place():
    x, y, c = lax.axis_index("x"), lax.axis_index("y"), lax.axis_index("c")
    chips = [(1 - x, y), (x, 1 - y), (1 - x, 1 - y)]
    return x, y, c, chips


ANY = pl.BlockSpec(memory_space=pl.ANY)


def _remote(src, dst, send_sem, recv_sem, to):
    return pltpu.make_async_remote_copy(src_ref=src, dst_ref=dst, send_sem=send_sem, recv_sem=recv_sem,
                                        device_id=to, device_id_type=MESH)


def _dma_sems(n):
    return [pltpu.SemaphoreType.DMA((n,)), pltpu.SemaphoreType.DMA((n,))]


class _GatherPhases:
    def __init__(self, ins, outs, send_sems, recv_sems, local_sems=None):
        self.ins, self.outs, self.send_sems, self.recv_sems = ins, outs, send_sems, recv_sems
        self.local_sems = local_sems
        self.x, self.y, self.c, self.chips = _place()
        self.me = 2 * self.x + self.y

    def _locals(self):
        if self.local_sems is None:
            return []
        return [pltpu.make_async_copy(self.ins[t], self.outs[t].at[self.me], self.local_sems.at[t])
                for t in range(len(self.ins))]

    def _copy(self, t, j, chip_idx, hlf, to, src=None):
        h = self.ins[t].shape[0] // 2
        dst = self.outs[t].at[chip_idx, pl.ds(hlf * h, h), :]
        return _remote(dst if src is None else src, dst, self.send_sems.at[6 * t + j], self.recv_sems.at[6 * t + j], to)

    def _sends(self):
        out = []
        for t in range(len(self.ins)):
            h = self.ins[t].shape[0] // 2
            for j, chip in enumerate(self.chips):
                out.append(self._copy(t, j, self.me, self.c, (*chip, self.c), src=self.ins[t].at[pl.ds(self.c * h, h), :]))
        return out

    def _forwards(self):
        return [self._copy(t, 3 + j, 2 * chip[0] + chip[1], self.c, (self.x, self.y, 1 - self.c))
                for t in range(len(self.ins)) for j, chip in enumerate(self.chips)]

    def send(self):
        for cp in self._sends() + self._locals():
            cp.start()

    def forward(self):
        here = (self.x, self.y, self.c)
        landed = [self._copy(t, j, 2 * chip[0] + chip[1], self.c, here)
                  for t in range(len(self.ins)) for j, chip in enumerate(self.chips)]
        for arrival, fwd in zip(landed, self._forwards()):
            arrival.wait_recv()
            fwd.start()

    def finish(self):
        here = (self.x, self.y, self.c)
        for t in range(len(self.ins)):
            for j, chip in enumerate(self.chips):
                self._copy(t, 3 + j, 2 * chip[0] + chip[1], 1 - self.c, here).wait_recv()
        for cp in self._sends() + self._forwards():
            cp.wait_send()
        for cp in self._locals():
            cp.wait()


def _all_gather_weights(shards):
    n = len(shards)

    def body(*refs):
        gather = _GatherPhases(refs[:n], refs[n:2 * n], *refs[2 * n:])
        gather.send()
        gather.forward()
        gather.finish()

    return pl.pallas_call(
        body, name="all_gather_weights", in_specs=[ANY] * n, out_specs=[ANY] * n,
        out_shape=[jax.ShapeDtypeStruct((N_CHIPS,) + s.shape, s.dtype) for s in shards],
        scratch_shapes=_dma_sems(6 * n),
    )(*shards)


def _exchange_copies(ins, outs, send_sems, recv_sems):
    x, y, c, _ = _place()
    cps = []
    for t in range(len(ins)):
        h = ins[t].shape[1] // 2
        cps.append(_remote(ins[t].at[:, pl.ds((1 - c) * h, h), :], outs[t], send_sems.at[t], recv_sems.at[t],
                           (x, y, 1 - c)))
    return cps


def _exchange_shapes(stacked):
    return [jax.ShapeDtypeStruct((N_CHIPS, s.shape[1] // 2, s.shape[2]), s.dtype) for s in stacked]


def _scatter_copies(ins, outs, send_sems, recv_sems):
    x, y, c, chips = _place()
    return [_remote(ins[t].at[2 * chip[0] + chip[1]], outs[t].at[j], send_sems.at[3 * t + j], recv_sems.at[3 * t + j],
                    (*chip, c)) for t in range(len(ins)) for j, chip in enumerate(chips)]


def _scatter_shapes(parts):
    return [jax.ShapeDtypeStruct((N_CHIPS - 1,) + s.shape[1:], s.dtype) for s in parts]


def _ride_along(copies_of, n, refs, first, last):
    if not n:
        return lambda: None

    def start():
        for cp in copies_of(*refs):
            cp.start()

    def wait():
        for cp in copies_of(*refs):
            cp.wait()

    pl.when(first)(start)
    return lambda: pl.when(last)(wait)


def _pair_exchange_halves(shards):
    n = len(shards)

    def body(*refs):
        bufs = refs[n:2 * n]
        send_sems, recv_sems = refs[2 * n:]
        x, y, c, _ = _place()
        cps = []
        for t in range(n):
            h = bufs[t].shape[0] // 2
            rows = bufs[t].at[pl.ds(c * h, h), :]
            cps.append(_remote(rows, rows, send_sems.at[t], recv_sems.at[t], (x, y, 1 - c)))
            cps[-1].start()
        for cp in cps:
            cp.wait()

    return pl.pallas_call(
        body, name="pair_exchange_halves", in_specs=[ANY] * n, out_specs=[ANY] * n,
        out_shape=[jax.ShapeDtypeStruct(s.shape, s.dtype) for s in shards],
        input_output_aliases={t: t for t in range(n)},
        scratch_shapes=_dma_sems(n),
    )(*shards)


def _pair_sum(gs, recv, place, name):
    _, r, cols = gs.shape
    h = r // 2

    def kern(p_ref, a_ref, b_ref, o_ref):
        o_ref[...] = (a_ref[...] + b_ref[...]).astype(BF16)

    blk = lambda f: pl.BlockSpec((1, h, cols), f)
    return pl.pallas_call(
        kern, name=name,
        grid_spec=pltpu.PrefetchScalarGridSpec(
            num_scalar_prefetch=1, grid=(N_CHIPS,),
            in_specs=[blk(lambda d, p: (d, p[1], 0)), blk(lambda d, p: (d, 0, 0))],
            out_specs=blk(lambda d, p: (d, 0, 0))),
        out_shape=jax.ShapeDtypeStruct((N_CHIPS, h, cols), BF16),
        compiler_params=pltpu.CompilerParams(dimension_semantics=("arbitrary",),
                                             vmem_limit_bytes=_vmem_limit(3 * _nbytes((h, cols), F32), 0)),
    )(place, gs, recv)


def _chip_sum(gs, recv, got, place, name):
    _, r, cols = gs.shape
    h = r // 2

    def kern(p_ref, a_ref, b_ref, g0, g1, g2, o_ref):
        own = a_ref[0] + b_ref[0]
        o_ref[...] = ((own + g0[0].astype(F32)) + g1[0].astype(F32)) + g2[0].astype(F32)

    blk = lambda f: pl.BlockSpec((1, h, cols), f)
    return pl.pallas_call(
        kern, name=name,
        grid_spec=pltpu.PrefetchScalarGridSpec(
            num_scalar_prefetch=1, grid=(1,),
            in_specs=[blk(lambda i, p: (p[0], p[1], 0)), blk(lambda i, p: (p[0], 0, 0)), blk(lambda i, p: (0, 0, 0)),
                      blk(lambda i, p: (1, 0, 0)), blk(lambda i, p: (2, 0, 0))],
            out_specs=pl.BlockSpec((h, cols), lambda i, p: (p[1], 0))),
        out_shape=jax.ShapeDtypeStruct((r, cols), F32),
        compiler_params=pltpu.CompilerParams(dimension_semantics=("arbitrary",),
                                             vmem_limit_bytes=_vmem_limit(5 * _nbytes((h, cols), F32), 0)),
    )(place, gs, recv, got, got, got)


def _all_reduce_small(vec, name):
    r, cols = vec.shape

    def body(in_ref, out_ref, gath, send_sems, recv_sems):
        x, y, c, _ = _place()
        me = 4 * x + 2 * y + c
        gath[me] = in_ref[...]
        sends = []
        for k in range(1, 8):
            to = (x ^ (k >> 2), y ^ ((k >> 1) & 1), c ^ (k & 1))
            cp = pltpu.make_async_remote_copy(src_ref=in_ref, dst_ref=gath.at[me], send_sem=send_sems.at[k - 1],
                                              recv_sem=recv_sems.at[k - 1], device_id=to, device_id_type=MESH)
            cp.start()
            sends.append(cp)
        for k in range(1, 8):
            peer = me ^ k
            pltpu.make_async_remote_copy(src_ref=in_ref, dst_ref=gath.at[peer], send_sem=send_sems.at[k - 1],
                                         recv_sem=recv_sems.at[k - 1], device_id=(x, y, c),
                                         device_id_type=MESH).wait_recv()
        for cp in sends:
            cp.wait_send()
        acc = gath[0]
        for d in range(1, 8):
            acc = acc + gath[d]
        out_ref[...] = acc

    vm = pl.BlockSpec(memory_space=pltpu.VMEM)
    return pl.pallas_call(
        body, name=name, in_specs=[vm], out_specs=vm,
        out_shape=jax.ShapeDtypeStruct((r, cols), F32),
        scratch_shapes=[pltpu.VMEM((8, r, cols), F32), pltpu.SemaphoreType.DMA((7,)), pltpu.SemaphoreType.DMA((7,))],
    )(vec)


def _pad_heads(w, heads, dim, axis):
    shp = w.shape[:axis] + (heads, dim) + w.shape[axis + 1:]
    pad = [(0, 0)] * len(shp)
    pad[axis + 1] = (0, LANE - dim)
    w = jnp.pad(w.reshape(shp), pad)
    return w.reshape(w.shape[:axis] + (heads * LANE,) + w.shape[axis + 2:])


def _unpad_heads(w, heads, dim, axis):
    shp = w.shape[:axis] + (heads, LANE) + w.shape[axis + 1:]
    w = lax.slice_in_dim(w.reshape(shp), 0, dim, axis=axis + 1)
    return w.reshape(w.shape[:axis] + (heads * dim,) + w.shape[axis + 2:])


def _w_in_layout(w_in):
    kr = jnp.pad(w_in[:, 384:416], ((0, 0), (ROPE_LO, LANE - ROPE_LO - MLA_ROPE)))
    sb = lambda lo: w_in[:, lo:lo + SB_WIDTH]
    return jnp.concatenate([w_in[:, 1952:2976], w_in[:, 2976:4000], sb(416), sb(928), sb(1440), w_in[:, 0:256],
                            w_in[:, 256:384], kr], axis=1)


def _w_in_unlayout(d):
    sb = lambda lo: d[:, lo:lo + SB_WIDTH]
    return jnp.concatenate([d[:, C_CQ:C_CQ + 256], d[:, C_CKV:C_CKV + 128], d[:, C_KR + ROPE_LO:C_KR + ROPE_LO + MLA_ROPE],
                            sb(C_SBQ), sb(C_SBK), sb(C_SBV), d[:, C_GA:C_GA + 1024], d[:, C_GB:C_GB + 1024]], axis=1)


def _w_ukv_layout(w):
    w3 = w.reshape(MLA_KV_RANK, MLA_HEADS, MLA_NOPE + MLA_V)
    pad = lambda part: jnp.pad(part, ((0, 0), (0, 0), (0, LANE - part.shape[2]))).reshape(MLA_KV_RANK, MLA_HEADS * LANE)
    return jnp.concatenate([pad(w3[:, :, :MLA_NOPE]), pad(w3[:, :, MLA_NOPE:])], axis=1)


def _w_ukv_unlayout(d):
    hw = MLA_HEADS * LANE
    kpart = d[:, :hw].reshape(MLA_KV_RANK, MLA_HEADS, LANE)[:, :, :MLA_NOPE]
    vpart = d[:, hw:].reshape(MLA_KV_RANK, MLA_HEADS, LANE)[:, :, :MLA_V]
    return jnp.concatenate([kpart, vpart], axis=2).reshape(MLA_KV_RANK, MLA_HEADS * (MLA_NOPE + MLA_V))


def _shard_of(full, d, axis):
    n = full.shape[axis] // N_CHIPS
    return lax.slice_in_dim(full, d * n, (d + 1) * n, axis=axis)


def _local_step(x, mem, pos, target, w, t_mla, t_sb, late=None, reduce=None):
    s = x.shape[0]
    w = dict(w)
    win = _w_in_layout(w["w_in"])
    wuq = _pad_heads(w["w_uq"], MLA_HEADS, MLA_NOPE + MLA_ROPE, 1)
    wkv = _w_ukv_layout(w["w_ukv"])
    inv_freq = ROPE_THETA ** (-jnp.arange(0, MLA_ROPE, 2, dtype=F32) / MLA_ROPE)
    freq_lane = jnp.pad(jnp.concatenate([inv_freq, inv_freq]), (ROPE_LO, LANE - ROPE_LO - MLA_ROPE)).reshape(1, LANE)
    add = lambda accs, ex: (accs[0] + ex[0],)

    def add_norm(accs, ex):
        y = accs[0] + ex[0]
        return y, _rms(y, ex[1])

    tab = _rope_tables(pos.reshape(s, 1), freq_lane)
    h = _rms_fwd_call(x, w["g_mix"], "rms_mix")
    proj = _mm(h, [win], name="proj_in", out_dtypes=(BF16,))
    cqn, ckvn, krope = _mla_prep_fwd(proj, tab, w["g_q_lat"], w["g_kv_lat"])
    hw = MLA_HEADS * LANE
    qa = _mm(cqn, [wuq], name="q_up", row_extras=(tab,), out_dtypes=(BF16,),
             epilogue=lambda accs, ex: (_per_head(lambda t: _rope(t, ex[0]) * (MLA_SCALE * LOG2E), accs[0]),))
    ka = _mm(ckvn, [wkv[:, :hw]], name="k_up", row_extras=(krope,), out_dtypes=(BF16,),
             epilogue=lambda accs, ex: (_per_head(lambda t: t + ex[0], accs[0]),))
    va = _mm(ckvn, [wkv[:, hw:]], name="v_up", out_dtypes=(BF16,))
    o_a, lse, gathered = _mla_fwd(qa, ka, va, t_mla[0], t_mla[1], late[0] if late else ())
    if late:
        w.update(late[1](gathered))
    wa = _pad_heads(w["w_a_proj"], MLA_HEADS, MLA_V, 0)
    wb = w["w_b_proj"]
    o_b = _sb_fwd(proj, t_sb)
    pa = _mm(o_a, [wa], name="proj_a", out_dtypes=(BF16,))
    pb = _mm(o_b, [wb], name="proj_b", out_dtypes=(BF16,))
    merged = _gate_fwd(proj, pa, pb, w["b_gate"])
    x1, hx = _mm(merged, [w["w_o"]], name="proj_o", extras=(x,), consts=(w["g_x"],), epilogue=add_norm,
                 out_dtypes=(F32, BF16))
    mn = _rms_fwd_call(mem, w["g_mem"], "rms_mem")
    xq = _mm(hx, [w["w_xq"]], name="xq", out_dtypes=(BF16,))
    xkv = _mm(mn, [w["w_xkv"]], name="xkv", out_dtypes=(BF16,))
    xo = _xattn_fwd(xq, xkv)
    x2, hf = _mm(xo, [w["w_xo"]], name="proj_xo", extras=(x1,), consts=(w["g_ffn"],), epilogue=add_norm,
                 out_dtypes=(F32, BF16))

    def swiglu(accs, ex):
        a, b = accs
        return a, b, a * _sigmoid(a) * b

    ga, gu, hmid = _mm(hf, [w["w_gate"], w["w_up"]], name="ffn_up", epilogue=swiglu, out_dtypes=(BF16, BF16, BF16),
                       tm=512, tn=1408)
    x3 = _mm(hmid, [w["w_down"]], name="ffn_down", extras=(x2,), epilogue=add, tk=2816)

    dx3, dg_final, sq = _loss_head(x3, target, w["g_final"].reshape(1, D_MODEL))
    g = {"g_final": dg_final.reshape(D_MODEL)}

    def swiglu_bwd(accs, ex):
        dh, a, b = accs[0], ex[0].astype(F32), ex[1].astype(F32)
        sg = _sigmoid(a)
        return dh * b * sg * (1.0 + a * (1.0 - sg)), dh * a * sg

    da, db = _mm(dx3, [w["w_down"]], name="ffn_down_dx", tb=True, extras=(ga, gu), epilogue=swiglu_bwd,
                 out_dtypes=(BF16, BF16), tm=512, tn=1408, chunk=MM_CHUNK)
    g["w_down"] = _mm(hmid, [dx3], name="ffn_down_dw", ta=True, tm=1408)
    g["w_gate"] = _mm(hf, [da], name="ffn_gate_dw", ta=True, tn=1408)
    g["w_up"] = _mm(hf, [db], name="ffn_up_dw", ta=True, tn=1408)
    dhf = _mm(da, [w["w_gate"]], name="ffn_gate_dx", tb=True, tk=2816)
    dhf = _mm(db, [w["w_up"]], name="ffn_up_dx", tb=True, extras=(dhf,), epilogue=add, tk=2816,
              out_dtypes=(BF16,))
    dx2, g["g_ffn"] = _rms_bwd_call(x2, w["g_ffn"], dhf, dx3, "rms_ffn_bwd")

    dxo = _mm(dx2, [w["w_xo"]], name="proj_xo_dx", tb=True, out_dtypes=(BF16,))
    g["w_xo"] = _mm(xo, [dx2], name="proj_xo_dw", ta=True)
    dxq, dxkv = _xattn_bwd(xq, xkv, dxo)
    dhx = _mm(dxq, [w["w_xq"]], name="xq_dx", tb=True, out_dtypes=(BF16,))
    g["w_xq"] = _mm(hx, [dxq], name="xq_dw", ta=True)
    dmn = _mm(dxkv, [w["w_xkv"]], name="xkv_dx", tb=True)
    g["w_xkv"] = _mm(mn, [dxkv], name="xkv_dw", ta=True)
    dx1, g["g_x"] = _rms_bwd_call(x1, w["g_x"], dhx, dx2, "rms_x_bwd")
    _, g["g_mem"] = _rms_bwd_call(mem, w["g_mem"], dmn, None, "rms_mem_bwd")

    dmerged = _mm(dx1, [w["w_o"]], name="proj_o_dx", tb=True, out_dtypes=(BF16,))
    g["w_o"] = _mm(merged, [dx1], name="proj_o_dw", ta=True)
    dpa, dpb, dga, dgb, g["b_gate"] = _gate_bwd(proj, pa, pb, w["b_gate"], dmerged)
    do_a = _mm(dpa, [wa], name="proj_a_dx", tb=True, out_dtypes=(BF16,))
    do_b = _mm(dpb, [wb], name="proj_b_dx", tb=True, out_dtypes=(BF16,))
    g["w_a_proj"] = _unpad_heads(_mm(o_a, [dpa], name="proj_a_dw", ta=True), MLA_HEADS, MLA_V, 0)
    g["w_b_proj"] = _mm(o_b, [dpb], name="proj_b_dw", ta=True)

    stacked = [reduce[1](n, g[n]) for n in reduce[0]] if reduce else []
    dsq, dsk, dsv, recv = _sb_bwd(proj, o_b, do_b, t_sb, stacked)
    parts = [reduce[2](n, gs, rv) for n, gs, rv in zip(reduce[0], stacked, recv)] if reduce else []
    dqa, dka, dva, got = _mla_bwd(qa, ka, va, o_a, do_a, lse, t_mla[0], t_mla[2], parts)
    riding = dict(zip(reduce[0], zip(stacked, recv, got))) if reduce else {}
    dqp, dkvp, dkr = _mla_rope_bwd(dqa, dka, dva, tab)
    g["w_uq"] = _unpad_heads(_mm(cqn, [dqp], name="q_up_dw", ta=True), MLA_HEADS, MLA_NOPE + MLA_ROPE, 1)
    g["w_ukv"] = _w_ukv_unlayout(_mm(ckvn, [dkvp], name="kv_up_dw", ta=True))
    dcqn = _mm(dqp, [wuq], name="q_up_dx", tb=True)
    dckvn = _mm(dkvp, [wkv], name="kv_up_dx", tb=True)
    dcq, dckv, g["g_q_lat"], g["g_kv_lat"] = _mla_prep_bwd(proj, w["g_q_lat"], w["g_kv_lat"], dcqn, dckvn)

    dproj = jnp.concatenate([dga, dgb, dsq.astype(BF16), dsk.astype(BF16), dsv.astype(BF16), dcq, dckv, dkr], axis=1)
    g["w_in"] = _w_in_unlayout(_mm(h, [dproj], name="proj_in_dw", ta=True))
    if not reduce:
        dh = _mm(dproj, [win], name="proj_in_dx", tb=True, tk=2048, out_dtypes=(BF16,))
        grad_x, g["g_mix"] = _rms_bwd_call(x, w["g_mix"], dh, dx1, "rms_mix_bwd")
        return sq, grad_x, g, riding
    stacked = [reduce[1](n, g[n]) for n in reduce[3]]
    dh, recv = _mm(dproj, [win], name="proj_in_dx", tb=True, tk=2048, out_dtypes=(BF16,),
                   ride=(_exchange_copies, stacked, _exchange_shapes(stacked), 1))
    parts = [reduce[2](n, gs, rv) for n, gs, rv in zip(reduce[3], stacked, recv)]
    grad_x, g["g_mix"], *got = _rms_bwd_call(x, w["g_mix"], dh, dx1, "rms_mix_bwd",
                                             ride=(_scatter_copies, parts, _scatter_shapes(parts), 3))
    riding.update(zip(reduce[3], zip(stacked, recv, got)))
    return sq, grad_x, g, riding


def _small_pack(d):
    row5 = jnp.concatenate([d["g_q_lat"].reshape(-1), d["g_kv_lat"].reshape(-1), jnp.zeros((640,), F32)])
    rows = [d[n].reshape(-1) for n in ("g_mix", "g_x", "g_mem", "g_ffn", "g_final")] + [row5]
    return rows


def _small_unpack(p, like):
    out = {n: p[i].reshape(like[n].shape) for i, n in enumerate(("g_mix", "g_x", "g_mem", "g_ffn", "g_final"))}
    out["g_q_lat"] = p[5, 0:256].reshape(like["g_q_lat"].shape)
    out["g_kv_lat"] = p[5, 256:384].reshape(like["g_kv_lat"].shape)
    return out


def kernel(x, mem, positions, g_mix, w_in, b_gate, g_q_lat, w_uq, g_kv_lat, w_ukv, w_a_proj, w_b_proj, w_o, g_x, g_mem, w_xq, w_xkv, w_xo, g_ffn, w_gate, w_up, w_down, g_final, loss_target, m_g_mix, m_w_in, m_b_gate, m_g_q_lat, m_w_uq, m_g_kv_lat, m_w_ukv, m_w_a_proj, m_w_b_proj, m_w_o, m_g_x, m_g_mem, m_w_xq, m_w_xkv, m_w_xo, m_g_ffn, m_w_gate, m_w_up, m_w_down, m_g_final, v_g_mix, v_w_in, v_b_gate, v_g_q_lat, v_w_uq, v_g_kv_lat, v_w_ukv, v_w_a_proj, v_w_b_proj, v_w_o, v_g_x, v_g_mem, v_w_xq, v_w_xkv, v_w_xo, v_g_ffn, v_w_gate, v_w_up, v_w_down, v_g_final):
    given = dict(locals())
    names = [n for n, _, _ in MATS] + ["b_gate"] + list(SMALL)
    wts = {n: given[n] for n in names}
    mom = {n: given["m_" + n] for n in names}
    var = {n: given["v_" + n] for n in names}
    shard2d = {n: shp for n, shp, _ in MATS}
    shard2d["b_gate"] = B_GATE_SHARD
    cx, cy, cc = lax.axis_index("x"), lax.axis_index("y"), lax.axis_index("c")
    me = 2 * cx + cy
    place = jnp.stack([me, cc]).astype(jnp.int32)
    bcol = me * B_GATE_SHARD[1]

    own = [wts[n].reshape(shard2d[n]).astype(BF16) for n, _, _ in MATS]
    bias = (("b_gate", (BIAS_ROWS, B_GATE_SHARD[1]), 1),)
    own_bias = [jnp.pad(wts["b_gate"].reshape(B_GATE_SHARD), ((0, BIAS_ROWS - B_GATE_SHARD[0]), (0, 0)))]

    def assemble(mats, gathered, mine=None):
        out = {}
        for k, ((n, shp, ax), g4) in enumerate(zip(mats, gathered)):
            if mine is not None:
                g4 = lax.dynamic_update_slice(g4, mine[k][None], (me, 0, 0))
            out[n] = g4.reshape(N_CHIPS * shp[0], shp[1]) if ax == 0 else jnp.concatenate(list(g4), axis=1)
        return out

    first = own[:N_EARLY] + own_bias
    full = assemble(MATS[:N_EARLY] + bias, _all_gather_weights(first), first)
    full["b_gate"] = full["b_gate"][0:B_GATE_SHARD[0]]
    late = (own[N_EARLY:], lambda gathered: assemble(MATS[N_EARLY:], gathered))
    for n in SMALL:
        full[n] = wts[n].reshape(1, -1) if n != "g_final" else wts[n]

    axis_of = {n: ax for n, _, ax in MATS}
    stack = lambda n, g: jnp.stack([_shard_of(g, d, axis_of[n]) for d in range(N_CHIPS)])
    pair_sum = lambda n, gs, rv: _pair_sum(gs, rv, place, "pair_sum_" + n)
    behind = [n for n, _, _ in MATS[N_EARLY:]]
    last = [n for n, _, _ in MATS[:N_EARLY]]
    sq, grad_x, grads, riding = _local_step(x[0], mem[0], positions[0], loss_target[0], full, t_mla=(1024, 1024, 1024), t_sb=256,
                                            late=late, reduce=(behind, stack, pair_sum, last))

    halves = [_chip_sum(*riding[n], place, "chip_sum_" + n) for n, _, _ in MATS]
    g_shard = dict(zip([n for n, _, _ in MATS], _pair_exchange_halves(halves)))

    small_rows = _small_pack({n: grads[n] for n in SMALL}) + [sq.reshape(-1), grads["b_gate"][0], grads["b_gate"][1]]
    small_rows += [jnp.zeros((D_MODEL,), F32)] * (SMALL_ROWS - len(small_rows))
    small = _all_reduce_small(jnp.stack(small_rows), "all_reduce_small")
    loss = (0.5 / D_MODEL) * jnp.sum(small[6])
    g_shard["b_gate"] = lax.dynamic_slice(small[7:9], (0, bcol), B_GATE_SHARD)

    out = {"grad": {}, "delta": {}, "m": {}, "v": {}}
    for n in [n for n, _, _ in MATS] + ["b_gate"]:
        shape = wts[n].shape
        r2 = lambda a: a.reshape(shard2d[n])
        d_n, m_n, v_n = _adamw(r2(wts[n]), g_shard[n], r2(mom[n]), r2(var[n]), "adamw_" + n)
        for key, a in (("grad", g_shard[n]), ("delta", d_n), ("m", m_n), ("v", v_n)):
            out[key][n] = a.reshape(shape)
    sp = lambda d: jnp.stack(_small_pack(d) + [jnp.zeros((D_MODEL,), F32)] * 2)
    delta_s, m_s, v_s = _adamw(sp(wts), small[0:8].at[6:8].set(0.0), sp(mom), sp(var), "adamw_small")
    for key, p in (("grad", small), ("delta", delta_s), ("m", m_s), ("v", v_s)):
        out[key].update(_small_unpack(p, wts))

    order = ["g_mix", "w_in", "b_gate", "g_q_lat", "w_uq", "g_kv_lat", "w_ukv", "w_a_proj", "w_b_proj", "w_o", "g_x",
             "g_mem", "w_xq", "w_xkv", "w_xo", "g_ffn", "w_gate", "w_up", "w_down", "g_final"]
    return (loss, grad_x[None], *[out[key][n] for key in ("grad", "delta", "m", "v") for n in order])
```

```python
import functools
import math

import jax
import jax.numpy as jnp
from jax import lax
from jax.experimental import pallas as pl
from jax.experimental.pallas import tpu as pltpu

F32 = jnp.float32
BF16 = jnp.bfloat16
MESH = pl.DeviceIdType.MESH

D_MODEL = 1024
MLA_HEADS = 8
MLA_Q_RANK = 256
MLA_KV_RANK = 128
MLA_NOPE = 64
MLA_ROPE = 32
MLA_V = 64
ROPE_THETA = 10000.0
SB_HEADS = 8
SB_HEAD_DIM = 64
X_HEADS = 4
X_HEAD_DIM = 128
EPS = 1e-6
ADAM_LR = 0.001
ADAM_B1 = 0.9
ADAM_B2 = 0.999
ADAM_EPS = 1e-08
ADAM_WD = 0.01
ADAM_STEP = 10

LANE = 128
LOG2E = 1.4426950408889634
MLA_SCALE = 1.0 / math.sqrt(MLA_NOPE + MLA_ROPE)
SB_SCALE = 1.0 / math.sqrt(SB_HEAD_DIM)
assert math.log2(SB_SCALE) == round(math.log2(SB_SCALE))
MM_CHUNK = 256
N_CHIPS = 4
VMEM_BYTES = 64 * 1024 * 1024

C_GA, C_GB, C_SBQ, C_SBK, C_SBV, C_CQ, C_CKV, C_KR = 0, 1024, 2048, 2560, 3072, 3584, 3840, 3968
SB_WIDTH = SB_HEADS * SB_HEAD_DIM
ROPE_LO = MLA_NOPE
HALF = MLA_ROPE // 2

SB_ZERO_LOG = -104.0

MATS = (
    ("w_in", (1024, 1000), 1), ("w_uq", (256, 192), 1), ("w_ukv", (128, 256), 1), ("w_a_proj", (512, 256), 1),
    ("w_b_proj", (512, 256), 1), ("w_o", (256, 1024), 0), ("w_xq", (256, 512), 0), ("w_xkv", (256, 1024), 0),
    ("w_xo", (512, 256), 1), ("w_gate", (1024, 704), 1), ("w_up", (1024, 704), 1), ("w_down", (704, 1024), 0),
)
N_EARLY = 3
B_GATE_SHARD = (2, 256)
BIAS_ROWS = 16
SMALL = ("g_mix", "g_x", "g_mem", "g_ffn", "g_final", "g_q_lat", "g_kv_lat")
SMALL_ROWS = 16


def _vmem_limit(block_bytes, temp_bytes):
    est = 2 * block_bytes + temp_bytes + (4 << 20)
    return int(min(max(est, 16 << 20), VMEM_BYTES - (6 << 20)))


def _nbytes(shape, dtype):
    return math.prod(shape) * jnp.dtype(dtype).itemsize


def _row_tile(rows, cap):
    if rows <= cap:
        return rows
    return max(t for t in range(8, cap + 1, 8) if rows % t == 0)


def _tile(n, cap):
    if n <= cap:
        return n
    best = None
    for t in range(LANE, cap + 1, LANE):
        if n % t == 0:
            best = t
    assert best is not None, (n, cap)
    return best


def _mm(a, bs, *, name, ta=False, tb=False, extras=(), row_extras=(), consts=(), epilogue=None, out_dtypes=(F32,),
        tm=1024, tn=1024, tk=1024, chunk=None, ride=None):
    bs = tuple(bs)
    m, k = (a.shape[1], a.shape[0]) if ta else a.shape
    n = bs[0].shape[0] if tb else bs[0].shape[1]
    tm, tn, tk = _tile(m, tm), _tile(n, tn), _tile(k, tk)
    assert m % tm == 0 and n % tn == 0 and k % tk == 0
    nk = k // tk
    nb, ne, no = len(bs), len(extras) + len(row_extras) + len(consts), len(out_dtypes)
    dims = (((0,) if ta else (1,)), ((1,) if tb else (0,))), ((), ())
    if epilogue is None:
        epilogue = lambda accs, ex: (accs[0],)

    rn = len(ride[1]) if ride else 0
    n_acc = nb if nk > 1 else 0
    gi, gj = m // tm, n // tn

    def body(*refs):
        a_ref, b_refs, e_refs = refs[0], refs[1:1 + nb], refs[1 + nb:1 + nb + ne]
        base = 1 + nb + ne + rn
        o_refs, acc_refs = refs[base:base + no], refs[base + no + rn:base + no + rn + n_acc]
        step = [pl.program_id(d) for d in range(3)]
        finish = _ride_along(ride[0] if ride else None, rn,
                             (refs[base - rn:base], refs[base + no:base + no + rn], *refs[base + no + rn + n_acc:]),
                             (step[0] == 0) & (step[1] == 0) & (step[2] == 0),
                             (step[0] == gi - 1) & (step[1] == gj - 1) & (step[2] == nk - 1))
        if nk == 1:
            ch = chunk or tm
            bvs = [b_ref[...].astype(BF16) for b_ref in b_refs]
            for r0 in range(0, tm, ch):
                rows = slice(r0, r0 + ch)
                av = (a_ref[:, rows] if ta else a_ref[rows, :]).astype(BF16)
                accs = [lax.dot_general(av, bv, dims, preferred_element_type=F32) for bv in bvs]
                ex = [e[rows, :] for e in e_refs[:ne - len(consts)]] + [e[...] for e in e_refs[ne - len(consts):]]
                for o_ref, v in zip(o_refs, epilogue(accs, ex)):
                    o_ref[rows, :] = v.astype(o_ref.dtype)
            finish()
            return
        kk = step[2]

        @pl.when(kk == 0)
        def _():
            for acc in acc_refs:
                acc[...] = jnp.zeros_like(acc)

        av = a_ref[...].astype(BF16)
        for b_ref, acc in zip(b_refs, acc_refs):
            acc[...] += lax.dot_general(av, b_ref[...].astype(BF16), dims, preferred_element_type=F32)

        @pl.when(kk == nk - 1)
        def _():
            outs = epilogue([acc[...] for acc in acc_refs], [e[...] for e in e_refs])
            for o_ref, v in zip(o_refs, outs):
                o_ref[...] = v.astype(o_ref.dtype)

        finish()

    a_spec = pl.BlockSpec((tk, tm), lambda i, j, kk: (kk, i)) if ta else pl.BlockSpec((tm, tk), lambda i, j, kk: (i, kk))
    b_spec = pl.BlockSpec((tn, tk), lambda i, j, kk: (j, kk)) if tb else pl.BlockSpec((tk, tn), lambda i, j, kk: (kk, j))
    mn_spec = pl.BlockSpec((tm, tn), lambda i, j, kk: (i, j))
    blocks = (_nbytes((tm, tk), a.dtype) + sum(_nbytes((tk, tn), b.dtype) for b in bs)
              + sum(_nbytes((tm, tn), e.dtype) for e in extras) + sum(_nbytes((tm, tn), d) for d in out_dtypes)
              + sum(_nbytes((tm, e.shape[1]), e.dtype) for e in row_extras))
    temps = (nb + 4) * _nbytes((tm, tn), F32)
    outs = pl.pallas_call(
        body, name=name, grid=(m // tm, n // tn, nk),
        in_specs=[a_spec] + [b_spec] * nb + [mn_spec] * len(extras)
        + [pl.BlockSpec((tm, e.shape[1]), lambda i, j, kk: (i, 0)) for e in row_extras]
        + [pl.BlockSpec(e.shape, lambda i, j, kk: (0, 0)) for e in consts] + [ANY] * rn,
        out_specs=[mn_spec] * no + [ANY] * rn,
        out_shape=[jax.ShapeDtypeStruct((m, n), d) for d in out_dtypes] + (list(ride[2]) if ride else []),
        scratch_shapes=[pltpu.VMEM((tm, tn), F32) for _ in range(n_acc)] + (_dma_sems(ride[3] * rn) if ride else []),
        compiler_params=pltpu.CompilerParams(
            dimension_semantics=("arbitrary",) * 3 if ride else ("parallel", "parallel", "arbitrary"),
            vmem_limit_bytes=_vmem_limit(blocks, temps)),
    )(a, *bs, *extras, *row_extras, *consts, *(ride[1] if ride else ()))
    if ride:
        return (outs[0] if no == 1 else outs[:no]), list(outs[no:])
    return outs[0] if no == 1 else outs


def _rowwise(body, *, name, rows, tr, row_ins, full_ins=(), row_outs=(), acc_outs=(), ride=None):
    tr = min(tr, rows)
    assert rows % tr == 0
    n_ri, n_fi, n_ro, n_ao = len(row_ins), len(full_ins), len(row_outs), len(acc_outs)
    rn = len(ride[1]) if ride else 0

    def kern(*refs):
        i = pl.program_id(0)
        n_in, n_out = n_ri + n_fi, n_ro + n_ao
        outs = refs[n_in + rn:n_in + rn + n_out]
        finish = _ride_along(ride[0] if ride else None, rn,
                             (refs[n_in:n_in + rn], refs[n_in + rn + n_out:n_in + 2 * rn + n_out],
                              *refs[n_in + 2 * rn + n_out:]), i == 0, i == rows // tr - 1)
        body(i, refs[:n_ri], refs[n_ri:n_in], outs[:n_ro], outs[n_ro:])
        finish()

    in_specs = [pl.BlockSpec((tr, w), functools.partial(lambda i, c: (i, c), c=ci)) for _, w, ci in row_ins]
    in_specs += [pl.BlockSpec(f.shape, lambda i: (0, 0)) for f in full_ins]
    in_specs += [ANY] * rn
    out_specs = [pl.BlockSpec((tr, w), lambda i: (i, 0)) for w, _ in row_outs]
    out_specs += [pl.BlockSpec(s, lambda i: (0, 0)) for s, _ in acc_outs] + [ANY] * rn
    out_shape = [jax.ShapeDtypeStruct((rows, w), d) for w, d in row_outs]
    out_shape += [jax.ShapeDtypeStruct(s, d) for s, d in acc_outs] + (list(ride[2]) if ride else [])
    blocks = (sum(_nbytes((tr, w), a.dtype) for a, w, _ in row_ins) + sum(_nbytes(f.shape, f.dtype) for f in full_ins)
              + sum(_nbytes((tr, w), d) for w, d in row_outs) + sum(_nbytes(s, d) for s, d in acc_outs))
    widest = max([w for _, w, _ in row_ins] + [w for w, _ in row_outs])
    outs = pl.pallas_call(
        kern, name=name, grid=(rows // tr,), in_specs=in_specs, out_specs=out_specs, out_shape=out_shape,
        scratch_shapes=_dma_sems(ride[3] * rn) if ride else [],
        compiler_params=pltpu.CompilerParams(
            dimension_semantics=("arbitrary",) if acc_outs or ride else ("parallel",),
            vmem_limit_bytes=_vmem_limit(blocks, 8 * _nbytes((tr, widest), F32))),
    )(*[a for a, _, _ in row_ins], *full_ins, *(ride[1] if ride else ()))
    return outs


def _rms(x, g):
    r = lax.rsqrt(jnp.mean(x * x, axis=-1, keepdims=True) + EPS)
    return x * r * g


def _rms_bwd(x, g, dy):
    r = lax.rsqrt(jnp.mean(x * x, axis=-1, keepdims=True) + EPS)
    xh = x * r
    dxh = dy * g
    dx = r * (dxh - xh * jnp.mean(dxh * xh, axis=-1, keepdims=True))
    return dx, jnp.sum(dy * xh, axis=0, keepdims=True)


def _sigmoid(x):
    return 1.0 / (1.0 + jnp.exp(-x))


def _acc_init(i, refs):
    @pl.when(i == 0)
    def _():
        for r in refs:
            r[...] = jnp.zeros_like(r)


def _rms_fwd_call(x, g, name):
    rows, c = x.shape

    def body(i, ins, fulls, outs, accs):
        outs[0][...] = _rms(ins[0][...], fulls[0][...]).astype(BF16)

    return _rowwise(body, name=name, rows=rows, tr=512, row_ins=[(x, c, 0)], full_ins=[g], row_outs=[(c, BF16)])[0]


def _rms_bwd_call(x, g, dy, res, name, ride=None):
    rows, c = x.shape
    row_ins = [(x, c, 0), (dy, c, 0)] + ([(res, c, 0)] if res is not None else [])

    def body(i, ins, fulls, outs, accs):
        _acc_init(i, accs)
        dx, dg = _rms_bwd(ins[0][...], fulls[0][...], ins[1][...].astype(F32))
        if res is not None:
            dx = dx + ins[2][...]
        outs[0][...] = dx
        accs[0][...] += dg

    return _rowwise(body, name=name, rows=rows, tr=512, row_ins=row_ins, full_ins=[g], row_outs=[(c, F32)],
                    acc_outs=[((1, c), F32)], ride=ride)


def _rope_tables(pos_col, freq_lane):
    rows = pos_col.shape[0]

    def body(i, ins, fulls, outs, accs):
        ang = ins[0][...].astype(F32) * fulls[0][...]
        lane = lax.broadcasted_iota(jnp.int32, ang.shape, 1)
        cos, sin = jnp.cos(ang), jnp.sin(ang)
        first = (lane >= ROPE_LO) & (lane < ROPE_LO + HALF)
        second = (lane >= ROPE_LO + HALF) & (lane < ROPE_LO + MLA_ROPE)
        outs[0][:, 0:LANE] = jnp.where(first | second, cos, 1.0)
        outs[0][:, LANE:2 * LANE] = jnp.where(first, -sin, 0.0)
        outs[0][:, 2 * LANE:3 * LANE] = jnp.where(second, sin, 0.0)

    return _rowwise(body, name="rope_tables", rows=rows, tr=1024, row_ins=[(pos_col, 1, 0)], full_ins=[freq_lane],
                    row_outs=[(3 * LANE, F32)])[0]


def _rope(x, tab):
    return (x * tab[:, 0:LANE] + pltpu.roll(x, LANE - HALF, 1) * tab[:, LANE:2 * LANE]
            + pltpu.roll(x, HALF, 1) * tab[:, 2 * LANE:3 * LANE])


def _rope_t(dy, tab):
    return (dy * tab[:, 0:LANE] + pltpu.roll(dy * tab[:, LANE:2 * LANE], HALF, 1)
            + pltpu.roll(dy * tab[:, 2 * LANE:3 * LANE], LANE - HALF, 1))


def _mla_prep_fwd(proj, tab, g_q, g_kv):
    rows = proj.shape[0]

    def body(i, ins, fulls, outs, accs):
        outs[0][...] = _rms(ins[0][...].astype(F32), fulls[0][...]).astype(BF16)
        outs[1][...] = _rms(ins[1][...].astype(F32), fulls[1][...]).astype(BF16)
        outs[2][...] = _rope(ins[2][...].astype(F32), ins[3][...])

    return _rowwise(body, name="mla_prep_fwd", rows=rows, tr=512,
                    row_ins=[(proj, MLA_Q_RANK, C_CQ // MLA_Q_RANK), (proj, LANE, C_CKV // LANE),
                             (proj, LANE, C_KR // LANE), (tab, 3 * LANE, 0)],
                    full_ins=[g_q, g_kv], row_outs=[(MLA_Q_RANK, BF16), (MLA_KV_RANK, BF16), (LANE, F32)])


def _mla_prep_bwd(proj, g_q, g_kv, dcqn, dckvn):
    rows = proj.shape[0]

    def body(i, ins, fulls, outs, accs):
        _acc_init(i, accs)
        dcq, dgq = _rms_bwd(ins[0][...].astype(F32), fulls[0][...], ins[2][...])
        dckv, dgkv = _rms_bwd(ins[1][...].astype(F32), fulls[1][...], ins[3][...])
        outs[0][...] = dcq.astype(BF16)
        outs[1][...] = dckv.astype(BF16)
        accs[0][...] += dgq
        accs[1][...] += dgkv

    return _rowwise(body, name="mla_prep_bwd", rows=rows, tr=512,
                    row_ins=[(proj, MLA_Q_RANK, C_CQ // MLA_Q_RANK), (proj, LANE, C_CKV // LANE),
                             (dcqn, MLA_Q_RANK, 0), (dckvn, MLA_KV_RANK, 0)],
                    full_ins=[g_q, g_kv], row_outs=[(MLA_Q_RANK, BF16), (MLA_KV_RANK, BF16)],
                    acc_outs=[((1, MLA_Q_RANK), F32), ((1, MLA_KV_RANK), F32)])


def _per_head(fn, x):
    return jnp.concatenate([fn(x[:, h * LANE:(h + 1) * LANE]) for h in range(x.shape[1] // LANE)], axis=1)


def _mla_rope_bwd(dq, dk, dv, tab):
    rows = dq.shape[0]
    hw = MLA_HEADS * LANE

    def body(i, ins, fulls, outs, accs):
        t = ins[3][...]
        dkr = jnp.zeros((ins[0].shape[0], LANE), F32)
        for h in range(MLA_HEADS):
            sl = slice(h * LANE, (h + 1) * LANE)
            outs[0][:, sl] = _rope_t(ins[0][:, sl], t).astype(BF16)
            dkr = dkr + ins[1][:, sl]
        outs[1][:, 0:hw] = ins[1][...].astype(BF16)
        outs[1][:, hw:2 * hw] = ins[2][...].astype(BF16)
        lane = lax.broadcasted_iota(jnp.int32, dkr.shape, 1)
        dkr = jnp.where((lane >= ROPE_LO) & (lane < ROPE_LO + MLA_ROPE), dkr, 0.0)
        outs[2][...] = _rope_t(dkr, t).astype(BF16)

    return _rowwise(body, name="mla_rope_bwd", rows=rows, tr=512,
                    row_ins=[(dq, hw, 0), (dk, hw, 0), (dv, hw, 0), (tab, 3 * LANE, 0)],
                    row_outs=[(hw, BF16), (2 * hw, BF16), (LANE, BF16)])


def _dot_nt(a, b):
    return lax.dot_general(a, b, (((1,), (1,)), ((), ())), preferred_element_type=F32)


def _dot_tn(a, b):
    return lax.dot_general(a, b, (((0,), (0,)), ((), ())), preferred_element_type=F32)


def _dot(a, b):
    return jnp.dot(a, b, preferred_element_type=F32)


def _attn_params(s, t, n_res_f32, n_res_bf16, ride=False):
    blocks = n_res_f32 * _nbytes((s, LANE), F32) + n_res_bf16 * _nbytes((s, LANE), BF16) + 6 * _nbytes((t, LANE), F32)
    return pltpu.CompilerParams(dimension_semantics=("arbitrary" if ride else "parallel", "arbitrary"),
                                vmem_limit_bytes=_vmem_limit(blocks, 12 * _nbytes((t, t), F32)))


def _mla_fwd(q, k, v, t, tk, shards=()):
    s, hw = q.shape
    heads, nq, r = hw // LANE, s // t, t // tk
    ng = len(shards)

    def body(q_ref, k_ref, v_ref, *rest):
        o_ref, l_ref = rest[ng], rest[ng + 1]
        h, i = pl.program_id(0), pl.program_id(1)
        if ng:
            gather = _GatherPhases(rest[:ng], rest[ng + 2:2 * ng + 2], *rest[2 * ng + 2:])
            pl.when((h == 0) & (i == 0))(gather.send)
            pl.when((h == heads // 2) & (i == 0))(gather.forward)
        qv = q_ref[...]

        def step(j, carry, off):
            m, l, acc = carry
            sl = pl.ds(pl.multiple_of(j * tk, tk), tk)
            sc = _dot_nt(qv, k_ref[sl, :])
            if off is not None:
                row = lax.broadcasted_iota(jnp.int32, (t, tk), 0)
                col = lax.broadcasted_iota(jnp.int32, (t, tk), 1)
                sc = jnp.where(col + off <= row, sc, -1e30)
            m_new = jnp.maximum(m, jnp.max(sc, axis=1, keepdims=True))
            p = jnp.exp2(sc - m_new)
            alpha = jnp.exp2(m - m_new)
            l = alpha * l + jnp.sum(p, axis=1, keepdims=True)
            acc = alpha * acc + _dot(p.astype(BF16), v_ref[sl, :])
            return m_new, l, acc

        init = (jnp.full((t, 1), -1e30, F32), jnp.zeros((t, 1), F32), jnp.zeros((t, LANE), F32))
        below = i * r
        carry = lax.fori_loop(0, below // 2, lambda j, c: step(2 * j + 1, step(2 * j, c, None), None), init)
        carry = lax.fori_loop(below // 2 * 2, below, lambda j, c: step(j, c, None), carry)
        for jj in range(r):
            carry = step(i * r + jj, carry, jj * tk)
        m, l, acc = carry
        o_ref[...] = (acc / l).astype(o_ref.dtype)
        l_ref[0] = m + jnp.log2(l)
        if ng:
            pl.when((h == heads - 1) & (i == nq - 1))(gather.finish)

    blk = pl.BlockSpec((t, LANE), lambda h, i: (i, h))
    res = pl.BlockSpec((s, LANE), lambda h, i: (0, h))
    outs = pl.pallas_call(
        body, name="mla_fwd", grid=(heads, nq), in_specs=[blk, res, res] + [ANY] * ng,
        out_specs=[blk, pl.BlockSpec((1, t, 1), lambda h, i: (h, i, 0))] + [ANY] * ng,
        out_shape=[jax.ShapeDtypeStruct((s, hw), BF16), jax.ShapeDtypeStruct((heads, s, 1), F32)]
        + [jax.ShapeDtypeStruct((N_CHIPS,) + sh.shape, sh.dtype) for sh in shards],
        scratch_shapes=_dma_sems(6 * ng) + [pltpu.SemaphoreType.DMA((ng,))] if ng else [],
        compiler_params=_attn_params(s, t, 0, 2, ride=ng > 0),
    )(q, k, v, *shards)
    return outs[0], outs[1], list(outs[2:])


def _mla_bwd(q, k, v, o, do, lse, t, tk, parts=()):
    s, hw = q.shape
    heads, nq, r = hw // LANE, s // t, t // tk
    ns = len(parts)

    def body(q_ref, k_ref, v_ref, o_ref, do_ref, l_ref, *rest):
        dq_ref, dk_ref, dv_ref = rest[ns:ns + 3]
        h, i = pl.program_id(0), pl.program_id(1)
        finish = _ride_along(_scatter_copies, ns, (rest[:ns], rest[ns + 3:2 * ns + 3], *rest[2 * ns + 3:]),
                             (h == 0) & (i == 0), (h == heads - 1) & (i == nq - 1))

        @pl.when(i == 0)
        def _():
            dk_ref[...] = jnp.zeros_like(dk_ref)
            dv_ref[...] = jnp.zeros_like(dv_ref)

        qv, dov, lv = q_ref[...], do_ref[...], l_ref[0]
        dlt = jnp.sum(dov.astype(F32) * o_ref[...].astype(F32), axis=1, keepdims=True)

        def step(j, dq, off):
            sl = pl.ds(pl.multiple_of(j * tk, tk), tk)
            kv, vv = k_ref[sl, :], v_ref[sl, :]
            p = jnp.exp2(_dot_nt(qv, kv) - lv)
            if off is not None:
                row = lax.broadcasted_iota(jnp.int32, (t, tk), 0)
                col = lax.broadcasted_iota(jnp.int32, (t, tk), 1)
                p = jnp.where(col + off <= row, p, 0.0)
            ds = (p * (_dot_nt(dov, vv) - dlt)).astype(BF16)
            dk_ref[sl, :] += _dot_tn(ds, qv) * (1.0 / LOG2E)
            dv_ref[sl, :] += _dot_tn(p.astype(BF16), dov)
            return dq + _dot(ds, kv)

        dq = lax.fori_loop(0, i * r, lambda j, c: step(j, c, None), jnp.zeros((t, LANE), F32))
        for jj in range(r):
            dq = step(i * r + jj, dq, jj * tk)
        dq_ref[...] = dq * MLA_SCALE
        finish()

    blk = pl.BlockSpec((t, LANE), lambda h, i: (i, h))
    res = pl.BlockSpec((s, LANE), lambda h, i: (0, h))
    full = jax.ShapeDtypeStruct((s, hw), F32)
    outs = pl.pallas_call(
        body, name="mla_bwd", grid=(heads, nq),
        in_specs=[blk, res, res, blk, blk, pl.BlockSpec((1, t, 1), lambda h, i: (h, i, 0))] + [ANY] * ns,
        out_specs=[blk, res, res] + [ANY] * ns, out_shape=[full, full, full] + _scatter_shapes(parts),
        scratch_shapes=_dma_sems(3 * ns) if ns else [],
        compiler_params=_attn_params(s, t, 2, 2, ride=ns > 0),
    )(q, k, v, o, do, lse, *parts)
    return outs[0], outs[1], outs[2], list(outs[3:])


def _sb_logits(qv, kv, keep, upper):
    z = _dot_nt(qv, kv)
    e = jnp.exp(-jnp.abs(z))
    l1p = jnp.log(1.0 + e)
    lb = jnp.minimum(z, 0.0) - l1p
    lo = -jnp.maximum(z, 0.0) - l1p
    if keep is not None:
        lo = jnp.where(keep, lo, 0.0)
    hi = lo.astype(BF16)
    rem = (lo - hi.astype(F32)).astype(BF16)
    suf = _dot(hi, upper) + _dot(rem, upper)
    return z, e, lb, lo, suf


def _tri(t, inclusive):
    row = lax.broadcasted_iota(jnp.int32, (t, t), 0)
    col = lax.broadcasted_iota(jnp.int32, (t, t), 1)
    return jnp.where((row >= col) if inclusive else (row > col), 1.0, 0.0).astype(BF16)


SB_QBLOCKS = 2
SB_PAIR = LANE // SB_HEAD_DIM
SB_CHAINS = SB_QBLOCKS * SB_PAIR
SB_FIRST = 2


def _sb_first_tile(b, t):
    start = jnp.maximum(b - (SB_FIRST - 1), 0) * t
    row = lax.broadcasted_iota(jnp.int32, (t, SB_FIRST * t), 0)
    col = lax.broadcasted_iota(jnp.int32, (t, SB_FIRST * t), 1)
    return pl.ds(pl.multiple_of(start, t), SB_FIRST * t), col + start < row + b * t


def _sb_head(x, hh):
    lane = lax.broadcasted_iota(jnp.int32, x.shape, 1)
    return jnp.where(lane // SB_HEAD_DIM == hh, x, jnp.zeros_like(x))


def _sb_walk(i, first, carries_of):
    n = SB_CHAINS
    carries = [first(c) for c in range(n)]
    width = len(carries[0])

    def alive(carry):
        return jnp.max(carry[0]) >= SB_ZERO_LOG

    def split(st):
        return [tuple(st[1 + c * width:1 + (c + 1) * width]) for c in range(n)]

    def live(st):
        any_alive = alive(split(st)[0])
        for cr in split(st)[1:]:
            any_alive = any_alive | alive(cr)
        return (st[0] <= SB_QBLOCKS * i) & any_alive

    def more(st):
        out = (st[0] + 1,)
        for c, cr in enumerate(split(st)):
            out += tuple(carries_of(c, st[0], cr))
        return out

    st = lax.while_loop(live, more, (jnp.int32(SB_FIRST),) + tuple(x for cr in carries for x in cr))
    jj, carries = st[0], split(st)
    for c in range(SB_PAIR, n):
        def live_c(s2, c=c):
            return (s2[0] <= SB_QBLOCKS * i + c // SB_PAIR) & alive(s2[1:])

        def more_c(s2, c=c):
            return (s2[0] + 1,) + tuple(carries_of(c, s2[0], s2[1:]))

        carries[c] = lax.while_loop(live_c, more_c, (jj,) + tuple(carries[c]))[1:]
    return carries


def _sb_fwd(proj, t):
    s = proj.shape[0]
    pairs, nq, nb = SB_HEADS // SB_PAIR, s // t, SB_QBLOCKS

    def body(q_ref, k_ref, v_ref, o_ref):
        i = pl.program_id(1)
        upper, upper_first = _tri(t, False), _tri(SB_FIRST * t, False)
        rows = [slice((c // SB_PAIR) * t, (c // SB_PAIR + 1) * t) for c in range(SB_CHAINS)]
        qs = [_sb_head(q_ref[rows[c], :] * SB_SCALE, c % SB_PAIR).astype(BF16) for c in range(SB_CHAINS)]

        def first(c):
            sl, keep = _sb_first_tile(nb * i + c // SB_PAIR, t)
            _, _, lb, lo, suf = _sb_logits(qs[c], k_ref[sl, :].astype(BF16), keep, upper_first)
            a = jnp.where(keep, jnp.exp(lb + suf), 0.0)
            return (jnp.sum(lo, axis=1, keepdims=True),
                    _dot(a.astype(BF16), _sb_head(v_ref[sl, :], c % SB_PAIR).astype(BF16)))

        def step(c, jj, carry):
            run, acc = carry
            sl = pl.ds(pl.multiple_of((nb * i + c // SB_PAIR - jj) * t, t), t)
            _, _, lb, lo, suf = _sb_logits(qs[c], k_ref[sl, :].astype(BF16), None, upper)
            a = jnp.exp(lb + suf + run)
            acc = acc + _dot(a.astype(BF16), _sb_head(v_ref[sl, :], c % SB_PAIR).astype(BF16))
            return run + jnp.sum(lo, axis=1, keepdims=True), acc

        carries = _sb_walk(i, first, step)
        for qb in range(nb):
            o_ref[qb * t:(qb + 1) * t, :] = sum(carries[qb * SB_PAIR + hh][1] for hh in range(SB_PAIR))

    return pl.pallas_call(
        body, name="sb_fwd", grid=(pairs, nq // nb),
        in_specs=[pl.BlockSpec((nb * t, LANE), lambda h, i: (i, C_SBQ // LANE + h)),
                  pl.BlockSpec((s, LANE), lambda h, i: (0, C_SBK // LANE + h)),
                  pl.BlockSpec((s, LANE), lambda h, i: (0, C_SBV // LANE + h))],
        out_specs=pl.BlockSpec((nb * t, LANE), lambda h, i: (i, h)),
        out_shape=jax.ShapeDtypeStruct((s, pairs * LANE), F32),
        compiler_params=_attn_params(s, nb * t, 0, 2),
    )(proj, proj, proj)


def _sb_bwd(proj, o, do, t, stacked=()):
    s = proj.shape[0]
    pairs, nq, nb = SB_HEADS // SB_PAIR, s // t, SB_QBLOCKS
    nx = len(stacked)

    def body(q_ref, k_ref, v_ref, o_ref, do_ref, *rest):
        dq_ref, dk_ref, dv_ref = rest[nx:nx + 3]
        hd, i = pl.program_id(0), pl.program_id(1)
        finish = _ride_along(_exchange_copies, nx, (rest[:nx], rest[nx + 3:2 * nx + 3], *rest[2 * nx + 3:]),
                             (hd == 0) & (i == 0), (hd == pairs - 1) & (i == nq // nb - 1))

        @pl.when(i == 0)
        def _():
            dk_ref[...] = jnp.zeros_like(dk_ref)
            dv_ref[...] = jnp.zeros_like(dv_ref)

        rows = [slice((c // SB_PAIR) * t, (c // SB_PAIR + 1) * t) for c in range(SB_CHAINS)]
        qs = [_sb_head(q_ref[rows[c], :] * SB_SCALE, c % SB_PAIR).astype(BF16) for c in range(SB_CHAINS)]
        dos = [_sb_head(do_ref[rows[c], :], c % SB_PAIR) for c in range(SB_CHAINS)]
        totals = [jnp.sum(dos[c].astype(F32) * o_ref[rows[c], :], axis=1, keepdims=True) for c in range(SB_CHAINS)]
        tris = {1: (_tri(t, False), _tri(t, True)), SB_FIRST: (_tri(SB_FIRST * t, False), _tri(SB_FIRST * t, True))}

        def tile(c, sl, keep, blocks, carry):
            run, g, dq = carry
            qv, dov = qs[c], dos[c]
            upper, upper_incl = tris[blocks]
            kv, vv = k_ref[sl, :].astype(BF16), v_ref[sl, :].astype(BF16)
            z, e, lb, lo, suf = _sb_logits(qv, kv, keep, upper)
            tail = suf + run
            a = jnp.exp(lb + tail)
            if keep is not None:
                a = jnp.where(keep, a, 0.0)
            ab = a.astype(BF16)
            gr = ab.astype(F32) * _dot_nt(dov, vv)
            ghi = gr.astype(BF16)
            grem = (gr - ghi.astype(F32)).astype(BF16)
            before = totals[c] - g - (_dot(ghi, upper_incl) + _dot(grem, upper_incl))
            before = jnp.where(tail < SB_ZERO_LOG, 0.0, before)
            r = 1.0 / (1.0 + e)
            pos = z >= 0.0
            dz = r * (gr * jnp.where(pos, e, 1.0) - before * jnp.where(pos, 1.0, e))
            if keep is not None:
                dz = jnp.where(keep, dz, 0.0)
            dzb = dz.astype(BF16)
            dk_ref[sl, :] += _dot_tn(dzb, qv)
            dv_ref[sl, :] += _dot_tn(ab, dov)
            return (run + jnp.sum(lo, axis=1, keepdims=True), g + jnp.sum(gr, axis=1, keepdims=True),
                    dq + _dot(dzb, _sb_head(kv, c % SB_PAIR)))

        zero = jnp.zeros((t, 1), F32)
        init = (zero, zero, jnp.zeros((t, LANE), F32))

        def first(c):
            sl, keep = _sb_first_tile(nb * i + c // SB_PAIR, t)
            return tile(c, sl, keep, SB_FIRST, init)

        def step(c, jj, carry):
            return tile(c, pl.ds(pl.multiple_of((nb * i + c // SB_PAIR - jj) * t, t), t), None, 1, carry)

        carries = _sb_walk(i, first, step)
        for qb in range(nb):
            dq_ref[qb * t:(qb + 1) * t, :] = sum(carries[qb * SB_PAIR + hh][2] for hh in range(SB_PAIR)) * SB_SCALE
        finish()

    blk = pl.BlockSpec((nb * t, LANE), lambda h, i: (i, h))
    res = pl.BlockSpec((s, LANE), lambda h, i: (0, h))
    full = jax.ShapeDtypeStruct((s, pairs * LANE), F32)
    outs = pl.pallas_call(
        body, name="sb_bwd", grid=(pairs, nq // nb),
        in_specs=[pl.BlockSpec((nb * t, LANE), lambda h, i: (i, C_SBQ // LANE + h)),
                  pl.BlockSpec((s, LANE), lambda h, i: (0, C_SBK // LANE + h)),
                  pl.BlockSpec((s, LANE), lambda h, i: (0, C_SBV // LANE + h)), blk, blk] + [ANY] * nx,
        out_specs=[blk, res, res] + [ANY] * nx, out_shape=[full, full, full] + _exchange_shapes(stacked),
        scratch_shapes=_dma_sems(nx) if nx else [],
        compiler_params=_attn_params(s, nb * t, 2, 2, ride=nx > 0),
    )(proj, proj, proj, o, do, *stacked)
    return outs[0], outs[1], outs[2], list(outs[3:])


def _xattn_probs(qh, kh):
    sc = _dot_nt(qh, kh) * (1.0 / math.sqrt(X_HEAD_DIM))
    p = jnp.exp(sc - jnp.max(sc, axis=1, keepdims=True))
    return p / jnp.sum(p, axis=1, keepdims=True)


def _xattn_fwd(xq, xkv):
    rows = xq.shape[0]
    w = X_HEADS * X_HEAD_DIM

    def body(i, ins, fulls, outs, accs):
        for h in range(X_HEADS):
            sl = slice(h * LANE, (h + 1) * LANE)
            p = _xattn_probs(ins[0][:, sl], fulls[0][:, sl])
            outs[0][:, sl] = _dot(p.astype(BF16), fulls[0][:, w + h * LANE:w + (h + 1) * LANE]).astype(BF16)

    return _rowwise(body, name="xattn_fwd", rows=rows, tr=512, row_ins=[(xq, w, 0)], full_ins=[xkv],
                    row_outs=[(w, BF16)])[0]


def _xattn_bwd(xq, xkv, dxo):
    rows = xq.shape[0]
    w = X_HEADS * X_HEAD_DIM

    def body(i, ins, fulls, outs, accs):
        _acc_init(i, accs)
        for h in range(X_HEADS):
            sl = slice(h * LANE, (h + 1) * LANE)
            slv = slice(w + h * LANE, w + (h + 1) * LANE)
            qh, kh, vh, doh = ins[0][:, sl], fulls[0][:, sl], fulls[0][:, slv], ins[1][:, sl]
            p = _xattn_probs(qh, kh)
            dp = _dot_nt(doh, vh)
            ds = (p * (dp - jnp.sum(p * dp, axis=1, keepdims=True)) * (1.0 / math.sqrt(X_HEAD_DIM))).astype(BF16)
            outs[0][:, sl] = _dot(ds, kh).astype(BF16)
            accs[0][:, sl] += _dot_tn(ds, qh)
            accs[0][:, slv] += _dot_tn(p.astype(BF16), doh)

    return _rowwise(body, name="xattn_bwd", rows=rows, tr=512, row_ins=[(xq, w, 0), (dxo, w, 0)], full_ins=[xkv],
                    row_outs=[(w, BF16)], acc_outs=[(xkv.shape, F32)])


def _gate_fwd(proj, pa, pb, b_gate):
    rows = proj.shape[0]

    def body(i, ins, fulls, outs, accs):
        sa = _sigmoid(ins[0][...].astype(F32) + fulls[0][0:1, :])
        sb = _sigmoid(ins[1][...].astype(F32) + fulls[0][1:2, :])
        outs[0][...] = (sa * ins[2][...].astype(F32) + sb * ins[3][...].astype(F32)).astype(BF16)

    return _rowwise(body, name="gate_fwd", rows=rows, tr=512,
                    row_ins=[(proj, D_MODEL, C_GA // D_MODEL), (proj, D_MODEL, C_GB // D_MODEL), (pa, D_MODEL, 0),
                             (pb, D_MODEL, 0)],
                    full_ins=[b_gate], row_outs=[(D_MODEL, BF16)])[0]


def _gate_bwd(proj, pa, pb, b_gate, dm):
    rows = proj.shape[0]

    def body(i, ins, fulls, outs, accs):
        _acc_init(i, accs)
        d = ins[4][...].astype(F32)
        sa = _sigmoid(ins[0][...].astype(F32) + fulls[0][0:1, :])
        sb = _sigmoid(ins[1][...].astype(F32) + fulls[0][1:2, :])
        dga = d * ins[2][...].astype(F32) * sa * (1.0 - sa)
        dgb = d * ins[3][...].astype(F32) * sb * (1.0 - sb)
        outs[0][...] = (d * sa).astype(BF16)
        outs[1][...] = (d * sb).astype(BF16)
        outs[2][...] = dga.astype(BF16)
        outs[3][...] = dgb.astype(BF16)
        accs[0][0:1, :] += jnp.sum(dga, axis=0, keepdims=True)
        accs[0][1:2, :] += jnp.sum(dgb, axis=0, keepdims=True)

    return _rowwise(body, name="gate_bwd", rows=rows, tr=512,
                    row_ins=[(proj, D_MODEL, C_GA // D_MODEL), (proj, D_MODEL, C_GB // D_MODEL), (pa, D_MODEL, 0),
                             (pb, D_MODEL, 0), (dm, D_MODEL, 0)],
                    full_ins=[b_gate], row_outs=[(D_MODEL, BF16)] * 4, acc_outs=[((2, D_MODEL), F32)])


def _loss_head(x3, target, g_final):
    rows = x3.shape[0]

    def body(i, ins, fulls, outs, accs):
        _acc_init(i, accs)
        xv, g = ins[0][...], fulls[0][...]
        d = _rms(xv, g) - ins[1][...]
        dx, dg = _rms_bwd(xv, g, d * (1.0 / D_MODEL))
        outs[0][...] = dx
        accs[0][...] += dg
        accs[1][...] += jnp.sum(d * d, axis=0, keepdims=True)

    return _rowwise(body, name="loss_head", rows=rows, tr=512, row_ins=[(x3, D_MODEL, 0), (target, D_MODEL, 0)],
                    full_ins=[g_final], row_outs=[(D_MODEL, F32)], acc_outs=[((1, D_MODEL), F32), ((1, D_MODEL), F32)])


def _adamw(w, g, m, v, name):
    rows, c = w.shape

    def body(i, ins, fulls, outs, accs):
        wv, gv = ins[0][...], ins[1][...]
        mn = ADAM_B1 * ins[2][...] + (1.0 - ADAM_B1) * gv
        vn = ADAM_B2 * ins[3][...] + (1.0 - ADAM_B2) * jnp.square(gv)
        m_hat = mn / (1.0 - ADAM_B1 ** ADAM_STEP)
        v_hat = vn / (1.0 - ADAM_B2 ** ADAM_STEP)
        outs[0][...] = -ADAM_LR * (m_hat / (jnp.sqrt(v_hat) + ADAM_EPS) + ADAM_WD * wv)
        outs[1][...] = mn
        outs[2][...] = vn

    return _rowwise(body, name=name, rows=rows, tr=_row_tile(rows, 256), row_ins=[(a, c, 0) for a in (w, g, m, v)],
                    row_outs=[(c, F32)] * 3)


def _place():
    x, y, c = lax.axis_index("x"), lax.axis_index("y"), lax.axis_index("c")
    chips = [(1 - x, y), (x, 1 - y), (1 - x, 1 - y)]
    return x, y, c, chips


ANY = pl.BlockSpec(memory_space=pl.ANY)


def _remote(src, dst, send_sem, recv_sem, to):
    return pltpu.make_async_remote_copy(src_ref=src, dst_ref=dst, send_sem=send_sem, recv_sem=recv_sem,
                                        device_id=to, device_id_type=MESH)


def _dma_sems(n):
    return [pltpu.SemaphoreType.DMA((n,)), pltpu.SemaphoreType.DMA((n,))]


class _GatherPhases:
    def __init__(self, ins, outs, send_sems, recv_sems, local_sems=None):
        self.ins, self.outs, self.send_sems, self.recv_sems = ins, outs, send_sems, recv_sems
        self.local_sems = local_sems
        self.x, self.y, self.c, self.chips = _place()
        self.me = 2 * self.x + self.y

    def _locals(self):
        if self.local_sems is None:
            return []
        return [pltpu.make_async_copy(self.ins[t], self.outs[t].at[self.me], self.local_sems.at[t])
                for t in range(len(self.ins))]

    def _copy(self, t, j, chip_idx, hlf, to, src=None):
        h = self.ins[t].shape[0] // 2
        dst = self.outs[t].at[chip_idx, pl.ds(hlf * h, h), :]
        return _remote(dst if src is None else src, dst, self.send_sems.at[6 * t + j], self.recv_sems.at[6 * t + j], to)

    def _sends(self):
        out = []
        for t in range(len(self.ins)):
            h = self.ins[t].shape[0] // 2
            for j, chip in enumerate(self.chips):
                out.append(self._copy(t, j, self.me, self.c, (*chip, self.c), src=self.ins[t].at[pl.ds(self.c * h, h), :]))
        return out

    def _forwards(self):
        return [self._copy(t, 3 + j, 2 * chip[0] + chip[1], self.c, (self.x, self.y, 1 - self.c))
                for t in range(len(self.ins)) for j, chip in enumerate(self.chips)]

    def send(self):
        for cp in self._sends() + self._locals():
            cp.start()

    def forward(self):
        here = (self.x, self.y, self.c)
        landed = [self._copy(t, j, 2 * chip[0] + chip[1], self.c, here)
                  for t in range(len(self.ins)) for j, chip in enumerate(self.chips)]
        for arrival, fwd in zip(landed, self._forwards()):
            arrival.wait_recv()
            fwd.start()

    def finish(self):
        here = (self.x, self.y, self.c)
        for t in range(len(self.ins)):
            for j, chip in enumerate(self.chips):
                self._copy(t, 3 + j, 2 * chip[0] + chip[1], 1 - self.c, here).wait_recv()
        for cp in self._sends() + self._forwards():
            cp.wait_send()
        for cp in self._locals():
            cp.wait()


def _all_gather_weights(shards):
    n = len(shards)

    def body(*refs):
        gather = _GatherPhases(refs[:n], refs[n:2 * n], *refs[2 * n:])
        gather.send()
        gather.forward()
        gather.finish()

    return pl.pallas_call(
        body, name="all_gather_weights", in_specs=[ANY] * n, out_specs=[ANY] * n,
        out_shape=[jax.ShapeDtypeStruct((N_CHIPS,) + s.shape, s.dtype) for s in shards],
        scratch_shapes=_dma_sems(6 * n),
    )(*shards)


def _exchange_copies(ins, outs, send_sems, recv_sems):
    x, y, c, _ = _place()
    cps = []
    for t in range(len(ins)):
        h = ins[t].shape[1] // 2
        cps.append(_remote(ins[t].at[:, pl.ds((1 - c) * h, h), :], outs[t], send_sems.at[t], recv_sems.at[t],
                           (x, y, 1 - c)))
    return cps


def _exchange_shapes(stacked):
    return [jax.ShapeDtypeStruct((N_CHIPS, s.shape[1] // 2, s.shape[2]), s.dtype) for s in stacked]


def _scatter_copies(ins, outs, send_sems, recv_sems):
    x, y, c, chips = _place()
    return [_remote(ins[t].at[2 * chip[0] + chip[1]], outs[t].at[j], send_sems.at[3 * t + j], recv_sems.at[3 * t + j],
                    (*chip, c)) for t in range(len(ins)) for j, chip in enumerate(chips)]


def _scatter_shapes(parts):
    return [jax.ShapeDtypeStruct((N_CHIPS - 1,) + s.shape[1:], s.dtype) for s in parts]


def _ride_along(copies_of, n, refs, first, last):
    if not n:
        return lambda: None

    def start():
        for cp in copies_of(*refs):
            cp.start()

    def wait():
        for cp in copies_of(*refs):
            cp.wait()

    pl.when(first)(start)
    return lambda: pl.when(last)(wait)


def _pair_exchange_halves(shards):
    n = len(shards)

    def body(*refs):
        bufs = refs[n:2 * n]
        send_sems, recv_sems = refs[2 * n:]
        x, y, c, _ = _place()
        cps = []
        for t in range(n):
            h = bufs[t].shape[0] // 2
            rows = bufs[t].at[pl.ds(c * h, h), :]
            cps.append(_remote(rows, rows, send_sems.at[t], recv_sems.at[t], (x, y, 1 - c)))
            cps[-1].start()
        for cp in cps:
            cp.wait()

    return pl.pallas_call(
        body, name="pair_exchange_halves", in_specs=[ANY] * n, out_specs=[ANY] * n,
        out_shape=[jax.ShapeDtypeStruct(s.shape, s.dtype) for s in shards],
        input_output_aliases={t: t for t in range(n)},
        scratch_shapes=_dma_sems(n),
    )(*shards)


def _pair_sum(gs, recv, place, name):
    _, r, cols = gs.shape
    h = r // 2

    def kern(p_ref, a_ref, b_ref, o_ref):
        o_ref[...] = (a_ref[...] + b_ref[...]).astype(BF16)

    blk = lambda f: pl.BlockSpec((1, h, cols), f)
    return pl.pallas_call(
        kern, name=name,
        grid_spec=pltpu.PrefetchScalarGridSpec(
            num_scalar_prefetch=1, grid=(N_CHIPS,),
            in_specs=[blk(lambda d, p: (d, p[1], 0)), blk(lambda d, p: (d, 0, 0))],
            out_specs=blk(lambda d, p: (d, 0, 0))),
        out_shape=jax.ShapeDtypeStruct((N_CHIPS, h, cols), BF16),
        compiler_params=pltpu.CompilerParams(dimension_semantics=("arbitrary",),
                                             vmem_limit_bytes=_vmem_limit(3 * _nbytes((h, cols), F32), 0)),
    )(place, gs, recv)


def _chip_sum(gs, recv, got, place, name):
    _, r, cols = gs.shape
    h = r // 2

    def kern(p_ref, a_ref, b_ref, g0, g1, g2, o_ref):
        own = a_ref[0] + b_ref[0]
        o_ref[...] = ((own + g0[0].astype(F32)) + g1[0].astype(F32)) + g2[0].astype(F32)

    blk = lambda f: pl.BlockSpec((1, h, cols), f)
    return pl.pallas_call(
        kern, name=name,
        grid_spec=pltpu.PrefetchScalarGridSpec(
            num_scalar_prefetch=1, grid=(1,),
            in_specs=[blk(lambda i, p: (p[0], p[1], 0)), blk(lambda i, p: (p[0], 0, 0)), blk(lambda i, p: (0, 0, 0)),
                      blk(lambda i, p: (1, 0, 0)), blk(lambda i, p: (2, 0, 0))],
            out_specs=pl.BlockSpec((h, cols), lambda i, p: (p[1], 0))),
        out_shape=jax.ShapeDtypeStruct((r, cols), F32),
        compiler_params=pltpu.CompilerParams(dimension_semantics=("arbitrary",),
                                             vmem_limit_bytes=_vmem_limit(5 * _nbytes((h, cols), F32), 0)),
    )(place, gs, recv, got, got, got)


def _all_reduce_small(vec, name):
    r, cols = vec.shape

    def body(in_ref, out_ref, gath, send_sems, recv_sems):
        x, y, c, _ = _place()
        me = 4 * x + 2 * y + c
        gath[me] = in_ref[...]
        sends = []
        for k in range(1, 8):
            to = (x ^ (k >> 2), y ^ ((k >> 1) & 1), c ^ (k & 1))
            cp = pltpu.make_async_remote_copy(src_ref=in_ref, dst_ref=gath.at[me], send_sem=send_sems.at[k - 1],
                                              recv_sem=recv_sems.at[k - 1], device_id=to, device_id_type=MESH)
            cp.start()
            sends.append(cp)
        for k in range(1, 8):
            peer = me ^ k
            pltpu.make_async_remote_copy(src_ref=in_ref, dst_ref=gath.at[peer], send_sem=send_sems.at[k - 1],
                                         recv_sem=recv_sems.at[k - 1], device_id=(x, y, c),
                                         device_id_type=MESH).wait_recv()
        for cp in sends:
            cp.wait_send()
        acc = gath[0]
        for d in range(1, 8):
            acc = acc + gath[d]
        out_ref[...] = acc

    vm = pl.BlockSpec(memory_space=pltpu.VMEM)
    return pl.pallas_call(
        body, name=name, in_specs=[vm], out_specs=vm,
        out_shape=jax.ShapeDtypeStruct((r, cols), F32),
        scratch_shapes=[pltpu.VMEM((8, r, cols), F32), pltpu.SemaphoreType.DMA((7,)), pltpu.SemaphoreType.DMA((7,))],
    )(vec)


def _pad_heads(w, heads, dim, axis):
    shp = w.shape[:axis] + (heads, dim) + w.shape[axis + 1:]
    pad = [(0, 0)] * len(shp)
    pad[axis + 1] = (0, LANE - dim)
    w = jnp.pad(w.reshape(shp), pad)
    return w.reshape(w.shape[:axis] + (heads * LANE,) + w.shape[axis + 2:])


def _unpad_heads(w, heads, dim, axis):
    shp = w.shape[:axis] + (heads, LANE) + w.shape[axis + 1:]
    w = lax.slice_in_dim(w.reshape(shp), 0, dim, axis=axis + 1)
    return w.reshape(w.shape[:axis] + (heads * dim,) + w.shape[axis + 2:])


def _w_in_layout(w_in):
    kr = jnp.pad(w_in[:, 384:416], ((0, 0), (ROPE_LO, LANE - ROPE_LO - MLA_ROPE)))
    sb = lambda lo: w_in[:, lo:lo + SB_WIDTH]
    return jnp.concatenate([w_in[:, 1952:2976], w_in[:, 2976:4000], sb(416), sb(928), sb(1440), w_in[:, 0:256],
                            w_in[:, 256:384], kr], axis=1)


def _w_in_unlayout(d):
    sb = lambda lo: d[:, lo:lo + SB_WIDTH]
    return jnp.concatenate([d[:, C_CQ:C_CQ + 256], d[:, C_CKV:C_CKV + 128], d[:, C_KR + ROPE_LO:C_KR + ROPE_LO + MLA_ROPE],
                            sb(C_SBQ), sb(C_SBK), sb(C_SBV), d[:, C_GA:C_GA + 1024], d[:, C_GB:C_GB + 1024]], axis=1)


def _w_ukv_layout(w):
    w3 = w.reshape(MLA_KV_RANK, MLA_HEADS, MLA_NOPE + MLA_V)
    pad = lambda part: jnp.pad(part, ((0, 0), (0, 0), (0, LANE - part.shape[2]))).reshape(MLA_KV_RANK, MLA_HEADS * LANE)
    return jnp.concatenate([pad(w3[:, :, :MLA_NOPE]), pad(w3[:, :, MLA_NOPE:])], axis=1)


def _w_ukv_unlayout(d):
    hw = MLA_HEADS * LANE
    kpart = d[:, :hw].reshape(MLA_KV_RANK, MLA_HEADS, LANE)[:, :, :MLA_NOPE]
    vpart = d[:, hw:].reshape(MLA_KV_RANK, MLA_HEADS, LANE)[:, :, :MLA_V]
    return jnp.concatenate([kpart, vpart], axis=2).reshape(MLA_KV_RANK, MLA_HEADS * (MLA_NOPE + MLA_V))


def _shard_of(full, d, axis):
    n = full.shape[axis] // N_CHIPS
    return lax.slice_in_dim(full, d * n, (d + 1) * n, axis=axis)


def _local_step(x, mem, pos, target, w, t_mla, t_sb, late=None, reduce=None):
    s = x.shape[0]
    w = dict(w)
    win = _w_in_layout(w["w_in"])
    wuq = _pad_heads(w["w_uq"], MLA_HEADS, MLA_NOPE + MLA_ROPE, 1)
    wkv = _w_ukv_layout(w["w_ukv"])
    inv_freq = ROPE_THETA ** (-jnp.arange(0, MLA_ROPE, 2, dtype=F32) / MLA_ROPE)
    freq_lane = jnp.pad(jnp.concatenate([inv_freq, inv_freq]), (ROPE_LO, LANE - ROPE_LO - MLA_ROPE)).reshape(1, LANE)
    add = lambda accs, ex: (accs[0] + ex[0],)

    def add_norm(accs, ex):
        y = accs[0] + ex[0]
        return y, _rms(y, ex[1])

    tab = _rope_tables(pos.reshape(s, 1), freq_lane)
    h = _rms_fwd_call(x, w["g_mix"], "rms_mix")
    proj = _mm(h, [win], name="proj_in", out_dtypes=(BF16,))
    cqn, ckvn, krope = _mla_prep_fwd(proj, tab, w["g_q_lat"], w["g_kv_lat"])
    hw = MLA_HEADS * LANE
    qa = _mm(cqn, [wuq], name="q_up", row_extras=(tab,), out_dtypes=(BF16,),
             epilogue=lambda accs, ex: (_per_head(lambda t: _rope(t, ex[0]) * (MLA_SCALE * LOG2E), accs[0]),))
    ka = _mm(ckvn, [wkv[:, :hw]], name="k_up", row_extras=(krope,), out_dtypes=(BF16,),
             epilogue=lambda accs, ex: (_per_head(lambda t: t + ex[0], accs[0]),))
    va = _mm(ckvn, [wkv[:, hw:]], name="v_up", out_dtypes=(BF16,))
    o_a, lse, gathered = _mla_fwd(qa, ka, va, t_mla[0], t_mla[1], late[0] if late else ())
    if late:
        w.update(late[1](gathered))
    wa = _pad_heads(w["w_a_proj"], MLA_HEADS, MLA_V, 0)
    wb = w["w_b_proj"]
    o_b = _sb_fwd(proj, t_sb)
    pa = _mm(o_a, [wa], name="proj_a", out_dtypes=(BF16,))
    pb = _mm(o_b, [wb], name="proj_b", out_dtypes=(BF16,))
    merged = _gate_fwd(proj, pa, pb, w["b_gate"])
    x1, hx = _mm(merged, [w["w_o"]], name="proj_o", extras=(x,), consts=(w["g_x"],), epilogue=add_norm,
                 out_dtypes=(F32, BF16))
    mn = _rms_fwd_call(mem, w["g_mem"], "rms_mem")
    xq = _mm(hx, [w["w_xq"]], name="xq", out_dtypes=(BF16,))
    xkv = _mm(mn, [w["w_xkv"]], name="xkv", out_dtypes=(BF16,))
    xo = _xattn_fwd(xq, xkv)
    x2, hf = _mm(xo, [w["w_xo"]], name="proj_xo", extras=(x1,), consts=(w["g_ffn"],), epilogue=add_norm,
                 out_dtypes=(F32, BF16))

    def swiglu(accs, ex):
        a, b = accs
        return a, b, a * _sigmoid(a) * b

    ga, gu, hmid = _mm(hf, [w["w_gate"], w["w_up"]], name="ffn_up", epilogue=swiglu, out_dtypes=(BF16, BF16, BF16),
                       tm=512, tn=1408)
    x3 = _mm(hmid, [w["w_down"]], name="ffn_down", extras=(x2,), epilogue=add, tk=2816)

    dx3, dg_final, sq = _loss_head(x3, target, w["g_final"].reshape(1, D_MODEL))
    g = {"g_final": dg_final.reshape(D_MODEL)}

    def swiglu_bwd(accs, ex):
        dh, a, b = accs[0], ex[0].astype(F32), ex[1].astype(F32)
        sg = _sigmoid(a)
        return dh * b * sg * (1.0 + a * (1.0 - sg)), dh * a * sg

    da, db = _mm(dx3, [w["w_down"]], name="ffn_down_dx", tb=True, extras=(ga, gu), epilogue=swiglu_bwd,
                 out_dtypes=(BF16, BF16), tm=512, tn=1408, chunk=MM_CHUNK)
    g["w_down"] = _mm(hmid, [dx3], name="ffn_down_dw", ta=True, tm=1408)
    g["w_gate"] = _mm(hf, [da], name="ffn_gate_dw", ta=True, tn=1408)
    g["w_up"] = _mm(hf, [db], name="ffn_up_dw", ta=True, tn=1408)
    dhf = _mm(da, [w["w_gate"]], name="ffn_gate_dx", tb=True, tk=2816)
    dhf = _mm(db, [w["w_up"]], name="ffn_up_dx", tb=True, extras=(dhf,), epilogue=add, tk=2816,
              out_dtypes=(BF16,))
    dx2, g["g_ffn"] = _rms_bwd_call(x2, w["g_ffn"], dhf, dx3, "rms_ffn_bwd")

    dxo = _mm(dx2, [w["w_xo"]], name="proj_xo_dx", tb=True, out_dtypes=(BF16,))
    g["w_xo"] = _mm(xo, [dx2], name="proj_xo_dw", ta=True)
    dxq, dxkv = _xattn_bwd(xq, xkv, dxo)
    dhx = _mm(dxq, [w["w_xq"]], name="xq_dx", tb=True, out_dtypes=(BF16,))
    g["w_xq"] = _mm(hx, [dxq], name="xq_dw", ta=True)
    dmn = _mm(dxkv, [w["w_xkv"]], name="xkv_dx", tb=True)
    g["w_xkv"] = _mm(mn, [dxkv], name="xkv_dw", ta=True)
    dx1, g["g_x"] = _rms_bwd_call(x1, w["g_x"], dhx, dx2, "rms_x_bwd")
    _, g["g_mem"] = _rms_bwd_call(mem, w["g_mem"], dmn, None, "rms_mem_bwd")

    dmerged = _mm(dx1, [w["w_o"]], name="proj_o_dx", tb=True, out_dtypes=(BF16,))
    g["w_o"] = _mm(merged, [dx1], name="proj_o_dw", ta=True)
    dpa, dpb, dga, dgb, g["b_gate"] = _gate_bwd(proj, pa, pb, w["b_gate"], dmerged)
    do_a = _mm(dpa, [wa], name="proj_a_dx", tb=True, out_dtypes=(BF16,))
    do_b = _mm(dpb, [wb], name="proj_b_dx", tb=True, out_dtypes=(BF16,))
    g["w_a_proj"] = _unpad_heads(_mm(o_a, [dpa], name="proj_a_dw", ta=True), MLA_HEADS, MLA_V, 0)
    g["w_b_proj"] = _mm(o_b, [dpb], name="proj_b_dw", ta=True)

    stacked = [reduce[1](n, g[n]) for n in reduce[0]] if reduce else []
    dsq, dsk, dsv, recv = _sb_bwd(proj, o_b, do_b, t_sb, stacked)
    parts = [reduce[2](n, gs, rv) for n, gs, rv in zip(reduce[0], stacked, recv)] if reduce else []
    dqa, dka, dva, got = _mla_bwd(qa, ka, va, o_a, do_a, lse, t_mla[0], t_mla[2], parts)
    riding = dict(zip(reduce[0], zip(stacked, recv, got))) if reduce else {}
    dqp, dkvp, dkr = _mla_rope_bwd(dqa, dka, dva, tab)
    g["w_uq"] = _unpad_heads(_mm(cqn, [dqp], name="q_up_dw", ta=True), MLA_HEADS, MLA_NOPE + MLA_ROPE, 1)
    g["w_ukv"] = _w_ukv_unlayout(_mm(ckvn, [dkvp], name="kv_up_dw", ta=True))
    dcqn = _mm(dqp, [wuq], name="q_up_dx", tb=True)
    dckvn = _mm(dkvp, [wkv], name="kv_up_dx", tb=True)
    dcq, dckv, g["g_q_lat"], g["g_kv_lat"] = _mla_prep_bwd(proj, w["g_q_lat"], w["g_kv_lat"], dcqn, dckvn)

    dproj = jnp.concatenate([dga, dgb, dsq.astype(BF16), dsk.astype(BF16), dsv.astype(BF16), dcq, dckv, dkr], axis=1)
    g["w_in"] = _w_in_unlayout(_mm(h, [dproj], name="proj_in_dw", ta=True))
    if not reduce:
        dh = _mm(dproj, [win], name="proj_in_dx", tb=True, tk=2048, out_dtypes=(BF16,))
        grad_x, g["g_mix"] = _rms_bwd_call(x, w["g_mix"], dh, dx1, "rms_mix_bwd")
        return sq, grad_x, g, riding
    stacked = [reduce[1](n, g[n]) for n in reduce[3]]
    dh, recv = _mm(dproj, [win], name="proj_in_dx", tb=True, tk=2048, out_dtypes=(BF16,),
                   ride=(_exchange_copies, stacked, _exchange_shapes(stacked), 1))
    parts = [reduce[2](n, gs, rv) for n, gs, rv in zip(reduce[3], stacked, recv)]
    grad_x, g["g_mix"], *got = _rms_bwd_call(x, w["g_mix"], dh, dx1, "rms_mix_bwd",
                                             ride=(_scatter_copies, parts, _scatter_shapes(parts), 3))
    riding.update(zip(reduce[3], zip(stacked, recv, got)))
    return sq, grad_x, g, riding


def _small_pack(d):
    row5 = jnp.concatenate([d["g_q_lat"].reshape(-1), d["g_kv_lat"].reshape(-1), jnp.zeros((640,), F32)])
    rows = [d[n].reshape(-1) for n in ("g_mix", "g_x", "g_mem", "g_ffn", "g_final")] + [row5]
    return rows


def _small_unpack(p, like):
    out = {n: p[i].reshape(like[n].shape) for i, n in enumerate(("g_mix", "g_x", "g_mem", "g_ffn", "g_final"))}
    out["g_q_lat"] = p[5, 0:256].reshape(like["g_q_lat"].shape)
    out["g_kv_lat"] = p[5, 256:384].reshape(like["g_kv_lat"].shape)
    return out


def kernel(x, mem, positions, g_mix, w_in, b_gate, g_q_lat, w_uq, g_kv_lat, w_ukv, w_a_proj, w_b_proj, w_o, g_x, g_mem, w_xq, w_xkv, w_xo, g_ffn, w_gate, w_up, w_down, g_final, loss_target, m_g_mix, m_w_in, m_b_gate, m_g_q_lat, m_w_uq, m_g_kv_lat, m_w_ukv, m_w_a_proj, m_w_b_proj, m_w_o, m_g_x, m_g_mem, m_w_xq, m_w_xkv, m_w_xo, m_g_ffn, m_w_gate, m_w_up, m_w_down, m_g_final, v_g_mix, v_w_in, v_b_gate, v_g_q_lat, v_w_uq, v_g_kv_lat, v_w_ukv, v_w_a_proj, v_w_b_proj, v_w_o, v_g_x, v_g_mem, v_w_xq, v_w_xkv, v_w_xo, v_g_ffn, v_w_gate, v_w_up, v_w_down, v_g_final):
    given = dict(locals())
    names = [n for n, _, _ in MATS] + ["b_gate"] + list(SMALL)
    wts = {n: given[n] for n in names}
    mom = {n: given["m_" + n] for n in names}
    var = {n: given["v_" + n] for n in names}
    shard2d = {n: shp for n, shp, _ in MATS}
    shard2d["b_gate"] = B_GATE_SHARD
    cx, cy, cc = lax.axis_index("x"), lax.axis_index("y"), lax.axis_index("c")
    me = 2 * cx + cy
    place = jnp.stack([me, cc]).astype(jnp.int32)
    bcol = me * B_GATE_SHARD[1]

    own = [wts[n].reshape(shard2d[n]).astype(BF16) for n, _, _ in MATS]
    bias = (("b_gate", (BIAS_ROWS, B_GATE_SHARD[1]), 1),)
    own_bias = [jnp.pad(wts["b_gate"].reshape(B_GATE_SHARD), ((0, BIAS_ROWS - B_GATE_SHARD[0]), (0, 0)))]

    def assemble(mats, gathered, mine=None):
        out = {}
        for k, ((n, shp, ax), g4) in enumerate(zip(mats, gathered)):
            if mine is not None:
                g4 = lax.dynamic_update_slice(g4, mine[k][None], (me, 0, 0))
            out[n] = g4.reshape(N_CHIPS * shp[0], shp[1]) if ax == 0 else jnp.concatenate(list(g4), axis=1)
        return out

    first = own[:N_EARLY] + own_bias
    full = assemble(MATS[:N_EARLY] + bias, _all_gather_weights(first), first)
    full["b_gate"] = full["b_gate"][0:B_GATE_SHARD[0]]
    late = (own[N_EARLY:], lambda gathered: assemble(MATS[N_EARLY:], gathered))
    for n in SMALL:
        full[n] = wts[n].reshape(1, -1) if n != "g_final" else wts[n]

    axis_of = {n: ax for n, _, ax in MATS}
    stack = lambda n, g: jnp.stack([_shard_of(g, d, axis_of[n]) for d in range(N_CHIPS)])
    pair_sum = lambda n, gs, rv: _pair_sum(gs, rv, place, "pair_sum_" + n)
    behind = [n for n, _, _ in MATS[N_EARLY:]]
    last = [n for n, _, _ in MATS[:N_EARLY]]
    sq, grad_x, grads, riding = _local_step(x[0], mem[0], positions[0], loss_target[0], full, t_mla=(1024, 1024, 1024), t_sb=256,
                                            late=late, reduce=(behind, stack, pair_sum, last))

    halves = [_chip_sum(*riding[n], place, "chip_sum_" + n) for n, _, _ in MATS]
    g_shard = dict(zip([n for n, _, _ in MATS], _pair_exchange_halves(halves)))

    small_rows = _small_pack({n: grads[n] for n in SMALL}) + [sq.reshape(-1), grads["b_gate"][0], grads["b_gate"][1]]
    small_rows += [jnp.zeros((D_MODEL,), F32)] * (SMALL_ROWS - len(small_rows))
    small = _all_reduce_small(jnp.stack(small_rows), "all_reduce_small")
    loss = (0.5 / D_MODEL) * jnp.sum(small[6])
    g_shard["b_gate"] = lax.dynamic_slice(small[7:9], (0, bcol), B_GATE_SHARD)

    out = {"grad": {}, "delta": {}, "m": {}, "v": {}}
    for n in [n for n, _, _ in MATS] + ["b_gate"]:
        shape = wts[n].shape
        r2 = lambda a: a.reshape(shard2d[n])
        d_n, m_n, v_n = _adamw(r2(wts[n]), g_shard[n], r2(mom[n]), r2(var[n]), "adamw_" + n)
        for key, a in (("grad", g_shard[n]), ("delta", d_n), ("m", m_n), ("v", v_n)):
            out[key][n] = a.reshape(shape)
    sp = lambda d: jnp.stack(_small_pack(d) + [jnp.zeros((D_MODEL,), F32)] * 2)
    delta_s, m_s, v_s = _adamw(sp(wts), small[0:8].at[6:8].set(0.0), sp(mom), sp(var), "adamw_small")
    for key, p in (("grad", small), ("delta", delta_s), ("m", m_s), ("v", v_s)):
        out[key].update(_small_unpack(p, wts))

    order = ["g_mix", "w_in", "b_gate", "g_q_lat", "w_uq", "g_kv_lat", "w_ukv", "w_a_proj", "w_b_proj", "w_o", "g_x",
             "g_mem", "w_xq", "w_xkv", "w_xo", "g_ffn", "w_gate", "w_up", "w_down", "g_final"]
    return (loss, grad_x[None], *[out[key][n] for key in ("grad", "delta", "m", "v") for n in order])
```

```python
import functools
import math

import jax
import jax.numpy as jnp
from jax import lax
from jax.experimental import pallas as pl
from jax.experimental.pallas import tpu as pltpu

F32 = jnp.float32
BF16 = jnp.bfloat16
MESH = pl.DeviceIdType.MESH

D_MODEL = 1024
MLA_HEADS = 8
MLA_Q_RANK = 256
MLA_KV_RANK = 128
MLA_NOPE = 64
MLA_ROPE = 32
MLA_V = 64
ROPE_THETA = 10000.0
SB_HEADS = 8
SB_HEAD_DIM = 64
X_HEADS = 4
X_HEAD_DIM = 128
EPS = 1e-6
ADAM_LR = 0.001
ADAM_B1 = 0.9
ADAM_B2 = 0.999
ADAM_EPS = 1e-08
ADAM_WD = 0.01
ADAM_STEP = 10

LANE = 128
LOG2E = 1.4426950408889634
MLA_SCALE = 1.0 / math.sqrt(MLA_NOPE + MLA_ROPE)
SB_SCALE = 1.0 / math.sqrt(SB_HEAD_DIM)
assert math.log2(SB_SCALE) == round(math.log2(SB_SCALE))
MM_CHUNK = 256
N_CHIPS = 4
VMEM_BYTES = 64 * 1024 * 1024

C_GA, C_GB, C_SBQ, C_SBK, C_SBV, C_CQ, C_CKV, C_KR = 0, 1024, 2048, 2560, 3072, 3584, 3840, 3968
SB_WIDTH = SB_HEADS * SB_HEAD_DIM
ROPE_LO = MLA_NOPE
HALF = MLA_ROPE // 2

SB_ZERO_LOG = -104.0

MATS = (
    ("w_in", (1024, 1000), 1), ("w_uq", (256, 192), 1), ("w_ukv", (128, 256), 1), ("w_a_proj", (512, 256), 1),
    ("w_b_proj", (512, 256), 1), ("w_o", (256, 1024), 0), ("w_xq", (256, 512), 0), ("w_xkv", (256, 1024), 0),
    ("w_xo", (512, 256), 1), ("w_gate", (1024, 704), 1), ("w_up", (1024, 704), 1), ("w_down", (704, 1024), 0),
)
N_EARLY = 3
B_GATE_SHARD = (2, 256)
BIAS_ROWS = 16
SMALL = ("g_mix", "g_x", "g_mem", "g_ffn", "g_final", "g_q_lat", "g_kv_lat")
SMALL_ROWS = 16


def _vmem_limit(block_bytes, temp_bytes):
    est = 2 * block_bytes + temp_bytes + (4 << 20)
    return int(min(max(est, 16 << 20), VMEM_BYTES - (6 << 20)))


def _nbytes(shape, dtype):
    return math.prod(shape) * jnp.dtype(dtype).itemsize


def _row_tile(rows, cap):
    if rows <= cap:
        return rows
    return max(t for t in range(8, cap + 1, 8) if rows % t == 0)


def _tile(n, cap):
    if n <= cap:
        return n
    best = None
    for t in range(LANE, cap + 1, LANE):
        if n % t == 0:
            best = t
    assert best is not None, (n, cap)
    return best


def _mm(a, bs, *, name, ta=False, tb=False, extras=(), row_extras=(), consts=(), epilogue=None, out_dtypes=(F32,),
        tm=1024, tn=1024, tk=1024, chunk=None, ride=None):
    bs = tuple(bs)
    m, k = (a.shape[1], a.shape[0]) if ta else a.shape
    n = bs[0].shape[0] if tb else bs[0].shape[1]
    tm, tn, tk = _tile(m, tm), _tile(n, tn), _tile(k, tk)
    assert m % tm == 0 and n % tn == 0 and k % tk == 0
    nk = k // tk
    nb, ne, no = len(bs), len(extras) + len(row_extras) + len(consts), len(out_dtypes)
    dims = (((0,) if ta else (1,)), ((1,) if tb else (0,))), ((), ())
    if epilogue is None:
        epilogue = lambda accs, ex: (accs[0],)

    rn = len(ride[1]) if ride else 0
    n_acc = nb if nk > 1 else 0
    gi, gj = m // tm, n // tn

    def body(*refs):
        a_ref, b_refs, e_refs = refs[0], refs[1:1 + nb], refs[1 + nb:1 + nb + ne]
        base = 1 + nb + ne + rn
        o_refs, acc_refs = refs[base:base + no], refs[base + no + rn:base + no + rn + n_acc]
        step = [pl.program_id(d) for d in range(3)]
        finish = _ride_along(ride[0] if ride else None, rn,
                             (refs[base - rn:base], refs[base + no:base + no + rn], *refs[base + no + rn + n_acc:]),
                             (step[0] == 0) & (step[1] == 0) & (step[2] == 0),
                             (step[0] == gi - 1) & (step[1] == gj - 1) & (step[2] == nk - 1))
        if nk == 1:
            ch = chunk or tm
            bvs = [b_ref[...].astype(BF16) for b_ref in b_refs]
            for r0 in range(0, tm, ch):
                rows = slice(r0, r0 + ch)
                av = (a_ref[:, rows] if ta else a_ref[rows, :]).astype(BF16)
                accs = [lax.dot_general(av, bv, dims, preferred_element_type=F32) for bv in bvs]
                ex = [e[rows, :] for e in e_refs[:ne - len(consts)]] + [e[...] for e in e_refs[ne - len(consts):]]
                for o_ref, v in zip(o_refs, epilogue(accs, ex)):
                    o_ref[rows, :] = v.astype(o_ref.dtype)
            finish()
            return
        kk = step[2]

        @pl.when(kk == 0)
        def _():
            for acc in acc_refs:
                acc[...] = jnp.zeros_like(acc)

        av = a_ref[...].astype(BF16)
        for b_ref, acc in zip(b_refs, acc_refs):
            acc[...] += lax.dot_general(av, b_ref[...].astype(BF16), dims, preferred_element_type=F32)

        @pl.when(kk == nk - 1)
        def _():
            outs = epilogue([acc[...] for acc in acc_refs], [e[...] for e in e_refs])
            for o_ref, v in zip(o_refs, outs):
                o_ref[...] = v.astype(o_ref.dtype)

        finish()

    a_spec = pl.BlockSpec((tk, tm), lambda i, j, kk: (kk, i)) if ta else pl.BlockSpec((tm, tk), lambda i, j, kk: (i, kk))
    b_spec = pl.BlockSpec((tn, tk), lambda i, j, kk: (j, kk)) if tb else pl.BlockSpec((tk, tn), lambda i, j, kk: (kk, j))
    mn_spec = pl.BlockSpec((tm, tn), lambda i, j, kk: (i, j))
    blocks = (_nbytes((tm, tk), a.dtype) + sum(_nbytes((tk, tn), b.dtype) for b in bs)
              + sum(_nbytes((tm, tn), e.dtype) for e in extras) + sum(_nbytes((tm, tn), d) for d in out_dtypes)
              + sum(_nbytes((tm, e.shape[1]), e.dtype) for e in row_extras))
    temps = (nb + 4) * _nbytes((tm, tn), F32)
    outs = pl.pallas_call(
        body, name=name, grid=(m // tm, n // tn, nk),
        in_specs=[a_spec] + [b_spec] * nb + [mn_spec] * len(extras)
        + [pl.BlockSpec((tm, e.shape[1]), lambda i, j, kk: (i, 0)) for e in row_extras]
        + [pl.BlockSpec(e.shape, lambda i, j, kk: (0, 0)) for e in consts] + [ANY] * rn,
        out_specs=[mn_spec] * no + [ANY] * rn,
        out_shape=[jax.ShapeDtypeStruct((m, n), d) for d in out_dtypes] + (list(ride[2]) if ride else []),
        scratch_shapes=[pltpu.VMEM((tm, tn), F32) for _ in range(n_acc)] + (_dma_sems(ride[3] * rn) if ride else []),
        compiler_params=pltpu.CompilerParams(
            dimension_semantics=("arbitrary",) * 3 if ride else ("parallel", "parallel", "arbitrary"),
            vmem_limit_bytes=_vmem_limit(blocks, temps)),
    )(a, *bs, *extras, *row_extras, *consts, *(ride[1] if ride else ()))
    if ride:
        return (outs[0] if no == 1 else outs[:no]), list(outs[no:])
    return outs[0] if no == 1 else outs


def _rowwise(body, *, name, rows, tr, row_ins, full_ins=(), row_outs=(), acc_outs=(), ride=None):
    tr = min(tr, rows)
    assert rows % tr == 0
    n_ri, n_fi, n_ro, n_ao = len(row_ins), len(full_ins), len(row_outs), len(acc_outs)
    rn = len(ride[1]) if ride else 0

    def kern(*refs):
        i = pl.program_id(0)
        n_in, n_out = n_ri + n_fi, n_ro + n_ao
        outs = refs[n_in + rn:n_in + rn + n_out]
        finish = _ride_along(ride[0] if ride else None, rn,
                             (refs[n_in:n_in + rn], refs[n_in + rn + n_out:n_in + 2 * rn + n_out],
                              *refs[n_in + 2 * rn + n_out:]), i == 0, i == rows // tr - 1)
        body(i, refs[:n_ri], refs[n_ri:n_in], outs[:n_ro], outs[n_ro:])
        finish()

    in_specs = [pl.BlockSpec((tr, w), functools.partial(lambda i, c: (i, c), c=ci)) for _, w, ci in row_ins]
    in_specs += [pl.BlockSpec(f.shape, lambda i: (0, 0)) for f in full_ins]
    in_specs += [ANY] * rn
    out_specs = [pl.BlockSpec((tr, w), lambda i: (i, 0)) for w, _ in row_outs]
    out_specs += [pl.BlockSpec(s, lambda i: (0, 0)) for s, _ in acc_outs] + [ANY] * rn
    out_shape = [jax.ShapeDtypeStruct((rows, w), d) for w, d in row_outs]
    out_shape += [jax.ShapeDtypeStruct(s, d) for s, d in acc_outs] + (list(ride[2]) if ride else [])
    blocks = (sum(_nbytes((tr, w), a.dtype) for a, w, _ in row_ins) + sum(_nbytes(f.shape, f.dtype) for f in full_ins)
              + sum(_nbytes((tr, w), d) for w, d in row_outs) + sum(_nbytes(s, d) for s, d in acc_outs))
    widest = max([w for _, w, _ in row_ins] + [w for w, _ in row_outs])
    outs = pl.pallas_call(
        kern, name=name, grid=(rows // tr,), in_specs=in_specs, out_specs=out_specs, out_shape=out_shape,
        scratch_shapes=_dma_sems(ride[3] * rn) if ride else [],
        compiler_params=pltpu.CompilerParams(
            dimension_semantics=("arbitrary",) if acc_outs or ride else ("parallel",),
            vmem_limit_bytes=_vmem_limit(blocks, 8 * _nbytes((tr, widest), F32))),
    )(*[a for a, _, _ in row_ins], *full_ins, *(ride[1] if ride else ()))
    return outs


def _rms(x, g):
    r = lax.rsqrt(jnp.mean(x * x, axis=-1, keepdims=True) + EPS)
    return x * r * g


def _rms_bwd(x, g, dy):
    r = lax.rsqrt(jnp.mean(x * x, axis=-1, keepdims=True) + EPS)
    xh = x * r
    dxh = dy * g
    dx = r * (dxh - xh * jnp.mean(dxh * xh, axis=-1, keepdims=True))
    return dx, jnp.sum(dy * xh, axis=0, keepdims=True)


def _sigmoid(x):
    return 1.0 / (1.0 + jnp.exp(-x))


def _acc_init(i, refs):
    @pl.when(i == 0)
    def _():
        for r in refs:
            r[...] = jnp.zeros_like(r)


def _rms_fwd_call(x, g, name):
    rows, c = x.shape

    def body(i, ins, fulls, outs, accs):
        outs[0][...] = _rms(ins[0][...], fulls[0][...]).astype(BF16)

    return _rowwise(body, name=name, rows=rows, tr=512, row_ins=[(x, c, 0)], full_ins=[g], row_outs=[(c, BF16)])[0]


def _rms_bwd_call(x, g, dy, res, name, ride=None):
    rows, c = x.shape
    row_ins = [(x, c, 0), (dy, c, 0)] + ([(res, c, 0)] if res is not None else [])

    def body(i, ins, fulls, outs, accs):
        _acc_init(i, accs)
        dx, dg = _rms_bwd(ins[0][...], fulls[0][...], ins[1][...].astype(F32))
        if res is not None:
            dx = dx + ins[2][...]
        outs[0][...] = dx
        accs[0][...] += dg

    return _rowwise(body, name=name, rows=rows, tr=512, row_ins=row_ins, full_ins=[g], row_outs=[(c, F32)],
                    acc_outs=[((1, c), F32)], ride=ride)


def _rope_tables(pos_col, freq_lane):
    rows = pos_col.shape[0]

    def body(i, ins, fulls, outs, accs):
        ang = ins[0][...].astype(F32) * fulls[0][...]
        lane = lax.broadcasted_iota(jnp.int32, ang.shape, 1)
        cos, sin = jnp.cos(ang), jnp.sin(ang)
        first = (lane >= ROPE_LO) & (lane < ROPE_LO + HALF)
        second = (lane >= ROPE_LO + HALF) & (lane < ROPE_LO + MLA_ROPE)
        outs[0][:, 0:LANE] = jnp.where(first | second, cos, 1.0)
        outs[0][:, LANE:2 * LANE] = jnp.where(first, -sin, 0.0)
        outs[0][:, 2 * LANE:3 * LANE] = jnp.where(second, sin, 0.0)

    return _rowwise(body, name="rope_tables", rows=rows, tr=1024, row_ins=[(pos_col, 1, 0)], full_ins=[freq_lane],
                    row_outs=[(3 * LANE, F32)])[0]


def _rope(x, tab):
    return (x * tab[:, 0:LANE] + pltpu.roll(x, LANE - HALF, 1) * tab[:, LANE:2 * LANE]
            + pltpu.roll(x, HALF, 1) * tab[:, 2 * LANE:3 * LANE])


def _rope_t(dy, tab):
    return (dy * tab[:, 0:LANE] + pltpu.roll(dy * tab[:, LANE:2 * LANE], HALF, 1)
            + pltpu.roll(dy * tab[:, 2 * LANE:3 * LANE], LANE - HALF, 1))


def _mla_prep_fwd(proj, tab, g_q, g_kv):
    rows = proj.shape[0]

    def body(i, ins, fulls, outs, accs):
        outs[0][...] = _rms(ins[0][...].astype(F32), fulls[0][...]).astype(BF16)
        outs[1][...] = _rms(ins[1][...].astype(F32), fulls[1][...]).astype(BF16)
        outs[2][...] = _rope(ins[2][...].astype(F32), ins[3][...])

    return _rowwise(body, name="mla_prep_fwd", rows=rows, tr=512,
                    row_ins=[(proj, MLA_Q_RANK, C_CQ // MLA_Q_RANK), (proj, LANE, C_CKV // LANE),
                             (proj, LANE, C_KR // LANE), (tab, 3 * LANE, 0)],
                    full_ins=[g_q, g_kv], row_outs=[(MLA_Q_RANK, BF16), (MLA_KV_RANK, BF16), (LANE, F32)])


def _mla_prep_bwd(proj, g_q, g_kv, dcqn, dckvn):
    rows = proj.shape[0]

    def body(i, ins, fulls, outs, accs):
        _acc_init(i, accs)
        dcq, dgq = _rms_bwd(ins[0][...].astype(F32), fulls[0][...], ins[2][...])
        dckv, dgkv = _rms_bwd(ins[1][...].astype(F32), fulls[1][...], ins[3][...])
        outs[0][...] = dcq.astype(BF16)
        outs[1][...] = dckv.astype(BF16)
        accs[0][...] += dgq
        accs[1][...] += dgkv

    return _rowwise(body, name="mla_prep_bwd", rows=rows, tr=512,
                    row_ins=[(proj, MLA_Q_RANK, C_CQ // MLA_Q_RANK), (proj, LANE, C_CKV // LANE),
                             (dcqn, MLA_Q_RANK, 0), (dckvn, MLA_KV_RANK, 0)],
                    full_ins=[g_q, g_kv], row_outs=[(MLA_Q_RANK, BF16), (MLA_KV_RANK, BF16)],
                    acc_outs=[((1, MLA_Q_RANK), F32), ((1, MLA_KV_RANK), F32)])


def _per_head(fn, x):
    return jnp.concatenate([fn(x[:, h * LANE:(h + 1) * LANE]) for h in range(x.shape[1] // LANE)], axis=1)


def _mla_rope_bwd(dq, dk, dv, tab):
    rows = dq.shape[0]
    hw = MLA_HEADS * LANE

    def body(i, ins, fulls, outs, accs):
        t = ins[3][...]
        dkr = jnp.zeros((ins[0].shape[0], LANE), F32)
        for h in range(MLA_HEADS):
            sl = slice(h * LANE, (h + 1) * LANE)
            outs[0][:, sl] = _rope_t(ins[0][:, sl], t).astype(BF16)
            dkr = dkr + ins[1][:, sl]
        outs[1][:, 0:hw] = ins[1][...].astype(BF16)
        outs[1][:, hw:2 * hw] = ins[2][...].astype(BF16)
        lane = lax.broadcasted_iota(jnp.int32, dkr.shape, 1)
        dkr = jnp.where((lane >= ROPE_LO) & (lane < ROPE_LO + MLA_ROPE), dkr, 0.0)
        outs[2][...] = _rope_t(dkr, t).astype(BF16)

    return _rowwise(body, name="mla_rope_bwd", rows=rows, tr=512,
                    row_ins=[(dq, hw, 0), (dk, hw, 0), (dv, hw, 0), (tab, 3 * LANE, 0)],
                    row_outs=[(hw, BF16), (2 * hw, BF16), (LANE, BF16)])


def _dot_nt(a, b):
    return lax.dot_general(a, b, (((1,), (1,)), ((), ())), preferred_element_type=F32)


def _dot_tn(a, b):
    return lax.dot_general(a, b, (((0,), (0,)), ((), ())), preferred_element_type=F32)


def _dot(a, b):
    return jnp.dot(a, b, preferred_element_type=F32)


def _attn_params(s, t, n_res_f32, n_res_bf16, ride=False):
    blocks = n_res_f32 * _nbytes((s, LANE), F32) + n_res_bf16 * _nbytes((s, LANE), BF16) + 6 * _nbytes((t, LANE), F32)
    return pltpu.CompilerParams(dimension_semantics=("arbitrary" if ride else "parallel", "arbitrary"),
                                vmem_limit_bytes=_vmem_limit(blocks, 12 * _nbytes((t, t), F32)))


def _mla_fwd(q, k, v, t, tk, shards=()):
    s, hw = q.shape
    heads, nq, r = hw // LANE, s // t, t // tk
    ng = len(shards)

    def body(q_ref, k_ref, v_ref, *rest):
        o_ref, l_ref = rest[ng], rest[ng + 1]
        h, i = pl.program_id(0), pl.program_id(1)
        if ng:
            gather = _GatherPhases(rest[:ng], rest[ng + 2:2 * ng + 2], *rest[2 * ng + 2:])
            pl.when((h == 0) & (i == 0))(gather.send)
            pl.when((h == heads // 2) & (i == 0))(gather.forward)
        qv = q_ref[...]

        def step(j, carry, off):
            m, l, acc = carry
            sl = pl.ds(pl.multiple_of(j * tk, tk), tk)
            sc = _dot_nt(qv, k_ref[sl, :])
            if off is not None:
                row = lax.broadcasted_iota(jnp.int32, (t, tk), 0)
                col = lax.broadcasted_iota(jnp.int32, (t, tk), 1)
                sc = jnp.where(col + off <= row, sc, -1e30)
            m_new = jnp.maximum(m, jnp.max(sc, axis=1, keepdims=True))
            p = jnp.exp2(sc - m_new)
            alpha = jnp.exp2(m - m_new)
            l = alpha * l + jnp.sum(p, axis=1, keepdims=True)
            acc = alpha * acc + _dot(p.astype(BF16), v_ref[sl, :])
            return m_new, l, acc

        init = (jnp.full((t, 1), -1e30, F32), jnp.zeros((t, 1), F32), jnp.zeros((t, LANE), F32))
        below = jnp.maximum(i * r - 1, 0)
        carry = lax.fori_loop(0, below // 2, lambda j, c: step(2 * j + 1, step(2 * j, c, None), None), init)
        carry = lax.fori_loop(below // 2 * 2, below, lambda j, c: step(j, c, None), carry)
        carry = lax.cond(i > 0, lambda c: step(i * r, step(i * r - 1, c, None), 0), lambda c: step(i * r, c, 0), carry)
        for jj in range(1, r):
            carry = step(i * r + jj, carry, jj * tk)
        m, l, acc = carry
        o_ref[...] = (acc / l).astype(o_ref.dtype)
        l_ref[0] = m + jnp.log2(l)
        if ng:
            pl.when((h == heads - 1) & (i == nq - 1))(gather.finish)

    blk = pl.BlockSpec((t, LANE), lambda h, i: (i, h))
    res = pl.BlockSpec((s, LANE), lambda h, i: (0, h))
    outs = pl.pallas_call(
        body, name="mla_fwd", grid=(heads, nq), in_specs=[blk, res, res] + [ANY] * ng,
        out_specs=[blk, pl.BlockSpec((1, t, 1), lambda h, i: (h, i, 0))] + [ANY] * ng,
        out_shape=[jax.ShapeDtypeStruct((s, hw), BF16), jax.ShapeDtypeStruct((heads, s, 1), F32)]
        + [jax.ShapeDtypeStruct((N_CHIPS,) + sh.shape, sh.dtype) for sh in shards],
        scratch_shapes=_dma_sems(6 * ng) + [pltpu.SemaphoreType.DMA((ng,))] if ng else [],
        compiler_params=_attn_params(s, t, 0, 2, ride=ng > 0),
    )(q, k, v, *shards)
    return outs[0], outs[1], list(outs[2:])


def _mla_bwd(q, k, v, o, do, lse, t, tk, parts=()):
    s, hw = q.shape
    heads, nq, r = hw // LANE, s // t, t // tk
    ns = len(parts)

    def body(q_ref, k_ref, v_ref, o_ref, do_ref, l_ref, *rest):
        dq_ref, dk_ref, dv_ref = rest[ns:ns + 3]
        h, i = pl.program_id(0), pl.program_id(1)
        finish = _ride_along(_scatter_copies, ns, (rest[:ns], rest[ns + 3:2 * ns + 3], *rest[2 * ns + 3:]),
                             (h == 0) & (i == 0), (h == heads - 1) & (i == nq - 1))

        @pl.when(i == 0)
        def _():
            dk_ref[...] = jnp.zeros_like(dk_ref)
            dv_ref[...] = jnp.zeros_like(dv_ref)

        qv, dov, lv = q_ref[...], do_ref[...], l_ref[0]
        dlt = jnp.sum(dov.astype(F32) * o_ref[...].astype(F32), axis=1, keepdims=True)

        def step(j, dq, off):
            sl = pl.ds(pl.multiple_of(j * tk, tk), tk)
            kv, vv = k_ref[sl, :], v_ref[sl, :]
            p = jnp.exp2(_dot_nt(qv, kv) - lv)
            if off is not None:
                row = lax.broadcasted_iota(jnp.int32, (t, tk), 0)
                col = lax.broadcasted_iota(jnp.int32, (t, tk), 1)
                p = jnp.where(col + off <= row, p, 0.0)
            ds = (p * (_dot_nt(dov, vv) - dlt)).astype(BF16)
            dk_ref[sl, :] += _dot_tn(ds, qv) * (1.0 / LOG2E)
            dv_ref[sl, :] += _dot_tn(p.astype(BF16), dov)
            return dq + _dot(ds, kv)

        dq = lax.fori_loop(0, i * r, lambda j, c: step(j, c, None), jnp.zeros((t, LANE), F32))
        for jj in range(r):
            dq = step(i * r + jj, dq, jj * tk)
        dq_ref[...] = dq * MLA_SCALE
        finish()

    blk = pl.BlockSpec((t, LANE), lambda h, i: (i, h))
    res = pl.BlockSpec((s, LANE), lambda h, i: (0, h))
    full = jax.ShapeDtypeStruct((s, hw), F32)
    outs = pl.pallas_call(
        body, name="mla_bwd", grid=(heads, nq),
        in_specs=[blk, res, res, blk, blk, pl.BlockSpec((1, t, 1), lambda h, i: (h, i, 0))] + [ANY] * ns,
        out_specs=[blk, res, res] + [ANY] * ns, out_shape=[full, full, full] + _scatter_shapes(parts),
        scratch_shapes=_dma_sems(3 * ns) if ns else [],
        compiler_params=_attn_params(s, t, 2, 2, ride=ns > 0),
    )(q, k, v, o, do, lse, *parts)
    return outs[0], outs[1], outs[2], list(outs[3:])


def _sb_logits(qv, kv, keep, upper):
    z = _dot_nt(qv, kv)
    e = jnp.exp(-jnp.abs(z))
    l1p = jnp.log(1.0 + e)
    lb = jnp.minimum(z, 0.0) - l1p
    lo = -jnp.maximum(z, 0.0) - l1p
    if keep is not None:
        lo = jnp.where(keep, lo, 0.0)
    hi = lo.astype(BF16)
    rem = (lo - hi.astype(F32)).astype(BF16)
    suf = _dot(hi, upper) + _dot(rem, upper)
    return z, e, lb, lo, suf


def _tri(t, inclusive):
    row = lax.broadcasted_iota(jnp.int32, (t, t), 0)
    col = lax.broadcasted_iota(jnp.int32, (t, t), 1)
    return jnp.where((row >= col) if inclusive else (row > col), 1.0, 0.0).astype(BF16)


SB_QBLOCKS = 2
SB_PAIR = LANE // SB_HEAD_DIM
SB_CHAINS = SB_QBLOCKS * SB_PAIR
SB_FIRST = 2


def _sb_first_tile(b, t):
    start = jnp.maximum(b - (SB_FIRST - 1), 0) * t
    row = lax.broadcasted_iota(jnp.int32, (t, SB_FIRST * t), 0)
    col = lax.broadcasted_iota(jnp.int32, (t, SB_FIRST * t), 1)
    return pl.ds(pl.multiple_of(start, t), SB_FIRST * t), col + start < row + b * t


def _sb_head(x, hh):
    lane = lax.broadcasted_iota(jnp.int32, x.shape, 1)
    return jnp.where(lane // SB_HEAD_DIM == hh, x, jnp.zeros_like(x))


def _sb_walk(i, first, carries_of):
    n = SB_CHAINS
    carries = [first(c) for c in range(n)]
    width = len(carries[0])

    def alive(carry):
        return jnp.max(carry[0]) >= SB_ZERO_LOG

    def split(st):
        return [tuple(st[1 + c * width:1 + (c + 1) * width]) for c in range(n)]

    def live(st):
        any_alive = alive(split(st)[0])
        for cr in split(st)[1:]:
            any_alive = any_alive | alive(cr)
        return (st[0] <= SB_QBLOCKS * i) & any_alive

    def more(st):
        out = (st[0] + 1,)
        for c, cr in enumerate(split(st)):
            out += tuple(carries_of(c, st[0], cr))
        return out

    st = lax.while_loop(live, more, (jnp.int32(SB_FIRST),) + tuple(x for cr in carries for x in cr))
    jj, carries = st[0], split(st)
    for c in range(SB_PAIR, n):
        def live_c(s2, c=c):
            return (s2[0] <= SB_QBLOCKS * i + c // SB_PAIR) & alive(s2[1:])

        def more_c(s2, c=c):
            return (s2[0] + 1,) + tuple(carries_of(c, s2[0], s2[1:]))

        carries[c] = lax.while_loop(live_c, more_c, (jj,) + tuple(carries[c]))[1:]
    return carries


def _sb_fwd(proj, t):
    s = proj.shape[0]
    pairs, nq, nb = SB_HEADS // SB_PAIR, s // t, SB_QBLOCKS

    def body(q_ref, k_ref, v_ref, o_ref):
        i = pl.program_id(1)
        upper, upper_first = _tri(t, False), _tri(SB_FIRST * t, False)
        rows = [slice((c // SB_PAIR) * t, (c // SB_PAIR + 1) * t) for c in range(SB_CHAINS)]
        qs = [_sb_head(q_ref[rows[c], :] * SB_SCALE, c % SB_PAIR).astype(BF16) for c in range(SB_CHAINS)]

        def first(c):
            sl, keep = _sb_first_tile(nb * i + c // SB_PAIR, t)
            _, _, lb, lo, suf = _sb_logits(qs[c], k_ref[sl, :].astype(BF16), keep, upper_first)
            a = jnp.where(keep, jnp.exp(lb + suf), 0.0)
            return (jnp.sum(lo, axis=1, keepdims=True),
                    _dot(a.astype(BF16), _sb_head(v_ref[sl, :], c % SB_PAIR).astype(BF16)))

        def step(c, jj, carry):
            run, acc = carry
            sl = pl.ds(pl.multiple_of((nb * i + c // SB_PAIR - jj) * t, t), t)
            _, _, lb, lo, suf = _sb_logits(qs[c], k_ref[sl, :].astype(BF16), None, upper)
            a = jnp.exp(lb + suf + run)
            acc = acc + _dot(a.astype(BF16), _sb_head(v_ref[sl, :], c % SB_PAIR).astype(BF16))
            return run + jnp.sum(lo, axis=1, keepdims=True), acc

        carries = _sb_walk(i, first, step)
        for qb in range(nb):
            o_ref[qb * t:(qb + 1) * t, :] = sum(carries[qb * SB_PAIR + hh][1] for hh in range(SB_PAIR))

    return pl.pallas_call(
        body, name="sb_fwd", grid=(pairs, nq // nb),
        in_specs=[pl.BlockSpec((nb * t, LANE), lambda h, i: (i, C_SBQ // LANE + h)),
                  pl.BlockSpec((s, LANE), lambda h, i: (0, C_SBK // LANE + h)),
                  pl.BlockSpec((s, LANE), lambda h, i: (0, C_SBV // LANE + h))],
        out_specs=pl.BlockSpec((nb * t, LANE), lambda h, i: (i, h)),
        out_shape=jax.ShapeDtypeStruct((s, pairs * LANE), F32),
        compiler_params=_attn_params(s, nb * t, 0, 2),
    )(proj, proj, proj)


def _sb_bwd(proj, o, do, t, stacked=()):
    s = proj.shape[0]
    pairs, nq, nb = SB_HEADS // SB_PAIR, s // t, SB_QBLOCKS
    nx = len(stacked)

    def body(q_ref, k_ref, v_ref, o_ref, do_ref, *rest):
        dq_ref, dk_ref, dv_ref = rest[nx:nx + 3]
        hd, i = pl.program_id(0), pl.program_id(1)
        finish = _ride_along(_exchange_copies, nx, (rest[:nx], rest[nx + 3:2 * nx + 3], *rest[2 * nx + 3:]),
                             (hd == 0) & (i == 0), (hd == pairs - 1) & (i == nq // nb - 1))

        @pl.when(i == 0)
        def _():
            dk_ref[...] = jnp.zeros_like(dk_ref)
            dv_ref[...] = jnp.zeros_like(dv_ref)

        rows = [slice((c // SB_PAIR) * t, (c // SB_PAIR + 1) * t) for c in range(SB_CHAINS)]
        qs = [_sb_head(q_ref[rows[c], :] * SB_SCALE, c % SB_PAIR).astype(BF16) for c in range(SB_CHAINS)]
        dos = [_sb_head(do_ref[rows[c], :], c % SB_PAIR) for c in range(SB_CHAINS)]
        totals = [jnp.sum(dos[c].astype(F32) * o_ref[rows[c], :], axis=1, keepdims=True) for c in range(SB_CHAINS)]
        tris = {1: (_tri(t, False), _tri(t, True)), SB_FIRST: (_tri(SB_FIRST * t, False), _tri(SB_FIRST * t, True))}

        def tile(c, sl, keep, blocks, carry):
            run, g, dq = carry
            qv, dov = qs[c], dos[c]
            upper, upper_incl = tris[blocks]
            kv, vv = k_ref[sl, :].astype(BF16), v_ref[sl, :].astype(BF16)
            z, e, lb, lo, suf = _sb_logits(qv, kv, keep, upper)
            tail = suf + run
            a = jnp.exp(lb + tail)
            if keep is not None:
                a = jnp.where(keep, a, 0.0)
            ab = a.astype(BF16)
            gr = ab.astype(F32) * _dot_nt(dov, vv)
            ghi = gr.astype(BF16)
            grem = (gr - ghi.astype(F32)).astype(BF16)
            before = totals[c] - g - (_dot(ghi, upper_incl) + _dot(grem, upper_incl))
            before = jnp.where(tail < SB_ZERO_LOG, 0.0, before)
            r = 1.0 / (1.0 + e)
            pos = z >= 0.0
            dz = r * (gr * jnp.where(pos, e, 1.0) - before * jnp.where(pos, 1.0, e))
            if keep is not None:
                dz = jnp.where(keep, dz, 0.0)
            dzb = dz.astype(BF16)
            dk_ref[sl, :] += _dot_tn(dzb, qv)
            dv_ref[sl, :] += _dot_tn(ab, dov)
            return (run + jnp.sum(lo, axis=1, keepdims=True), g + jnp.sum(gr, axis=1, keepdims=True),
                    dq + _dot(dzb, _sb_head(kv, c % SB_PAIR)))

        zero = jnp.zeros((t, 1), F32)
        init = (zero, zero, jnp.zeros((t, LANE), F32))

        def first(c):
            sl, keep = _sb_first_tile(nb * i + c // SB_PAIR, t)
            return tile(c, sl, keep, SB_FIRST, init)

        def step(c, jj, carry):
            return tile(c, pl.ds(pl.multiple_of((nb * i + c // SB_PAIR - jj) * t, t), t), None, 1, carry)

        carries = _sb_walk(i, first, step)
        for qb in range(nb):
            dq_ref[qb * t:(qb + 1) * t, :] = sum(carries[qb * SB_PAIR + hh][2] for hh in range(SB_PAIR)) * SB_SCALE
        finish()

    blk = pl.BlockSpec((nb * t, LANE), lambda h, i: (i, h))
    res = pl.BlockSpec((s, LANE), lambda h, i: (0, h))
    full = jax.ShapeDtypeStruct((s, pairs * LANE), F32)
    outs = pl.pallas_call(
        body, name="sb_bwd", grid=(pairs, nq // nb),
        in_specs=[pl.BlockSpec((nb * t, LANE), lambda h, i: (i, C_SBQ // LANE + h)),
                  pl.BlockSpec((s, LANE), lambda h, i: (0, C_SBK // LANE + h)),
                  pl.BlockSpec((s, LANE), lambda h, i: (0, C_SBV // LANE + h)), blk, blk] + [ANY] * nx,
        out_specs=[blk, res, res] + [ANY] * nx, out_shape=[full, full, full] + _exchange_shapes(stacked),
        scratch_shapes=_dma_sems(nx) if nx else [],
        compiler_params=_attn_params(s, nb * t, 2, 2, ride=nx > 0),
    )(proj, proj, proj, o, do, *stacked)
    return outs[0], outs[1], outs[2], list(outs[3:])


def _xattn_probs(qh, kh):
    sc = _dot_nt(qh, kh) * (1.0 / math.sqrt(X_HEAD_DIM))
    p = jnp.exp(sc - jnp.max(sc, axis=1, keepdims=True))
    return p / jnp.sum(p, axis=1, keepdims=True)


def _xattn_fwd(xq, xkv):
    rows = xq.shape[0]
    w = X_HEADS * X_HEAD_DIM

    def body(i, ins, fulls, outs, accs):
        for h in range(X_HEADS):
            sl = slice(h * LANE, (h + 1) * LANE)
            p = _xattn_probs(ins[0][:, sl], fulls[0][:, sl])
            outs[0][:, sl] = _dot(p.astype(BF16), fulls[0][:, w + h * LANE:w + (h + 1) * LANE]).astype(BF16)

    return _rowwise(body, name="xattn_fwd", rows=rows, tr=512, row_ins=[(xq, w, 0)], full_ins=[xkv],
                    row_outs=[(w, BF16)])[0]


def _xattn_bwd(xq, xkv, dxo):
    rows = xq.shape[0]
    w = X_HEADS * X_HEAD_DIM

    def body(i, ins, fulls, outs, accs):
        _acc_init(i, accs)
        for h in range(X_HEADS):
            sl = slice(h * LANE, (h + 1) * LANE)
            slv = slice(w + h * LANE, w + (h + 1) * LANE)
            qh, kh, vh, doh = ins[0][:, sl], fulls[0][:, sl], fulls[0][:, slv], ins[1][:, sl]
            p = _xattn_probs(qh, kh)
            dp = _dot_nt(doh, vh)
            ds = (p * (dp - jnp.sum(p * dp, axis=1, keepdims=True)) * (1.0 / math.sqrt(X_HEAD_DIM))).astype(BF16)
            outs[0][:, sl] = _dot(ds, kh).astype(BF16)
            accs[0][:, sl] += _dot_tn(ds, qh)
            accs[0][:, slv] += _dot_tn(p.astype(BF16), doh)

    return _rowwise(body, name="xattn_bwd", rows=rows, tr=512, row_ins=[(xq, w, 0), (dxo, w, 0)], full_ins=[xkv],
                    row_outs=[(w, BF16)], acc_outs=[(xkv.shape, F32)])


def _gate_fwd(proj, pa, pb, b_gate):
    rows = proj.shape[0]

    def body(i, ins, fulls, outs, accs):
        sa = _sigmoid(ins[0][...].astype(F32) + fulls[0][0:1, :])
        sb = _sigmoid(ins[1][...].astype(F32) + fulls[0][1:2, :])
        outs[0][...] = (sa * ins[2][...].astype(F32) + sb * ins[3][...].astype(F32)).astype(BF16)

    return _rowwise(body, name="gate_fwd", rows=rows, tr=512,
                    row_ins=[(proj, D_MODEL, C_GA // D_MODEL), (proj, D_MODEL, C_GB // D_MODEL), (pa, D_MODEL, 0),
                             (pb, D_MODEL, 0)],
                    full_ins=[b_gate], row_outs=[(D_MODEL, BF16)])[0]


def _gate_bwd(proj, pa, pb, b_gate, dm):
    rows = proj.shape[0]

    def body(i, ins, fulls, outs, accs):
        _acc_init(i, accs)
        d = ins[4][...].astype(F32)
        sa = _sigmoid(ins[0][...].astype(F32) + fulls[0][0:1, :])
        sb = _sigmoid(ins[1][...].astype(F32) + fulls[0][1:2, :])
        dga = d * ins[2][...].astype(F32) * sa * (1.0 - sa)
        dgb = d * ins[3][...].astype(F32) * sb * (1.0 - sb)
        outs[0][...] = (d * sa).astype(BF16)
        outs[1][...] = (d * sb).astype(BF16)
        outs[2][...] = dga.astype(BF16)
        outs[3][...] = dgb.astype(BF16)
        accs[0][0:1, :] += jnp.sum(dga, axis=0, keepdims=True)
        accs[0][1:2, :] += jnp.sum(dgb, axis=0, keepdims=True)

    return _rowwise(body, name="gate_bwd", rows=rows, tr=512,
                    row_ins=[(proj, D_MODEL, C_GA // D_MODEL), (proj, D_MODEL, C_GB // D_MODEL), (pa, D_MODEL, 0),
                             (pb, D_MODEL, 0), (dm, D_MODEL, 0)],
                    full_ins=[b_gate], row_outs=[(D_MODEL, BF16)] * 4, acc_outs=[((2, D_MODEL), F32)])


def _loss_head(x3, target, g_final):
    rows = x3.shape[0]

    def body(i, ins, fulls, outs, accs):
        _acc_init(i, accs)
        xv, g = ins[0][...], fulls[0][...]
        d = _rms(xv, g) - ins[1][...]
        dx, dg = _rms_bwd(xv, g, d * (1.0 / D_MODEL))
        outs[0][...] = dx
        accs[0][...] += dg
        accs[1][...] += jnp.sum(d * d, axis=0, keepdims=True)

    return _rowwise(body, name="loss_head", rows=rows, tr=512, row_ins=[(x3, D_MODEL, 0), (target, D_MODEL, 0)],
                    full_ins=[g_final], row_outs=[(D_MODEL, F32)], acc_outs=[((1, D_MODEL), F32), ((1, D_MODEL), F32)])


def _adamw(w, g, m, v, name):
    rows, c = w.shape

    def body(i, ins, fulls, outs, accs):
        wv, gv = ins[0][...], ins[1][...]
        mn = ADAM_B1 * ins[2][...] + (1.0 - ADAM_B1) * gv
        vn = ADAM_B2 * ins[3][...] + (1.0 - ADAM_B2) * jnp.square(gv)
        m_hat = mn / (1.0 - ADAM_B1 ** ADAM_STEP)
        v_hat = vn / (1.0 - ADAM_B2 ** ADAM_STEP)
        outs[0][...] = -ADAM_LR * (m_hat / (jnp.sqrt(v_hat) + ADAM_EPS) + ADAM_WD * wv)
        outs[1][...] = mn
        outs[2][...] = vn

    return _rowwise(body, name=name, rows=rows, tr=_row_tile(rows, 256), row_ins=[(a, c, 0) for a in (w, g, m, v)],
                    row_outs=[(c, F32)] * 3)


def _place():
    x, y, c = lax.axis_index("x"), lax.axis_index("y"), lax.axis_index("c")
    chips = [(1 - x, y), (x, 1 - y), (1 - x, 1 - y)]
    return x, y, c, chips


ANY = pl.BlockSpec(memory_space=pl.ANY)


def _remote(src, dst, send_sem, recv_sem, to):
    return pltpu.make_async_remote_copy(src_ref=src, dst_ref=dst, send_sem=send_sem, recv_sem=recv_sem,
                                        device_id=to, device_id_type=MESH)


def _dma_sems(n):
    return [pltpu.SemaphoreType.DMA((n,)), pltpu.SemaphoreType.DMA((n,))]


class _GatherPhases:
    def __init__(self, ins, outs, send_sems, recv_sems, local_sems=None):
        self.ins, self.outs, self.send_sems, self.recv_sems = ins, outs, send_sems, recv_sems
        self.local_sems = local_sems
        self.x, self.y, self.c, self.chips = _place()
        self.me = 2 * self.x + self.y

    def _locals(self):
        if self.local_sems is None:
            return []
        return [pltpu.make_async_copy(self.ins[t], self.outs[t].at[self.me], self.local_sems.at[t])
                for t in range(len(self.ins))]

    def _copy(self, t, j, chip_idx, hlf, to, src=None):
        h = self.ins[t].shape[0] // 2
        dst = self.outs[t].at[chip_idx, pl.ds(hlf * h, h), :]
        return _remote(dst if src is None else src, dst, self.send_sems.at[6 * t + j], self.recv_sems.at[6 * t + j], to)

    def _sends(self):
        out = []
        for t in range(len(self.ins)):
            h = self.ins[t].shape[0] // 2
            for j, chip in enumerate(self.chips):
                out.append(self._copy(t, j, self.me, self.c, (*chip, self.c), src=self.ins[t].at[pl.ds(self.c * h, h), :]))
        return out

    def _forwards(self):
        return [self._copy(t, 3 + j, 2 * chip[0] + chip[1], self.c, (self.x, self.y, 1 - self.c))
                for t in range(len(self.ins)) for j, chip in enumerate(self.chips)]

    def send(self):
        for cp in self._sends() + self._locals():
            cp.start()

    def forward(self):
        here = (self.x, self.y, self.c)
        landed = [self._copy(t, j, 2 * chip[0] + chip[1], self.c, here)
                  for t in range(len(self.ins)) for j, chip in enumerate(self.chips)]
        for arrival, fwd in zip(landed, self._forwards()):
            arrival.wait_recv()
            fwd.start()

    def finish(self):
        here = (self.x, self.y, self.c)
        for t in range(len(self.ins)):
            for j, chip in enumerate(self.chips):
                self._copy(t, 3 + j, 2 * chip[0] + chip[1], 1 - self.c, here).wait_recv()
        for cp in self._sends() + self._forwards():
            cp.wait_send()
        for cp in self._locals():
            cp.wait()


def _all_gather_weights(shards):
    n = len(shards)

    def body(*refs):
        gather = _GatherPhases(refs[:n], refs[n:2 * n], *refs[2 * n:])
        gather.send()
        gather.forward()
        gather.finish()

    return pl.pallas_call(
        body, name="all_gather_weights", in_specs=[ANY] * n, out_specs=[ANY] * n,
        out_shape=[jax.ShapeDtypeStruct((N_CHIPS,) + s.shape, s.dtype) for s in shards],
        scratch_shapes=_dma_sems(6 * n),
    )(*shards)


def _exchange_copies(ins, outs, send_sems, recv_sems):
    x, y, c, _ = _place()
    cps = []
    for t in range(len(ins)):
        h = ins[t].shape[1] // 2
        cps.append(_remote(ins[t].at[:, pl.ds((1 - c) * h, h), :], outs[t], send_sems.at[t], recv_sems.at[t],
                           (x, y, 1 - c)))
    return cps


def _exchange_shapes(stacked):
    return [jax.ShapeDtypeStruct((N_CHIPS, s.shape[1] // 2, s.shape[2]), s.dtype) for s in stacked]


def _scatter_copies(ins, outs, send_sems, recv_sems):
    x, y, c, chips = _place()
    return [_remote(ins[t].at[2 * chip[0] + chip[1]], outs[t].at[j], send_sems.at[3 * t + j], recv_sems.at[3 * t + j],
                    (*chip, c)) for t in range(len(ins)) for j, chip in enumerate(chips)]


def _scatter_shapes(parts):
    return [jax.ShapeDtypeStruct((N_CHIPS - 1,) + s.shape[1:], s.dtype) for s in parts]


def _ride_along(copies_of, n, refs, first, last):
    if not n:
        return lambda: None

    def start():
        for cp in copies_of(*refs):
            cp.start()

    def wait():
        for cp in copies_of(*refs):
            cp.wait()

    pl.when(first)(start)
    return lambda: pl.when(last)(wait)


def _pair_exchange_halves(shards):
    n = len(shards)

    def body(*refs):
        bufs = refs[n:2 * n]
        send_sems, recv_sems = refs[2 * n:]
        x, y, c, _ = _place()
        cps = []
        for t in range(n):
            h = bufs[t].shape[0] // 2
            rows = bufs[t].at[pl.ds(c * h, h), :]
            cps.append(_remote(rows, rows, send_sems.at[t], recv_sems.at[t], (x, y, 1 - c)))
            cps[-1].start()
        for cp in cps:
            cp.wait()

    return pl.pallas_call(
        body, name="pair_exchange_halves", in_specs=[ANY] * n, out_specs=[ANY] * n,
        out_shape=[jax.ShapeDtypeStruct(s.shape, s.dtype) for s in shards],
        input_output_aliases={t: t for t in range(n)},
        scratch_shapes=_dma_sems(n),
    )(*shards)


def _pair_sum(gs, recv, place, name):
    _, r, cols = gs.shape
    h = r // 2

    def kern(p_ref, a_ref, b_ref, o_ref):
        o_ref[...] = (a_ref[...] + b_ref[...]).astype(BF16)

    blk = lambda f: pl.BlockSpec((1, h, cols), f)
    return pl.pallas_call(
        kern, name=name,
        grid_spec=pltpu.PrefetchScalarGridSpec(
            num_scalar_prefetch=1, grid=(N_CHIPS,),
            in_specs=[blk(lambda d, p: (d, p[1], 0)), blk(lambda d, p: (d, 0, 0))],
            out_specs=blk(lambda d, p: (d, 0, 0))),
        out_shape=jax.ShapeDtypeStruct((N_CHIPS, h, cols), BF16),
        compiler_params=pltpu.CompilerParams(dimension_semantics=("arbitrary",),
                                             vmem_limit_bytes=_vmem_limit(3 * _nbytes((h, cols), F32), 0)),
    )(place, gs, recv)


def _chip_sum(gs, recv, got, place, name):
    _, r, cols = gs.shape
    h = r // 2

    def kern(p_ref, a_ref, b_ref, g0, g1, g2, o_ref):
        own = a_ref[0] + b_ref[0]
        o_ref[...] = ((own + g0[0].astype(F32)) + g1[0].astype(F32)) + g2[0].astype(F32)

    blk = lambda f: pl.BlockSpec((1, h, cols), f)
    return pl.pallas_call(
        kern, name=name,
        grid_spec=pltpu.PrefetchScalarGridSpec(
            num_scalar_prefetch=1, grid=(1,),
            in_specs=[blk(lambda i, p: (p[0], p[1], 0)), blk(lambda i, p: (p[0], 0, 0)), blk(lambda i, p: (0, 0, 0)),
                      blk(lambda i, p: (1, 0, 0)), blk(lambda i, p: (2, 0, 0))],
            out_specs=pl.BlockSpec((h, cols), lambda i, p: (p[1], 0))),
        out_shape=jax.ShapeDtypeStruct((r, cols), F32),
        compiler_params=pltpu.CompilerParams(dimension_semantics=("arbitrary",),
                                             vmem_limit_bytes=_vmem_limit(5 * _nbytes((h, cols), F32), 0)),
    )(place, gs, recv, got, got, got)


def _all_reduce_small(vec, name):
    r, cols = vec.shape

    def body(in_ref, out_ref, gath, send_sems, recv_sems):
        x, y, c, _ = _place()
        me = 4 * x + 2 * y + c
        gath[me] = in_ref[...]
        sends = []
        for k in range(1, 8):
            to = (x ^ (k >> 2), y ^ ((k >> 1) & 1), c ^ (k & 1))
            cp = pltpu.make_async_remote_copy(src_ref=in_ref, dst_ref=gath.at[me], send_sem=send_sems.at[k - 1],
                                              recv_sem=recv_sems.at[k - 1], device_id=to, device_id_type=MESH)
            cp.start()
            sends.append(cp)
        for k in range(1, 8):
            peer = me ^ k
            pltpu.make_async_remote_copy(src_ref=in_ref, dst_ref=gath.at[peer], send_sem=send_sems.at[k - 1],
                                         recv_sem=recv_sems.at[k - 1], device_id=(x, y, c),
                                         device_id_type=MESH).wait_recv()
        for cp in sends:
            cp.wait_send()
        acc = gath[0]
        for d in range(1, 8):
            acc = acc + gath[d]
        out_ref[...] = acc

    vm = pl.BlockSpec(memory_space=pltpu.VMEM)
    return pl.pallas_call(
        body, name=name, in_specs=[vm], out_specs=vm,
        out_shape=jax.ShapeDtypeStruct((r, cols), F32),
        scratch_shapes=[pltpu.VMEM((8, r, cols), F32), pltpu.SemaphoreType.DMA((7,)), pltpu.SemaphoreType.DMA((7,))],
    )(vec)


def _pad_heads(w, heads, dim, axis):
    shp = w.shape[:axis] + (heads, dim) + w.shape[axis + 1:]
    pad = [(0, 0)] * len(shp)
    pad[axis + 1] = (0, LANE - dim)
    w = jnp.pad(w.reshape(shp), pad)
    return w.reshape(w.shape[:axis] + (heads * LANE,) + w.shape[axis + 2:])


def _unpad_heads(w, heads, dim, axis):
    shp = w.shape[:axis] + (heads, LANE) + w.shape[axis + 1:]
    w = lax.slice_in_dim(w.reshape(shp), 0, dim, axis=axis + 1)
    return w.reshape(w.shape[:axis] + (heads * dim,) + w.shape[axis + 2:])


def _w_in_layout(w_in):
    kr = jnp.pad(w_in[:, 384:416], ((0, 0), (ROPE_LO, LANE - ROPE_LO - MLA_ROPE)))
    sb = lambda lo: w_in[:, lo:lo + SB_WIDTH]
    return jnp.concatenate([w_in[:, 1952:2976], w_in[:, 2976:4000], sb(416), sb(928), sb(1440), w_in[:, 0:256],
                            w_in[:, 256:384], kr], axis=1)


def _w_in_unlayout(d):
    sb = lambda lo: d[:, lo:lo + SB_WIDTH]
    return jnp.concatenate([d[:, C_CQ:C_CQ + 256], d[:, C_CKV:C_CKV + 128], d[:, C_KR + ROPE_LO:C_KR + ROPE_LO + MLA_ROPE],
                            sb(C_SBQ), sb(C_SBK), sb(C_SBV), d[:, C_GA:C_GA + 1024], d[:, C_GB:C_GB + 1024]], axis=1)


def _w_ukv_layout(w):
    w3 = w.reshape(MLA_KV_RANK, MLA_HEADS, MLA_NOPE + MLA_V)
    pad = lambda part: jnp.pad(part, ((0, 0), (0, 0), (0, LANE - part.shape[2]))).reshape(MLA_KV_RANK, MLA_HEADS * LANE)
    return jnp.concatenate([pad(w3[:, :, :MLA_NOPE]), pad(w3[:, :, MLA_NOPE:])], axis=1)


def _w_ukv_unlayout(d):
    hw = MLA_HEADS * LANE
    kpart = d[:, :hw].reshape(MLA_KV_RANK, MLA_HEADS, LANE)[:, :, :MLA_NOPE]
    vpart = d[:, hw:].reshape(MLA_KV_RANK, MLA_HEADS, LANE)[:, :, :MLA_V]
    return jnp.concatenate([kpart, vpart], axis=2).reshape(MLA_KV_RANK, MLA_HEADS * (MLA_NOPE + MLA_V))


def _shard_of(full, d, axis):
    n = full.shape[axis] // N_CHIPS
    return lax.slice_in_dim(full, d * n, (d + 1) * n, axis=axis)


def _local_step(x, mem, pos, target, w, t_mla, t_sb, late=None, reduce=None):
    s = x.shape[0]
    w = dict(w)
    win = _w_in_layout(w["w_in"])
    wuq = _pad_heads(w["w_uq"], MLA_HEADS, MLA_NOPE + MLA_ROPE, 1)
    wkv = _w_ukv_layout(w["w_ukv"])
    inv_freq = ROPE_THETA ** (-jnp.arange(0, MLA_ROPE, 2, dtype=F32) / MLA_ROPE)
    freq_lane = jnp.pad(jnp.concatenate([inv_freq, inv_freq]), (ROPE_LO, LANE - ROPE_LO - MLA_ROPE)).reshape(1, LANE)
    add = lambda accs, ex: (accs[0] + ex[0],)

    def add_norm(accs, ex):
        y = accs[0] + ex[0]
        return y, _rms(y, ex[1])

    tab = _rope_tables(pos.reshape(s, 1), freq_lane)
    h = _rms_fwd_call(x, w["g_mix"], "rms_mix")
    proj = _mm(h, [win], name="proj_in", out_dtypes=(BF16,))
    cqn, ckvn, krope = _mla_prep_fwd(proj, tab, w["g_q_lat"], w["g_kv_lat"])
    hw = MLA_HEADS * LANE
    qa = _mm(cqn, [wuq], name="q_up", row_extras=(tab,), out_dtypes=(BF16,),
             epilogue=lambda accs, ex: (_per_head(lambda t: _rope(t, ex[0]) * (MLA_SCALE * LOG2E), accs[0]),))
    ka = _mm(ckvn, [wkv[:, :hw]], name="k_up", row_extras=(krope,), out_dtypes=(BF16,),
             epilogue=lambda accs, ex: (_per_head(lambda t: t + ex[0], accs[0]),))
    va = _mm(ckvn, [wkv[:, hw:]], name="v_up", out_dtypes=(BF16,))
    o_a, lse, gathered = _mla_fwd(qa, ka, va, t_mla[0], t_mla[1], late[0] if late else ())
    if late:
        w.update(late[1](gathered))
    wa = _pad_heads(w["w_a_proj"], MLA_HEADS, MLA_V, 0)
    wb = w["w_b_proj"]
    o_b = _sb_fwd(proj, t_sb)
    pa = _mm(o_a, [wa], name="proj_a", out_dtypes=(BF16,))
    pb = _mm(o_b, [wb], name="proj_b", out_dtypes=(BF16,))
    merged = _gate_fwd(proj, pa, pb, w["b_gate"])
    x1, hx = _mm(merged, [w["w_o"]], name="proj_o", extras=(x,), consts=(w["g_x"],), epilogue=add_norm,
                 out_dtypes=(F32, BF16))
    mn = _rms_fwd_call(mem, w["g_mem"], "rms_mem")
    xq = _mm(hx, [w["w_xq"]], name="xq", out_dtypes=(BF16,))
    xkv = _mm(mn, [w["w_xkv"]], name="xkv", out_dtypes=(BF16,))
    xo = _xattn_fwd(xq, xkv)
    x2, hf = _mm(xo, [w["w_xo"]], name="proj_xo", extras=(x1,), consts=(w["g_ffn"],), epilogue=add_norm,
                 out_dtypes=(F32, BF16))

    def swiglu(accs, ex):
        a, b = accs
        return a, b, a * _sigmoid(a) * b

    ga, gu, hmid = _mm(hf, [w["w_gate"], w["w_up"]], name="ffn_up", epilogue=swiglu, out_dtypes=(BF16, BF16, BF16),
                       tm=512, tn=1408)
    x3 = _mm(hmid, [w["w_down"]], name="ffn_down", extras=(x2,), epilogue=add, tk=2816)

    dx3, dg_final, sq = _loss_head(x3, target, w["g_final"].reshape(1, D_MODEL))
    g = {"g_final": dg_final.reshape(D_MODEL)}

    def swiglu_bwd(accs, ex):
        dh, a, b = accs[0], ex[0].astype(F32), ex[1].astype(F32)
        sg = _sigmoid(a)
        return dh * b * sg * (1.0 + a * (1.0 - sg)), dh * a * sg

    da, db = _mm(dx3, [w["w_down"]], name="ffn_down_dx", tb=True, extras=(ga, gu), epilogue=swiglu_bwd,
                 out_dtypes=(BF16, BF16), tm=512, tn=1408, chunk=MM_CHUNK)
    g["w_down"] = _mm(hmid, [dx3], name="ffn_down_dw", ta=True, tm=1408)
    g["w_gate"] = _mm(hf, [da], name="ffn_gate_dw", ta=True, tn=1408)
    g["w_up"] = _mm(hf, [db], name="ffn_up_dw", ta=True, tn=1408)
    dhf = _mm(da, [w["w_gate"]], name="ffn_gate_dx", tb=True, tk=2816)
    dhf = _mm(db, [w["w_up"]], name="ffn_up_dx", tb=True, extras=(dhf,), epilogue=add, tk=2816,
              out_dtypes=(BF16,))
    dx2, g["g_ffn"] = _rms_bwd_call(x2, w["g_ffn"], dhf, dx3, "rms_ffn_bwd")

    dxo = _mm(dx2, [w["w_xo"]], name="proj_xo_dx", tb=True, out_dtypes=(BF16,))
    g["w_xo"] = _mm(xo, [dx2], name="proj_xo_dw", ta=True)
    dxq, dxkv = _xattn_bwd(xq, xkv, dxo)
    dhx = _mm(dxq, [w["w_xq"]], name="xq_dx", tb=True, out_dtypes=(BF16,))
    g["w_xq"] = _mm(hx, [dxq], name="xq_dw", ta=True)
    dmn = _mm(dxkv, [w["w_xkv"]], name="xkv_dx", tb=True)
    g["w_xkv"] = _mm(mn, [dxkv], name="xkv_dw", ta=True)
    dx1, g["g_x"] = _rms_bwd_call(x1, w["g_x"], dhx, dx2, "rms_x_bwd")
    _, g["g_mem"] = _rms_bwd_call(mem, w["g_mem"], dmn, None, "rms_mem_bwd")

    dmerged = _mm(dx1, [w["w_o"]], name="proj_o_dx", tb=True, out_dtypes=(BF16,))
    g["w_o"] = _mm(merged, [dx1], name="proj_o_dw", ta=True)
    dpa, dpb, dga, dgb, g["b_gate"] = _gate_bwd(proj, pa, pb, w["b_gate"], dmerged)
    do_a = _mm(dpa, [wa], name="proj_a_dx", tb=True, out_dtypes=(BF16,))
    do_b = _mm(dpb, [wb], name="proj_b_dx", tb=True, out_dtypes=(BF16,))
    g["w_a_proj"] = _unpad_heads(_mm(o_a, [dpa], name="proj_a_dw", ta=True), MLA_HEADS, MLA_V, 0)
    g["w_b_proj"] = _mm(o_b, [dpb], name="proj_b_dw", ta=True)

    stacked = [reduce[1](n, g[n]) for n in reduce[0]] if reduce else []
    dsq, dsk, dsv, recv = _sb_bwd(proj, o_b, do_b, t_sb, stacked)
    parts = [reduce[2](n, gs, rv) for n, gs, rv in zip(reduce[0], stacked, recv)] if reduce else []
    dqa, dka, dva, got = _mla_bwd(qa, ka, va, o_a, do_a, lse, t_mla[0], t_mla[2], parts)
    riding = dict(zip(reduce[0], zip(stacked, recv, got))) if reduce else {}
    dqp, dkvp, dkr = _mla_rope_bwd(dqa, dka, dva, tab)
    g["w_uq"] = _unpad_heads(_mm(cqn, [dqp], name="q_up_dw", ta=True), MLA_HEADS, MLA_NOPE + MLA_ROPE, 1)
    g["w_ukv"] = _w_ukv_unlayout(_mm(ckvn, [dkvp], name="kv_up_dw", ta=True))
    dcqn = _mm(dqp, [wuq], name="q_up_dx", tb=True)
    dckvn = _mm(dkvp, [wkv], name="kv_up_dx", tb=True)
    dcq, dckv, g["g_q_lat"], g["g_kv_lat"] = _mla_prep_bwd(proj, w["g_q_lat"], w["g_kv_lat"], dcqn, dckvn)

    dproj = jnp.concatenate([dga, dgb, dsq.astype(BF16), dsk.astype(BF16), dsv.astype(BF16), dcq, dckv, dkr], axis=1)
    g["w_in"] = _w_in_unlayout(_mm(h, [dproj], name="proj_in_dw", ta=True))
    if not reduce:
        dh = _mm(dproj, [win], name="proj_in_dx", tb=True, tk=2048, out_dtypes=(BF16,))
        grad_x, g["g_mix"] = _rms_bwd_call(x, w["g_mix"], dh, dx1, "rms_mix_bwd")
        return sq, grad_x, g, riding
    stacked = [reduce[1](n, g[n]) for n in reduce[3]]
    dh, recv = _mm(dproj, [win], name="proj_in_dx", tb=True, tk=2048, out_dtypes=(BF16,),
                   ride=(_exchange_copies, stacked, _exchange_shapes(stacked), 1))
    parts = [reduce[2](n, gs, rv) for n, gs, rv in zip(reduce[3], stacked, recv)]
    grad_x, g["g_mix"], *got = _rms_bwd_call(x, w["g_mix"], dh, dx1, "rms_mix_bwd",
                                             ride=(_scatter_copies, parts, _scatter_shapes(parts), 3))
    riding.update(zip(reduce[3], zip(stacked, recv, got)))
    return sq, grad_x, g, riding


def _small_pack(d):
    row5 = jnp.concatenate([d["g_q_lat"].reshape(-1), d["g_kv_lat"].reshape(-1), jnp.zeros((640,), F32)])
    rows = [d[n].reshape(-1) for n in ("g_mix", "g_x", "g_mem", "g_ffn", "g_final")] + [row5]
    return rows


def _small_unpack(p, like):
    out = {n: p[i].reshape(like[n].shape) for i, n in enumerate(("g_mix", "g_x", "g_mem", "g_ffn", "g_final"))}
    out["g_q_lat"] = p[5, 0:256].reshape(like["g_q_lat"].shape)
    out["g_kv_lat"] = p[5, 256:384].reshape(like["g_kv_lat"].shape)
    return out


def kernel(x, mem, positions, g_mix, w_in, b_gate, g_q_lat, w_uq, g_kv_lat, w_ukv, w_a_proj, w_b_proj, w_o, g_x, g_mem, w_xq, w_xkv, w_xo, g_ffn, w_gate, w_up, w_down, g_final, loss_target, m_g_mix, m_w_in, m_b_gate, m_g_q_lat, m_w_uq, m_g_kv_lat, m_w_ukv, m_w_a_proj, m_w_b_proj, m_w_o, m_g_x, m_g_mem, m_w_xq, m_w_xkv, m_w_xo, m_g_ffn, m_w_gate, m_w_up, m_w_down, m_g_final, v_g_mix, v_w_in, v_b_gate, v_g_q_lat, v_w_uq, v_g_kv_lat, v_w_ukv, v_w_a_proj, v_w_b_proj, v_w_o, v_g_x, v_g_mem, v_w_xq, v_w_xkv, v_w_xo, v_g_ffn, v_w_gate, v_w_up, v_w_down, v_g_final):
    given = dict(locals())
    names = [n for n, _, _ in MATS] + ["b_gate"] + list(SMALL)
    wts = {n: given[n] for n in names}
    mom = {n: given["m_" + n] for n in names}
    var = {n: given["v_" + n] for n in names}
    shard2d = {n: shp for n, shp, _ in MATS}
    shard2d["b_gate"] = B_GATE_SHARD
    cx, cy, cc = lax.axis_index("x"), lax.axis_index("y"), lax.axis_index("c")
    me = 2 * cx + cy
    place = jnp.stack([me, cc]).astype(jnp.int32)
    bcol = me * B_GATE_SHARD[1]

    own = [wts[n].reshape(shard2d[n]).astype(BF16) for n, _, _ in MATS]
    bias = (("b_gate", (BIAS_ROWS, B_GATE_SHARD[1]), 1),)
    own_bias = [jnp.pad(wts["b_gate"].reshape(B_GATE_SHARD), ((0, BIAS_ROWS - B_GATE_SHARD[0]), (0, 0)))]

    def assemble(mats, gathered, mine=None):
        out = {}
        for k, ((n, shp, ax), g4) in enumerate(zip(mats, gathered)):
            if mine is not None:
                g4 = lax.dynamic_update_slice(g4, mine[k][None], (me, 0, 0))
            out[n] = g4.reshape(N_CHIPS * shp[0], shp[1]) if ax == 0 else jnp.concatenate(list(g4), axis=1)
        return out

    first = own[:N_EARLY] + own_bias
    full = assemble(MATS[:N_EARLY] + bias, _all_gather_weights(first), first)
    full["b_gate"] = full["b_gate"][0:B_GATE_SHARD[0]]
    late = (own[N_EARLY:], lambda gathered: assemble(MATS[N_EARLY:], gathered))
    for n in SMALL:
        full[n] = wts[n].reshape(1, -1) if n != "g_final" else wts[n]

    axis_of = {n: ax for n, _, ax in MATS}
    stack = lambda n, g: jnp.stack([_shard_of(g, d, axis_of[n]) for d in range(N_CHIPS)])
    pair_sum = lambda n, gs, rv: _pair_sum(gs, rv, place, "pair_sum_" + n)
    behind = [n for n, _, _ in MATS[N_EARLY:]]
    last = [n for n, _, _ in MATS[:N_EARLY]]
    sq, grad_x, grads, riding = _local_step(x[0], mem[0], positions[0], loss_target[0], full, t_mla=(1024, 1024, 1024), t_sb=256,
                                            late=late, reduce=(behind, stack, pair_sum, last))

    halves = [_chip_sum(*riding[n], place, "chip_sum_" + n) for n, _, _ in MATS]
    g_shard = dict(zip([n for n, _, _ in MATS], _pair_exchange_halves(halves)))

    small_rows = _small_pack({n: grads[n] for n in SMALL}) + [sq.reshape(-1), grads["b_gate"][0], grads["b_gate"][1]]
    small_rows += [jnp.zeros((D_MODEL,), F32)] * (SMALL_ROWS - len(small_rows))
    small = _all_reduce_small(jnp.stack(small_rows), "all_reduce_small")
    loss = (0.5 / D_MODEL) * jnp.sum(small[6])
    g_shard["b_gate"] = lax.dynamic_slice(small[7:9], (0, bcol), B_GATE_SHARD)

    out = {"grad": {}, "delta": {}, "m": {}, "v": {}}
    for n in [n for n, _, _ in MATS] + ["b_gate"]:
        shape = wts[n].shape
        r2 = lambda a: a.reshape(shard2d[n])
        d_n, m_n, v_n = _adamw(r2(wts[n]), g_shard[n], r2(mom[n]), r2(var[n]), "adamw_" + n)
        for key, a in (("grad", g_shard[n]), ("delta", d_n), ("m", m_n), ("v", v_n)):
            out[key][n] = a.reshape(shape)
    sp = lambda d: jnp.stack(_small_pack(d) + [jnp.zeros((D_MODEL,), F32)] * 2)
    delta_s, m_s, v_s = _adamw(sp(wts), small[0:8].at[6:8].set(0.0), sp(mom), sp(var), "adamw_small")
    for key, p in (("grad", small), ("delta", delta_s), ("m", m_s), ("v", v_s)):
        out[key].update(_small_unpack(p, wts))

    order = ["g_mix", "w_in", "b_gate", "g_q_lat", "w_uq", "g_kv_lat", "w_ukv", "w_a_proj", "w_b_proj", "w_o", "g_x",
             "g_mem", "w_xq", "w_xkv", "w_xo", "g_ffn", "w_gate", "w_up", "w_down", "g_final"]
    return (loss, grad_x[None], *[out[key][n] for key in ("grad", "delta", "m", "v") for n in order])
```

```python
import functools
import math

import jax
import jax.numpy as jnp
from jax import lax
from jax.experimental import pallas as pl
from jax.experimental.pallas import tpu as pltpu

F32 = jnp.float32
BF16 = jnp.bfloat16
MESH = pl.DeviceIdType.MESH

D_MODEL = 1024
MLA_HEADS = 8
MLA_Q_RANK = 256
MLA_KV_RANK = 128
MLA_NOPE = 64
MLA_ROPE = 32
MLA_V = 64
ROPE_THETA = 10000.0
SB_HEADS = 8
SB_HEAD_DIM = 64
X_HEADS = 4
X_HEAD_DIM = 128
EPS = 1e-6
ADAM_LR = 0.001
ADAM_B1 = 0.9
ADAM_B2 = 0.999
ADAM_EPS = 1e-08
ADAM_WD = 0.01
ADAM_STEP = 10

LANE = 128
LOG2E = 1.4426950408889634
MLA_SCALE = 1.0 / math.sqrt(MLA_NOPE + MLA_ROPE)
SB_SCALE = 1.0 / math.sqrt(SB_HEAD_DIM)
assert math.log2(SB_SCALE) == round(math.log2(SB_SCALE))
MM_CHUNK = 256
N_CHIPS = 4
VMEM_BYTES = 64 * 1024 * 1024

C_GA, C_GB, C_SBQ, C_SBK, C_SBV, C_CQ, C_CKV, C_KR = 0, 1024, 2048, 2560, 3072, 3584, 3840, 3968
SB_WIDTH = SB_HEADS * SB_HEAD_DIM
ROPE_LO = MLA_NOPE
HALF = MLA_ROPE // 2

SB_ZERO_LOG = -104.0

MATS = (
    ("w_in", (1024, 1000), 1), ("w_uq", (256, 192), 1), ("w_ukv", (128, 256), 1), ("w_a_proj", (512, 256), 1),
    ("w_b_proj", (512, 256), 1), ("w_o", (256, 1024), 0), ("w_xq", (256, 512), 0), ("w_xkv", (256, 1024), 0),
    ("w_xo", (512, 256), 1), ("w_gate", (1024, 704), 1), ("w_up", (1024, 704), 1), ("w_down", (704, 1024), 0),
)
N_EARLY = 3
B_GATE_SHARD = (2, 256)
BIAS_ROWS = 16
SMALL = ("g_mix", "g_x", "g_mem", "g_ffn", "g_final", "g_q_lat", "g_kv_lat")
SMALL_ROWS = 16


def _vmem_limit(block_bytes, temp_bytes):
    est = 2 * block_bytes + temp_bytes + (4 << 20)
    return int(min(max(est, 16 << 20), VMEM_BYTES - (6 << 20)))


def _nbytes(shape, dtype):
    return math.prod(shape) * jnp.dtype(dtype).itemsize


def _row_tile(rows, cap):
    if rows <= cap:
        return rows
    return max(t for t in range(8, cap + 1, 8) if rows % t == 0)


def _tile(n, cap):
    if n <= cap:
        return n
    best = None
    for t in range(LANE, cap + 1, LANE):
        if n % t == 0:
            best = t
    assert best is not None, (n, cap)
    return best


def _mm(a, bs, *, name, ta=False, tb=False, extras=(), row_extras=(), consts=(), epilogue=None, out_dtypes=(F32,),
        tm=1024, tn=1024, tk=1024, chunk=None, ride=None):
    bs = tuple(bs)
    m, k = (a.shape[1], a.shape[0]) if ta else a.shape
    n = bs[0].shape[0] if tb else bs[0].shape[1]
    tm, tn, tk = _tile(m, tm), _tile(n, tn), _tile(k, tk)
    assert m % tm == 0 and n % tn == 0 and k % tk == 0
    nk = k // tk
    nb, ne, no = len(bs), len(extras) + len(row_extras) + len(consts), len(out_dtypes)
    dims = (((0,) if ta else (1,)), ((1,) if tb else (0,))), ((), ())
    if epilogue is None:
        epilogue = lambda accs, ex: (accs[0],)

    rn = len(ride[1]) if ride else 0
    n_acc = nb if nk > 1 else 0
    gi, gj = m // tm, n // tn

    def body(*refs):
        a_ref, b_refs, e_refs = refs[0], refs[1:1 + nb], refs[1 + nb:1 + nb + ne]
        base = 1 + nb + ne + rn
        o_refs, acc_refs = refs[base:base + no], refs[base + no + rn:base + no + rn + n_acc]
        step = [pl.program_id(d) for d in range(3)]
        finish = _ride_along(ride[0] if ride else None, rn,
                             (refs[base - rn:base], refs[base + no:base + no + rn], *refs[base + no + rn + n_acc:]),
                             (step[0] == 0) & (step[1] == 0) & (step[2] == 0),
                             (step[0] == gi - 1) & (step[1] == gj - 1) & (step[2] == nk - 1))
        if nk == 1:
            ch = chunk or tm
            bvs = [b_ref[...].astype(BF16) for b_ref in b_refs]
            for r0 in range(0, tm, ch):
                rows = slice(r0, r0 + ch)
                av = (a_ref[:, rows] if ta else a_ref[rows, :]).astype(BF16)
                accs = [lax.dot_general(av, bv, dims, preferred_element_type=F32) for bv in bvs]
                ex = [e[rows, :] for e in e_refs[:ne - len(consts)]] + [e[...] for e in e_refs[ne - len(consts):]]
                for o_ref, v in zip(o_refs, epilogue(accs, ex)):
                    o_ref[rows, :] = v.astype(o_ref.dtype)
            finish()
            return
        kk = step[2]

        @pl.when(kk == 0)
        def _():
            for acc in acc_refs:
                acc[...] = jnp.zeros_like(acc)

        av = a_ref[...].astype(BF16)
        for b_ref, acc in zip(b_refs, acc_refs):
            acc[...] += lax.dot_general(av, b_ref[...].astype(BF16), dims, preferred_element_type=F32)

        @pl.when(kk == nk - 1)
        def _():
            outs = epilogue([acc[...] for acc in acc_refs], [e[...] for e in e_refs])
            for o_ref, v in zip(o_refs, outs):
                o_ref[...] = v.astype(o_ref.dtype)

        finish()

    a_spec = pl.BlockSpec((tk, tm), lambda i, j, kk: (kk, i)) if ta else pl.BlockSpec((tm, tk), lambda i, j, kk: (i, kk))
    b_spec = pl.BlockSpec((tn, tk), lambda i, j, kk: (j, kk)) if tb else pl.BlockSpec((tk, tn), lambda i, j, kk: (kk, j))
    mn_spec = pl.BlockSpec((tm, tn), lambda i, j, kk: (i, j))
    blocks = (_nbytes((tm, tk), a.dtype) + sum(_nbytes((tk, tn), b.dtype) for b in bs)
              + sum(_nbytes((tm, tn), e.dtype) for e in extras) + sum(_nbytes((tm, tn), d) for d in out_dtypes)
              + sum(_nbytes((tm, e.shape[1]), e.dtype) for e in row_extras))
    temps = (nb + 4) * _nbytes((tm, tn), F32)
    outs = pl.pallas_call(
        body, name=name, grid=(m // tm, n // tn, nk),
        in_specs=[a_spec] + [b_spec] * nb + [mn_spec] * len(extras)
        + [pl.BlockSpec((tm, e.shape[1]), lambda i, j, kk: (i, 0)) for e in row_extras]
        + [pl.BlockSpec(e.shape, lambda i, j, kk: (0, 0)) for e in consts] + [ANY] * rn,
        out_specs=[mn_spec] * no + [ANY] * rn,
        out_shape=[jax.ShapeDtypeStruct((m, n), d) for d in out_dtypes] + (list(ride[2]) if ride else []),
        scratch_shapes=[pltpu.VMEM((tm, tn), F32) for _ in range(n_acc)] + (_dma_sems(ride[3] * rn) if ride else []),
        compiler_params=pltpu.CompilerParams(
            dimension_semantics=("arbitrary",) * 3 if ride else ("parallel", "parallel", "arbitrary"),
            vmem_limit_bytes=_vmem_limit(blocks, temps)),
    )(a, *bs, *extras, *row_extras, *consts, *(ride[1] if ride else ()))
    if ride:
        return (outs[0] if no == 1 else outs[:no]), list(outs[no:])
    return outs[0] if no == 1 else outs


def _rowwise(body, *, name, rows, tr, row_ins, full_ins=(), row_outs=(), acc_outs=(), ride=None):
    tr = min(tr, rows)
    assert rows % tr == 0
    n_ri, n_fi, n_ro, n_ao = len(row_ins), len(full_ins), len(row_outs), len(acc_outs)
    rn = len(ride[1]) if ride else 0

    def kern(*refs):
        i = pl.program_id(0)
        n_in, n_out = n_ri + n_fi, n_ro + n_ao
        outs = refs[n_in + rn:n_in + rn + n_out]
        finish = _ride_along(ride[0] if ride else None, rn,
                             (refs[n_in:n_in + rn], refs[n_in + rn + n_out:n_in + 2 * rn + n_out],
                              *refs[n_in + 2 * rn + n_out:]), i == 0, i == rows // tr - 1)
        body(i, refs[:n_ri], refs[n_ri:n_in], outs[:n_ro], outs[n_ro:])
        finish()

    in_specs = [pl.BlockSpec((tr, w), functools.partial(lambda i, c: (i, c), c=ci)) for _, w, ci in row_ins]
    in_specs += [pl.BlockSpec(f.shape, lambda i: (0, 0)) for f in full_ins]
    in_specs += [ANY] * rn
    out_specs = [pl.BlockSpec((tr, w), lambda i: (i, 0)) for w, _ in row_outs]
    out_specs += [pl.BlockSpec(s, lambda i: (0, 0)) for s, _ in acc_outs] + [ANY] * rn
    out_shape = [jax.ShapeDtypeStruct((rows, w), d) for w, d in row_outs]
    out_shape += [jax.ShapeDtypeStruct(s, d) for s, d in acc_outs] + (list(ride[2]) if ride else [])
    blocks = (sum(_nbytes((tr, w), a.dtype) for a, w, _ in row_ins) + sum(_nbytes(f.shape, f.dtype) for f in full_ins)
              + sum(_nbytes((tr, w), d) for w, d in row_outs) + sum(_nbytes(s, d) for s, d in acc_outs))
    widest = max([w for _, w, _ in row_ins] + [w for w, _ in row_outs])
    outs = pl.pallas_call(
        kern, name=name, grid=(rows // tr,), in_specs=in_specs, out_specs=out_specs, out_shape=out_shape,
        scratch_shapes=_dma_sems(ride[3] * rn) if ride else [],
        compiler_params=pltpu.CompilerParams(
            dimension_semantics=("arbitrary",) if acc_outs or ride else ("parallel",),
            vmem_limit_bytes=_vmem_limit(blocks, 8 * _nbytes((tr, widest), F32))),
    )(*[a for a, _, _ in row_ins], *full_ins, *(ride[1] if ride else ()))
    return outs


def _rms(x, g):
    r = lax.rsqrt(jnp.mean(x * x, axis=-1, keepdims=True) + EPS)
    return x * r * g


def _rms_bwd(x, g, dy):
    r = lax.rsqrt(jnp.mean(x * x, axis=-1, keepdims=True) + EPS)
    xh = x * r
    dxh = dy * g
    dx = r * (dxh - xh * jnp.mean(dxh * xh, axis=-1, keepdims=True))
    return dx, jnp.sum(dy * xh, axis=0, keepdims=True)


def _sigmoid(x):
    return 1.0 / (1.0 + jnp.exp(-x))


def _acc_init(i, refs):
    @pl.when(i == 0)
    def _():
        for r in refs:
            r[...] = jnp.zeros_like(r)


def _rms_fwd_call(x, g, name):
    rows, c = x.shape

    def body(i, ins, fulls, outs, accs):
        outs[0][...] = _rms(ins[0][...], fulls[0][...]).astype(BF16)

    return _rowwise(body, name=name, rows=rows, tr=512, row_ins=[(x, c, 0)], full_ins=[g], row_outs=[(c, BF16)])[0]


def _rms_bwd_call(x, g, dy, res, name, ride=None):
    rows, c = x.shape
    row_ins = [(x, c, 0), (dy, c, 0)] + ([(res, c, 0)] if res is not None else [])

    def body(i, ins, fulls, outs, accs):
        _acc_init(i, accs)
        dx, dg = _rms_bwd(ins[0][...], fulls[0][...], ins[1][...].astype(F32))
        if res is not None:
            dx = dx + ins[2][...]
        outs[0][...] = dx
        accs[0][...] += dg

    return _rowwise(body, name=name, rows=rows, tr=512, row_ins=row_ins, full_ins=[g], row_outs=[(c, F32)],
                    acc_outs=[((1, c), F32)], ride=ride)


def _rope_tables(pos_col, freq_lane):
    rows = pos_col.shape[0]

    def body(i, ins, fulls, outs, accs):
        ang = ins[0][...].astype(F32) * fulls[0][...]
        lane = lax.broadcasted_iota(jnp.int32, ang.shape, 1)
        cos, sin = jnp.cos(ang), jnp.sin(ang)
        first = (lane >= ROPE_LO) & (lane < ROPE_LO + HALF)
        second = (lane >= ROPE_LO + HALF) & (lane < ROPE_LO + MLA_ROPE)
        outs[0][:, 0:LANE] = jnp.where(first | second, cos, 1.0)
        outs[0][:, LANE:2 * LANE] = jnp.where(first, -sin, 0.0)
        outs[0][:, 2 * LANE:3 * LANE] = jnp.where(second, sin, 0.0)

    return _rowwise(body, name="rope_tables", rows=rows, tr=1024, row_ins=[(pos_col, 1, 0)], full_ins=[freq_lane],
                    row_outs=[(3 * LANE, F32)])[0]


def _rope(x, tab):
    return (x * tab[:, 0:LANE] + pltpu.roll(x, LANE - HALF, 1) * tab[:, LANE:2 * LANE]
            + pltpu.roll(x, HALF, 1) * tab[:, 2 * LANE:3 * LANE])


def _rope_t(dy, tab):
    return (dy * tab[:, 0:LANE] + pltpu.roll(dy * tab[:, LANE:2 * LANE], HALF, 1)
            + pltpu.roll(dy * tab[:, 2 * LANE:3 * LANE], LANE - HALF, 1))


def _mla_prep_fwd(proj, tab, g_q, g_kv):
    rows = proj.shape[0]

    def body(i, ins, fulls, outs, accs):
        outs[0][...] = _rms(ins[0][...].astype(F32), fulls[0][...]).astype(BF16)
        outs[1][...] = _rms(ins[1][...].astype(F32), fulls[1][...]).astype(BF16)
        outs[2][...] = _rope(ins[2][...].astype(F32), ins[3][...])

    return _rowwise(body, name="mla_prep_fwd", rows=rows, tr=512,
                    row_ins=[(proj, MLA_Q_RANK, C_CQ // MLA_Q_RANK), (proj, LANE, C_CKV // LANE),
                             (proj, LANE, C_KR // LANE), (tab, 3 * LANE, 0)],
                    full_ins=[g_q, g_kv], row_outs=[(MLA_Q_RANK, BF16), (MLA_KV_RANK, BF16), (LANE, F32)])


def _mla_prep_bwd(proj, g_q, g_kv, dcqn, dckvn):
    rows = proj.shape[0]

    def body(i, ins, fulls, outs, accs):
        _acc_init(i, accs)
        dcq, dgq = _rms_bwd(ins[0][...].astype(F32), fulls[0][...], ins[2][...])
        dckv, dgkv = _rms_bwd(ins[1][...].astype(F32), fulls[1][...], ins[3][...])
        outs[0][...] = dcq.astype(BF16)
        outs[1][...] = dckv.astype(BF16)
        accs[0][...] += dgq
        accs[1][...] += dgkv

    return _rowwise(body, name="mla_prep_bwd", rows=rows, tr=512,
                    row_ins=[(proj, MLA_Q_RANK, C_CQ // MLA_Q_RANK), (proj, LANE, C_CKV // LANE),
                             (dcqn, MLA_Q_RANK, 0), (dckvn, MLA_KV_RANK, 0)],
                    full_ins=[g_q, g_kv], row_outs=[(MLA_Q_RANK, BF16), (MLA_KV_RANK, BF16)],
                    acc_outs=[((1, MLA_Q_RANK), F32), ((1, MLA_KV_RANK), F32)])


def _per_head(fn, x):
    return jnp.concatenate([fn(x[:, h * LANE:(h + 1) * LANE]) for h in range(x.shape[1] // LANE)], axis=1)


def _mla_rope_bwd(dq, dk, dv, tab):
    rows = dq.shape[0]
    hw = MLA_HEADS * LANE

    def body(i, ins, fulls, outs, accs):
        t = ins[3][...]
        dkr = jnp.zeros((ins[0].shape[0], LANE), F32)
        for h in range(MLA_HEADS):
            sl = slice(h * LANE, (h + 1) * LANE)
            outs[0][:, sl] = _rope_t(ins[0][:, sl], t).astype(BF16)
            dkr = dkr + ins[1][:, sl]
        outs[1][:, 0:hw] = ins[1][...].astype(BF16)
        outs[1][:, hw:2 * hw] = ins[2][...].astype(BF16)
        lane = lax.broadcasted_iota(jnp.int32, dkr.shape, 1)
        dkr = jnp.where((lane >= ROPE_LO) & (lane < ROPE_LO + MLA_ROPE), dkr, 0.0)
        outs[2][...] = _rope_t(dkr, t).astype(BF16)

    return _rowwise(body, name="mla_rope_bwd", rows=rows, tr=512,
                    row_ins=[(dq, hw, 0), (dk, hw, 0), (dv, hw, 0), (tab, 3 * LANE, 0)],
                    row_outs=[(hw, BF16), (2 * hw, BF16), (LANE, BF16)])


def _dot_nt(a, b):
    return lax.dot_general(a, b, (((1,), (1,)), ((), ())), preferred_element_type=F32)


def _dot_tn(a, b):
    return lax.dot_general(a, b, (((0,), (0,)), ((), ())), preferred_element_type=F32)


def _dot(a, b):
    return jnp.dot(a, b, preferred_element_type=F32)


def _attn_params(s, t, n_res_f32, n_res_bf16, ride=False):
    blocks = n_res_f32 * _nbytes((s, LANE), F32) + n_res_bf16 * _nbytes((s, LANE), BF16) + 6 * _nbytes((t, LANE), F32)
    return pltpu.CompilerParams(dimension_semantics=("arbitrary" if ride else "parallel", "arbitrary"),
                                vmem_limit_bytes=_vmem_limit(blocks, 12 * _nbytes((t, t), F32)))


def _mla_fwd(q, k, v, t, tk, shards=()):
    s, hw = q.shape
    heads, nq, r = hw // LANE, s // t, t // tk
    ng = len(shards)

    def body(q_ref, k_ref, v_ref, *rest):
        o_ref, l_ref = rest[ng], rest[ng + 1]
        h, i = pl.program_id(0), pl.program_id(1)
        if ng:
            gather = _GatherPhases(rest[:ng], rest[ng + 2:2 * ng + 2], *rest[2 * ng + 2:])
            pl.when((h == 0) & (i == 0))(gather.send)
            pl.when((h == heads // 2) & (i == 0))(gather.forward)
        qv = q_ref[...]

        def step(j, carry, off):
            m, l, acc = carry
            sl = pl.ds(pl.multiple_of(j * tk, tk), tk)
            sc = _dot_nt(qv, k_ref[sl, :])
            if off is not None:
                row = lax.broadcasted_iota(jnp.int32, (t, tk), 0)
                col = lax.broadcasted_iota(jnp.int32, (t, tk), 1)
                sc = jnp.where(col + off <= row, sc, -1e30)
            m_new = jnp.maximum(m, jnp.max(sc, axis=1, keepdims=True))
            p = jnp.exp2(sc - m_new)
            alpha = jnp.exp2(m - m_new)
            l = alpha * l + jnp.sum(p, axis=1, keepdims=True)
            acc = alpha * acc + _dot(p.astype(BF16), v_ref[sl, :])
            return m_new, l, acc

        init = (jnp.full((t, 1), -1e30, F32), jnp.zeros((t, 1), F32), jnp.zeros((t, LANE), F32))
        below = i * r
        carry = lax.fori_loop(0, below // 2, lambda j, c: step(2 * j + 1, step(2 * j, c, None), None), init)
        carry = lax.fori_loop(below // 2 * 2, below, lambda j, c: step(j, c, None), carry)
        for jj in range(r):
            carry = step(i * r + jj, carry, jj * tk)
        m, l, acc = carry
        o_ref[...] = (acc / l).astype(o_ref.dtype)
        l_ref[0] = m + jnp.log2(l)
        if ng:
            pl.when((h == heads - 1) & (i == nq - 1))(gather.finish)

    blk = pl.BlockSpec((t, LANE), lambda h, i: (i, h))
    res = pl.BlockSpec((s, LANE), lambda h, i: (0, h))
    outs = pl.pallas_call(
        body, name="mla_fwd", grid=(heads, nq), in_specs=[blk, res, res] + [ANY] * ng,
        out_specs=[blk, pl.BlockSpec((1, t, 1), lambda h, i: (h, i, 0))] + [ANY] * ng,
        out_shape=[jax.ShapeDtypeStruct((s, hw), BF16), jax.ShapeDtypeStruct((heads, s, 1), F32)]
        + [jax.ShapeDtypeStruct((N_CHIPS,) + sh.shape, sh.dtype) for sh in shards],
        scratch_shapes=_dma_sems(6 * ng) + [pltpu.SemaphoreType.DMA((ng,))] if ng else [],
        compiler_params=_attn_params(s, t, 0, 2, ride=ng > 0),
    )(q, k, v, *shards)
    return outs[0], outs[1], list(outs[2:])


def _mla_bwd(q, k, v, o, do, lse, t, tk, parts=()):
    s, hw = q.shape
    heads, nq, r = hw // LANE, s // t, t // tk
    ns = len(parts)

    def body(q_ref, k_ref, v_ref, o_ref, do_ref, l_ref, *rest):
        dq_ref, dk_ref, dv_ref = rest[ns:ns + 3]
        h, i = pl.program_id(0), pl.program_id(1)
        finish = _ride_along(_scatter_copies, ns, (rest[:ns], rest[ns + 3:2 * ns + 3], *rest[2 * ns + 3:]),
                             (h == 0) & (i == 0), (h == heads - 1) & (i == nq - 1))

        @pl.when(i == 0)
        def _():
            dk_ref[...] = jnp.zeros_like(dk_ref)
            dv_ref[...] = jnp.zeros_like(dv_ref)

        qv, dov, lv = q_ref[...], do_ref[...], l_ref[0]
        dlt = jnp.sum(dov.astype(F32) * o_ref[...].astype(F32), axis=1, keepdims=True)

        def step(j, dq, off):
            sl = pl.ds(pl.multiple_of(j * tk, tk), tk)
            kv, vv = k_ref[sl, :], v_ref[sl, :]
            p = jnp.exp2(_dot_nt(qv, kv) - lv)
            if off is not None:
                row = lax.broadcasted_iota(jnp.int32, (t, tk), 0)
                col = lax.broadcasted_iota(jnp.int32, (t, tk), 1)
                p = jnp.where(col + off <= row, p, 0.0)
            ds = (p * (_dot_nt(dov, vv) - dlt)).astype(BF16)
            dk_ref[sl, :] += _dot_tn(ds, qv) * (1.0 / LOG2E)
            dv_ref[sl, :] += _dot_tn(p.astype(BF16), dov)
            return dq + _dot(ds, kv)

        dq = lax.fori_loop(0, i * r, lambda j, c: step(j, c, None), jnp.zeros((t, LANE), F32))
        for jj in range(r):
            dq = step(i * r + jj, dq, jj * tk)
        dq_ref[...] = dq * MLA_SCALE
        finish()

    blk = pl.BlockSpec((t, LANE), lambda h, i: (i, h))
    res = pl.BlockSpec((s, LANE), lambda h, i: (0, h))
    full = jax.ShapeDtypeStruct((s, hw), F32)
    outs = pl.pallas_call(
        body, name="mla_bwd", grid=(heads, nq),
        in_specs=[blk, res, res, blk, blk, pl.BlockSpec((1, t, 1), lambda h, i: (h, i, 0))] + [ANY] * ns,
        out_specs=[blk, res, res] + [ANY] * ns, out_shape=[full, full, full] + _scatter_shapes(parts),
        scratch_shapes=_dma_sems(3 * ns) if ns else [],
        compiler_params=_attn_params(s, t, 2, 2, ride=ns > 0),
    )(q, k, v, o, do, lse, *parts)
    return outs[0], outs[1], outs[2], list(outs[3:])


def _sb_logits(qv, kv, keep, upper):
    z = _dot_nt(qv, kv)
    e = jnp.exp(-jnp.abs(z))
    l1p = jnp.log(1.0 + e)
    lb = jnp.minimum(z, 0.0) - l1p
    lo = -jnp.maximum(z, 0.0) - l1p
    if keep is not None:
        lo = jnp.where(keep, lo, 0.0)
    hi = lo.astype(BF16)
    rem = (lo - hi.astype(F32)).astype(BF16)
    suf = _dot(hi, upper) + _dot(rem, upper)
    return z, e, lb, lo, suf


def _tri(t, inclusive):
    row = lax.broadcasted_iota(jnp.int32, (t, t), 0)
    col = lax.broadcasted_iota(jnp.int32, (t, t), 1)
    return jnp.where((row >= col) if inclusive else (row > col), 1.0, 0.0).astype(BF16)


SB_QBLOCKS = 2
SB_PAIR = LANE // SB_HEAD_DIM
SB_CHAINS = SB_QBLOCKS * SB_PAIR
SB_FIRST = 2


def _sb_first_tile(b, t):
    start = jnp.maximum(b - (SB_FIRST - 1), 0) * t
    row = lax.broadcasted_iota(jnp.int32, (t, SB_FIRST * t), 0)
    col = lax.broadcasted_iota(jnp.int32, (t, SB_FIRST * t), 1)
    return pl.ds(pl.multiple_of(start, t), SB_FIRST * t), col + start < row + b * t


def _sb_head(x, hh):
    lane = lax.broadcasted_iota(jnp.int32, x.shape, 1)
    return jnp.where(lane // SB_HEAD_DIM == hh, x, jnp.zeros_like(x))


def _sb_walk(i, first, carries_of):
    n = SB_CHAINS
    carries = [first(c) for c in range(n)]
    width = len(carries[0])

    def alive(carry):
        return jnp.max(carry[0]) >= SB_ZERO_LOG

    def split(st):
        return [tuple(st[1 + c * width:1 + (c + 1) * width]) for c in range(n)]

    def live(st):
        any_alive = alive(split(st)[0])
        for cr in split(st)[1:]:
            any_alive = any_alive | alive(cr)
        return (st[0] <= SB_QBLOCKS * i) & any_alive

    def more(st):
        out = (st[0] + 1,)
        for c, cr in enumerate(split(st)):
            out += tuple(carries_of(c, st[0], cr))
        return out

    st = lax.while_loop(live, more, (jnp.int32(SB_FIRST),) + tuple(x for cr in carries for x in cr))
    jj, carries = st[0], split(st)
    for c in range(SB_PAIR, n):
        def live_c(s2, c=c):
            return (s2[0] <= SB_QBLOCKS * i + c // SB_PAIR) & alive(s2[1:])

        def more_c(s2, c=c):
            return (s2[0] + 1,) + tuple(carries_of(c, s2[0], s2[1:]))

        carries[c] = lax.while_loop(live_c, more_c, (jj,) + tuple(carries[c]))[1:]
    return carries


def _sb_fwd(proj, t):
    s = proj.shape[0]
    pairs, nq, nb = SB_HEADS // SB_PAIR, s // t, SB_QBLOCKS

    def body(q_ref, k_ref, v_ref, o_ref):
        i = pl.program_id(1)
        upper, upper_first = _tri(t, False), _tri(SB_FIRST * t, False)
        rows = [slice((c // SB_PAIR) * t, (c // SB_PAIR + 1) * t) for c in range(SB_CHAINS)]
        qs = [_sb_head(q_ref[rows[c], :] * SB_SCALE, c % SB_PAIR).astype(BF16) for c in range(SB_CHAINS)]

        def first(c):
            sl, keep = _sb_first_tile(nb * i + c // SB_PAIR, t)
            _, _, lb, lo, suf = _sb_logits(qs[c], k_ref[sl, :].astype(BF16), keep, upper_first)
            a = jnp.where(keep, jnp.exp(lb + suf), 0.0)
            return (jnp.sum(lo, axis=1, keepdims=True),
                    _dot(a.astype(BF16), _sb_head(v_ref[sl, :], c % SB_PAIR).astype(BF16)))

        def step(c, jj, carry):
            run, acc = carry
            sl = pl.ds(pl.multiple_of((nb * i + c // SB_PAIR - jj) * t, t), t)
            _, _, lb, lo, suf = _sb_logits(qs[c], k_ref[sl, :].astype(BF16), None, upper)
            a = jnp.exp(lb + suf + run)
            acc = acc + _dot(a.astype(BF16), _sb_head(v_ref[sl, :], c % SB_PAIR).astype(BF16))
            return run + jnp.sum(lo, axis=1, keepdims=True), acc

        carries = _sb_walk(i, first, step)
        for qb in range(nb):
            o_ref[qb * t:(qb + 1) * t, :] = sum(carries[qb * SB_PAIR + hh][1] for hh in range(SB_PAIR))

    return pl.pallas_call(
        body, name="sb_fwd", grid=(pairs, nq // nb),
        in_specs=[pl.BlockSpec((nb * t, LANE), lambda h, i: (i, C_SBQ // LANE + h)),
                  pl.BlockSpec((s, LANE), lambda h, i: (0, C_SBK // LANE + h)),
                  pl.BlockSpec((s, LANE), lambda h, i: (0, C_SBV // LANE + h))],
        out_specs=pl.BlockSpec((nb * t, LANE), lambda h, i: (i, h)),
        out_shape=jax.ShapeDtypeStruct((s, pairs * LANE), F32),
        compiler_params=_attn_params(s, nb * t, 0, 2),
    )(proj, proj, proj)


def _sb_bwd(proj, o, do, t, stacked=()):
    s = proj.shape[0]
    pairs, nq, nb = SB_HEADS // SB_PAIR, s // t, SB_QBLOCKS
    nx = len(stacked)

    def body(q_ref, k_ref, v_ref, o_ref, do_ref, *rest):
        dq_ref, dk_ref, dv_ref = rest[nx:nx + 3]
        hd, i = pl.program_id(0), pl.program_id(1)
        finish = _ride_along(_exchange_copies, nx, (rest[:nx], rest[nx + 3:2 * nx + 3], *rest[2 * nx + 3:]),
                             (hd == 0) & (i == 0), (hd == pairs - 1) & (i == nq // nb - 1))

        @pl.when(i == 0)
        def _():
            dk_ref[...] = jnp.zeros_like(dk_ref)
            dv_ref[...] = jnp.zeros_like(dv_ref)

        rows = [slice((c // SB_PAIR) * t, (c // SB_PAIR + 1) * t) for c in range(SB_CHAINS)]
        qs = [_sb_head(q_ref[rows[c], :] * SB_SCALE, c % SB_PAIR).astype(BF16) for c in range(SB_CHAINS)]
        dos = [_sb_head(do_ref[rows[c], :], c % SB_PAIR) for c in range(SB_CHAINS)]
        totals = [jnp.sum(dos[c].astype(F32) * o_ref[rows[c], :], axis=1, keepdims=True) for c in range(SB_CHAINS)]
        tris = {1: (_tri(t, False), _tri(t, True)), SB_FIRST: (_tri(SB_FIRST * t, False), _tri(SB_FIRST * t, True))}

        def tile(c, sl, keep, blocks, carry):
            run, g, dq = carry
            qv, dov = qs[c], dos[c]
            upper, upper_incl = tris[blocks]
            kv, vv = k_ref[sl, :].astype(BF16), v_ref[sl, :].astype(BF16)
            z, e, lb, lo, suf = _sb_logits(qv, kv, keep, upper)
            tail = suf + run
            a = jnp.exp(lb + tail)
            if keep is not None:
                a = jnp.where(keep, a, 0.0)
            ab = a.astype(BF16)
            gr = ab.astype(F32) * _dot_nt(dov, vv)
            ghi = gr.astype(BF16)
            grem = (gr - ghi.astype(F32)).astype(BF16)
            before = totals[c] - g - (_dot(ghi, upper_incl) + _dot(grem, upper_incl))
            before = jnp.where(tail < SB_ZERO_LOG, 0.0, before)
            r = 1.0 / (1.0 + e)
            pos = z >= 0.0
            dz = r * (gr * jnp.where(pos, e, 1.0) - before * jnp.where(pos, 1.0, e))
            if keep is not None:
                dz = jnp.where(keep, dz, 0.0)
            dzb = dz.astype(BF16)
            dk_ref[sl, :] += _dot_tn(dzb, qv)
            dv_ref[sl, :] += _dot_tn(ab, dov)
            return (run + jnp.sum(lo, axis=1, keepdims=True), g + jnp.sum(gr, axis=1, keepdims=True),
                    dq + _dot(dzb, _sb_head(kv, c % SB_PAIR)))

        zero = jnp.zeros((t, 1), F32)
        init = (zero, zero, jnp.zeros((t, LANE), F32))

        def first(c):
            sl, keep = _sb_first_tile(nb * i + c // SB_PAIR, t)
            return tile(c, sl, keep, SB_FIRST, init)

        def step(c, jj, carry):
            return tile(c, pl.ds(pl.multiple_of((nb * i + c // SB_PAIR - jj) * t, t), t), None, 1, carry)

        carries = _sb_walk(i, first, step)
        for qb in range(nb):
            dq_ref[qb * t:(qb + 1) * t, :] = sum(carries[qb * SB_PAIR + hh][2] for hh in range(SB_PAIR)) * SB_SCALE
        finish()

    blk = pl.BlockSpec((nb * t, LANE), lambda h, i: (i, h))
    res = pl.BlockSpec((s, LANE), lambda h, i: (0, h))
    full = jax.ShapeDtypeStruct((s, pairs * LANE), F32)
    outs = pl.pallas_call(
        body, name="sb_bwd", grid=(pairs, nq // nb),
        in_specs=[pl.BlockSpec((nb * t, LANE), lambda h, i: (i, C_SBQ // LANE + h)),
                  pl.BlockSpec((s, LANE), lambda h, i: (0, C_SBK // LANE + h)),
                  pl.BlockSpec((s, LANE), lambda h, i: (0, C_SBV // LANE + h)), blk, blk] + [ANY] * nx,
        out_specs=[blk, res, res] + [ANY] * nx, out_shape=[full, full, full] + _exchange_shapes(stacked),
        scratch_shapes=_dma_sems(nx) if nx else [],
        compiler_params=_attn_params(s, nb * t, 2, 2, ride=nx > 0),
    )(proj, proj, proj, o, do, *stacked)
    return outs[0], outs[1], outs[2], list(outs[3:])


def _xattn_probs(qh, kh):
    sc = _dot_nt(qh, kh) * (1.0 / math.sqrt(X_HEAD_DIM))
    p = jnp.exp(sc - jnp.max(sc, axis=1, keepdims=True))
    return p / jnp.sum(p, axis=1, keepdims=True)


def _xattn_fwd(xq, xkv):
    rows = xq.shape[0]
    w = X_HEADS * X_HEAD_DIM

    def body(i, ins, fulls, outs, accs):
        for h in range(X_HEADS):
            sl = slice(h * LANE, (h + 1) * LANE)
            p = _xattn_probs(ins[0][:, sl], fulls[0][:, sl])
            outs[0][:, sl] = _dot(p.astype(BF16), fulls[0][:, w + h * LANE:w + (h + 1) * LANE]).astype(BF16)

    return _rowwise(body, name="xattn_fwd", rows=rows, tr=512, row_ins=[(xq, w, 0)], full_ins=[xkv],
                    row_outs=[(w, BF16)])[0]


def _xattn_bwd(xq, xkv, dxo):
    rows = xq.shape[0]
    w = X_HEADS * X_HEAD_DIM

    def body(i, ins, fulls, outs, accs):
        _acc_init(i, accs)
        for h in range(X_HEADS):
            sl = slice(h * LANE, (h + 1) * LANE)
            slv = slice(w + h * LANE, w + (h + 1) * LANE)
            qh, kh, vh, doh = ins[0][:, sl], fulls[0][:, sl], fulls[0][:, slv], ins[1][:, sl]
            p = _xattn_probs(qh, kh)
            dp = _dot_nt(doh, vh)
            ds = (p * (dp - jnp.sum(p * dp, axis=1, keepdims=True)) * (1.0 / math.sqrt(X_HEAD_DIM))).astype(BF16)
            outs[0][:, sl] = _dot(ds, kh).astype(BF16)
            accs[0][:, sl] += _dot_tn(ds, qh)
            accs[0][:, slv] += _dot_tn(p.astype(BF16), doh)

    return _rowwise(body, name="xattn_bwd", rows=rows, tr=512, row_ins=[(xq, w, 0), (dxo, w, 0)], full_ins=[xkv],
                    row_outs=[(w, BF16)], acc_outs=[(xkv.shape, F32)])


def _gate_fwd(proj, pa, pb, b_gate):
    rows = proj.shape[0]

    def body(i, ins, fulls, outs, accs):
        sa = _sigmoid(ins[0][...].astype(F32) + fulls[0][0:1, :])
        sb = _sigmoid(ins[1][...].astype(F32) + fulls[0][1:2, :])
        outs[0][...] = (sa * ins[2][...].astype(F32) + sb * ins[3][...].astype(F32)).astype(BF16)

    return _rowwise(body, name="gate_fwd", rows=rows, tr=512,
                    row_ins=[(proj, D_MODEL, C_GA // D_MODEL), (proj, D_MODEL, C_GB // D_MODEL), (pa, D_MODEL, 0),
                             (pb, D_MODEL, 0)],
                    full_ins=[b_gate], row_outs=[(D_MODEL, BF16)])[0]


def _gate_bwd(proj, pa, pb, b_gate, dm):
    rows = proj.shape[0]

    def body(i, ins, fulls, outs, accs):
        _acc_init(i, accs)
        d = ins[4][...].astype(F32)
        sa = _sigmoid(ins[0][...].astype(F32) + fulls[0][0:1, :])
        sb = _sigmoid(ins[1][...].astype(F32) + fulls[0][1:2, :])
        dga = d * ins[2][...].astype(F32) * sa * (1.0 - sa)
        dgb = d * ins[3][...].astype(F32) * sb * (1.0 - sb)
        outs[0][...] = (d * sa).astype(BF16)
        outs[1][...] = (d * sb).astype(BF16)
        outs[2][...] = dga.astype(BF16)
        outs[3][...] = dgb.astype(BF16)
        accs[0][0:1, :] += jnp.sum(dga, axis=0, keepdims=True)
        accs[0][1:2, :] += jnp.sum(dgb, axis=0, keepdims=True)

    return _rowwise(body, name="gate_bwd", rows=rows, tr=512,
                    row_ins=[(proj, D_MODEL, C_GA // D_MODEL), (proj, D_MODEL, C_GB // D_MODEL), (pa, D_MODEL, 0),
                             (pb, D_MODEL, 0), (dm, D_MODEL, 0)],
                    full_ins=[b_gate], row_outs=[(D_MODEL, BF16)] * 4, acc_outs=[((2, D_MODEL), F32)])


def _loss_head(x3, target, g_final):
    rows = x3.shape[0]

    def body(i, ins, fulls, outs, accs):
        _acc_init(i, accs)
        xv, g = ins[0][...], fulls[0][...]
        d = _rms(xv, g) - ins[1][...]
        dx, dg = _rms_bwd(xv, g, d * (1.0 / D_MODEL))
        outs[0][...] = dx
        accs[0][...] += dg
        accs[1][...] += jnp.sum(d * d, axis=0, keepdims=True)

    return _rowwise(body, name="loss_head", rows=rows, tr=512, row_ins=[(x3, D_MODEL, 0), (target, D_MODEL, 0)],
                    full_ins=[g_final], row_outs=[(D_MODEL, F32)], acc_outs=[((1, D_MODEL), F32), ((1, D_MODEL), F32)])


def _adamw(w, g, m, v, name):
    rows, c = w.shape

    def body(i, ins, fulls, outs, accs):
        wv, gv = ins[0][...], ins[1][...]
        mn = ADAM_B1 * ins[2][...] + (1.0 - ADAM_B1) * gv
        vn = ADAM_B2 * ins[3][...] + (1.0 - ADAM_B2) * jnp.square(gv)
        m_hat = mn / (1.0 - ADAM_B1 ** ADAM_STEP)
        v_hat = vn / (1.0 - ADAM_B2 ** ADAM_STEP)
        outs[0][...] = -ADAM_LR * (m_hat / (jnp.sqrt(v_hat) + ADAM_EPS) + ADAM_WD * wv)
        outs[1][...] = mn
        outs[2][...] = vn

    return _rowwise(body, name=name, rows=rows, tr=_row_tile(rows, 256), row_ins=[(a, c, 0) for a in (w, g, m, v)],
                    row_outs=[(c, F32)] * 3)


def _place():
    x, y, c = lax.axis_index("x"), lax.axis_index("y"), lax.axis_index("c")
    chips = [(1 - x, y), (x, 1 - y), (1 - x, 1 - y)]
    return x, y, c, chips


ANY = pl.BlockSpec(memory_space=pl.ANY)


def _remote(src, dst, send_sem, recv_sem, to):
    return pltpu.make_async_remote_copy(src_ref=src, dst_ref=dst, send_sem=send_sem, recv_sem=recv_sem,
                                        device_id=to, device_id_type=MESH)


def _dma_sems(n):
    return [pltpu.SemaphoreType.DMA((n,)), pltpu.SemaphoreType.DMA((n,))]


class _GatherPhases:
    def __init__(self, ins, outs, send_sems, recv_sems, local_sems=None):
        self.ins, self.outs, self.send_sems, self.recv_sems = ins, outs, send_sems, recv_sems
        self.local_sems = local_sems
        self.x, self.y, self.c, self.chips = _place()
        self.me = 2 * self.x + self.y

    def _locals(self):
        if self.local_sems is None:
            return []
        return [pltpu.make_async_copy(self.ins[t], self.outs[t].at[self.me], self.local_sems.at[t])
                for t in range(len(self.ins))]

    def _copy(self, t, j, chip_idx, hlf, to, src=None):
        h = self.ins[t].shape[0] // 2
        dst = self.outs[t].at[chip_idx, pl.ds(hlf * h, h), :]
        return _remote(dst if src is None else src, dst, self.send_sems.at[6 * t + j], self.recv_sems.at[6 * t + j], to)

    def _sends(self):
        out = []
        for t in range(len(self.ins)):
            h = self.ins[t].shape[0] // 2
            for j, chip in enumerate(self.chips):
                out.append(self._copy(t, j, self.me, self.c, (*chip, self.c), src=self.ins[t].at[pl.ds(self.c * h, h), :]))
        return out

    def _forwards(self):
        return [self._copy(t, 3 + j, 2 * chip[0] + chip[1], self.c, (self.x, self.y, 1 - self.c))
                for t in range(len(self.ins)) for j, chip in enumerate(self.chips)]

    def send(self):
        for cp in self._sends() + self._locals():
            cp.start()

    def forward(self):
        here = (self.x, self.y, self.c)
        landed = [self._copy(t, j, 2 * chip[0] + chip[1], self.c, here)
                  for t in range(len(self.ins)) for j, chip in enumerate(self.chips)]
        for arrival, fwd in zip(landed, self._forwards()):
            arrival.wait_recv()
            fwd.start()

    def finish(self):
        here = (self.x, self.y, self.c)
        for t in range(len(self.ins)):
            for j, chip in enumerate(self.chips):
                self._copy(t, 3 + j, 2 * chip[0] + chip[1], 1 - self.c, here).wait_recv()
        for cp in self._sends() + self._forwards():
            cp.wait_send()
        for cp in self._locals():
            cp.wait()


def _all_gather_weights(shards):
    n = len(shards)

    def body(*refs):
        gather = _GatherPhases(refs[:n], refs[n:2 * n], *refs[2 * n:])
        gather.send()
        gather.forward()
        gather.finish()

    return pl.pallas_call(
        body, name="all_gather_weights", in_specs=[ANY] * n, out_specs=[ANY] * n,
        out_shape=[jax.ShapeDtypeStruct((N_CHIPS,) + s.shape, s.dtype) for s in shards],
        scratch_shapes=_dma_sems(6 * n),
    )(*shards)


def _exchange_copies(ins, outs, send_sems, recv_sems):
    x, y, c, _ = _place()
    cps = []
    for t in range(len(ins)):
        h = ins[t].shape[1] // 2
        cps.append(_remote(ins[t].at[:, pl.ds((1 - c) * h, h), :], outs[t], send_sems.at[t], recv_sems.at[t],
                           (x, y, 1 - c)))
    return cps


def _exchange_shapes(stacked):
    return [jax.ShapeDtypeStruct((N_CHIPS, s.shape[1] // 2, s.shape[2]), s.dtype) for s in stacked]


def _scatter_copies(ins, outs, send_sems, recv_sems):
    x, y, c, chips = _place()
    return [_remote(ins[t].at[2 * chip[0] + chip[1]], outs[t].at[j], send_sems.at[3 * t + j], recv_sems.at[3 * t + j],
                    (*chip, c)) for t in range(len(ins)) for j, chip in enumerate(chips)]


def _scatter_shapes(parts):
    return [jax.ShapeDtypeStruct((N_CHIPS - 1,) + s.shape[1:], s.dtype) for s in parts]


def _ride_along(copies_of, n, refs, first, last):
    if not n:
        return lambda: None

    def start():
        for cp in copies_of(*refs):
            cp.start()

    def wait():
        for cp in copies_of(*refs):
            cp.wait()

    pl.when(first)(start)
    return lambda: pl.when(last)(wait)


def _pair_exchange_halves(shards):
    n = len(shards)

    def body(*refs):
        bufs = refs[n:2 * n]
        send_sems, recv_sems = refs[2 * n:]
        x, y, c, _ = _place()
        cps = []
        for t in range(n):
            h = bufs[t].shape[0] // 2
            rows = bufs[t].at[pl.ds(c * h, h), :]
            cps.append(_remote(rows, rows, send_sems.at[t], recv_sems.at[t], (x, y, 1 - c)))
            cps[-1].start()
        for cp in cps:
            cp.wait()

    return pl.pallas_call(
        body, name="pair_exchange_halves", in_specs=[ANY] * n, out_specs=[ANY] * n,
        out_shape=[jax.ShapeDtypeStruct(s.shape, s.dtype) for s in shards],
        input_output_aliases={t: t for t in range(n)},
        scratch_shapes=_dma_sems(n),
    )(*shards)


def _pair_sum(gs, recv, place, name):
    _, r, cols = gs.shape
    h = r // 2

    def kern(p_ref, a_ref, b_ref, o_ref):
        o_ref[...] = (a_ref[...] + b_ref[...]).astype(BF16)

    blk = lambda f: pl.BlockSpec((1, h, cols), f)
    return pl.pallas_call(
        kern, name=name,
        grid_spec=pltpu.PrefetchScalarGridSpec(
            num_scalar_prefetch=1, grid=(N_CHIPS,),
            in_specs=[blk(lambda d, p: (d, p[1], 0)), blk(lambda d, p: (d, 0, 0))],
            out_specs=blk(lambda d, p: (d, 0, 0))),
        out_shape=jax.ShapeDtypeStruct((N_CHIPS, h, cols), BF16),
        compiler_params=pltpu.CompilerParams(dimension_semantics=("arbitrary",),
                                             vmem_limit_bytes=_vmem_limit(3 * _nbytes((h, cols), F32), 0)),
    )(place, gs, recv)


def _chip_sum(gs, recv, got, place, name):
    _, r, cols = gs.shape
    h = r // 2

    def kern(p_ref, a_ref, b_ref, g0, g1, g2, o_ref):
        own = a_ref[0] + b_ref[0]
        o_ref[...] = ((own + g0[0].astype(F32)) + g1[0].astype(F32)) + g2[0].astype(F32)

    blk = lambda f: pl.BlockSpec((1, h, cols), f)
    return pl.pallas_call(
        kern, name=name,
        grid_spec=pltpu.PrefetchScalarGridSpec(
            num_scalar_prefetch=1, grid=(1,),
            in_specs=[blk(lambda i, p: (p[0], p[1], 0)), blk(lambda i, p: (p[0], 0, 0)), blk(lambda i, p: (0, 0, 0)),
                      blk(lambda i, p: (1, 0, 0)), blk(lambda i, p: (2, 0, 0))],
            out_specs=pl.BlockSpec((h, cols), lambda i, p: (p[1], 0))),
        out_shape=jax.ShapeDtypeStruct((r, cols), F32),
        compiler_params=pltpu.CompilerParams(dimension_semantics=("arbitrary",),
                                             vmem_limit_bytes=_vmem_limit(5 * _nbytes((h, cols), F32), 0)),
    )(place, gs, recv, got, got, got)


def _all_reduce_small(vec, name):
    r, cols = vec.shape

    def body(in_ref, out_ref, gath, send_sems, recv_sems):
        x, y, c, _ = _place()
        me = 4 * x + 2 * y + c
        gath[me] = in_ref[...]
        sends = []
        for k in range(1, 8):
            to = (x ^ (k >> 2), y ^ ((k >> 1) & 1), c ^ (k & 1))
            cp = pltpu.make_async_remote_copy(src_ref=in_ref, dst_ref=gath.at[me], send_sem=send_sems.at[k - 1],
                                              recv_sem=recv_sems.at[k - 1], device_id=to, device_id_type=MESH)
            cp.start()
            sends.append(cp)
        for k in range(1, 8):
            peer = me ^ k
            pltpu.make_async_remote_copy(src_ref=in_ref, dst_ref=gath.at[peer], send_sem=send_sems.at[k - 1],
                                         recv_sem=recv_sems.at[k - 1], device_id=(x, y, c),
                                         device_id_type=MESH).wait_recv()
        for cp in sends:
            cp.wait_send()
        acc = gath[0]
        for d in range(1, 8):
            acc = acc + gath[d]
        out_ref[...] = acc

    vm = pl.BlockSpec(memory_space=pltpu.VMEM)
    return pl.pallas_call(
        body, name=name, in_specs=[vm], out_specs=vm,
        out_shape=jax.ShapeDtypeStruct((r, cols), F32),
        scratch_shapes=[pltpu.VMEM((8, r, cols), F32), pltpu.SemaphoreType.DMA((7,)), pltpu.SemaphoreType.DMA((7,))],
    )(vec)


def _pad_heads(w, heads, dim, axis):
    shp = w.shape[:axis] + (heads, dim) + w.shape[axis + 1:]
    pad = [(0, 0)] * len(shp)
    pad[axis + 1] = (0, LANE - dim)
    w = jnp.pad(w.reshape(shp), pad)
    return w.reshape(w.shape[:axis] + (heads * LANE,) + w.shape[axis + 2:])


def _unpad_heads(w, heads, dim, axis):
    shp = w.shape[:axis] + (heads, LANE) + w.shape[axis + 1:]
    w = lax.slice_in_dim(w.reshape(shp), 0, dim, axis=axis + 1)
    return w.reshape(w.shape[:axis] + (heads * dim,) + w.shape[axis + 2:])


def _w_in_layout(w_in):
    kr = jnp.pad(w_in[:, 384:416], ((0, 0), (ROPE_LO, LANE - ROPE_LO - MLA_ROPE)))
    sb = lambda lo: w_in[:, lo:lo + SB_WIDTH]
    return jnp.concatenate([w_in[:, 1952:2976], w_in[:, 2976:4000], sb(416), sb(928), sb(1440), w_in[:, 0:256],
                            w_in[:, 256:384], kr], axis=1)


def _w_in_unlayout(d):
    sb = lambda lo: d[:, lo:lo + SB_WIDTH]
    return jnp.concatenate([d[:, C_CQ:C_CQ + 256], d[:, C_CKV:C_CKV + 128], d[:, C_KR + ROPE_LO:C_KR + ROPE_LO + MLA_ROPE],
                            sb(C_SBQ), sb(C_SBK), sb(C_SBV), d[:, C_GA:C_GA + 1024], d[:, C_GB:C_GB + 1024]], axis=1)


def _w_ukv_layout(w):
    w3 = w.reshape(MLA_KV_RANK, MLA_HEADS, MLA_NOPE + MLA_V)
    pad = lambda part: jnp.pad(part, ((0, 0), (0, 0), (0, LANE - part.shape[2]))).reshape(MLA_KV_RANK, MLA_HEADS * LANE)
    return jnp.concatenate([pad(w3[:, :, :MLA_NOPE]), pad(w3[:, :, MLA_NOPE:])], axis=1)


def _w_ukv_unlayout(d):
    hw = MLA_HEADS * LANE
    kpart = d[:, :hw].reshape(MLA_KV_RANK, MLA_HEADS, LANE)[:, :, :MLA_NOPE]
    vpart = d[:, hw:].reshape(MLA_KV_RANK, MLA_HEADS, LANE)[:, :, :MLA_V]
    return jnp.concatenate([kpart, vpart], axis=2).reshape(MLA_KV_RANK, MLA_HEADS * (MLA_NOPE + MLA_V))


def _shard_of(full, d, axis):
    n = full.shape[axis] // N_CHIPS
    return lax.slice_in_dim(full, d * n, (d + 1) * n, axis=axis)


def _local_step(x, mem, pos, target, w, t_mla, t_sb, late=None, reduce=None):
    s = x.shape[0]
    w = dict(w)
    win = _w_in_layout(w["w_in"])
    wuq = _pad_heads(w["w_uq"], MLA_HEADS, MLA_NOPE + MLA_ROPE, 1)
    wkv = _w_ukv_layout(w["w_ukv"])
    inv_freq = ROPE_THETA ** (-jnp.arange(0, MLA_ROPE, 2, dtype=F32) / MLA_ROPE)
    freq_lane = jnp.pad(jnp.concatenate([inv_freq, inv_freq]), (ROPE_LO, LANE - ROPE_LO - MLA_ROPE)).reshape(1, LANE)
    add = lambda accs, ex: (accs[0] + ex[0],)

    def add_norm(accs, ex):
        y = accs[0] + ex[0]
        return y, _rms(y, ex[1])

    tab = _rope_tables(pos.reshape(s, 1), freq_lane)
    h = _rms_fwd_call(x, w["g_mix"], "rms_mix")
    proj = _mm(h, [win], name="proj_in", out_dtypes=(BF16,))
    cqn, ckvn, krope = _mla_prep_fwd(proj, tab, w["g_q_lat"], w["g_kv_lat"])
    hw = MLA_HEADS * LANE
    qa = _mm(cqn, [wuq], name="q_up", row_extras=(tab,), out_dtypes=(BF16,),
             epilogue=lambda accs, ex: (_per_head(lambda t: _rope(t, ex[0]) * (MLA_SCALE * LOG2E), accs[0]),))
    ka = _mm(ckvn, [wkv[:, :hw]], name="k_up", row_extras=(krope,), out_dtypes=(BF16,),
             epilogue=lambda accs, ex: (_per_head(lambda t: t + ex[0], accs[0]),))
    va = _mm(ckvn, [wkv[:, hw:]], name="v_up", out_dtypes=(BF16,))
    o_a, lse, gathered = _mla_fwd(qa, ka, va, t_mla[0], t_mla[1], late[0] if late else ())
    if late:
        w.update(late[1](gathered))
    wa = _pad_heads(w["w_a_proj"], MLA_HEADS, MLA_V, 0)
    wb = w["w_b_proj"]
    o_b = _sb_fwd(proj, t_sb)
    pa = _mm(o_a, [wa], name="proj_a", out_dtypes=(BF16,))
    pb = _mm(o_b, [wb], name="proj_b", out_dtypes=(BF16,))
    merged = _gate_fwd(proj, pa, pb, w["b_gate"])
    x1, hx = _mm(merged, [w["w_o"]], name="proj_o", extras=(x,), consts=(w["g_x"],), epilogue=add_norm,
                 out_dtypes=(F32, BF16))
    mn = _rms_fwd_call(mem, w["g_mem"], "rms_mem")
    xq = _mm(hx, [w["w_xq"]], name="xq", out_dtypes=(BF16,))
    xkv = _mm(mn, [w["w_xkv"]], name="xkv", out_dtypes=(BF16,))
    xo = _xattn_fwd(xq, xkv)
    x2, hf = _mm(xo, [w["w_xo"]], name="proj_xo", extras=(x1,), consts=(w["g_ffn"],), epilogue=add_norm,
                 out_dtypes=(F32, BF16))

    def swiglu(accs, ex):
        a, b = accs
        return a, b, a * _sigmoid(a) * b

    ga, gu, hmid = _mm(hf, [w["w_gate"], w["w_up"]], name="ffn_up", epilogue=swiglu, out_dtypes=(BF16, BF16, BF16),
                       tm=1024, tn=1408, chunk=MM_CHUNK)
    x3 = _mm(hmid, [w["w_down"]], name="ffn_down", extras=(x2,), epilogue=add, tk=2816)

    dx3, dg_final, sq = _loss_head(x3, target, w["g_final"].reshape(1, D_MODEL))
    g = {"g_final": dg_final.reshape(D_MODEL)}

    def swiglu_bwd(accs, ex):
        dh, a, b = accs[0], ex[0].astype(F32), ex[1].astype(F32)
        sg = _sigmoid(a)
        return dh * b * sg * (1.0 + a * (1.0 - sg)), dh * a * sg

    da, db = _mm(dx3, [w["w_down"]], name="ffn_down_dx", tb=True, extras=(ga, gu), epilogue=swiglu_bwd,
                 out_dtypes=(BF16, BF16), tm=1024, tn=1408, chunk=MM_CHUNK)
    g["w_down"] = _mm(hmid, [dx3], name="ffn_down_dw", ta=True, tm=1408)
    g["w_gate"] = _mm(hf, [da], name="ffn_gate_dw", ta=True, tn=1408)
    g["w_up"] = _mm(hf, [db], name="ffn_up_dw", ta=True, tn=1408)
    dhf = _mm(da, [w["w_gate"]], name="ffn_gate_dx", tb=True, tk=2816)
    dhf = _mm(db, [w["w_up"]], name="ffn_up_dx", tb=True, extras=(dhf,), epilogue=add, tk=2816,
              out_dtypes=(BF16,))
    dx2, g["g_ffn"] = _rms_bwd_call(x2, w["g_ffn"], dhf, dx3, "rms_ffn_bwd")

    dxo = _mm(dx2, [w["w_xo"]], name="proj_xo_dx", tb=True, out_dtypes=(BF16,))
    g["w_xo"] = _mm(xo, [dx2], name="proj_xo_dw", ta=True)
    dxq, dxkv = _xattn_bwd(xq, xkv, dxo)
    dhx = _mm(dxq, [w["w_xq"]], name="xq_dx", tb=True, out_dtypes=(BF16,))
    g["w_xq"] = _mm(hx, [dxq], name="xq_dw", ta=True)
    dmn = _mm(dxkv, [w["w_xkv"]], name="xkv_dx", tb=True)
    g["w_xkv"] = _mm(mn, [dxkv], name="xkv_dw", ta=True)
    dx1, g["g_x"] = _rms_bwd_call(x1, w["g_x"], dhx, dx2, "rms_x_bwd")
    _, g["g_mem"] = _rms_bwd_call(mem, w["g_mem"], dmn, None, "rms_mem_bwd")

    dmerged = _mm(dx1, [w["w_o"]], name="proj_o_dx", tb=True, out_dtypes=(BF16,))
    g["w_o"] = _mm(merged, [dx1], name="proj_o_dw", ta=True)
    dpa, dpb, dga, dgb, g["b_gate"] = _gate_bwd(proj, pa, pb, w["b_gate"], dmerged)
    do_a = _mm(dpa, [wa], name="proj_a_dx", tb=True, out_dtypes=(BF16,))
    do_b = _mm(dpb, [wb], name="proj_b_dx", tb=True, out_dtypes=(BF16,))
    g["w_a_proj"] = _unpad_heads(_mm(o_a, [dpa], name="proj_a_dw", ta=True), MLA_HEADS, MLA_V, 0)
    g["w_b_proj"] = _mm(o_b, [dpb], name="proj_b_dw", ta=True)

    stacked = [reduce[1](n, g[n]) for n in reduce[0]] if reduce else []
    dsq, dsk, dsv, recv = _sb_bwd(proj, o_b, do_b, t_sb, stacked)
    parts = [reduce[2](n, gs, rv) for n, gs, rv in zip(reduce[0], stacked, recv)] if reduce else []
    dqa, dka, dva, got = _mla_bwd(qa, ka, va, o_a, do_a, lse, t_mla[0], t_mla[2], parts)
    riding = dict(zip(reduce[0], zip(stacked, recv, got))) if reduce else {}
    dqp, dkvp, dkr = _mla_rope_bwd(dqa, dka, dva, tab)
    g["w_uq"] = _unpad_heads(_mm(cqn, [dqp], name="q_up_dw", ta=True), MLA_HEADS, MLA_NOPE + MLA_ROPE, 1)
    g["w_ukv"] = _w_ukv_unlayout(_mm(ckvn, [dkvp], name="kv_up_dw", ta=True))
    dcqn = _mm(dqp, [wuq], name="q_up_dx", tb=True)
    dckvn = _mm(dkvp, [wkv], name="kv_up_dx", tb=True)
    dcq, dckv, g["g_q_lat"], g["g_kv_lat"] = _mla_prep_bwd(proj, w["g_q_lat"], w["g_kv_lat"], dcqn, dckvn)

    dproj = jnp.concatenate([dga, dgb, dsq.astype(BF16), dsk.astype(BF16), dsv.astype(BF16), dcq, dckv, dkr], axis=1)
    g["w_in"] = _w_in_unlayout(_mm(h, [dproj], name="proj_in_dw", ta=True))
    if not reduce:
        dh = _mm(dproj, [win], name="proj_in_dx", tb=True, tk=2048, out_dtypes=(BF16,))
        grad_x, g["g_mix"] = _rms_bwd_call(x, w["g_mix"], dh, dx1, "rms_mix_bwd")
        return sq, grad_x, g, riding
    stacked = [reduce[1](n, g[n]) for n in reduce[3]]
    dh, recv = _mm(dproj, [win], name="proj_in_dx", tb=True, tk=2048, out_dtypes=(BF16,),
                   ride=(_exchange_copies, stacked, _exchange_shapes(stacked), 1))
    parts = [reduce[2](n, gs, rv) for n, gs, rv in zip(reduce[3], stacked, recv)]
    grad_x, g["g_mix"], *got = _rms_bwd_call(x, w["g_mix"], dh, dx1, "rms_mix_bwd",
                                             ride=(_scatter_copies, parts, _scatter_shapes(parts), 3))
    riding.update(zip(reduce[3], zip(stacked, recv, got)))
    return sq, grad_x, g, riding


def _small_pack(d):
    row5 = jnp.concatenate([d["g_q_lat"].reshape(-1), d["g_kv_lat"].reshape(-1), jnp.zeros((640,), F32)])
    rows = [d[n].reshape(-1) for n in ("g_mix", "g_x", "g_mem", "g_ffn", "g_final")] + [row5]
    return rows


def _small_unpack(p, like):
    out = {n: p[i].reshape(like[n].shape) for i, n in enumerate(("g_mix", "g_x", "g_mem", "g_ffn", "g_final"))}
    out["g_q_lat"] = p[5, 0:256].reshape(like["g_q_lat"].shape)
    out["g_kv_lat"] = p[5, 256:384].reshape(like["g_kv_lat"].shape)
    return out


def kernel(x, mem, positions, g_mix, w_in, b_gate, g_q_lat, w_uq, g_kv_lat, w_ukv, w_a_proj, w_b_proj, w_o, g_x, g_mem, w_xq, w_xkv, w_xo, g_ffn, w_gate, w_up, w_down, g_final, loss_target, m_g_mix, m_w_in, m_b_gate, m_g_q_lat, m_w_uq, m_g_kv_lat, m_w_ukv, m_w_a_proj, m_w_b_proj, m_w_o, m_g_x, m_g_mem, m_w_xq, m_w_xkv, m_w_xo, m_g_ffn, m_w_gate, m_w_up, m_w_down, m_g_final, v_g_mix, v_w_in, v_b_gate, v_g_q_lat, v_w_uq, v_g_kv_lat, v_w_ukv, v_w_a_proj, v_w_b_proj, v_w_o, v_g_x, v_g_mem, v_w_xq, v_w_xkv, v_w_xo, v_g_ffn, v_w_gate, v_w_up, v_w_down, v_g_final):
    given = dict(locals())
    names = [n for n, _, _ in MATS] + ["b_gate"] + list(SMALL)
    wts = {n: given[n] for n in names}
    mom = {n: given["m_" + n] for n in names}
    var = {n: given["v_" + n] for n in names}
    shard2d = {n: shp for n, shp, _ in MATS}
    shard2d["b_gate"] = B_GATE_SHARD
    cx, cy, cc = lax.axis_index("x"), lax.axis_index("y"), lax.axis_index("c")
    me = 2 * cx + cy
    place = jnp.stack([me, cc]).astype(jnp.int32)
    bcol = me * B_GATE_SHARD[1]

    own = [wts[n].reshape(shard2d[n]).astype(BF16) for n, _, _ in MATS]
    bias = (("b_gate", (BIAS_ROWS, B_GATE_SHARD[1]), 1),)
    own_bias = [jnp.pad(wts["b_gate"].reshape(B_GATE_SHARD), ((0, BIAS_ROWS - B_GATE_SHARD[0]), (0, 0)))]

    def assemble(mats, gathered, mine=None):
        out = {}
        for k, ((n, shp, ax), g4) in enumerate(zip(mats, gathered)):
            if mine is not None:
                g4 = lax.dynamic_update_slice(g4, mine[k][None], (me, 0, 0))
            out[n] = g4.reshape(N_CHIPS * shp[0], shp[1]) if ax == 0 else jnp.concatenate(list(g4), axis=1)
        return out

    first = own[:N_EARLY] + own_bias
    full = assemble(MATS[:N_EARLY] + bias, _all_gather_weights(first), first)
    full["b_gate"] = full["b_gate"][0:B_GATE_SHARD[0]]
    late = (own[N_EARLY:], lambda gathered: assemble(MATS[N_EARLY:], gathered))
    for n in SMALL:
        full[n] = wts[n].reshape(1, -1) if n != "g_final" else wts[n]

    axis_of = {n: ax for n, _, ax in MATS}
    stack = lambda n, g: jnp.stack([_shard_of(g, d, axis_of[n]) for d in range(N_CHIPS)])
    pair_sum = lambda n, gs, rv: _pair_sum(gs, rv, place, "pair_sum_" + n)
    behind = [n for n, _, _ in MATS[N_EARLY:]]
    last = [n for n, _, _ in MATS[:N_EARLY]]
    sq, grad_x, grads, riding = _local_step(x[0], mem[0], positions[0], loss_target[0], full, t_mla=(1024, 1024, 1024), t_sb=256,
                                            late=late, reduce=(behind, stack, pair_sum, last))

    halves = [_chip_sum(*riding[n], place, "chip_sum_" + n) for n, _, _ in MATS]
    g_shard = dict(zip([n for n, _, _ in MATS], _pair_exchange_halves(halves)))

    small_rows = _small_pack({n: grads[n] for n in SMALL}) + [sq.reshape(-1), grads["b_gate"][0], grads["b_gate"][1]]
    small_rows += [jnp.zeros((D_MODEL,), F32)] * (SMALL_ROWS - len(small_rows))
    small = _all_reduce_small(jnp.stack(small_rows), "all_reduce_small")
    loss = (0.5 / D_MODEL) * jnp.sum(small[6])
    g_shard["b_gate"] = lax.dynamic_slice(small[7:9], (0, bcol), B_GATE_SHARD)

    out = {"grad": {}, "delta": {}, "m": {}, "v": {}}
    for n in [n for n, _, _ in MATS] + ["b_gate"]:
        shape = wts[n].shape
        r2 = lambda a: a.reshape(shard2d[n])
        d_n, m_n, v_n = _adamw(r2(wts[n]), g_shard[n], r2(mom[n]), r2(var[n]), "adamw_" + n)
        for key, a in (("grad", g_shard[n]), ("delta", d_n), ("m", m_n), ("v", v_n)):
            out[key][n] = a.reshape(shape)
    sp = lambda d: jnp.stack(_small_pack(d) + [jnp.zeros((D_MODEL,), F32)] * 2)
    delta_s, m_s, v_s = _adamw(sp(wts), small[0:8].at[6:8].set(0.0), sp(mom), sp(var), "adamw_small")
    for key, p in (("grad", small), ("delta", delta_s), ("m", m_s), ("v", v_s)):
        out[key].update(_small_unpack(p, wts))

    order = ["g_mix", "w_in", "b_gate", "g_q_lat", "w_uq", "g_kv_lat", "w_ukv", "w_a_proj", "w_b_proj", "w_o", "g_x",
             "g_mem", "w_xq", "w_xkv", "w_xo", "g_ffn", "w_gate", "w_up", "w_down", "g_final"]
    return (loss, grad_x[None], *[out[key][n] for key in ("grad", "delta", "m", "v") for n in order])
```

```python
import functools
import math

import jax
import jax.numpy as jnp
from jax import lax
from jax.experimental import pallas as pl
from jax.experimental.pallas import tpu as pltpu

F32 = jnp.float32
BF16 = jnp.bfloat16
MESH = pl.DeviceIdType.MESH

D_MODEL = 1024
MLA_HEADS = 8
MLA_Q_RANK = 256
MLA_KV_RANK = 128
MLA_NOPE = 64
MLA_ROPE = 32
MLA_V = 64
ROPE_THETA = 10000.0
SB_HEADS = 8
SB_HEAD_DIM = 64
X_HEADS = 4
X_HEAD_DIM = 128
EPS = 1e-6
ADAM_LR = 0.001
ADAM_B1 = 0.9
ADAM_B2 = 0.999
ADAM_EPS = 1e-08
ADAM_WD = 0.01
ADAM_STEP = 10

LANE = 128
LOG2E = 1.4426950408889634
MLA_SCALE = 1.0 / math.sqrt(MLA_NOPE + MLA_ROPE)
SB_SCALE = 1.0 / math.sqrt(SB_HEAD_DIM)
assert math.log2(SB_SCALE) == round(math.log2(SB_SCALE))
MM_CHUNK = 256
N_CHIPS = 4
VMEM_BYTES = 64 * 1024 * 1024

C_GA, C_GB, C_SBQ, C_SBK, C_SBV, C_CQ, C_CKV, C_KR = 0, 1024, 2048, 2560, 3072, 3584, 3840, 3968
SB_WIDTH = SB_HEADS * SB_HEAD_DIM
ROPE_LO = MLA_NOPE
HALF = MLA_ROPE // 2

SB_ZERO_LOG = -104.0

MATS = (
    ("w_in", (1024, 1000), 1), ("w_uq", (256, 192), 1), ("w_ukv", (128, 256), 1), ("w_a_proj", (512, 256), 1),
    ("w_b_proj", (512, 256), 1), ("w_o", (256, 1024), 0), ("w_xq", (256, 512), 0), ("w_xkv", (256, 1024), 0),
    ("w_xo", (512, 256), 1), ("w_gate", (1024, 704), 1), ("w_up", (1024, 704), 1), ("w_down", (704, 1024), 0),
)
N_EARLY = 3
B_GATE_SHARD = (2, 256)
BIAS_ROWS = 16
SMALL = ("g_mix", "g_x", "g_mem", "g_ffn", "g_final", "g_q_lat", "g_kv_lat")
SMALL_ROWS = 16


def _vmem_limit(block_bytes, temp_bytes):
    est = 2 * block_bytes + temp_bytes + (4 << 20)
    return int(min(max(est, 16 << 20), VMEM_BYTES - (6 << 20)))


def _nbytes(shape, dtype):
    return math.prod(shape) * jnp.dtype(dtype).itemsize


def _row_tile(rows, cap):
    if rows <= cap:
        return rows
    return max(t for t in range(8, cap + 1, 8) if rows % t == 0)


def _tile(n, cap):
    if n <= cap:
        return n
    best = None
    for t in range(LANE, cap + 1, LANE):
        if n % t == 0:
            best = t
    assert best is not None, (n, cap)
    return best


def _mm(a, bs, *, name, ta=False, tb=False, extras=(), row_extras=(), consts=(), epilogue=None, out_dtypes=(F32,),
        tm=1024, tn=1024, tk=1024, chunk=None, ride=None):
    bs = tuple(bs)
    m, k = (a.shape[1], a.shape[0]) if ta else a.shape
    n = bs[0].shape[0] if tb else bs[0].shape[1]
    tm, tn, tk = _tile(m, tm), _tile(n, tn), _tile(k, tk)
    assert m % tm == 0 and n % tn == 0 and k % tk == 0
    nk = k // tk
    nb, ne, no = len(bs), len(extras) + len(row_extras) + len(consts), len(out_dtypes)
    dims = (((0,) if ta else (1,)), ((1,) if tb else (0,))), ((), ())
    if epilogue is None:
        epilogue = lambda accs, ex: (accs[0],)

    rn, rn_out = (len(ride[1]), len(ride[2])) if ride else (0, 0)
    n_acc = nb if nk > 1 else 0
    gi, gj = m // tm, n // tn

    def body(*refs):
        a_ref, b_refs, e_refs = refs[0], refs[1:1 + nb], refs[1 + nb:1 + nb + ne]
        base = 1 + nb + ne + rn
        o_refs, acc_refs = refs[base:base + no], refs[base + no + rn_out:base + no + rn_out + n_acc]
        step = [pl.program_id(d) for d in range(3)]
        finish = _ride_along(ride[0] if ride else None, rn,
                             (refs[base - rn:base], refs[base + no:base + no + rn_out],
                              *refs[base + no + rn_out + n_acc:]),
                             (step[0] == 0) & (step[1] == 0) & (step[2] == 0),
                             (step[0] == gi - 1) & (step[1] == gj - 1) & (step[2] == nk - 1))
        if nk == 1:
            ch = chunk or tm
            bvs = [b_ref[...].astype(BF16) for b_ref in b_refs]
            for r0 in range(0, tm, ch):
                rows = slice(r0, r0 + ch)
                av = (a_ref[:, rows] if ta else a_ref[rows, :]).astype(BF16)
                accs = [lax.dot_general(av, bv, dims, preferred_element_type=F32) for bv in bvs]
                ex = [e[rows, :] for e in e_refs[:ne - len(consts)]] + [e[...] for e in e_refs[ne - len(consts):]]
                for o_ref, v in zip(o_refs, epilogue(accs, ex)):
                    o_ref[rows, :] = v.astype(o_ref.dtype)
            finish()
            return
        kk = step[2]

        @pl.when(kk == 0)
        def _():
            for acc in acc_refs:
                acc[...] = jnp.zeros_like(acc)

        av = a_ref[...].astype(BF16)
        for b_ref, acc in zip(b_refs, acc_refs):
            acc[...] += lax.dot_general(av, b_ref[...].astype(BF16), dims, preferred_element_type=F32)

        @pl.when(kk == nk - 1)
        def _():
            outs = epilogue([acc[...] for acc in acc_refs], [e[...] for e in e_refs])
            for o_ref, v in zip(o_refs, outs):
                o_ref[...] = v.astype(o_ref.dtype)

        finish()

    a_spec = pl.BlockSpec((tk, tm), lambda i, j, kk: (kk, i)) if ta else pl.BlockSpec((tm, tk), lambda i, j, kk: (i, kk))
    b_spec = pl.BlockSpec((tn, tk), lambda i, j, kk: (j, kk)) if tb else pl.BlockSpec((tk, tn), lambda i, j, kk: (kk, j))
    mn_spec = pl.BlockSpec((tm, tn), lambda i, j, kk: (i, j))
    blocks = (_nbytes((tm, tk), a.dtype) + sum(_nbytes((tk, tn), b.dtype) for b in bs)
              + sum(_nbytes((tm, tn), e.dtype) for e in extras) + sum(_nbytes((tm, tn), d) for d in out_dtypes)
              + sum(_nbytes((tm, e.shape[1]), e.dtype) for e in row_extras))
    temps = (nb + 4) * _nbytes((tm, tn), F32)
    outs = pl.pallas_call(
        body, name=name, grid=(m // tm, n // tn, nk),
        in_specs=[a_spec] + [b_spec] * nb + [mn_spec] * len(extras)
        + [pl.BlockSpec((tm, e.shape[1]), lambda i, j, kk: (i, 0)) for e in row_extras]
        + [pl.BlockSpec(e.shape, lambda i, j, kk: (0, 0)) for e in consts] + [ANY] * rn,
        out_specs=[mn_spec] * no + [ANY] * rn_out,
        out_shape=[jax.ShapeDtypeStruct((m, n), d) for d in out_dtypes] + (list(ride[2]) if ride else []),
        scratch_shapes=[pltpu.VMEM((tm, tn), F32) for _ in range(n_acc)] + (_dma_sems(ride[3]) if ride else []),
        compiler_params=pltpu.CompilerParams(
            dimension_semantics=("arbitrary",) * 3 if ride else ("parallel", "parallel", "arbitrary"),
            vmem_limit_bytes=_vmem_limit(blocks, temps)),
    )(a, *bs, *extras, *row_extras, *consts, *(ride[1] if ride else ()))
    if ride:
        return (outs[0] if no == 1 else outs[:no]), list(outs[no:])
    return outs[0] if no == 1 else outs


def _rowwise(body, *, name, rows, tr, row_ins, full_ins=(), row_outs=(), acc_outs=()):
    tr = min(tr, rows)
    assert rows % tr == 0
    n_ri, n_fi, n_ro = len(row_ins), len(full_ins), len(row_outs)

    def kern(*refs):
        body(pl.program_id(0), refs[:n_ri], refs[n_ri:n_ri + n_fi], refs[n_ri + n_fi:n_ri + n_fi + n_ro],
             refs[n_ri + n_fi + n_ro:])

    in_specs = [pl.BlockSpec((tr, w), functools.partial(lambda i, c: (i, c), c=ci)) for _, w, ci in row_ins]
    in_specs += [pl.BlockSpec(f.shape, lambda i: (0, 0)) for f in full_ins]
    out_specs = [pl.BlockSpec((tr, w), lambda i: (i, 0)) for w, _ in row_outs]
    out_specs += [pl.BlockSpec(s, lambda i: (0, 0)) for s, _ in acc_outs]
    out_shape = [jax.ShapeDtypeStruct((rows, w), d) for w, d in row_outs]
    out_shape += [jax.ShapeDtypeStruct(s, d) for s, d in acc_outs]
    blocks = (sum(_nbytes((tr, w), a.dtype) for a, w, _ in row_ins) + sum(_nbytes(f.shape, f.dtype) for f in full_ins)
              + sum(_nbytes((tr, w), d) for w, d in row_outs) + sum(_nbytes(s, d) for s, d in acc_outs))
    widest = max([w for _, w, _ in row_ins] + [w for w, _ in row_outs])
    outs = pl.pallas_call(
        kern, name=name, grid=(rows // tr,), in_specs=in_specs, out_specs=out_specs, out_shape=out_shape,
        compiler_params=pltpu.CompilerParams(
            dimension_semantics=("arbitrary",) if acc_outs else ("parallel",),
            vmem_limit_bytes=_vmem_limit(blocks, 8 * _nbytes((tr, widest), F32))),
    )(*[a for a, _, _ in row_ins], *full_ins)
    return outs


def _rms(x, g):
    r = lax.rsqrt(jnp.mean(x * x, axis=-1, keepdims=True) + EPS)
    return x * r * g


def _rms_bwd(x, g, dy):
    r = lax.rsqrt(jnp.mean(x * x, axis=-1, keepdims=True) + EPS)
    xh = x * r
    dxh = dy * g
    dx = r * (dxh - xh * jnp.mean(dxh * xh, axis=-1, keepdims=True))
    return dx, jnp.sum(dy * xh, axis=0, keepdims=True)


def _sigmoid(x):
    return 1.0 / (1.0 + jnp.exp(-x))


def _acc_init(i, refs):
    @pl.when(i == 0)
    def _():
        for r in refs:
            r[...] = jnp.zeros_like(r)


def _rms_fwd_call(x, g, name):
    rows, c = x.shape

    def body(i, ins, fulls, outs, accs):
        outs[0][...] = _rms(ins[0][...], fulls[0][...]).astype(BF16)

    return _rowwise(body, name=name, rows=rows, tr=512, row_ins=[(x, c, 0)], full_ins=[g], row_outs=[(c, BF16)])[0]


def _rms_bwd_call(x, g, dy, res, name):
    rows, c = x.shape
    row_ins = [(x, c, 0), (dy, c, 0)] + ([(res, c, 0)] if res is not None else [])

    def body(i, ins, fulls, outs, accs):
        _acc_init(i, accs)
        dx, dg = _rms_bwd(ins[0][...], fulls[0][...], ins[1][...].astype(F32))
        if res is not None:
            dx = dx + ins[2][...]
        outs[0][...] = dx
        accs[0][...] += dg

    return _rowwise(body, name=name, rows=rows, tr=512, row_ins=row_ins, full_ins=[g], row_outs=[(c, F32)],
                    acc_outs=[((1, c), F32)])


def _rope_tables(pos_col, freq_lane):
    rows = pos_col.shape[0]

    def body(i, ins, fulls, outs, accs):
        ang = ins[0][...].astype(F32) * fulls[0][...]
        lane = lax.broadcasted_iota(jnp.int32, ang.shape, 1)
        cos, sin = jnp.cos(ang), jnp.sin(ang)
        first = (lane >= ROPE_LO) & (lane < ROPE_LO + HALF)
        second = (lane >= ROPE_LO + HALF) & (lane < ROPE_LO + MLA_ROPE)
        outs[0][:, 0:LANE] = jnp.where(first | second, cos, 1.0)
        outs[0][:, LANE:2 * LANE] = jnp.where(first, -sin, 0.0)
        outs[0][:, 2 * LANE:3 * LANE] = jnp.where(second, sin, 0.0)

    return _rowwise(body, name="rope_tables", rows=rows, tr=1024, row_ins=[(pos_col, 1, 0)], full_ins=[freq_lane],
                    row_outs=[(3 * LANE, F32)])[0]


def _rope(x, tab):
    return (x * tab[:, 0:LANE] + pltpu.roll(x, LANE - HALF, 1) * tab[:, LANE:2 * LANE]
            + pltpu.roll(x, HALF, 1) * tab[:, 2 * LANE:3 * LANE])


def _rope_t(dy, tab):
    return (dy * tab[:, 0:LANE] + pltpu.roll(dy * tab[:, LANE:2 * LANE], HALF, 1)
            + pltpu.roll(dy * tab[:, 2 * LANE:3 * LANE], LANE - HALF, 1))


def _mla_prep_fwd(proj, tab, g_q, g_kv):
    rows = proj.shape[0]

    def body(i, ins, fulls, outs, accs):
        outs[0][...] = _rms(ins[0][...].astype(F32), fulls[0][...]).astype(BF16)
        outs[1][...] = _rms(ins[1][...].astype(F32), fulls[1][...]).astype(BF16)
        outs[2][...] = _rope(ins[2][...].astype(F32), ins[3][...])

    return _rowwise(body, name="mla_prep_fwd", rows=rows, tr=512,
                    row_ins=[(proj, MLA_Q_RANK, C_CQ // MLA_Q_RANK), (proj, LANE, C_CKV // LANE),
                             (proj, LANE, C_KR // LANE), (tab, 3 * LANE, 0)],
                    full_ins=[g_q, g_kv], row_outs=[(MLA_Q_RANK, BF16), (MLA_KV_RANK, BF16), (LANE, F32)])


def _mla_prep_bwd(proj, g_q, g_kv, dcqn, dckvn):
    rows = proj.shape[0]

    def body(i, ins, fulls, outs, accs):
        _acc_init(i, accs)
        dcq, dgq = _rms_bwd(ins[0][...].astype(F32), fulls[0][...], ins[2][...])
        dckv, dgkv = _rms_bwd(ins[1][...].astype(F32), fulls[1][...], ins[3][...])
        outs[0][...] = dcq.astype(BF16)
        outs[1][...] = dckv.astype(BF16)
        accs[0][...] += dgq
        accs[1][...] += dgkv

    return _rowwise(body, name="mla_prep_bwd", rows=rows, tr=512,
                    row_ins=[(proj, MLA_Q_RANK, C_CQ // MLA_Q_RANK), (proj, LANE, C_CKV // LANE),
                             (dcqn, MLA_Q_RANK, 0), (dckvn, MLA_KV_RANK, 0)],
                    full_ins=[g_q, g_kv], row_outs=[(MLA_Q_RANK, BF16), (MLA_KV_RANK, BF16)],
                    acc_outs=[((1, MLA_Q_RANK), F32), ((1, MLA_KV_RANK), F32)])


def _per_head(fn, x):
    return jnp.concatenate([fn(x[:, h * LANE:(h + 1) * LANE]) for h in range(x.shape[1] // LANE)], axis=1)


def _mla_rope_bwd(dq, dk, dv, tab):
    rows = dq.shape[0]
    hw = MLA_HEADS * LANE

    def body(i, ins, fulls, outs, accs):
        t = ins[3][...]
        dkr = jnp.zeros((ins[0].shape[0], LANE), F32)
        for h in range(MLA_HEADS):
            sl = slice(h * LANE, (h + 1) * LANE)
            outs[0][:, sl] = _rope_t(ins[0][:, sl], t).astype(BF16)
            dkr = dkr + ins[1][:, sl]
        outs[1][:, 0:hw] = ins[1][...].astype(BF16)
        outs[1][:, hw:2 * hw] = ins[2][...].astype(BF16)
        lane = lax.broadcasted_iota(jnp.int32, dkr.shape, 1)
        dkr = jnp.where((lane >= ROPE_LO) & (lane < ROPE_LO + MLA_ROPE), dkr, 0.0)
        outs[2][...] = _rope_t(dkr, t).astype(BF16)

    return _rowwise(body, name="mla_rope_bwd", rows=rows, tr=512,
                    row_ins=[(dq, hw, 0), (dk, hw, 0), (dv, hw, 0), (tab, 3 * LANE, 0)],
                    row_outs=[(hw, BF16), (2 * hw, BF16), (LANE, BF16)])


def _dot_nt(a, b):
    return lax.dot_general(a, b, (((1,), (1,)), ((), ())), preferred_element_type=F32)


def _dot_tn(a, b):
    return lax.dot_general(a, b, (((0,), (0,)), ((), ())), preferred_element_type=F32)


def _dot(a, b):
    return jnp.dot(a, b, preferred_element_type=F32)


def _attn_params(s, t, n_res_f32, n_res_bf16, ride=False):
    blocks = n_res_f32 * _nbytes((s, LANE), F32) + n_res_bf16 * _nbytes((s, LANE), BF16) + 6 * _nbytes((t, LANE), F32)
    return pltpu.CompilerParams(dimension_semantics=("arbitrary" if ride else "parallel", "arbitrary"),
                                vmem_limit_bytes=_vmem_limit(blocks, 12 * _nbytes((t, t), F32)))


def _mla_fwd(q, k, v, t, tk, shards=()):
    s, hw = q.shape
    heads, nq, r = hw // LANE, s // t, t // tk
    ng = len(shards)

    def body(q_ref, k_ref, v_ref, *rest):
        o_ref, l_ref = rest[ng], rest[ng + 1]
        h, i = pl.program_id(0), pl.program_id(1)
        if ng:
            gather = _GatherPhases(rest[:ng], rest[ng + 2:2 * ng + 2], *rest[2 * ng + 2:])
            pl.when((h == 0) & (i == 0))(gather.send)
            pl.when((h == heads // 2) & (i == 0))(gather.forward)
        qv = q_ref[...]

        def step(j, carry, off):
            m, l, acc = carry
            sl = pl.ds(pl.multiple_of(j * tk, tk), tk)
            sc = _dot_nt(qv, k_ref[sl, :])
            if off is not None:
                row = lax.broadcasted_iota(jnp.int32, (t, tk), 0)
                col = lax.broadcasted_iota(jnp.int32, (t, tk), 1)
                sc = jnp.where(col + off <= row, sc, -1e30)
            m_new = jnp.maximum(m, jnp.max(sc, axis=1, keepdims=True))
            p = jnp.exp2(sc - m_new)
            alpha = jnp.exp2(m - m_new)
            l = alpha * l + jnp.sum(p, axis=1, keepdims=True)
            acc = alpha * acc + _dot(p.astype(BF16), v_ref[sl, :])
            return m_new, l, acc

        init = (jnp.full((t, 1), -1e30, F32), jnp.zeros((t, 1), F32), jnp.zeros((t, LANE), F32))
        below = i * r
        carry = lax.fori_loop(0, below // 2, lambda j, c: step(2 * j + 1, step(2 * j, c, None), None), init)
        carry = lax.fori_loop(below // 2 * 2, below, lambda j, c: step(j, c, None), carry)
        for jj in range(r):
            carry = step(i * r + jj, carry, jj * tk)
        m, l, acc = carry
        o_ref[...] = (acc / l).astype(o_ref.dtype)
        l_ref[0] = m + jnp.log2(l)
        if ng:
            pl.when((h == heads - 1) & (i == nq - 1))(gather.finish)

    blk = pl.BlockSpec((t, LANE), lambda h, i: (i, h))
    res = pl.BlockSpec((s, LANE), lambda h, i: (0, h))
    outs = pl.pallas_call(
        body, name="mla_fwd", grid=(heads, nq), in_specs=[blk, res, res] + [ANY] * ng,
        out_specs=[blk, pl.BlockSpec((1, t, 1), lambda h, i: (h, i, 0))] + [ANY] * ng,
        out_shape=[jax.ShapeDtypeStruct((s, hw), BF16), jax.ShapeDtypeStruct((heads, s, 1), F32)]
        + [jax.ShapeDtypeStruct((N_CHIPS,) + sh.shape, sh.dtype) for sh in shards],
        scratch_shapes=_dma_sems(6 * ng) + [pltpu.SemaphoreType.DMA((ng,))] if ng else [],
        compiler_params=_attn_params(s, t, 0, 2, ride=ng > 0),
    )(q, k, v, *shards)
    return outs[0], outs[1], list(outs[2:])


def _mla_bwd(q, k, v, o, do, lse, t, tk, parts=()):
    s, hw = q.shape
    heads, nq, r = hw // LANE, s // t, t // tk
    ns = len(parts)

    def body(q_ref, k_ref, v_ref, o_ref, do_ref, l_ref, *rest):
        dq_ref, dk_ref, dv_ref = rest[ns:ns + 3]
        h, i = pl.program_id(0), pl.program_id(1)
        finish = _ride_along(_scatter_copies, ns, (rest[:ns], rest[ns + 3:2 * ns + 3], *rest[2 * ns + 3:]),
                             (h == 0) & (i == 0), (h == heads - 1) & (i == nq - 1))

        @pl.when(i == 0)
        def _():
            dk_ref[...] = jnp.zeros_like(dk_ref)
            dv_ref[...] = jnp.zeros_like(dv_ref)

        qv, dov, lv = q_ref[...], do_ref[...], l_ref[0]
        dlt = jnp.sum(dov.astype(F32) * o_ref[...].astype(F32), axis=1, keepdims=True)

        def step(j, dq, off):
            sl = pl.ds(pl.multiple_of(j * tk, tk), tk)
            kv, vv = k_ref[sl, :], v_ref[sl, :]
            p = jnp.exp2(_dot_nt(qv, kv) - lv)
            if off is not None:
                row = lax.broadcasted_iota(jnp.int32, (t, tk), 0)
                col = lax.broadcasted_iota(jnp.int32, (t, tk), 1)
                p = jnp.where(col + off <= row, p, 0.0)
            ds = (p * (_dot_nt(dov, vv) - dlt)).astype(BF16)
            dk_ref[sl, :] += _dot_tn(ds, qv) * (1.0 / LOG2E)
            dv_ref[sl, :] += _dot_tn(p.astype(BF16), dov)
            return dq + _dot(ds, kv)

        dq = lax.fori_loop(0, i * r, lambda j, c: step(j, c, None), jnp.zeros((t, LANE), F32))
        for jj in range(r):
            dq = step(i * r + jj, dq, jj * tk)
        dq_ref[...] = dq * MLA_SCALE
        finish()

    blk = pl.BlockSpec((t, LANE), lambda h, i: (i, h))
    res = pl.BlockSpec((s, LANE), lambda h, i: (0, h))
    full = jax.ShapeDtypeStruct((s, hw), F32)
    outs = pl.pallas_call(
        body, name="mla_bwd", grid=(heads, nq),
        in_specs=[blk, res, res, blk, blk, pl.BlockSpec((1, t, 1), lambda h, i: (h, i, 0))] + [ANY] * ns,
        out_specs=[blk, res, res] + [ANY] * ns, out_shape=[full, full, full] + _scatter_shapes(parts),
        scratch_shapes=_dma_sems(3 * ns) if ns else [],
        compiler_params=_attn_params(s, t, 2, 2, ride=ns > 0),
    )(q, k, v, o, do, lse, *parts)
    return outs[0], outs[1], outs[2], list(outs[3:])


def _sb_logits(qv, kv, keep, upper):
    z = _dot_nt(qv, kv)
    e = jnp.exp(-jnp.abs(z))
    l1p = jnp.log(1.0 + e)
    lb = jnp.minimum(z, 0.0) - l1p
    lo = -jnp.maximum(z, 0.0) - l1p
    if keep is not None:
        lo = jnp.where(keep, lo, 0.0)
    hi = lo.astype(BF16)
    rem = (lo - hi.astype(F32)).astype(BF16)
    suf = _dot(hi, upper) + _dot(rem, upper)
    return z, e, lb, lo, suf


def _tri(t, inclusive):
    row = lax.broadcasted_iota(jnp.int32, (t, t), 0)
    col = lax.broadcasted_iota(jnp.int32, (t, t), 1)
    return jnp.where((row >= col) if inclusive else (row > col), 1.0, 0.0).astype(BF16)


SB_QBLOCKS = 2
SB_PAIR = LANE // SB_HEAD_DIM
SB_CHAINS = SB_QBLOCKS * SB_PAIR
SB_FIRST = 2


def _sb_first_tile(b, t):
    start = jnp.maximum(b - (SB_FIRST - 1), 0) * t
    row = lax.broadcasted_iota(jnp.int32, (t, SB_FIRST * t), 0)
    col = lax.broadcasted_iota(jnp.int32, (t, SB_FIRST * t), 1)
    return pl.ds(pl.multiple_of(start, t), SB_FIRST * t), col + start < row + b * t


def _sb_head(x, hh):
    lane = lax.broadcasted_iota(jnp.int32, x.shape, 1)
    return jnp.where(lane // SB_HEAD_DIM == hh, x, jnp.zeros_like(x))


def _sb_walk(i, first, carries_of):
    n = SB_CHAINS
    carries = [first(c) for c in range(n)]
    width = len(carries[0])

    def alive(carry):
        return jnp.max(carry[0]) >= SB_ZERO_LOG

    def split(st):
        return [tuple(st[1 + c * width:1 + (c + 1) * width]) for c in range(n)]

    def live(st):
        any_alive = alive(split(st)[0])
        for cr in split(st)[1:]:
            any_alive = any_alive | alive(cr)
        return (st[0] <= SB_QBLOCKS * i) & any_alive

    def more(st):
        out = (st[0] + 1,)
        for c, cr in enumerate(split(st)):
            out += tuple(carries_of(c, st[0], cr))
        return out

    st = lax.while_loop(live, more, (jnp.int32(SB_FIRST),) + tuple(x for cr in carries for x in cr))
    jj, carries = st[0], split(st)
    for c in range(SB_PAIR, n):
        def live_c(s2, c=c):
            return (s2[0] <= SB_QBLOCKS * i + c // SB_PAIR) & alive(s2[1:])

        def more_c(s2, c=c):
            return (s2[0] + 1,) + tuple(carries_of(c, s2[0], s2[1:]))

        carries[c] = lax.while_loop(live_c, more_c, (jj,) + tuple(carries[c]))[1:]
    return carries


def _sb_fwd(proj, t):
    s = proj.shape[0]
    pairs, nq, nb = SB_HEADS // SB_PAIR, s // t, SB_QBLOCKS

    def body(q_ref, k_ref, v_ref, o_ref):
        i = pl.program_id(1)
        upper, upper_first = _tri(t, False), _tri(SB_FIRST * t, False)
        rows = [slice((c // SB_PAIR) * t, (c // SB_PAIR + 1) * t) for c in range(SB_CHAINS)]
        qs = [_sb_head(q_ref[rows[c], :] * SB_SCALE, c % SB_PAIR).astype(BF16) for c in range(SB_CHAINS)]

        def first(c):
            sl, keep = _sb_first_tile(nb * i + c // SB_PAIR, t)
            _, _, lb, lo, suf = _sb_logits(qs[c], k_ref[sl, :].astype(BF16), keep, upper_first)
            a = jnp.where(keep, jnp.exp(lb + suf), 0.0)
            return (jnp.sum(lo, axis=1, keepdims=True),
                    _dot(a.astype(BF16), _sb_head(v_ref[sl, :], c % SB_PAIR).astype(BF16)))

        def step(c, jj, carry):
            run, acc = carry
            sl = pl.ds(pl.multiple_of((nb * i + c // SB_PAIR - jj) * t, t), t)
            _, _, lb, lo, suf = _sb_logits(qs[c], k_ref[sl, :].astype(BF16), None, upper)
            a = jnp.exp(lb + suf + run)
            acc = acc + _dot(a.astype(BF16), _sb_head(v_ref[sl, :], c % SB_PAIR).astype(BF16))
            return run + jnp.sum(lo, axis=1, keepdims=True), acc

        carries = _sb_walk(i, first, step)
        for qb in range(nb):
            o_ref[qb * t:(qb + 1) * t, :] = sum(carries[qb * SB_PAIR + hh][1] for hh in range(SB_PAIR))

    return pl.pallas_call(
        body, name="sb_fwd", grid=(pairs, nq // nb),
        in_specs=[pl.BlockSpec((nb * t, LANE), lambda h, i: (i, C_SBQ // LANE + h)),
                  pl.BlockSpec((s, LANE), lambda h, i: (0, C_SBK // LANE + h)),
                  pl.BlockSpec((s, LANE), lambda h, i: (0, C_SBV // LANE + h))],
        out_specs=pl.BlockSpec((nb * t, LANE), lambda h, i: (i, h)),
        out_shape=jax.ShapeDtypeStruct((s, pairs * LANE), F32),
        compiler_params=_attn_params(s, nb * t, 0, 2),
    )(proj, proj, proj)


def _sb_bwd(proj, o, do, t, stacked=()):
    s = proj.shape[0]
    pairs, nq, nb = SB_HEADS // SB_PAIR, s // t, SB_QBLOCKS
    nx = len(stacked)

    def body(q_ref, k_ref, v_ref, o_ref, do_ref, *rest):
        dq_ref, dk_ref, dv_ref = rest[nx:nx + 3]
        hd, i = pl.program_id(0), pl.program_id(1)
        finish = _ride_along(_exchange_copies, nx, (rest[:nx], rest[nx + 3:2 * nx + 3], *rest[2 * nx + 3:]),
                             (hd == 0) & (i == 0), (hd == pairs - 1) & (i == nq // nb - 1))

        @pl.when(i == 0)
        def _():
            dk_ref[...] = jnp.zeros_like(dk_ref)
            dv_ref[...] = jnp.zeros_like(dv_ref)

        rows = [slice((c // SB_PAIR) * t, (c // SB_PAIR + 1) * t) for c in range(SB_CHAINS)]
        qs = [_sb_head(q_ref[rows[c], :] * SB_SCALE, c % SB_PAIR).astype(BF16) for c in range(SB_CHAINS)]
        dos = [_sb_head(do_ref[rows[c], :], c % SB_PAIR) for c in range(SB_CHAINS)]
        totals = [jnp.sum(dos[c].astype(F32) * o_ref[rows[c], :], axis=1, keepdims=True) for c in range(SB_CHAINS)]
        tris = {1: (_tri(t, False), _tri(t, True)), SB_FIRST: (_tri(SB_FIRST * t, False), _tri(SB_FIRST * t, True))}

        def tile(c, sl, keep, blocks, carry):
            run, g, dq = carry
            qv, dov = qs[c], dos[c]
            upper, upper_incl = tris[blocks]
            kv, vv = k_ref[sl, :].astype(BF16), v_ref[sl, :].astype(BF16)
            z, e, lb, lo, suf = _sb_logits(qv, kv, keep, upper)
            tail = suf + run
            a = jnp.exp(lb + tail)
            if keep is not None:
                a = jnp.where(keep, a, 0.0)
            ab = a.astype(BF16)
            gr = ab.astype(F32) * _dot_nt(dov, vv)
            ghi = gr.astype(BF16)
            grem = (gr - ghi.astype(F32)).astype(BF16)
            before = totals[c] - g - (_dot(ghi, upper_incl) + _dot(grem, upper_incl))
            before = jnp.where(tail < SB_ZERO_LOG, 0.0, before)
            r = 1.0 / (1.0 + e)
            pos = z >= 0.0
            dz = r * (gr * jnp.where(pos, e, 1.0) - before * jnp.where(pos, 1.0, e))
            if keep is not None:
                dz = jnp.where(keep, dz, 0.0)
            dzb = dz.astype(BF16)
            dk_ref[sl, :] += _dot_tn(dzb, qv)
            dv_ref[sl, :] += _dot_tn(ab, dov)
            return (run + jnp.sum(lo, axis=1, keepdims=True), g + jnp.sum(gr, axis=1, keepdims=True),
                    dq + _dot(dzb, _sb_head(kv, c % SB_PAIR)))

        zero = jnp.zeros((t, 1), F32)
        init = (zero, zero, jnp.zeros((t, LANE), F32))

        def first(c):
            sl, keep = _sb_first_tile(nb * i + c // SB_PAIR, t)
            return tile(c, sl, keep, SB_FIRST, init)

        def step(c, jj, carry):
            return tile(c, pl.ds(pl.multiple_of((nb * i + c // SB_PAIR - jj) * t, t), t), None, 1, carry)

        carries = _sb_walk(i, first, step)
        for qb in range(nb):
            dq_ref[qb * t:(qb + 1) * t, :] = sum(carries[qb * SB_PAIR + hh][2] for hh in range(SB_PAIR)) * SB_SCALE
        finish()

    blk = pl.BlockSpec((nb * t, LANE), lambda h, i: (i, h))
    res = pl.BlockSpec((s, LANE), lambda h, i: (0, h))
    full = jax.ShapeDtypeStruct((s, pairs * LANE), F32)
    outs = pl.pallas_call(
        body, name="sb_bwd", grid=(pairs, nq // nb),
        in_specs=[pl.BlockSpec((nb * t, LANE), lambda h, i: (i, C_SBQ // LANE + h)),
                  pl.BlockSpec((s, LANE), lambda h, i: (0, C_SBK // LANE + h)),
                  pl.BlockSpec((s, LANE), lambda h, i: (0, C_SBV // LANE + h)), blk, blk] + [ANY] * nx,
        out_specs=[blk, res, res] + [ANY] * nx, out_shape=[full, full, full] + _exchange_shapes(stacked),
        scratch_shapes=_dma_sems(nx) if nx else [],
        compiler_params=_attn_params(s, nb * t, 2, 2, ride=nx > 0),
    )(proj, proj, proj, o, do, *stacked)
    return outs[0], outs[1], outs[2], list(outs[3:])


def _xattn_probs(qh, kh):
    sc = _dot_nt(qh, kh) * (1.0 / math.sqrt(X_HEAD_DIM))
    p = jnp.exp(sc - jnp.max(sc, axis=1, keepdims=True))
    return p / jnp.sum(p, axis=1, keepdims=True)


def _xattn_fwd(xq, xkv):
    rows = xq.shape[0]
    w = X_HEADS * X_HEAD_DIM

    def body(i, ins, fulls, outs, accs):
        for h in range(X_HEADS):
            sl = slice(h * LANE, (h + 1) * LANE)
            p = _xattn_probs(ins[0][:, sl], fulls[0][:, sl])
            outs[0][:, sl] = _dot(p.astype(BF16), fulls[0][:, w + h * LANE:w + (h + 1) * LANE]).astype(BF16)

    return _rowwise(body, name="xattn_fwd", rows=rows, tr=512, row_ins=[(xq, w, 0)], full_ins=[xkv],
                    row_outs=[(w, BF16)])[0]


def _xattn_bwd(xq, xkv, dxo):
    rows = xq.shape[0]
    w = X_HEADS * X_HEAD_DIM

    def body(i, ins, fulls, outs, accs):
        _acc_init(i, accs)
        for h in range(X_HEADS):
            sl = slice(h * LANE, (h + 1) * LANE)
            slv = slice(w + h * LANE, w + (h + 1) * LANE)
            qh, kh, vh, doh = ins[0][:, sl], fulls[0][:, sl], fulls[0][:, slv], ins[1][:, sl]
            p = _xattn_probs(qh, kh)
            dp = _dot_nt(doh, vh)
            ds = (p * (dp - jnp.sum(p * dp, axis=1, keepdims=True)) * (1.0 / math.sqrt(X_HEAD_DIM))).astype(BF16)
            outs[0][:, sl] = _dot(ds, kh).astype(BF16)
            accs[0][:, sl] += _dot_tn(ds, qh)
            accs[0][:, slv] += _dot_tn(p.astype(BF16), doh)

    return _rowwise(body, name="xattn_bwd", rows=rows, tr=512, row_ins=[(xq, w, 0), (dxo, w, 0)], full_ins=[xkv],
                    row_outs=[(w, BF16)], acc_outs=[(xkv.shape, F32)])


def _gate_fwd(proj, pa, pb, b_gate):
    rows = proj.shape[0]

    def body(i, ins, fulls, outs, accs):
        sa = _sigmoid(ins[0][...].astype(F32) + fulls[0][0:1, :])
        sb = _sigmoid(ins[1][...].astype(F32) + fulls[0][1:2, :])
        outs[0][...] = (sa * ins[2][...].astype(F32) + sb * ins[3][...].astype(F32)).astype(BF16)

    return _rowwise(body, name="gate_fwd", rows=rows, tr=512,
                    row_ins=[(proj, D_MODEL, C_GA // D_MODEL), (proj, D_MODEL, C_GB // D_MODEL), (pa, D_MODEL, 0),
                             (pb, D_MODEL, 0)],
                    full_ins=[b_gate], row_outs=[(D_MODEL, BF16)])[0]


def _gate_bwd(proj, pa, pb, b_gate, dm):
    rows = proj.shape[0]

    def body(i, ins, fulls, outs, accs):
        _acc_init(i, accs)
        d = ins[4][...].astype(F32)
        sa = _sigmoid(ins[0][...].astype(F32) + fulls[0][0:1, :])
        sb = _sigmoid(ins[1][...].astype(F32) + fulls[0][1:2, :])
        dga = d * ins[2][...].astype(F32) * sa * (1.0 - sa)
        dgb = d * ins[3][...].astype(F32) * sb * (1.0 - sb)
        outs[0][...] = (d * sa).astype(BF16)
        outs[1][...] = (d * sb).astype(BF16)
        outs[2][...] = dga.astype(BF16)
        outs[3][...] = dgb.astype(BF16)
        accs[0][0:1, :] += jnp.sum(dga, axis=0, keepdims=True)
        accs[0][1:2, :] += jnp.sum(dgb, axis=0, keepdims=True)

    return _rowwise(body, name="gate_bwd", rows=rows, tr=512,
                    row_ins=[(proj, D_MODEL, C_GA // D_MODEL), (proj, D_MODEL, C_GB // D_MODEL), (pa, D_MODEL, 0),
                             (pb, D_MODEL, 0), (dm, D_MODEL, 0)],
                    full_ins=[b_gate], row_outs=[(D_MODEL, BF16)] * 4, acc_outs=[((2, D_MODEL), F32)])


def _loss_head(x3, target, g_final):
    rows = x3.shape[0]

    def body(i, ins, fulls, outs, accs):
        _acc_init(i, accs)
        xv, g = ins[0][...], fulls[0][...]
        d = _rms(xv, g) - ins[1][...]
        dx, dg = _rms_bwd(xv, g, d * (1.0 / D_MODEL))
        outs[0][...] = dx
        accs[0][...] += dg
        accs[1][...] += jnp.sum(d * d, axis=0, keepdims=True)

    return _rowwise(body, name="loss_head", rows=rows, tr=512, row_ins=[(x3, D_MODEL, 0), (target, D_MODEL, 0)],
                    full_ins=[g_final], row_outs=[(D_MODEL, F32)], acc_outs=[((1, D_MODEL), F32), ((1, D_MODEL), F32)])


def _adamw(w, g, m, v, name):
    rows, c = w.shape

    def body(i, ins, fulls, outs, accs):
        wv, gv = ins[0][...], ins[1][...]
        mn = ADAM_B1 * ins[2][...] + (1.0 - ADAM_B1) * gv
        vn = ADAM_B2 * ins[3][...] + (1.0 - ADAM_B2) * jnp.square(gv)
        m_hat = mn / (1.0 - ADAM_B1 ** ADAM_STEP)
        v_hat = vn / (1.0 - ADAM_B2 ** ADAM_STEP)
        outs[0][...] = -ADAM_LR * (m_hat / (jnp.sqrt(v_hat) + ADAM_EPS) + ADAM_WD * wv)
        outs[1][...] = mn
        outs[2][...] = vn

    return _rowwise(body, name=name, rows=rows, tr=_row_tile(rows, 256), row_ins=[(a, c, 0) for a in (w, g, m, v)],
                    row_outs=[(c, F32)] * 3)


def _place():
    x, y, c = lax.axis_index("x"), lax.axis_index("y"), lax.axis_index("c")
    chips = [(1 - x, y), (x, 1 - y), (1 - x, 1 - y)]
    return x, y, c, chips


ANY = pl.BlockSpec(memory_space=pl.ANY)


def _remote(src, dst, send_sem, recv_sem, to):
    return pltpu.make_async_remote_copy(src_ref=src, dst_ref=dst, send_sem=send_sem, recv_sem=recv_sem,
                                        device_id=to, device_id_type=MESH)


def _dma_sems(n):
    return [pltpu.SemaphoreType.DMA((n,)), pltpu.SemaphoreType.DMA((n,))]


class _GatherPhases:
    def __init__(self, ins, outs, send_sems, recv_sems, local_sems=None):
        self.ins, self.outs, self.send_sems, self.recv_sems = ins, outs, send_sems, recv_sems
        self.local_sems = local_sems
        self.x, self.y, self.c, self.chips = _place()
        self.me = 2 * self.x + self.y

    def _locals(self):
        if self.local_sems is None:
            return []
        return [pltpu.make_async_copy(self.ins[t], self.outs[t].at[self.me], self.local_sems.at[t])
                for t in range(len(self.ins))]

    def _copy(self, t, j, chip_idx, hlf, to, src=None):
        h = self.ins[t].shape[0] // 2
        dst = self.outs[t].at[chip_idx, pl.ds(hlf * h, h), :]
        return _remote(dst if src is None else src, dst, self.send_sems.at[6 * t + j], self.recv_sems.at[6 * t + j], to)

    def _sends(self):
        out = []
        for t in range(len(self.ins)):
            h = self.ins[t].shape[0] // 2
            for j, chip in enumerate(self.chips):
                out.append(self._copy(t, j, self.me, self.c, (*chip, self.c), src=self.ins[t].at[pl.ds(self.c * h, h), :]))
        return out

    def _forwards(self):
        return [self._copy(t, 3 + j, 2 * chip[0] + chip[1], self.c, (self.x, self.y, 1 - self.c))
                for t in range(len(self.ins)) for j, chip in enumerate(self.chips)]

    def send(self):
        for cp in self._sends() + self._locals():
            cp.start()

    def forward(self):
        here = (self.x, self.y, self.c)
        landed = [self._copy(t, j, 2 * chip[0] + chip[1], self.c, here)
                  for t in range(len(self.ins)) for j, chip in enumerate(self.chips)]
        for arrival, fwd in zip(landed, self._forwards()):
            arrival.wait_recv()
            fwd.start()

    def finish(self):
        here = (self.x, self.y, self.c)
        for t in range(len(self.ins)):
            for j, chip in enumerate(self.chips):
                self._copy(t, 3 + j, 2 * chip[0] + chip[1], 1 - self.c, here).wait_recv()
        for cp in self._sends() + self._forwards():
            cp.wait_send()
        for cp in self._locals():
            cp.wait()


def _all_gather_weights(shards):
    n = len(shards)

    def body(*refs):
        gather = _GatherPhases(refs[:n], refs[n:2 * n], *refs[2 * n:])
        gather.send()
        gather.forward()
        gather.finish()

    return pl.pallas_call(
        body, name="all_gather_weights", in_specs=[ANY] * n, out_specs=[ANY] * n,
        out_shape=[jax.ShapeDtypeStruct((N_CHIPS,) + s.shape, s.dtype) for s in shards],
        scratch_shapes=_dma_sems(6 * n),
    )(*shards)


def _exchange_copies(ins, outs, send_sems, recv_sems):
    x, y, c, _ = _place()
    cps = []
    for t in range(len(ins)):
        h = ins[t].shape[1] // 2
        cps.append(_remote(ins[t].at[:, pl.ds((1 - c) * h, h), :], outs[t], send_sems.at[t], recv_sems.at[t],
                           (x, y, 1 - c)))
    return cps


def _exchange_shapes(stacked):
    return [jax.ShapeDtypeStruct((N_CHIPS, s.shape[1] // 2, s.shape[2]), s.dtype) for s in stacked]


def _scatter_copies(ins, outs, send_sems, recv_sems):
    x, y, c, chips = _place()
    return [_remote(ins[t].at[2 * chip[0] + chip[1]], outs[t].at[j], send_sems.at[3 * t + j], recv_sems.at[3 * t + j],
                    (*chip, c)) for t in range(len(ins)) for j, chip in enumerate(chips)]


def _scatter_shapes(parts):
    return [jax.ShapeDtypeStruct((N_CHIPS - 1,) + s.shape[1:], s.dtype) for s in parts]


def _direct_copies(ins, outs, send_sems, recv_sems):
    x, y, k, chips = _place()
    me = 2 * x + y
    n = len(outs) // 2
    starts, arrivals = [], []
    for t in range(n):
        full, half_prec, landed32, landed16 = ins[t], ins[n + t], outs[t], outs[n + t]
        h = full.shape[1] // 2
        base = 7 * t
        pair = _remote(full.at[me, pl.ds((1 - k) * h, h), :], landed32, send_sems.at[base], recv_sems.at[base],
                       (x, y, 1 - k))
        starts.append(pair)
        arrivals.append(pair)
        for j, chip in enumerate(chips):
            for core in range(2):
                starts.append(_remote(half_prec.at[2 * chip[0] + chip[1], pl.ds(core * h, h), :], landed16.at[2 * j + k],
                                      send_sems.at[base + 1 + 2 * j + core], recv_sems.at[base + 1 + 2 * j + k],
                                      (*chip, core)))
                arrivals.append(_remote(half_prec.at[0, pl.ds(0, h), :], landed16.at[2 * j + core],
                                        send_sems.at[base + 1 + 2 * j + core], recv_sems.at[base + 1 + 2 * j + core],
                                        (x, y, k)))
    return starts, arrivals


def _direct_shapes(stacked):
    half = lambda s: (s.shape[1] // 2, s.shape[2])
    return ([jax.ShapeDtypeStruct(half(s), F32) for s in stacked]
            + [jax.ShapeDtypeStruct((6,) + half(s), BF16) for s in stacked])


def _ride_along(copies_of, n, refs, first, last):
    if not n:
        return lambda: None

    def start():
        built = copies_of(*refs)
        for cp in (built[0] if isinstance(built, tuple) else built):
            cp.start()

    def wait():
        built = copies_of(*refs)
        if not isinstance(built, tuple):
            for cp in built:
                cp.wait()
            return
        for cp in built[0]:
            cp.wait_send()
        for cp in built[1]:
            cp.wait_recv()

    pl.when(first)(start)
    return lambda: pl.when(last)(wait)


def _pair_exchange_halves(shards):
    n = len(shards)

    def body(*refs):
        bufs = refs[n:2 * n]
        send_sems, recv_sems = refs[2 * n:]
        x, y, c, _ = _place()
        cps = []
        for t in range(n):
            h = bufs[t].shape[0] // 2
            rows = bufs[t].at[pl.ds(c * h, h), :]
            cps.append(_remote(rows, rows, send_sems.at[t], recv_sems.at[t], (x, y, 1 - c)))
            cps[-1].start()
        for cp in cps:
            cp.wait()

    return pl.pallas_call(
        body, name="pair_exchange_halves", in_specs=[ANY] * n, out_specs=[ANY] * n,
        out_shape=[jax.ShapeDtypeStruct(s.shape, s.dtype) for s in shards],
        input_output_aliases={t: t for t in range(n)},
        scratch_shapes=_dma_sems(n),
    )(*shards)


def _pair_sum(gs, recv, place, name):
    _, r, cols = gs.shape
    h = r // 2

    def kern(p_ref, a_ref, b_ref, o_ref):
        o_ref[...] = (a_ref[...] + b_ref[...]).astype(BF16)

    blk = lambda f: pl.BlockSpec((1, h, cols), f)
    return pl.pallas_call(
        kern, name=name,
        grid_spec=pltpu.PrefetchScalarGridSpec(
            num_scalar_prefetch=1, grid=(N_CHIPS,),
            in_specs=[blk(lambda d, p: (d, p[1], 0)), blk(lambda d, p: (d, 0, 0))],
            out_specs=blk(lambda d, p: (d, 0, 0))),
        out_shape=jax.ShapeDtypeStruct((N_CHIPS, h, cols), BF16),
        compiler_params=pltpu.CompilerParams(dimension_semantics=("arbitrary",),
                                             vmem_limit_bytes=_vmem_limit(3 * _nbytes((h, cols), F32), 0)),
    )(place, gs, recv)


def _chip_sum(gs, recv, got, place, name):
    _, r, cols = gs.shape
    h = r // 2

    def kern(p_ref, a_ref, b_ref, g0, g1, g2, o_ref):
        own = a_ref[0] + b_ref[0]
        o_ref[...] = ((own + g0[0].astype(F32)) + g1[0].astype(F32)) + g2[0].astype(F32)

    blk = lambda f: pl.BlockSpec((1, h, cols), f)
    return pl.pallas_call(
        kern, name=name,
        grid_spec=pltpu.PrefetchScalarGridSpec(
            num_scalar_prefetch=1, grid=(1,),
            in_specs=[blk(lambda i, p: (p[0], p[1], 0)), blk(lambda i, p: (p[0], 0, 0)), blk(lambda i, p: (0, 0, 0)),
                      blk(lambda i, p: (1, 0, 0)), blk(lambda i, p: (2, 0, 0))],
            out_specs=pl.BlockSpec((h, cols), lambda i, p: (p[1], 0))),
        out_shape=jax.ShapeDtypeStruct((r, cols), F32),
        compiler_params=pltpu.CompilerParams(dimension_semantics=("arbitrary",),
                                             vmem_limit_bytes=_vmem_limit(5 * _nbytes((h, cols), F32), 0)),
    )(place, gs, recv, got, got, got)


def _chip_sum_direct(gs, landed32, landed16, place, name):
    _, r, cols = gs.shape
    h = r // 2

    def kern(p_ref, a_ref, b_ref, g_ref, o_ref):
        acc = a_ref[0] + b_ref[...]
        for slot in range(6):
            acc = acc + g_ref[slot].astype(F32)
        o_ref[...] = acc

    return pl.pallas_call(
        kern, name=name,
        grid_spec=pltpu.PrefetchScalarGridSpec(
            num_scalar_prefetch=1, grid=(1,),
            in_specs=[pl.BlockSpec((1, h, cols), lambda i, p: (p[0], p[1], 0)),
                      pl.BlockSpec((h, cols), lambda i, p: (0, 0)), pl.BlockSpec((6, h, cols), lambda i, p: (0, 0, 0))],
            out_specs=pl.BlockSpec((h, cols), lambda i, p: (p[1], 0))),
        out_shape=jax.ShapeDtypeStruct((r, cols), F32),
        compiler_params=pltpu.CompilerParams(dimension_semantics=("arbitrary",),
                                             vmem_limit_bytes=_vmem_limit(6 * _nbytes((h, cols), F32), 0)),
    )(place, gs, landed32, landed16)


def _all_reduce_small(vec, name):
    r, cols = vec.shape

    def body(in_ref, out_ref, gath, send_sems, recv_sems):
        x, y, c, _ = _place()
        me = 4 * x + 2 * y + c
        gath[me] = in_ref[...]
        sends = []
        for k in range(1, 8):
            to = (x ^ (k >> 2), y ^ ((k >> 1) & 1), c ^ (k & 1))
            cp = pltpu.make_async_remote_copy(src_ref=in_ref, dst_ref=gath.at[me], send_sem=send_sems.at[k - 1],
                                              recv_sem=recv_sems.at[k - 1], device_id=to, device_id_type=MESH)
            cp.start()
            sends.append(cp)
        for k in range(1, 8):
            peer = me ^ k
            pltpu.make_async_remote_copy(src_ref=in_ref, dst_ref=gath.at[peer], send_sem=send_sems.at[k - 1],
                                         recv_sem=recv_sems.at[k - 1], device_id=(x, y, c),
                                         device_id_type=MESH).wait_recv()
        for cp in sends:
            cp.wait_send()
        acc = gath[0]
        for d in range(1, 8):
            acc = acc + gath[d]
        out_ref[...] = acc

    vm = pl.BlockSpec(memory_space=pltpu.VMEM)
    return pl.pallas_call(
        body, name=name, in_specs=[vm], out_specs=vm,
        out_shape=jax.ShapeDtypeStruct((r, cols), F32),
        scratch_shapes=[pltpu.VMEM((8, r, cols), F32), pltpu.SemaphoreType.DMA((7,)), pltpu.SemaphoreType.DMA((7,))],
    )(vec)


def _pad_heads(w, heads, dim, axis):
    shp = w.shape[:axis] + (heads, dim) + w.shape[axis + 1:]
    pad = [(0, 0)] * len(shp)
    pad[axis + 1] = (0, LANE - dim)
    w = jnp.pad(w.reshape(shp), pad)
    return w.reshape(w.shape[:axis] + (heads * LANE,) + w.shape[axis + 2:])


def _unpad_heads(w, heads, dim, axis):
    shp = w.shape[:axis] + (heads, LANE) + w.shape[axis + 1:]
    w = lax.slice_in_dim(w.reshape(shp), 0, dim, axis=axis + 1)
    return w.reshape(w.shape[:axis] + (heads * dim,) + w.shape[axis + 2:])


def _w_in_layout(w_in):
    kr = jnp.pad(w_in[:, 384:416], ((0, 0), (ROPE_LO, LANE - ROPE_LO - MLA_ROPE)))
    sb = lambda lo: w_in[:, lo:lo + SB_WIDTH]
    return jnp.concatenate([w_in[:, 1952:2976], w_in[:, 2976:4000], sb(416), sb(928), sb(1440), w_in[:, 0:256],
                            w_in[:, 256:384], kr], axis=1)


def _w_in_unlayout(d):
    sb = lambda lo: d[:, lo:lo + SB_WIDTH]
    return jnp.concatenate([d[:, C_CQ:C_CQ + 256], d[:, C_CKV:C_CKV + 128], d[:, C_KR + ROPE_LO:C_KR + ROPE_LO + MLA_ROPE],
                            sb(C_SBQ), sb(C_SBK), sb(C_SBV), d[:, C_GA:C_GA + 1024], d[:, C_GB:C_GB + 1024]], axis=1)


def _w_ukv_layout(w):
    w3 = w.reshape(MLA_KV_RANK, MLA_HEADS, MLA_NOPE + MLA_V)
    pad = lambda part: jnp.pad(part, ((0, 0), (0, 0), (0, LANE - part.shape[2]))).reshape(MLA_KV_RANK, MLA_HEADS * LANE)
    return jnp.concatenate([pad(w3[:, :, :MLA_NOPE]), pad(w3[:, :, MLA_NOPE:])], axis=1)


def _w_ukv_unlayout(d):
    hw = MLA_HEADS * LANE
    kpart = d[:, :hw].reshape(MLA_KV_RANK, MLA_HEADS, LANE)[:, :, :MLA_NOPE]
    vpart = d[:, hw:].reshape(MLA_KV_RANK, MLA_HEADS, LANE)[:, :, :MLA_V]
    return jnp.concatenate([kpart, vpart], axis=2).reshape(MLA_KV_RANK, MLA_HEADS * (MLA_NOPE + MLA_V))


def _shard_of(full, d, axis):
    n = full.shape[axis] // N_CHIPS
    return lax.slice_in_dim(full, d * n, (d + 1) * n, axis=axis)


def _local_step(x, mem, pos, target, w, t_mla, t_sb, late=None, reduce=None):
    s = x.shape[0]
    w = dict(w)
    win = _w_in_layout(w["w_in"])
    wuq = _pad_heads(w["w_uq"], MLA_HEADS, MLA_NOPE + MLA_ROPE, 1)
    wkv = _w_ukv_layout(w["w_ukv"])
    inv_freq = ROPE_THETA ** (-jnp.arange(0, MLA_ROPE, 2, dtype=F32) / MLA_ROPE)
    freq_lane = jnp.pad(jnp.concatenate([inv_freq, inv_freq]), (ROPE_LO, LANE - ROPE_LO - MLA_ROPE)).reshape(1, LANE)
    add = lambda accs, ex: (accs[0] + ex[0],)

    def add_norm(accs, ex):
        y = accs[0] + ex[0]
        return y, _rms(y, ex[1])

    tab = _rope_tables(pos.reshape(s, 1), freq_lane)
    h = _rms_fwd_call(x, w["g_mix"], "rms_mix")
    proj = _mm(h, [win], name="proj_in", out_dtypes=(BF16,))
    cqn, ckvn, krope = _mla_prep_fwd(proj, tab, w["g_q_lat"], w["g_kv_lat"])
    hw = MLA_HEADS * LANE
    qa = _mm(cqn, [wuq], name="q_up", row_extras=(tab,), out_dtypes=(BF16,),
             epilogue=lambda accs, ex: (_per_head(lambda t: _rope(t, ex[0]) * (MLA_SCALE * LOG2E), accs[0]),))
    ka = _mm(ckvn, [wkv[:, :hw]], name="k_up", row_extras=(krope,), out_dtypes=(BF16,),
             epilogue=lambda accs, ex: (_per_head(lambda t: t + ex[0], accs[0]),))
    va = _mm(ckvn, [wkv[:, hw:]], name="v_up", out_dtypes=(BF16,))
    o_a, lse, gathered = _mla_fwd(qa, ka, va, t_mla[0], t_mla[1], late[0] if late else ())
    if late:
        w.update(late[1](gathered))
    wa = _pad_heads(w["w_a_proj"], MLA_HEADS, MLA_V, 0)
    wb = w["w_b_proj"]
    o_b = _sb_fwd(proj, t_sb)
    pa = _mm(o_a, [wa], name="proj_a", out_dtypes=(BF16,))
    pb = _mm(o_b, [wb], name="proj_b", out_dtypes=(BF16,))
    merged = _gate_fwd(proj, pa, pb, w["b_gate"])
    x1, hx = _mm(merged, [w["w_o"]], name="proj_o", extras=(x,), consts=(w["g_x"],), epilogue=add_norm,
                 out_dtypes=(F32, BF16))
    mn = _rms_fwd_call(mem, w["g_mem"], "rms_mem")
    xq = _mm(hx, [w["w_xq"]], name="xq", out_dtypes=(BF16,))
    xkv = _mm(mn, [w["w_xkv"]], name="xkv", out_dtypes=(BF16,))
    xo = _xattn_fwd(xq, xkv)
    x2, hf = _mm(xo, [w["w_xo"]], name="proj_xo", extras=(x1,), consts=(w["g_ffn"],), epilogue=add_norm,
                 out_dtypes=(F32, BF16))

    def swiglu(accs, ex):
        a, b = accs
        return a, b, a * _sigmoid(a) * b

    ga, gu, hmid = _mm(hf, [w["w_gate"], w["w_up"]], name="ffn_up", epilogue=swiglu, out_dtypes=(BF16, BF16, BF16),
                       tm=1024, tn=1408, chunk=MM_CHUNK)
    x3 = _mm(hmid, [w["w_down"]], name="ffn_down", extras=(x2,), epilogue=add, tk=2816)

    dx3, dg_final, sq = _loss_head(x3, target, w["g_final"].reshape(1, D_MODEL))
    g = {"g_final": dg_final.reshape(D_MODEL)}

    def swiglu_bwd(accs, ex):
        dh, a, b = accs[0], ex[0].astype(F32), ex[1].astype(F32)
        sg = _sigmoid(a)
        return dh * b * sg * (1.0 + a * (1.0 - sg)), dh * a * sg

    da, db = _mm(dx3, [w["w_down"]], name="ffn_down_dx", tb=True, extras=(ga, gu), epilogue=swiglu_bwd,
                 out_dtypes=(BF16, BF16), tm=1024, tn=1408, chunk=MM_CHUNK)
    g["w_down"] = _mm(hmid, [dx3], name="ffn_down_dw", ta=True, tm=1408)
    g["w_gate"] = _mm(hf, [da], name="ffn_gate_dw", ta=True, tn=1408)
    g["w_up"] = _mm(hf, [db], name="ffn_up_dw", ta=True, tn=1408)
    dhf = _mm(da, [w["w_gate"]], name="ffn_gate_dx", tb=True, tk=2816)
    dhf = _mm(db, [w["w_up"]], name="ffn_up_dx", tb=True, extras=(dhf,), epilogue=add, tk=2816,
              out_dtypes=(BF16,))
    dx2, g["g_ffn"] = _rms_bwd_call(x2, w["g_ffn"], dhf, dx3, "rms_ffn_bwd")

    dxo = _mm(dx2, [w["w_xo"]], name="proj_xo_dx", tb=True, out_dtypes=(BF16,))
    g["w_xo"] = _mm(xo, [dx2], name="proj_xo_dw", ta=True)
    dxq, dxkv = _xattn_bwd(xq, xkv, dxo)
    dhx = _mm(dxq, [w["w_xq"]], name="xq_dx", tb=True, out_dtypes=(BF16,))
    g["w_xq"] = _mm(hx, [dxq], name="xq_dw", ta=True)
    dmn = _mm(dxkv, [w["w_xkv"]], name="xkv_dx", tb=True)
    g["w_xkv"] = _mm(mn, [dxkv], name="xkv_dw", ta=True)
    dx1, g["g_x"] = _rms_bwd_call(x1, w["g_x"], dhx, dx2, "rms_x_bwd")
    _, g["g_mem"] = _rms_bwd_call(mem, w["g_mem"], dmn, None, "rms_mem_bwd")

    dmerged = _mm(dx1, [w["w_o"]], name="proj_o_dx", tb=True, out_dtypes=(BF16,))
    g["w_o"] = _mm(merged, [dx1], name="proj_o_dw", ta=True)
    dpa, dpb, dga, dgb, g["b_gate"] = _gate_bwd(proj, pa, pb, w["b_gate"], dmerged)
    do_a = _mm(dpa, [wa], name="proj_a_dx", tb=True, out_dtypes=(BF16,))
    do_b = _mm(dpb, [wb], name="proj_b_dx", tb=True, out_dtypes=(BF16,))
    g["w_a_proj"] = _unpad_heads(_mm(o_a, [dpa], name="proj_a_dw", ta=True), MLA_HEADS, MLA_V, 0)
    g["w_b_proj"] = _mm(o_b, [dpb], name="proj_b_dw", ta=True)

    stacked = [reduce[1](n, g[n]) for n in reduce[0]] if reduce else []
    dsq, dsk, dsv, recv = _sb_bwd(proj, o_b, do_b, t_sb, stacked)
    parts = [reduce[2](n, gs, rv) for n, gs, rv in zip(reduce[0], stacked, recv)] if reduce else []
    dqa, dka, dva, got = _mla_bwd(qa, ka, va, o_a, do_a, lse, t_mla[0], t_mla[2], parts)
    riding = dict(zip(reduce[0], zip(stacked, recv, got))) if reduce else {}
    dqp, dkvp, dkr = _mla_rope_bwd(dqa, dka, dva, tab)
    g["w_uq"] = _unpad_heads(_mm(cqn, [dqp], name="q_up_dw", ta=True), MLA_HEADS, MLA_NOPE + MLA_ROPE, 1)
    g["w_ukv"] = _w_ukv_unlayout(_mm(ckvn, [dkvp], name="kv_up_dw", ta=True))
    dcqn = _mm(dqp, [wuq], name="q_up_dx", tb=True)
    dckvn = _mm(dkvp, [wkv], name="kv_up_dx", tb=True)
    dcq, dckv, g["g_q_lat"], g["g_kv_lat"] = _mla_prep_bwd(proj, w["g_q_lat"], w["g_kv_lat"], dcqn, dckvn)

    dproj = jnp.concatenate([dga, dgb, dsq.astype(BF16), dsk.astype(BF16), dsv.astype(BF16), dcq, dckv, dkr], axis=1)
    g["w_in"] = _w_in_unlayout(_mm(h, [dproj], name="proj_in_dw", ta=True))
    if not reduce:
        dh = _mm(dproj, [win], name="proj_in_dx", tb=True, tk=2048, out_dtypes=(BF16,))
        grad_x, g["g_mix"] = _rms_bwd_call(x, w["g_mix"], dh, dx1, "rms_mix_bwd")
        return sq, grad_x, g, riding
    stacked = [reduce[1](n, g[n]) for n in reduce[3]]
    dh, landed = _mm(dproj, [win], name="proj_in_dx", tb=True, tk=2048, out_dtypes=(BF16,),
                     ride=(_direct_copies, stacked + [s.astype(BF16) for s in stacked], _direct_shapes(stacked),
                           7 * len(stacked)))
    grad_x, g["g_mix"] = _rms_bwd_call(x, w["g_mix"], dh, dx1, "rms_mix_bwd")
    nl = len(stacked)
    riding.update(zip(reduce[3], zip(stacked, landed[:nl], landed[nl:])))
    return sq, grad_x, g, riding


def _small_pack(d):
    row5 = jnp.concatenate([d["g_q_lat"].reshape(-1), d["g_kv_lat"].reshape(-1), jnp.zeros((640,), F32)])
    rows = [d[n].reshape(-1) for n in ("g_mix", "g_x", "g_mem", "g_ffn", "g_final")] + [row5]
    return rows


def _small_unpack(p, like):
    out = {n: p[i].reshape(like[n].shape) for i, n in enumerate(("g_mix", "g_x", "g_mem", "g_ffn", "g_final"))}
    out["g_q_lat"] = p[5, 0:256].reshape(like["g_q_lat"].shape)
    out["g_kv_lat"] = p[5, 256:384].reshape(like["g_kv_lat"].shape)
    return out


def kernel(x, mem, positions, g_mix, w_in, b_gate, g_q_lat, w_uq, g_kv_lat, w_ukv, w_a_proj, w_b_proj, w_o, g_x, g_mem, w_xq, w_xkv, w_xo, g_ffn, w_gate, w_up, w_down, g_final, loss_target, m_g_mix, m_w_in, m_b_gate, m_g_q_lat, m_w_uq, m_g_kv_lat, m_w_ukv, m_w_a_proj, m_w_b_proj, m_w_o, m_g_x, m_g_mem, m_w_xq, m_w_xkv, m_w_xo, m_g_ffn, m_w_gate, m_w_up, m_w_down, m_g_final, v_g_mix, v_w_in, v_b_gate, v_g_q_lat, v_w_uq, v_g_kv_lat, v_w_ukv, v_w_a_proj, v_w_b_proj, v_w_o, v_g_x, v_g_mem, v_w_xq, v_w_xkv, v_w_xo, v_g_ffn, v_w_gate, v_w_up, v_w_down, v_g_final):
    given = dict(locals())
    names = [n for n, _, _ in MATS] + ["b_gate"] + list(SMALL)
    wts = {n: given[n] for n in names}
    mom = {n: given["m_" + n] for n in names}
    var = {n: given["v_" + n] for n in names}
    shard2d = {n: shp for n, shp, _ in MATS}
    shard2d["b_gate"] = B_GATE_SHARD
    cx, cy, cc = lax.axis_index("x"), lax.axis_index("y"), lax.axis_index("c")
    me = 2 * cx + cy
    place = jnp.stack([me, cc]).astype(jnp.int32)
    bcol = me * B_GATE_SHARD[1]

    own = [wts[n].reshape(shard2d[n]).astype(BF16) for n, _, _ in MATS]
    bias = (("b_gate", (BIAS_ROWS, B_GATE_SHARD[1]), 1),)
    own_bias = [jnp.pad(wts["b_gate"].reshape(B_GATE_SHARD), ((0, BIAS_ROWS - B_GATE_SHARD[0]), (0, 0)))]

    def assemble(mats, gathered, mine=None):
        out = {}
        for k, ((n, shp, ax), g4) in enumerate(zip(mats, gathered)):
            if mine is not None:
                g4 = lax.dynamic_update_slice(g4, mine[k][None], (me, 0, 0))
            out[n] = g4.reshape(N_CHIPS * shp[0], shp[1]) if ax == 0 else jnp.concatenate(list(g4), axis=1)
        return out

    first = own[:N_EARLY] + own_bias
    full = assemble(MATS[:N_EARLY] + bias, _all_gather_weights(first), first)
    full["b_gate"] = full["b_gate"][0:B_GATE_SHARD[0]]
    late = (own[N_EARLY:], lambda gathered: assemble(MATS[N_EARLY:], gathered))
    for n in SMALL:
        full[n] = wts[n].reshape(1, -1) if n != "g_final" else wts[n]

    axis_of = {n: ax for n, _, ax in MATS}
    stack = lambda n, g: jnp.stack([_shard_of(g, d, axis_of[n]) for d in range(N_CHIPS)])
    pair_sum = lambda n, gs, rv: _pair_sum(gs, rv, place, "pair_sum_" + n)
    behind = [n for n, _, _ in MATS[N_EARLY:]]
    last = [n for n, _, _ in MATS[:N_EARLY]]
    sq, grad_x, grads, riding = _local_step(x[0], mem[0], positions[0], loss_target[0], full, t_mla=(1024, 1024, 1024), t_sb=256,
                                            late=late, reduce=(behind, stack, pair_sum, last))

    halves = [(_chip_sum_direct if n in last else _chip_sum)(*riding[n], place, "chip_sum_" + n) for n, _, _ in MATS]
    g_shard = dict(zip([n for n, _, _ in MATS], _pair_exchange_halves(halves)))

    small_rows = _small_pack({n: grads[n] for n in SMALL}) + [sq.reshape(-1), grads["b_gate"][0], grads["b_gate"][1]]
    small_rows += [jnp.zeros((D_MODEL,), F32)] * (SMALL_ROWS - len(small_rows))
    small = _all_reduce_small(jnp.stack(small_rows), "all_reduce_small")
    loss = (0.5 / D_MODEL) * jnp.sum(small[6])
    g_shard["b_gate"] = lax.dynamic_slice(small[7:9], (0, bcol), B_GATE_SHARD)

    out = {"grad": {}, "delta": {}, "m": {}, "v": {}}
    for n in [n for n, _, _ in MATS] + ["b_gate"]:
        shape = wts[n].shape
        r2 = lambda a: a.reshape(shard2d[n])
        d_n, m_n, v_n = _adamw(r2(wts[n]), g_shard[n], r2(mom[n]), r2(var[n]), "adamw_" + n)
        for key, a in (("grad", g_shard[n]), ("delta", d_n), ("m", m_n), ("v", v_n)):
            out[key][n] = a.reshape(shape)
    sp = lambda d: jnp.stack(_small_pack(d) + [jnp.zeros((D_MODEL,), F32)] * 2)
    delta_s, m_s, v_s = _adamw(sp(wts), small[0:8].at[6:8].set(0.0), sp(mom), sp(var), "adamw_small")
    for key, p in (("grad", small), ("delta", delta_s), ("m", m_s), ("v", v_s)):
        out[key].update(_small_unpack(p, wts))

    order = ["g_mix", "w_in", "b_gate", "g_q_lat", "w_uq", "g_kv_lat", "w_ukv", "w_a_proj", "w_b_proj", "w_o", "g_x",
             "g_mem", "w_xq", "w_xkv", "w_xo", "g_ffn", "w_gate", "w_up", "w_down", "g_final"]
    return (loss, grad_x[None], *[out[key][n] for key in ("grad", "delta", "m", "v") for n in order])
```

```python
import functools
import math

import jax
import jax.numpy as jnp
from jax import lax
from jax.experimental import pallas as pl
from jax.experimental.pallas import tpu as pltpu

F32 = jnp.float32
BF16 = jnp.bfloat16
MESH = pl.DeviceIdType.MESH

D_MODEL = 1024
MLA_HEADS = 8
MLA_Q_RANK = 256
MLA_KV_RANK = 128
MLA_NOPE = 64
MLA_ROPE = 32
MLA_V = 64
ROPE_THETA = 10000.0
SB_HEADS = 8
SB_HEAD_DIM = 64
X_HEADS = 4
X_HEAD_DIM = 128
EPS = 1e-6
ADAM_LR = 0.001
ADAM_B1 = 0.9
ADAM_B2 = 0.999
ADAM_EPS = 1e-08
ADAM_WD = 0.01
ADAM_STEP = 10

LANE = 128
LOG2E = 1.4426950408889634
MLA_SCALE = 1.0 / math.sqrt(MLA_NOPE + MLA_ROPE)
SB_SCALE = 1.0 / math.sqrt(SB_HEAD_DIM)
assert math.log2(SB_SCALE) == round(math.log2(SB_SCALE))
MM_CHUNK = 256
N_CHIPS = 4
VMEM_BYTES = 64 * 1024 * 1024

C_GA, C_GB, C_SBQ, C_SBK, C_SBV, C_CQ, C_CKV, C_KR = 0, 1024, 2048, 2560, 3072, 3584, 3840, 3968
SB_WIDTH = SB_HEADS * SB_HEAD_DIM
ROPE_LO = MLA_NOPE
HALF = MLA_ROPE // 2

SB_ZERO_LOG = -104.0

MATS = (
    ("w_in", (1024, 1000), 1), ("w_uq", (256, 192), 1), ("w_ukv", (128, 256), 1), ("w_a_proj", (512, 256), 1),
    ("w_b_proj", (512, 256), 1), ("w_o", (256, 1024), 0), ("w_xq", (256, 512), 0), ("w_xkv", (256, 1024), 0),
    ("w_xo", (512, 256), 1), ("w_gate", (1024, 704), 1), ("w_up", (1024, 704), 1), ("w_down", (704, 1024), 0),
)
N_EARLY = 3
B_GATE_SHARD = (2, 256)
BIAS_ROWS = 16
SMALL = ("g_mix", "g_x", "g_mem", "g_ffn", "g_final", "g_q_lat", "g_kv_lat")
SMALL_ROWS = 16


def _vmem_limit(block_bytes, temp_bytes):
    est = 2 * block_bytes + temp_bytes + (4 << 20)
    return int(min(max(est, 16 << 20), VMEM_BYTES - (6 << 20)))


def _nbytes(shape, dtype):
    return math.prod(shape) * jnp.dtype(dtype).itemsize


def _row_tile(rows, cap):
    if rows <= cap:
        return rows
    return max(t for t in range(8, cap + 1, 8) if rows % t == 0)


def _tile(n, cap):
    if n <= cap:
        return n
    best = None
    for t in range(LANE, cap + 1, LANE):
        if n % t == 0:
            best = t
    assert best is not None, (n, cap)
    return best


def _mm(a, bs, *, name, ta=False, tb=False, extras=(), row_extras=(), consts=(), epilogue=None, out_dtypes=(F32,),
        tm=1024, tn=1024, tk=1024, chunk=None, ride=None):
    bs = tuple(bs)
    m, k = (a.shape[1], a.shape[0]) if ta else a.shape
    n = bs[0].shape[0] if tb else bs[0].shape[1]
    tm, tn, tk = _tile(m, tm), _tile(n, tn), _tile(k, tk)
    assert m % tm == 0 and n % tn == 0 and k % tk == 0
    nk = k // tk
    nb, ne, no = len(bs), len(extras) + len(row_extras) + len(consts), len(out_dtypes)
    dims = (((0,) if ta else (1,)), ((1,) if tb else (0,))), ((), ())
    if epilogue is None:
        epilogue = lambda accs, ex: (accs[0],)

    rn, rn_out = (len(ride[1]), len(ride[2])) if ride else (0, 0)
    n_acc = nb if nk > 1 else 0
    gi, gj = m // tm, n // tn

    def body(*refs):
        a_ref, b_refs, e_refs = refs[0], refs[1:1 + nb], refs[1 + nb:1 + nb + ne]
        base = 1 + nb + ne + rn
        o_refs, acc_refs = refs[base:base + no], refs[base + no + rn_out:base + no + rn_out + n_acc]
        step = [pl.program_id(d) for d in range(3)]
        finish = _ride_along(ride[0] if ride else None, rn,
                             (refs[base - rn:base], refs[base + no:base + no + rn_out],
                              *refs[base + no + rn_out + n_acc:]),
                             (step[0] == 0) & (step[1] == 0) & (step[2] == 0),
                             (step[0] == gi - 1) & (step[1] == gj - 1) & (step[2] == nk - 1))
        if nk == 1:
            ch = chunk or tm
            bvs = [b_ref[...].astype(BF16) for b_ref in b_refs]
            for r0 in range(0, tm, ch):
                rows = slice(r0, r0 + ch)
                av = (a_ref[:, rows] if ta else a_ref[rows, :]).astype(BF16)
                accs = [lax.dot_general(av, bv, dims, preferred_element_type=F32) for bv in bvs]
                ex = [e[rows, :] for e in e_refs[:ne - len(consts)]] + [e[...] for e in e_refs[ne - len(consts):]]
                for o_ref, v in zip(o_refs, epilogue(accs, ex)):
                    o_ref[rows, :] = v.astype(o_ref.dtype)
            finish()
            return
        kk = step[2]

        @pl.when(kk == 0)
        def _():
            for acc in acc_refs:
                acc[...] = jnp.zeros_like(acc)

        av = a_ref[...].astype(BF16)
        for b_ref, acc in zip(b_refs, acc_refs):
            acc[...] += lax.dot_general(av, b_ref[...].astype(BF16), dims, preferred_element_type=F32)

        @pl.when(kk == nk - 1)
        def _():
            outs = epilogue([acc[...] for acc in acc_refs], [e[...] for e in e_refs])
            for o_ref, v in zip(o_refs, outs):
                o_ref[...] = v.astype(o_ref.dtype)

        finish()

    a_spec = pl.BlockSpec((tk, tm), lambda i, j, kk: (kk, i)) if ta else pl.BlockSpec((tm, tk), lambda i, j, kk: (i, kk))
    b_spec = pl.BlockSpec((tn, tk), lambda i, j, kk: (j, kk)) if tb else pl.BlockSpec((tk, tn), lambda i, j, kk: (kk, j))
    mn_spec = pl.BlockSpec((tm, tn), lambda i, j, kk: (i, j))
    blocks = (_nbytes((tm, tk), a.dtype) + sum(_nbytes((tk, tn), b.dtype) for b in bs)
              + sum(_nbytes((tm, tn), e.dtype) for e in extras) + sum(_nbytes((tm, tn), d) for d in out_dtypes)
              + sum(_nbytes((tm, e.shape[1]), e.dtype) for e in row_extras))
    temps = (nb + 4) * _nbytes((tm, tn), F32)
    outs = pl.pallas_call(
        body, name=name, grid=(m // tm, n // tn, nk),
        in_specs=[a_spec] + [b_spec] * nb + [mn_spec] * len(extras)
        + [pl.BlockSpec((tm, e.shape[1]), lambda i, j, kk: (i, 0)) for e in row_extras]
        + [pl.BlockSpec(e.shape, lambda i, j, kk: (0, 0)) for e in consts] + [ANY] * rn,
        out_specs=[mn_spec] * no + [ANY] * rn_out,
        out_shape=[jax.ShapeDtypeStruct((m, n), d) for d in out_dtypes] + (list(ride[2]) if ride else []),
        scratch_shapes=[pltpu.VMEM((tm, tn), F32) for _ in range(n_acc)] + (_dma_sems(ride[3]) if ride else []),
        compiler_params=pltpu.CompilerParams(
            dimension_semantics=("arbitrary",) * 3 if ride else ("parallel", "parallel", "arbitrary"),
            vmem_limit_bytes=_vmem_limit(blocks, temps)),
    )(a, *bs, *extras, *row_extras, *consts, *(ride[1] if ride else ()))
    if ride:
        return (outs[0] if no == 1 else outs[:no]), list(outs[no:])
    return outs[0] if no == 1 else outs


def _rowwise(body, *, name, rows, tr, row_ins, full_ins=(), row_outs=(), acc_outs=()):
    tr = min(tr, rows)
    assert rows % tr == 0
    n_ri, n_fi, n_ro = len(row_ins), len(full_ins), len(row_outs)

    def kern(*refs):
        body(pl.program_id(0), refs[:n_ri], refs[n_ri:n_ri + n_fi], refs[n_ri + n_fi:n_ri + n_fi + n_ro],
             refs[n_ri + n_fi + n_ro:])

    in_specs = [pl.BlockSpec((tr, w), functools.partial(lambda i, c: (i, c), c=ci)) for _, w, ci in row_ins]
    in_specs += [pl.BlockSpec(f.shape, lambda i: (0, 0)) for f in full_ins]
    out_specs = [pl.BlockSpec((tr, w), lambda i: (i, 0)) for w, _ in row_outs]
    out_specs += [pl.BlockSpec(s, lambda i: (0, 0)) for s, _ in acc_outs]
    out_shape = [jax.ShapeDtypeStruct((rows, w), d) for w, d in row_outs]
    out_shape += [jax.ShapeDtypeStruct(s, d) for s, d in acc_outs]
    blocks = (sum(_nbytes((tr, w), a.dtype) for a, w, _ in row_ins) + sum(_nbytes(f.shape, f.dtype) for f in full_ins)
              + sum(_nbytes((tr, w), d) for w, d in row_outs) + sum(_nbytes(s, d) for s, d in acc_outs))
    widest = max([w for _, w, _ in row_ins] + [w for w, _ in row_outs])
    outs = pl.pallas_call(
        kern, name=name, grid=(rows // tr,), in_specs=in_specs, out_specs=out_specs, out_shape=out_shape,
        compiler_params=pltpu.CompilerParams(
            dimension_semantics=("arbitrary",) if acc_outs else ("parallel",),
            vmem_limit_bytes=_vmem_limit(blocks, 8 * _nbytes((tr, widest), F32))),
    )(*[a for a, _, _ in row_ins], *full_ins)
    return outs


def _rms(x, g):
    r = lax.rsqrt(jnp.mean(x * x, axis=-1, keepdims=True) + EPS)
    return x * r * g


def _rms_bwd(x, g, dy):
    r = lax.rsqrt(jnp.mean(x * x, axis=-1, keepdims=True) + EPS)
    xh = x * r
    dxh = dy * g
    dx = r * (dxh - xh * jnp.mean(dxh * xh, axis=-1, keepdims=True))
    return dx, jnp.sum(dy * xh, axis=0, keepdims=True)


def _sigmoid(x):
    return 1.0 / (1.0 + jnp.exp(-x))


def _acc_init(i, refs):
    @pl.when(i == 0)
    def _():
        for r in refs:
            r[...] = jnp.zeros_like(r)


def _rms_fwd_call(x, g, name):
    rows, c = x.shape

    def body(i, ins, fulls, outs, accs):
        outs[0][...] = _rms(ins[0][...], fulls[0][...]).astype(BF16)

    return _rowwise(body, name=name, rows=rows, tr=512, row_ins=[(x, c, 0)], full_ins=[g], row_outs=[(c, BF16)])[0]


def _rms_bwd_call(x, g, dy, res, name):
    rows, c = x.shape
    row_ins = [(x, c, 0), (dy, c, 0)] + ([(res, c, 0)] if res is not None else [])

    def body(i, ins, fulls, outs, accs):
        _acc_init(i, accs)
        dx, dg = _rms_bwd(ins[0][...], fulls[0][...], ins[1][...].astype(F32))
        if res is not None:
            dx = dx + ins[2][...]
        outs[0][...] = dx
        accs[0][...] += dg

    return _rowwise(body, name=name, rows=rows, tr=512, row_ins=row_ins, full_ins=[g], row_outs=[(c, F32)],
                    acc_outs=[((1, c), F32)])


def _rope_tables(pos_col, freq_lane):
    rows = pos_col.shape[0]

    def body(i, ins, fulls, outs, accs):
        ang = ins[0][...].astype(F32) * fulls[0][...]
        lane = lax.broadcasted_iota(jnp.int32, ang.shape, 1)
        cos, sin = jnp.cos(ang), jnp.sin(ang)
        first = (lane >= ROPE_LO) & (lane < ROPE_LO + HALF)
        second = (lane >= ROPE_LO + HALF) & (lane < ROPE_LO + MLA_ROPE)
        outs[0][:, 0:LANE] = jnp.where(first | second, cos, 1.0)
        outs[0][:, LANE:2 * LANE] = jnp.where(first, -sin, 0.0)
        outs[0][:, 2 * LANE:3 * LANE] = jnp.where(second, sin, 0.0)

    return _rowwise(body, name="rope_tables", rows=rows, tr=1024, row_ins=[(pos_col, 1, 0)], full_ins=[freq_lane],
                    row_outs=[(3 * LANE, F32)])[0]


def _rope(x, tab):
    return (x * tab[:, 0:LANE] + pltpu.roll(x, LANE - HALF, 1) * tab[:, LANE:2 * LANE]
            + pltpu.roll(x, HALF, 1) * tab[:, 2 * LANE:3 * LANE])


def _rope_t(dy, tab):
    return (dy * tab[:, 0:LANE] + pltpu.roll(dy * tab[:, LANE:2 * LANE], HALF, 1)
            + pltpu.roll(dy * tab[:, 2 * LANE:3 * LANE], LANE - HALF, 1))


def _mla_prep_fwd(proj, tab, g_q, g_kv):
    rows = proj.shape[0]

    def body(i, ins, fulls, outs, accs):
        outs[0][...] = _rms(ins[0][...].astype(F32), fulls[0][...]).astype(BF16)
        outs[1][...] = _rms(ins[1][...].astype(F32), fulls[1][...]).astype(BF16)
        outs[2][...] = _rope(ins[2][...].astype(F32), ins[3][...])

    return _rowwise(body, name="mla_prep_fwd", rows=rows, tr=512,
                    row_ins=[(proj, MLA_Q_RANK, C_CQ // MLA_Q_RANK), (proj, LANE, C_CKV // LANE),
                             (proj, LANE, C_KR // LANE), (tab, 3 * LANE, 0)],
                    full_ins=[g_q, g_kv], row_outs=[(MLA_Q_RANK, BF16), (MLA_KV_RANK, BF16), (LANE, F32)])


def _mla_prep_bwd(proj, g_q, g_kv, dcqn, dckvn):
    rows = proj.shape[0]

    def body(i, ins, fulls, outs, accs):
        _acc_init(i, accs)
        dcq, dgq = _rms_bwd(ins[0][...].astype(F32), fulls[0][...], ins[2][...])
        dckv, dgkv = _rms_bwd(ins[1][...].astype(F32), fulls[1][...], ins[3][...])
        outs[0][...] = dcq.astype(BF16)
        outs[1][...] = dckv.astype(BF16)
        accs[0][...] += dgq
        accs[1][...] += dgkv

    return _rowwise(body, name="mla_prep_bwd", rows=rows, tr=512,
                    row_ins=[(proj, MLA_Q_RANK, C_CQ // MLA_Q_RANK), (proj, LANE, C_CKV // LANE),
                             (dcqn, MLA_Q_RANK, 0), (dckvn, MLA_KV_RANK, 0)],
                    full_ins=[g_q, g_kv], row_outs=[(MLA_Q_RANK, BF16), (MLA_KV_RANK, BF16)],
                    acc_outs=[((1, MLA_Q_RANK), F32), ((1, MLA_KV_RANK), F32)])


def _per_head(fn, x):
    return jnp.concatenate([fn(x[:, h * LANE:(h + 1) * LANE]) for h in range(x.shape[1] // LANE)], axis=1)


def _mla_rope_bwd(dk, dv, tab):
    rows = dk.shape[0]
    hw = MLA_HEADS * LANE

    def body(i, ins, fulls, outs, accs):
        dkr = jnp.zeros((ins[0].shape[0], LANE), F32)
        for h in range(MLA_HEADS):
            dkr = dkr + ins[0][:, h * LANE:(h + 1) * LANE]
        outs[0][:, 0:hw] = ins[0][...].astype(BF16)
        outs[0][:, hw:2 * hw] = ins[1][...].astype(BF16)
        lane = lax.broadcasted_iota(jnp.int32, dkr.shape, 1)
        dkr = jnp.where((lane >= ROPE_LO) & (lane < ROPE_LO + MLA_ROPE), dkr, 0.0)
        outs[1][...] = _rope_t(dkr, ins[2][...]).astype(BF16)

    return _rowwise(body, name="mla_rope_bwd", rows=rows, tr=512,
                    row_ins=[(dk, hw, 0), (dv, hw, 0), (tab, 3 * LANE, 0)], row_outs=[(2 * hw, BF16), (LANE, BF16)])


def _dot_nt(a, b):
    return lax.dot_general(a, b, (((1,), (1,)), ((), ())), preferred_element_type=F32)


def _dot_tn(a, b):
    return lax.dot_general(a, b, (((0,), (0,)), ((), ())), preferred_element_type=F32)


def _dot(a, b):
    return jnp.dot(a, b, preferred_element_type=F32)


def _attn_params(s, t, n_res_f32, n_res_bf16, ride=False):
    blocks = n_res_f32 * _nbytes((s, LANE), F32) + n_res_bf16 * _nbytes((s, LANE), BF16) + 6 * _nbytes((t, LANE), F32)
    return pltpu.CompilerParams(dimension_semantics=("arbitrary" if ride else "parallel", "arbitrary"),
                                vmem_limit_bytes=_vmem_limit(blocks, 12 * _nbytes((t, t), F32)))


def _mla_fwd(q, k, v, t, tk, shards=()):
    s, hw = q.shape
    heads, nq, r = hw // LANE, s // t, t // tk
    ng = len(shards)

    def body(q_ref, k_ref, v_ref, *rest):
        o_ref, l_ref = rest[ng], rest[ng + 1]
        h, i = pl.program_id(0), pl.program_id(1)
        if ng:
            gather = _GatherPhases(rest[:ng], rest[ng + 2:2 * ng + 2], *rest[2 * ng + 2:])
            pl.when((h == 0) & (i == 0))(gather.send)
            pl.when((h == heads // 2) & (i == 0))(gather.forward)
        qv = q_ref[...]

        def step(j, carry, off):
            m, l, acc = carry
            sl = pl.ds(pl.multiple_of(j * tk, tk), tk)
            sc = _dot_nt(qv, k_ref[sl, :])
            if off is not None:
                row = lax.broadcasted_iota(jnp.int32, (t, tk), 0)
                col = lax.broadcasted_iota(jnp.int32, (t, tk), 1)
                sc = jnp.where(col + off <= row, sc, -1e30)
            m_new = jnp.maximum(m, jnp.max(sc, axis=1, keepdims=True))
            p = jnp.exp2(sc - m_new)
            alpha = jnp.exp2(m - m_new)
            l = alpha * l + jnp.sum(p, axis=1, keepdims=True)
            acc = alpha * acc + _dot(p.astype(BF16), v_ref[sl, :])
            return m_new, l, acc

        init = (jnp.full((t, 1), -1e30, F32), jnp.zeros((t, 1), F32), jnp.zeros((t, LANE), F32))
        below = i * r
        carry = lax.fori_loop(0, below // 2, lambda j, c: step(2 * j + 1, step(2 * j, c, None), None), init)
        carry = lax.fori_loop(below // 2 * 2, below, lambda j, c: step(j, c, None), carry)
        for jj in range(r):
            carry = step(i * r + jj, carry, jj * tk)
        m, l, acc = carry
        o_ref[...] = (acc / l).astype(o_ref.dtype)
        l_ref[0] = m + jnp.log2(l)
        if ng:
            pl.when((h == heads - 1) & (i == nq - 1))(gather.finish)

    blk = pl.BlockSpec((t, LANE), lambda h, i: (i, h))
    res = pl.BlockSpec((s, LANE), lambda h, i: (0, h))
    outs = pl.pallas_call(
        body, name="mla_fwd", grid=(heads, nq), in_specs=[blk, res, res] + [ANY] * ng,
        out_specs=[blk, pl.BlockSpec((1, t, 1), lambda h, i: (h, i, 0))] + [ANY] * ng,
        out_shape=[jax.ShapeDtypeStruct((s, hw), BF16), jax.ShapeDtypeStruct((heads, s, 1), F32)]
        + [jax.ShapeDtypeStruct((N_CHIPS,) + sh.shape, sh.dtype) for sh in shards],
        scratch_shapes=_dma_sems(6 * ng) + [pltpu.SemaphoreType.DMA((ng,))] if ng else [],
        compiler_params=_attn_params(s, t, 0, 2, ride=ng > 0),
    )(q, k, v, *shards)
    return outs[0], outs[1], list(outs[2:])


def _mla_bwd(q, k, v, o, do, lse, tab, t, tk, parts=()):
    s, hw = q.shape
    heads, nq, r = hw // LANE, s // t, t // tk
    ns = len(parts)

    def body(q_ref, k_ref, v_ref, o_ref, do_ref, l_ref, tab_ref, *rest):
        dq_ref, dk_ref, dv_ref = rest[ns:ns + 3]
        h, i = pl.program_id(0), pl.program_id(1)
        finish = _ride_along(_scatter_copies, ns, (rest[:ns], rest[ns + 3:2 * ns + 3], *rest[2 * ns + 3:]),
                             (h == 0) & (i == 0), (h == heads - 1) & (i == nq - 1))

        @pl.when(i == 0)
        def _():
            dk_ref[...] = jnp.zeros_like(dk_ref)
            dv_ref[...] = jnp.zeros_like(dv_ref)

        qv, dov, lv = q_ref[...], do_ref[...], l_ref[0]
        dlt = jnp.sum(dov.astype(F32) * o_ref[...].astype(F32), axis=1, keepdims=True)

        def step(j, dq, off):
            sl = pl.ds(pl.multiple_of(j * tk, tk), tk)
            kv, vv = k_ref[sl, :], v_ref[sl, :]
            p = jnp.exp2(_dot_nt(qv, kv) - lv)
            if off is not None:
                row = lax.broadcasted_iota(jnp.int32, (t, tk), 0)
                col = lax.broadcasted_iota(jnp.int32, (t, tk), 1)
                p = jnp.where(col + off <= row, p, 0.0)
            ds = (p * (_dot_nt(dov, vv) - dlt)).astype(BF16)
            dk_ref[sl, :] += _dot_tn(ds, qv) * (1.0 / LOG2E)
            dv_ref[sl, :] += _dot_tn(p.astype(BF16), dov)
            return dq + _dot(ds, kv)

        dq = lax.fori_loop(0, i * r, lambda j, c: step(j, c, None), jnp.zeros((t, LANE), F32))
        for jj in range(r):
            dq = step(i * r + jj, dq, jj * tk)
        dq_ref[...] = _rope_t(dq * MLA_SCALE, tab_ref[...]).astype(dq_ref.dtype)
        finish()

    blk = pl.BlockSpec((t, LANE), lambda h, i: (i, h))
    res = pl.BlockSpec((s, LANE), lambda h, i: (0, h))
    full = jax.ShapeDtypeStruct((s, hw), F32)
    outs = pl.pallas_call(
        body, name="mla_bwd", grid=(heads, nq),
        in_specs=[blk, res, res, blk, blk, pl.BlockSpec((1, t, 1), lambda h, i: (h, i, 0)),
                  pl.BlockSpec((t, 3 * LANE), lambda h, i: (i, 0))] + [ANY] * ns,
        out_specs=[blk, res, res] + [ANY] * ns,
        out_shape=[jax.ShapeDtypeStruct((s, hw), BF16), full, full] + _scatter_shapes(parts),
        scratch_shapes=_dma_sems(3 * ns) if ns else [],
        compiler_params=_attn_params(s, t, 2, 2, ride=ns > 0),
    )(q, k, v, o, do, lse, tab, *parts)
    return outs[0], outs[1], outs[2], list(outs[3:])


def _sb_logits(qv, kv, keep, upper):
    z = _dot_nt(qv, kv)
    e = jnp.exp(-jnp.abs(z))
    l1p = jnp.log(1.0 + e)
    lb = jnp.minimum(z, 0.0) - l1p
    lo = -jnp.maximum(z, 0.0) - l1p
    if keep is not None:
        lo = jnp.where(keep, lo, 0.0)
    hi = lo.astype(BF16)
    rem = (lo - hi.astype(F32)).astype(BF16)
    suf = _dot(hi, upper) + _dot(rem, upper)
    return z, e, lb, lo, suf


def _tri(t, inclusive):
    row = lax.broadcasted_iota(jnp.int32, (t, t), 0)
    col = lax.broadcasted_iota(jnp.int32, (t, t), 1)
    return jnp.where((row >= col) if inclusive else (row > col), 1.0, 0.0).astype(BF16)


SB_QBLOCKS = 2
SB_PAIR = LANE // SB_HEAD_DIM
SB_CHAINS = SB_QBLOCKS * SB_PAIR
SB_FIRST = 2


def _sb_first_tile(b, t):
    start = jnp.maximum(b - (SB_FIRST - 1), 0) * t
    row = lax.broadcasted_iota(jnp.int32, (t, SB_FIRST * t), 0)
    col = lax.broadcasted_iota(jnp.int32, (t, SB_FIRST * t), 1)
    return pl.ds(pl.multiple_of(start, t), SB_FIRST * t), col + start < row + b * t


def _sb_head(x, hh):
    lane = lax.broadcasted_iota(jnp.int32, x.shape, 1)
    return jnp.where(lane // SB_HEAD_DIM == hh, x, jnp.zeros_like(x))


def _sb_walk(i, first, carries_of):
    n = SB_CHAINS
    carries = [first(c) for c in range(n)]
    width = len(carries[0])

    def alive(carry):
        return jnp.max(carry[0]) >= SB_ZERO_LOG

    def split(st):
        return [tuple(st[1 + c * width:1 + (c + 1) * width]) for c in range(n)]

    def live(st):
        any_alive = alive(split(st)[0])
        for cr in split(st)[1:]:
            any_alive = any_alive | alive(cr)
        return (st[0] <= SB_QBLOCKS * i) & any_alive

    def more(st):
        out = (st[0] + 1,)
        for c, cr in enumerate(split(st)):
            out += tuple(carries_of(c, st[0], cr))
        return out

    st = lax.while_loop(live, more, (jnp.int32(SB_FIRST),) + tuple(x for cr in carries for x in cr))
    jj, carries = st[0], split(st)
    for c in range(SB_PAIR, n):
        def live_c(s2, c=c):
            return (s2[0] <= SB_QBLOCKS * i + c // SB_PAIR) & alive(s2[1:])

        def more_c(s2, c=c):
            return (s2[0] + 1,) + tuple(carries_of(c, s2[0], s2[1:]))

        carries[c] = lax.while_loop(live_c, more_c, (jj,) + tuple(carries[c]))[1:]
    return carries


def _sb_fwd(proj, t):
    s = proj.shape[0]
    pairs, nq, nb = SB_HEADS // SB_PAIR, s // t, SB_QBLOCKS

    def body(q_ref, k_ref, v_ref, o_ref):
        i = pl.program_id(1)
        upper, upper_first = _tri(t, False), _tri(SB_FIRST * t, False)
        rows = [slice((c // SB_PAIR) * t, (c // SB_PAIR + 1) * t) for c in range(SB_CHAINS)]
        qs = [_sb_head(q_ref[rows[c], :] * SB_SCALE, c % SB_PAIR).astype(BF16) for c in range(SB_CHAINS)]

        def first(c):
            sl, keep = _sb_first_tile(nb * i + c // SB_PAIR, t)
            _, _, lb, lo, suf = _sb_logits(qs[c], k_ref[sl, :].astype(BF16), keep, upper_first)
            a = jnp.where(keep, jnp.exp(lb + suf), 0.0)
            return (jnp.sum(lo, axis=1, keepdims=True),
                    _dot(a.astype(BF16), _sb_head(v_ref[sl, :], c % SB_PAIR).astype(BF16)))

        def step(c, jj, carry):
            run, acc = carry
            sl = pl.ds(pl.multiple_of((nb * i + c // SB_PAIR - jj) * t, t), t)
            _, _, lb, lo, suf = _sb_logits(qs[c], k_ref[sl, :].astype(BF16), None, upper)
            a = jnp.exp(lb + suf + run)
            acc = acc + _dot(a.astype(BF16), _sb_head(v_ref[sl, :], c % SB_PAIR).astype(BF16))
            return run + jnp.sum(lo, axis=1, keepdims=True), acc

        carries = _sb_walk(i, first, step)
        for qb in range(nb):
            o_ref[qb * t:(qb + 1) * t, :] = sum(carries[qb * SB_PAIR + hh][1] for hh in range(SB_PAIR))

    return pl.pallas_call(
        body, name="sb_fwd", grid=(pairs, nq // nb),
        in_specs=[pl.BlockSpec((nb * t, LANE), lambda h, i: (i, C_SBQ // LANE + h)),
                  pl.BlockSpec((s, LANE), lambda h, i: (0, C_SBK // LANE + h)),
                  pl.BlockSpec((s, LANE), lambda h, i: (0, C_SBV // LANE + h))],
        out_specs=pl.BlockSpec((nb * t, LANE), lambda h, i: (i, h)),
        out_shape=jax.ShapeDtypeStruct((s, pairs * LANE), F32),
        compiler_params=_attn_params(s, nb * t, 0, 2),
    )(proj, proj, proj)


def _sb_bwd(proj, o, do, t, stacked=()):
    s = proj.shape[0]
    pairs, nq, nb = SB_HEADS // SB_PAIR, s // t, SB_QBLOCKS
    nx = len(stacked)

    def body(q_ref, k_ref, v_ref, o_ref, do_ref, *rest):
        dq_ref, dk_ref, dv_ref = rest[nx:nx + 3]
        hd, i = pl.program_id(0), pl.program_id(1)
        finish = _ride_along(_exchange_copies, nx, (rest[:nx], rest[nx + 3:2 * nx + 3], *rest[2 * nx + 3:]),
                             (hd == 0) & (i == 0), (hd == pairs - 1) & (i == nq // nb - 1))

        @pl.when(i == 0)
        def _():
            dk_ref[...] = jnp.zeros_like(dk_ref)
            dv_ref[...] = jnp.zeros_like(dv_ref)

        rows = [slice((c // SB_PAIR) * t, (c // SB_PAIR + 1) * t) for c in range(SB_CHAINS)]
        qs = [_sb_head(q_ref[rows[c], :] * SB_SCALE, c % SB_PAIR).astype(BF16) for c in range(SB_CHAINS)]
        dos = [_sb_head(do_ref[rows[c], :], c % SB_PAIR) for c in range(SB_CHAINS)]
        totals = [jnp.sum(dos[c].astype(F32) * o_ref[rows[c], :], axis=1, keepdims=True) for c in range(SB_CHAINS)]
        tris = {1: (_tri(t, False), _tri(t, True)), SB_FIRST: (_tri(SB_FIRST * t, False), _tri(SB_FIRST * t, True))}

        def tile(c, sl, keep, blocks, carry):
            run, g, dq = carry
            qv, dov = qs[c], dos[c]
            upper, upper_incl = tris[blocks]
            kv, vv = k_ref[sl, :].astype(BF16), v_ref[sl, :].astype(BF16)
            z, e, lb, lo, suf = _sb_logits(qv, kv, keep, upper)
            tail = suf + run
            a = jnp.exp(lb + tail)
            if keep is not None:
                a = jnp.where(keep, a, 0.0)
            ab = a.astype(BF16)
            gr = ab.astype(F32) * _dot_nt(dov, vv)
            ghi = gr.astype(BF16)
            grem = (gr - ghi.astype(F32)).astype(BF16)
            before = totals[c] - g - (_dot(ghi, upper_incl) + _dot(grem, upper_incl))
            before = jnp.where(tail < SB_ZERO_LOG, 0.0, before)
            r = 1.0 / (1.0 + e)
            pos = z >= 0.0
            dz = r * (gr * jnp.where(pos, e, 1.0) - before * jnp.where(pos, 1.0, e))
            if keep is not None:
                dz = jnp.where(keep, dz, 0.0)
            dzb = dz.astype(BF16)
            dk_ref[sl, :] += _dot_tn(dzb, qv)
            dv_ref[sl, :] += _dot_tn(ab, dov)
            return (run + jnp.sum(lo, axis=1, keepdims=True), g + jnp.sum(gr, axis=1, keepdims=True),
                    dq + _dot(dzb, _sb_head(kv, c % SB_PAIR)))

        zero = jnp.zeros((t, 1), F32)
        init = (zero, zero, jnp.zeros((t, LANE), F32))

        def first(c):
            sl, keep = _sb_first_tile(nb * i + c // SB_PAIR, t)
            return tile(c, sl, keep, SB_FIRST, init)

        def step(c, jj, carry):
            return tile(c, pl.ds(pl.multiple_of((nb * i + c // SB_PAIR - jj) * t, t), t), None, 1, carry)

        carries = _sb_walk(i, first, step)
        for qb in range(nb):
            dq_ref[qb * t:(qb + 1) * t, :] = sum(carries[qb * SB_PAIR + hh][2] for hh in range(SB_PAIR)) * SB_SCALE
        finish()

    blk = pl.BlockSpec((nb * t, LANE), lambda h, i: (i, h))
    res = pl.BlockSpec((s, LANE), lambda h, i: (0, h))
    full = jax.ShapeDtypeStruct((s, pairs * LANE), F32)
    outs = pl.pallas_call(
        body, name="sb_bwd", grid=(pairs, nq // nb),
        in_specs=[pl.BlockSpec((nb * t, LANE), lambda h, i: (i, C_SBQ // LANE + h)),
                  pl.BlockSpec((s, LANE), lambda h, i: (0, C_SBK // LANE + h)),
                  pl.BlockSpec((s, LANE), lambda h, i: (0, C_SBV // LANE + h)), blk, blk] + [ANY] * nx,
        out_specs=[blk, res, res] + [ANY] * nx, out_shape=[full, full, full] + _exchange_shapes(stacked),
        scratch_shapes=_dma_sems(nx) if nx else [],
        compiler_params=_attn_params(s, nb * t, 2, 2, ride=nx > 0),
    )(proj, proj, proj, o, do, *stacked)
    return outs[0], outs[1], outs[2], list(outs[3:])


def _xattn_probs(qh, kh):
    sc = _dot_nt(qh, kh) * (1.0 / math.sqrt(X_HEAD_DIM))
    p = jnp.exp(sc - jnp.max(sc, axis=1, keepdims=True))
    return p / jnp.sum(p, axis=1, keepdims=True)


def _xattn_fwd(xq, xkv):
    rows = xq.shape[0]
    w = X_HEADS * X_HEAD_DIM

    def body(i, ins, fulls, outs, accs):
        for h in range(X_HEADS):
            sl = slice(h * LANE, (h + 1) * LANE)
            p = _xattn_probs(ins[0][:, sl], fulls[0][:, sl])
            outs[0][:, sl] = _dot(p.astype(BF16), fulls[0][:, w + h * LANE:w + (h + 1) * LANE]).astype(BF16)

    return _rowwise(body, name="xattn_fwd", rows=rows, tr=512, row_ins=[(xq, w, 0)], full_ins=[xkv],
                    row_outs=[(w, BF16)])[0]


def _xattn_bwd(xq, xkv, dxo):
    rows = xq.shape[0]
    w = X_HEADS * X_HEAD_DIM

    def body(i, ins, fulls, outs, accs):
        _acc_init(i, accs)
        for h in range(X_HEADS):
            sl = slice(h * LANE, (h + 1) * LANE)
            slv = slice(w + h * LANE, w + (h + 1) * LANE)
            qh, kh, vh, doh = ins[0][:, sl], fulls[0][:, sl], fulls[0][:, slv], ins[1][:, sl]
            p = _xattn_probs(qh, kh)
            dp = _dot_nt(doh, vh)
            ds = (p * (dp - jnp.sum(p * dp, axis=1, keepdims=True)) * (1.0 / math.sqrt(X_HEAD_DIM))).astype(BF16)
            outs[0][:, sl] = _dot(ds, kh).astype(BF16)
            accs[0][:, sl] += _dot_tn(ds, qh)
            accs[0][:, slv] += _dot_tn(p.astype(BF16), doh)

    return _rowwise(body, name="xattn_bwd", rows=rows, tr=512, row_ins=[(xq, w, 0), (dxo, w, 0)], full_ins=[xkv],
                    row_outs=[(w, BF16)], acc_outs=[(xkv.shape, F32)])


def _gate_fwd(proj, pa, pb, b_gate):
    rows = proj.shape[0]

    def body(i, ins, fulls, outs, accs):
        sa = _sigmoid(ins[0][...].astype(F32) + fulls[0][0:1, :])
        sb = _sigmoid(ins[1][...].astype(F32) + fulls[0][1:2, :])
        outs[0][...] = (sa * ins[2][...].astype(F32) + sb * ins[3][...].astype(F32)).astype(BF16)

    return _rowwise(body, name="gate_fwd", rows=rows, tr=512,
                    row_ins=[(proj, D_MODEL, C_GA // D_MODEL), (proj, D_MODEL, C_GB // D_MODEL), (pa, D_MODEL, 0),
                             (pb, D_MODEL, 0)],
                    full_ins=[b_gate], row_outs=[(D_MODEL, BF16)])[0]


def _gate_bwd(proj, pa, pb, b_gate, dm):
    rows = proj.shape[0]

    def body(i, ins, fulls, outs, accs):
        _acc_init(i, accs)
        d = ins[4][...].astype(F32)
        sa = _sigmoid(ins[0][...].astype(F32) + fulls[0][0:1, :])
        sb = _sigmoid(ins[1][...].astype(F32) + fulls[0][1:2, :])
        dga = d * ins[2][...].astype(F32) * sa * (1.0 - sa)
        dgb = d * ins[3][...].astype(F32) * sb * (1.0 - sb)
        outs[0][...] = (d * sa).astype(BF16)
        outs[1][...] = (d * sb).astype(BF16)
        outs[2][...] = dga.astype(BF16)
        outs[3][...] = dgb.astype(BF16)
        accs[0][0:1, :] += jnp.sum(dga, axis=0, keepdims=True)
        accs[0][1:2, :] += jnp.sum(dgb, axis=0, keepdims=True)

    return _rowwise(body, name="gate_bwd", rows=rows, tr=512,
                    row_ins=[(proj, D_MODEL, C_GA // D_MODEL), (proj, D_MODEL, C_GB // D_MODEL), (pa, D_MODEL, 0),
                             (pb, D_MODEL, 0), (dm, D_MODEL, 0)],
                    full_ins=[b_gate], row_outs=[(D_MODEL, BF16)] * 4, acc_outs=[((2, D_MODEL), F32)])


def _loss_head(x3, target, g_final):
    rows = x3.shape[0]

    def body(i, ins, fulls, outs, accs):
        _acc_init(i, accs)
        xv, g = ins[0][...], fulls[0][...]
        d = _rms(xv, g) - ins[1][...]
        dx, dg = _rms_bwd(xv, g, d * (1.0 / D_MODEL))
        outs[0][...] = dx
        accs[0][...] += dg
        accs[1][...] += jnp.sum(d * d, axis=0, keepdims=True)

    return _rowwise(body, name="loss_head", rows=rows, tr=512, row_ins=[(x3, D_MODEL, 0), (target, D_MODEL, 0)],
                    full_ins=[g_final], row_outs=[(D_MODEL, F32)], acc_outs=[((1, D_MODEL), F32), ((1, D_MODEL), F32)])


def _adamw(w, g, m, v, name):
    rows, c = w.shape

    def body(i, ins, fulls, outs, accs):
        wv, gv = ins[0][...], ins[1][...]
        mn = ADAM_B1 * ins[2][...] + (1.0 - ADAM_B1) * gv
        vn = ADAM_B2 * ins[3][...] + (1.0 - ADAM_B2) * jnp.square(gv)
        m_hat = mn / (1.0 - ADAM_B1 ** ADAM_STEP)
        v_hat = vn / (1.0 - ADAM_B2 ** ADAM_STEP)
        outs[0][...] = -ADAM_LR * (m_hat / (jnp.sqrt(v_hat) + ADAM_EPS) + ADAM_WD * wv)
        outs[1][...] = mn
        outs[2][...] = vn

    return _rowwise(body, name=name, rows=rows, tr=_row_tile(rows, 256), row_ins=[(a, c, 0) for a in (w, g, m, v)],
                    row_outs=[(c, F32)] * 3)


def _place():
    x, y, c = lax.axis_index("x"), lax.axis_index("y"), lax.axis_index("c")
    chips = [(1 - x, y), (x, 1 - y), (1 - x, 1 - y)]
    return x, y, c, chips


ANY = pl.BlockSpec(memory_space=pl.ANY)


def _remote(src, dst, send_sem, recv_sem, to):
    return pltpu.make_async_remote_copy(src_ref=src, dst_ref=dst, send_sem=send_sem, recv_sem=recv_sem,
                                        device_id=to, device_id_type=MESH)


def _dma_sems(n):
    return [pltpu.SemaphoreType.DMA((n,)), pltpu.SemaphoreType.DMA((n,))]


class _GatherPhases:
    def __init__(self, ins, outs, send_sems, recv_sems, local_sems=None):
        self.ins, self.outs, self.send_sems, self.recv_sems = ins, outs, send_sems, recv_sems
        self.local_sems = local_sems
        self.x, self.y, self.c, self.chips = _place()
        self.me = 2 * self.x + self.y

    def _locals(self):
        if self.local_sems is None:
            return []
        return [pltpu.make_async_copy(self.ins[t], self.outs[t].at[self.me], self.local_sems.at[t])
                for t in range(len(self.ins))]

    def _copy(self, t, j, chip_idx, hlf, to, src=None):
        h = self.ins[t].shape[0] // 2
        dst = self.outs[t].at[chip_idx, pl.ds(hlf * h, h), :]
        return _remote(dst if src is None else src, dst, self.send_sems.at[6 * t + j], self.recv_sems.at[6 * t + j], to)

    def _sends(self):
        out = []
        for t in range(len(self.ins)):
            h = self.ins[t].shape[0] // 2
            for j, chip in enumerate(self.chips):
                out.append(self._copy(t, j, self.me, self.c, (*chip, self.c), src=self.ins[t].at[pl.ds(self.c * h, h), :]))
        return out

    def _forwards(self):
        return [self._copy(t, 3 + j, 2 * chip[0] + chip[1], self.c, (self.x, self.y, 1 - self.c))
                for t in range(len(self.ins)) for j, chip in enumerate(self.chips)]

    def send(self):
        for cp in self._sends() + self._locals():
            cp.start()

    def forward(self):
        here = (self.x, self.y, self.c)
        landed = [self._copy(t, j, 2 * chip[0] + chip[1], self.c, here)
                  for t in range(len(self.ins)) for j, chip in enumerate(self.chips)]
        for arrival, fwd in zip(landed, self._forwards()):
            arrival.wait_recv()
            fwd.start()

    def finish(self):
        here = (self.x, self.y, self.c)
        for t in range(len(self.ins)):
            for j, chip in enumerate(self.chips):
                self._copy(t, 3 + j, 2 * chip[0] + chip[1], 1 - self.c, here).wait_recv()
        for cp in self._sends() + self._forwards():
            cp.wait_send()
        for cp in self._locals():
            cp.wait()


def _all_gather_weights(shards):
    n = len(shards)

    def body(*refs):
        gather = _GatherPhases(refs[:n], refs[n:2 * n], *refs[2 * n:])
        gather.send()
        gather.forward()
        gather.finish()

    return pl.pallas_call(
        body, name="all_gather_weights", in_specs=[ANY] * n, out_specs=[ANY] * n,
        out_shape=[jax.ShapeDtypeStruct((N_CHIPS,) + s.shape, s.dtype) for s in shards],
        scratch_shapes=_dma_sems(6 * n),
    )(*shards)


def _exchange_copies(ins, outs, send_sems, recv_sems):
    x, y, c, _ = _place()
    cps = []
    for t in range(len(ins)):
        h = ins[t].shape[1] // 2
        cps.append(_remote(ins[t].at[:, pl.ds((1 - c) * h, h), :], outs[t], send_sems.at[t], recv_sems.at[t],
                           (x, y, 1 - c)))
    return cps


def _exchange_shapes(stacked):
    return [jax.ShapeDtypeStruct((N_CHIPS, s.shape[1] // 2, s.shape[2]), s.dtype) for s in stacked]


def _scatter_copies(ins, outs, send_sems, recv_sems):
    x, y, c, chips = _place()
    return [_remote(ins[t].at[2 * chip[0] + chip[1]], outs[t].at[j], send_sems.at[3 * t + j], recv_sems.at[3 * t + j],
                    (*chip, c)) for t in range(len(ins)) for j, chip in enumerate(chips)]


def _scatter_shapes(parts):
    return [jax.ShapeDtypeStruct((N_CHIPS - 1,) + s.shape[1:], s.dtype) for s in parts]


def _direct_copies(ins, outs, send_sems, recv_sems):
    x, y, k, chips = _place()
    me = 2 * x + y
    n = len(outs) // 2
    starts, arrivals = [], []
    for t in range(n):
        full, half_prec, landed32, landed16 = ins[t], ins[n + t], outs[t], outs[n + t]
        h = full.shape[1] // 2
        base = 7 * t
        pair = _remote(full.at[me, pl.ds((1 - k) * h, h), :], landed32, send_sems.at[base], recv_sems.at[base],
                       (x, y, 1 - k))
        starts.append(pair)
        arrivals.append(pair)
        for j, chip in enumerate(chips):
            for core in range(2):
                starts.append(_remote(half_prec.at[2 * chip[0] + chip[1], pl.ds(core * h, h), :], landed16.at[2 * j + k],
                                      send_sems.at[base + 1 + 2 * j + core], recv_sems.at[base + 1 + 2 * j + k],
                                      (*chip, core)))
                arrivals.append(_remote(half_prec.at[0, pl.ds(0, h), :], landed16.at[2 * j + core],
                                        send_sems.at[base + 1 + 2 * j + core], recv_sems.at[base + 1 + 2 * j + core],
                                        (x, y, k)))
    return starts, arrivals


def _direct_shapes(stacked):
    half = lambda s: (s.shape[1] // 2, s.shape[2])
    return ([jax.ShapeDtypeStruct(half(s), F32) for s in stacked]
            + [jax.ShapeDtypeStruct((6,) + half(s), BF16) for s in stacked])


def _ride_along(copies_of, n, refs, first, last):
    if not n:
        return lambda: None

    def start():
        built = copies_of(*refs)
        for cp in (built[0] if isinstance(built, tuple) else built):
            cp.start()

    def wait():
        built = copies_of(*refs)
        if not isinstance(built, tuple):
            for cp in built:
                cp.wait()
            return
        for cp in built[0]:
            cp.wait_send()
        for cp in built[1]:
            cp.wait_recv()

    pl.when(first)(start)
    return lambda: pl.when(last)(wait)


def _pair_exchange_halves(shards):
    n = len(shards)

    def body(*refs):
        bufs = refs[n:2 * n]
        send_sems, recv_sems = refs[2 * n:]
        x, y, c, _ = _place()
        cps = []
        for t in range(n):
            h = bufs[t].shape[0] // 2
            rows = bufs[t].at[pl.ds(c * h, h), :]
            cps.append(_remote(rows, rows, send_sems.at[t], recv_sems.at[t], (x, y, 1 - c)))
            cps[-1].start()
        for cp in cps:
            cp.wait()

    return pl.pallas_call(
        body, name="pair_exchange_halves", in_specs=[ANY] * n, out_specs=[ANY] * n,
        out_shape=[jax.ShapeDtypeStruct(s.shape, s.dtype) for s in shards],
        input_output_aliases={t: t for t in range(n)},
        scratch_shapes=_dma_sems(n),
    )(*shards)


def _pair_sum(gs, recv, place, name):
    _, r, cols = gs.shape
    h = r // 2

    def kern(p_ref, a_ref, b_ref, o_ref):
        o_ref[...] = (a_ref[...] + b_ref[...]).astype(BF16)

    blk = lambda f: pl.BlockSpec((1, h, cols), f)
    return pl.pallas_call(
        kern, name=name,
        grid_spec=pltpu.PrefetchScalarGridSpec(
            num_scalar_prefetch=1, grid=(N_CHIPS,),
            in_specs=[blk(lambda d, p: (d, p[1], 0)), blk(lambda d, p: (d, 0, 0))],
            out_specs=blk(lambda d, p: (d, 0, 0))),
        out_shape=jax.ShapeDtypeStruct((N_CHIPS, h, cols), BF16),
        compiler_params=pltpu.CompilerParams(dimension_semantics=("arbitrary",),
                                             vmem_limit_bytes=_vmem_limit(3 * _nbytes((h, cols), F32), 0)),
    )(place, gs, recv)


def _chip_sum(gs, recv, got, place, name):
    _, r, cols = gs.shape
    h = r // 2

    def kern(p_ref, a_ref, b_ref, g0, g1, g2, o_ref):
        own = a_ref[0] + b_ref[0]
        o_ref[...] = ((own + g0[0].astype(F32)) + g1[0].astype(F32)) + g2[0].astype(F32)

    blk = lambda f: pl.BlockSpec((1, h, cols), f)
    return pl.pallas_call(
        kern, name=name,
        grid_spec=pltpu.PrefetchScalarGridSpec(
            num_scalar_prefetch=1, grid=(1,),
            in_specs=[blk(lambda i, p: (p[0], p[1], 0)), blk(lambda i, p: (p[0], 0, 0)), blk(lambda i, p: (0, 0, 0)),
                      blk(lambda i, p: (1, 0, 0)), blk(lambda i, p: (2, 0, 0))],
            out_specs=pl.BlockSpec((h, cols), lambda i, p: (p[1], 0))),
        out_shape=jax.ShapeDtypeStruct((r, cols), F32),
        compiler_params=pltpu.CompilerParams(dimension_semantics=("arbitrary",),
                                             vmem_limit_bytes=_vmem_limit(5 * _nbytes((h, cols), F32), 0)),
    )(place, gs, recv, got, got, got)


def _chip_sum_direct(gs, landed32, landed16, place, name):
    _, r, cols = gs.shape
    h = r // 2

    def kern(p_ref, a_ref, b_ref, g_ref, o_ref):
        acc = a_ref[0] + b_ref[...]
        for slot in range(6):
            acc = acc + g_ref[slot].astype(F32)
        o_ref[...] = acc

    return pl.pallas_call(
        kern, name=name,
        grid_spec=pltpu.PrefetchScalarGridSpec(
            num_scalar_prefetch=1, grid=(1,),
            in_specs=[pl.BlockSpec((1, h, cols), lambda i, p: (p[0], p[1], 0)),
                      pl.BlockSpec((h, cols), lambda i, p: (0, 0)), pl.BlockSpec((6, h, cols), lambda i, p: (0, 0, 0))],
            out_specs=pl.BlockSpec((h, cols), lambda i, p: (p[1], 0))),
        out_shape=jax.ShapeDtypeStruct((r, cols), F32),
        compiler_params=pltpu.CompilerParams(dimension_semantics=("arbitrary",),
                                             vmem_limit_bytes=_vmem_limit(6 * _nbytes((h, cols), F32), 0)),
    )(place, gs, landed32, landed16)


def _all_reduce_small(vec, name):
    r, cols = vec.shape

    def body(in_ref, out_ref, gath, send_sems, recv_sems):
        x, y, c, _ = _place()
        me = 4 * x + 2 * y + c
        gath[me] = in_ref[...]
        sends = []
        for k in range(1, 8):
            to = (x ^ (k >> 2), y ^ ((k >> 1) & 1), c ^ (k & 1))
            cp = pltpu.make_async_remote_copy(src_ref=in_ref, dst_ref=gath.at[me], send_sem=send_sems.at[k - 1],
                                              recv_sem=recv_sems.at[k - 1], device_id=to, device_id_type=MESH)
            cp.start()
            sends.append(cp)
        for k in range(1, 8):
            peer = me ^ k
            pltpu.make_async_remote_copy(src_ref=in_ref, dst_ref=gath.at[peer], send_sem=send_sems.at[k - 1],
                                         recv_sem=recv_sems.at[k - 1], device_id=(x, y, c),
                                         device_id_type=MESH).wait_recv()
        for cp in sends:
            cp.wait_send()
        acc = gath[0]
        for d in range(1, 8):
            acc = acc + gath[d]
        out_ref[...] = acc

    vm = pl.BlockSpec(memory_space=pltpu.VMEM)
    return pl.pallas_call(
        body, name=name, in_specs=[vm], out_specs=vm,
        out_shape=jax.ShapeDtypeStruct((r, cols), F32),
        scratch_shapes=[pltpu.VMEM((8, r, cols), F32), pltpu.SemaphoreType.DMA((7,)), pltpu.SemaphoreType.DMA((7,))],
    )(vec)


def _pad_heads(w, heads, dim, axis):
    shp = w.shape[:axis] + (heads, dim) + w.shape[axis + 1:]
    pad = [(0, 0)] * len(shp)
    pad[axis + 1] = (0, LANE - dim)
    w = jnp.pad(w.reshape(shp), pad)
    return w.reshape(w.shape[:axis] + (heads * LANE,) + w.shape[axis + 2:])


def _unpad_heads(w, heads, dim, axis):
    shp = w.shape[:axis] + (heads, LANE) + w.shape[axis + 1:]
    w = lax.slice_in_dim(w.reshape(shp), 0, dim, axis=axis + 1)
    return w.reshape(w.shape[:axis] + (heads * dim,) + w.shape[axis + 2:])


def _w_in_layout(w_in):
    kr = jnp.pad(w_in[:, 384:416], ((0, 0), (ROPE_LO, LANE - ROPE_LO - MLA_ROPE)))
    sb = lambda lo: w_in[:, lo:lo + SB_WIDTH]
    return jnp.concatenate([w_in[:, 1952:2976], w_in[:, 2976:4000], sb(416), sb(928), sb(1440), w_in[:, 0:256],
                            w_in[:, 256:384], kr], axis=1)


def _w_in_unlayout(d):
    sb = lambda lo: d[:, lo:lo + SB_WIDTH]
    return jnp.concatenate([d[:, C_CQ:C_CQ + 256], d[:, C_CKV:C_CKV + 128], d[:, C_KR + ROPE_LO:C_KR + ROPE_LO + MLA_ROPE],
                            sb(C_SBQ), sb(C_SBK), sb(C_SBV), d[:, C_GA:C_GA + 1024], d[:, C_GB:C_GB + 1024]], axis=1)


def _w_ukv_layout(w):
    w3 = w.reshape(MLA_KV_RANK, MLA_HEADS, MLA_NOPE + MLA_V)
    pad = lambda part: jnp.pad(part, ((0, 0), (0, 0), (0, LANE - part.shape[2]))).reshape(MLA_KV_RANK, MLA_HEADS * LANE)
    return jnp.concatenate([pad(w3[:, :, :MLA_NOPE]), pad(w3[:, :, MLA_NOPE:])], axis=1)


def _w_ukv_unlayout(d):
    hw = MLA_HEADS * LANE
    kpart = d[:, :hw].reshape(MLA_KV_RANK, MLA_HEADS, LANE)[:, :, :MLA_NOPE]
    vpart = d[:, hw:].reshape(MLA_KV_RANK, MLA_HEADS, LANE)[:, :, :MLA_V]
    return jnp.concatenate([kpart, vpart], axis=2).reshape(MLA_KV_RANK, MLA_HEADS * (MLA_NOPE + MLA_V))


def _shard_of(full, d, axis):
    n = full.shape[axis] // N_CHIPS
    return lax.slice_in_dim(full, d * n, (d + 1) * n, axis=axis)


def _local_step(x, mem, pos, target, w, t_mla, t_sb, late=None, reduce=None):
    s = x.shape[0]
    w = dict(w)
    win = _w_in_layout(w["w_in"])
    wuq = _pad_heads(w["w_uq"], MLA_HEADS, MLA_NOPE + MLA_ROPE, 1)
    wkv = _w_ukv_layout(w["w_ukv"])
    inv_freq = ROPE_THETA ** (-jnp.arange(0, MLA_ROPE, 2, dtype=F32) / MLA_ROPE)
    freq_lane = jnp.pad(jnp.concatenate([inv_freq, inv_freq]), (ROPE_LO, LANE - ROPE_LO - MLA_ROPE)).reshape(1, LANE)
    add = lambda accs, ex: (accs[0] + ex[0],)

    def add_norm(accs, ex):
        y = accs[0] + ex[0]
        return y, _rms(y, ex[1])

    tab = _rope_tables(pos.reshape(s, 1), freq_lane)
    h = _rms_fwd_call(x, w["g_mix"], "rms_mix")
    proj = _mm(h, [win], name="proj_in", out_dtypes=(BF16,))
    cqn, ckvn, krope = _mla_prep_fwd(proj, tab, w["g_q_lat"], w["g_kv_lat"])
    hw = MLA_HEADS * LANE
    qa = _mm(cqn, [wuq], name="q_up", row_extras=(tab,), out_dtypes=(BF16,),
             epilogue=lambda accs, ex: (_per_head(lambda t: _rope(t, ex[0]) * (MLA_SCALE * LOG2E), accs[0]),))
    ka = _mm(ckvn, [wkv[:, :hw]], name="k_up", row_extras=(krope,), out_dtypes=(BF16,),
             epilogue=lambda accs, ex: (_per_head(lambda t: t + ex[0], accs[0]),))
    va = _mm(ckvn, [wkv[:, hw:]], name="v_up", out_dtypes=(BF16,))
    o_a, lse, gathered = _mla_fwd(qa, ka, va, t_mla[0], t_mla[1], late[0] if late else ())
    if late:
        w.update(late[1](gathered))
    wa = _pad_heads(w["w_a_proj"], MLA_HEADS, MLA_V, 0)
    wb = w["w_b_proj"]
    o_b = _sb_fwd(proj, t_sb)
    pa = _mm(o_a, [wa], name="proj_a", out_dtypes=(BF16,))
    pb = _mm(o_b, [wb], name="proj_b", out_dtypes=(BF16,))
    merged = _gate_fwd(proj, pa, pb, w["b_gate"])
    x1, hx = _mm(merged, [w["w_o"]], name="proj_o", extras=(x,), consts=(w["g_x"],), epilogue=add_norm,
                 out_dtypes=(F32, BF16))
    mn = _rms_fwd_call(mem, w["g_mem"], "rms_mem")
    xq = _mm(hx, [w["w_xq"]], name="xq", out_dtypes=(BF16,))
    xkv = _mm(mn, [w["w_xkv"]], name="xkv", out_dtypes=(BF16,))
    xo = _xattn_fwd(xq, xkv)
    x2, hf = _mm(xo, [w["w_xo"]], name="proj_xo", extras=(x1,), consts=(w["g_ffn"],), epilogue=add_norm,
                 out_dtypes=(F32, BF16))

    def swiglu(accs, ex):
        a, b = accs
        return a, b, a * _sigmoid(a) * b

    ga, gu, hmid = _mm(hf, [w["w_gate"], w["w_up"]], name="ffn_up", epilogue=swiglu, out_dtypes=(BF16, BF16, BF16),
                       tm=1024, tn=1408, chunk=MM_CHUNK)
    x3 = _mm(hmid, [w["w_down"]], name="ffn_down", extras=(x2,), epilogue=add, tk=2816)

    dx3, dg_final, sq = _loss_head(x3, target, w["g_final"].reshape(1, D_MODEL))
    g = {"g_final": dg_final.reshape(D_MODEL)}

    def swiglu_bwd(accs, ex):
        dh, a, b = accs[0], ex[0].astype(F32), ex[1].astype(F32)
        sg = _sigmoid(a)
        return dh * b * sg * (1.0 + a * (1.0 - sg)), dh * a * sg

    da, db = _mm(dx3, [w["w_down"]], name="ffn_down_dx", tb=True, extras=(ga, gu), epilogue=swiglu_bwd,
                 out_dtypes=(BF16, BF16), tm=1024, tn=1408, chunk=MM_CHUNK)
    g["w_down"] = _mm(hmid, [dx3], name="ffn_down_dw", ta=True, tm=1408)
    g["w_gate"] = _mm(hf, [da], name="ffn_gate_dw", ta=True, tn=1408)
    g["w_up"] = _mm(hf, [db], name="ffn_up_dw", ta=True, tn=1408)
    dhf = _mm(da, [w["w_gate"]], name="ffn_gate_dx", tb=True, tk=2816)
    dhf = _mm(db, [w["w_up"]], name="ffn_up_dx", tb=True, extras=(dhf,), epilogue=add, tk=2816,
              out_dtypes=(BF16,))
    dx2, g["g_ffn"] = _rms_bwd_call(x2, w["g_ffn"], dhf, dx3, "rms_ffn_bwd")

    dxo = _mm(dx2, [w["w_xo"]], name="proj_xo_dx", tb=True, out_dtypes=(BF16,))
    g["w_xo"] = _mm(xo, [dx2], name="proj_xo_dw", ta=True)
    dxq, dxkv = _xattn_bwd(xq, xkv, dxo)
    dhx = _mm(dxq, [w["w_xq"]], name="xq_dx", tb=True, out_dtypes=(BF16,))
    g["w_xq"] = _mm(hx, [dxq], name="xq_dw", ta=True)
    dmn = _mm(dxkv, [w["w_xkv"]], name="xkv_dx", tb=True)
    g["w_xkv"] = _mm(mn, [dxkv], name="xkv_dw", ta=True)
    dx1, g["g_x"] = _rms_bwd_call(x1, w["g_x"], dhx, dx2, "rms_x_bwd")
    _, g["g_mem"] = _rms_bwd_call(mem, w["g_mem"], dmn, None, "rms_mem_bwd")

    dmerged = _mm(dx1, [w["w_o"]], name="proj_o_dx", tb=True, out_dtypes=(BF16,))
    g["w_o"] = _mm(merged, [dx1], name="proj_o_dw", ta=True)
    dpa, dpb, dga, dgb, g["b_gate"] = _gate_bwd(proj, pa, pb, w["b_gate"], dmerged)
    do_a = _mm(dpa, [wa], name="proj_a_dx", tb=True, out_dtypes=(BF16,))
    do_b = _mm(dpb, [wb], name="proj_b_dx", tb=True, out_dtypes=(BF16,))
    g["w_a_proj"] = _unpad_heads(_mm(o_a, [dpa], name="proj_a_dw", ta=True), MLA_HEADS, MLA_V, 0)
    g["w_b_proj"] = _mm(o_b, [dpb], name="proj_b_dw", ta=True)

    stacked = [reduce[1](n, g[n]) for n in reduce[0]] if reduce else []
    dsq, dsk, dsv, recv = _sb_bwd(proj, o_b, do_b, t_sb, stacked)
    parts = [reduce[2](n, gs, rv) for n, gs, rv in zip(reduce[0], stacked, recv)] if reduce else []
    dqp, dka, dva, got = _mla_bwd(qa, ka, va, o_a, do_a, lse, tab, t_mla[0], t_mla[2], parts)
    riding = dict(zip(reduce[0], zip(stacked, recv, got))) if reduce else {}
    dkvp, dkr = _mla_rope_bwd(dka, dva, tab)
    g["w_uq"] = _unpad_heads(_mm(cqn, [dqp], name="q_up_dw", ta=True), MLA_HEADS, MLA_NOPE + MLA_ROPE, 1)
    g["w_ukv"] = _w_ukv_unlayout(_mm(ckvn, [dkvp], name="kv_up_dw", ta=True))
    dcqn = _mm(dqp, [wuq], name="q_up_dx", tb=True)
    dckvn = _mm(dkvp, [wkv], name="kv_up_dx", tb=True)
    dcq, dckv, g["g_q_lat"], g["g_kv_lat"] = _mla_prep_bwd(proj, w["g_q_lat"], w["g_kv_lat"], dcqn, dckvn)

    dproj = jnp.concatenate([dga, dgb, dsq.astype(BF16), dsk.astype(BF16), dsv.astype(BF16), dcq, dckv, dkr], axis=1)
    g["w_in"] = _w_in_unlayout(_mm(h, [dproj], name="proj_in_dw", ta=True))
    if not reduce:
        dh = _mm(dproj, [win], name="proj_in_dx", tb=True, tk=2048, out_dtypes=(BF16,))
        grad_x, g["g_mix"] = _rms_bwd_call(x, w["g_mix"], dh, dx1, "rms_mix_bwd")
        return sq, grad_x, g, riding
    stacked = [reduce[1](n, g[n]) for n in reduce[3]]
    dh, landed = _mm(dproj, [win], name="proj_in_dx", tb=True, tk=2048, out_dtypes=(BF16,),
                     ride=(_direct_copies, stacked + [s.astype(BF16) for s in stacked], _direct_shapes(stacked),
                           7 * len(stacked)))
    grad_x, g["g_mix"] = _rms_bwd_call(x, w["g_mix"], dh, dx1, "rms_mix_bwd")
    nl = len(stacked)
    riding.update(zip(reduce[3], zip(stacked, landed[:nl], landed[nl:])))
    return sq, grad_x, g, riding


def _small_pack(d):
    row5 = jnp.concatenate([d["g_q_lat"].reshape(-1), d["g_kv_lat"].reshape(-1), jnp.zeros((640,), F32)])
    rows = [d[n].reshape(-1) for n in ("g_mix", "g_x", "g_mem", "g_ffn", "g_final")] + [row5]
    return rows


def _small_unpack(p, like):
    out = {n: p[i].reshape(like[n].shape) for i, n in enumerate(("g_mix", "g_x", "g_mem", "g_ffn", "g_final"))}
    out["g_q_lat"] = p[5, 0:256].reshape(like["g_q_lat"].shape)
    out["g_kv_lat"] = p[5, 256:384].reshape(like["g_kv_lat"].shape)
    return out


def kernel(x, mem, positions, g_mix, w_in, b_gate, g_q_lat, w_uq, g_kv_lat, w_ukv, w_a_proj, w_b_proj, w_o, g_x, g_mem, w_xq, w_xkv, w_xo, g_ffn, w_gate, w_up, w_down, g_final, loss_target, m_g_mix, m_w_in, m_b_gate, m_g_q_lat, m_w_uq, m_g_kv_lat, m_w_ukv, m_w_a_proj, m_w_b_proj, m_w_o, m_g_x, m_g_mem, m_w_xq, m_w_xkv, m_w_xo, m_g_ffn, m_w_gate, m_w_up, m_w_down, m_g_final, v_g_mix, v_w_in, v_b_gate, v_g_q_lat, v_w_uq, v_g_kv_lat, v_w_ukv, v_w_a_proj, v_w_b_proj, v_w_o, v_g_x, v_g_mem, v_w_xq, v_w_xkv, v_w_xo, v_g_ffn, v_w_gate, v_w_up, v_w_down, v_g_final):
    given = dict(locals())
    names = [n for n, _, _ in MATS] + ["b_gate"] + list(SMALL)
    wts = {n: given[n] for n in names}
    mom = {n: given["m_" + n] for n in names}
    var = {n: given["v_" + n] for n in names}
    shard2d = {n: shp for n, shp, _ in MATS}
    shard2d["b_gate"] = B_GATE_SHARD
    cx, cy, cc = lax.axis_index("x"), lax.axis_index("y"), lax.axis_index("c")
    me = 2 * cx + cy
    place = jnp.stack([me, cc]).astype(jnp.int32)
    bcol = me * B_GATE_SHARD[1]

    own = [wts[n].reshape(shard2d[n]).astype(BF16) for n, _, _ in MATS]
    bias = (("b_gate", (BIAS_ROWS, B_GATE_SHARD[1]), 1),)
    own_bias = [jnp.pad(wts["b_gate"].reshape(B_GATE_SHARD), ((0, BIAS_ROWS - B_GATE_SHARD[0]), (0, 0)))]

    def assemble(mats, gathered, mine=None):
        out = {}
        for k, ((n, shp, ax), g4) in enumerate(zip(mats, gathered)):
            if mine is not None:
                g4 = lax.dynamic_update_slice(g4, mine[k][None], (me, 0, 0))
            out[n] = g4.reshape(N_CHIPS * shp[0], shp[1]) if ax == 0 else jnp.concatenate(list(g4), axis=1)
        return out

    first = own[:N_EARLY] + own_bias
    full = assemble(MATS[:N_EARLY] + bias, _all_gather_weights(first), first)
    full["b_gate"] = full["b_gate"][0:B_GATE_SHARD[0]]
    late = (own[N_EARLY:], lambda gathered: assemble(MATS[N_EARLY:], gathered))
    for n in SMALL:
        full[n] = wts[n].reshape(1, -1) if n != "g_final" else wts[n]

    axis_of = {n: ax for n, _, ax in MATS}
    stack = lambda n, g: jnp.stack([_shard_of(g, d, axis_of[n]) for d in range(N_CHIPS)])
    pair_sum = lambda n, gs, rv: _pair_sum(gs, rv, place, "pair_sum_" + n)
    behind = [n for n, _, _ in MATS[N_EARLY:]]
    last = [n for n, _, _ in MATS[:N_EARLY]]
    sq, grad_x, grads, riding = _local_step(x[0], mem[0], positions[0], loss_target[0], full, t_mla=(1024, 1024, 1024), t_sb=256,
                                            late=late, reduce=(behind, stack, pair_sum, last))

    halves = [(_chip_sum_direct if n in last else _chip_sum)(*riding[n], place, "chip_sum_" + n) for n, _, _ in MATS]
    g_shard = dict(zip([n for n, _, _ in MATS], _pair_exchange_halves(halves)))

    small_rows = _small_pack({n: grads[n] for n in SMALL}) + [sq.reshape(-1), grads["b_gate"][0], grads["b_gate"][1]]
    small_rows += [jnp.zeros((D_MODEL,), F32)] * (SMALL_ROWS - len(small_rows))
    small = _all_reduce_small(jnp.stack(small_rows), "all_reduce_small")
    loss = (0.5 / D_MODEL) * jnp.sum(small[6])
    g_shard["b_gate"] = lax.dynamic_slice(small[7:9], (0, bcol), B_GATE_SHARD)

    out = {"grad": {}, "delta": {}, "m": {}, "v": {}}
    for n in [n for n, _, _ in MATS] + ["b_gate"]:
        shape = wts[n].shape
        r2 = lambda a: a.reshape(shard2d[n])
        d_n, m_n, v_n = _adamw(r2(wts[n]), g_shard[n], r2(mom[n]), r2(var[n]), "adamw_" + n)
        for key, a in (("grad", g_shard[n]), ("delta", d_n), ("m", m_n), ("v", v_n)):
            out[key][n] = a.reshape(shape)
    sp = lambda d: jnp.stack(_small_pack(d) + [jnp.zeros((D_MODEL,), F32)] * 2)
    delta_s, m_s, v_s = _adamw(sp(wts), small[0:8].at[6:8].set(0.0), sp(mom), sp(var), "adamw_small")
    for key, p in (("grad", small), ("delta", delta_s), ("m", m_s), ("v", v_s)):
        out[key].update(_small_unpack(p, wts))

    order = ["g_mix", "w_in", "b_gate", "g_q_lat", "w_uq", "g_kv_lat", "w_ukv", "w_a_proj", "w_b_proj", "w_o", "g_x",
             "g_mem", "w_xq", "w_xkv", "w_xo", "g_ffn", "w_gate", "w_up", "w_down", "g_final"]
    return (loss, grad_x[None], *[out[key][n] for key in ("grad", "delta", "m", "v") for n in order])
```

```python
import functools
import math

import jax
import jax.numpy as jnp
from jax import lax
from jax.experimental import pallas as pl
from jax.experimental.pallas import tpu as pltpu

F32 = jnp.float32
BF16 = jnp.bfloat16
MESH = pl.DeviceIdType.MESH

D_MODEL = 1024
MLA_HEADS = 8
MLA_Q_RANK = 256
MLA_KV_RANK = 128
MLA_NOPE = 64
MLA_ROPE = 32
MLA_V = 64
ROPE_THETA = 10000.0
SB_HEADS = 8
SB_HEAD_DIM = 64
X_HEADS = 4
X_HEAD_DIM = 128
EPS = 1e-6
ADAM_LR = 0.001
ADAM_B1 = 0.9
ADAM_B2 = 0.999
ADAM_EPS = 1e-08
ADAM_WD = 0.01
ADAM_STEP = 10

LANE = 128
LOG2E = 1.4426950408889634
MLA_SCALE = 1.0 / math.sqrt(MLA_NOPE + MLA_ROPE)
SB_SCALE = 1.0 / math.sqrt(SB_HEAD_DIM)
assert math.log2(SB_SCALE) == round(math.log2(SB_SCALE))
MM_CHUNK = 256
N_CHIPS = 4
VMEM_BYTES = 64 * 1024 * 1024

C_GA, C_GB, C_SBQ, C_SBK, C_SBV, C_CQ, C_CKV, C_KR = 0, 1024, 2048, 2560, 3072, 3584, 3840, 3968
SB_WIDTH = SB_HEADS * SB_HEAD_DIM
ROPE_LO = MLA_NOPE
HALF = MLA_ROPE // 2

SB_ZERO_LOG = -104.0

MATS = (
    ("w_in", (1024, 1000), 1), ("w_uq", (256, 192), 1), ("w_ukv", (128, 256), 1), ("w_a_proj", (512, 256), 1),
    ("w_b_proj", (512, 256), 1), ("w_o", (256, 1024), 0), ("w_xq", (256, 512), 0), ("w_xkv", (256, 1024), 0),
    ("w_xo", (512, 256), 1), ("w_gate", (1024, 704), 1), ("w_up", (1024, 704), 1), ("w_down", (704, 1024), 0),
)
N_EARLY = 3
B_GATE_SHARD = (2, 256)
BIAS_ROWS = 16
SMALL = ("g_mix", "g_x", "g_mem", "g_ffn", "g_final", "g_q_lat", "g_kv_lat")
SMALL_ROWS = 16


def _vmem_limit(block_bytes, temp_bytes):
    est = 2 * block_bytes + temp_bytes + (4 << 20)
    return int(min(max(est, 16 << 20), VMEM_BYTES - (6 << 20)))


def _nbytes(shape, dtype):
    return math.prod(shape) * jnp.dtype(dtype).itemsize


def _row_tile(rows, cap):
    if rows <= cap:
        return rows
    return max(t for t in range(8, cap + 1, 8) if rows % t == 0)


def _tile(n, cap):
    if n <= cap:
        return n
    best = None
    for t in range(LANE, cap + 1, LANE):
        if n % t == 0:
            best = t
    assert best is not None, (n, cap)
    return best


def _mm(a, bs, *, name, ta=False, tb=False, extras=(), row_extras=(), consts=(), epilogue=None, out_dtypes=(F32,),
        tm=1024, tn=1024, tk=1024, chunk=None, ride=None):
    bs = tuple(bs)
    m, k = (a.shape[1], a.shape[0]) if ta else a.shape
    n = bs[0].shape[0] if tb else bs[0].shape[1]
    tm, tn, tk = _tile(m, tm), _tile(n, tn), _tile(k, tk)
    assert m % tm == 0 and n % tn == 0 and k % tk == 0
    nk = k // tk
    nb, ne, no = len(bs), len(extras) + len(row_extras) + len(consts), len(out_dtypes)
    dims = (((0,) if ta else (1,)), ((1,) if tb else (0,))), ((), ())
    if epilogue is None:
        epilogue = lambda accs, ex: (accs[0],)

    rn, rn_out = (len(ride[1]), len(ride[2])) if ride else (0, 0)
    n_acc = nb if nk > 1 else 0
    gi, gj = m // tm, n // tn

    def body(*refs):
        a_ref, b_refs, e_refs = refs[0], refs[1:1 + nb], refs[1 + nb:1 + nb + ne]
        base = 1 + nb + ne + rn
        o_refs, acc_refs = refs[base:base + no], refs[base + no + rn_out:base + no + rn_out + n_acc]
        step = [pl.program_id(d) for d in range(3)]
        finish = _ride_along(ride[0] if ride else None, rn,
                             (refs[base - rn:base], refs[base + no:base + no + rn_out],
                              *refs[base + no + rn_out + n_acc:]),
                             (step[0] == 0) & (step[1] == 0) & (step[2] == 0),
                             (step[0] == gi - 1) & (step[1] == gj - 1) & (step[2] == nk - 1))
        if nk == 1:
            ch = chunk or tm
            bvs = [b_ref[...].astype(BF16) for b_ref in b_refs]
            for r0 in range(0, tm, ch):
                rows = slice(r0, r0 + ch)
                av = (a_ref[:, rows] if ta else a_ref[rows, :]).astype(BF16)
                accs = [lax.dot_general(av, bv, dims, preferred_element_type=F32) for bv in bvs]
                ex = [e[rows, :] for e in e_refs[:ne - len(consts)]] + [e[...] for e in e_refs[ne - len(consts):]]
                for o_ref, v in zip(o_refs, epilogue(accs, ex)):
                    o_ref[rows, :] = v.astype(o_ref.dtype)
            finish()
            return
        kk = step[2]

        @pl.when(kk == 0)
        def _():
            for acc in acc_refs:
                acc[...] = jnp.zeros_like(acc)

        av = a_ref[...].astype(BF16)
        for b_ref, acc in zip(b_refs, acc_refs):
            acc[...] += lax.dot_general(av, b_ref[...].astype(BF16), dims, preferred_element_type=F32)

        @pl.when(kk == nk - 1)
        def _():
            outs = epilogue([acc[...] for acc in acc_refs], [e[...] for e in e_refs])
            for o_ref, v in zip(o_refs, outs):
                o_ref[...] = v.astype(o_ref.dtype)

        finish()

    a_spec = pl.BlockSpec((tk, tm), lambda i, j, kk: (kk, i)) if ta else pl.BlockSpec((tm, tk), lambda i, j, kk: (i, kk))
    b_spec = pl.BlockSpec((tn, tk), lambda i, j, kk: (j, kk)) if tb else pl.BlockSpec((tk, tn), lambda i, j, kk: (kk, j))
    mn_spec = pl.BlockSpec((tm, tn), lambda i, j, kk: (i, j))
    blocks = (_nbytes((tm, tk), a.dtype) + sum(_nbytes((tk, tn), b.dtype) for b in bs)
              + sum(_nbytes((tm, tn), e.dtype) for e in extras) + sum(_nbytes((tm, tn), d) for d in out_dtypes)
              + sum(_nbytes((tm, e.shape[1]), e.dtype) for e in row_extras))
    temps = (nb + 4) * _nbytes((tm, tn), F32)
    outs = pl.pallas_call(
        body, name=name, grid=(m // tm, n // tn, nk),
        in_specs=[a_spec] + [b_spec] * nb + [mn_spec] * len(extras)
        + [pl.BlockSpec((tm, e.shape[1]), lambda i, j, kk: (i, 0)) for e in row_extras]
        + [pl.BlockSpec(e.shape, lambda i, j, kk: (0, 0)) for e in consts] + [ANY] * rn,
        out_specs=[mn_spec] * no + [ANY] * rn_out,
        out_shape=[jax.ShapeDtypeStruct((m, n), d) for d in out_dtypes] + (list(ride[2]) if ride else []),
        scratch_shapes=[pltpu.VMEM((tm, tn), F32) for _ in range(n_acc)] + (_dma_sems(ride[3]) if ride else []),
        compiler_params=pltpu.CompilerParams(
            dimension_semantics=("arbitrary",) * 3 if ride else ("parallel", "parallel", "arbitrary"),
            vmem_limit_bytes=_vmem_limit(blocks, temps)),
    )(a, *bs, *extras, *row_extras, *consts, *(ride[1] if ride else ()))
    if ride:
        return (outs[0] if no == 1 else outs[:no]), list(outs[no:])
    return outs[0] if no == 1 else outs


def _rowwise(body, *, name, rows, tr, row_ins, full_ins=(), row_outs=(), acc_outs=()):
    tr = min(tr, rows)
    assert rows % tr == 0
    n_ri, n_fi, n_ro = len(row_ins), len(full_ins), len(row_outs)

    def kern(*refs):
        body(pl.program_id(0), refs[:n_ri], refs[n_ri:n_ri + n_fi], refs[n_ri + n_fi:n_ri + n_fi + n_ro],
             refs[n_ri + n_fi + n_ro:])

    in_specs = [pl.BlockSpec((tr, w), functools.partial(lambda i, c: (i, c), c=ci)) for _, w, ci in row_ins]
    in_specs += [pl.BlockSpec(f.shape, lambda i: (0, 0)) for f in full_ins]
    out_specs = [pl.BlockSpec((tr, w), lambda i: (i, 0)) for w, _ in row_outs]
    out_specs += [pl.BlockSpec(s, lambda i: (0, 0)) for s, _ in acc_outs]
    out_shape = [jax.ShapeDtypeStruct((rows, w), d) for w, d in row_outs]
    out_shape += [jax.ShapeDtypeStruct(s, d) for s, d in acc_outs]
    blocks = (sum(_nbytes((tr, w), a.dtype) for a, w, _ in row_ins) + sum(_nbytes(f.shape, f.dtype) for f in full_ins)
              + sum(_nbytes((tr, w), d) for w, d in row_outs) + sum(_nbytes(s, d) for s, d in acc_outs))
    widest = max([w for _, w, _ in row_ins] + [w for w, _ in row_outs])
    outs = pl.pallas_call(
        kern, name=name, grid=(rows // tr,), in_specs=in_specs, out_specs=out_specs, out_shape=out_shape,
        compiler_params=pltpu.CompilerParams(
            dimension_semantics=("arbitrary",) if acc_outs else ("parallel",),
            vmem_limit_bytes=_vmem_limit(blocks, 8 * _nbytes((tr, widest), F32))),
    )(*[a for a, _, _ in row_ins], *full_ins)
    return outs


def _rms(x, g):
    r = lax.rsqrt(jnp.mean(x * x, axis=-1, keepdims=True) + EPS)
    return x * r * g


def _rms_bwd(x, g, dy):
    r = lax.rsqrt(jnp.mean(x * x, axis=-1, keepdims=True) + EPS)
    xh = x * r
    dxh = dy * g
    dx = r * (dxh - xh * jnp.mean(dxh * xh, axis=-1, keepdims=True))
    return dx, jnp.sum(dy * xh, axis=0, keepdims=True)


def _sigmoid(x):
    return 1.0 / (1.0 + jnp.exp(-x))


def _acc_init(i, refs):
    @pl.when(i == 0)
    def _():
        for r in refs:
            r[...] = jnp.zeros_like(r)


def _rms_fwd_call(x, g, name):
    rows, c = x.shape

    def body(i, ins, fulls, outs, accs):
        outs[0][...] = _rms(ins[0][...], fulls[0][...]).astype(BF16)

    return _rowwise(body, name=name, rows=rows, tr=512, row_ins=[(x, c, 0)], full_ins=[g], row_outs=[(c, BF16)])[0]


def _rms_bwd_call(x, g, dy, res, name):
    rows, c = x.shape
    row_ins = [(x, c, 0), (dy, c, 0)] + ([(res, c, 0)] if res is not None else [])

    def body(i, ins, fulls, outs, accs):
        _acc_init(i, accs)
        dx, dg = _rms_bwd(ins[0][...], fulls[0][...], ins[1][...].astype(F32))
        if res is not None:
            dx = dx + ins[2][...]
        outs[0][...] = dx
        accs[0][...] += dg

    return _rowwise(body, name=name, rows=rows, tr=1024, row_ins=row_ins, full_ins=[g], row_outs=[(c, F32)],
                    acc_outs=[((1, c), F32)])


def _rope_tables(pos_col, freq_lane):
    rows = pos_col.shape[0]

    def body(i, ins, fulls, outs, accs):
        ang = ins[0][...].astype(F32) * fulls[0][...]
        lane = lax.broadcasted_iota(jnp.int32, ang.shape, 1)
        cos, sin = jnp.cos(ang), jnp.sin(ang)
        first = (lane >= ROPE_LO) & (lane < ROPE_LO + HALF)
        second = (lane >= ROPE_LO + HALF) & (lane < ROPE_LO + MLA_ROPE)
        outs[0][:, 0:LANE] = jnp.where(first | second, cos, 1.0)
        outs[0][:, LANE:2 * LANE] = jnp.where(first, -sin, 0.0)
        outs[0][:, 2 * LANE:3 * LANE] = jnp.where(second, sin, 0.0)

    return _rowwise(body, name="rope_tables", rows=rows, tr=1024, row_ins=[(pos_col, 1, 0)], full_ins=[freq_lane],
                    row_outs=[(3 * LANE, F32)])[0]


def _rope(x, tab):
    return (x * tab[:, 0:LANE] + pltpu.roll(x, LANE - HALF, 1) * tab[:, LANE:2 * LANE]
            + pltpu.roll(x, HALF, 1) * tab[:, 2 * LANE:3 * LANE])


def _rope_t(dy, tab):
    return (dy * tab[:, 0:LANE] + pltpu.roll(dy * tab[:, LANE:2 * LANE], HALF, 1)
            + pltpu.roll(dy * tab[:, 2 * LANE:3 * LANE], LANE - HALF, 1))


def _mla_prep_fwd(proj, tab, g_q, g_kv):
    rows = proj.shape[0]

    def body(i, ins, fulls, outs, accs):
        outs[0][...] = _rms(ins[0][...].astype(F32), fulls[0][...]).astype(BF16)
        outs[1][...] = _rms(ins[1][...].astype(F32), fulls[1][...]).astype(BF16)
        outs[2][...] = _rope(ins[2][...].astype(F32), ins[3][...])

    return _rowwise(body, name="mla_prep_fwd", rows=rows, tr=512,
                    row_ins=[(proj, MLA_Q_RANK, C_CQ // MLA_Q_RANK), (proj, LANE, C_CKV // LANE),
                             (proj, LANE, C_KR // LANE), (tab, 3 * LANE, 0)],
                    full_ins=[g_q, g_kv], row_outs=[(MLA_Q_RANK, BF16), (MLA_KV_RANK, BF16), (LANE, F32)])


def _mla_prep_bwd(proj, g_q, g_kv, dcqn, dckvn):
    rows = proj.shape[0]

    def body(i, ins, fulls, outs, accs):
        _acc_init(i, accs)
        dcq, dgq = _rms_bwd(ins[0][...].astype(F32), fulls[0][...], ins[2][...])
        dckv, dgkv = _rms_bwd(ins[1][...].astype(F32), fulls[1][...], ins[3][...])
        outs[0][...] = dcq.astype(BF16)
        outs[1][...] = dckv.astype(BF16)
        accs[0][...] += dgq
        accs[1][...] += dgkv

    return _rowwise(body, name="mla_prep_bwd", rows=rows, tr=512,
                    row_ins=[(proj, MLA_Q_RANK, C_CQ // MLA_Q_RANK), (proj, LANE, C_CKV // LANE),
                             (dcqn, MLA_Q_RANK, 0), (dckvn, MLA_KV_RANK, 0)],
                    full_ins=[g_q, g_kv], row_outs=[(MLA_Q_RANK, BF16), (MLA_KV_RANK, BF16)],
                    acc_outs=[((1, MLA_Q_RANK), F32), ((1, MLA_KV_RANK), F32)])


def _per_head(fn, x):
    return jnp.concatenate([fn(x[:, h * LANE:(h + 1) * LANE]) for h in range(x.shape[1] // LANE)], axis=1)


def _mla_rope_bwd(dq, dk, dv, tab):
    rows = dq.shape[0]
    hw = MLA_HEADS * LANE

    def body(i, ins, fulls, outs, accs):
        t = ins[3][...]
        dkr = jnp.zeros((ins[0].shape[0], LANE), F32)
        for h in range(MLA_HEADS):
            sl = slice(h * LANE, (h + 1) * LANE)
            outs[0][:, sl] = _rope_t(ins[0][:, sl], t).astype(BF16)
            dkr = dkr + ins[1][:, sl]
        outs[1][:, 0:hw] = ins[1][...].astype(BF16)
        outs[1][:, hw:2 * hw] = ins[2][...].astype(BF16)
        lane = lax.broadcasted_iota(jnp.int32, dkr.shape, 1)
        dkr = jnp.where((lane >= ROPE_LO) & (lane < ROPE_LO + MLA_ROPE), dkr, 0.0)
        outs[2][...] = _rope_t(dkr, t).astype(BF16)

    return _rowwise(body, name="mla_rope_bwd", rows=rows, tr=512,
                    row_ins=[(dq, hw, 0), (dk, hw, 0), (dv, hw, 0), (tab, 3 * LANE, 0)],
                    row_outs=[(hw, BF16), (2 * hw, BF16), (LANE, BF16)])


def _dot_nt(a, b):
    return lax.dot_general(a, b, (((1,), (1,)), ((), ())), preferred_element_type=F32)


def _dot_tn(a, b):
    return lax.dot_general(a, b, (((0,), (0,)), ((), ())), preferred_element_type=F32)


def _dot(a, b):
    return jnp.dot(a, b, preferred_element_type=F32)


def _attn_params(s, t, n_res_f32, n_res_bf16, ride=False):
    blocks = n_res_f32 * _nbytes((s, LANE), F32) + n_res_bf16 * _nbytes((s, LANE), BF16) + 6 * _nbytes((t, LANE), F32)
    return pltpu.CompilerParams(dimension_semantics=("arbitrary" if ride else "parallel", "arbitrary"),
                                vmem_limit_bytes=_vmem_limit(blocks, 12 * _nbytes((t, t), F32)))


def _mla_fwd(q, k, v, t, tk, shards=()):
    s, hw = q.shape
    heads, nq, r = hw // LANE, s // t, t // tk
    ng = len(shards)

    def body(q_ref, k_ref, v_ref, *rest):
        o_ref, l_ref = rest[ng], rest[ng + 1]
        h, i = pl.program_id(0), pl.program_id(1)
        if ng:
            gather = _GatherPhases(rest[:ng], rest[ng + 2:2 * ng + 2], *rest[2 * ng + 2:])
            pl.when((h == 0) & (i == 0))(gather.send)
            pl.when((h == heads // 2) & (i == 0))(gather.forward)
        qv = q_ref[...]

        def step(j, carry, off):
            m, l, acc = carry
            sl = pl.ds(pl.multiple_of(j * tk, tk), tk)
            sc = _dot_nt(qv, k_ref[sl, :])
            if off is not None:
                row = lax.broadcasted_iota(jnp.int32, (t, tk), 0)
                col = lax.broadcasted_iota(jnp.int32, (t, tk), 1)
                sc = jnp.where(col + off <= row, sc, -1e30)
            m_new = jnp.maximum(m, jnp.max(sc, axis=1, keepdims=True))
            p = jnp.exp2(sc - m_new)
            alpha = jnp.exp2(m - m_new)
            l = alpha * l + jnp.sum(p, axis=1, keepdims=True)
            acc = alpha * acc + _dot(p.astype(BF16), v_ref[sl, :])
            return m_new, l, acc

        init = (jnp.full((t, 1), -1e30, F32), jnp.zeros((t, 1), F32), jnp.zeros((t, LANE), F32))
        below = i * r
        carry = lax.fori_loop(0, below // 2, lambda j, c: step(2 * j + 1, step(2 * j, c, None), None), init)
        carry = lax.fori_loop(below // 2 * 2, below, lambda j, c: step(j, c, None), carry)
        for jj in range(r):
            carry = step(i * r + jj, carry, jj * tk)
        m, l, acc = carry
        o_ref[...] = (acc / l).astype(o_ref.dtype)
        l_ref[0] = m + jnp.log2(l)
        if ng:
            pl.when((h == heads - 1) & (i == nq - 1))(gather.finish)

    blk = pl.BlockSpec((t, LANE), lambda h, i: (i, h))
    res = pl.BlockSpec((s, LANE), lambda h, i: (0, h))
    outs = pl.pallas_call(
        body, name="mla_fwd", grid=(heads, nq), in_specs=[blk, res, res] + [ANY] * ng,
        out_specs=[blk, pl.BlockSpec((1, t, 1), lambda h, i: (h, i, 0))] + [ANY] * ng,
        out_shape=[jax.ShapeDtypeStruct((s, hw), BF16), jax.ShapeDtypeStruct((heads, s, 1), F32)]
        + [jax.ShapeDtypeStruct((N_CHIPS,) + sh.shape, sh.dtype) for sh in shards],
        scratch_shapes=_dma_sems(6 * ng) + [pltpu.SemaphoreType.DMA((ng,))] if ng else [],
        compiler_params=_attn_params(s, t, 0, 2, ride=ng > 0),
    )(q, k, v, *shards)
    return outs[0], outs[1], list(outs[2:])


def _mla_bwd(q, k, v, o, do, lse, t, tk, parts=()):
    s, hw = q.shape
    heads, nq, r = hw // LANE, s // t, t // tk
    ns = len(parts)

    def body(q_ref, k_ref, v_ref, o_ref, do_ref, l_ref, *rest):
        dq_ref, dk_ref, dv_ref = rest[ns:ns + 3]
        h, i = pl.program_id(0), pl.program_id(1)
        finish = _ride_along(_scatter_copies, ns, (rest[:ns], rest[ns + 3:2 * ns + 3], *rest[2 * ns + 3:]),
                             (h == 0) & (i == 0), (h == heads - 1) & (i == nq - 1))

        @pl.when(i == 0)
        def _():
            dk_ref[...] = jnp.zeros_like(dk_ref)
            dv_ref[...] = jnp.zeros_like(dv_ref)

        qv, dov, lv = q_ref[...], do_ref[...], l_ref[0]
        dlt = jnp.sum(dov.astype(F32) * o_ref[...].astype(F32), axis=1, keepdims=True)

        def step(j, dq, off):
            sl = pl.ds(pl.multiple_of(j * tk, tk), tk)
            kv, vv = k_ref[sl, :], v_ref[sl, :]
            p = jnp.exp2(_dot_nt(qv, kv) - lv)
            if off is not None:
                row = lax.broadcasted_iota(jnp.int32, (t, tk), 0)
                col = lax.broadcasted_iota(jnp.int32, (t, tk), 1)
                p = jnp.where(col + off <= row, p, 0.0)
            ds = (p * (_dot_nt(dov, vv) - dlt)).astype(BF16)
            dk_ref[sl, :] += _dot_tn(ds, qv) * (1.0 / LOG2E)
            dv_ref[sl, :] += _dot_tn(p.astype(BF16), dov)
            return dq + _dot(ds, kv)

        dq = lax.fori_loop(0, i * r, lambda j, c: step(j, c, None), jnp.zeros((t, LANE), F32))
        for jj in range(r):
            dq = step(i * r + jj, dq, jj * tk)
        dq_ref[...] = dq * MLA_SCALE
        finish()

    blk = pl.BlockSpec((t, LANE), lambda h, i: (i, h))
    res = pl.BlockSpec((s, LANE), lambda h, i: (0, h))
    full = jax.ShapeDtypeStruct((s, hw), F32)
    outs = pl.pallas_call(
        body, name="mla_bwd", grid=(heads, nq),
        in_specs=[blk, res, res, blk, blk, pl.BlockSpec((1, t, 1), lambda h, i: (h, i, 0))] + [ANY] * ns,
        out_specs=[blk, res, res] + [ANY] * ns, out_shape=[full, full, full] + _scatter_shapes(parts),
        scratch_shapes=_dma_sems(3 * ns) if ns else [],
        compiler_params=_attn_params(s, t, 2, 2, ride=ns > 0),
    )(q, k, v, o, do, lse, *parts)
    return outs[0], outs[1], outs[2], list(outs[3:])


def _sb_logits(qv, kv, keep, upper):
    z = _dot_nt(qv, kv)
    e = jnp.exp(-jnp.abs(z))
    l1p = jnp.log(1.0 + e)
    lb = jnp.minimum(z, 0.0) - l1p
    lo = -jnp.maximum(z, 0.0) - l1p
    if keep is not None:
        lo = jnp.where(keep, lo, 0.0)
    hi = lo.astype(BF16)
    rem = (lo - hi.astype(F32)).astype(BF16)
    suf = _dot(hi, upper) + _dot(rem, upper)
    return z, e, lb, lo, suf


def _tri(t, inclusive):
    row = lax.broadcasted_iota(jnp.int32, (t, t), 0)
    col = lax.broadcasted_iota(jnp.int32, (t, t), 1)
    return jnp.where((row >= col) if inclusive else (row > col), 1.0, 0.0).astype(BF16)


SB_QBLOCKS = 2
SB_PAIR = LANE // SB_HEAD_DIM
SB_CHAINS = SB_QBLOCKS * SB_PAIR
SB_FIRST = 2


def _sb_first_tile(b, t):
    start = jnp.maximum(b - (SB_FIRST - 1), 0) * t
    row = lax.broadcasted_iota(jnp.int32, (t, SB_FIRST * t), 0)
    col = lax.broadcasted_iota(jnp.int32, (t, SB_FIRST * t), 1)
    return pl.ds(pl.multiple_of(start, t), SB_FIRST * t), col + start < row + b * t


def _sb_head(x, hh):
    lane = lax.broadcasted_iota(jnp.int32, x.shape, 1)
    return jnp.where(lane // SB_HEAD_DIM == hh, x, jnp.zeros_like(x))


def _sb_walk(i, first, carries_of):
    n = SB_CHAINS
    carries = [first(c) for c in range(n)]
    width = len(carries[0])

    def alive(carry):
        return jnp.max(carry[0]) >= SB_ZERO_LOG

    def split(st):
        return [tuple(st[1 + c * width:1 + (c + 1) * width]) for c in range(n)]

    def live(st):
        any_alive = alive(split(st)[0])
        for cr in split(st)[1:]:
            any_alive = any_alive | alive(cr)
        return (st[0] <= SB_QBLOCKS * i) & any_alive

    def more(st):
        out = (st[0] + 1,)
        for c, cr in enumerate(split(st)):
            out += tuple(carries_of(c, st[0], cr))
        return out

    st = lax.while_loop(live, more, (jnp.int32(SB_FIRST),) + tuple(x for cr in carries for x in cr))
    jj, carries = st[0], split(st)
    for c in range(SB_PAIR, n):
        def live_c(s2, c=c):
            return (s2[0] <= SB_QBLOCKS * i + c // SB_PAIR) & alive(s2[1:])

        def more_c(s2, c=c):
            return (s2[0] + 1,) + tuple(carries_of(c, s2[0], s2[1:]))

        carries[c] = lax.while_loop(live_c, more_c, (jj,) + tuple(carries[c]))[1:]
    return carries


def _sb_fwd(proj, t):
    s = proj.shape[0]
    pairs, nq, nb = SB_HEADS // SB_PAIR, s // t, SB_QBLOCKS

    def body(q_ref, k_ref, v_ref, o_ref):
        i = pl.program_id(1)
        upper, upper_first = _tri(t, False), _tri(SB_FIRST * t, False)
        rows = [slice((c // SB_PAIR) * t, (c // SB_PAIR + 1) * t) for c in range(SB_CHAINS)]
        qs = [_sb_head(q_ref[rows[c], :] * SB_SCALE, c % SB_PAIR).astype(BF16) for c in range(SB_CHAINS)]

        def first(c):
            sl, keep = _sb_first_tile(nb * i + c // SB_PAIR, t)
            _, _, lb, lo, suf = _sb_logits(qs[c], k_ref[sl, :].astype(BF16), keep, upper_first)
            a = jnp.where(keep, jnp.exp(lb + suf), 0.0)
            return (jnp.sum(lo, axis=1, keepdims=True),
                    _dot(a.astype(BF16), _sb_head(v_ref[sl, :], c % SB_PAIR).astype(BF16)))

        def step(c, jj, carry):
            run, acc = carry
            sl = pl.ds(pl.multiple_of((nb * i + c // SB_PAIR - jj) * t, t), t)
            _, _, lb, lo, suf = _sb_logits(qs[c], k_ref[sl, :].astype(BF16), None, upper)
            a = jnp.exp(lb + suf + run)
            acc = acc + _dot(a.astype(BF16), _sb_head(v_ref[sl, :], c % SB_PAIR).astype(BF16))
            return run + jnp.sum(lo, axis=1, keepdims=True), acc

        carries = _sb_walk(i, first, step)
        for qb in range(nb):
            o_ref[qb * t:(qb + 1) * t, :] = sum(carries[qb * SB_PAIR + hh][1] for hh in range(SB_PAIR))

    return pl.pallas_call(
        body, name="sb_fwd", grid=(pairs, nq // nb),
        in_specs=[pl.BlockSpec((nb * t, LANE), lambda h, i: (i, C_SBQ // LANE + h)),
                  pl.BlockSpec((s, LANE), lambda h, i: (0, C_SBK // LANE + h)),
                  pl.BlockSpec((s, LANE), lambda h, i: (0, C_SBV // LANE + h))],
        out_specs=pl.BlockSpec((nb * t, LANE), lambda h, i: (i, h)),
        out_shape=jax.ShapeDtypeStruct((s, pairs * LANE), F32),
        compiler_params=_attn_params(s, nb * t, 0, 2),
    )(proj, proj, proj)


def _sb_bwd(proj, o, do, t, stacked=()):
    s = proj.shape[0]
    pairs, nq, nb = SB_HEADS // SB_PAIR, s // t, SB_QBLOCKS
    nx = len(stacked)

    def body(q_ref, k_ref, v_ref, o_ref, do_ref, *rest):
        dq_ref, dk_ref, dv_ref = rest[nx:nx + 3]
        hd, i = pl.program_id(0), pl.program_id(1)
        finish = _ride_along(_exchange_copies, nx, (rest[:nx], rest[nx + 3:2 * nx + 3], *rest[2 * nx + 3:]),
                             (hd == 0) & (i == 0), (hd == pairs - 1) & (i == nq // nb - 1))

        @pl.when(i == 0)
        def _():
            dk_ref[...] = jnp.zeros_like(dk_ref)
            dv_ref[...] = jnp.zeros_like(dv_ref)

        rows = [slice((c // SB_PAIR) * t, (c // SB_PAIR + 1) * t) for c in range(SB_CHAINS)]
        qs = [_sb_head(q_ref[rows[c], :] * SB_SCALE, c % SB_PAIR).astype(BF16) for c in range(SB_CHAINS)]
        dos = [_sb_head(do_ref[rows[c], :], c % SB_PAIR) for c in range(SB_CHAINS)]
        totals = [jnp.sum(dos[c].astype(F32) * o_ref[rows[c], :], axis=1, keepdims=True) for c in range(SB_CHAINS)]
        tris = {1: (_tri(t, False), _tri(t, True)), SB_FIRST: (_tri(SB_FIRST * t, False), _tri(SB_FIRST * t, True))}

        def tile(c, sl, keep, blocks, carry):
            run, g, dq = carry
            qv, dov = qs[c], dos[c]
            upper, upper_incl = tris[blocks]
            kv, vv = k_ref[sl, :].astype(BF16), v_ref[sl, :].astype(BF16)
            z, e, lb, lo, suf = _sb_logits(qv, kv, keep, upper)
            tail = suf + run
            a = jnp.exp(lb + tail)
            if keep is not None:
                a = jnp.where(keep, a, 0.0)
            ab = a.astype(BF16)
            gr = ab.astype(F32) * _dot_nt(dov, vv)
            ghi = gr.astype(BF16)
            grem = (gr - ghi.astype(F32)).astype(BF16)
            before = totals[c] - g - (_dot(ghi, upper_incl) + _dot(grem, upper_incl))
            before = jnp.where(tail < SB_ZERO_LOG, 0.0, before)
            r = 1.0 / (1.0 + e)
            pos = z >= 0.0
            dz = r * (gr * jnp.where(pos, e, 1.0) - before * jnp.where(pos, 1.0, e))
            if keep is not None:
                dz = jnp.where(keep, dz, 0.0)
            dzb = dz.astype(BF16)
            dk_ref[sl, :] += _dot_tn(dzb, qv)
            dv_ref[sl, :] += _dot_tn(ab, dov)
            return (run + jnp.sum(lo, axis=1, keepdims=True), g + jnp.sum(gr, axis=1, keepdims=True),
                    dq + _dot(dzb, _sb_head(kv, c % SB_PAIR)))

        zero = jnp.zeros((t, 1), F32)
        init = (zero, zero, jnp.zeros((t, LANE), F32))

        def first(c):
            sl, keep = _sb_first_tile(nb * i + c // SB_PAIR, t)
            return tile(c, sl, keep, SB_FIRST, init)

        def step(c, jj, carry):
            return tile(c, pl.ds(pl.multiple_of((nb * i + c // SB_PAIR - jj) * t, t), t), None, 1, carry)

        carries = _sb_walk(i, first, step)
        for qb in range(nb):
            dq_ref[qb * t:(qb + 1) * t, :] = sum(carries[qb * SB_PAIR + hh][2] for hh in range(SB_PAIR)) * SB_SCALE
        finish()

    blk = pl.BlockSpec((nb * t, LANE), lambda h, i: (i, h))
    res = pl.BlockSpec((s, LANE), lambda h, i: (0, h))
    full = jax.ShapeDtypeStruct((s, pairs * LANE), F32)
    outs = pl.pallas_call(
        body, name="sb_bwd", grid=(pairs, nq // nb),
        in_specs=[pl.BlockSpec((nb * t, LANE), lambda h, i: (i, C_SBQ // LANE + h)),
                  pl.BlockSpec((s, LANE), lambda h, i: (0, C_SBK // LANE + h)),
                  pl.BlockSpec((s, LANE), lambda h, i: (0, C_SBV // LANE + h)), blk, blk] + [ANY] * nx,
        out_specs=[blk, res, res] + [ANY] * nx, out_shape=[full, full, full] + _exchange_shapes(stacked),
        scratch_shapes=_dma_sems(nx) if nx else [],
        compiler_params=_attn_params(s, nb * t, 2, 2, ride=nx > 0),
    )(proj, proj, proj, o, do, *stacked)
    return outs[0], outs[1], outs[2], list(outs[3:])


def _xattn_probs(qh, kh):
    sc = _dot_nt(qh, kh) * (1.0 / math.sqrt(X_HEAD_DIM))
    p = jnp.exp(sc - jnp.max(sc, axis=1, keepdims=True))
    return p / jnp.sum(p, axis=1, keepdims=True)


def _xattn_fwd(xq, xkv):
    rows = xq.shape[0]
    w = X_HEADS * X_HEAD_DIM

    def body(i, ins, fulls, outs, accs):
        for h in range(X_HEADS):
            sl = slice(h * LANE, (h + 1) * LANE)
            p = _xattn_probs(ins[0][:, sl], fulls[0][:, sl])
            outs[0][:, sl] = _dot(p.astype(BF16), fulls[0][:, w + h * LANE:w + (h + 1) * LANE]).astype(BF16)

    return _rowwise(body, name="xattn_fwd", rows=rows, tr=512, row_ins=[(xq, w, 0)], full_ins=[xkv],
                    row_outs=[(w, BF16)])[0]


def _xattn_bwd(xq, xkv, dxo):
    rows = xq.shape[0]
    w = X_HEADS * X_HEAD_DIM

    def body(i, ins, fulls, outs, accs):
        _acc_init(i, accs)
        for h in range(X_HEADS):
            sl = slice(h * LANE, (h + 1) * LANE)
            slv = slice(w + h * LANE, w + (h + 1) * LANE)
            qh, kh, vh, doh = ins[0][:, sl], fulls[0][:, sl], fulls[0][:, slv], ins[1][:, sl]
            p = _xattn_probs(qh, kh)
            dp = _dot_nt(doh, vh)
            ds = (p * (dp - jnp.sum(p * dp, axis=1, keepdims=True)) * (1.0 / math.sqrt(X_HEAD_DIM))).astype(BF16)
            outs[0][:, sl] = _dot(ds, kh).astype(BF16)
            accs[0][:, sl] += _dot_tn(ds, qh)
            accs[0][:, slv] += _dot_tn(p.astype(BF16), doh)

    return _rowwise(body, name="xattn_bwd", rows=rows, tr=512, row_ins=[(xq, w, 0), (dxo, w, 0)], full_ins=[xkv],
                    row_outs=[(w, BF16)], acc_outs=[(xkv.shape, F32)])


def _gate_fwd(proj, pa, pb, b_gate):
    rows = proj.shape[0]

    def body(i, ins, fulls, outs, accs):
        sa = _sigmoid(ins[0][...].astype(F32) + fulls[0][0:1, :])
        sb = _sigmoid(ins[1][...].astype(F32) + fulls[0][1:2, :])
        outs[0][...] = (sa * ins[2][...].astype(F32) + sb * ins[3][...].astype(F32)).astype(BF16)

    return _rowwise(body, name="gate_fwd", rows=rows, tr=512,
                    row_ins=[(proj, D_MODEL, C_GA // D_MODEL), (proj, D_MODEL, C_GB // D_MODEL), (pa, D_MODEL, 0),
                             (pb, D_MODEL, 0)],
                    full_ins=[b_gate], row_outs=[(D_MODEL, BF16)])[0]


def _gate_bwd(proj, pa, pb, b_gate, dm):
    rows = proj.shape[0]

    def body(i, ins, fulls, outs, accs):
        _acc_init(i, accs)
        d = ins[4][...].astype(F32)
        sa = _sigmoid(ins[0][...].astype(F32) + fulls[0][0:1, :])
        sb = _sigmoid(ins[1][...].astype(F32) + fulls[0][1:2, :])
        dga = d * ins[2][...].astype(F32) * sa * (1.0 - sa)
        dgb = d * ins[3][...].astype(F32) * sb * (1.0 - sb)
        outs[0][...] = (d * sa).astype(BF16)
        outs[1][...] = (d * sb).astype(BF16)
        outs[2][...] = dga.astype(BF16)
        outs[3][...] = dgb.astype(BF16)
        accs[0][0:1, :] += jnp.sum(dga, axis=0, keepdims=True)
        accs[0][1:2, :] += jnp.sum(dgb, axis=0, keepdims=True)

    return _rowwise(body, name="gate_bwd", rows=rows, tr=512,
                    row_ins=[(proj, D_MODEL, C_GA // D_MODEL), (proj, D_MODEL, C_GB // D_MODEL), (pa, D_MODEL, 0),
                             (pb, D_MODEL, 0), (dm, D_MODEL, 0)],
                    full_ins=[b_gate], row_outs=[(D_MODEL, BF16)] * 4, acc_outs=[((2, D_MODEL), F32)])


def _loss_head(x3, target, g_final):
    rows = x3.shape[0]

    def body(i, ins, fulls, outs, accs):
        _acc_init(i, accs)
        xv, g = ins[0][...], fulls[0][...]
        d = _rms(xv, g) - ins[1][...]
        dx, dg = _rms_bwd(xv, g, d * (1.0 / D_MODEL))
        outs[0][...] = dx
        accs[0][...] += dg
        accs[1][...] += jnp.sum(d * d, axis=0, keepdims=True)

    return _rowwise(body, name="loss_head", rows=rows, tr=1024, row_ins=[(x3, D_MODEL, 0), (target, D_MODEL, 0)],
                    full_ins=[g_final], row_outs=[(D_MODEL, F32)], acc_outs=[((1, D_MODEL), F32), ((1, D_MODEL), F32)])


def _adamw(w, g, m, v, name):
    rows, c = w.shape

    def body(i, ins, fulls, outs, accs):
        wv, gv = ins[0][...], ins[1][...]
        mn = ADAM_B1 * ins[2][...] + (1.0 - ADAM_B1) * gv
        vn = ADAM_B2 * ins[3][...] + (1.0 - ADAM_B2) * jnp.square(gv)
        m_hat = mn / (1.0 - ADAM_B1 ** ADAM_STEP)
        v_hat = vn / (1.0 - ADAM_B2 ** ADAM_STEP)
        outs[0][...] = -ADAM_LR * (m_hat / (jnp.sqrt(v_hat) + ADAM_EPS) + ADAM_WD * wv)
        outs[1][...] = mn
        outs[2][...] = vn

    return _rowwise(body, name=name, rows=rows, tr=_row_tile(rows, 256), row_ins=[(a, c, 0) for a in (w, g, m, v)],
                    row_outs=[(c, F32)] * 3)


def _place():
    x, y, c = lax.axis_index("x"), lax.axis_index("y"), lax.axis_index("c")
    chips = [(1 - x, y), (x, 1 - y), (1 - x, 1 - y)]
    return x, y, c, chips


ANY = pl.BlockSpec(memory_space=pl.ANY)


def _remote(src, dst, send_sem, recv_sem, to):
    return pltpu.make_async_remote_copy(src_ref=src, dst_ref=dst, send_sem=send_sem, recv_sem=recv_sem,
                                        device_id=to, device_id_type=MESH)


def _dma_sems(n):
    return [pltpu.SemaphoreType.DMA((n,)), pltpu.SemaphoreType.DMA((n,))]


class _GatherPhases:
    def __init__(self, ins, outs, send_sems, recv_sems, local_sems=None):
        self.ins, self.outs, self.send_sems, self.recv_sems = ins, outs, send_sems, recv_sems
        self.local_sems = local_sems
        self.x, self.y, self.c, self.chips = _place()
        self.me = 2 * self.x + self.y

    def _locals(self):
        if self.local_sems is None:
            return []
        return [pltpu.make_async_copy(self.ins[t], self.outs[t].at[self.me], self.local_sems.at[t])
                for t in range(len(self.ins))]

    def _copy(self, t, j, chip_idx, hlf, to, src=None):
        h = self.ins[t].shape[0] // 2
        dst = self.outs[t].at[chip_idx, pl.ds(hlf * h, h), :]
        return _remote(dst if src is None else src, dst, self.send_sems.at[6 * t + j], self.recv_sems.at[6 * t + j], to)

    def _sends(self):
        out = []
        for t in range(len(self.ins)):
            h = self.ins[t].shape[0] // 2
            for j, chip in enumerate(self.chips):
                out.append(self._copy(t, j, self.me, self.c, (*chip, self.c), src=self.ins[t].at[pl.ds(self.c * h, h), :]))
        return out

    def _forwards(self):
        return [self._copy(t, 3 + j, 2 * chip[0] + chip[1], self.c, (self.x, self.y, 1 - self.c))
                for t in range(len(self.ins)) for j, chip in enumerate(self.chips)]

    def send(self):
        for cp in self._sends() + self._locals():
            cp.start()

    def forward(self):
        here = (self.x, self.y, self.c)
        landed = [self._copy(t, j, 2 * chip[0] + chip[1], self.c, here)
                  for t in range(len(self.ins)) for j, chip in enumerate(self.chips)]
        for arrival, fwd in zip(landed, self._forwards()):
            arrival.wait_recv()
            fwd.start()

    def finish(self):
        here = (self.x, self.y, self.c)
        for t in range(len(self.ins)):
            for j, chip in enumerate(self.chips):
                self._copy(t, 3 + j, 2 * chip[0] + chip[1], 1 - self.c, here).wait_recv()
        for cp in self._sends() + self._forwards():
            cp.wait_send()
        for cp in self._locals():
            cp.wait()


def _all_gather_weights(shards):
    n = len(shards)

    def body(*refs):
        gather = _GatherPhases(refs[:n], refs[n:2 * n], *refs[2 * n:])
        gather.send()
        gather.forward()
        gather.finish()

    return pl.pallas_call(
        body, name="all_gather_weights", in_specs=[ANY] * n, out_specs=[ANY] * n,
        out_shape=[jax.ShapeDtypeStruct((N_CHIPS,) + s.shape, s.dtype) for s in shards],
        scratch_shapes=_dma_sems(6 * n),
    )(*shards)


def _exchange_copies(ins, outs, send_sems, recv_sems):
    x, y, c, _ = _place()
    cps = []
    for t in range(len(ins)):
        h = ins[t].shape[1] // 2
        cps.append(_remote(ins[t].at[:, pl.ds((1 - c) * h, h), :], outs[t], send_sems.at[t], recv_sems.at[t],
                           (x, y, 1 - c)))
    return cps


def _exchange_shapes(stacked):
    return [jax.ShapeDtypeStruct((N_CHIPS, s.shape[1] // 2, s.shape[2]), s.dtype) for s in stacked]


def _scatter_copies(ins, outs, send_sems, recv_sems):
    x, y, c, chips = _place()
    return [_remote(ins[t].at[2 * chip[0] + chip[1]], outs[t].at[j], send_sems.at[3 * t + j], recv_sems.at[3 * t + j],
                    (*chip, c)) for t in range(len(ins)) for j, chip in enumerate(chips)]


def _scatter_shapes(parts):
    return [jax.ShapeDtypeStruct((N_CHIPS - 1,) + s.shape[1:], s.dtype) for s in parts]


def _direct_copies(ins, outs, send_sems, recv_sems):
    x, y, k, chips = _place()
    me = 2 * x + y
    n = len(outs) // 2
    starts, arrivals = [], []
    for t in range(n):
        full, half_prec, landed32, landed16 = ins[t], ins[n + t], outs[t], outs[n + t]
        h = full.shape[1] // 2
        base = 7 * t
        pair = _remote(full.at[me, pl.ds((1 - k) * h, h), :], landed32, send_sems.at[base], recv_sems.at[base],
                       (x, y, 1 - k))
        starts.append(pair)
        arrivals.append(pair)
        for j, chip in enumerate(chips):
            for core in range(2):
                starts.append(_remote(half_prec.at[2 * chip[0] + chip[1], pl.ds(core * h, h), :], landed16.at[2 * j + k],
                                      send_sems.at[base + 1 + 2 * j + core], recv_sems.at[base + 1 + 2 * j + k],
                                      (*chip, core)))
                arrivals.append(_remote(half_prec.at[0, pl.ds(0, h), :], landed16.at[2 * j + core],
                                        send_sems.at[base + 1 + 2 * j + core], recv_sems.at[base + 1 + 2 * j + core],
                                        (x, y, k)))
    return starts, arrivals


def _direct_shapes(stacked):
    half = lambda s: (s.shape[1] // 2, s.shape[2])
    return ([jax.ShapeDtypeStruct(half(s), F32) for s in stacked]
            + [jax.ShapeDtypeStruct((6,) + half(s), BF16) for s in stacked])


def _ride_along(copies_of, n, refs, first, last):
    if not n:
        return lambda: None

    def start():
        built = copies_of(*refs)
        for cp in (built[0] if isinstance(built, tuple) else built):
            cp.start()

    def wait():
        built = copies_of(*refs)
        if not isinstance(built, tuple):
            for cp in built:
                cp.wait()
            return
        for cp in built[0]:
            cp.wait_send()
        for cp in built[1]:
            cp.wait_recv()

    pl.when(first)(start)
    return lambda: pl.when(last)(wait)


def _pair_exchange_halves(shards):
    n = len(shards)

    def body(*refs):
        bufs = refs[n:2 * n]
        send_sems, recv_sems = refs[2 * n:]
        x, y, c, _ = _place()
        cps = []
        for t in range(n):
            h = bufs[t].shape[0] // 2
            rows = bufs[t].at[pl.ds(c * h, h), :]
            cps.append(_remote(rows, rows, send_sems.at[t], recv_sems.at[t], (x, y, 1 - c)))
            cps[-1].start()
        for cp in cps:
            cp.wait()

    return pl.pallas_call(
        body, name="pair_exchange_halves", in_specs=[ANY] * n, out_specs=[ANY] * n,
        out_shape=[jax.ShapeDtypeStruct(s.shape, s.dtype) for s in shards],
        input_output_aliases={t: t for t in range(n)},
        scratch_shapes=_dma_sems(n),
    )(*shards)


def _pair_sum(gs, recv, place, name):
    _, r, cols = gs.shape
    h = r // 2

    def kern(p_ref, a_ref, b_ref, o_ref):
        o_ref[...] = (a_ref[...] + b_ref[...]).astype(BF16)

    blk = lambda f: pl.BlockSpec((1, h, cols), f)
    return pl.pallas_call(
        kern, name=name,
        grid_spec=pltpu.PrefetchScalarGridSpec(
            num_scalar_prefetch=1, grid=(N_CHIPS,),
            in_specs=[blk(lambda d, p: (d, p[1], 0)), blk(lambda d, p: (d, 0, 0))],
            out_specs=blk(lambda d, p: (d, 0, 0))),
        out_shape=jax.ShapeDtypeStruct((N_CHIPS, h, cols), BF16),
        compiler_params=pltpu.CompilerParams(dimension_semantics=("arbitrary",),
                                             vmem_limit_bytes=_vmem_limit(3 * _nbytes((h, cols), F32), 0)),
    )(place, gs, recv)


def _chip_sum(gs, recv, got, place, name):
    _, r, cols = gs.shape
    h = r // 2

    def kern(p_ref, a_ref, b_ref, g0, g1, g2, o_ref):
        own = a_ref[0] + b_ref[0]
        o_ref[...] = ((own + g0[0].astype(F32)) + g1[0].astype(F32)) + g2[0].astype(F32)

    blk = lambda f: pl.BlockSpec((1, h, cols), f)
    return pl.pallas_call(
        kern, name=name,
        grid_spec=pltpu.PrefetchScalarGridSpec(
            num_scalar_prefetch=1, grid=(1,),
            in_specs=[blk(lambda i, p: (p[0], p[1], 0)), blk(lambda i, p: (p[0], 0, 0)), blk(lambda i, p: (0, 0, 0)),
                      blk(lambda i, p: (1, 0, 0)), blk(lambda i, p: (2, 0, 0))],
            out_specs=pl.BlockSpec((h, cols), lambda i, p: (p[1], 0))),
        out_shape=jax.ShapeDtypeStruct((r, cols), F32),
        compiler_params=pltpu.CompilerParams(dimension_semantics=("arbitrary",),
                                             vmem_limit_bytes=_vmem_limit(5 * _nbytes((h, cols), F32), 0)),
    )(place, gs, recv, got, got, got)


def _chip_sum_direct(gs, landed32, landed16, place, name):
    _, r, cols = gs.shape
    h = r // 2

    def kern(p_ref, a_ref, b_ref, g_ref, o_ref):
        acc = a_ref[0] + b_ref[...]
        for slot in range(6):
            acc = acc + g_ref[slot].astype(F32)
        o_ref[...] = acc

    return pl.pallas_call(
        kern, name=name,
        grid_spec=pltpu.PrefetchScalarGridSpec(
            num_scalar_prefetch=1, grid=(1,),
            in_specs=[pl.BlockSpec((1, h, cols), lambda i, p: (p[0], p[1], 0)),
                      pl.BlockSpec((h, cols), lambda i, p: (0, 0)), pl.BlockSpec((6, h, cols), lambda i, p: (0, 0, 0))],
            out_specs=pl.BlockSpec((h, cols), lambda i, p: (p[1], 0))),
        out_shape=jax.ShapeDtypeStruct((r, cols), F32),
        compiler_params=pltpu.CompilerParams(dimension_semantics=("arbitrary",),
                                             vmem_limit_bytes=_vmem_limit(6 * _nbytes((h, cols), F32), 0)),
    )(place, gs, landed32, landed16)


def _all_reduce_small(vec, name):
    r, cols = vec.shape

    def body(in_ref, out_ref, gath, send_sems, recv_sems):
        x, y, c, _ = _place()
        me = 4 * x + 2 * y + c
        gath[me] = in_ref[...]
        sends = []
        for k in range(1, 8):
            to = (x ^ (k >> 2), y ^ ((k >> 1) & 1), c ^ (k & 1))
            cp = pltpu.make_async_remote_copy(src_ref=in_ref, dst_ref=gath.at[me], send_sem=send_sems.at[k - 1],
                                              recv_sem=recv_sems.at[k - 1], device_id=to, device_id_type=MESH)
            cp.start()
            sends.append(cp)
        for k in range(1, 8):
            peer = me ^ k
            pltpu.make_async_remote_copy(src_ref=in_ref, dst_ref=gath.at[peer], send_sem=send_sems.at[k - 1],
                                         recv_sem=recv_sems.at[k - 1], device_id=(x, y, c),
                                         device_id_type=MESH).wait_recv()
        for cp in sends:
            cp.wait_send()
        acc = gath[0]
        for d in range(1, 8):
            acc = acc + gath[d]
        out_ref[...] = acc

    vm = pl.BlockSpec(memory_space=pltpu.VMEM)
    return pl.pallas_call(
        body, name=name, in_specs=[vm], out_specs=vm,
        out_shape=jax.ShapeDtypeStruct((r, cols), F32),
        scratch_shapes=[pltpu.VMEM((8, r, cols), F32), pltpu.SemaphoreType.DMA((7,)), pltpu.SemaphoreType.DMA((7,))],
    )(vec)


def _pad_heads(w, heads, dim, axis):
    shp = w.shape[:axis] + (heads, dim) + w.shape[axis + 1:]
    pad = [(0, 0)] * len(shp)
    pad[axis + 1] = (0, LANE - dim)
    w = jnp.pad(w.reshape(shp), pad)
    return w.reshape(w.shape[:axis] + (heads * LANE,) + w.shape[axis + 2:])


def _unpad_heads(w, heads, dim, axis):
    shp = w.shape[:axis] + (heads, LANE) + w.shape[axis + 1:]
    w = lax.slice_in_dim(w.reshape(shp), 0, dim, axis=axis + 1)
    return w.reshape(w.shape[:axis] + (heads * dim,) + w.shape[axis + 2:])


def _w_in_layout(w_in):
    kr = jnp.pad(w_in[:, 384:416], ((0, 0), (ROPE_LO, LANE - ROPE_LO - MLA_ROPE)))
    sb = lambda lo: w_in[:, lo:lo + SB_WIDTH]
    return jnp.concatenate([w_in[:, 1952:2976], w_in[:, 2976:4000], sb(416), sb(928), sb(1440), w_in[:, 0:256],
                            w_in[:, 256:384], kr], axis=1)


def _w_in_unlayout(d):
    sb = lambda lo: d[:, lo:lo + SB_WIDTH]
    return jnp.concatenate([d[:, C_CQ:C_CQ + 256], d[:, C_CKV:C_CKV + 128], d[:, C_KR + ROPE_LO:C_KR + ROPE_LO + MLA_ROPE],
                            sb(C_SBQ), sb(C_SBK), sb(C_SBV), d[:, C_GA:C_GA + 1024], d[:, C_GB:C_GB + 1024]], axis=1)


def _w_ukv_layout(w):
    w3 = w.reshape(MLA_KV_RANK, MLA_HEADS, MLA_NOPE + MLA_V)
    pad = lambda part: jnp.pad(part, ((0, 0), (0, 0), (0, LANE - part.shape[2]))).reshape(MLA_KV_RANK, MLA_HEADS * LANE)
    return jnp.concatenate([pad(w3[:, :, :MLA_NOPE]), pad(w3[:, :, MLA_NOPE:])], axis=1)


def _w_ukv_unlayout(d):
    hw = MLA_HEADS * LANE
    kpart = d[:, :hw].reshape(MLA_KV_RANK, MLA_HEADS, LANE)[:, :, :MLA_NOPE]
    vpart = d[:, hw:].reshape(MLA_KV_RANK, MLA_HEADS, LANE)[:, :, :MLA_V]
    return jnp.concatenate([kpart, vpart], axis=2).reshape(MLA_KV_RANK, MLA_HEADS * (MLA_NOPE + MLA_V))


def _shard_of(full, d, axis):
    n = full.shape[axis] // N_CHIPS
    return lax.slice_in_dim(full, d * n, (d + 1) * n, axis=axis)


def _local_step(x, mem, pos, target, w, t_mla, t_sb, late=None, reduce=None):
    s = x.shape[0]
    w = dict(w)
    win = _w_in_layout(w["w_in"])
    wuq = _pad_heads(w["w_uq"], MLA_HEADS, MLA_NOPE + MLA_ROPE, 1)
    wkv = _w_ukv_layout(w["w_ukv"])
    inv_freq = ROPE_THETA ** (-jnp.arange(0, MLA_ROPE, 2, dtype=F32) / MLA_ROPE)
    freq_lane = jnp.pad(jnp.concatenate([inv_freq, inv_freq]), (ROPE_LO, LANE - ROPE_LO - MLA_ROPE)).reshape(1, LANE)
    add = lambda accs, ex: (accs[0] + ex[0],)

    def add_norm(accs, ex):
        y = accs[0] + ex[0]
        return y, _rms(y, ex[1])

    tab = _rope_tables(pos.reshape(s, 1), freq_lane)
    h = _rms_fwd_call(x, w["g_mix"], "rms_mix")
    proj = _mm(h, [win], name="proj_in", out_dtypes=(BF16,))
    cqn, ckvn, krope = _mla_prep_fwd(proj, tab, w["g_q_lat"], w["g_kv_lat"])
    hw = MLA_HEADS * LANE
    qa = _mm(cqn, [wuq], name="q_up", row_extras=(tab,), out_dtypes=(BF16,),
             epilogue=lambda accs, ex: (_per_head(lambda t: _rope(t, ex[0]) * (MLA_SCALE * LOG2E), accs[0]),))
    ka = _mm(ckvn, [wkv[:, :hw]], name="k_up", row_extras=(krope,), out_dtypes=(BF16,),
             epilogue=lambda accs, ex: (_per_head(lambda t: t + ex[0], accs[0]),))
    va = _mm(ckvn, [wkv[:, hw:]], name="v_up", out_dtypes=(BF16,))
    o_a, lse, gathered = _mla_fwd(qa, ka, va, t_mla[0], t_mla[1], late[0] if late else ())
    if late:
        w.update(late[1](gathered))
    wa = _pad_heads(w["w_a_proj"], MLA_HEADS, MLA_V, 0)
    wb = w["w_b_proj"]
    o_b = _sb_fwd(proj, t_sb)
    pa = _mm(o_a, [wa], name="proj_a", out_dtypes=(BF16,))
    pb = _mm(o_b, [wb], name="proj_b", out_dtypes=(BF16,))
    merged = _gate_fwd(proj, pa, pb, w["b_gate"])
    x1, hx = _mm(merged, [w["w_o"]], name="proj_o", extras=(x,), consts=(w["g_x"],), epilogue=add_norm,
                 out_dtypes=(F32, BF16))
    mn = _rms_fwd_call(mem, w["g_mem"], "rms_mem")
    xq = _mm(hx, [w["w_xq"]], name="xq", out_dtypes=(BF16,))
    xkv = _mm(mn, [w["w_xkv"]], name="xkv", out_dtypes=(BF16,))
    xo = _xattn_fwd(xq, xkv)
    x2, hf = _mm(xo, [w["w_xo"]], name="proj_xo", extras=(x1,), consts=(w["g_ffn"],), epilogue=add_norm,
                 out_dtypes=(F32, BF16))

    def swiglu(accs, ex):
        a, b = accs
        return a, b, a * _sigmoid(a) * b

    ga, gu, hmid = _mm(hf, [w["w_gate"], w["w_up"]], name="ffn_up", epilogue=swiglu, out_dtypes=(BF16, BF16, BF16),
                       tm=1024, tn=1408, chunk=MM_CHUNK)
    x3 = _mm(hmid, [w["w_down"]], name="ffn_down", extras=(x2,), epilogue=add, tk=2816)

    dx3, dg_final, sq = _loss_head(x3, target, w["g_final"].reshape(1, D_MODEL))
    g = {"g_final": dg_final.reshape(D_MODEL)}

    def swiglu_bwd(accs, ex):
        dh, a, b = accs[0], ex[0].astype(F32), ex[1].astype(F32)
        sg = _sigmoid(a)
        return dh * b * sg * (1.0 + a * (1.0 - sg)), dh * a * sg

    da, db = _mm(dx3, [w["w_down"]], name="ffn_down_dx", tb=True, extras=(ga, gu), epilogue=swiglu_bwd,
                 out_dtypes=(BF16, BF16), tm=1024, tn=1408, chunk=MM_CHUNK)
    g["w_down"] = _mm(hmid, [dx3], name="ffn_down_dw", ta=True, tm=1408)
    g["w_gate"] = _mm(hf, [da], name="ffn_gate_dw", ta=True, tn=1408)
    g["w_up"] = _mm(hf, [db], name="ffn_up_dw", ta=True, tn=1408)
    dhf = _mm(da, [w["w_gate"]], name="ffn_gate_dx", tb=True, tk=2816)
    dhf = _mm(db, [w["w_up"]], name="ffn_up_dx", tb=True, extras=(dhf,), epilogue=add, tk=2816,
              out_dtypes=(BF16,))
    dx2, g["g_ffn"] = _rms_bwd_call(x2, w["g_ffn"], dhf, dx3, "rms_ffn_bwd")

    dxo = _mm(dx2, [w["w_xo"]], name="proj_xo_dx", tb=True, out_dtypes=(BF16,))
    g["w_xo"] = _mm(xo, [dx2], name="proj_xo_dw", ta=True)
    dxq, dxkv = _xattn_bwd(xq, xkv, dxo)
    dhx = _mm(dxq, [w["w_xq"]], name="xq_dx", tb=True, out_dtypes=(BF16,))
    g["w_xq"] = _mm(hx, [dxq], name="xq_dw", ta=True)
    dmn = _mm(dxkv, [w["w_xkv"]], name="xkv_dx", tb=True)
    g["w_xkv"] = _mm(mn, [dxkv], name="xkv_dw", ta=True)
    dx1, g["g_x"] = _rms_bwd_call(x1, w["g_x"], dhx, dx2, "rms_x_bwd")
    _, g["g_mem"] = _rms_bwd_call(mem, w["g_mem"], dmn, None, "rms_mem_bwd")

    dmerged = _mm(dx1, [w["w_o"]], name="proj_o_dx", tb=True, out_dtypes=(BF16,))
    g["w_o"] = _mm(merged, [dx1], name="proj_o_dw", ta=True)
    dpa, dpb, dga, dgb, g["b_gate"] = _gate_bwd(proj, pa, pb, w["b_gate"], dmerged)
    do_a = _mm(dpa, [wa], name="proj_a_dx", tb=True, out_dtypes=(BF16,))
    do_b = _mm(dpb, [wb], name="proj_b_dx", tb=True, out_dtypes=(BF16,))
    g["w_a_proj"] = _unpad_heads(_mm(o_a, [dpa], name="proj_a_dw", ta=True), MLA_HEADS, MLA_V, 0)
    g["w_b_proj"] = _mm(o_b, [dpb], name="proj_b_dw", ta=True)

    stacked = [reduce[1](n, g[n]) for n in reduce[0]] if reduce else []
    dsq, dsk, dsv, recv = _sb_bwd(proj, o_b, do_b, t_sb, stacked)
    parts = [reduce[2](n, gs, rv) for n, gs, rv in zip(reduce[0], stacked, recv)] if reduce else []
    dqa, dka, dva, got = _mla_bwd(qa, ka, va, o_a, do_a, lse, t_mla[0], t_mla[2], parts)
    riding = dict(zip(reduce[0], zip(stacked, recv, got))) if reduce else {}
    dqp, dkvp, dkr = _mla_rope_bwd(dqa, dka, dva, tab)
    g["w_uq"] = _unpad_heads(_mm(cqn, [dqp], name="q_up_dw", ta=True), MLA_HEADS, MLA_NOPE + MLA_ROPE, 1)
    g["w_ukv"] = _w_ukv_unlayout(_mm(ckvn, [dkvp], name="kv_up_dw", ta=True))
    dcqn = _mm(dqp, [wuq], name="q_up_dx", tb=True)
    dckvn = _mm(dkvp, [wkv], name="kv_up_dx", tb=True)
    dcq, dckv, g["g_q_lat"], g["g_kv_lat"] = _mla_prep_bwd(proj, w["g_q_lat"], w["g_kv_lat"], dcqn, dckvn)

    dproj = jnp.concatenate([dga, dgb, dsq.astype(BF16), dsk.astype(BF16), dsv.astype(BF16), dcq, dckv, dkr], axis=1)
    g["w_in"] = _w_in_unlayout(_mm(h, [dproj], name="proj_in_dw", ta=True))
    if not reduce:
        dh = _mm(dproj, [win], name="proj_in_dx", tb=True, tk=2048, out_dtypes=(BF16,))
        grad_x, g["g_mix"] = _rms_bwd_call(x, w["g_mix"], dh, dx1, "rms_mix_bwd")
        return sq, grad_x, g, riding
    stacked = [reduce[1](n, g[n]) for n in reduce[3]]
    dh, landed = _mm(dproj, [win], name="proj_in_dx", tb=True, tk=2048, out_dtypes=(BF16,),
                     ride=(_direct_copies, stacked + [s.astype(BF16) for s in stacked], _direct_shapes(stacked),
                           7 * len(stacked)))
    grad_x, g["g_mix"] = _rms_bwd_call(x, w["g_mix"], dh, dx1, "rms_mix_bwd")
    nl = len(stacked)
    riding.update(zip(reduce[3], zip(stacked, landed[:nl], landed[nl:])))
    return sq, grad_x, g, riding


def _small_pack(d):
    row5 = jnp.concatenate([d["g_q_lat"].reshape(-1), d["g_kv_lat"].reshape(-1), jnp.zeros((640,), F32)])
    rows = [d[n].reshape(-1) for n in ("g_mix", "g_x", "g_mem", "g_ffn", "g_final")] + [row5]
    return rows


def _small_unpack(p, like):
    out = {n: p[i].reshape(like[n].shape) for i, n in enumerate(("g_mix", "g_x", "g_mem", "g_ffn", "g_final"))}
    out["g_q_lat"] = p[5, 0:256].reshape(like["g_q_lat"].shape)
    out["g_kv_lat"] = p[5, 256:384].reshape(like["g_kv_lat"].shape)
    return out


def kernel(x, mem, positions, g_mix, w_in, b_gate, g_q_lat, w_uq, g_kv_lat, w_ukv, w_a_proj, w_b_proj, w_o, g_x, g_mem, w_xq, w_xkv, w_xo, g_ffn, w_gate, w_up, w_down, g_final, loss_target, m_g_mix, m_w_in, m_b_gate, m_g_q_lat, m_w_uq, m_g_kv_lat, m_w_ukv, m_w_a_proj, m_w_b_proj, m_w_o, m_g_x, m_g_mem, m_w_xq, m_w_xkv, m_w_xo, m_g_ffn, m_w_gate, m_w_up, m_w_down, m_g_final, v_g_mix, v_w_in, v_b_gate, v_g_q_lat, v_w_uq, v_g_kv_lat, v_w_ukv, v_w_a_proj, v_w_b_proj, v_w_o, v_g_x, v_g_mem, v_w_xq, v_w_xkv, v_w_xo, v_g_ffn, v_w_gate, v_w_up, v_w_down, v_g_final):
    given = dict(locals())
    names = [n for n, _, _ in MATS] + ["b_gate"] + list(SMALL)
    wts = {n: given[n] for n in names}
    mom = {n: given["m_" + n] for n in names}
    var = {n: given["v_" + n] for n in names}
    shard2d = {n: shp for n, shp, _ in MATS}
    shard2d["b_gate"] = B_GATE_SHARD
    cx, cy, cc = lax.axis_index("x"), lax.axis_index("y"), lax.axis_index("c")
    me = 2 * cx + cy
    place = jnp.stack([me, cc]).astype(jnp.int32)
    bcol = me * B_GATE_SHARD[1]

    own = [wts[n].reshape(shard2d[n]).astype(BF16) for n, _, _ in MATS]
    bias = (("b_gate", (BIAS_ROWS, B_GATE_SHARD[1]), 1),)
    own_bias = [jnp.pad(wts["b_gate"].reshape(B_GATE_SHARD), ((0, BIAS_ROWS - B_GATE_SHARD[0]), (0, 0)))]

    def assemble(mats, gathered, mine=None):
        out = {}
        for k, ((n, shp, ax), g4) in enumerate(zip(mats, gathered)):
            if mine is not None:
                g4 = lax.dynamic_update_slice(g4, mine[k][None], (me, 0, 0))
            out[n] = g4.reshape(N_CHIPS * shp[0], shp[1]) if ax == 0 else jnp.concatenate(list(g4), axis=1)
        return out

    first = own[:N_EARLY] + own_bias
    full = assemble(MATS[:N_EARLY] + bias, _all_gather_weights(first), first)
    full["b_gate"] = full["b_gate"][0:B_GATE_SHARD[0]]
    late = (own[N_EARLY:], lambda gathered: assemble(MATS[N_EARLY:], gathered))
    for n in SMALL:
        full[n] = wts[n].reshape(1, -1) if n != "g_final" else wts[n]

    axis_of = {n: ax for n, _, ax in MATS}
    stack = lambda n, g: jnp.stack([_shard_of(g, d, axis_of[n]) for d in range(N_CHIPS)])
    pair_sum = lambda n, gs, rv: _pair_sum(gs, rv, place, "pair_sum_" + n)
    behind = [n for n, _, _ in MATS[N_EARLY:]]
    last = [n for n, _, _ in MATS[:N_EARLY]]
    sq, grad_x, grads, riding = _local_step(x[0], mem[0], positions[0], loss_target[0], full, t_mla=(1024, 1024, 1024), t_sb=256,
                                            late=late, reduce=(behind, stack, pair_sum, last))

    halves = [(_chip_sum_direct if n in last else _chip_sum)(*riding[n], place, "chip_sum_" + n) for n, _, _ in MATS]
    g_shard = dict(zip([n for n, _, _ in MATS], _pair_exchange_halves(halves)))

    small_rows = _small_pack({n: grads[n] for n in SMALL}) + [sq.reshape(-1), grads["b_gate"][0], grads["b_gate"][1]]
    small_rows += [jnp.zeros((D_MODEL,), F32)] * (SMALL_ROWS - len(small_rows))
    small = _all_reduce_small(jnp.stack(small_rows), "all_reduce_small")
    loss = (0.5 / D_MODEL) * jnp.sum(small[6])
    g_shard["b_gate"] = lax.dynamic_slice(small[7:9], (0, bcol), B_GATE_SHARD)

    out = {"grad": {}, "delta": {}, "m": {}, "v": {}}
    for n in [n for n, _, _ in MATS] + ["b_gate"]:
        shape = wts[n].shape
        r2 = lambda a: a.reshape(shard2d[n])
        d_n, m_n, v_n = _adamw(r2(wts[n]), g_shard[n], r2(mom[n]), r2(var[n]), "adamw_" + n)
        for key, a in (("grad", g_shard[n]), ("delta", d_n), ("m", m_n), ("v", v_n)):
            out[key][n] = a.reshape(shape)
    sp = lambda d: jnp.stack(_small_pack(d) + [jnp.zeros((D_MODEL,), F32)] * 2)
    delta_s, m_s, v_s = _adamw(sp(wts), small[0:8].at[6:8].set(0.0), sp(mom), sp(var), "adamw_small")
    for key, p in (("grad", small), ("delta", delta_s), ("m", m_s), ("v", v_s)):
        out[key].update(_small_unpack(p, wts))

    order = ["g_mix", "w_in", "b_gate", "g_q_lat", "w_uq", "g_kv_lat", "w_ukv", "w_a_proj", "w_b_proj", "w_o", "g_x",
             "g_mem", "w_xq", "w_xkv", "w_xo", "g_ffn", "w_gate", "w_up", "w_down", "g_final"]
    return (loss, grad_x[None], *[out[key][n] for key in ("grad", "delta", "m", "v") for n in order])
```
